```python
import jax, jax.numpy as jnp
from jax import lax
import numpy as np

D_MODEL = 1024
BATCH = 8
SEQ = 8192
DEPTH = 2

MEM_LEN = 256
HEAD_DIM = 64
ROPE_THETA = 10000.0
NORM_EPS = 1e-6
BLOCK = 128

SWA_HEADS = 8
SWA_KV_HEADS = 2
SWA_WINDOW = 128
MLA_HEADS = 8
MLA_Q_RANK = 384
MLA_KV_RANK = 256
MLA_NOPE_DIM = 64
MLA_ROPE_DIM = 32
MLA_V_DIM = 64
A_Q = SWA_HEADS * HEAD_DIM
A_KV = SWA_KV_HEADS * HEAD_DIM
EVEN_IN = A_Q + 2 * A_KV + MLA_Q_RANK + MLA_KV_RANK + MLA_ROPE_DIM
EVEN_SPLITS = [A_Q, A_Q + A_KV, A_Q + 2 * A_KV, A_Q + 2 * A_KV + MLA_Q_RANK,
               A_Q + 2 * A_KV + MLA_Q_RANK + MLA_KV_RANK]
EVEN_OUT = SWA_HEADS * HEAD_DIM + MLA_HEADS * MLA_V_DIM
DIL_HEADS = D_MODEL // HEAD_DIM
DIL_PATTERNS = ((128, 1), (512, 4), (2048, 16))
X_HEADS = 4
X_HEAD_DIM = 128
FFN_HIDDEN = -(-8 * D_MODEL // (3 * 256)) * 256

kernel_name = 'hybrid_swa_mla_dilated_block'


def rms_norm(x, g):
    xf = x.astype(jnp.float32)
    y = xf * lax.rsqrt(jnp.mean(xf * xf, axis=-1, keepdims=True) + NORM_EPS)
    return (y * g.astype(jnp.float32)).astype(x.dtype)


def rope(x, positions):
    dh = x.shape[-1]
    inv_freq = ROPE_THETA ** (-jnp.arange(0, dh, 2, dtype=jnp.float32) / dh)
    ang = positions.astype(jnp.float32)[..., None] * inv_freq
    c = jnp.cos(ang)[:, :, None, :]
    s = jnp.sin(ang)[:, :, None, :]
    x1, x2 = jnp.split(x.astype(jnp.float32), 2, axis=-1)
    return jnp.concatenate([x1 * c - x2 * s, x2 * c + x1 * s], axis=-1).astype(x.dtype)


def banded_attention(q, k, v, max_dist, sink=None):
    b, L, h, dh = q.shape
    g = k.shape[2]
    rep = h // g
    nb = L // BLOCK
    qb = q.reshape(b, nb, BLOCK, g, rep, dh)

    def two_blocks(t):
        tb = t.reshape(b, nb, BLOCK, g, t.shape[-1])
        prev = jnp.pad(tb, ((0, 0), (1, 0), (0, 0), (0, 0), (0, 0)))[:, :-1]
        return jnp.concatenate([prev, tb], axis=2)

    kk = two_blocks(k)
    vv = two_blocks(v)
    s = jnp.einsum('bnqgrd,bnkgd->bngrqk', qb, kk).astype(jnp.float32) * (dh ** -0.5)
    qi = jnp.arange(BLOCK)[:, None]
    kj = jnp.arange(2 * BLOCK)[None, :]
    dist = BLOCK + qi - kj
    band = (dist >= 0) & (dist <= max_dist)
    exists = (jnp.arange(nb)[:, None, None] > 0) | (kj >= BLOCK)[None]
    mask = band[None] & exists
    s = jnp.where(mask[None, :, None, None], s, -jnp.inf)
    m = jnp.max(s, axis=-1, keepdims=True)
    if sink is not None:
        sk = sink.astype(jnp.float32).reshape(g, rep)[None, None, :, :, None, None]
        m = jnp.maximum(m, sk)
    p = jnp.exp(s - m)
    l = jnp.sum(p, axis=-1, keepdims=True)
    if sink is not None:
        l = l + jnp.exp(sk - m)
    o = jnp.einsum('bngrqk,bnkgd->bnqgrd', (p / l).astype(v.dtype), vv)
    lse = (m + jnp.log(l))[..., 0].transpose(0, 1, 4, 2, 3).reshape(b, L, h)
    return o.reshape(b, L, h, -1), lse


def causal_mla_attention(q_nope, q_rope, k_nope, k_rope, v):
    S = q_nope.shape[1]
    scale = (q_nope.shape[-1] + q_rope.shape[-1]) ** -0.5
    outs = []
    for i in range(S // BLOCK):
        q0, q1 = i * BLOCK, (i + 1) * BLOCK
        s = (jnp.einsum('bqhd,bkhd->bhqk', q_nope[:, q0:q1], k_nope[:, :q1])
             + jnp.einsum('bqhd,bkd->bhqk', q_rope[:, q0:q1], k_rope[:, :q1])).astype(jnp.float32) * scale
        causal = jnp.arange(q0, q1)[:, None] >= jnp.arange(q1)[None, :]
        p = jax.nn.softmax(jnp.where(causal, s, -jnp.inf), axis=-1).astype(v.dtype)
        outs.append(jnp.einsum('bhqk,bkhd->bqhd', p, v[:, :q1]))
    return jnp.concatenate(outs, axis=1)


def even_mixer(h, positions, w_in, sinks, q_norm, w_uq, kv_norm, w_ukv, w_out):
    b, s, _ = h.shape
    z = h @ w_in
    qa, ka, va, cq, ckv, kr = jnp.split(z, EVEN_SPLITS, axis=-1)
    qa = rope(qa.reshape(b, s, SWA_HEADS, HEAD_DIM), positions)
    ka = rope(ka.reshape(b, s, SWA_KV_HEADS, HEAD_DIM), positions)
    va = va.reshape(b, s, SWA_KV_HEADS, HEAD_DIM)
    oa, _ = banded_attention(qa, ka, va, SWA_WINDOW - 1, sink=sinks)
    qb = (rms_norm(cq, q_norm) @ w_uq).reshape(b, s, MLA_HEADS, MLA_NOPE_DIM + MLA_ROPE_DIM)
    q_nope, q_rope = jnp.split(qb, [MLA_NOPE_DIM], axis=-1)
    q_rope = rope(q_rope, positions)
    kvb = (rms_norm(ckv, kv_norm) @ w_ukv).reshape(b, s, MLA_HEADS, MLA_NOPE_DIM + MLA_V_DIM)
    k_nope, vb = jnp.split(kvb, [MLA_NOPE_DIM], axis=-1)
    k_rope = rope(kr[:, :, None, :], positions)[:, :, 0]
    ob = causal_mla_attention(q_nope, q_rope, k_nope, k_rope, vb)
    o = jnp.concatenate([oa.reshape(b, s, -1), ob.reshape(b, s, -1)], axis=-1)
    return o @ w_out


def dilated_attention(q, k, v):
    b, s, h, dh = q.shape
    outs, lses = [], []
    for window, dil in DIL_PATTERNS:
        span = dil * BLOCK
        L = -(-s // span) * span
        n = L // dil

        def deinterleave(t):
            t = jnp.pad(t, ((0, 0), (0, L - s), (0, 0), (0, 0)))
            return t.reshape(b, n, dil, h, dh).transpose(0, 2, 1, 3, 4).reshape(b * dil, n, h, dh)

        o, lse = banded_attention(deinterleave(q), deinterleave(k), deinterleave(v), window // dil)
        outs.append(o.reshape(b, dil, n, h, dh).transpose(0, 2, 1, 3, 4).reshape(b, L, h, dh)[:, :s])
        lses.append(lse.reshape(b, dil, n, h).transpose(0, 2, 1, 3).reshape(b, L, h)[:, :s])
    wts = jax.nn.softmax(jnp.stack(lses, axis=-1), axis=-1).astype(q.dtype)
    return jnp.einsum('bshdn,bshn->bshd', jnp.stack(outs, axis=-1), wts)


def odd_mixer(h, positions, w_qkv, w_out):
    b, s, _ = h.shape
    q, k, v = jnp.split((h @ w_qkv).reshape(b, s, 3 * DIL_HEADS, HEAD_DIM), 3, axis=2)
    o = dilated_attention(rope(q, positions), rope(k, positions), v)
    return o.reshape(b, s, -1) @ w_out


def memory_cross_attention(h, mem_n, w_q, w_kv, w_o):
    b, s, _ = h.shape
    q = (h @ w_q).reshape(b, s, X_HEADS, X_HEAD_DIM)
    k, v = jnp.split((mem_n @ w_kv).reshape(b, mem_n.shape[1], 2 * X_HEADS, X_HEAD_DIM), 2, axis=2)
    sc = jnp.einsum('bqhd,bkhd->bhqk', q, k).astype(jnp.float32) * (X_HEAD_DIM ** -0.5)
    p = jax.nn.softmax(sc, axis=-1).astype(v.dtype)
    o = jnp.einsum('bhqk,bkhd->bqhd', p, v).reshape(b, s, -1)
    return o @ w_o


def swiglu(h, w_gate, w_up, w_down):
    return (jax.nn.silu(h @ w_gate) * (h @ w_up)) @ w_down


def _fwd_setup_inputs(seed: int = 0) -> dict:
    key = jax.random.key(seed)
    keys = iter(jax.random.split(key, 64))

    def w(shape, fan_in, gain=1.0):
        return jax.random.normal(next(keys), shape, jnp.float32) * (gain * fan_in ** -0.5)

    def gain_vec(n):
        return 1.0 + 0.02 * jax.random.normal(next(keys), (n,), jnp.float32)

    res_gain = (2.0 * DEPTH) ** -0.5
    inp = {}
    inp['x'] = jax.random.normal(next(keys), (BATCH, SEQ, D_MODEL), jnp.float32)
    inp['mem'] = jax.random.normal(next(keys), (BATCH, MEM_LEN, D_MODEL), jnp.float32)
    offsets = jax.random.randint(next(keys), (BATCH, 1), 0, 4096, dtype=jnp.int32)
    inp['positions'] = jnp.arange(SEQ, dtype=jnp.int32)[None, :] + offsets
    for l in range(DEPTH):
        p = 'l%d_' % l
        inp[p + 'mix_norm'] = gain_vec(D_MODEL)
        if l % 2 == 0:
            inp[p + 'w_in'] = w((D_MODEL, EVEN_IN), D_MODEL)
            inp[p + 'sinks'] = jax.random.normal(next(keys), (SWA_HEADS,), jnp.float32)
            inp[p + 'q_norm'] = gain_vec(MLA_Q_RANK)
            inp[p + 'w_uq'] = w((MLA_Q_RANK, MLA_HEADS * (MLA_NOPE_DIM + MLA_ROPE_DIM)), MLA_Q_RANK)
            inp[p + 'kv_norm'] = gain_vec(MLA_KV_RANK)
            inp[p + 'w_ukv'] = w((MLA_KV_RANK, MLA_HEADS * (MLA_NOPE_DIM + MLA_V_DIM)), MLA_KV_RANK)
            inp[p + 'w_out'] = w((EVEN_OUT, D_MODEL), EVEN_OUT, res_gain)
        else:
            inp[p + 'w_qkv'] = w((D_MODEL, 3 * DIL_HEADS * HEAD_DIM), D_MODEL)
            inp[p + 'w_out'] = w((DIL_HEADS * HEAD_DIM, D_MODEL), DIL_HEADS * HEAD_DIM, res_gain)
        inp[p + 'x_norm'] = gain_vec(D_MODEL)
        inp[p + 'mem_norm'] = gain_vec(D_MODEL)
        inp[p + 'w_xq'] = w((D_MODEL, X_HEADS * X_HEAD_DIM), D_MODEL)
        inp[p + 'w_xkv'] = w((D_MODEL, 2 * X_HEADS * X_HEAD_DIM), D_MODEL)
        inp[p + 'w_xo'] = w((X_HEADS * X_HEAD_DIM, D_MODEL), X_HEADS * X_HEAD_DIM, res_gain)
        inp[p + 'ffn_norm'] = gain_vec(D_MODEL)
        inp[p + 'w_gate'] = w((D_MODEL, FFN_HIDDEN), D_MODEL)
        inp[p + 'w_up'] = w((D_MODEL, FFN_HIDDEN), D_MODEL)
        inp[p + 'w_down'] = w((FFN_HIDDEN, D_MODEL), FFN_HIDDEN, res_gain)
    inp['final_norm'] = gain_vec(D_MODEL)
    return inp


def _fwd_reference(x, mem, positions,
              l0_mix_norm, l0_w_in, l0_sinks, l0_q_norm, l0_w_uq, l0_kv_norm, l0_w_ukv, l0_w_out,
              l0_x_norm, l0_mem_norm, l0_w_xq, l0_w_xkv, l0_w_xo,
              l0_ffn_norm, l0_w_gate, l0_w_up, l0_w_down,
              l1_mix_norm, l1_w_qkv, l1_w_out,
              l1_x_norm, l1_mem_norm, l1_w_xq, l1_w_xkv, l1_w_xo,
              l1_ffn_norm, l1_w_gate, l1_w_up, l1_w_down,
              final_norm):
    mixers = [
        lambda h: even_mixer(h, positions, l0_w_in, l0_sinks, l0_q_norm, l0_w_uq,
                             l0_kv_norm, l0_w_ukv, l0_w_out),
        lambda h: odd_mixer(h, positions, l1_w_qkv, l1_w_out),
    ]
    mix_norms = [l0_mix_norm, l1_mix_norm]
    xattn = [(l0_x_norm, l0_mem_norm, l0_w_xq, l0_w_xkv, l0_w_xo),
             (l1_x_norm, l1_mem_norm, l1_w_xq, l1_w_xkv, l1_w_xo)]
    ffns = [(l0_ffn_norm, l0_w_gate, l0_w_up, l0_w_down),
            (l1_ffn_norm, l1_w_gate, l1_w_up, l1_w_down)]
    for layer in range(DEPTH):
        x = x + mixers[layer](rms_norm(x, mix_norms[layer]))
        xn, mn, wq, wkv, wo = xattn[layer]
        x = x + memory_cross_attention(rms_norm(x, xn), rms_norm(mem, mn), wq, wkv, wo)
        fn, wg, wu, wd = ffns[layer]
        x = x + swiglu(rms_norm(x, fn), wg, wu, wd)
    return rms_norm(x, final_norm)


import jax as _jax
import jax.numpy as _jnp

TWIN_FORMAT = 'train_step'
FWD_PARAMS = ['x', 'mem', 'positions', 'l0_mix_norm', 'l0_w_in', 'l0_sinks', 'l0_q_norm', 'l0_w_uq', 'l0_kv_norm', 'l0_w_ukv', 'l0_w_out', 'l0_x_norm', 'l0_mem_norm', 'l0_w_xq', 'l0_w_xkv', 'l0_w_xo', 'l0_ffn_norm', 'l0_w_gate', 'l0_w_up', 'l0_w_down', 'l1_mix_norm', 'l1_w_qkv', 'l1_w_out', 'l1_x_norm', 'l1_mem_norm', 'l1_w_xq', 'l1_w_xkv', 'l1_w_xo', 'l1_ffn_norm', 'l1_w_gate', 'l1_w_up', 'l1_w_down', 'final_norm']
TWIN_WEIGHTS = ['l0_mix_norm', 'l0_w_in', 'l0_sinks', 'l0_q_norm', 'l0_w_uq', 'l0_kv_norm', 'l0_w_ukv', 'l0_w_out', 'l0_x_norm', 'l0_mem_norm', 'l0_w_xq', 'l0_w_xkv', 'l0_w_xo', 'l0_ffn_norm', 'l0_w_gate', 'l0_w_up', 'l0_w_down', 'l1_mix_norm', 'l1_w_qkv', 'l1_w_out', 'l1_x_norm', 'l1_mem_norm', 'l1_w_xq', 'l1_w_xkv', 'l1_w_xo', 'l1_ffn_norm', 'l1_w_gate', 'l1_w_up', 'l1_w_down', 'final_norm']
TWIN_DIFF_INPUT = 'x'
TWIN_INPUTS = ['x', 'mem', 'positions', 'l0_mix_norm', 'l0_w_in', 'l0_sinks', 'l0_q_norm', 'l0_w_uq', 'l0_kv_norm', 'l0_w_ukv', 'l0_w_out', 'l0_x_norm', 'l0_mem_norm', 'l0_w_xq', 'l0_w_xkv', 'l0_w_xo', 'l0_ffn_norm', 'l0_w_gate', 'l0_w_up', 'l0_w_down', 'l1_mix_norm', 'l1_w_qkv', 'l1_w_out', 'l1_x_norm', 'l1_mem_norm', 'l1_w_xq', 'l1_w_xkv', 'l1_w_xo', 'l1_ffn_norm', 'l1_w_gate', 'l1_w_up', 'l1_w_down', 'final_norm', 'loss_target', 'm_l0_mix_norm', 'm_l0_w_in', 'm_l0_sinks', 'm_l0_q_norm', 'm_l0_w_uq', 'm_l0_kv_norm', 'm_l0_w_ukv', 'm_l0_w_out', 'm_l0_x_norm', 'm_l0_mem_norm', 'm_l0_w_xq', 'm_l0_w_xkv', 'm_l0_w_xo', 'm_l0_ffn_norm', 'm_l0_w_gate', 'm_l0_w_up', 'm_l0_w_down', 'm_l1_mix_norm', 'm_l1_w_qkv', 'm_l1_w_out', 'm_l1_x_norm', 'm_l1_mem_norm', 'm_l1_w_xq', 'm_l1_w_xkv', 'm_l1_w_xo', 'm_l1_ffn_norm', 'm_l1_w_gate', 'm_l1_w_up', 'm_l1_w_down', 'm_final_norm', 'v_l0_mix_norm', 'v_l0_w_in', 'v_l0_sinks', 'v_l0_q_norm', 'v_l0_w_uq', 'v_l0_kv_norm', 'v_l0_w_ukv', 'v_l0_w_out', 'v_l0_x_norm', 'v_l0_mem_norm', 'v_l0_w_xq', 'v_l0_w_xkv', 'v_l0_w_xo', 'v_l0_ffn_norm', 'v_l0_w_gate', 'v_l0_w_up', 'v_l0_w_down', 'v_l1_mix_norm', 'v_l1_w_qkv', 'v_l1_w_out', 'v_l1_x_norm', 'v_l1_mem_norm', 'v_l1_w_xq', 'v_l1_w_xkv', 'v_l1_w_xo', 'v_l1_ffn_norm', 'v_l1_w_gate', 'v_l1_w_up', 'v_l1_w_down', 'v_final_norm']
TWIN_OUTPUTS = ['loss', 'grad_x', 'grad_l0_mix_norm', 'grad_l0_w_in', 'grad_l0_sinks', 'grad_l0_q_norm', 'grad_l0_w_uq', 'grad_l0_kv_norm', 'grad_l0_w_ukv', 'grad_l0_w_out', 'grad_l0_x_norm', 'grad_l0_mem_norm', 'grad_l0_w_xq', 'grad_l0_w_xkv', 'grad_l0_w_xo', 'grad_l0_ffn_norm', 'grad_l0_w_gate', 'grad_l0_w_up', 'grad_l0_w_down', 'grad_l1_mix_norm', 'grad_l1_w_qkv', 'grad_l1_w_out', 'grad_l1_x_norm', 'grad_l1_mem_norm', 'grad_l1_w_xq', 'grad_l1_w_xkv', 'grad_l1_w_xo', 'grad_l1_ffn_norm', 'grad_l1_w_gate', 'grad_l1_w_up', 'grad_l1_w_down', 'grad_final_norm', 'delta_l0_mix_norm', 'delta_l0_w_in', 'delta_l0_sinks', 'delta_l0_q_norm', 'delta_l0_w_uq', 'delta_l0_kv_norm', 'delta_l0_w_ukv', 'delta_l0_w_out', 'delta_l0_x_norm', 'delta_l0_mem_norm', 'delta_l0_w_xq', 'delta_l0_w_xkv', 'delta_l0_w_xo', 'delta_l0_ffn_norm', 'delta_l0_w_gate', 'delta_l0_w_up', 'delta_l0_w_down', 'delta_l1_mix_norm', 'delta_l1_w_qkv', 'delta_l1_w_out', 'delta_l1_x_norm', 'delta_l1_mem_norm', 'delta_l1_w_xq', 'delta_l1_w_xkv', 'delta_l1_w_xo', 'delta_l1_ffn_norm', 'delta_l1_w_gate', 'delta_l1_w_up', 'delta_l1_w_down', 'delta_final_norm', 'new_m_l0_mix_norm', 'new_m_l0_w_in', 'new_m_l0_sinks', 'new_m_l0_q_norm', 'new_m_l0_w_uq', 'new_m_l0_kv_norm', 'new_m_l0_w_ukv', 'new_m_l0_w_out', 'new_m_l0_x_norm', 'new_m_l0_mem_norm', 'new_m_l0_w_xq', 'new_m_l0_w_xkv', 'new_m_l0_w_xo', 'new_m_l0_ffn_norm', 'new_m_l0_w_gate', 'new_m_l0_w_up', 'new_m_l0_w_down', 'new_m_l1_mix_norm', 'new_m_l1_w_qkv', 'new_m_l1_w_out', 'new_m_l1_x_norm', 'new_m_l1_mem_norm', 'new_m_l1_w_xq', 'new_m_l1_w_xkv', 'new_m_l1_w_xo', 'new_m_l1_ffn_norm', 'new_m_l1_w_gate', 'new_m_l1_w_up', 'new_m_l1_w_down', 'new_m_final_norm', 'new_v_l0_mix_norm', 'new_v_l0_w_in', 'new_v_l0_sinks', 'new_v_l0_q_norm', 'new_v_l0_w_uq', 'new_v_l0_kv_norm', 'new_v_l0_w_ukv', 'new_v_l0_w_out', 'new_v_l0_x_norm', 'new_v_l0_mem_norm', 'new_v_l0_w_xq', 'new_v_l0_w_xkv', 'new_v_l0_w_xo', 'new_v_l0_ffn_norm', 'new_v_l0_w_gate', 'new_v_l0_w_up', 'new_v_l0_w_down', 'new_v_l1_mix_norm', 'new_v_l1_w_qkv', 'new_v_l1_w_out', 'new_v_l1_x_norm', 'new_v_l1_mem_norm', 'new_v_l1_w_xq', 'new_v_l1_w_xkv', 'new_v_l1_w_xo', 'new_v_l1_ffn_norm', 'new_v_l1_w_gate', 'new_v_l1_w_up', 'new_v_l1_w_down', 'new_v_final_norm']
TWIN_LEAF_KINDS = {'loss': 'loss', 'grad_x': 'grad_x', 'grad_l0_mix_norm': 'grad_w', 'grad_l0_w_in': 'grad_w', 'grad_l0_sinks': 'grad_w', 'grad_l0_q_norm': 'grad_w', 'grad_l0_w_uq': 'grad_w', 'grad_l0_kv_norm': 'grad_w', 'grad_l0_w_ukv': 'grad_w', 'grad_l0_w_out': 'grad_w', 'grad_l0_x_norm': 'grad_w', 'grad_l0_mem_norm': 'grad_w', 'grad_l0_w_xq': 'grad_w', 'grad_l0_w_xkv': 'grad_w', 'grad_l0_w_xo': 'grad_w', 'grad_l0_ffn_norm': 'grad_w', 'grad_l0_w_gate': 'grad_w', 'grad_l0_w_up': 'grad_w', 'grad_l0_w_down': 'grad_w', 'grad_l1_mix_norm': 'grad_w', 'grad_l1_w_qkv': 'grad_w', 'grad_l1_w_out': 'grad_w', 'grad_l1_x_norm': 'grad_w', 'grad_l1_mem_norm': 'grad_w', 'grad_l1_w_xq': 'grad_w', 'grad_l1_w_xkv': 'grad_w', 'grad_l1_w_xo': 'grad_w', 'grad_l1_ffn_norm': 'grad_w', 'grad_l1_w_gate': 'grad_w', 'grad_l1_w_up': 'grad_w', 'grad_l1_w_down': 'grad_w', 'grad_final_norm': 'grad_w', 'delta_l0_mix_norm': 'delta_w', 'delta_l0_w_in': 'delta_w', 'delta_l0_sinks': 'delta_w', 'delta_l0_q_norm': 'delta_w', 'delta_l0_w_uq': 'delta_w', 'delta_l0_kv_norm': 'delta_w', 'delta_l0_w_ukv': 'delta_w', 'delta_l0_w_out': 'delta_w', 'delta_l0_x_norm': 'delta_w', 'delta_l0_mem_norm': 'delta_w', 'delta_l0_w_xq': 'delta_w', 'delta_l0_w_xkv': 'delta_w', 'delta_l0_w_xo': 'delta_w', 'delta_l0_ffn_norm': 'delta_w', 'delta_l0_w_gate': 'delta_w', 'delta_l0_w_up': 'delta_w', 'delta_l0_w_down': 'delta_w', 'delta_l1_mix_norm': 'delta_w', 'delta_l1_w_qkv': 'delta_w', 'delta_l1_w_out': 'delta_w', 'delta_l1_x_norm': 'delta_w', 'delta_l1_mem_norm': 'delta_w', 'delta_l1_w_xq': 'delta_w', 'delta_l1_w_xkv': 'delta_w', 'delta_l1_w_xo': 'delta_w', 'delta_l1_ffn_norm': 'delta_w', 'delta_l1_w_gate': 'delta_w', 'delta_l1_w_up': 'delta_w', 'delta_l1_w_down': 'delta_w', 'delta_final_norm': 'delta_w', 'new_m_l0_mix_norm': 'new_m', 'new_m_l0_w_in': 'new_m', 'new_m_l0_sinks': 'new_m', 'new_m_l0_q_norm': 'new_m', 'new_m_l0_w_uq': 'new_m', 'new_m_l0_kv_norm': 'new_m', 'new_m_l0_w_ukv': 'new_m', 'new_m_l0_w_out': 'new_m', 'new_m_l0_x_norm': 'new_m', 'new_m_l0_mem_norm': 'new_m', 'new_m_l0_w_xq': 'new_m', 'new_m_l0_w_xkv': 'new_m', 'new_m_l0_w_xo': 'new_m', 'new_m_l0_ffn_norm': 'new_m', 'new_m_l0_w_gate': 'new_m', 'new_m_l0_w_up': 'new_m', 'new_m_l0_w_down': 'new_m', 'new_m_l1_mix_norm': 'new_m', 'new_m_l1_w_qkv': 'new_m', 'new_m_l1_w_out': 'new_m', 'new_m_l1_x_norm': 'new_m', 'new_m_l1_mem_norm': 'new_m', 'new_m_l1_w_xq': 'new_m', 'new_m_l1_w_xkv': 'new_m', 'new_m_l1_w_xo': 'new_m', 'new_m_l1_ffn_norm': 'new_m', 'new_m_l1_w_gate': 'new_m', 'new_m_l1_w_up': 'new_m', 'new_m_l1_w_down': 'new_m', 'new_m_final_norm': 'new_m', 'new_v_l0_mix_norm': 'new_v', 'new_v_l0_w_in': 'new_v', 'new_v_l0_sinks': 'new_v', 'new_v_l0_q_norm': 'new_v', 'new_v_l0_w_uq': 'new_v', 'new_v_l0_kv_norm': 'new_v', 'new_v_l0_w_ukv': 'new_v', 'new_v_l0_w_out': 'new_v', 'new_v_l0_x_norm': 'new_v', 'new_v_l0_mem_norm': 'new_v', 'new_v_l0_w_xq': 'new_v', 'new_v_l0_w_xkv': 'new_v', 'new_v_l0_w_xo': 'new_v', 'new_v_l0_ffn_norm': 'new_v', 'new_v_l0_w_gate': 'new_v', 'new_v_l0_w_up': 'new_v', 'new_v_l0_w_down': 'new_v', 'new_v_l1_mix_norm': 'new_v', 'new_v_l1_w_qkv': 'new_v', 'new_v_l1_w_out': 'new_v', 'new_v_l1_x_norm': 'new_v', 'new_v_l1_mem_norm': 'new_v', 'new_v_l1_w_xq': 'new_v', 'new_v_l1_w_xkv': 'new_v', 'new_v_l1_w_xo': 'new_v', 'new_v_l1_ffn_norm': 'new_v', 'new_v_l1_w_gate': 'new_v', 'new_v_l1_w_up': 'new_v', 'new_v_l1_w_down': 'new_v', 'new_v_final_norm': 'new_v'}


def _forward(args):
    return _fwd_reference(*[args[k] for k in FWD_PARAMS])


def _output_shape():
    out = _jax.eval_shape(lambda: _forward(_fwd_setup_inputs(0)))
    return out.shape, out.dtype

N_MICROBATCH = 1
ADAM_LR = 0.001
ADAM_B1 = 0.9
ADAM_B2 = 0.999
ADAM_EPS = 1e-08
ADAM_WD = 0.01
ADAM_STEP = 10
PER_EXAMPLE_BATCH_AXIS = {'x': 0, 'mem': 0, 'positions': 0, 'loss_target': 0}
SHARED_INPUTS = []
_WEIGHT_DTYPES = {'l0_mix_norm': _jnp.float32, 'l0_w_in': _jnp.float32, 'l0_sinks': _jnp.float32, 'l0_q_norm': _jnp.float32, 'l0_w_uq': _jnp.float32, 'l0_kv_norm': _jnp.float32, 'l0_w_ukv': _jnp.float32, 'l0_w_out': _jnp.float32, 'l0_x_norm': _jnp.float32, 'l0_mem_norm': _jnp.float32, 'l0_w_xq': _jnp.float32, 'l0_w_xkv': _jnp.float32, 'l0_w_xo': _jnp.float32, 'l0_ffn_norm': _jnp.float32, 'l0_w_gate': _jnp.float32, 'l0_w_up': _jnp.float32, 'l0_w_down': _jnp.float32, 'l1_mix_norm': _jnp.float32, 'l1_w_qkv': _jnp.float32, 'l1_w_out': _jnp.float32, 'l1_x_norm': _jnp.float32, 'l1_mem_norm': _jnp.float32, 'l1_w_xq': _jnp.float32, 'l1_w_xkv': _jnp.float32, 'l1_w_xo': _jnp.float32, 'l1_ffn_norm': _jnp.float32, 'l1_w_gate': _jnp.float32, 'l1_w_up': _jnp.float32, 'l1_w_down': _jnp.float32, 'final_norm': _jnp.float32}
MOMENT_SCALE = {'l0_mix_norm': 4.538818e-02, 'l0_w_in': 3.806957e-02, 'l0_sinks': 2.648775e-02, 'l0_q_norm': 2.839763e-02, 'l0_w_uq': 2.037244e-02, 'l0_kv_norm': 5.540838e-02, 'l0_w_ukv': 2.686034e-02, 'l0_w_out': 6.012863e-02, 'l0_x_norm': 1.522949e-02, 'l0_mem_norm': 2.177388e-02, 'l0_w_xq': 2.022946e-02, 'l0_w_xkv': 2.119243e-02, 'l0_w_xo': 3.009698e-02, 'l0_ffn_norm': 1.087899e-01, 'l0_w_gate': 4.643702e-02, 'l0_w_up': 4.497555e-02, 'l0_w_down': 1.491242e-01, 'l1_mix_norm': 4.356478e-02, 'l1_w_qkv': 2.514389e-02, 'l1_w_out': 5.874637e-02, 'l1_x_norm': 1.359594e-02, 'l1_mem_norm': 1.960217e-02, 'l1_w_xq': 1.849902e-02, 'l1_w_xkv': 1.890308e-02, 'l1_w_xo': 2.696003e-02, 'l1_ffn_norm': 1.135251e-01, 'l1_w_gate': 4.302490e-02, 'l1_w_up': 4.176320e-02, 'l1_w_down': 1.377563e-01, 'final_norm': 6.388582e+01}


def _to_microbatches(a, axis):
    t = _jnp.moveaxis(a, axis, 0)
    t = t.reshape((N_MICROBATCH, t.shape[0] // N_MICROBATCH) + t.shape[1:])
    return _jnp.moveaxis(t, 1, axis + 1)


def setup_inputs(seed: int = 0) -> dict:
    inp = _fwd_setup_inputs(seed)
    key = _jax.random.fold_in(_jax.random.key(seed), 7919)
    shape, _ = _output_shape()
    out = dict(inp)
    out["loss_target"] = _jax.random.normal(_jax.random.fold_in(key, 0), shape, _jnp.float32)
    for i, name in enumerate(TWIN_WEIGHTS):
        w = inp[name].astype(_jnp.float32)
        if MOMENT_SCALE is None:
            s = _jnp.sqrt(_jnp.mean(_jnp.square(w)) + 1e-30)
        else:
            s = MOMENT_SCALE[name]
        km, kv = _jax.random.split(_jax.random.fold_in(key, i + 1))
        out[name] = w
        out["m_" + name] = s * _jax.random.normal(km, w.shape, _jnp.float32)
        out["v_" + name] = (s * s) * _jax.random.uniform(kv, w.shape, _jnp.float32, 0.5, 1.5)
    if N_MICROBATCH > 1:
        for name, axis in PER_EXAMPLE_BATCH_AXIS.items():
            out[name] = _to_microbatches(out[name], axis)
    return {'x': out['x'], 'mem': out['mem'], 'positions': out['positions'], 'l0_mix_norm': out['l0_mix_norm'], 'l0_w_in': out['l0_w_in'], 'l0_sinks': out['l0_sinks'], 'l0_q_norm': out['l0_q_norm'], 'l0_w_uq': out['l0_w_uq'], 'l0_kv_norm': out['l0_kv_norm'], 'l0_w_ukv': out['l0_w_ukv'], 'l0_w_out': out['l0_w_out'], 'l0_x_norm': out['l0_x_norm'], 'l0_mem_norm': out['l0_mem_norm'], 'l0_w_xq': out['l0_w_xq'], 'l0_w_xkv': out['l0_w_xkv'], 'l0_w_xo': out['l0_w_xo'], 'l0_ffn_norm': out['l0_ffn_norm'], 'l0_w_gate': out['l0_w_gate'], 'l0_w_up': out['l0_w_up'], 'l0_w_down': out['l0_w_down'], 'l1_mix_norm': out['l1_mix_norm'], 'l1_w_qkv': out['l1_w_qkv'], 'l1_w_out': out['l1_w_out'], 'l1_x_norm': out['l1_x_norm'], 'l1_mem_norm': out['l1_mem_norm'], 'l1_w_xq': out['l1_w_xq'], 'l1_w_xkv': out['l1_w_xkv'], 'l1_w_xo': out['l1_w_xo'], 'l1_ffn_norm': out['l1_ffn_norm'], 'l1_w_gate': out['l1_w_gate'], 'l1_w_up': out['l1_w_up'], 'l1_w_down': out['l1_w_down'], 'final_norm': out['final_norm'], 'loss_target': out['loss_target'], 'm_l0_mix_norm': out['m_l0_mix_norm'], 'm_l0_w_in': out['m_l0_w_in'], 'm_l0_sinks': out['m_l0_sinks'], 'm_l0_q_norm': out['m_l0_q_norm'], 'm_l0_w_uq': out['m_l0_w_uq'], 'm_l0_kv_norm': out['m_l0_kv_norm'], 'm_l0_w_ukv': out['m_l0_w_ukv'], 'm_l0_w_out': out['m_l0_w_out'], 'm_l0_x_norm': out['m_l0_x_norm'], 'm_l0_mem_norm': out['m_l0_mem_norm'], 'm_l0_w_xq': out['m_l0_w_xq'], 'm_l0_w_xkv': out['m_l0_w_xkv'], 'm_l0_w_xo': out['m_l0_w_xo'], 'm_l0_ffn_norm': out['m_l0_ffn_norm'], 'm_l0_w_gate': out['m_l0_w_gate'], 'm_l0_w_up': out['m_l0_w_up'], 'm_l0_w_down': out['m_l0_w_down'], 'm_l1_mix_norm': out['m_l1_mix_norm'], 'm_l1_w_qkv': out['m_l1_w_qkv'], 'm_l1_w_out': out['m_l1_w_out'], 'm_l1_x_norm': out['m_l1_x_norm'], 'm_l1_mem_norm': out['m_l1_mem_norm'], 'm_l1_w_xq': out['m_l1_w_xq'], 'm_l1_w_xkv': out['m_l1_w_xkv'], 'm_l1_w_xo': out['m_l1_w_xo'], 'm_l1_ffn_norm': out['m_l1_ffn_norm'], 'm_l1_w_gate': out['m_l1_w_gate'], 'm_l1_w_up': out['m_l1_w_up'], 'm_l1_w_down': out['m_l1_w_down'], 'm_final_norm': out['m_final_norm'], 'v_l0_mix_norm': out['v_l0_mix_norm'], 'v_l0_w_in': out['v_l0_w_in'], 'v_l0_sinks': out['v_l0_sinks'], 'v_l0_q_norm': out['v_l0_q_norm'], 'v_l0_w_uq': out['v_l0_w_uq'], 'v_l0_kv_norm': out['v_l0_kv_norm'], 'v_l0_w_ukv': out['v_l0_w_ukv'], 'v_l0_w_out': out['v_l0_w_out'], 'v_l0_x_norm': out['v_l0_x_norm'], 'v_l0_mem_norm': out['v_l0_mem_norm'], 'v_l0_w_xq': out['v_l0_w_xq'], 'v_l0_w_xkv': out['v_l0_w_xkv'], 'v_l0_w_xo': out['v_l0_w_xo'], 'v_l0_ffn_norm': out['v_l0_ffn_norm'], 'v_l0_w_gate': out['v_l0_w_gate'], 'v_l0_w_up': out['v_l0_w_up'], 'v_l0_w_down': out['v_l0_w_down'], 'v_l1_mix_norm': out['v_l1_mix_norm'], 'v_l1_w_qkv': out['v_l1_w_qkv'], 'v_l1_w_out': out['v_l1_w_out'], 'v_l1_x_norm': out['v_l1_x_norm'], 'v_l1_mem_norm': out['v_l1_mem_norm'], 'v_l1_w_xq': out['v_l1_w_xq'], 'v_l1_w_xkv': out['v_l1_w_xkv'], 'v_l1_w_xo': out['v_l1_w_xo'], 'v_l1_ffn_norm': out['v_l1_ffn_norm'], 'v_l1_w_gate': out['v_l1_w_gate'], 'v_l1_w_up': out['v_l1_w_up'], 'v_l1_w_down': out['v_l1_w_down'], 'v_final_norm': out['v_final_norm']}


def _loss(weights, diff, rest, loss_target):
    with _jax.named_scope("forward"):
        args = {**rest, TWIN_DIFF_INPUT: diff, **{k: w.astype(_WEIGHT_DTYPES[k]) for k, w in weights.items()}}
        y = _forward(args)
    with _jax.named_scope("loss_head"):
        err = _jnp.square(y.astype(_jnp.float32) - loss_target)
        return 0.5 * _jnp.sum(_jnp.mean(err, axis=-1)) if err.ndim else 0.5 * err


def _adamw(w, g, m, v):
    m = ADAM_B1 * m + (1.0 - ADAM_B1) * g
    v = ADAM_B2 * v + (1.0 - ADAM_B2) * _jnp.square(g)
    m_hat = m / (1.0 - ADAM_B1 ** ADAM_STEP)
    v_hat = v / (1.0 - ADAM_B2 ** ADAM_STEP)
    delta = -ADAM_LR * (m_hat / (_jnp.sqrt(v_hat) + ADAM_EPS) + ADAM_WD * w)
    return delta, m, v


def reference(x, mem, positions, l0_mix_norm, l0_w_in, l0_sinks, l0_q_norm, l0_w_uq, l0_kv_norm, l0_w_ukv, l0_w_out, l0_x_norm, l0_mem_norm, l0_w_xq, l0_w_xkv, l0_w_xo, l0_ffn_norm, l0_w_gate, l0_w_up, l0_w_down, l1_mix_norm, l1_w_qkv, l1_w_out, l1_x_norm, l1_mem_norm, l1_w_xq, l1_w_xkv, l1_w_xo, l1_ffn_norm, l1_w_gate, l1_w_up, l1_w_down, final_norm, loss_target, m_l0_mix_norm, m_l0_w_in, m_l0_sinks, m_l0_q_norm, m_l0_w_uq, m_l0_kv_norm, m_l0_w_ukv, m_l0_w_out, m_l0_x_norm, m_l0_mem_norm, m_l0_w_xq, m_l0_w_xkv, m_l0_w_xo, m_l0_ffn_norm, m_l0_w_gate, m_l0_w_up, m_l0_w_down, m_l1_mix_norm, m_l1_w_qkv, m_l1_w_out, m_l1_x_norm, m_l1_mem_norm, m_l1_w_xq, m_l1_w_xkv, m_l1_w_xo, m_l1_ffn_norm, m_l1_w_gate, m_l1_w_up, m_l1_w_down, m_final_norm, v_l0_mix_norm, v_l0_w_in, v_l0_sinks, v_l0_q_norm, v_l0_w_uq, v_l0_kv_norm, v_l0_w_ukv, v_l0_w_out, v_l0_x_norm, v_l0_mem_norm, v_l0_w_xq, v_l0_w_xkv, v_l0_w_xo, v_l0_ffn_norm, v_l0_w_gate, v_l0_w_up, v_l0_w_down, v_l1_mix_norm, v_l1_w_qkv, v_l1_w_out, v_l1_x_norm, v_l1_mem_norm, v_l1_w_xq, v_l1_w_xkv, v_l1_w_xo, v_l1_ffn_norm, v_l1_w_gate, v_l1_w_up, v_l1_w_down, v_final_norm):
    given = dict(x=x, mem=mem, positions=positions, l0_mix_norm=l0_mix_norm, l0_w_in=l0_w_in, l0_sinks=l0_sinks, l0_q_norm=l0_q_norm, l0_w_uq=l0_w_uq, l0_kv_norm=l0_kv_norm, l0_w_ukv=l0_w_ukv, l0_w_out=l0_w_out, l0_x_norm=l0_x_norm, l0_mem_norm=l0_mem_norm, l0_w_xq=l0_w_xq, l0_w_xkv=l0_w_xkv, l0_w_xo=l0_w_xo, l0_ffn_norm=l0_ffn_norm, l0_w_gate=l0_w_gate, l0_w_up=l0_w_up, l0_w_down=l0_w_down, l1_mix_norm=l1_mix_norm, l1_w_qkv=l1_w_qkv, l1_w_out=l1_w_out, l1_x_norm=l1_x_norm, l1_mem_norm=l1_mem_norm, l1_w_xq=l1_w_xq, l1_w_xkv=l1_w_xkv, l1_w_xo=l1_w_xo, l1_ffn_norm=l1_ffn_norm, l1_w_gate=l1_w_gate, l1_w_up=l1_w_up, l1_w_down=l1_w_down, final_norm=final_norm, loss_target=loss_target, m_l0_mix_norm=m_l0_mix_norm, m_l0_w_in=m_l0_w_in, m_l0_sinks=m_l0_sinks, m_l0_q_norm=m_l0_q_norm, m_l0_w_uq=m_l0_w_uq, m_l0_kv_norm=m_l0_kv_norm, m_l0_w_ukv=m_l0_w_ukv, m_l0_w_out=m_l0_w_out, m_l0_x_norm=m_l0_x_norm, m_l0_mem_norm=m_l0_mem_norm, m_l0_w_xq=m_l0_w_xq, m_l0_w_xkv=m_l0_w_xkv, m_l0_w_xo=m_l0_w_xo, m_l0_ffn_norm=m_l0_ffn_norm, m_l0_w_gate=m_l0_w_gate, m_l0_w_up=m_l0_w_up, m_l0_w_down=m_l0_w_down, m_l1_mix_norm=m_l1_mix_norm, m_l1_w_qkv=m_l1_w_qkv, m_l1_w_out=m_l1_w_out, m_l1_x_norm=m_l1_x_norm, m_l1_mem_norm=m_l1_mem_norm, m_l1_w_xq=m_l1_w_xq, m_l1_w_xkv=m_l1_w_xkv, m_l1_w_xo=m_l1_w_xo, m_l1_ffn_norm=m_l1_ffn_norm, m_l1_w_gate=m_l1_w_gate, m_l1_w_up=m_l1_w_up, m_l1_w_down=m_l1_w_down, m_final_norm=m_final_norm, v_l0_mix_norm=v_l0_mix_norm, v_l0_w_in=v_l0_w_in, v_l0_sinks=v_l0_sinks, v_l0_q_norm=v_l0_q_norm, v_l0_w_uq=v_l0_w_uq, v_l0_kv_norm=v_l0_kv_norm, v_l0_w_ukv=v_l0_w_ukv, v_l0_w_out=v_l0_w_out, v_l0_x_norm=v_l0_x_norm, v_l0_mem_norm=v_l0_mem_norm, v_l0_w_xq=v_l0_w_xq, v_l0_w_xkv=v_l0_w_xkv, v_l0_w_xo=v_l0_w_xo, v_l0_ffn_norm=v_l0_ffn_norm, v_l0_w_gate=v_l0_w_gate, v_l0_w_up=v_l0_w_up, v_l0_w_down=v_l0_w_down, v_l1_mix_norm=v_l1_mix_norm, v_l1_w_qkv=v_l1_w_qkv, v_l1_w_out=v_l1_w_out, v_l1_x_norm=v_l1_x_norm, v_l1_mem_norm=v_l1_mem_norm, v_l1_w_xq=v_l1_w_xq, v_l1_w_xkv=v_l1_w_xkv, v_l1_w_xo=v_l1_w_xo, v_l1_ffn_norm=v_l1_ffn_norm, v_l1_w_gate=v_l1_w_gate, v_l1_w_up=v_l1_w_up, v_l1_w_down=v_l1_w_down, v_final_norm=v_final_norm)
    weights = {n: given[n] for n in TWIN_WEIGHTS}
    shared = {n: given[n] for n in SHARED_INPUTS}
    per_example = {n: given[n] for n in ['x', 'mem', 'positions']}
    grad_fn = _jax.value_and_grad(_loss, argnums=(0, 1))

    def one_microbatch(ex, loss_target):
        ex = dict(ex)
        diff = ex.pop(TWIN_DIFF_INPUT)
        return grad_fn(weights, diff, {**shared, **ex}, loss_target)

    if N_MICROBATCH == 1:
        loss, (grad_w, grad_x) = one_microbatch(per_example, given["loss_target"])
    else:
        def body(carry, xs):
            loss_sum, grad_sum = carry
            l_k, (gw_k, gx_k) = one_microbatch(xs[0], xs[1])
            with _jax.named_scope("update"):
                return (loss_sum + l_k, _jax.tree.map(_jnp.add, grad_sum, gw_k)), gx_k

        init = (_jnp.zeros((), _jnp.float32), _jax.tree.map(_jnp.zeros_like, weights))
        (loss, grad_w), grad_x = _jax.lax.scan(body, init, (per_example, given["loss_target"]))
    with _jax.named_scope("update"):
        delta_w, new_m, new_v = {}, {}, {}
        for n in TWIN_WEIGHTS:
            delta_w[n], new_m[n], new_v[n] = _adamw(weights[n], grad_w[n], given["m_" + n], given["v_" + n])
    return (loss, grad_x, *[grad_w[n] for n in TWIN_WEIGHTS], *[delta_w[n] for n in TWIN_WEIGHTS],
            *[new_m[n] for n in TWIN_WEIGHTS], *[new_v[n] for n in TWIN_WEIGHTS])
```

```python
import functools

import jax
import jax.numpy as jnp
from jax import lax
from jax.experimental import pallas as pl
from jax.experimental.pallas import tpu as pltpu

F32 = jnp.float32
MXU_DTYPE = jnp.bfloat16
LANES = 128
VMEM_LIMIT_BYTES = 56 * 1024 * 1024

NORM_EPS = 1e-6
ROPE_THETA = 10000.0
BLOCK = 128
HEAD_DIM = 64
SWA_HEADS, SWA_KV_HEADS, SWA_WINDOW = 8, 2, 128
MLA_HEADS, MLA_Q_RANK, MLA_KV_RANK, MLA_NOPE, MLA_ROPE, MLA_V = 8, 384, 256, 64, 32, 64
DIL_HEADS = 16
DIL_PATTERNS = ((128, 1), (512, 4), (2048, 16))
X_HEADS, X_HEAD_DIM = 4, 128
ADAM_LR, ADAM_B1, ADAM_B2, ADAM_EPS, ADAM_WD, ADAM_STEP = 0.001, 0.9, 0.999, 1e-08, 0.01, 10
MESH = pl.DeviceIdType.MESH
NEG_BIG = -1e30

NN = (((1,), (0,)), ((), ()))
NT = (((1,), (1,)), ((), ()))


def _dot(a, b, dims=NN):
    return lax.dot_general(a.astype(MXU_DTYPE), b.astype(MXU_DTYPE), dims, preferred_element_type=F32)


def _pcall(body, *, name, dims=None, **kw):
    params = pltpu.CompilerParams(dimension_semantics=dims, vmem_limit_bytes=VMEM_LIMIT_BYTES)
    return pl.pallas_call(body, name=name, compiler_params=params, **kw)


def _tile(n, pref):
    t = (min(pref, n) // LANES) * LANES
    while t >= LANES:
        if n % t == 0:
            return t
        t -= LANES
    return n


def _lane(shape):
    return lax.broadcasted_iota(jnp.int32, shape, 1)


def _cols_to_lanes(cols, rows):
    lane = _lane((rows, LANES))
    out = jnp.zeros((rows, LANES), F32)
    for j, col in enumerate(cols):
        out = jnp.where(lane == j, col, out)
    return out


def _mm(a, b, *, mode, name, res=None, out_dtype=F32, tm=512, tn=1024, tk=1024):
    if mode == "nn":
        (M, K), (K2, N) = a.shape, b.shape
    elif mode == "nt":
        (M, K), (N, K2) = a.shape, b.shape
    else:
        (K, M), (K2, N) = a.shape, b.shape
    assert K == K2, (a.shape, b.shape, mode)
    tm, tn, tk = _tile(M, tm), _tile(N, tn), _tile(K, tk)
    nk = K // tk

    def body(*refs):
        if res is None:
            a_ref, b_ref, o_ref, acc = refs
        else:
            a_ref, b_ref, r_ref, o_ref, acc = refs
        k = pl.program_id(2)

        @pl.when(k == 0)
        def _():
            acc[...] = jnp.zeros_like(acc) if res is None else r_ref[...].astype(F32)

        if mode == "nn":
            acc[...] += _dot(a_ref[...], b_ref[...], NN)
        elif mode == "nt":
            acc[...] += _dot(a_ref[...], b_ref[...], NT)
        else:
            acc[...] += _dot(a_ref[...].T, b_ref[...], NN)

        @pl.when(k == nk - 1)
        def _():
            o_ref[...] = acc[...].astype(o_ref.dtype)

    if mode == "nn":
        a_spec = pl.BlockSpec((tm, tk), lambda i, j, k: (i, k))
        b_spec = pl.BlockSpec((tk, tn), lambda i, j, k: (k, j))
    elif mode == "nt":
        a_spec = pl.BlockSpec((tm, tk), lambda i, j, k: (i, k))
        b_spec = pl.BlockSpec((tn, tk), lambda i, j, k: (j, k))
    else:
        a_spec = pl.BlockSpec((tk, tm), lambda i, j, k: (k, i))
        b_spec = pl.BlockSpec((tk, tn), lambda i, j, k: (k, j))
    o_spec = pl.BlockSpec((tm, tn), lambda i, j, k: (i, j))
    in_specs = [a_spec, b_spec] + ([] if res is None else [o_spec])
    args = (a, b) + (() if res is None else (res,))
    return _pcall(
        body, name=name, dims=("parallel", "parallel", "arbitrary"),
        grid=(M // tm, N // tn, nk), in_specs=in_specs, out_specs=o_spec,
        out_shape=jax.ShapeDtypeStruct((M, N), out_dtype),
        scratch_shapes=[pltpu.VMEM((tm, tn), F32)],
    )(*args)


def _rms_parts(xf):
    r = lax.rsqrt(jnp.mean(xf * xf, axis=-1, keepdims=True) + NORM_EPS)
    return xf * r, r


def _rms_bwd_rows(xf, g, dy):
    xhat, r = _rms_parts(xf)
    dxhat = dy * g
    dx = r * (dxhat - xhat * jnp.mean(dxhat * xhat, axis=-1, keepdims=True))
    return dx, dy * xhat


def _rmsnorm(x, g, *, name, out_dtype=MXU_DTYPE, tm=512):
    M, D = x.shape
    tm = _tile(M, tm)

    def body(x_ref, g_ref, o_ref):
        xhat, _ = _rms_parts(x_ref[...].astype(F32))
        o_ref[...] = (xhat * g_ref[...]).astype(o_ref.dtype)

    return _pcall(
        body, name=name, dims=("parallel",), grid=(M // tm,),
        in_specs=[pl.BlockSpec((tm, D), lambda i: (i, 0)), pl.BlockSpec((1, D), lambda i: (0, 0))],
        out_specs=pl.BlockSpec((tm, D), lambda i: (i, 0)),
        out_shape=jax.ShapeDtypeStruct((M, D), out_dtype),
    )(x, g.reshape(1, D))


def _rmsnorm_bwd(x, g, dy, *, name, dres=None, tm=512):
    M, D = x.shape
    tm = _tile(M, tm)

    def body(*refs):
        if dres is None:
            x_ref, g_ref, dy_ref, dx_ref, dg_ref = refs
        else:
            x_ref, g_ref, dy_ref, dr_ref, dx_ref, dg_ref = refs
        dx, dgp = _rms_bwd_rows(x_ref[...].astype(F32), g_ref[...], dy_ref[...].astype(F32))
        if dres is not None:
            dx = dx + dr_ref[...]
        dx_ref[...] = dx

        @pl.when(pl.program_id(0) == 0)
        def _():
            dg_ref[...] = jnp.zeros_like(dg_ref)

        dg_ref[...] += jnp.sum(dgp, axis=0, keepdims=True)

    row = pl.BlockSpec((tm, D), lambda i: (i, 0))
    vec = pl.BlockSpec((1, D), lambda i: (0, 0))
    in_specs = [row, vec, row] + ([] if dres is None else [row])
    args = (x, g.reshape(1, D), dy) + (() if dres is None else (dres,))
    return _pcall(
        body, name=name, dims=("arbitrary",), grid=(M // tm,), in_specs=in_specs, out_specs=[row, vec],
        out_shape=[jax.ShapeDtypeStruct((M, D), F32), jax.ShapeDtypeStruct((1, D), F32)],
    )(*args)


def _rope_chunk(t, c, s, half):
    lane = _lane(t.shape)
    swapped = jnp.where((lane % (2 * half)) < half, pltpu.roll(t, LANES - half, 1), pltpu.roll(t, half, 1))
    return t * c + swapped * s


def _rope_tables(positions):
    pos = positions.reshape(-1).astype(F32)[:, None]
    S = pos.shape[0]

    def cs(dh):
        inv_freq = ROPE_THETA ** (-jnp.arange(0, dh, 2, dtype=F32) / dh)
        ang = pos * inv_freq
        return jnp.cos(ang), jnp.sin(ang)

    c64, s64 = cs(HEAD_DIM)
    c32, s32 = cs(MLA_ROPE)
    z32, z64, z96 = (jnp.zeros((S, n), F32) for n in (32, 64, 96))
    return dict(
        c64=jnp.concatenate([c64, c64, c64, c64], 1), s64=jnp.concatenate([-s64, s64, -s64, s64], 1),
        ck=jnp.concatenate([c32, c32, z96], 1), sk=jnp.concatenate([-s32, s32, z96], 1),
        cm=jnp.concatenate([jnp.ones((S, 64), F32), c32, c32, z32], 1),
        sm=jnp.concatenate([z64, -s32, s32, z32], 1),
    )


def _attn_steps(mode, n_other, t_self, t_other):
    if mode == "band":
        assert t_self == t_other
        return 2
    return n_other


def _kv_block(mode, qi, kj):
    if mode == "band":
        return jnp.maximum(qi - 1 + kj, 0), (qi + kj) >= 1
    if mode == "causal":
        return jnp.minimum(kj, qi), kj <= qi
    return kj, None


def _q_block(mode, ki, qj, nq):
    if mode == "band":
        return jnp.minimum(ki + qj, nq - 1), (ki + qj) <= nq - 1
    if mode == "causal":
        return jnp.maximum(qj, ki), qj >= ki
    return qj, None


def _mask(mode, max_dist, qpos, kpos):
    d = qpos - kpos
    if mode == "band":
        return (d >= 0) & (d <= max_dist)
    if mode == "causal":
        return d >= 0
    return None


def _when(cond, fn):
    if cond is None:
        fn()
    else:
        pl.when(cond)(fn)


class _Attn:
    def __init__(self, *, T, Tk, G, nh, rep, dqk, dv, tq, tk, mode, scale, qcol, kcol, vcol, ocol, o_width,
                 max_dist=0):
        self.__dict__.update(locals())
        self.nkv = nh // rep
        assert T % tq == 0 and Tk % tk == 0 and nh <= LANES


def _attn_fwd(cfg, q, k, v, *, name, sink=None, out_dtype=F32):
    c = cfg
    nq, nk = c.T // c.tq, c.Tk // c.tk
    steps = _attn_steps(c.mode, nk, c.tq, c.tk)

    def body(*refs):
        if sink is None:
            q_ref, k_ref, v_ref, o_ref, lse_ref, m_scr, l_scr, acc = refs
        else:
            q_ref, k_ref, v_ref, sink_ref, o_ref, lse_ref, m_scr, l_scr, acc = refs
        qi, kj = pl.program_id(1), pl.program_id(2)
        kb, valid = _kv_block(c.mode, qi, kj)

        @pl.when(kj == 0)
        def _():
            if sink is None:
                m_scr[...] = jnp.full_like(m_scr, NEG_BIG)
                l_scr[...] = jnp.zeros_like(l_scr)
            else:
                m_scr[...] = jnp.broadcast_to(sink_ref[...], m_scr.shape)
                l_scr[...] = jnp.ones_like(l_scr)
            acc[...] = jnp.zeros_like(acc)

        def step():
            qpos = qi * c.tq + lax.broadcasted_iota(jnp.int32, (c.tq, c.tk), 0)
            kpos = kb * c.tk + lax.broadcasted_iota(jnp.int32, (c.tq, c.tk), 1)
            mask = _mask(c.mode, c.max_dist, qpos, kpos)
            for j in range(c.nh):
                g = j // c.rep
                s = _dot(q_ref[:, j * c.dqk:(j + 1) * c.dqk], k_ref[:, g * c.dqk:(g + 1) * c.dqk], NT) * c.scale
                if mask is not None:
                    s = jnp.where(mask, s, -jnp.inf)
                m_prev = m_scr[:, j:j + 1]
                m_new = jnp.maximum(m_prev, jnp.max(s, axis=1, keepdims=True))
                alpha = jnp.exp(m_prev - m_new)
                p = jnp.exp(s - m_new)
                l_scr[:, j:j + 1] = alpha * l_scr[:, j:j + 1] + jnp.sum(p, axis=1, keepdims=True)
                acc[:, j * c.dv:(j + 1) * c.dv] = (
                    alpha * acc[:, j * c.dv:(j + 1) * c.dv] + _dot(p, v_ref[:, g * c.dv:(g + 1) * c.dv], NN))
                m_scr[:, j:j + 1] = m_new

        _when(valid, step)

        @pl.when(kj == steps - 1)
        def _():
            for j in range(c.nh):
                o_ref[:, j * c.dv:(j + 1) * c.dv] = (
                    acc[:, j * c.dv:(j + 1) * c.dv] / l_scr[:, j:j + 1]).astype(o_ref.dtype)
            lane = _lane((c.tq, LANES))
            lse_ref[...] = jnp.where(lane < c.nh, m_scr[...] + jnp.log(jnp.maximum(l_scr[...], 1e-37)), 0.0)

    in_specs = [
        pl.BlockSpec((c.tq, c.nh * c.dqk), lambda g, i, j: (i, c.qcol(g))),
        pl.BlockSpec((c.tk, c.nkv * c.dqk), lambda g, i, j: (_kv_block(c.mode, i, j)[0], c.kcol(g))),
        pl.BlockSpec((c.tk, c.nkv * c.dv), lambda g, i, j: (_kv_block(c.mode, i, j)[0], c.vcol(g))),
    ]
    args = [q, k, v]
    if sink is not None:
        in_specs.append(pl.BlockSpec((1, LANES), lambda g, i, j: (0, 0)))
        args.append(sink)
    return _pcall(
        body, name=name, dims=("parallel", "parallel", "arbitrary"), grid=(c.G, nq, steps),
        in_specs=in_specs,
        out_specs=[pl.BlockSpec((c.tq, c.nh * c.dv), lambda g, i, j: (i, c.ocol(g))),
                   pl.BlockSpec((c.tq, LANES), lambda g, i, j: (i, g))],
        out_shape=[jax.ShapeDtypeStruct((c.T, c.o_width), out_dtype),
                   jax.ShapeDtypeStruct((c.T, LANES * c.G), F32)],
        scratch_shapes=[pltpu.VMEM((c.tq, LANES), F32), pltpu.VMEM((c.tq, LANES), F32),
                        pltpu.VMEM((c.tq, c.nh * c.dv), F32)],
    )(*args)


def _attn_delta(cfg, o, do, *, name, w=None, lse=None, sink=None, tm=512):
    c = cfg
    tm = _tile(c.T, tm)
    width = c.nh * c.dv

    def body(*refs):
        refs = list(refs)
        o_ref, do_ref = refs[:2]
        rest = refs[2:]
        w_ref = rest.pop(0) if w is not None else None
        lse_ref, sink_ref = (rest.pop(0), rest.pop(0)) if sink is not None else (None, None)
        d_ref = rest.pop(0)
        prod = o_ref[...].astype(F32) * do_ref[...].astype(F32)
        cols = [jnp.sum(prod[:, j * c.dv:(j + 1) * c.dv], axis=1, keepdims=True) for j in range(c.nh)]
        delta = _cols_to_lanes(cols, tm)
        if w is not None:
            delta = delta * w_ref[...]
        d_ref[...] = delta
        if sink is not None:
            ds_ref = rest.pop(0)

            @pl.when(pl.program_id(1) == 0)
            def _():
                ds_ref[...] = jnp.zeros_like(ds_ref)

            lane = _lane((tm, LANES))
            ps = jnp.where(lane < c.nh, jnp.exp(sink_ref[...] - lse_ref[...]), 0.0)
            ds_ref[...] -= jnp.sum(ps * delta, axis=0, keepdims=True)

    stat = pl.BlockSpec((tm, LANES), lambda g, i: (i, g))
    in_specs = [pl.BlockSpec((tm, width), lambda g, i: (i, c.ocol(g)))] * 2
    args = [o, do]
    out_specs, out_shape = [stat], [jax.ShapeDtypeStruct((c.T, LANES * c.G), F32)]
    if w is not None:
        in_specs.append(stat)
        args.append(w)
    if sink is not None:
        assert c.G == 1
        in_specs += [stat, pl.BlockSpec((1, LANES), lambda g, i: (0, 0))]
        args += [lse, sink]
        out_specs.append(pl.BlockSpec((1, LANES), lambda g, i: (0, 0)))
        out_shape.append(jax.ShapeDtypeStruct((1, LANES), F32))
    out = _pcall(
        body, name=name, dims=("arbitrary", "arbitrary"), grid=(c.G, c.T // tm),
        in_specs=in_specs, out_specs=out_specs, out_shape=out_shape,
    )(*args)
    return out if sink is not None else (out[0], None)


def _attn_dq(cfg, q, k, v, do, lse, delta, *, name, init=None, out_dtype=F32):
    c = cfg
    nq, nk = c.T // c.tq, c.Tk // c.tk
    steps = _attn_steps(c.mode, nk, c.tq, c.tk)
    qw = c.nh * c.dqk

    def body(*refs):
        if init is None:
            q_ref, k_ref, v_ref, do_ref, lse_ref, d_ref, dq_ref, acc = refs
        else:
            q_ref, k_ref, v_ref, do_ref, lse_ref, d_ref, init_ref, dq_ref, acc = refs
        qi, kj = pl.program_id(1), pl.program_id(2)
        kb, valid = _kv_block(c.mode, qi, kj)

        @pl.when(kj == 0)
        def _():
            acc[...] = jnp.zeros_like(acc) if init is None else init_ref[...].astype(F32)

        def step():
            qpos = qi * c.tq + lax.broadcasted_iota(jnp.int32, (c.tq, c.tk), 0)
            kpos = kb * c.tk + lax.broadcasted_iota(jnp.int32, (c.tq, c.tk), 1)
            mask = _mask(c.mode, c.max_dist, qpos, kpos)
            for j in range(c.nh):
                g = j // c.rep
                kh = k_ref[:, g * c.dqk:(g + 1) * c.dqk]
                s = _dot(q_ref[:, j * c.dqk:(j + 1) * c.dqk], kh, NT) * c.scale
                if mask is not None:
                    s = jnp.where(mask, s, -jnp.inf)
                p = jnp.exp(s - lse_ref[:, j:j + 1])
                dp = _dot(do_ref[:, j * c.dv:(j + 1) * c.dv], v_ref[:, g * c.dv:(g + 1) * c.dv], NT)
                ds = p * (dp - d_ref[:, j:j + 1]) * c.scale
                acc[:, j * c.dqk:(j + 1) * c.dqk] += _dot(ds, kh, NN)

        _when(valid, step)

        @pl.when(kj == steps - 1)
        def _():
            dq_ref[...] = acc[...].astype(dq_ref.dtype)

    kvb = lambda i, j: _kv_block(c.mode, i, j)[0]
    qspec = pl.BlockSpec((c.tq, qw), lambda g, i, j: (i, c.qcol(g)))
    stat = pl.BlockSpec((c.tq, LANES), lambda g, i, j: (i, g))
    in_specs = [
        qspec,
        pl.BlockSpec((c.tk, c.nkv * c.dqk), lambda g, i, j: (kvb(i, j), c.kcol(g))),
        pl.BlockSpec((c.tk, c.nkv * c.dv), lambda g, i, j: (kvb(i, j), c.vcol(g))),
        pl.BlockSpec((c.tq, c.nh * c.dv), lambda g, i, j: (i, c.ocol(g))),
        stat, stat,
    ]
    args = [q, k, v, do, lse, delta]
    dq_spec = pl.BlockSpec((c.tq, qw), lambda g, i, j: (i, g))
    if init is not None:
        in_specs.append(dq_spec)
        args.append(init)
    return _pcall(
        body, name=name, dims=("parallel", "parallel", "arbitrary"), grid=(c.G, nq, steps),
        in_specs=in_specs, out_specs=dq_spec,
        out_shape=jax.ShapeDtypeStruct((c.T, c.G * qw), out_dtype),
        scratch_shapes=[pltpu.VMEM((c.tq, qw), F32)],
    )(*args)


def _attn_dkv(cfg, q, k, v, do, lse, delta, *, name, init=None, out_dtype=F32):
    c = cfg
    nq, nk = c.T // c.tq, c.Tk // c.tk
    steps = _attn_steps(c.mode, nq, c.tk, c.tq)
    kw, vw = c.nkv * c.dqk, c.nkv * c.dv

    def body(*refs):
        if init is None:
            q_ref, k_ref, v_ref, do_ref, lse_ref, d_ref, dk_ref, dv_ref, dk_acc, dv_acc = refs
        else:
            q_ref, k_ref, v_ref, do_ref, lse_ref, d_ref, ik_ref, iv_ref, dk_ref, dv_ref, dk_acc, dv_acc = refs
        ki, qj = pl.program_id(1), pl.program_id(2)
        qb, valid = _q_block(c.mode, ki, qj, nq)

        @pl.when(qj == 0)
        def _():
            dk_acc[...] = jnp.zeros_like(dk_acc) if init is None else ik_ref[...].astype(F32)
            dv_acc[...] = jnp.zeros_like(dv_acc) if init is None else iv_ref[...].astype(F32)

        def step():
            kpos = ki * c.tk + lax.broadcasted_iota(jnp.int32, (c.tk, c.tq), 0)
            qpos = qb * c.tq + lax.broadcasted_iota(jnp.int32, (c.tk, c.tq), 1)
            mask = _mask(c.mode, c.max_dist, qpos, kpos)
            lse_t = lse_ref[...].T
            d_t = d_ref[...].T
            for j in range(c.nh):
                g = j // c.rep
                qh = q_ref[:, j * c.dqk:(j + 1) * c.dqk]
                doh = do_ref[:, j * c.dv:(j + 1) * c.dv]
                s_t = _dot(k_ref[:, g * c.dqk:(g + 1) * c.dqk], qh, NT) * c.scale
                if mask is not None:
                    s_t = jnp.where(mask, s_t, -jnp.inf)
                p_t = jnp.exp(s_t - lse_t[j:j + 1, :])
                dv_acc[:, g * c.dv:(g + 1) * c.dv] += _dot(p_t, doh, NN)
                dp_t = _dot(v_ref[:, g * c.dv:(g + 1) * c.dv], doh, NT)
                ds_t = p_t * (dp_t - d_t[j:j + 1, :]) * c.scale
                dk_acc[:, g * c.dqk:(g + 1) * c.dqk] += _dot(ds_t, qh, NN)

        _when(valid, step)

        @pl.when(qj == steps - 1)
        def _():
            dk_ref[...] = dk_acc[...].astype(dk_ref.dtype)
            dv_ref[...] = dv_acc[...].astype(dv_ref.dtype)

    qbi = lambda i, j: _q_block(c.mode, i, j, nq)[0]
    stat = pl.BlockSpec((c.tq, LANES), lambda g, i, j: (qbi(i, j), g))
    in_specs = [
        pl.BlockSpec((c.tq, c.nh * c.dqk), lambda g, i, j: (qbi(i, j), c.qcol(g))),
        pl.BlockSpec((c.tk, kw), lambda g, i, j: (i, c.kcol(g))),
        pl.BlockSpec((c.tk, vw), lambda g, i, j: (i, c.vcol(g))),
        pl.BlockSpec((c.tq, c.nh * c.dv), lambda g, i, j: (qbi(i, j), c.ocol(g))),
        stat, stat,
    ]
    args = [q, k, v, do, lse, delta]
    dk_spec = pl.BlockSpec((c.tk, kw), lambda g, i, j: (i, g))
    dv_spec = pl.BlockSpec((c.tk, vw), lambda g, i, j: (i, g))
    if init is not None:
        in_specs += [dk_spec, dv_spec]
        args += list(init)
    return _pcall(
        body, name=name, dims=("parallel", "parallel", "arbitrary"), grid=(c.G, nk, steps),
        in_specs=in_specs, out_specs=[dk_spec, dv_spec],
        out_shape=[jax.ShapeDtypeStruct((c.Tk, c.G * kw), out_dtype),
                   jax.ShapeDtypeStruct((c.Tk, c.G * vw), out_dtype)],
        scratch_shapes=[pltpu.VMEM((c.tk, kw), F32), pltpu.VMEM((c.tk, vw), F32)],
    )(*args)


def _rowwise(body, ins, outs, *, name, rows, tm=512, accs=()):
    tm = _tile(rows, tm)

    def spec(a):
        if a.shape[0] == rows:
            return pl.BlockSpec((tm, a.shape[1]), lambda i: (i, 0))
        assert a.shape[0] == 1
        return pl.BlockSpec((1, a.shape[1]), lambda i: (0, 0))

    return _pcall(
        functools.partial(body, tm), name=name, dims=("arbitrary" if accs else "parallel",), grid=(rows // tm,),
        in_specs=[spec(a) for a in ins], out_specs=[spec(a) for a in outs], out_shape=list(outs),
    )(*ins)


def _sds(shape, dtype=F32):
    return jax.ShapeDtypeStruct(shape, dtype)


def _acc_rows(ref, val):
    @pl.when(pl.program_id(0) == 0)
    def _():
        ref[...] = jnp.zeros_like(ref)

    ref[...] += jnp.sum(val, axis=0, keepdims=True)


Z_QA, Z_KA, Z_VA, Z_CQ, Z_CKV, Z_KR, Z_END = 0, 512, 640, 768, 1152, 1408, 1536


def _l0_prep(z, tabs, q_norm, kv_norm, *, name):
    S = z.shape[0]

    def body(tm, z_ref, c64, s64, ck, sk, gq, gkv, qa_o, ka_o, va_o, cq_o, ckv_o, kr_o):
        for i in range(4):
            sl = slice(Z_QA + i * LANES, Z_QA + (i + 1) * LANES)
            qa_o[:, i * LANES:(i + 1) * LANES] = _rope_chunk(z_ref[:, sl], c64[...], s64[...], 32).astype(qa_o.dtype)
        ka_o[...] = _rope_chunk(z_ref[:, Z_KA:Z_VA], c64[...], s64[...], 32).astype(ka_o.dtype)
        va_o[...] = z_ref[:, Z_VA:Z_CQ].astype(va_o.dtype)
        cq_o[...] = (_rms_parts(z_ref[:, Z_CQ:Z_CKV])[0] * gq[...]).astype(cq_o.dtype)
        ckv_o[...] = (_rms_parts(z_ref[:, Z_CKV:Z_KR])[0] * gkv[...]).astype(ckv_o.dtype)
        kr_o[...] = _rope_chunk(z_ref[:, Z_KR:Z_END], ck[...], sk[...], 16)

    outs = [_sds((S, 512), MXU_DTYPE), _sds((S, 128), MXU_DTYPE), _sds((S, 128), MXU_DTYPE),
            _sds((S, MLA_Q_RANK), MXU_DTYPE), _sds((S, MLA_KV_RANK), MXU_DTYPE), _sds((S, LANES))]
    ins = [z, tabs["c64"], tabs["s64"], tabs["ck"], tabs["sk"], q_norm.reshape(1, -1), kv_norm.reshape(1, -1)]
    return _rowwise(body, ins, outs, name=name, rows=S)


def _l0_prep_bwd(z, tabs, q_norm, kv_norm, dqa, dka, dva, dcq, dckv, dkr, *, name):
    S = z.shape[0]

    def body(tm, z_ref, c64, s64, ck, sk, gq, gkv, dqa_r, dka_r, dva_r, dcq_r, dckv_r, dkr_r, dz_o, dgq_o, dgkv_o):
        for i in range(4):
            sl = slice(i * LANES, (i + 1) * LANES)
            dz_o[:, sl] = _rope_chunk(dqa_r[:, sl].astype(F32), c64[...], -s64[...], 32).astype(dz_o.dtype)
        dz_o[:, Z_KA:Z_VA] = _rope_chunk(dka_r[...].astype(F32), c64[...], -s64[...], 32).astype(dz_o.dtype)
        dz_o[:, Z_VA:Z_CQ] = dva_r[...].astype(dz_o.dtype)
        dx, dgp = _rms_bwd_rows(z_ref[:, Z_CQ:Z_CKV], gq[...], dcq_r[...].astype(F32))
        dz_o[:, Z_CQ:Z_CKV] = dx.astype(dz_o.dtype)
        _acc_rows(dgq_o, dgp)
        dx, dgp = _rms_bwd_rows(z_ref[:, Z_CKV:Z_KR], gkv[...], dckv_r[...].astype(F32))
        dz_o[:, Z_CKV:Z_KR] = dx.astype(dz_o.dtype)
        _acc_rows(dgkv_o, dgp)
        dz_o[:, Z_KR:Z_END] = _rope_chunk(dkr_r[...], ck[...], -sk[...], 16).astype(dz_o.dtype)

    outs = [_sds((S, Z_END), MXU_DTYPE), _sds((1, MLA_Q_RANK)), _sds((1, MLA_KV_RANK))]
    ins = [z, tabs["c64"], tabs["s64"], tabs["ck"], tabs["sk"], q_norm.reshape(1, -1), kv_norm.reshape(1, -1),
           dqa, dka, dva, dcq, dckv, dkr]
    return _rowwise(body, ins, outs, name=name, rows=S, accs=(1, 2))


def _mla_prep(qb, kvb, kr, tabs, *, name):
    S = qb.shape[0]

    def body(tm, qb_r, kvb_r, kr_r, cm, sm, q_o, k_o, v_o):
        lane = _lane((tm, LANES))
        kr_at_64 = pltpu.roll(kr_r[...], 64, 1)
        for h in range(MLA_HEADS):
            sl = slice(h * LANES, (h + 1) * LANES)
            q_o[:, sl] = _rope_chunk(qb_r[:, sl], cm[...], sm[...], 16).astype(q_o.dtype)
            k_o[:, sl] = jnp.where(lane < 64, kvb_r[:, sl], kr_at_64).astype(k_o.dtype)
        for p in range(MLA_HEADS // 2):
            even = pltpu.roll(kvb_r[:, (2 * p) * LANES:(2 * p + 1) * LANES], 64, 1)
            odd = kvb_r[:, (2 * p + 1) * LANES:(2 * p + 2) * LANES]
            v_o[:, p * LANES:(p + 1) * LANES] = jnp.where(lane < 64, even, odd).astype(v_o.dtype)

    outs = [_sds((S, 1024), MXU_DTYPE), _sds((S, 1024), MXU_DTYPE), _sds((S, 512), MXU_DTYPE)]
    return _rowwise(body, [qb, kvb, kr, tabs["cm"], tabs["sm"]], outs, name=name, rows=S)


def _mla_prep_bwd(dq, dk, dv, tabs, *, name):
    S = dq.shape[0]

    def body(tm, dq_r, dk_r, dv_r, cm, sm, dqb_o, dkvb_o, dkr_o):
        lane = _lane((tm, LANES))
        dkr = jnp.zeros((tm, LANES), F32)
        for h in range(MLA_HEADS):
            sl = slice(h * LANES, (h + 1) * LANES)
            dqb_o[:, sl] = _rope_chunk(dq_r[:, sl].astype(F32), cm[...], -sm[...], 16).astype(dqb_o.dtype)
            dkh = dk_r[:, sl].astype(F32)
            dvp = dv_r[:, (h // 2) * LANES:(h // 2 + 1) * LANES].astype(F32)
            dvh = pltpu.roll(dvp, 64, 1) if h % 2 == 0 else dvp
            dkvb_o[:, sl] = jnp.where(lane < 64, dkh, dvh).astype(dkvb_o.dtype)
            dkr = dkr + pltpu.roll(dkh, 64, 1)
        dkr_o[...] = jnp.where(lane < MLA_ROPE, dkr, 0.0)

    outs = [_sds((S, 1024), MXU_DTYPE), _sds((S, 1024), MXU_DTYPE), _sds((S, LANES))]
    return _rowwise(body, [dq, dk, dv, tabs["cm"], tabs["sm"]], outs, name=name, rows=S)


def _l1_prep(qkv, tabs, *, name):
    S = qkv.shape[0]

    def body(tm, x_r, c64, s64, q_o, k_o, v_o):
        for i in range(8):
            sl = slice(i * LANES, (i + 1) * LANES)
            q_o[:, sl] = _rope_chunk(x_r[:, sl], c64[...], s64[...], 32).astype(q_o.dtype)
            k_o[:, sl] = _rope_chunk(x_r[:, 1024 + i * LANES:1024 + (i + 1) * LANES], c64[...], s64[...], 32).astype(k_o.dtype)
        v_o[...] = x_r[:, 2048:3072].astype(v_o.dtype)

    outs = [_sds((S, 1024), MXU_DTYPE)] * 3
    return _rowwise(body, [qkv, tabs["c64"], tabs["s64"]], outs, name=name, rows=S)


def _l1_prep_bwd(dq, dk, dv, tabs, *, name):
    S = dq.shape[0]

    def body(tm, dq_r, dk_r, dv_r, c64, s64, o):
        for i in range(8):
            sl = slice(i * LANES, (i + 1) * LANES)
            o[:, sl] = _rope_chunk(dq_r[:, sl], c64[...], -s64[...], 32).astype(o.dtype)
            o[:, 1024 + i * LANES:1024 + (i + 1) * LANES] = _rope_chunk(dk_r[:, sl], c64[...], -s64[...], 32).astype(o.dtype)
        o[:, 2048:3072] = dv_r[...].astype(o.dtype)

    return _rowwise(body, [dq, dk, dv, tabs["c64"], tabs["s64"]], [_sds((S, 3072), MXU_DTYPE)], name=name, rows=S)[0]


def _sigmoid(x):
    return 1.0 / (1.0 + jnp.exp(-x))


def _swiglu(gate, up, *, name):
    def body(tm, g_r, u_r, o):
        g = g_r[...]
        o[...] = (g * _sigmoid(g) * u_r[...]).astype(o.dtype)

    return _rowwise(body, [gate, up], [_sds(gate.shape, MXU_DTYPE)], name=name, rows=gate.shape[0], tm=256)[0]


def _swiglu_bwd(gate, up, dact, *, name):
    def body(tm, g_r, u_r, d_r, dg_o, du_o):
        g, d = g_r[...], d_r[...]
        sg = _sigmoid(g)
        dg_o[...] = (d * u_r[...] * (sg * (1.0 + g * (1.0 - sg)))).astype(dg_o.dtype)
        du_o[...] = (d * g * sg).astype(du_o.dtype)

    outs = [_sds(gate.shape, MXU_DTYPE)] * 2
    return _rowwise(body, [gate, up, dact], outs, name=name, rows=gate.shape[0], tm=256)


def _head_scale(w, o, j):
    return w[:, j:j + 1] * o[:, j * HEAD_DIM:(j + 1) * HEAD_DIM]


def _merge(os_, lses, *, name):
    S = os_[0].shape[0]

    def body(tm, o0, o1, o2, l0, l1, l2, o_o, w0_o, w1_o, w2_o):
        ls = [l0[...], l1[...], l2[...]]
        m = jnp.maximum(jnp.maximum(ls[0], ls[1]), ls[2])
        es = [jnp.exp(l - m) for l in ls]
        tot = es[0] + es[1] + es[2]
        ws = [e / tot for e in es]
        for w_o, w in zip((w0_o, w1_o, w2_o), ws):
            w_o[...] = w
        for j in range(DIL_HEADS):
            sl = slice(j * HEAD_DIM, (j + 1) * HEAD_DIM)
            o_o[:, sl] = _head_scale(ws[0], o0, j) + _head_scale(ws[1], o1, j) + _head_scale(ws[2], o2, j)

    outs = [_sds((S, 1024))] + [_sds((S, LANES))] * 3
    return _rowwise(body, list(os_) + list(lses), outs, name=name, rows=S, tm=256)


def _merge_bwd(do, ws, *, name):
    S = do.shape[0]

    def body(tm, do_r, w0, w1, w2, d0, d1, d2):
        for w_r, d_o in zip((w0, w1, w2), (d0, d1, d2)):
            w = w_r[...]
            for j in range(DIL_HEADS):
                d_o[:, j * HEAD_DIM:(j + 1) * HEAD_DIM] = _head_scale(w, do_r, j).astype(d_o.dtype)

    return _rowwise(body, [do] + list(ws), [_sds((S, 1024), MXU_DTYPE)] * 3, name=name, rows=S, tm=256)


def _loss_head(x, g, target, *, name):
    S, D = x.shape

    def body(tm, x_r, g_r, t_r, dx_o, dg_o, sq_o):
        xf = x_r[...]
        xhat, _ = _rms_parts(xf)
        err = xhat * g_r[...] - t_r[...]
        dx, dgp = _rms_bwd_rows(xf, g_r[...], err * (1.0 / D))
        dx_o[...] = dx
        _acc_rows(dg_o, dgp)
        _acc_rows(sq_o, err * err)

    return _rowwise(body, [x, g.reshape(1, D), target], [_sds((S, D)), _sds((1, D)), _sds((1, D))],
                    name=name, rows=S, accs=(1, 2))


def _adamw(w, g0, g1, m, v, *, name):
    c1 = 1.0 - ADAM_B1 ** ADAM_STEP
    c2 = 1.0 - ADAM_B2 ** ADAM_STEP

    def body(tm, w_r, g0_r, g1_r, m_r, v_r, g_o, d_o, m_o, v_o):
        g = g0_r[...] + g1_r[...]
        m_new = ADAM_B1 * m_r[...] + (1.0 - ADAM_B1) * g
        v_new = ADAM_B2 * v_r[...] + (1.0 - ADAM_B2) * (g * g)
        g_o[...] = g
        m_o[...] = m_new
        v_o[...] = v_new
        d_o[...] = -ADAM_LR * ((m_new / c1) / (jnp.sqrt(v_new / c2) + ADAM_EPS) + ADAM_WD * w_r[...])

    return _rowwise(body, [w, g0, g1, m, v], [_sds(w.shape)] * 4, name=name, rows=w.shape[0], tm=256)


def _sum4(parts, *, name):
    _, R, C = parts.shape
    tm = _tile(R, 256) if R % LANES == 0 else R

    def body(p_ref, o_ref):
        o_ref[...] = ((p_ref[0] + p_ref[1]) + p_ref[2]) + p_ref[3]

    return _pcall(
        body, name=name, dims=("parallel",), grid=(R // tm,),
        in_specs=[pl.BlockSpec((4, tm, C), lambda i: (0, i, 0))], out_specs=pl.BlockSpec((tm, C), lambda i: (i, 0)),
        out_shape=_sds((R, C)),
    )(parts)


def _position():
    return lax.axis_index("x"), lax.axis_index("y"), lax.axis_index("c")


def _exchange_chips(src, *, name, gather):
    shape = (4,) + tuple(src.shape[-2:])

    def body(src_ref, out_ref, send_sems, recv_sems, local_sem):
        x, y, c = _position()
        me = 2 * x + y
        peers = [(1 - x, y), (x, 1 - y), (1 - x, 1 - y)]

        def block(k):
            return src_ref if gather else src_ref.at[k]

        def copy(j, src_block, dst_slot):
            px, py = peers[j]
            return pltpu.make_async_remote_copy(
                src_ref=block(src_block), dst_ref=out_ref.at[dst_slot], send_sem=send_sems.at[j],
                recv_sem=recv_sems.at[j], device_id=(px, py, c), device_id_type=MESH)

        local = pltpu.make_async_copy(block(me), out_ref.at[me], local_sem)
        local.start()
        sends = [copy(j, 2 * px + py, me) for j, (px, py) in enumerate(peers)]
        for cp in sends:
            cp.start()
        for j, (px, py) in enumerate(peers):
            copy(j, me, 2 * px + py).wait_recv()
        for cp in sends:
            cp.wait_send()
        local.wait()

    hbm = pl.BlockSpec(memory_space=pltpu.HBM)
    return pl.pallas_call(
        body, name=name, in_specs=[hbm], out_specs=hbm, out_shape=jax.ShapeDtypeStruct(shape, src.dtype),
        scratch_shapes=[pltpu.SemaphoreType.DMA((3,)), pltpu.SemaphoreType.DMA((3,)), pltpu.SemaphoreType.DMA],
    )(src)


def _exchange_sibling(src, *, name):
    def body(src_ref, out_ref, send_sem, recv_sem):
        x, y, c = _position()
        cp = pltpu.make_async_remote_copy(
            src_ref=src_ref, dst_ref=out_ref, send_sem=send_sem, recv_sem=recv_sem,
            device_id=(x, y, 1 - c), device_id_type=MESH)
        cp.start()
        cp.wait()

    hbm = pl.BlockSpec(memory_space=pltpu.HBM)
    return pl.pallas_call(
        body, name=name, in_specs=[hbm], out_specs=hbm, out_shape=jax.ShapeDtypeStruct(src.shape, src.dtype),
        scratch_shapes=[pltpu.SemaphoreType.DMA, pltpu.SemaphoreType.DMA],
    )(src)


def _allreduce_small(vec, *, name):
    R, C = vec.shape

    def body(v_ref, o_ref, slots, send_sems, recv_sems):
        x, y, c = _position()
        me = 4 * x + 2 * y + c

        def peer(k):
            return x ^ ((k >> 2) & 1), y ^ ((k >> 1) & 1), c ^ (k & 1)

        def copy(k, slot):
            return pltpu.make_async_remote_copy(
                src_ref=v_ref, dst_ref=slots.at[slot], send_sem=send_sems.at[k - 1], recv_sem=recv_sems.at[k - 1],
                device_id=peer(k), device_id_type=MESH)

        slots[me] = v_ref[...]
        sends = [copy(k, me) for k in range(1, 8)]
        for cp in sends:
            cp.start()
        for k in range(1, 8):
            px, py, pc = peer(k)
            copy(k, 4 * px + 2 * py + pc).wait_recv()
        total = slots[0]
        for d in range(1, 8):
            total = total + slots[d]
        o_ref[...] = total
        for cp in sends:
            cp.wait_send()

    vmem = pl.BlockSpec(memory_space=pltpu.VMEM)
    return pl.pallas_call(
        body, name=name, in_specs=[vmem], out_specs=vmem, out_shape=jax.ShapeDtypeStruct((R, C), vec.dtype),
        scratch_shapes=[pltpu.VMEM((8, R, C), vec.dtype), pltpu.SemaphoreType.DMA((7,)), pltpu.SemaphoreType.DMA((7,))],
    )(vec)


def _cross_cfg(S, mem_len):
    return _Attn(T=S, Tk=mem_len, G=1, nh=X_HEADS, rep=1, dqk=X_HEAD_DIM, dv=X_HEAD_DIM, tq=512, tk=mem_len,
                 mode="none", scale=X_HEAD_DIM ** -0.5, qcol=lambda g: 0, kcol=lambda g: 0, vcol=lambda g: 1,
                 ocol=lambda g: 0, o_width=X_HEADS * X_HEAD_DIM)


def _swa_cfg(S):
    return _Attn(T=S, Tk=S, G=1, nh=SWA_HEADS, rep=SWA_HEADS // SWA_KV_HEADS, dqk=HEAD_DIM, dv=HEAD_DIM, tq=BLOCK,
                 tk=BLOCK, mode="band", max_dist=SWA_WINDOW - 1, scale=HEAD_DIM ** -0.5, qcol=lambda g: 0,
                 kcol=lambda g: 0, vcol=lambda g: 0, ocol=lambda g: 0, o_width=SWA_HEADS * HEAD_DIM)


def _mla_cfg(S):
    t = _tile(S, 512)
    return _Attn(T=S, Tk=S, G=MLA_HEADS // 2, nh=2, rep=1, dqk=LANES, dv=MLA_V, tq=t, tk=t, mode="causal",
                 scale=(MLA_NOPE + MLA_ROPE) ** -0.5, qcol=lambda g: g, kcol=lambda g: g, vcol=lambda g: g,
                 ocol=lambda g: g, o_width=MLA_HEADS * MLA_V)


def _dil_cfg(S, window, dil):
    return _Attn(T=S // dil, Tk=S // dil, G=dil, nh=DIL_HEADS, rep=1, dqk=HEAD_DIM, dv=HEAD_DIM, tq=BLOCK, tk=BLOCK,
                 mode="band", max_dist=window // dil, scale=HEAD_DIM ** -0.5, qcol=lambda g: g, kcol=lambda g: g,
                 vcol=lambda g: g, ocol=lambda g: g, o_width=dil * DIL_HEADS * HEAD_DIM)


def _rows_cfg(S):
    return _Attn(T=S, Tk=S, G=1, nh=DIL_HEADS, rep=1, dqk=HEAD_DIM, dv=HEAD_DIM, tq=BLOCK, tk=BLOCK, mode="none",
                 scale=1.0, qcol=lambda g: 0, kcol=lambda g: 0, vcol=lambda g: 0, ocol=lambda g: 0,
                 o_width=DIL_HEADS * HEAD_DIM)


def _cross_fwd(p, x, mem, W, vec):
    S = x.shape[0]
    cfg = _cross_cfg(S, mem.shape[0])
    hx = _rmsnorm(x, vec[p + "x_norm"], name=p + "x_norm")
    qx = _mm(hx, W[p + "w_xq"], mode="nn", name=p + "xq", out_dtype=MXU_DTYPE)
    memn = _rmsnorm(mem, vec[p + "mem_norm"], name=p + "mem_norm")
    kvx = _mm(memn, W[p + "w_xkv"], mode="nn", name=p + "xkv", out_dtype=MXU_DTYPE)
    ox, lse = _attn_fwd(cfg, qx, kvx, kvx, name=p + "x_attn", out_dtype=MXU_DTYPE)
    out = _mm(ox, W[p + "w_xo"], mode="nn", name=p + "xo", res=x)
    return out, (x, hx, qx, memn, kvx, ox, lse)


def _cross_bwd(p, dx, saved, mem, W, vec, dW, dvec):
    x, hx, qx, memn, kvx, ox, lse = saved
    cfg = _cross_cfg(x.shape[0], mem.shape[0])
    dox = _mm(dx, W[p + "w_xo"], mode="nt", name=p + "xo_dx", out_dtype=MXU_DTYPE)
    dW[p + "w_xo"] = _mm(ox, dx, mode="tn", name=p + "xo_dw")
    delta, _ = _attn_delta(cfg, ox, dox, name=p + "x_delta")
    dqx = _attn_dq(cfg, qx, kvx, kvx, dox, lse, delta, name=p + "x_dq", out_dtype=MXU_DTYPE)
    dkx, dvx = _attn_dkv(cfg, qx, kvx, kvx, dox, lse, delta, name=p + "x_dkv", out_dtype=MXU_DTYPE)
    dkvx = jnp.concatenate([dkx, dvx], axis=1)
    dhx = _mm(dqx, W[p + "w_xq"], mode="nt", name=p + "xq_dx")
    dW[p + "w_xq"] = _mm(hx, dqx, mode="tn", name=p + "xq_dw")
    dW[p + "w_xkv"] = _mm(memn, dkvx, mode="tn", name=p + "xkv_dw")
    dmemn = _mm(dkvx, W[p + "w_xkv"], mode="nt", name=p + "xkv_dx")
    _, dvec[p + "mem_norm"] = _rmsnorm_bwd(mem, vec[p + "mem_norm"], dmemn, name=p + "mem_norm_bwd")
    dx_in, dvec[p + "x_norm"] = _rmsnorm_bwd(x, vec[p + "x_norm"], dhx, name=p + "x_norm_bwd", dres=dx)
    return dx_in


def _ffn_fwd(p, x, W, vec):
    hf = _rmsnorm(x, vec[p + "ffn_norm"], name=p + "ffn_norm")
    gate = _mm(hf, W[p + "w_gate"], mode="nn", name=p + "gate")
    up = _mm(hf, W[p + "w_up"], mode="nn", name=p + "up")
    act = _swiglu(gate, up, name=p + "swiglu")
    out = _mm(act, W[p + "w_down"], mode="nn", name=p + "down", res=x)
    return out, (x, hf, gate, up, act)


def _ffn_bwd(p, dx, saved, W, vec, dW, dvec):
    x, hf, gate, up, act = saved
    dact = _mm(dx, W[p + "w_down"], mode="nt", name=p + "down_dx")
    dW[p + "w_down"] = _mm(act, dx, mode="tn", name=p + "down_dw")
    dgate, dup = _swiglu_bwd(gate, up, dact, name=p + "swiglu_bwd")
    dhf = _mm(dgate, W[p + "w_gate"], mode="nt", name=p + "gate_dx")
    dhf = _mm(dup, W[p + "w_up"], mode="nt", name=p + "up_dx", res=dhf)
    dW[p + "w_gate"] = _mm(hf, dgate, mode="tn", name=p + "gate_dw")
    dW[p + "w_up"] = _mm(hf, dup, mode="tn", name=p + "up_dw")
    dx_in, dvec[p + "ffn_norm"] = _rmsnorm_bwd(x, vec[p + "ffn_norm"], dhf, name=p + "ffn_norm_bwd", dres=dx)
    return dx_in


def _even_fwd(p, x, tabs, W, vec):
    S = x.shape[0]
    h = _rmsnorm(x, vec[p + "mix_norm"], name=p + "mix_norm")
    z = _mm(h, W[p + "w_in"], mode="nn", name=p + "in")
    qa, ka, va, cqn, ckvn, kr = _l0_prep(z, tabs, vec[p + "q_norm"], vec[p + "kv_norm"], name=p + "prep")
    sink = jnp.pad(vec[p + "sinks"], (0, LANES - SWA_HEADS)).reshape(1, LANES)
    oa, lse_a = _attn_fwd(_swa_cfg(S), qa, ka, va, name=p + "swa", sink=sink, out_dtype=MXU_DTYPE)
    qb = _mm(cqn, W[p + "w_uq"], mode="nn", name=p + "uq")
    kvb = _mm(ckvn, W[p + "w_ukv"], mode="nn", name=p + "ukv")
    Q, K, V = _mla_prep(qb, kvb, kr, tabs, name=p + "mla_prep")
    ob, lse_b = _attn_fwd(_mla_cfg(S), Q, K, V, name=p + "mla", out_dtype=MXU_DTYPE)
    o = jnp.concatenate([oa, ob], axis=1)
    out = _mm(o, W[p + "w_out"], mode="nn", name=p + "out", res=x)
    return out, (x, h, z, qa, ka, va, cqn, ckvn, sink, oa, lse_a, Q, K, V, ob, lse_b, o)


def _even_bwd(p, dx, saved, tabs, W, vec, dW, dvec):
    x, h, z, qa, ka, va, cqn, ckvn, sink, oa, lse_a, Q, K, V, ob, lse_b, o = saved
    S = x.shape[0]
    do = _mm(dx, W[p + "w_out"], mode="nt", name=p + "out_dx", out_dtype=MXU_DTYPE)
    dW[p + "w_out"] = _mm(o, dx, mode="tn", name=p + "out_dw")
    doa, dob = do[:, :SWA_HEADS * HEAD_DIM], do[:, SWA_HEADS * HEAD_DIM:]
    cfg = _swa_cfg(S)
    delta, dsink = _attn_delta(cfg, oa, doa, name=p + "swa_delta", lse=lse_a, sink=sink)
    dvec[p + "sinks"] = dsink
    dqa = _attn_dq(cfg, qa, ka, va, doa, lse_a, delta, name=p + "swa_dq")
    dka, dva = _attn_dkv(cfg, qa, ka, va, doa, lse_a, delta, name=p + "swa_dkv")
    cfg = _mla_cfg(S)
    delta, _ = _attn_delta(cfg, ob, dob, name=p + "mla_delta")
    dQ = _attn_dq(cfg, Q, K, V, dob, lse_b, delta, name=p + "mla_dq")
    dK, dV = _attn_dkv(cfg, Q, K, V, dob, lse_b, delta, name=p + "mla_dkv")
    dqb, dkvb, dkr = _mla_prep_bwd(dQ, dK, dV, tabs, name=p + "mla_prep_bwd")
    dcqn = _mm(dqb, W[p + "w_uq"], mode="nt", name=p + "uq_dx")
    dW[p + "w_uq"] = _mm(cqn, dqb, mode="tn", name=p + "uq_dw")
    dckvn = _mm(dkvb, W[p + "w_ukv"], mode="nt", name=p + "ukv_dx")
    dW[p + "w_ukv"] = _mm(ckvn, dkvb, mode="tn", name=p + "ukv_dw")
    dz, dvec[p + "q_norm"], dvec[p + "kv_norm"] = _l0_prep_bwd(
        z, tabs, vec[p + "q_norm"], vec[p + "kv_norm"], dqa, dka, dva, dcqn, dckvn, dkr, name=p + "prep_bwd")
    dh = _mm(dz, W[p + "w_in"], mode="nt", name=p + "in_dx")
    dW[p + "w_in"] = _mm(h, dz, mode="tn", name=p + "in_dw")
    dx_in, dvec[p + "mix_norm"] = _rmsnorm_bwd(x, vec[p + "mix_norm"], dh, name=p + "mix_norm_bwd", dres=dx)
    return dx_in


def _odd_fwd(p, x, tabs, W, vec):
    S = x.shape[0]
    assert S % (DIL_PATTERNS[-1][1] * BLOCK) == 0, "keys past the end of the sequence are never attended"
    h = _rmsnorm(x, vec[p + "mix_norm"], name=p + "mix_norm")
    qkv = _mm(h, W[p + "w_qkv"], mode="nn", name=p + "qkv")
    q, k, v = _l1_prep(qkv, tabs, name=p + "prep")
    outs, lses = [], []
    for window, dil in DIL_PATTERNS:
        cfg = _dil_cfg(S, window, dil)
        view = lambda t: t.reshape(S // dil, dil * t.shape[1])
        o_b, lse_b = _attn_fwd(cfg, view(q), view(k), view(v), name=p + "dil%d" % dil)
        outs.append(o_b.reshape(S, -1))
        lses.append(lse_b.reshape(S, LANES))
    o, w0, w1, w2 = _merge(outs, lses, name=p + "merge")
    out = _mm(o, W[p + "w_out"], mode="nn", name=p + "out", res=x)
    return out, (x, h, q, k, v, lses, (w0, w1, w2), o)


def _odd_bwd(p, dx, saved, tabs, W, vec, dW, dvec):
    x, h, q, k, v, lses, ws, o = saved
    S = x.shape[0]
    do = _mm(dx, W[p + "w_out"], mode="nt", name=p + "out_dx")
    dW[p + "w_out"] = _mm(o, dx, mode="tn", name=p + "out_dw")
    dos = _merge_bwd(do, ws, name=p + "merge_bwd")
    dq = dk = dv = None
    for b, (window, dil) in enumerate(DIL_PATTERNS):
        cfg = _dil_cfg(S, window, dil)
        view = lambda t: t.reshape(S // dil, dil * t.shape[1])
        delta, _ = _attn_delta(_rows_cfg(S), o, do, name=p + "delta%d" % dil, w=ws[b])
        args = (view(q), view(k), view(v), view(dos[b]), view(lses[b]), view(delta))
        dq = _attn_dq(cfg, *args, name=p + "dil%d_dq" % dil, init=None if dq is None else view(dq)).reshape(S, -1)
        dk, dv = _attn_dkv(cfg, *args, name=p + "dil%d_dkv" % dil, init=None if dk is None else (view(dk), view(dv)))
        dk, dv = dk.reshape(S, -1), dv.reshape(S, -1)
    dqkv = _l1_prep_bwd(dq, dk, dv, tabs, name=p + "prep_bwd")
    dh = _mm(dqkv, W[p + "w_qkv"], mode="nt", name=p + "qkv_dx")
    dW[p + "w_qkv"] = _mm(h, dqkv, mode="tn", name=p + "qkv_dw")
    dx_in, dvec[p + "mix_norm"] = _rmsnorm_bwd(x, vec[p + "mix_norm"], dh, name=p + "mix_norm_bwd", dres=dx)
    return dx_in


def _local_step(x, mem, positions, target, W, vec):
    tabs = _rope_tables(positions)
    x1, s_mix0 = _even_fwd("l0_", x, tabs, W, vec)
    x2, s_x0 = _cross_fwd("l0_", x1, mem, W, vec)
    x3, s_f0 = _ffn_fwd("l0_", x2, W, vec)
    x4, s_mix1 = _odd_fwd("l1_", x3, tabs, W, vec)
    x5, s_x1 = _cross_fwd("l1_", x4, mem, W, vec)
    x6, s_f1 = _ffn_fwd("l1_", x5, W, vec)
    dW, dvec = {}, {}
    dx, dvec["final_norm"], sq = _loss_head(x6, vec["final_norm"], target, name="loss_head")
    dx = _ffn_bwd("l1_", dx, s_f1, W, vec, dW, dvec)
    dx = _cross_bwd("l1_", dx, s_x1, mem, W, vec, dW, dvec)
    dx = _odd_bwd("l1_", dx, s_mix1, tabs, W, vec, dW, dvec)
    dx = _ffn_bwd("l0_", dx, s_f0, W, vec, dW, dvec)
    dx = _cross_bwd("l0_", dx, s_x0, mem, W, vec, dW, dvec)
    dx = _even_bwd("l0_", dx, s_mix0, tabs, W, vec, dW, dvec)
    return sq, dx, dW, dvec


_LAYER_MATS = {
    0: [("w_in", "col"), ("w_uq", "col"), ("w_ukv", "col"), ("w_out", "row"), ("w_xq", "row"), ("w_xkv", "row"),
        ("w_xo", "col"), ("w_gate", "col"), ("w_up", "col"), ("w_down", "row")],
    1: [("w_qkv", "col"), ("w_out", "row"), ("w_xq", "row"), ("w_xkv", "row"), ("w_xo", "col"), ("w_gate", "col"),
        ("w_up", "col"), ("w_down", "row")],
}
MATS = [("l%d_%s" % (l, n), kind) for l in (0, 1) for n, kind in _LAYER_MATS[l]]
_LAYER_VECS = {0: ["mix_norm", "sinks", "q_norm", "kv_norm", "x_norm", "mem_norm", "ffn_norm"],
               1: ["mix_norm", "x_norm", "mem_norm", "ffn_norm"]}
VECS = ["l%d_%s" % (l, n) for l in (0, 1) for n in _LAYER_VECS[l]] + ["final_norm"]
WEIGHT_ORDER = (["l0_mix_norm", "l0_w_in", "l0_sinks", "l0_q_norm", "l0_w_uq", "l0_kv_norm", "l0_w_ukv", "l0_w_out",
                 "l0_x_norm", "l0_mem_norm", "l0_w_xq", "l0_w_xkv", "l0_w_xo", "l0_ffn_norm", "l0_w_gate", "l0_w_up",
                 "l0_w_down", "l1_mix_norm", "l1_w_qkv", "l1_w_out", "l1_x_norm", "l1_mem_norm", "l1_w_xq",
                 "l1_w_xkv", "l1_w_xo", "l1_ffn_norm", "l1_w_gate", "l1_w_up", "l1_w_down", "final_norm"])
PACK_COLS = 1024
PACK_ROW_TILE = 256
VEC_ROWS = 16
LOSS_ROW = len(VECS)
N_CHIPS = 4


def _pack_layout(shards):
    layout, off = {}, 0
    for name, _ in MATS:
        n = shards[name].size // PACK_COLS
        assert n * PACK_COLS == shards[name].size
        layout[name] = (off, n)
        off += n
    return layout, -(-off // PACK_ROW_TILE) * PACK_ROW_TILE


def _pack_shards(shards, layout, rows, dtype):
    parts = [shards[name].astype(dtype).reshape(-1, PACK_COLS) for name, _ in MATS]
    used = sum(p.shape[0] for p in parts)
    return jnp.concatenate(parts + [jnp.zeros((rows - used, PACK_COLS), dtype)], axis=0)


def _unpack_shards(packed, layout, shards):
    return {name: packed[off:off + n].reshape(shards[name].shape) for name, (off, n) in layout.items()}


def _full_weights(gathered, layout, shards):
    W = {}
    for name, kind in MATS:
        off, n = layout[name]
        r, cw = shards[name].shape
        blocks = gathered[:, off:off + n].reshape(N_CHIPS, r, cw)
        W[name] = blocks.reshape(N_CHIPS * r, cw) if kind == "row" else (
            jnp.transpose(blocks, (1, 0, 2)).reshape(r, N_CHIPS * cw))
    W["l0_w_in"] = jnp.pad(W["l0_w_in"], ((0, 0), (0, Z_END - W["l0_w_in"].shape[1])))
    per_head = MLA_NOPE + MLA_ROPE
    uq = W["l0_w_uq"].reshape(MLA_Q_RANK, MLA_HEADS, per_head)
    W["l0_w_uq"] = jnp.pad(uq, ((0, 0), (0, 0), (0, LANES - per_head))).reshape(MLA_Q_RANK, MLA_HEADS * LANES)
    return W


def _pack_grads(dW, layout, rows, shards):
    per_head = MLA_NOPE + MLA_ROPE
    dW = dict(dW)
    dW["l0_w_in"] = dW["l0_w_in"][:, :Z_KR + MLA_ROPE]
    dW["l0_w_uq"] = dW["l0_w_uq"].reshape(MLA_Q_RANK, MLA_HEADS, LANES)[:, :, :per_head].reshape(MLA_Q_RANK, -1)
    parts = []
    for name, kind in MATS:
        r, cw = shards[name].shape
        g = dW[name]
        if kind == "col":
            g = jnp.transpose(g.reshape(r, N_CHIPS, cw), (1, 0, 2))
        parts.append(g.reshape(N_CHIPS, -1, PACK_COLS))
    used = sum(p.shape[1] for p in parts)
    return jnp.concatenate(parts + [jnp.zeros((N_CHIPS, rows - used, PACK_COLS), F32)], axis=1)


def _pack_vecs(vecs):
    rows = [jnp.pad(vecs[n].reshape(-1).astype(F32), (0, PACK_COLS - vecs[n].size)) for n in VECS]
    rows += [jnp.zeros((PACK_COLS,), F32)] * (VEC_ROWS - len(rows))
    return jnp.stack(rows)


def _unpack_vecs(packed, like):
    return {n: packed[i, :like[n].size].reshape(like[n].shape) for i, n in enumerate(VECS)}


def _step(a):
    weights = {n: a[n] for n in WEIGHT_ORDER}
    shards = {n: weights[n] for n, _ in MATS}
    vec = {n: weights[n] for n in VECS}
    layout, rows = _pack_layout(shards)

    gathered = _exchange_chips(_pack_shards(shards, layout, rows, MXU_DTYPE), name="gather_weights", gather=True)
    W = _full_weights(gathered, layout, shards)
    sq, grad_x, dW, dvec = _local_step(a["x"][0], a["mem"][0], a["positions"], a["loss_target"][0], W, vec)

    dvec = dict(dvec)
    dvec["l0_sinks"] = dvec["l0_sinks"][0, :SWA_HEADS]
    small = _pack_vecs(dvec)
    small = small.at[LOSS_ROW, 0].set(0.5 / a["x"].shape[-1] * jnp.sum(sq))
    small = _allreduce_small(small, name="reduce_gains")
    loss = small[LOSS_ROW, 0]
    g_s, d_s, m_s, v_s = _adamw(_pack_vecs(vec), small.at[LOSS_ROW, 0].set(0.0), jnp.zeros_like(small),
                                _pack_vecs({n: a["m_" + n] for n in VECS}), _pack_vecs({n: a["v_" + n] for n in VECS}),
                                name="adamw_gains")

    parts = _exchange_chips(_pack_grads(dW, layout, rows, shards), name="scatter_grads", gather=False)
    mine = _sum4(parts, name="sum_chips")
    other = _exchange_sibling(mine, name="swap_cores")
    g_w, d_w, m_w, v_w = _adamw(
        _pack_shards(shards, layout, rows, F32), mine, other,
        _pack_shards({n: a["m_" + n] for n, _ in MATS}, layout, rows, F32),
        _pack_shards({n: a["v_" + n] for n, _ in MATS}, layout, rows, F32), name="adamw_mats")

    out = [loss, grad_x[None]]
    for packed_w, packed_s in ((g_w, g_s), (d_w, d_s), (m_w, m_s), (v_w, v_s)):
        got = {**_unpack_shards(packed_w, layout, shards), **_unpack_vecs(packed_s, vec)}
        out += [got[n] for n in WEIGHT_ORDER]
    return tuple(out)


def kernel(x, mem, positions, l0_mix_norm, l0_w_in, l0_sinks, l0_q_norm, l0_w_uq, l0_kv_norm, l0_w_ukv, l0_w_out, l0_x_norm, l0_mem_norm, l0_w_xq, l0_w_xkv, l0_w_xo, l0_ffn_norm, l0_w_gate, l0_w_up, l0_w_down, l1_mix_norm, l1_w_qkv, l1_w_out, l1_x_norm, l1_mem_norm, l1_w_xq, l1_w_xkv, l1_w_xo, l1_ffn_norm, l1_w_gate, l1_w_up, l1_w_down, final_norm, loss_target, m_l0_mix_norm, m_l0_w_in, m_l0_sinks, m_l0_q_norm, m_l0_w_uq, m_l0_kv_norm, m_l0_w_ukv, m_l0_w_out, m_l0_x_norm, m_l0_mem_norm, m_l0_w_xq, m_l0_w_xkv, m_l0_w_xo, m_l0_ffn_norm, m_l0_w_gate, m_l0_w_up, m_l0_w_down, m_l1_mix_norm, m_l1_w_qkv, m_l1_w_out, m_l1_x_norm, m_l1_mem_norm, m_l1_w_xq, m_l1_w_xkv, m_l1_w_xo, m_l1_ffn_norm, m_l1_w_gate, m_l1_w_up, m_l1_w_down, m_final_norm, v_l0_mix_norm, v_l0_w_in, v_l0_sinks, v_l0_q_norm, v_l0_w_uq, v_l0_kv_norm, v_l0_w_ukv, v_l0_w_out, v_l0_x_norm, v_l0_mem_norm, v_l0_w_xq, v_l0_w_xkv, v_l0_w_xo, v_l0_ffn_norm, v_l0_w_gate, v_l0_w_up, v_l0_w_down, v_l1_mix_norm, v_l1_w_qkv, v_l1_w_out, v_l1_x_norm, v_l1_mem_norm, v_l1_w_xq, v_l1_w_xkv, v_l1_w_xo, v_l1_ffn_norm, v_l1_w_gate, v_l1_w_up, v_l1_w_down, v_final_norm):
    return _step(dict(locals()))
```

```python
import functools

import jax
import jax.numpy as jnp
from jax import lax
from jax.experimental import pallas as pl
from jax.experimental.pallas import tpu as pltpu

F32 = jnp.float32
MXU_DTYPE = jnp.bfloat16
LANES = 128
VMEM_LIMIT_BYTES = 56 * 1024 * 1024

NORM_EPS = 1e-6
ROPE_THETA = 10000.0
BLOCK = 128
HEAD_DIM = 64
SWA_HEADS, SWA_KV_HEADS, SWA_WINDOW = 8, 2, 128
MLA_HEADS, MLA_Q_RANK, MLA_KV_RANK, MLA_NOPE, MLA_ROPE, MLA_V = 8, 384, 256, 64, 32, 64
DIL_HEADS = 16
DIL_PATTERNS = ((128, 1), (512, 4), (2048, 16))
X_HEADS, X_HEAD_DIM = 4, 128
ADAM_LR, ADAM_B1, ADAM_B2, ADAM_EPS, ADAM_WD, ADAM_STEP = 0.001, 0.9, 0.999, 1e-08, 0.01, 10
MESH = pl.DeviceIdType.MESH
NEG_BIG = -1e30

NN = (((1,), (0,)), ((), ()))
NT = (((1,), (1,)), ((), ()))


def _dot(a, b, dims=NN):
    return lax.dot_general(a.astype(MXU_DTYPE), b.astype(MXU_DTYPE), dims, preferred_element_type=F32)


def _pcall(body, *, name, dims=None, **kw):
    params = pltpu.CompilerParams(dimension_semantics=dims, vmem_limit_bytes=VMEM_LIMIT_BYTES)
    return pl.pallas_call(body, name=name, compiler_params=params, **kw)


def _tile(n, pref):
    t = (min(pref, n) // LANES) * LANES
    while t >= LANES:
        if n % t == 0:
            return t
        t -= LANES
    return n


def _lane(shape):
    return lax.broadcasted_iota(jnp.int32, shape, 1)


def _cols_to_lanes(cols, rows):
    lane = _lane((rows, LANES))
    out = jnp.zeros((rows, LANES), F32)
    for j, col in enumerate(cols):
        out = jnp.where(lane == j, col, out)
    return out


def _mm(a, b, *, mode, name, res=None, out_dtype=F32, tm=1024, tn=1536, tk=1408):
    if mode == "nn":
        (M, K), (K2, N) = a.shape, b.shape
    elif mode == "nt":
        (M, K), (N, K2) = a.shape, b.shape
    else:
        (K, M), (K2, N) = a.shape, b.shape
    assert K == K2, (a.shape, b.shape, mode)
    tm, tn, tk = _tile(M, tm), _tile(N, tn), _tile(K, tk)
    nk = K // tk
    in_place = out_dtype == F32 or nk == 1

    def body(*refs):
        refs = list(refs)
        a_ref, b_ref = refs[:2]
        r_ref = refs[2] if res is not None else None
        o_ref = refs[3 if res is not None else 2]
        acc = o_ref if in_place else refs[-1]
        k = pl.program_id(2)
        if mode == "nn":
            part = _dot(a_ref[...], b_ref[...], NN)
        elif mode == "nt":
            part = _dot(a_ref[...], b_ref[...], NT)
        else:
            part = _dot(a_ref[...].T, b_ref[...], NN)
        if nk == 1:
            o_ref[...] = (part if res is None else part + r_ref[...].astype(F32)).astype(o_ref.dtype)
            return

        @pl.when(k == 0)
        def _():
            acc[...] = part if res is None else part + r_ref[...].astype(F32)

        @pl.when(k > 0)
        def _():
            acc[...] += part

        if not in_place:
            @pl.when(k == nk - 1)
            def _():
                o_ref[...] = acc[...].astype(o_ref.dtype)

    if mode == "nn":
        a_spec = pl.BlockSpec((tm, tk), lambda i, j, k: (i, k))
        b_spec = pl.BlockSpec((tk, tn), lambda i, j, k: (k, j))
    elif mode == "nt":
        a_spec = pl.BlockSpec((tm, tk), lambda i, j, k: (i, k))
        b_spec = pl.BlockSpec((tn, tk), lambda i, j, k: (j, k))
    else:
        a_spec = pl.BlockSpec((tk, tm), lambda i, j, k: (k, i))
        b_spec = pl.BlockSpec((tk, tn), lambda i, j, k: (k, j))
    o_spec = pl.BlockSpec((tm, tn), lambda i, j, k: (i, j))
    in_specs = [a_spec, b_spec] + ([] if res is None else [o_spec])
    args = (a, b) + (() if res is None else (res,))
    return _pcall(
        body, name=name, dims=("parallel", "parallel", "arbitrary"),
        grid=(M // tm, N // tn, nk), in_specs=in_specs, out_specs=o_spec,
        out_shape=jax.ShapeDtypeStruct((M, N), out_dtype),
        scratch_shapes=[] if in_place else [pltpu.VMEM((tm, tn), F32)],
    )(*args)


def _rms_parts(xf):
    r = lax.rsqrt(jnp.mean(xf * xf, axis=-1, keepdims=True) + NORM_EPS)
    return xf * r, r


def _rms_bwd_rows(xf, g, dy):
    xhat, r = _rms_parts(xf)
    dxhat = dy * g
    dx = r * (dxhat - xhat * jnp.mean(dxhat * xhat, axis=-1, keepdims=True))
    return dx, dy * xhat


def _rmsnorm(x, g, *, name, out_dtype=MXU_DTYPE, tm=512):
    M, D = x.shape
    tm = _tile(M, tm)

    def body(x_ref, g_ref, o_ref):
        xhat, _ = _rms_parts(x_ref[...].astype(F32))
        o_ref[...] = (xhat * g_ref[...]).astype(o_ref.dtype)

    return _pcall(
        body, name=name, dims=("parallel",), grid=(M // tm,),
        in_specs=[pl.BlockSpec((tm, D), lambda i: (i, 0)), pl.BlockSpec((1, D), lambda i: (0, 0))],
        out_specs=pl.BlockSpec((tm, D), lambda i: (i, 0)),
        out_shape=jax.ShapeDtypeStruct((M, D), out_dtype),
    )(x, g.reshape(1, D))


def _rmsnorm_bwd(x, g, dy, *, name, dres=None, tm=512):
    M, D = x.shape
    tm = _tile(M, tm)

    def body(*refs):
        if dres is None:
            x_ref, g_ref, dy_ref, dx_ref, dg_ref = refs
        else:
            x_ref, g_ref, dy_ref, dr_ref, dx_ref, dg_ref = refs
        dx, dgp = _rms_bwd_rows(x_ref[...].astype(F32), g_ref[...], dy_ref[...].astype(F32))
        if dres is not None:
            dx = dx + dr_ref[...]
        dx_ref[...] = dx

        @pl.when(pl.program_id(0) == 0)
        def _():
            dg_ref[...] = jnp.zeros_like(dg_ref)

        dg_ref[...] += jnp.sum(dgp, axis=0, keepdims=True)

    row = pl.BlockSpec((tm, D), lambda i: (i, 0))
    vec = pl.BlockSpec((1, D), lambda i: (0, 0))
    in_specs = [row, vec, row] + ([] if dres is None else [row])
    args = (x, g.reshape(1, D), dy) + (() if dres is None else (dres,))
    return _pcall(
        body, name=name, dims=("arbitrary",), grid=(M // tm,), in_specs=in_specs, out_specs=[row, vec],
        out_shape=[jax.ShapeDtypeStruct((M, D), F32), jax.ShapeDtypeStruct((1, D), F32)],
    )(*args)


def _rope_chunk(t, c, s, half):
    lane = _lane(t.shape)
    swapped = jnp.where((lane % (2 * half)) < half, pltpu.roll(t, LANES - half, 1), pltpu.roll(t, half, 1))
    return t * c + swapped * s


def _rope_tables(positions):
    pos = positions.reshape(-1).astype(F32)[:, None]
    S = pos.shape[0]

    def cs(dh):
        inv_freq = ROPE_THETA ** (-jnp.arange(0, dh, 2, dtype=F32) / dh)
        ang = pos * inv_freq
        return jnp.cos(ang), jnp.sin(ang)

    c64, s64 = cs(HEAD_DIM)
    c32, s32 = cs(MLA_ROPE)
    z32, z64, z96 = (jnp.zeros((S, n), F32) for n in (32, 64, 96))
    return dict(
        c64=jnp.concatenate([c64, c64, c64, c64], 1), s64=jnp.concatenate([-s64, s64, -s64, s64], 1),
        ck=jnp.concatenate([c32, c32, z96], 1), sk=jnp.concatenate([-s32, s32, z96], 1),
        cm=jnp.concatenate([jnp.ones((S, 64), F32), c32, c32, z32], 1),
        sm=jnp.concatenate([z64, -s32, s32, z32], 1),
    )


def _attn_steps(mode, n_other, t_self, t_other):
    if mode == "band":
        assert t_self == t_other
        return 2
    return n_other


def _kv_block(mode, qi, kj):
    if mode == "band":
        return jnp.maximum(qi - 1 + kj, 0), (qi + kj) >= 1
    if mode == "causal":
        return jnp.minimum(kj, qi), kj <= qi
    return kj, None


def _q_block(mode, ki, qj, nq):
    if mode == "band":
        return jnp.minimum(ki + qj, nq - 1), (ki + qj) <= nq - 1
    if mode == "causal":
        return jnp.maximum(qj, ki), qj >= ki
    return qj, None


def _mask(mode, max_dist, qpos, kpos):
    d = qpos - kpos
    if mode == "band":
        return (d >= 0) & (d <= max_dist)
    if mode == "causal":
        return d >= 0
    return None


def _when(cond, fn):
    if cond is None:
        fn()
    else:
        pl.when(cond)(fn)


class _Attn:
    def __init__(self, *, T, Tk, G, nh, rep, dqk, dv, tq, tk, mode, scale, qcol, kcol, vcol, ocol, o_width,
                 max_dist=0):
        self.__dict__.update(locals())
        self.nkv = nh // rep
        assert T % tq == 0 and Tk % tk == 0 and nh <= LANES


def _attn_fwd(cfg, q, k, v, *, name, sink=None, out_dtype=F32):
    c = cfg
    nq, nk = c.T // c.tq, c.Tk // c.tk
    steps = _attn_steps(c.mode, nk, c.tq, c.tk)

    def body(*refs):
        if sink is None:
            q_ref, k_ref, v_ref, o_ref, lse_ref, m_scr, l_scr, acc = refs
        else:
            q_ref, k_ref, v_ref, sink_ref, o_ref, lse_ref, m_scr, l_scr, acc = refs
        qi, kj = pl.program_id(1), pl.program_id(2)
        kb, valid = _kv_block(c.mode, qi, kj)

        @pl.when(kj == 0)
        def _():
            if sink is None:
                m_scr[...] = jnp.full_like(m_scr, NEG_BIG)
                l_scr[...] = jnp.zeros_like(l_scr)
            else:
                m_scr[...] = jnp.broadcast_to(sink_ref[...], m_scr.shape)
                l_scr[...] = jnp.ones_like(l_scr)
            acc[...] = jnp.zeros_like(acc)

        def step():
            qpos = qi * c.tq + lax.broadcasted_iota(jnp.int32, (c.tq, c.tk), 0)
            kpos = kb * c.tk + lax.broadcasted_iota(jnp.int32, (c.tq, c.tk), 1)
            mask = _mask(c.mode, c.max_dist, qpos, kpos)
            for j in range(c.nh):
                g = j // c.rep
                s = _dot(q_ref[:, j * c.dqk:(j + 1) * c.dqk], k_ref[:, g * c.dqk:(g + 1) * c.dqk], NT) * c.scale
                if mask is not None:
                    s = jnp.where(mask, s, -jnp.inf)
                m_prev = m_scr[:, j:j + 1]
                m_new = jnp.maximum(m_prev, jnp.max(s, axis=1, keepdims=True))
                alpha = jnp.exp(m_prev - m_new)
                p = jnp.exp(s - m_new)
                l_scr[:, j:j + 1] = alpha * l_scr[:, j:j + 1] + jnp.sum(p, axis=1, keepdims=True)
                acc[:, j * c.dv:(j + 1) * c.dv] = (
                    alpha * acc[:, j * c.dv:(j + 1) * c.dv] + _dot(p, v_ref[:, g * c.dv:(g + 1) * c.dv], NN))
                m_scr[:, j:j + 1] = m_new

        _when(valid, step)

        @pl.when(kj == steps - 1)
        def _():
            for j in range(c.nh):
                o_ref[:, j * c.dv:(j + 1) * c.dv] = (
                    acc[:, j * c.dv:(j + 1) * c.dv] / l_scr[:, j:j + 1]).astype(o_ref.dtype)
            lane = _lane((c.tq, LANES))
            lse_ref[...] = jnp.where(lane < c.nh, m_scr[...] + jnp.log(jnp.maximum(l_scr[...], 1e-37)), 0.0)

    in_specs = [
        pl.BlockSpec((c.tq, c.nh * c.dqk), lambda g, i, j: (i, c.qcol(g))),
        pl.BlockSpec((c.tk, c.nkv * c.dqk), lambda g, i, j: (_kv_block(c.mode, i, j)[0], c.kcol(g))),
        pl.BlockSpec((c.tk, c.nkv * c.dv), lambda g, i, j: (_kv_block(c.mode, i, j)[0], c.vcol(g))),
    ]
    args = [q, k, v]
    if sink is not None:
        in_specs.append(pl.BlockSpec((1, LANES), lambda g, i, j: (0, 0)))
        args.append(sink)
    return _pcall(
        body, name=name, dims=("parallel", "parallel", "arbitrary"), grid=(c.G, nq, steps),
        in_specs=in_specs,
        out_specs=[pl.BlockSpec((c.tq, c.nh * c.dv), lambda g, i, j: (i, c.ocol(g))),
                   pl.BlockSpec((c.tq, LANES), lambda g, i, j: (i, g))],
        out_shape=[jax.ShapeDtypeStruct((c.T, c.o_width), out_dtype),
                   jax.ShapeDtypeStruct((c.T, LANES * c.G), F32)],
        scratch_shapes=[pltpu.VMEM((c.tq, LANES), F32), pltpu.VMEM((c.tq, LANES), F32),
                        pltpu.VMEM((c.tq, c.nh * c.dv), F32)],
    )(*args)


def _attn_delta(cfg, o, do, *, name, w=None, lse=None, sink=None, tm=512):
    c = cfg
    tm = _tile(c.T, tm)
    width = c.nh * c.dv

    def body(*refs):
        refs = list(refs)
        o_ref, do_ref = refs[:2]
        rest = refs[2:]
        w_ref = rest.pop(0) if w is not None else None
        lse_ref, sink_ref = (rest.pop(0), rest.pop(0)) if sink is not None else (None, None)
        d_ref = rest.pop(0)
        prod = o_ref[...].astype(F32) * do_ref[...].astype(F32)
        cols = [jnp.sum(prod[:, j * c.dv:(j + 1) * c.dv], axis=1, keepdims=True) for j in range(c.nh)]
        delta = _cols_to_lanes(cols, tm)
        if w is not None:
            delta = delta * w_ref[...]
        d_ref[...] = delta
        if sink is not None:
            ds_ref = rest.pop(0)

            @pl.when(pl.program_id(1) == 0)
            def _():
                ds_ref[...] = jnp.zeros_like(ds_ref)

            lane = _lane((tm, LANES))
            ps = jnp.where(lane < c.nh, jnp.exp(sink_ref[...] - lse_ref[...]), 0.0)
            ds_ref[...] -= jnp.sum(ps * delta, axis=0, keepdims=True)

    stat = pl.BlockSpec((tm, LANES), lambda g, i: (i, g))
    in_specs = [pl.BlockSpec((tm, width), lambda g, i: (i, c.ocol(g)))] * 2
    args = [o, do]
    out_specs, out_shape = [stat], [jax.ShapeDtypeStruct((c.T, LANES * c.G), F32)]
    if w is not None:
        in_specs.append(stat)
        args.append(w)
    if sink is not None:
        assert c.G == 1
        in_specs += [stat, pl.BlockSpec((1, LANES), lambda g, i: (0, 0))]
        args += [lse, sink]
        out_specs.append(pl.BlockSpec((1, LANES), lambda g, i: (0, 0)))
        out_shape.append(jax.ShapeDtypeStruct((1, LANES), F32))
    out = _pcall(
        body, name=name, dims=("arbitrary", "arbitrary"), grid=(c.G, c.T // tm),
        in_specs=in_specs, out_specs=out_specs, out_shape=out_shape,
    )(*args)
    return out if sink is not None else (out[0], None)


def _attn_dq(cfg, q, k, v, do, lse, delta, *, name, init=None, out_dtype=F32):
    c = cfg
    nq, nk = c.T // c.tq, c.Tk // c.tk
    steps = _attn_steps(c.mode, nk, c.tq, c.tk)
    qw = c.nh * c.dqk

    def body(*refs):
        if init is None:
            q_ref, k_ref, v_ref, do_ref, lse_ref, d_ref, dq_ref, acc = refs
        else:
            q_ref, k_ref, v_ref, do_ref, lse_ref, d_ref, init_ref, dq_ref, acc = refs
        qi, kj = pl.program_id(1), pl.program_id(2)
        kb, valid = _kv_block(c.mode, qi, kj)

        @pl.when(kj == 0)
        def _():
            acc[...] = jnp.zeros_like(acc) if init is None else init_ref[...].astype(F32)

        def step():
            qpos = qi * c.tq + lax.broadcasted_iota(jnp.int32, (c.tq, c.tk), 0)
            kpos = kb * c.tk + lax.broadcasted_iota(jnp.int32, (c.tq, c.tk), 1)
            mask = _mask(c.mode, c.max_dist, qpos, kpos)
            for j in range(c.nh):
                g = j // c.rep
                kh = k_ref[:, g * c.dqk:(g + 1) * c.dqk]
                s = _dot(q_ref[:, j * c.dqk:(j + 1) * c.dqk], kh, NT) * c.scale
                if mask is not None:
                    s = jnp.where(mask, s, -jnp.inf)
                p = jnp.exp(s - lse_ref[:, j:j + 1])
                dp = _dot(do_ref[:, j * c.dv:(j + 1) * c.dv], v_ref[:, g * c.dv:(g + 1) * c.dv], NT)
                ds = p * (dp - d_ref[:, j:j + 1]) * c.scale
                acc[:, j * c.dqk:(j + 1) * c.dqk] += _dot(ds, kh, NN)

        _when(valid, step)

        @pl.when(kj == steps - 1)
        def _():
            dq_ref[...] = acc[...].astype(dq_ref.dtype)

    kvb = lambda i, j: _kv_block(c.mode, i, j)[0]
    qspec = pl.BlockSpec((c.tq, qw), lambda g, i, j: (i, c.qcol(g)))
    stat = pl.BlockSpec((c.tq, LANES), lambda g, i, j: (i, g))
    in_specs = [
        qspec,
        pl.BlockSpec((c.tk, c.nkv * c.dqk), lambda g, i, j: (kvb(i, j), c.kcol(g))),
        pl.BlockSpec((c.tk, c.nkv * c.dv), lambda g, i, j: (kvb(i, j), c.vcol(g))),
        pl.BlockSpec((c.tq, c.nh * c.dv), lambda g, i, j: (i, c.ocol(g))),
        stat, stat,
    ]
    args = [q, k, v, do, lse, delta]
    dq_spec = pl.BlockSpec((c.tq, qw), lambda g, i, j: (i, g))
    if init is not None:
        in_specs.append(dq_spec)
        args.append(init)
    return _pcall(
        body, name=name, dims=("parallel", "parallel", "arbitrary"), grid=(c.G, nq, steps),
        in_specs=in_specs, out_specs=dq_spec,
        out_shape=jax.ShapeDtypeStruct((c.T, c.G * qw), out_dtype),
        scratch_shapes=[pltpu.VMEM((c.tq, qw), F32)],
    )(*args)


def _attn_dkv(cfg, q, k, v, do, lse, delta, *, name, init=None, out_dtype=F32):
    c = cfg
    nq, nk = c.T // c.tq, c.Tk // c.tk
    steps = _attn_steps(c.mode, nq, c.tk, c.tq)
    kw, vw = c.nkv * c.dqk, c.nkv * c.dv

    def body(*refs):
        if init is None:
            q_ref, k_ref, v_ref, do_ref, lse_ref, d_ref, dk_ref, dv_ref, dk_acc, dv_acc = refs
        else:
            q_ref, k_ref, v_ref, do_ref, lse_ref, d_ref, ik_ref, iv_ref, dk_ref, dv_ref, dk_acc, dv_acc = refs
        ki, qj = pl.program_id(1), pl.program_id(2)
        qb, valid = _q_block(c.mode, ki, qj, nq)

        @pl.when(qj == 0)
        def _():
            dk_acc[...] = jnp.zeros_like(dk_acc) if init is None else ik_ref[...].astype(F32)
            dv_acc[...] = jnp.zeros_like(dv_acc) if init is None else iv_ref[...].astype(F32)

        def step():
            kpos = ki * c.tk + lax.broadcasted_iota(jnp.int32, (c.tk, c.tq), 0)
            qpos = qb * c.tq + lax.broadcasted_iota(jnp.int32, (c.tk, c.tq), 1)
            mask = _mask(c.mode, c.max_dist, qpos, kpos)
            lse_t = lse_ref[...].T
            d_t = d_ref[...].T
            for j in range(c.nh):
                g = j // c.rep
                qh = q_ref[:, j * c.dqk:(j + 1) * c.dqk]
                doh = do_ref[:, j * c.dv:(j + 1) * c.dv]
                s_t = _dot(k_ref[:, g * c.dqk:(g + 1) * c.dqk], qh, NT) * c.scale
                if mask is not None:
                    s_t = jnp.where(mask, s_t, -jnp.inf)
                p_t = jnp.exp(s_t - lse_t[j:j + 1, :])
                dv_acc[:, g * c.dv:(g + 1) * c.dv] += _dot(p_t, doh, NN)
                dp_t = _dot(v_ref[:, g * c.dv:(g + 1) * c.dv], doh, NT)
                ds_t = p_t * (dp_t - d_t[j:j + 1, :]) * c.scale
                dk_acc[:, g * c.dqk:(g + 1) * c.dqk] += _dot(ds_t, qh, NN)

        _when(valid, step)

        @pl.when(qj == steps - 1)
        def _():
            dk_ref[...] = dk_acc[...].astype(dk_ref.dtype)
            dv_ref[...] = dv_acc[...].astype(dv_ref.dtype)

    qbi = lambda i, j: _q_block(c.mode, i, j, nq)[0]
    stat = pl.BlockSpec((c.tq, LANES), lambda g, i, j: (qbi(i, j), g))
    in_specs = [
        pl.BlockSpec((c.tq, c.nh * c.dqk), lambda g, i, j: (qbi(i, j), c.qcol(g))),
        pl.BlockSpec((c.tk, kw), lambda g, i, j: (i, c.kcol(g))),
        pl.BlockSpec((c.tk, vw), lambda g, i, j: (i, c.vcol(g))),
        pl.BlockSpec((c.tq, c.nh * c.dv), lambda g, i, j: (qbi(i, j), c.ocol(g))),
        stat, stat,
    ]
    args = [q, k, v, do, lse, delta]
    dk_spec = pl.BlockSpec((c.tk, kw), lambda g, i, j: (i, g))
    dv_spec = pl.BlockSpec((c.tk, vw), lambda g, i, j: (i, g))
    if init is not None:
        in_specs += [dk_spec, dv_spec]
        args += list(init)
    return _pcall(
        body, name=name, dims=("parallel", "parallel", "arbitrary"), grid=(c.G, nk, steps),
        in_specs=in_specs, out_specs=[dk_spec, dv_spec],
        out_shape=[jax.ShapeDtypeStruct((c.Tk, c.G * kw), out_dtype),
                   jax.ShapeDtypeStruct((c.Tk, c.G * vw), out_dtype)],
        scratch_shapes=[pltpu.VMEM((c.tk, kw), F32), pltpu.VMEM((c.tk, vw), F32)],
    )(*args)


TN = (((0,), (0,)), ((), ()))


def _band_mask(c, i):
    row = lax.broadcasted_iota(jnp.int32, (BLOCK, 2 * BLOCK), 0)
    col = lax.broadcasted_iota(jnp.int32, (BLOCK, 2 * BLOCK), 1)
    d = BLOCK + row - col
    return (d >= 0) & (d <= c.max_dist) & ((col >= BLOCK) | (i > 0))


def _band_fwd(cfg, q, k, v, *, name, sink=None, out_dtype=F32):
    c = cfg
    assert c.mode == "band" and c.tq == c.tk == BLOCK and c.T == c.Tk
    nq = c.T // BLOCK

    def body(*refs):
        if sink is None:
            q_ref, kp_ref, kc_ref, vp_ref, vc_ref, o_ref, lse_ref = refs
        else:
            q_ref, kp_ref, kc_ref, vp_ref, vc_ref, sink_ref, o_ref, lse_ref = refs
        mask = _band_mask(c, pl.program_id(1))
        k2 = jnp.concatenate([kp_ref[...], kc_ref[...]], axis=0)
        v2 = jnp.concatenate([vp_ref[...], vc_ref[...]], axis=0)
        lses = []
        for j in range(c.nh):
            g = j // c.rep
            s = _dot(q_ref[:, j * c.dqk:(j + 1) * c.dqk], k2[:, g * c.dqk:(g + 1) * c.dqk], NT) * c.scale
            s = jnp.where(mask, s, -jnp.inf)
            m = jnp.max(s, axis=1, keepdims=True)
            if sink is not None:
                sk = sink_ref[:, j:j + 1]
                m = jnp.maximum(m, sk)
            p = jnp.exp(s - m)
            l = jnp.sum(p, axis=1, keepdims=True)
            if sink is not None:
                l = l + jnp.exp(sk - m)
            o_ref[:, j * c.dv:(j + 1) * c.dv] = (_dot(p, v2[:, g * c.dv:(g + 1) * c.dv], NN) / l).astype(o_ref.dtype)
            lses.append(m + jnp.log(l))
        lse_ref[...] = _cols_to_lanes(lses, BLOCK)

    prev = lambda i: jnp.maximum(i - 1, 0)
    kw, vw = c.nkv * c.dqk, c.nkv * c.dv
    in_specs = [
        pl.BlockSpec((BLOCK, c.nh * c.dqk), lambda g, i: (i, c.qcol(g))),
        pl.BlockSpec((BLOCK, kw), lambda g, i: (prev(i), c.kcol(g))),
        pl.BlockSpec((BLOCK, kw), lambda g, i: (i, c.kcol(g))),
        pl.BlockSpec((BLOCK, vw), lambda g, i: (prev(i), c.vcol(g))),
        pl.BlockSpec((BLOCK, vw), lambda g, i: (i, c.vcol(g))),
    ]
    args = [q, k, k, v, v]
    if sink is not None:
        in_specs.append(pl.BlockSpec((1, LANES), lambda g, i: (0, 0)))
        args.append(sink)
    return _pcall(
        body, name=name, dims=("parallel", "parallel"), grid=(c.G, nq), in_specs=in_specs,
        out_specs=[pl.BlockSpec((BLOCK, c.nh * c.dv), lambda g, i: (i, c.ocol(g))),
                   pl.BlockSpec((BLOCK, LANES), lambda g, i: (i, g))],
        out_shape=[jax.ShapeDtypeStruct((c.T, c.o_width), out_dtype),
                   jax.ShapeDtypeStruct((c.T, LANES * c.G), F32)],
    )(*args)


def _band_bwd(cfg, q, k, v, do, lse, delta, *, name, init=None):
    c = cfg
    assert c.mode == "band" and c.tq == c.tk == BLOCK and c.T == c.Tk
    nq = c.T // BLOCK
    qw, kw, vw = c.nh * c.dqk, c.nkv * c.dqk, c.nkv * c.dv

    def body(*refs):
        refs = list(refs)
        q_ref, kp_ref, kc_ref, vp_ref, vc_ref, do_ref, lse_ref, d_ref = refs[:8]
        iq_ref, ik_ref, iv_ref = refs[8:11] if init is not None else (None, None, None)
        dq_ref, dk_ref, dv_ref, dk_c, dv_c = refs[-5:]
        n = pl.program_id(1)

        def plus(val, ref, sl):
            return val if ref is None else val + ref[:, sl]

        @pl.when(n == 0)
        def _():
            dk_c[...] = jnp.zeros_like(dk_c)
            dv_c[...] = jnp.zeros_like(dv_c)

        @pl.when(n < nq)
        def _():
            mask = _band_mask(c, n)
            k2 = jnp.concatenate([kp_ref[...], kc_ref[...]], axis=0)
            v2 = jnp.concatenate([vp_ref[...], vc_ref[...]], axis=0)
            dk2, dv2 = [None] * c.nkv, [None] * c.nkv
            for j in range(c.nh):
                g = j // c.rep
                qs, os_ = slice(j * c.dqk, (j + 1) * c.dqk), slice(j * c.dv, (j + 1) * c.dv)
                qh, doh = q_ref[:, qs], do_ref[:, os_]
                kh, vh = k2[:, g * c.dqk:(g + 1) * c.dqk], v2[:, g * c.dv:(g + 1) * c.dv]
                s = jnp.where(mask, _dot(qh, kh, NT) * c.scale, -jnp.inf)
                p = jnp.exp(s - lse_ref[:, j:j + 1])
                ds = p * (_dot(doh, vh, NT) - d_ref[:, j:j + 1]) * c.scale
                dq_ref[:, qs] = plus(_dot(ds, kh, NN), iq_ref, qs)
                dvh, dkh = _dot(p, doh, TN), _dot(ds, qh, TN)
                dv2[g] = dvh if dv2[g] is None else dv2[g] + dvh
                dk2[g] = dkh if dk2[g] is None else dk2[g] + dkh
            for g in range(c.nkv):
                ks, vs = slice(g * c.dqk, (g + 1) * c.dqk), slice(g * c.dv, (g + 1) * c.dv)
                dk_ref[:, ks] = plus(dk_c[:, ks] + dk2[g][:BLOCK], ik_ref, ks)
                dv_ref[:, vs] = plus(dv_c[:, vs] + dv2[g][:BLOCK], iv_ref, vs)
                dk_c[:, ks] = dk2[g][BLOCK:]
                dv_c[:, vs] = dv2[g][BLOCK:]

        @pl.when(n == nq)
        def _():
            dk_ref[...] = plus(dk_c[...], ik_ref, slice(None))
            dv_ref[...] = plus(dv_c[...], iv_ref, slice(None))

    cur = lambda n: jnp.minimum(n, nq - 1)
    prev = lambda n: jnp.maximum(cur(n) - 1, 0)
    out_blk = lambda n: jnp.maximum(n - 1, 0)
    stat = pl.BlockSpec((BLOCK, LANES), lambda g, n: (cur(n), g))
    dq_spec = pl.BlockSpec((BLOCK, qw), lambda g, n: (cur(n), g))
    dk_spec = pl.BlockSpec((BLOCK, kw), lambda g, n: (out_blk(n), g))
    dv_spec = pl.BlockSpec((BLOCK, vw), lambda g, n: (out_blk(n), g))
    in_specs = [
        pl.BlockSpec((BLOCK, qw), lambda g, n: (cur(n), c.qcol(g))),
        pl.BlockSpec((BLOCK, kw), lambda g, n: (prev(n), c.kcol(g))),
        pl.BlockSpec((BLOCK, kw), lambda g, n: (cur(n), c.kcol(g))),
        pl.BlockSpec((BLOCK, vw), lambda g, n: (prev(n), c.vcol(g))),
        pl.BlockSpec((BLOCK, vw), lambda g, n: (cur(n), c.vcol(g))),
        pl.BlockSpec((BLOCK, c.nh * c.dv), lambda g, n: (cur(n), c.ocol(g))),
        stat, stat,
    ]
    args = [q, k, k, v, v, do, lse, delta]
    if init is not None:
        in_specs += [dq_spec, dk_spec, dv_spec]
        args += list(init)
    return _pcall(
        body, name=name, dims=("parallel", "arbitrary"), grid=(c.G, nq + 1), in_specs=in_specs,
        out_specs=[dq_spec, dk_spec, dv_spec],
        out_shape=[_sds((c.T, c.G * qw)), _sds((c.T, c.G * kw)), _sds((c.T, c.G * vw))],
        scratch_shapes=[pltpu.VMEM((BLOCK, kw), F32), pltpu.VMEM((BLOCK, vw), F32)],
    )(*args)


def _rowwise(body, ins, outs, *, name, rows, tm=512, accs=()):
    tm = _tile(rows, tm)

    def spec(a):
        if a.shape[0] == rows:
            return pl.BlockSpec((tm, a.shape[1]), lambda i: (i, 0))
        assert a.shape[0] == 1
        return pl.BlockSpec((1, a.shape[1]), lambda i: (0, 0))

    return _pcall(
        functools.partial(body, tm), name=name, dims=("arbitrary" if accs else "parallel",), grid=(rows // tm,),
        in_specs=[spec(a) for a in ins], out_specs=[spec(a) for a in outs], out_shape=list(outs),
    )(*ins)


def _sds(shape, dtype=F32):
    return jax.ShapeDtypeStruct(shape, dtype)


def _acc_rows(ref, val):
    @pl.when(pl.program_id(0) == 0)
    def _():
        ref[...] = jnp.zeros_like(ref)

    ref[...] += jnp.sum(val, axis=0, keepdims=True)


Z_QA, Z_KA, Z_VA, Z_CQ, Z_CKV, Z_KR, Z_END = 0, 512, 640, 768, 1152, 1408, 1536


def _l0_prep(z, tabs, q_norm, kv_norm, *, name):
    S = z.shape[0]

    def body(tm, z_ref, c64, s64, ck, sk, gq, gkv, qa_o, ka_o, va_o, cq_o, ckv_o, kr_o):
        for i in range(4):
            sl = slice(Z_QA + i * LANES, Z_QA + (i + 1) * LANES)
            qa_o[:, i * LANES:(i + 1) * LANES] = _rope_chunk(z_ref[:, sl], c64[...], s64[...], 32).astype(qa_o.dtype)
        ka_o[...] = _rope_chunk(z_ref[:, Z_KA:Z_VA], c64[...], s64[...], 32).astype(ka_o.dtype)
        va_o[...] = z_ref[:, Z_VA:Z_CQ].astype(va_o.dtype)
        cq_o[...] = (_rms_parts(z_ref[:, Z_CQ:Z_CKV])[0] * gq[...]).astype(cq_o.dtype)
        ckv_o[...] = (_rms_parts(z_ref[:, Z_CKV:Z_KR])[0] * gkv[...]).astype(ckv_o.dtype)
        kr_o[...] = _rope_chunk(z_ref[:, Z_KR:Z_END], ck[...], sk[...], 16)

    outs = [_sds((S, 512), MXU_DTYPE), _sds((S, 128), MXU_DTYPE), _sds((S, 128), MXU_DTYPE),
            _sds((S, MLA_Q_RANK), MXU_DTYPE), _sds((S, MLA_KV_RANK), MXU_DTYPE), _sds((S, LANES))]
    ins = [z, tabs["c64"], tabs["s64"], tabs["ck"], tabs["sk"], q_norm.reshape(1, -1), kv_norm.reshape(1, -1)]
    return _rowwise(body, ins, outs, name=name, rows=S)


def _l0_prep_bwd(z, tabs, q_norm, kv_norm, dqa, dka, dva, dcq, dckv, dkr, *, name):
    S = z.shape[0]

    def body(tm, z_ref, c64, s64, ck, sk, gq, gkv, dqa_r, dka_r, dva_r, dcq_r, dckv_r, dkr_r, dz_o, dgq_o, dgkv_o):
        for i in range(4):
            sl = slice(i * LANES, (i + 1) * LANES)
            dz_o[:, sl] = _rope_chunk(dqa_r[:, sl].astype(F32), c64[...], -s64[...], 32).astype(dz_o.dtype)
        dz_o[:, Z_KA:Z_VA] = _rope_chunk(dka_r[...].astype(F32), c64[...], -s64[...], 32).astype(dz_o.dtype)
        dz_o[:, Z_VA:Z_CQ] = dva_r[...].astype(dz_o.dtype)
        dx, dgp = _rms_bwd_rows(z_ref[:, Z_CQ:Z_CKV], gq[...], dcq_r[...].astype(F32))
        dz_o[:, Z_CQ:Z_CKV] = dx.astype(dz_o.dtype)
        _acc_rows(dgq_o, dgp)
        dx, dgp = _rms_bwd_rows(z_ref[:, Z_CKV:Z_KR], gkv[...], dckv_r[...].astype(F32))
        dz_o[:, Z_CKV:Z_KR] = dx.astype(dz_o.dtype)
        _acc_rows(dgkv_o, dgp)
        dz_o[:, Z_KR:Z_END] = _rope_chunk(dkr_r[...], ck[...], -sk[...], 16).astype(dz_o.dtype)

    outs = [_sds((S, Z_END), MXU_DTYPE), _sds((1, MLA_Q_RANK)), _sds((1, MLA_KV_RANK))]
    ins = [z, tabs["c64"], tabs["s64"], tabs["ck"], tabs["sk"], q_norm.reshape(1, -1), kv_norm.reshape(1, -1),
           dqa, dka, dva, dcq, dckv, dkr]
    return _rowwise(body, ins, outs, name=name, rows=S, accs=(1, 2))


def _mla_prep(qb, kvb, kr, tabs, *, name):
    S = qb.shape[0]

    def body(tm, qb_r, kvb_r, kr_r, cm, sm, q_o, k_o, v_o):
        lane = _lane((tm, LANES))
        kr_at_64 = pltpu.roll(kr_r[...], 64, 1)
        for h in range(MLA_HEADS):
            sl = slice(h * LANES, (h + 1) * LANES)
            q_o[:, sl] = _rope_chunk(qb_r[:, sl], cm[...], sm[...], 16).astype(q_o.dtype)
            k_o[:, sl] = jnp.where(lane < 64, kvb_r[:, sl], kr_at_64).astype(k_o.dtype)
        for p in range(MLA_HEADS // 2):
            even = pltpu.roll(kvb_r[:, (2 * p) * LANES:(2 * p + 1) * LANES], 64, 1)
            odd = kvb_r[:, (2 * p + 1) * LANES:(2 * p + 2) * LANES]
            v_o[:, p * LANES:(p + 1) * LANES] = jnp.where(lane < 64, even, odd).astype(v_o.dtype)

    outs = [_sds((S, 1024), MXU_DTYPE), _sds((S, 1024), MXU_DTYPE), _sds((S, 512), MXU_DTYPE)]
    return _rowwise(body, [qb, kvb, kr, tabs["cm"], tabs["sm"]], outs, name=name, rows=S)


def _mla_prep_bwd(dq, dk, dv, tabs, *, name):
    S = dq.shape[0]

    def body(tm, dq_r, dk_r, dv_r, cm, sm, dqb_o, dkvb_o, dkr_o):
        lane = _lane((tm, LANES))
        dkr = jnp.zeros((tm, LANES), F32)
        for h in range(MLA_HEADS):
            sl = slice(h * LANES, (h + 1) * LANES)
            dqb_o[:, sl] = _rope_chunk(dq_r[:, sl].astype(F32), cm[...], -sm[...], 16).astype(dqb_o.dtype)
            dkh = dk_r[:, sl].astype(F32)
            dvp = dv_r[:, (h // 2) * LANES:(h // 2 + 1) * LANES].astype(F32)
            dvh = pltpu.roll(dvp, 64, 1) if h % 2 == 0 else dvp
            dkvb_o[:, sl] = jnp.where(lane < 64, dkh, dvh).astype(dkvb_o.dtype)
            dkr = dkr + pltpu.roll(dkh, 64, 1)
        dkr_o[...] = jnp.where(lane < MLA_ROPE, dkr, 0.0)

    outs = [_sds((S, 1024), MXU_DTYPE), _sds((S, 1024), MXU_DTYPE), _sds((S, LANES))]
    return _rowwise(body, [dq, dk, dv, tabs["cm"], tabs["sm"]], outs, name=name, rows=S)


def _l1_prep(qkv, tabs, *, name):
    S = qkv.shape[0]

    def body(tm, x_r, c64, s64, q_o, k_o, v_o):
        for i in range(8):
            sl = slice(i * LANES, (i + 1) * LANES)
            q_o[:, sl] = _rope_chunk(x_r[:, sl], c64[...], s64[...], 32).astype(q_o.dtype)
            k_o[:, sl] = _rope_chunk(x_r[:, 1024 + i * LANES:1024 + (i + 1) * LANES], c64[...], s64[...], 32).astype(k_o.dtype)
        v_o[...] = x_r[:, 2048:3072].astype(v_o.dtype)

    outs = [_sds((S, 1024), MXU_DTYPE)] * 3
    return _rowwise(body, [qkv, tabs["c64"], tabs["s64"]], outs, name=name, rows=S)


def _l1_prep_bwd(dq, dk, dv, tabs, *, name):
    S = dq.shape[0]

    def body(tm, dq_r, dk_r, dv_r, c64, s64, o):
        for i in range(8):
            sl = slice(i * LANES, (i + 1) * LANES)
            o[:, sl] = _rope_chunk(dq_r[:, sl], c64[...], -s64[...], 32).astype(o.dtype)
            o[:, 1024 + i * LANES:1024 + (i + 1) * LANES] = _rope_chunk(dk_r[:, sl], c64[...], -s64[...], 32).astype(o.dtype)
        o[:, 2048:3072] = dv_r[...].astype(o.dtype)

    return _rowwise(body, [dq, dk, dv, tabs["c64"], tabs["s64"]], [_sds((S, 3072), MXU_DTYPE)], name=name, rows=S)[0]


def _sigmoid(x):
    return 1.0 / (1.0 + jnp.exp(-x))


def _swiglu(gate, up, *, name):
    def body(tm, g_r, u_r, o):
        g = g_r[...]
        o[...] = (g * _sigmoid(g) * u_r[...]).astype(o.dtype)

    return _rowwise(body, [gate, up], [_sds(gate.shape, MXU_DTYPE)], name=name, rows=gate.shape[0], tm=256)[0]


def _swiglu_bwd(gate, up, dact, *, name):
    def body(tm, g_r, u_r, d_r, dg_o, du_o):
        g, d = g_r[...], d_r[...]
        sg = _sigmoid(g)
        dg_o[...] = (d * u_r[...] * (sg * (1.0 + g * (1.0 - sg)))).astype(dg_o.dtype)
        du_o[...] = (d * g * sg).astype(du_o.dtype)

    outs = [_sds(gate.shape, MXU_DTYPE)] * 2
    return _rowwise(body, [gate, up, dact], outs, name=name, rows=gate.shape[0], tm=256)


def _head_scale(w, o, j):
    return w[:, j:j + 1] * o[:, j * HEAD_DIM:(j + 1) * HEAD_DIM]


def _merge(os_, lses, *, name):
    S = os_[0].shape[0]

    def body(tm, o0, o1, o2, l0, l1, l2, o_o, w0_o, w1_o, w2_o):
        ls = [l0[...], l1[...], l2[...]]
        m = jnp.maximum(jnp.maximum(ls[0], ls[1]), ls[2])
        es = [jnp.exp(l - m) for l in ls]
        tot = es[0] + es[1] + es[2]
        ws = [e / tot for e in es]
        for w_o, w in zip((w0_o, w1_o, w2_o), ws):
            w_o[...] = w
        for j in range(DIL_HEADS):
            sl = slice(j * HEAD_DIM, (j + 1) * HEAD_DIM)
            o_o[:, sl] = _head_scale(ws[0], o0, j) + _head_scale(ws[1], o1, j) + _head_scale(ws[2], o2, j)

    outs = [_sds((S, 1024))] + [_sds((S, LANES))] * 3
    return _rowwise(body, list(os_) + list(lses), outs, name=name, rows=S, tm=256)


def _merge_bwd(do, ws, *, name):
    S = do.shape[0]

    def body(tm, do_r, w0, w1, w2, d0, d1, d2):
        for w_r, d_o in zip((w0, w1, w2), (d0, d1, d2)):
            w = w_r[...]
            for j in range(DIL_HEADS):
                d_o[:, j * HEAD_DIM:(j + 1) * HEAD_DIM] = _head_scale(w, do_r, j).astype(d_o.dtype)

    return _rowwise(body, [do] + list(ws), [_sds((S, 1024), MXU_DTYPE)] * 3, name=name, rows=S, tm=256)


def _loss_head(x, g, target, *, name):
    S, D = x.shape

    def body(tm, x_r, g_r, t_r, dx_o, dg_o, sq_o):
        xf = x_r[...]
        xhat, _ = _rms_parts(xf)
        err = xhat * g_r[...] - t_r[...]
        dx, dgp = _rms_bwd_rows(xf, g_r[...], err * (1.0 / D))
        dx_o[...] = dx
        _acc_rows(dg_o, dgp)
        _acc_rows(sq_o, err * err)

    return _rowwise(body, [x, g.reshape(1, D), target], [_sds((S, D)), _sds((1, D)), _sds((1, D))],
                    name=name, rows=S, accs=(1, 2))


def _adamw(w, g, m, v, *, name):
    c1 = 1.0 - ADAM_B1 ** ADAM_STEP
    c2 = 1.0 - ADAM_B2 ** ADAM_STEP

    def body(tm, w_r, g_r, m_r, v_r, d_o, m_o, v_o):
        g = g_r[...]
        m_new = ADAM_B1 * m_r[...] + (1.0 - ADAM_B1) * g
        v_new = ADAM_B2 * v_r[...] + (1.0 - ADAM_B2) * (g * g)
        m_o[...] = m_new
        v_o[...] = v_new
        d_o[...] = -ADAM_LR * ((m_new / c1) / (jnp.sqrt(v_new / c2) + ADAM_EPS) + ADAM_WD * w_r[...])

    return _rowwise(body, [w, g, m, v], [_sds(w.shape)] * 3, name=name, rows=w.shape[0], tm=256)


SUM_ROW_TILE = 256


def _sum_cores(grads, theirs, half_index, *, name):
    _, R, C = grads.shape
    h = R // 2
    nb = h // SUM_ROW_TILE

    def body(c_ref, g_ref, t_ref, o_ref):
        o_ref[...] = (g_ref[...].astype(F32) + t_ref[...].astype(F32)).astype(o_ref.dtype)

    grid_spec = pltpu.PrefetchScalarGridSpec(
        num_scalar_prefetch=1, grid=(4, nb),
        in_specs=[pl.BlockSpec((1, SUM_ROW_TILE, C), lambda k, i, c_ref: (k, c_ref[0] * nb + i, 0)),
                  pl.BlockSpec((1, SUM_ROW_TILE, C), lambda k, i, c_ref: (k, i, 0))],
        out_specs=pl.BlockSpec((1, SUM_ROW_TILE, C), lambda k, i, c_ref: (k, i, 0)))
    return _pcall(body, name=name, dims=("parallel", "parallel"), grid_spec=grid_spec,
                  out_shape=_sds((4, h, C), grads.dtype))(half_index, grads, theirs)


def _sum_chips(parts, *, name):
    _, R, C = parts.shape

    def body(p_ref, o_ref):
        p = [p_ref[k].astype(F32) for k in range(4)]
        o_ref[...] = ((p[0] + p[1]) + p[2]) + p[3]

    return _pcall(
        body, name=name, dims=("parallel",), grid=(R // SUM_ROW_TILE,),
        in_specs=[pl.BlockSpec((4, SUM_ROW_TILE, C), lambda i: (0, i, 0))],
        out_specs=pl.BlockSpec((SUM_ROW_TILE, C), lambda i: (i, 0)), out_shape=_sds((R, C)),
    )(parts)


def _position():
    return lax.axis_index("x"), lax.axis_index("y"), lax.axis_index("c")


def _chip_peers(x, y):
    return [(1 - x, y), (x, 1 - y), (1 - x, 1 - y)]


_HBM = pl.BlockSpec(memory_space=pltpu.HBM)


def _gather_weights(src, *, name):
    R, C = src.shape
    h = R // 2

    def body(src_ref, out_ref, send_sems, recv_sems, local_sem):
        x, y, c = _position()
        me = 2 * x + y
        peers = _chip_peers(x, y)
        mine, other = pl.ds(c * h, h), pl.ds((1 - c) * h, h)

        def copy(sem, src_part, dst_part, device):
            return pltpu.make_async_remote_copy(
                src_ref=src_part, dst_ref=dst_part, send_sem=send_sems.at[sem], recv_sem=recv_sems.at[sem],
                device_id=device, device_id_type=MESH)

        local = pltpu.make_async_copy(src_ref, out_ref.at[me], local_sem)
        local.start()
        sends = [copy(j, src_ref.at[mine], out_ref.at[me, mine], (px, py, c)) for j, (px, py) in enumerate(peers)]
        for cp in sends:
            cp.start()
        passed = []
        for j, (px, py) in enumerate(peers):
            landed = out_ref.at[2 * px + py, mine]
            copy(j, landed, landed, (px, py, c)).wait_recv()
            passed.append(copy(3 + j, landed, landed, (x, y, 1 - c)))
            passed[-1].start()
        for j, (px, py) in enumerate(peers):
            theirs = out_ref.at[2 * px + py, other]
            copy(3 + j, theirs, theirs, (x, y, 1 - c)).wait_recv()
        for cp in sends + passed:
            cp.wait_send()
        local.wait()

    return pl.pallas_call(
        body, name=name, in_specs=[_HBM], out_specs=_HBM, out_shape=jax.ShapeDtypeStruct((4, R, C), src.dtype),
        scratch_shapes=[pltpu.SemaphoreType.DMA((6,)), pltpu.SemaphoreType.DMA((6,)), pltpu.SemaphoreType.DMA],
    )(src)


def _swap_other_half(src, *, name):
    _, R, C = src.shape
    h = R // 2

    def body(src_ref, out_ref, send_sem, recv_sem):
        x, y, c = _position()
        cp = pltpu.make_async_remote_copy(
            src_ref=src_ref.at[:, pl.ds((1 - c) * h, h)], dst_ref=out_ref, send_sem=send_sem, recv_sem=recv_sem,
            device_id=(x, y, 1 - c), device_id_type=MESH)
        cp.start()
        cp.wait()

    return pl.pallas_call(
        body, name=name, in_specs=[_HBM], out_specs=_HBM, out_shape=jax.ShapeDtypeStruct((4, h, C), src.dtype),
        scratch_shapes=[pltpu.SemaphoreType.DMA, pltpu.SemaphoreType.DMA],
    )(src)


def _scatter_chips(src, *, name):
    def body(src_ref, out_ref, send_sems, recv_sems, local_sem):
        x, y, c = _position()
        me = 2 * x + y
        peers = _chip_peers(x, y)

        def copy(j, src_block, dst_slot):
            px, py = peers[j]
            return pltpu.make_async_remote_copy(
                src_ref=src_ref.at[src_block], dst_ref=out_ref.at[dst_slot], send_sem=send_sems.at[j],
                recv_sem=recv_sems.at[j], device_id=(px, py, c), device_id_type=MESH)

        local = pltpu.make_async_copy(src_ref.at[me], out_ref.at[me], local_sem)
        local.start()
        sends = [copy(j, 2 * px + py, me) for j, (px, py) in enumerate(peers)]
        for cp in sends:
            cp.start()
        for j, (px, py) in enumerate(peers):
            copy(j, me, 2 * px + py).wait_recv()
        for cp in sends:
            cp.wait_send()
        local.wait()

    return pl.pallas_call(
        body, name=name, in_specs=[_HBM], out_specs=_HBM, out_shape=jax.ShapeDtypeStruct(src.shape, src.dtype),
        scratch_shapes=[pltpu.SemaphoreType.DMA((3,)), pltpu.SemaphoreType.DMA((3,)), pltpu.SemaphoreType.DMA],
    )(src)


def _join_halves(src, *, name):
    h, C = src.shape

    def body(src_ref, out_ref, send_sem, recv_sem, local_sem):
        x, y, c = _position()
        mine = pl.ds(c * h, h)
        local = pltpu.make_async_copy(src_ref, out_ref.at[mine], local_sem)
        local.start()
        cp = pltpu.make_async_remote_copy(
            src_ref=src_ref, dst_ref=out_ref.at[mine], send_sem=send_sem, recv_sem=recv_sem,
            device_id=(x, y, 1 - c), device_id_type=MESH)
        cp.start()
        theirs = out_ref.at[pl.ds((1 - c) * h, h)]
        pltpu.make_async_remote_copy(
            src_ref=src_ref, dst_ref=theirs, send_sem=send_sem, recv_sem=recv_sem,
            device_id=(x, y, 1 - c), device_id_type=MESH).wait_recv()
        cp.wait_send()
        local.wait()

    return pl.pallas_call(
        body, name=name, in_specs=[_HBM], out_specs=_HBM, out_shape=jax.ShapeDtypeStruct((2 * h, C), src.dtype),
        scratch_shapes=[pltpu.SemaphoreType.DMA, pltpu.SemaphoreType.DMA, pltpu.SemaphoreType.DMA],
    )(src)


def _allreduce_small(vec, *, name):
    R, C = vec.shape

    def body(v_ref, o_ref, slots, send_sems, recv_sems):
        x, y, c = _position()
        me = 4 * x + 2 * y + c

        def peer(k):
            return x ^ ((k >> 2) & 1), y ^ ((k >> 1) & 1), c ^ (k & 1)

        def copy(k, slot):
            return pltpu.make_async_remote_copy(
                src_ref=v_ref, dst_ref=slots.at[slot], send_sem=send_sems.at[k - 1], recv_sem=recv_sems.at[k - 1],
                device_id=peer(k), device_id_type=MESH)

        slots[me] = v_ref[...]
        sends = [copy(k, me) for k in range(1, 8)]
        for cp in sends:
            cp.start()
        for k in range(1, 8):
            px, py, pc = peer(k)
            copy(k, 4 * px + 2 * py + pc).wait_recv()
        total = slots[0]
        for d in range(1, 8):
            total = total + slots[d]
        o_ref[...] = total
        for cp in sends:
            cp.wait_send()

    vmem = pl.BlockSpec(memory_space=pltpu.VMEM)
    return pl.pallas_call(
        body, name=name, in_specs=[vmem], out_specs=vmem, out_shape=jax.ShapeDtypeStruct((R, C), vec.dtype),
        scratch_shapes=[pltpu.VMEM((8, R, C), vec.dtype), pltpu.SemaphoreType.DMA((7,)), pltpu.SemaphoreType.DMA((7,))],
    )(vec)


def _cross_cfg(S, mem_len):
    return _Attn(T=S, Tk=mem_len, G=1, nh=X_HEADS, rep=1, dqk=X_HEAD_DIM, dv=X_HEAD_DIM, tq=512, tk=mem_len,
                 mode="none", scale=X_HEAD_DIM ** -0.5, qcol=lambda g: 0, kcol=lambda g: 0, vcol=lambda g: 1,
                 ocol=lambda g: 0, o_width=X_HEADS * X_HEAD_DIM)


def _swa_cfg(S):
    return _Attn(T=S, Tk=S, G=1, nh=SWA_HEADS, rep=SWA_HEADS // SWA_KV_HEADS, dqk=HEAD_DIM, dv=HEAD_DIM, tq=BLOCK,
                 tk=BLOCK, mode="band", max_dist=SWA_WINDOW - 1, scale=HEAD_DIM ** -0.5, qcol=lambda g: 0,
                 kcol=lambda g: 0, vcol=lambda g: 0, ocol=lambda g: 0, o_width=SWA_HEADS * HEAD_DIM)


def _mla_cfg(S):
    t = _tile(S, 512)
    return _Attn(T=S, Tk=S, G=MLA_HEADS // 2, nh=2, rep=1, dqk=LANES, dv=MLA_V, tq=t, tk=t, mode="causal",
                 scale=(MLA_NOPE + MLA_ROPE) ** -0.5, qcol=lambda g: g, kcol=lambda g: g, vcol=lambda g: g,
                 ocol=lambda g: g, o_width=MLA_HEADS * MLA_V)


def _dil_cfg(S, window, dil):
    return _Attn(T=S // dil, Tk=S // dil, G=dil, nh=DIL_HEADS, rep=1, dqk=HEAD_DIM, dv=HEAD_DIM, tq=BLOCK, tk=BLOCK,
                 mode="band", max_dist=window // dil, scale=HEAD_DIM ** -0.5, qcol=lambda g: g, kcol=lambda g: g,
                 vcol=lambda g: g, ocol=lambda g: g, o_width=dil * DIL_HEADS * HEAD_DIM)


def _rows_cfg(S):
    return _Attn(T=S, Tk=S, G=1, nh=DIL_HEADS, rep=1, dqk=HEAD_DIM, dv=HEAD_DIM, tq=BLOCK, tk=BLOCK, mode="none",
                 scale=1.0, qcol=lambda g: 0, kcol=lambda g: 0, vcol=lambda g: 0, ocol=lambda g: 0,
                 o_width=DIL_HEADS * HEAD_DIM)


def _cross_fwd(p, x, mem, W, vec):
    S = x.shape[0]
    cfg = _cross_cfg(S, mem.shape[0])
    hx = _rmsnorm(x, vec[p + "x_norm"], name=p + "x_norm")
    qx = _mm(hx, W[p + "w_xq"], mode="nn", name=p + "xq", out_dtype=MXU_DTYPE)
    memn = _rmsnorm(mem, vec[p + "mem_norm"], name=p + "mem_norm")
    kvx = _mm(memn, W[p + "w_xkv"], mode="nn", name=p + "xkv", out_dtype=MXU_DTYPE)
    ox, lse = _attn_fwd(cfg, qx, kvx, kvx, name=p + "x_attn", out_dtype=MXU_DTYPE)
    out = _mm(ox, W[p + "w_xo"], mode="nn", name=p + "xo", res=x)
    return out, (x, hx, qx, memn, kvx, ox, lse)


def _cross_bwd(p, dx, saved, mem, W, vec, dW, dvec):
    x, hx, qx, memn, kvx, ox, lse = saved
    cfg = _cross_cfg(x.shape[0], mem.shape[0])
    dox = _mm(dx, W[p + "w_xo"], mode="nt", name=p + "xo_dx", out_dtype=MXU_DTYPE)
    dW[p + "w_xo"] = _mm(ox, dx, mode="tn", name=p + "xo_dw")
    delta, _ = _attn_delta(cfg, ox, dox, name=p + "x_delta")
    dqx = _attn_dq(cfg, qx, kvx, kvx, dox, lse, delta, name=p + "x_dq", out_dtype=MXU_DTYPE)
    dkx, dvx = _attn_dkv(cfg, qx, kvx, kvx, dox, lse, delta, name=p + "x_dkv", out_dtype=MXU_DTYPE)
    dkvx = jnp.concatenate([dkx, dvx], axis=1)
    dhx = _mm(dqx, W[p + "w_xq"], mode="nt", name=p + "xq_dx")
    dW[p + "w_xq"] = _mm(hx, dqx, mode="tn", name=p + "xq_dw")
    dW[p + "w_xkv"] = _mm(memn, dkvx, mode="tn", name=p + "xkv_dw")
    dmemn = _mm(dkvx, W[p + "w_xkv"], mode="nt", name=p + "xkv_dx")
    _, dvec[p + "mem_norm"] = _rmsnorm_bwd(mem, vec[p + "mem_norm"], dmemn, name=p + "mem_norm_bwd")
    dx_in, dvec[p + "x_norm"] = _rmsnorm_bwd(x, vec[p + "x_norm"], dhx, name=p + "x_norm_bwd", dres=dx)
    return dx_in


def _ffn_fwd(p, x, W, vec):
    hf = _rmsnorm(x, vec[p + "ffn_norm"], name=p + "ffn_norm")
    gate = _mm(hf, W[p + "w_gate"], mode="nn", name=p + "gate")
    up = _mm(hf, W[p + "w_up"], mode="nn", name=p + "up")
    act = _swiglu(gate, up, name=p + "swiglu")
    out = _mm(act, W[p + "w_down"], mode="nn", name=p + "down", res=x)
    return out, (x, hf, gate, up, act)


def _ffn_bwd(p, dx, saved, W, vec, dW, dvec):
    x, hf, gate, up, act = saved
    dact = _mm(dx, W[p + "w_down"], mode="nt", name=p + "down_dx")
    dW[p + "w_down"] = _mm(act, dx, mode="tn", name=p + "down_dw")
    dgate, dup = _swiglu_bwd(gate, up, dact, name=p + "swiglu_bwd")
    dhf = _mm(dgate, W[p + "w_gate"], mode="nt", name=p + "gate_dx")
    dhf = _mm(dup, W[p + "w_up"], mode="nt", name=p + "up_dx", res=dhf)
    dW[p + "w_gate"] = _mm(hf, dgate, mode="tn", name=p + "gate_dw")
    dW[p + "w_up"] = _mm(hf, dup, mode="tn", name=p + "up_dw")
    dx_in, dvec[p + "ffn_norm"] = _rmsnorm_bwd(x, vec[p + "ffn_norm"], dhf, name=p + "ffn_norm_bwd", dres=dx)
    return dx_in


def _even_fwd(p, x, tabs, W, vec):
    S = x.shape[0]
    h = _rmsnorm(x, vec[p + "mix_norm"], name=p + "mix_norm")
    z = _mm(h, W[p + "w_in"], mode="nn", name=p + "in")
    qa, ka, va, cqn, ckvn, kr = _l0_prep(z, tabs, vec[p + "q_norm"], vec[p + "kv_norm"], name=p + "prep")
    sink = jnp.pad(vec[p + "sinks"], (0, LANES - SWA_HEADS)).reshape(1, LANES)
    oa, lse_a = _band_fwd(_swa_cfg(S), qa, ka, va, name=p + "swa", sink=sink, out_dtype=MXU_DTYPE)
    qb = _mm(cqn, W[p + "w_uq"], mode="nn", name=p + "uq")
    kvb = _mm(ckvn, W[p + "w_ukv"], mode="nn", name=p + "ukv")
    Q, K, V = _mla_prep(qb, kvb, kr, tabs, name=p + "mla_prep")
    ob, lse_b = _attn_fwd(_mla_cfg(S), Q, K, V, name=p + "mla", out_dtype=MXU_DTYPE)
    o = jnp.concatenate([oa, ob], axis=1)
    out = _mm(o, W[p + "w_out"], mode="nn", name=p + "out", res=x)
    return out, (x, h, z, qa, ka, va, cqn, ckvn, sink, oa, lse_a, Q, K, V, ob, lse_b, o)


def _even_bwd(p, dx, saved, tabs, W, vec, dW, dvec):
    x, h, z, qa, ka, va, cqn, ckvn, sink, oa, lse_a, Q, K, V, ob, lse_b, o = saved
    S = x.shape[0]
    do = _mm(dx, W[p + "w_out"], mode="nt", name=p + "out_dx", out_dtype=MXU_DTYPE)
    dW[p + "w_out"] = _mm(o, dx, mode="tn", name=p + "out_dw")
    doa, dob = do[:, :SWA_HEADS * HEAD_DIM], do[:, SWA_HEADS * HEAD_DIM:]
    cfg = _swa_cfg(S)
    delta, dsink = _attn_delta(cfg, oa, doa, name=p + "swa_delta", lse=lse_a, sink=sink)
    dvec[p + "sinks"] = dsink
    dqa, dka, dva = _band_bwd(cfg, qa, ka, va, doa, lse_a, delta, name=p + "swa_bwd")
    cfg = _mla_cfg(S)
    delta, _ = _attn_delta(cfg, ob, dob, name=p + "mla_delta")
    dQ = _attn_dq(cfg, Q, K, V, dob, lse_b, delta, name=p + "mla_dq")
    dK, dV = _attn_dkv(cfg, Q, K, V, dob, lse_b, delta, name=p + "mla_dkv")
    dqb, dkvb, dkr = _mla_prep_bwd(dQ, dK, dV, tabs, name=p + "mla_prep_bwd")
    dcqn = _mm(dqb, W[p + "w_uq"], mode="nt", name=p + "uq_dx")
    dW[p + "w_uq"] = _mm(cqn, dqb, mode="tn", name=p + "uq_dw")
    dckvn = _mm(dkvb, W[p + "w_ukv"], mode="nt", name=p + "ukv_dx")
    dW[p + "w_ukv"] = _mm(ckvn, dkvb, mode="tn", name=p + "ukv_dw")
    dz, dvec[p + "q_norm"], dvec[p + "kv_norm"] = _l0_prep_bwd(
        z, tabs, vec[p + "q_norm"], vec[p + "kv_norm"], dqa, dka, dva, dcqn, dckvn, dkr, name=p + "prep_bwd")
    dh = _mm(dz, W[p + "w_in"], mode="nt", name=p + "in_dx")
    dW[p + "w_in"] = _mm(h, dz, mode="tn", name=p + "in_dw")
    dx_in, dvec[p + "mix_norm"] = _rmsnorm_bwd(x, vec[p + "mix_norm"], dh, name=p + "mix_norm_bwd", dres=dx)
    return dx_in


def _odd_fwd(p, x, tabs, W, vec):
    S = x.shape[0]
    assert S % (DIL_PATTERNS[-1][1] * BLOCK) == 0, "keys past the end of the sequence are never attended"
    h = _rmsnorm(x, vec[p + "mix_norm"], name=p + "mix_norm")
    qkv = _mm(h, W[p + "w_qkv"], mode="nn", name=p + "qkv")
    q, k, v = _l1_prep(qkv, tabs, name=p + "prep")
    outs, lses = [], []
    for window, dil in DIL_PATTERNS:
        cfg = _dil_cfg(S, window, dil)
        view = lambda t: t.reshape(S // dil, dil * t.shape[1])
        o_b, lse_b = _band_fwd(cfg, view(q), view(k), view(v), name=p + "dil%d" % dil)
        outs.append(o_b.reshape(S, -1))
        lses.append(lse_b.reshape(S, LANES))
    o, w0, w1, w2 = _merge(outs, lses, name=p + "merge")
    out = _mm(o, W[p + "w_out"], mode="nn", name=p + "out", res=x)
    return out, (x, h, q, k, v, lses, (w0, w1, w2), o)


def _odd_bwd(p, dx, saved, tabs, W, vec, dW, dvec):
    x, h, q, k, v, lses, ws, o = saved
    S = x.shape[0]
    do = _mm(dx, W[p + "w_out"], mode="nt", name=p + "out_dx")
    dW[p + "w_out"] = _mm(o, dx, mode="tn", name=p + "out_dw")
    dos = _merge_bwd(do, ws, name=p + "merge_bwd")
    dq = dk = dv = None
    for b, (window, dil) in enumerate(DIL_PATTERNS):
        cfg = _dil_cfg(S, window, dil)
        view = lambda t: t.reshape(S // dil, dil * t.shape[1])
        delta, _ = _attn_delta(_rows_cfg(S), o, do, name=p + "delta%d" % dil, w=ws[b])
        args = (view(q), view(k), view(v), view(dos[b]), view(lses[b]), view(delta))
        init = None if dq is None else (view(dq), view(dk), view(dv))
        dq, dk, dv = (t.reshape(S, -1) for t in _band_bwd(cfg, *args, name=p + "dil%d_bwd" % dil, init=init))
    dqkv = _l1_prep_bwd(dq, dk, dv, tabs, name=p + "prep_bwd")
    dh = _mm(dqkv, W[p + "w_qkv"], mode="nt", name=p + "qkv_dx")
    dW[p + "w_qkv"] = _mm(h, dqkv, mode="tn", name=p + "qkv_dw")
    dx_in, dvec[p + "mix_norm"] = _rmsnorm_bwd(x, vec[p + "mix_norm"], dh, name=p + "mix_norm_bwd", dres=dx)
    return dx_in


def _local_step(x, mem, positions, target, W, vec):
    tabs = _rope_tables(positions)
    x1, s_mix0 = _even_fwd("l0_", x, tabs, W, vec)
    x2, s_x0 = _cross_fwd("l0_", x1, mem, W, vec)
    x3, s_f0 = _ffn_fwd("l0_", x2, W, vec)
    x4, s_mix1 = _odd_fwd("l1_", x3, tabs, W, vec)
    x5, s_x1 = _cross_fwd("l1_", x4, mem, W, vec)
    x6, s_f1 = _ffn_fwd("l1_", x5, W, vec)
    dW, dvec = {}, {}
    dx, dvec["final_norm"], sq = _loss_head(x6, vec["final_norm"], target, name="loss_head")
    dx = _ffn_bwd("l1_", dx, s_f1, W, vec, dW, dvec)
    dx = _cross_bwd("l1_", dx, s_x1, mem, W, vec, dW, dvec)
    dx = _odd_bwd("l1_", dx, s_mix1, tabs, W, vec, dW, dvec)
    dx = _ffn_bwd("l0_", dx, s_f0, W, vec, dW, dvec)
    dx = _cross_bwd("l0_", dx, s_x0, mem, W, vec, dW, dvec)
    dx = _even_bwd("l0_", dx, s_mix0, tabs, W, vec, dW, dvec)
    return sq, dx, dW, dvec


_LAYER_MATS = {
    0: [("w_in", "col"), ("w_uq", "col"), ("w_ukv", "col"), ("w_out", "row"), ("w_xq", "row"), ("w_xkv", "row"),
        ("w_xo", "col"), ("w_gate", "col"), ("w_up", "col"), ("w_down", "row")],
    1: [("w_qkv", "col"), ("w_out", "row"), ("w_xq", "row"), ("w_xkv", "row"), ("w_xo", "col"), ("w_gate", "col"),
        ("w_up", "col"), ("w_down", "row")],
}
MATS = [("l%d_%s" % (l, n), kind) for l in (0, 1) for n, kind in _LAYER_MATS[l]]
_LAYER_VECS = {0: ["mix_norm", "sinks", "q_norm", "kv_norm", "x_norm", "mem_norm", "ffn_norm"],
               1: ["mix_norm", "x_norm", "mem_norm", "ffn_norm"]}
VECS = ["l%d_%s" % (l, n) for l in (0, 1) for n in _LAYER_VECS[l]] + ["final_norm"]
WEIGHT_ORDER = (["l0_mix_norm", "l0_w_in", "l0_sinks", "l0_q_norm", "l0_w_uq", "l0_kv_norm", "l0_w_ukv", "l0_w_out",
                 "l0_x_norm", "l0_mem_norm", "l0_w_xq", "l0_w_xkv", "l0_w_xo", "l0_ffn_norm", "l0_w_gate", "l0_w_up",
                 "l0_w_down", "l1_mix_norm", "l1_w_qkv", "l1_w_out", "l1_x_norm", "l1_mem_norm", "l1_w_xq",
                 "l1_w_xkv", "l1_w_xo", "l1_ffn_norm", "l1_w_gate", "l1_w_up", "l1_w_down", "final_norm"])
PACK_COLS = 1024
PACK_ROW_TILE = 2 * SUM_ROW_TILE
EXCHANGE_DTYPE = jnp.bfloat16
VEC_ROWS = 16
LOSS_ROW = len(VECS)
N_CHIPS = 4


def _pack_layout(shards):
    layout, off = {}, 0
    for name, _ in MATS:
        n = shards[name].size // PACK_COLS
        assert n * PACK_COLS == shards[name].size
        layout[name] = (off, n)
        off += n
    return layout, -(-off // PACK_ROW_TILE) * PACK_ROW_TILE


def _pack_shards(shards, layout, rows, dtype):
    parts = [shards[name].astype(dtype).reshape(-1, PACK_COLS) for name, _ in MATS]
    used = sum(p.shape[0] for p in parts)
    return jnp.concatenate(parts + [jnp.zeros((rows - used, PACK_COLS), dtype)], axis=0)


def _unpack_shards(packed, layout, shards):
    return {name: packed[off:off + n].reshape(shards[name].shape) for name, (off, n) in layout.items()}


def _full_weights(gathered, layout, shards):
    W = {}
    for name, kind in MATS:
        off, n = layout[name]
        r, cw = shards[name].shape
        blocks = gathered[:, off:off + n].reshape(N_CHIPS, r, cw)
        W[name] = blocks.reshape(N_CHIPS * r, cw) if kind == "row" else (
            jnp.transpose(blocks, (1, 0, 2)).reshape(r, N_CHIPS * cw))
    W["l0_w_in"] = jnp.pad(W["l0_w_in"], ((0, 0), (0, Z_END - W["l0_w_in"].shape[1])))
    per_head = MLA_NOPE + MLA_ROPE
    uq = W["l0_w_uq"].reshape(MLA_Q_RANK, MLA_HEADS, per_head)
    W["l0_w_uq"] = jnp.pad(uq, ((0, 0), (0, 0), (0, LANES - per_head))).reshape(MLA_Q_RANK, MLA_HEADS * LANES)
    return W


def _pack_grads(dW, layout, rows, shards):
    per_head = MLA_NOPE + MLA_ROPE
    dW = dict(dW)
    dW["l0_w_in"] = dW["l0_w_in"][:, :Z_KR + MLA_ROPE]
    dW["l0_w_uq"] = dW["l0_w_uq"].reshape(MLA_Q_RANK, MLA_HEADS, LANES)[:, :, :per_head].reshape(MLA_Q_RANK, -1)
    parts = []
    for name, kind in MATS:
        r, cw = shards[name].shape
        g = dW[name]
        if kind == "col":
            g = jnp.transpose(g.reshape(r, N_CHIPS, cw), (1, 0, 2))
        parts.append(g.reshape(N_CHIPS, -1, PACK_COLS).astype(EXCHANGE_DTYPE))
    used = sum(p.shape[1] for p in parts)
    return jnp.concatenate(parts + [jnp.zeros((N_CHIPS, rows - used, PACK_COLS), EXCHANGE_DTYPE)], axis=1)


def _pack_vecs(vecs):
    rows = [jnp.pad(vecs[n].reshape(-1).astype(F32), (0, PACK_COLS - vecs[n].size)) for n in VECS]
    rows += [jnp.zeros((PACK_COLS,), F32)] * (VEC_ROWS - len(rows))
    return jnp.stack(rows)


def _unpack_vecs(packed, like):
    return {n: packed[i, :like[n].size].reshape(like[n].shape) for i, n in enumerate(VECS)}


def _step(a):
    weights = {n: a[n] for n in WEIGHT_ORDER}
    shards = {n: weights[n] for n, _ in MATS}
    vec = {n: weights[n] for n in VECS}
    layout, rows = _pack_layout(shards)

    gathered = _gather_weights(_pack_shards(shards, layout, rows, MXU_DTYPE), name="gather_weights")
    W = _full_weights(gathered, layout, shards)
    sq, grad_x, dW, dvec = _local_step(a["x"][0], a["mem"][0], a["positions"], a["loss_target"][0], W, vec)

    dvec = dict(dvec)
    dvec["l0_sinks"] = dvec["l0_sinks"][0, :SWA_HEADS]
    small = _pack_vecs(dvec)
    small = small.at[LOSS_ROW, 0].set(0.5 / a["x"].shape[-1] * jnp.sum(sq))
    small = _allreduce_small(small, name="reduce_gains")
    loss = small[LOSS_ROW, 0]
    g_s = small.at[LOSS_ROW, 0].set(0.0)
    d_s, m_s, v_s = _adamw(_pack_vecs(vec), g_s, _pack_vecs({n: a["m_" + n] for n in VECS}),
                           _pack_vecs({n: a["v_" + n] for n in VECS}), name="adamw_gains")

    grads = _pack_grads(dW, layout, rows, shards)
    half_index = lax.axis_index("c").astype(jnp.int32).reshape(1)
    chip_sum = _sum_cores(grads, _swap_other_half(grads, name="swap_other_half"), half_index, name="sum_cores")
    g_w = _join_halves(_sum_chips(_scatter_chips(chip_sum, name="scatter_grads"), name="sum_chips"), name="join_halves")
    d_w, m_w, v_w = _adamw(
        _pack_shards(shards, layout, rows, F32), g_w,
        _pack_shards({n: a["m_" + n] for n, _ in MATS}, layout, rows, F32),
        _pack_shards({n: a["v_" + n] for n, _ in MATS}, layout, rows, F32), name="adamw_mats")

    out = [loss, grad_x[None]]
    for packed_w, packed_s in ((g_w, g_s), (d_w, d_s), (m_w, m_s), (v_w, v_s)):
        got = {**_unpack_shards(packed_w, layout, shards), **_unpack_vecs(packed_s, vec)}
        out += [got[n] for n in WEIGHT_ORDER]
    return tuple(out)


def kernel(x, mem, positions, l0_mix_norm, l0_w_in, l0_sinks, l0_q_norm, l0_w_uq, l0_kv_norm, l0_w_ukv, l0_w_out, l0_x_norm, l0_mem_norm, l0_w_xq, l0_w_xkv, l0_w_xo, l0_ffn_norm, l0_w_gate, l0_w_up, l0_w_down, l1_mix_norm, l1_w_qkv, l1_w_out, l1_x_norm, l1_mem_norm, l1_w_xq, l1_w_xkv, l1_w_xo, l1_ffn_norm, l1_w_gate, l1_w_up, l1_w_down, final_norm, loss_target, m_l0_mix_norm, m_l0_w_in, m_l0_sinks, m_l0_q_norm, m_l0_w_uq, m_l0_kv_norm, m_l0_w_ukv, m_l0_w_out, m_l0_x_norm, m_l0_mem_norm, m_l0_w_xq, m_l0_w_xkv, m_l0_w_xo, m_l0_ffn_norm, m_l0_w_gate, m_l0_w_up, m_l0_w_down, m_l1_mix_norm, m_l1_w_qkv, m_l1_w_out, m_l1_x_norm, m_l1_mem_norm, m_l1_w_xq, m_l1_w_xkv, m_l1_w_xo, m_l1_ffn_norm, m_l1_w_gate, m_l1_w_up, m_l1_w_down, m_final_norm, v_l0_mix_norm, v_l0_w_in, v_l0_sinks, v_l0_q_norm, v_l0_w_uq, v_l0_kv_norm, v_l0_w_ukv, v_l0_w_out, v_l0_x_norm, v_l0_mem_norm, v_l0_w_xq, v_l0_w_xkv, v_l0_w_xo, v_l0_ffn_norm, v_l0_w_gate, v_l0_w_up, v_l0_w_down, v_l1_mix_norm, v_l1_w_qkv, v_l1_w_out, v_l1_x_norm, v_l1_mem_norm, v_l1_w_xq, v_l1_w_xkv, v_l1_w_xo, v_l1_ffn_norm, v_l1_w_gate, v_l1_w_up, v_l1_w_down, v_final_norm):
    return _step(dict(locals()))
```

```python
import functools

import jax
import jax.numpy as jnp
import numpy as np
from jax import lax
from jax.experimental import pallas as pl
from jax.experimental.pallas import tpu as pltpu

F32 = jnp.float32
MXU_DTYPE = jnp.bfloat16
LANES = 128
VMEM_LIMIT_BYTES = 56 * 1024 * 1024

NORM_EPS = 1e-6
ROPE_THETA = 10000.0
BLOCK = 128
HEAD_DIM = 64
SWA_HEADS, SWA_KV_HEADS, SWA_WINDOW = 8, 2, 128
MLA_HEADS, MLA_Q_RANK, MLA_KV_RANK, MLA_NOPE, MLA_ROPE, MLA_V = 8, 384, 256, 64, 32, 64
DIL_HEADS = 16
DIL_PATTERNS = ((128, 1), (512, 4), (2048, 16))
X_HEADS, X_HEAD_DIM = 4, 128
ADAM_LR, ADAM_B1, ADAM_B2, ADAM_EPS, ADAM_WD, ADAM_STEP = 0.001, 0.9, 0.999, 1e-08, 0.01, 10
MESH = pl.DeviceIdType.MESH
NEG_BIG = -1e30

NN = (((1,), (0,)), ((), ()))
NT = (((1,), (1,)), ((), ()))


def _dot(a, b, dims=NN):
    return lax.dot_general(a.astype(MXU_DTYPE), b.astype(MXU_DTYPE), dims, preferred_element_type=F32)


def _pcall(body, *, name, dims=None, **kw):
    params = pltpu.CompilerParams(dimension_semantics=dims, vmem_limit_bytes=VMEM_LIMIT_BYTES)
    return pl.pallas_call(body, name=name, compiler_params=params, **kw)


def _tile(n, pref):
    t = (min(pref, n) // LANES) * LANES
    while t >= LANES:
        if n % t == 0:
            return t
        t -= LANES
    return n


def _lane(shape):
    return lax.broadcasted_iota(jnp.int32, shape, 1)


def _cols_to_lanes(cols, rows):
    lane = _lane((rows, LANES))
    out = jnp.zeros((rows, LANES), F32)
    for j, col in enumerate(cols):
        out = jnp.where(lane == j, col, out)
    return out


def _mm(a, b, *, mode, name, res=None, out_dtype=F32, tm=1024, tn=1536, tk=1408):
    if mode == "nn":
        (M, K), (K2, N) = a.shape, b.shape
    elif mode == "nt":
        (M, K), (N, K2) = a.shape, b.shape
    else:
        (K, M), (K2, N) = a.shape, b.shape
    assert K == K2, (a.shape, b.shape, mode)
    tm, tn, tk = _tile(M, tm), _tile(N, tn), _tile(K, tk)
    nk = K // tk
    in_place = out_dtype == F32 or nk == 1

    def body(*refs):
        refs = list(refs)
        a_ref, b_ref = refs[:2]
        r_ref = refs[2] if res is not None else None
        o_ref = refs[3 if res is not None else 2]
        acc = o_ref if in_place else refs[-1]
        k = pl.program_id(2)
        if mode == "nn":
            part = _dot(a_ref[...], b_ref[...], NN)
        elif mode == "nt":
            part = _dot(a_ref[...], b_ref[...], NT)
        else:
            part = _dot(a_ref[...].T, b_ref[...], NN)
        if nk == 1:
            o_ref[...] = (part if res is None else part + r_ref[...].astype(F32)).astype(o_ref.dtype)
            return

        @pl.when(k == 0)
        def _():
            acc[...] = part if res is None else part + r_ref[...].astype(F32)

        @pl.when(k > 0)
        def _():
            acc[...] += part

        if not in_place:
            @pl.when(k == nk - 1)
            def _():
                o_ref[...] = acc[...].astype(o_ref.dtype)

    if mode == "nn":
        a_spec = pl.BlockSpec((tm, tk), lambda i, j, k: (i, k))
        b_spec = pl.BlockSpec((tk, tn), lambda i, j, k: (k, j))
    elif mode == "nt":
        a_spec = pl.BlockSpec((tm, tk), lambda i, j, k: (i, k))
        b_spec = pl.BlockSpec((tn, tk), lambda i, j, k: (j, k))
    else:
        a_spec = pl.BlockSpec((tk, tm), lambda i, j, k: (k, i))
        b_spec = pl.BlockSpec((tk, tn), lambda i, j, k: (k, j))
    o_spec = pl.BlockSpec((tm, tn), lambda i, j, k: (i, j))
    in_specs = [a_spec, b_spec] + ([] if res is None else [o_spec])
    args = (a, b) + (() if res is None else (res,))
    return _pcall(
        body, name=name, dims=("parallel", "parallel", "arbitrary"),
        grid=(M // tm, N // tn, nk), in_specs=in_specs, out_specs=o_spec,
        out_shape=jax.ShapeDtypeStruct((M, N), out_dtype),
        scratch_shapes=[] if in_place else [pltpu.VMEM((tm, tn), F32)],
    )(*args)


def _rms_parts(xf):
    r = lax.rsqrt(jnp.mean(xf * xf, axis=-1, keepdims=True) + NORM_EPS)
    return xf * r, r


def _rms_bwd_rows(xf, g, dy):
    xhat, r = _rms_parts(xf)
    dxhat = dy * g
    dx = r * (dxhat - xhat * jnp.mean(dxhat * xhat, axis=-1, keepdims=True))
    return dx, dy * xhat


def _rmsnorm(x, g, *, name, out_dtype=MXU_DTYPE, tm=512):
    M, D = x.shape
    tm = _tile(M, tm)

    def body(x_ref, g_ref, o_ref):
        xhat, _ = _rms_parts(x_ref[...].astype(F32))
        o_ref[...] = (xhat * g_ref[...]).astype(o_ref.dtype)

    return _pcall(
        body, name=name, dims=("parallel",), grid=(M // tm,),
        in_specs=[pl.BlockSpec((tm, D), lambda i: (i, 0)), pl.BlockSpec((1, D), lambda i: (0, 0))],
        out_specs=pl.BlockSpec((tm, D), lambda i: (i, 0)),
        out_shape=jax.ShapeDtypeStruct((M, D), out_dtype),
    )(x, g.reshape(1, D))


def _rmsnorm_bwd(x, g, dy, *, name, dres=None, tm=512):
    M, D = x.shape
    tm = _tile(M, tm)

    def body(*refs):
        if dres is None:
            x_ref, g_ref, dy_ref, dx_ref, dg_ref = refs
        else:
            x_ref, g_ref, dy_ref, dr_ref, dx_ref, dg_ref = refs
        dx, dgp = _rms_bwd_rows(x_ref[...].astype(F32), g_ref[...], dy_ref[...].astype(F32))
        if dres is not None:
            dx = dx + dr_ref[...]
        dx_ref[...] = dx

        @pl.when(pl.program_id(0) == 0)
        def _():
            dg_ref[...] = jnp.zeros_like(dg_ref)

        dg_ref[...] += jnp.sum(dgp, axis=0, keepdims=True)

    row = pl.BlockSpec((tm, D), lambda i: (i, 0))
    vec = pl.BlockSpec((1, D), lambda i: (0, 0))
    in_specs = [row, vec, row] + ([] if dres is None else [row])
    args = (x, g.reshape(1, D), dy) + (() if dres is None else (dres,))
    return _pcall(
        body, name=name, dims=("arbitrary",), grid=(M // tm,), in_specs=in_specs, out_specs=[row, vec],
        out_shape=[jax.ShapeDtypeStruct((M, D), F32), jax.ShapeDtypeStruct((1, D), F32)],
    )(*args)


def _rope_chunk(t, c, s, half):
    lane = _lane(t.shape)
    swapped = jnp.where((lane % (2 * half)) < half, pltpu.roll(t, LANES - half, 1), pltpu.roll(t, half, 1))
    return t * c + swapped * s


def _rope_tables(positions):
    pos = positions.reshape(-1).astype(F32)[:, None]
    S = pos.shape[0]

    def cs(dh):
        inv_freq = ROPE_THETA ** (-jnp.arange(0, dh, 2, dtype=F32) / dh)
        ang = pos * inv_freq
        return jnp.cos(ang), jnp.sin(ang)

    c64, s64 = cs(HEAD_DIM)
    c32, s32 = cs(MLA_ROPE)
    z32, z64, z96 = (jnp.zeros((S, n), F32) for n in (32, 64, 96))
    return dict(
        c64=jnp.concatenate([c64, c64, c64, c64], 1), s64=jnp.concatenate([-s64, s64, -s64, s64], 1),
        ck=jnp.concatenate([c32, c32, z96], 1), sk=jnp.concatenate([-s32, s32, z96], 1),
        cm=jnp.concatenate([jnp.ones((S, 64), F32), c32, c32, z32], 1),
        sm=jnp.concatenate([z64, -s32, s32, z32], 1),
    )


def _attn_steps(mode, n_other, t_self, t_other):
    if mode == "band":
        assert t_self == t_other
        return 2
    return n_other


def _kv_block(mode, qi, kj):
    if mode == "band":
        return jnp.maximum(qi - 1 + kj, 0), (qi + kj) >= 1
    if mode == "causal":
        return jnp.minimum(kj, qi), kj <= qi
    return kj, None


def _q_block(mode, ki, qj, nq):
    if mode == "band":
        return jnp.minimum(ki + qj, nq - 1), (ki + qj) <= nq - 1
    if mode == "causal":
        return jnp.maximum(qj, ki), qj >= ki
    return qj, None


def _mask(mode, max_dist, qpos, kpos):
    d = qpos - kpos
    if mode == "band":
        return (d >= 0) & (d <= max_dist)
    if mode == "causal":
        return d >= 0
    return None


def _when(cond, fn):
    if cond is None:
        fn()
    else:
        pl.when(cond)(fn)


class _Attn:
    def __init__(self, *, T, Tk, G, nh, rep, dqk, dv, tq, tk, mode, scale, qcol, kcol, vcol, ocol, o_width,
                 max_dist=0):
        self.__dict__.update(locals())
        self.nkv = nh // rep
        assert T % tq == 0 and Tk % tk == 0 and nh <= LANES


def _attn_fwd(cfg, q, k, v, *, name, sink=None, out_dtype=F32):
    c = cfg
    nq, nk = c.T // c.tq, c.Tk // c.tk
    steps = _attn_steps(c.mode, nk, c.tq, c.tk)

    def body(*refs):
        if sink is None:
            q_ref, k_ref, v_ref, o_ref, lse_ref, m_scr, l_scr, acc = refs
        else:
            q_ref, k_ref, v_ref, sink_ref, o_ref, lse_ref, m_scr, l_scr, acc = refs
        qi, kj = pl.program_id(1), pl.program_id(2)
        kb, valid = _kv_block(c.mode, qi, kj)

        @pl.when(kj == 0)
        def _():
            if sink is None:
                m_scr[...] = jnp.full_like(m_scr, NEG_BIG)
                l_scr[...] = jnp.zeros_like(l_scr)
            else:
                m_scr[...] = jnp.broadcast_to(sink_ref[...], m_scr.shape)
                l_scr[...] = jnp.ones_like(l_scr)
            acc[...] = jnp.zeros_like(acc)

        def step():
            qpos = qi * c.tq + lax.broadcasted_iota(jnp.int32, (c.tq, c.tk), 0)
            kpos = kb * c.tk + lax.broadcasted_iota(jnp.int32, (c.tq, c.tk), 1)
            mask = _mask(c.mode, c.max_dist, qpos, kpos)
            for j in range(c.nh):
                g = j // c.rep
                s = _dot(q_ref[:, j * c.dqk:(j + 1) * c.dqk], k_ref[:, g * c.dqk:(g + 1) * c.dqk], NT) * c.scale
                if mask is not None:
                    s = jnp.where(mask, s, -jnp.inf)
                m_prev = m_scr[:, j:j + 1]
                m_new = jnp.maximum(m_prev, jnp.max(s, axis=1, keepdims=True))
                alpha = jnp.exp(m_prev - m_new)
                p = jnp.exp(s - m_new)
                l_scr[:, j:j + 1] = alpha * l_scr[:, j:j + 1] + jnp.sum(p, axis=1, keepdims=True)
                acc[:, j * c.dv:(j + 1) * c.dv] = (
                    alpha * acc[:, j * c.dv:(j + 1) * c.dv] + _dot(p, v_ref[:, g * c.dv:(g + 1) * c.dv], NN))
                m_scr[:, j:j + 1] = m_new

        _when(valid, step)

        @pl.when(kj == steps - 1)
        def _():
            for j in range(c.nh):
                o_ref[:, j * c.dv:(j + 1) * c.dv] = (
                    acc[:, j * c.dv:(j + 1) * c.dv] / l_scr[:, j:j + 1]).astype(o_ref.dtype)
            lane = _lane((c.tq, LANES))
            lse_ref[...] = jnp.where(lane < c.nh, m_scr[...] + jnp.log(jnp.maximum(l_scr[...], 1e-37)), 0.0)

    in_specs = [
        pl.BlockSpec((c.tq, c.nh * c.dqk), lambda g, i, j: (i, c.qcol(g))),
        pl.BlockSpec((c.tk, c.nkv * c.dqk), lambda g, i, j: (_kv_block(c.mode, i, j)[0], c.kcol(g))),
        pl.BlockSpec((c.tk, c.nkv * c.dv), lambda g, i, j: (_kv_block(c.mode, i, j)[0], c.vcol(g))),
    ]
    args = [q, k, v]
    if sink is not None:
        in_specs.append(pl.BlockSpec((1, LANES), lambda g, i, j: (0, 0)))
        args.append(sink)
    return _pcall(
        body, name=name, dims=("parallel", "parallel", "arbitrary"), grid=(c.G, nq, steps),
        in_specs=in_specs,
        out_specs=[pl.BlockSpec((c.tq, c.nh * c.dv), lambda g, i, j: (i, c.ocol(g))),
                   pl.BlockSpec((c.tq, LANES), lambda g, i, j: (i, g))],
        out_shape=[jax.ShapeDtypeStruct((c.T, c.o_width), out_dtype),
                   jax.ShapeDtypeStruct((c.T, LANES * c.G), F32)],
        scratch_shapes=[pltpu.VMEM((c.tq, LANES), F32), pltpu.VMEM((c.tq, LANES), F32),
                        pltpu.VMEM((c.tq, c.nh * c.dv), F32)],
    )(*args)


def _attn_delta(cfg, o, do, *, name, w=None, lse=None, sink=None, tm=512):
    c = cfg
    tm = _tile(c.T, tm)
    width = c.nh * c.dv

    def body(*refs):
        refs = list(refs)
        o_ref, do_ref = refs[:2]
        rest = refs[2:]
        w_ref = rest.pop(0) if w is not None else None
        lse_ref, sink_ref = (rest.pop(0), rest.pop(0)) if sink is not None else (None, None)
        d_ref = rest.pop(0)
        prod = o_ref[...].astype(F32) * do_ref[...].astype(F32)
        cols = [jnp.sum(prod[:, j * c.dv:(j + 1) * c.dv], axis=1, keepdims=True) for j in range(c.nh)]
        delta = _cols_to_lanes(cols, tm)
        if w is not None:
            delta = delta * w_ref[...]
        d_ref[...] = delta
        if sink is not None:
            ds_ref = rest.pop(0)

            @pl.when(pl.program_id(1) == 0)
            def _():
                ds_ref[...] = jnp.zeros_like(ds_ref)

            lane = _lane((tm, LANES))
            ps = jnp.where(lane < c.nh, jnp.exp(sink_ref[...] - lse_ref[...]), 0.0)
            ds_ref[...] -= jnp.sum(ps * delta, axis=0, keepdims=True)

    stat = pl.BlockSpec((tm, LANES), lambda g, i: (i, g))
    in_specs = [pl.BlockSpec((tm, width), lambda g, i: (i, c.ocol(g)))] * 2
    args = [o, do]
    out_specs, out_shape = [stat], [jax.ShapeDtypeStruct((c.T, LANES * c.G), F32)]
    if w is not None:
        in_specs.append(stat)
        args.append(w)
    if sink is not None:
        assert c.G == 1
        in_specs += [stat, pl.BlockSpec((1, LANES), lambda g, i: (0, 0))]
        args += [lse, sink]
        out_specs.append(pl.BlockSpec((1, LANES), lambda g, i: (0, 0)))
        out_shape.append(jax.ShapeDtypeStruct((1, LANES), F32))
    out = _pcall(
        body, name=name, dims=("arbitrary", "arbitrary"), grid=(c.G, c.T // tm),
        in_specs=in_specs, out_specs=out_specs, out_shape=out_shape,
    )(*args)
    return out if sink is not None else (out[0], None)


def _attn_dq(cfg, q, k, v, do, lse, delta, *, name, init=None, out_dtype=F32):
    c = cfg
    nq, nk = c.T // c.tq, c.Tk // c.tk
    steps = _attn_steps(c.mode, nk, c.tq, c.tk)
    qw = c.nh * c.dqk

    def body(*refs):
        if init is None:
            q_ref, k_ref, v_ref, do_ref, lse_ref, d_ref, dq_ref, acc = refs
        else:
            q_ref, k_ref, v_ref, do_ref, lse_ref, d_ref, init_ref, dq_ref, acc = refs
        qi, kj = pl.program_id(1), pl.program_id(2)
        kb, valid = _kv_block(c.mode, qi, kj)

        @pl.when(kj == 0)
        def _():
            acc[...] = jnp.zeros_like(acc) if init is None else init_ref[...].astype(F32)

        def step():
            qpos = qi * c.tq + lax.broadcasted_iota(jnp.int32, (c.tq, c.tk), 0)
            kpos = kb * c.tk + lax.broadcasted_iota(jnp.int32, (c.tq, c.tk), 1)
            mask = _mask(c.mode, c.max_dist, qpos, kpos)
            for j in range(c.nh):
                g = j // c.rep
                kh = k_ref[:, g * c.dqk:(g + 1) * c.dqk]
                s = _dot(q_ref[:, j * c.dqk:(j + 1) * c.dqk], kh, NT) * c.scale
                if mask is not None:
                    s = jnp.where(mask, s, -jnp.inf)
                p = jnp.exp(s - lse_ref[:, j:j + 1])
                dp = _dot(do_ref[:, j * c.dv:(j + 1) * c.dv], v_ref[:, g * c.dv:(g + 1) * c.dv], NT)
                ds = p * (dp - d_ref[:, j:j + 1]) * c.scale
                acc[:, j * c.dqk:(j + 1) * c.dqk] += _dot(ds, kh, NN)

        _when(valid, step)

        @pl.when(kj == steps - 1)
        def _():
            dq_ref[...] = acc[...].astype(dq_ref.dtype)

    kvb = lambda i, j: _kv_block(c.mode, i, j)[0]
    qspec = pl.BlockSpec((c.tq, qw), lambda g, i, j: (i, c.qcol(g)))
    stat = pl.BlockSpec((c.tq, LANES), lambda g, i, j: (i, g))
    in_specs = [
        qspec,
        pl.BlockSpec((c.tk, c.nkv * c.dqk), lambda g, i, j: (kvb(i, j), c.kcol(g))),
        pl.BlockSpec((c.tk, c.nkv * c.dv), lambda g, i, j: (kvb(i, j), c.vcol(g))),
        pl.BlockSpec((c.tq, c.nh * c.dv), lambda g, i, j: (i, c.ocol(g))),
        stat, stat,
    ]
    args = [q, k, v, do, lse, delta]
    dq_spec = pl.BlockSpec((c.tq, qw), lambda g, i, j: (i, g))
    if init is not None:
        in_specs.append(dq_spec)
        args.append(init)
    return _pcall(
        body, name=name, dims=("parallel", "parallel", "arbitrary"), grid=(c.G, nq, steps),
        in_specs=in_specs, out_specs=dq_spec,
        out_shape=jax.ShapeDtypeStruct((c.T, c.G * qw), out_dtype),
        scratch_shapes=[pltpu.VMEM((c.tq, qw), F32)],
    )(*args)


def _attn_dkv(cfg, q, k, v, do, lse, delta, *, name, init=None, out_dtype=F32):
    c = cfg
    nq, nk = c.T // c.tq, c.Tk // c.tk
    steps = _attn_steps(c.mode, nq, c.tk, c.tq)
    kw, vw = c.nkv * c.dqk, c.nkv * c.dv

    def body(*refs):
        if init is None:
            q_ref, k_ref, v_ref, do_ref, lse_ref, d_ref, dk_ref, dv_ref, dk_acc, dv_acc = refs
        else:
            q_ref, k_ref, v_ref, do_ref, lse_ref, d_ref, ik_ref, iv_ref, dk_ref, dv_ref, dk_acc, dv_acc = refs
        ki, qj = pl.program_id(1), pl.program_id(2)
        qb, valid = _q_block(c.mode, ki, qj, nq)

        @pl.when(qj == 0)
        def _():
            dk_acc[...] = jnp.zeros_like(dk_acc) if init is None else ik_ref[...].astype(F32)
            dv_acc[...] = jnp.zeros_like(dv_acc) if init is None else iv_ref[...].astype(F32)

        def step():
            kpos = ki * c.tk + lax.broadcasted_iota(jnp.int32, (c.tk, c.tq), 0)
            qpos = qb * c.tq + lax.broadcasted_iota(jnp.int32, (c.tk, c.tq), 1)
            mask = _mask(c.mode, c.max_dist, qpos, kpos)
            lse_t = lse_ref[...].T
            d_t = d_ref[...].T
            for j in range(c.nh):
                g = j // c.rep
                qh = q_ref[:, j * c.dqk:(j + 1) * c.dqk]
                doh = do_ref[:, j * c.dv:(j + 1) * c.dv]
                s_t = _dot(k_ref[:, g * c.dqk:(g + 1) * c.dqk], qh, NT) * c.scale
                if mask is not None:
                    s_t = jnp.where(mask, s_t, -jnp.inf)
                p_t = jnp.exp(s_t - lse_t[j:j + 1, :])
                dv_acc[:, g * c.dv:(g + 1) * c.dv] += _dot(p_t, doh, NN)
                dp_t = _dot(v_ref[:, g * c.dv:(g + 1) * c.dv], doh, NT)
                ds_t = p_t * (dp_t - d_t[j:j + 1, :]) * c.scale
                dk_acc[:, g * c.dqk:(g + 1) * c.dqk] += _dot(ds_t, qh, NN)

        _when(valid, step)

        @pl.when(qj == steps - 1)
        def _():
            dk_ref[...] = dk_acc[...].astype(dk_ref.dtype)
            dv_ref[...] = dv_acc[...].astype(dv_ref.dtype)

    qbi = lambda i, j: _q_block(c.mode, i, j, nq)[0]
    stat = pl.BlockSpec((c.tq, LANES), lambda g, i, j: (qbi(i, j), g))
    in_specs = [
        pl.BlockSpec((c.tq, c.nh * c.dqk), lambda g, i, j: (qbi(i, j), c.qcol(g))),
        pl.BlockSpec((c.tk, kw), lambda g, i, j: (i, c.kcol(g))),
        pl.BlockSpec((c.tk, vw), lambda g, i, j: (i, c.vcol(g))),
        pl.BlockSpec((c.tq, c.nh * c.dv), lambda g, i, j: (qbi(i, j), c.ocol(g))),
        stat, stat,
    ]
    args = [q, k, v, do, lse, delta]
    dk_spec = pl.BlockSpec((c.tk, kw), lambda g, i, j: (i, g))
    dv_spec = pl.BlockSpec((c.tk, vw), lambda g, i, j: (i, g))
    if init is not None:
        in_specs += [dk_spec, dv_spec]
        args += list(init)
    return _pcall(
        body, name=name, dims=("parallel", "parallel", "arbitrary"), grid=(c.G, nk, steps),
        in_specs=in_specs, out_specs=[dk_spec, dv_spec],
        out_shape=[jax.ShapeDtypeStruct((c.Tk, c.G * kw), out_dtype),
                   jax.ShapeDtypeStruct((c.Tk, c.G * vw), out_dtype)],
        scratch_shapes=[pltpu.VMEM((c.tk, kw), F32), pltpu.VMEM((c.tk, vw), F32)],
    )(*args)


TN = (((0,), (0,)), ((), ()))


def _band_mask(c, i):
    row = lax.broadcasted_iota(jnp.int32, (BLOCK, 2 * BLOCK), 0)
    col = lax.broadcasted_iota(jnp.int32, (BLOCK, 2 * BLOCK), 1)
    d = BLOCK + row - col
    return (d >= 0) & (d <= c.max_dist) & ((col >= BLOCK) | (i > 0))


def _band_fwd(cfg, q, k, v, *, name, sink=None, out_dtype=F32):
    c = cfg
    assert c.mode == "band" and c.tq == c.tk == BLOCK and c.T == c.Tk
    nq = c.T // BLOCK

    def body(*refs):
        if sink is None:
            q_ref, kp_ref, kc_ref, vp_ref, vc_ref, o_ref, lse_ref = refs
        else:
            q_ref, kp_ref, kc_ref, vp_ref, vc_ref, sink_ref, o_ref, lse_ref = refs
        mask = _band_mask(c, pl.program_id(1))
        k2 = jnp.concatenate([kp_ref[...], kc_ref[...]], axis=0)
        v2 = jnp.concatenate([vp_ref[...], vc_ref[...]], axis=0)
        lses = []
        for j in range(c.nh):
            g = j // c.rep
            s = _dot(q_ref[:, j * c.dqk:(j + 1) * c.dqk], k2[:, g * c.dqk:(g + 1) * c.dqk], NT) * c.scale
            s = jnp.where(mask, s, -jnp.inf)
            m = jnp.max(s, axis=1, keepdims=True)
            if sink is not None:
                sk = sink_ref[:, j:j + 1]
                m = jnp.maximum(m, sk)
            p = jnp.exp(s - m)
            l = jnp.sum(p, axis=1, keepdims=True)
            if sink is not None:
                l = l + jnp.exp(sk - m)
            o_ref[:, j * c.dv:(j + 1) * c.dv] = (_dot(p, v2[:, g * c.dv:(g + 1) * c.dv], NN) / l).astype(o_ref.dtype)
            lses.append(m + jnp.log(l))
        lse_ref[...] = _cols_to_lanes(lses, BLOCK)

    prev = lambda i: jnp.maximum(i - 1, 0)
    kw, vw = c.nkv * c.dqk, c.nkv * c.dv
    in_specs = [
        pl.BlockSpec((BLOCK, c.nh * c.dqk), lambda g, i: (i, c.qcol(g))),
        pl.BlockSpec((BLOCK, kw), lambda g, i: (prev(i), c.kcol(g))),
        pl.BlockSpec((BLOCK, kw), lambda g, i: (i, c.kcol(g))),
        pl.BlockSpec((BLOCK, vw), lambda g, i: (prev(i), c.vcol(g))),
        pl.BlockSpec((BLOCK, vw), lambda g, i: (i, c.vcol(g))),
    ]
    args = [q, k, k, v, v]
    if sink is not None:
        in_specs.append(pl.BlockSpec((1, LANES), lambda g, i: (0, 0)))
        args.append(sink)
    return _pcall(
        body, name=name, dims=("parallel", "parallel"), grid=(c.G, nq), in_specs=in_specs,
        out_specs=[pl.BlockSpec((BLOCK, c.nh * c.dv), lambda g, i: (i, c.ocol(g))),
                   pl.BlockSpec((BLOCK, LANES), lambda g, i: (i, g))],
        out_shape=[jax.ShapeDtypeStruct((c.T, c.o_width), out_dtype),
                   jax.ShapeDtypeStruct((c.T, LANES * c.G), F32)],
    )(*args)


def _band_bwd(cfg, q, k, v, do, lse, delta, *, name, init=None):
    c = cfg
    assert c.mode == "band" and c.tq == c.tk == BLOCK and c.T == c.Tk
    nq = c.T // BLOCK
    qw, kw, vw = c.nh * c.dqk, c.nkv * c.dqk, c.nkv * c.dv

    def body(*refs):
        refs = list(refs)
        q_ref, kp_ref, kc_ref, vp_ref, vc_ref, do_ref, lse_ref, d_ref = refs[:8]
        iq_ref, ik_ref, iv_ref = refs[8:11] if init is not None else (None, None, None)
        dq_ref, dk_ref, dv_ref, dk_c, dv_c = refs[-5:]
        n = pl.program_id(1)

        def plus(val, ref, sl):
            return val if ref is None else val + ref[:, sl]

        @pl.when(n == 0)
        def _():
            dk_c[...] = jnp.zeros_like(dk_c)
            dv_c[...] = jnp.zeros_like(dv_c)

        @pl.when(n < nq)
        def _():
            mask = _band_mask(c, n)
            k2 = jnp.concatenate([kp_ref[...], kc_ref[...]], axis=0)
            v2 = jnp.concatenate([vp_ref[...], vc_ref[...]], axis=0)
            dk2, dv2 = [None] * c.nkv, [None] * c.nkv
            for j in range(c.nh):
                g = j // c.rep
                qs, os_ = slice(j * c.dqk, (j + 1) * c.dqk), slice(j * c.dv, (j + 1) * c.dv)
                qh, doh = q_ref[:, qs], do_ref[:, os_]
                kh, vh = k2[:, g * c.dqk:(g + 1) * c.dqk], v2[:, g * c.dv:(g + 1) * c.dv]
                s = jnp.where(mask, _dot(qh, kh, NT) * c.scale, -jnp.inf)
                p = jnp.exp(s - lse_ref[:, j:j + 1])
                ds = p * (_dot(doh, vh, NT) - d_ref[:, j:j + 1]) * c.scale
                dq_ref[:, qs] = plus(_dot(ds, kh, NN), iq_ref, qs)
                dvh, dkh = _dot(p, doh, TN), _dot(ds, qh, TN)
                dv2[g] = dvh if dv2[g] is None else dv2[g] + dvh
                dk2[g] = dkh if dk2[g] is None else dk2[g] + dkh
            for g in range(c.nkv):
                ks, vs = slice(g * c.dqk, (g + 1) * c.dqk), slice(g * c.dv, (g + 1) * c.dv)
                dk_ref[:, ks] = plus(dk_c[:, ks] + dk2[g][:BLOCK], ik_ref, ks)
                dv_ref[:, vs] = plus(dv_c[:, vs] + dv2[g][:BLOCK], iv_ref, vs)
                dk_c[:, ks] = dk2[g][BLOCK:]
                dv_c[:, vs] = dv2[g][BLOCK:]

        @pl.when(n == nq)
        def _():
            dk_ref[...] = plus(dk_c[...], ik_ref, slice(None))
            dv_ref[...] = plus(dv_c[...], iv_ref, slice(None))

    cur = lambda n: jnp.minimum(n, nq - 1)
    prev = lambda n: jnp.maximum(cur(n) - 1, 0)
    out_blk = lambda n: jnp.maximum(n - 1, 0)
    stat = pl.BlockSpec((BLOCK, LANES), lambda g, n: (cur(n), g))
    dq_spec = pl.BlockSpec((BLOCK, qw), lambda g, n: (cur(n), g))
    dk_spec = pl.BlockSpec((BLOCK, kw), lambda g, n: (out_blk(n), g))
    dv_spec = pl.BlockSpec((BLOCK, vw), lambda g, n: (out_blk(n), g))
    in_specs = [
        pl.BlockSpec((BLOCK, qw), lambda g, n: (cur(n), c.qcol(g))),
        pl.BlockSpec((BLOCK, kw), lambda g, n: (prev(n), c.kcol(g))),
        pl.BlockSpec((BLOCK, kw), lambda g, n: (cur(n), c.kcol(g))),
        pl.BlockSpec((BLOCK, vw), lambda g, n: (prev(n), c.vcol(g))),
        pl.BlockSpec((BLOCK, vw), lambda g, n: (cur(n), c.vcol(g))),
        pl.BlockSpec((BLOCK, c.nh * c.dv), lambda g, n: (cur(n), c.ocol(g))),
        stat, stat,
    ]
    args = [q, k, k, v, v, do, lse, delta]
    if init is not None:
        in_specs += [dq_spec, dk_spec, dv_spec]
        args += list(init)
    return _pcall(
        body, name=name, dims=("parallel", "arbitrary"), grid=(c.G, nq + 1), in_specs=in_specs,
        out_specs=[dq_spec, dk_spec, dv_spec],
        out_shape=[_sds((c.T, c.G * qw)), _sds((c.T, c.G * kw)), _sds((c.T, c.G * vw))],
        scratch_shapes=[pltpu.VMEM((BLOCK, kw), F32), pltpu.VMEM((BLOCK, vw), F32)],
    )(*args)


def _causal_pairs(n, kv_major):
    pairs =[(i, j) for j in range(n) for i in range(j, n)] if kv_major else [(i, j) for i in range(n) for j in range(i + 1)]
    return jnp.asarray(np.array([p[0] for p in pairs], np.int32)), jnp.asarray(np.array([p[1] for p in pairs], np.int32))


def _causal_mask(t):
    return lax.broadcasted_iota(jnp.int32, (t, t), 0) >= lax.broadcasted_iota(jnp.int32, (t, t), 1)


def _causal_fwd(cfg, q, k, v, *, name, out_dtype=F32):
    c = cfg
    assert c.mode == "causal" and c.tq == c.tk and c.T == c.Tk
    t, n = c.tq, c.T // c.tq
    qi_tab, kj_tab = _causal_pairs(n, kv_major=False)

    def body(qi_ref, kj_ref, q_ref, k_ref, v_ref, o_ref, lse_ref, m_scr, l_scr, acc):
        pair = pl.program_id(1)
        qi, kj = qi_ref[pair], kj_ref[pair]

        @pl.when(kj == 0)
        def _():
            m_scr[...] = jnp.full_like(m_scr, NEG_BIG)
            l_scr[...] = jnp.zeros_like(l_scr)
            acc[...] = jnp.zeros_like(acc)

        def step(diagonal):
            mask = _causal_mask(t) if diagonal else None
            for j in range(c.nh):
                g = j // c.rep
                s = _dot(q_ref[:, j * c.dqk:(j + 1) * c.dqk], k_ref[:, g * c.dqk:(g + 1) * c.dqk], NT) * c.scale
                if diagonal:
                    s = jnp.where(mask, s, -jnp.inf)
                m_prev = m_scr[j]
                m_new = jnp.maximum(m_prev, jnp.max(s, axis=1, keepdims=True))
                alpha = jnp.exp(m_prev - m_new)
                p = jnp.exp(s - m_new)
                l_scr[j] = alpha * l_scr[j] + jnp.sum(p, axis=1, keepdims=True)
                acc[j] = alpha * acc[j] + _dot(p, v_ref[:, g * c.dv:(g + 1) * c.dv], NN)
                m_scr[j] = m_new

        pl.when(kj == qi)(lambda: step(True))
        pl.when(kj != qi)(lambda: step(False))

        @pl.when(kj == qi)
        def _():
            lses = []
            for j in range(c.nh):
                o_ref[:, j * c.dv:(j + 1) * c.dv] = (acc[j] / l_scr[j]).astype(o_ref.dtype)
                lses.append(m_scr[j] + jnp.log(l_scr[j]))
            lse_ref[...] = _cols_to_lanes(lses, t)

    grid_spec = pltpu.PrefetchScalarGridSpec(
        num_scalar_prefetch=2, grid=(c.G, int(qi_tab.shape[0])),
        in_specs=[pl.BlockSpec((t, c.nh * c.dqk), lambda g, p, qi, kj: (qi[p], c.qcol(g))),
                  pl.BlockSpec((t, c.nkv * c.dqk), lambda g, p, qi, kj: (kj[p], c.kcol(g))),
                  pl.BlockSpec((t, c.nkv * c.dv), lambda g, p, qi, kj: (kj[p], c.vcol(g)))],
        out_specs=[pl.BlockSpec((t, c.nh * c.dv), lambda g, p, qi, kj: (qi[p], c.ocol(g))),
                   pl.BlockSpec((t, LANES), lambda g, p, qi, kj: (qi[p], g))],
        scratch_shapes=[pltpu.VMEM((c.nh, t, 1), F32), pltpu.VMEM((c.nh, t, 1), F32), pltpu.VMEM((c.nh, t, c.dv), F32)])
    return _pcall(
        body, name=name, dims=("parallel", "arbitrary"), grid_spec=grid_spec,
        out_shape=[jax.ShapeDtypeStruct((c.T, c.o_width), out_dtype), jax.ShapeDtypeStruct((c.T, LANES * c.G), F32)],
    )(qi_tab, kj_tab, q, k, v)


def _causal_bwd(cfg, q, k, v, do, lse, delta, *, name):
    c = cfg
    assert c.mode == "causal" and c.tq == c.tk and c.T == c.Tk
    t, n = c.tq, c.T // c.tq
    qw, kw, vw = c.nh * c.dqk, c.nkv * c.dqk, c.nkv * c.dv
    qi_tab, kj_tab = _causal_pairs(n, kv_major=True)

    def body(qi_ref, kj_ref, q_ref, k_ref, v_ref, do_ref, lse_ref, d_ref, dq_ref, dk_ref, dv_ref, dk_acc, dv_acc):
        pair = pl.program_id(1)
        qi, kj = qi_ref[pair], kj_ref[pair]

        @pl.when(pair == 0)
        def _():
            dq_ref[...] = jnp.zeros_like(dq_ref)

        @pl.when(qi == kj)
        def _():
            dk_acc[...] = jnp.zeros_like(dk_acc)
            dv_acc[...] = jnp.zeros_like(dv_acc)

        rows = pl.ds(pl.multiple_of(qi * t, t), t)

        def step(diagonal):
            mask = _causal_mask(t) if diagonal else None
            for j in range(c.nh):
                g = j // c.rep
                qs, ks, vs = (slice(j * c.dqk, (j + 1) * c.dqk), slice(g * c.dqk, (g + 1) * c.dqk),
                              slice(g * c.dv, (g + 1) * c.dv))
                qh, doh, kh = q_ref[:, qs], do_ref[:, j * c.dv:(j + 1) * c.dv], k_ref[:, ks]
                s = _dot(qh, kh, NT) * c.scale
                if diagonal:
                    s = jnp.where(mask, s, -jnp.inf)
                p = jnp.exp(s - lse_ref[:, j:j + 1])
                ds = p * (_dot(doh, v_ref[:, vs], NT) - d_ref[:, j:j + 1]) * c.scale
                dq_ref[rows, qs] += _dot(ds, kh, NN)
                dv_acc[:, vs] += _dot(p, doh, TN)
                dk_acc[:, ks] += _dot(ds, qh, TN)

        pl.when(qi == kj)(lambda: step(True))
        pl.when(qi != kj)(lambda: step(False))

        @pl.when(qi == n - 1)
        def _():
            dk_ref[...] = dk_acc[...]
            dv_ref[...] = dv_acc[...]

    stat = pl.BlockSpec((t, LANES), lambda g, p, qi, kj: (qi[p], g))
    grid_spec = pltpu.PrefetchScalarGridSpec(
        num_scalar_prefetch=2, grid=(c.G, int(qi_tab.shape[0])),
        in_specs=[pl.BlockSpec((t, qw), lambda g, p, qi, kj: (qi[p], c.qcol(g))),
                  pl.BlockSpec((t, kw), lambda g, p, qi, kj: (kj[p], c.kcol(g))),
                  pl.BlockSpec((t, vw), lambda g, p, qi, kj: (kj[p], c.vcol(g))),
                  pl.BlockSpec((t, c.nh * c.dv), lambda g, p, qi, kj: (qi[p], c.ocol(g))),
                  stat, stat],
        out_specs=[pl.BlockSpec((c.T, qw), lambda g, p, qi, kj: (0, g)),
                   pl.BlockSpec((t, kw), lambda g, p, qi, kj: (kj[p], g)),
                   pl.BlockSpec((t, vw), lambda g, p, qi, kj: (kj[p], g))],
        scratch_shapes=[pltpu.VMEM((t, kw), F32), pltpu.VMEM((t, vw), F32)])
    return _pcall(
        body, name=name, dims=("parallel", "arbitrary"), grid_spec=grid_spec,
        out_shape=[_sds((c.T, c.G * qw)), _sds((c.T, c.G * kw)), _sds((c.T, c.G * vw))],
    )(qi_tab, kj_tab, q, k, v, do, lse, delta)


def _rowwise(body, ins, outs, *, name, rows, tm=512, accs=()):
    tm = _tile(rows, tm)

    def spec(a):
        if a.shape[0] == rows:
            return pl.BlockSpec((tm, a.shape[1]), lambda i: (i, 0))
        assert a.shape[0] == 1
        return pl.BlockSpec((1, a.shape[1]), lambda i: (0, 0))

    return _pcall(
        functools.partial(body, tm), name=name, dims=("arbitrary" if accs else "parallel",), grid=(rows // tm,),
        in_specs=[spec(a) for a in ins], out_specs=[spec(a) for a in outs], out_shape=list(outs),
    )(*ins)


def _sds(shape, dtype=F32):
    return jax.ShapeDtypeStruct(shape, dtype)


def _acc_rows(ref, val):
    @pl.when(pl.program_id(0) == 0)
    def _():
        ref[...] = jnp.zeros_like(ref)

    ref[...] += jnp.sum(val, axis=0, keepdims=True)


Z_QA, Z_KA, Z_VA, Z_CQ, Z_CKV, Z_KR, Z_END = 0, 512, 640, 768, 1152, 1408, 1536


def _l0_prep(z, tabs, q_norm, kv_norm, *, name):
    S = z.shape[0]

    def body(tm, z_ref, c64, s64, ck, sk, gq, gkv, qa_o, ka_o, va_o, cq_o, ckv_o, kr_o):
        for i in range(4):
            sl = slice(Z_QA + i * LANES, Z_QA + (i + 1) * LANES)
            qa_o[:, i * LANES:(i + 1) * LANES] = _rope_chunk(z_ref[:, sl], c64[...], s64[...], 32).astype(qa_o.dtype)
        ka_o[...] = _rope_chunk(z_ref[:, Z_KA:Z_VA], c64[...], s64[...], 32).astype(ka_o.dtype)
        va_o[...] = z_ref[:, Z_VA:Z_CQ].astype(va_o.dtype)
        cq_o[...] = (_rms_parts(z_ref[:, Z_CQ:Z_CKV])[0] * gq[...]).astype(cq_o.dtype)
        ckv_o[...] = (_rms_parts(z_ref[:, Z_CKV:Z_KR])[0] * gkv[...]).astype(ckv_o.dtype)
        kr_o[...] = _rope_chunk(z_ref[:, Z_KR:Z_END], ck[...], sk[...], 16)

    outs = [_sds((S, 512), MXU_DTYPE), _sds((S, 128), MXU_DTYPE), _sds((S, 128), MXU_DTYPE),
            _sds((S, MLA_Q_RANK), MXU_DTYPE), _sds((S, MLA_KV_RANK), MXU_DTYPE), _sds((S, LANES))]
    ins = [z, tabs["c64"], tabs["s64"], tabs["ck"], tabs["sk"], q_norm.reshape(1, -1), kv_norm.reshape(1, -1)]
    return _rowwise(body, ins, outs, name=name, rows=S)


def _l0_prep_bwd(z, tabs, q_norm, kv_norm, dqa, dka, dva, dcq, dckv, dkr, *, name):
    S = z.shape[0]

    def body(tm, z_ref, c64, s64, ck, sk, gq, gkv, dqa_r, dka_r, dva_r, dcq_r, dckv_r, dkr_r, dz_o, dgq_o, dgkv_o):
        for i in range(4):
            sl = slice(i * LANES, (i + 1) * LANES)
            dz_o[:, sl] = _rope_chunk(dqa_r[:, sl].astype(F32), c64[...], -s64[...], 32).astype(dz_o.dtype)
        dz_o[:, Z_KA:Z_VA] = _rope_chunk(dka_r[...].astype(F32), c64[...], -s64[...], 32).astype(dz_o.dtype)
        dz_o[:, Z_VA:Z_CQ] = dva_r[...].astype(dz_o.dtype)
        dx, dgp = _rms_bwd_rows(z_ref[:, Z_CQ:Z_CKV], gq[...], dcq_r[...].astype(F32))
        dz_o[:, Z_CQ:Z_CKV] = dx.astype(dz_o.dtype)
        _acc_rows(dgq_o, dgp)
        dx, dgp = _rms_bwd_rows(z_ref[:, Z_CKV:Z_KR], gkv[...], dckv_r[...].astype(F32))
        dz_o[:, Z_CKV:Z_KR] = dx.astype(dz_o.dtype)
        _acc_rows(dgkv_o, dgp)
        dz_o[:, Z_KR:Z_END] = _rope_chunk(dkr_r[...], ck[...], -sk[...], 16).astype(dz_o.dtype)

    outs = [_sds((S, Z_END), MXU_DTYPE), _sds((1, MLA_Q_RANK)), _sds((1, MLA_KV_RANK))]
    ins = [z, tabs["c64"], tabs["s64"], tabs["ck"], tabs["sk"], q_norm.reshape(1, -1), kv_norm.reshape(1, -1),
           dqa, dka, dva, dcq, dckv, dkr]
    return _rowwise(body, ins, outs, name=name, rows=S, accs=(1, 2))


def _mla_prep(qb, kvb, kr, tabs, *, name):
    S = qb.shape[0]

    def body(tm, qb_r, kvb_r, kr_r, cm, sm, q_o, k_o, v_o):
        lane = _lane((tm, LANES))
        kr_at_64 = pltpu.roll(kr_r[...], 64, 1)
        for h in range(MLA_HEADS):
            sl = slice(h * LANES, (h + 1) * LANES)
            q_o[:, sl] = _rope_chunk(qb_r[:, sl], cm[...], sm[...], 16).astype(q_o.dtype)
            k_o[:, sl] = jnp.where(lane < 64, kvb_r[:, sl], kr_at_64).astype(k_o.dtype)
        for p in range(MLA_HEADS // 2):
            even = pltpu.roll(kvb_r[:, (2 * p) * LANES:(2 * p + 1) * LANES], 64, 1)
            odd = kvb_r[:, (2 * p + 1) * LANES:(2 * p + 2) * LANES]
            v_o[:, p * LANES:(p + 1) * LANES] = jnp.where(lane < 64, even, odd).astype(v_o.dtype)

    outs = [_sds((S, 1024), MXU_DTYPE), _sds((S, 1024), MXU_DTYPE), _sds((S, 512), MXU_DTYPE)]
    return _rowwise(body, [qb, kvb, kr, tabs["cm"], tabs["sm"]], outs, name=name, rows=S)


def _mla_prep_bwd(dq, dk, dv, tabs, *, name):
    S = dq.shape[0]

    def body(tm, dq_r, dk_r, dv_r, cm, sm, dqb_o, dkvb_o, dkr_o):
        lane = _lane((tm, LANES))
        dkr = jnp.zeros((tm, LANES), F32)
        for h in range(MLA_HEADS):
            sl = slice(h * LANES, (h + 1) * LANES)
            dqb_o[:, sl] = _rope_chunk(dq_r[:, sl].astype(F32), cm[...], -sm[...], 16).astype(dqb_o.dtype)
            dkh = dk_r[:, sl].astype(F32)
            dvp = dv_r[:, (h // 2) * LANES:(h // 2 + 1) * LANES].astype(F32)
            dvh = pltpu.roll(dvp, 64, 1) if h % 2 == 0 else dvp
            dkvb_o[:, sl] = jnp.where(lane < 64, dkh, dvh).astype(dkvb_o.dtype)
            dkr = dkr + pltpu.roll(dkh, 64, 1)
        dkr_o[...] = jnp.where(lane < MLA_ROPE, dkr, 0.0)

    outs = [_sds((S, 1024), MXU_DTYPE), _sds((S, 1024), MXU_DTYPE), _sds((S, LANES))]
    return _rowwise(body, [dq, dk, dv, tabs["cm"], tabs["sm"]], outs, name=name, rows=S)


def _l1_prep(qkv, tabs, *, name):
    S = qkv.shape[0]

    def body(tm, x_r, c64, s64, q_o, k_o, v_o):
        for i in range(8):
            sl = slice(i * LANES, (i + 1) * LANES)
            q_o[:, sl] = _rope_chunk(x_r[:, sl], c64[...], s64[...], 32).astype(q_o.dtype)
            k_o[:, sl] = _rope_chunk(x_r[:, 1024 + i * LANES:1024 + (i + 1) * LANES], c64[...], s64[...], 32).astype(k_o.dtype)
        v_o[...] = x_r[:, 2048:3072].astype(v_o.dtype)

    outs = [_sds((S, 1024), MXU_DTYPE)] * 3
    return _rowwise(body, [qkv, tabs["c64"], tabs["s64"]], outs, name=name, rows=S)


def _l1_prep_bwd(dq, dk, dv, tabs, *, name):
    S = dq.shape[0]

    def body(tm, dq_r, dk_r, dv_r, c64, s64, o):
        for i in range(8):
            sl = slice(i * LANES, (i + 1) * LANES)
            o[:, sl] = _rope_chunk(dq_r[:, sl], c64[...], -s64[...], 32).astype(o.dtype)
            o[:, 1024 + i * LANES:1024 + (i + 1) * LANES] = _rope_chunk(dk_r[:, sl], c64[...], -s64[...], 32).astype(o.dtype)
        o[:, 2048:3072] = dv_r[...].astype(o.dtype)

    return _rowwise(body, [dq, dk, dv, tabs["c64"], tabs["s64"]], [_sds((S, 3072), MXU_DTYPE)], name=name, rows=S)[0]


def _sigmoid(x):
    return 1.0 / (1.0 + jnp.exp(-x))


def _swiglu(gate, up, *, name):
    def body(tm, g_r, u_r, o):
        g = g_r[...]
        o[...] = (g * _sigmoid(g) * u_r[...]).astype(o.dtype)

    return _rowwise(body, [gate, up], [_sds(gate.shape, MXU_DTYPE)], name=name, rows=gate.shape[0], tm=256)[0]


def _swiglu_bwd(gate, up, dact, *, name):
    def body(tm, g_r, u_r, d_r, dg_o, du_o):
        g, d = g_r[...], d_r[...]
        sg = _sigmoid(g)
        dg_o[...] = (d * u_r[...] * (sg * (1.0 + g * (1.0 - sg)))).astype(dg_o.dtype)
        du_o[...] = (d * g * sg).astype(du_o.dtype)

    outs = [_sds(gate.shape, MXU_DTYPE)] * 2
    return _rowwise(body, [gate, up, dact], outs, name=name, rows=gate.shape[0], tm=256)


def _head_scale(w, o, j):
    return w[:, j:j + 1] * o[:, j * HEAD_DIM:(j + 1) * HEAD_DIM]


def _merge(os_, lses, *, name):
    S = os_[0].shape[0]

    def body(tm, o0, o1, o2, l0, l1, l2, o_o, w0_o, w1_o, w2_o):
        ls = [l0[...], l1[...], l2[...]]
        m = jnp.maximum(jnp.maximum(ls[0], ls[1]), ls[2])
        es = [jnp.exp(l - m) for l in ls]
        tot = es[0] + es[1] + es[2]
        ws = [e / tot for e in es]
        for w_o, w in zip((w0_o, w1_o, w2_o), ws):
            w_o[...] = w
        for j in range(DIL_HEADS):
            sl = slice(j * HEAD_DIM, (j + 1) * HEAD_DIM)
            o_o[:, sl] = _head_scale(ws[0], o0, j) + _head_scale(ws[1], o1, j) + _head_scale(ws[2], o2, j)

    outs = [_sds((S, 1024))] + [_sds((S, LANES))] * 3
    return _rowwise(body, list(os_) + list(lses), outs, name=name, rows=S, tm=256)


def _merge_bwd(do, ws, *, name):
    S = do.shape[0]

    def body(tm, do_r, w0, w1, w2, d0, d1, d2):
        for w_r, d_o in zip((w0, w1, w2), (d0, d1, d2)):
            w = w_r[...]
            for j in range(DIL_HEADS):
                d_o[:, j * HEAD_DIM:(j + 1) * HEAD_DIM] = _head_scale(w, do_r, j).astype(d_o.dtype)

    return _rowwise(body, [do] + list(ws), [_sds((S, 1024), MXU_DTYPE)] * 3, name=name, rows=S, tm=256)


def _loss_head(x, g, target, *, name):
    S, D = x.shape

    def body(tm, x_r, g_r, t_r, dx_o, dg_o, sq_o):
        xf = x_r[...]
        xhat, _ = _rms_parts(xf)
        err = xhat * g_r[...] - t_r[...]
        dx, dgp = _rms_bwd_rows(xf, g_r[...], err * (1.0 / D))
        dx_o[...] = dx
        _acc_rows(dg_o, dgp)
        _acc_rows(sq_o, err * err)

    return _rowwise(body, [x, g.reshape(1, D), target], [_sds((S, D)), _sds((1, D)), _sds((1, D))],
                    name=name, rows=S, accs=(1, 2))


def _adamw(w, g, m, v, *, name):
    c1 = 1.0 - ADAM_B1 ** ADAM_STEP
    c2 = 1.0 - ADAM_B2 ** ADAM_STEP

    def body(tm, w_r, g_r, m_r, v_r, d_o, m_o, v_o):
        g = g_r[...]
        m_new = ADAM_B1 * m_r[...] + (1.0 - ADAM_B1) * g
        v_new = ADAM_B2 * v_r[...] + (1.0 - ADAM_B2) * (g * g)
        m_o[...] = m_new
        v_o[...] = v_new
        d_o[...] = -ADAM_LR * ((m_new / c1) / (jnp.sqrt(v_new / c2) + ADAM_EPS) + ADAM_WD * w_r[...])

    return _rowwise(body, [w, g, m, v], [_sds(w.shape)] * 3, name=name, rows=w.shape[0], tm=256)


SUM_ROW_TILE = 256


def _sum_cores(grads, theirs, half_index, *, name):
    _, R, C = grads.shape
    h = R // 2
    nb = h // SUM_ROW_TILE

    def body(c_ref, g_ref, t_ref, o_ref):
        o_ref[...] = (g_ref[...].astype(F32) + t_ref[...].astype(F32)).astype(o_ref.dtype)

    grid_spec = pltpu.PrefetchScalarGridSpec(
        num_scalar_prefetch=1, grid=(4, nb),
        in_specs=[pl.BlockSpec((1, SUM_ROW_TILE, C), lambda k, i, c_ref: (k, c_ref[0] * nb + i, 0)),
                  pl.BlockSpec((1, SUM_ROW_TILE, C), lambda k, i, c_ref: (k, i, 0))],
        out_specs=pl.BlockSpec((1, SUM_ROW_TILE, C), lambda k, i, c_ref: (k, i, 0)))
    return _pcall(body, name=name, dims=("parallel", "parallel"), grid_spec=grid_spec,
                  out_shape=_sds((4, h, C), grads.dtype))(half_index, grads, theirs)


def _sum_chips(parts, half_index, *, name):
    _, h, C = parts.shape
    nb = h // SUM_ROW_TILE

    def body(c_ref, p_ref, o_ref):
        p = [p_ref[k].astype(F32) for k in range(4)]
        o_ref[...] = ((p[0] + p[1]) + p[2]) + p[3]

    grid_spec = pltpu.PrefetchScalarGridSpec(
        num_scalar_prefetch=1, grid=(nb,),
        in_specs=[pl.BlockSpec((4, SUM_ROW_TILE, C), lambda i, c_ref: (0, i, 0))],
        out_specs=pl.BlockSpec((SUM_ROW_TILE, C), lambda i, c_ref: (c_ref[0] * nb + i, 0)))
    return _pcall(body, name=name, dims=("parallel",), grid_spec=grid_spec,
                  out_shape=_sds((2 * h, C)))(half_index, parts)


def _position():
    return lax.axis_index("x"), lax.axis_index("y"), lax.axis_index("c")


def _chip_peers(x, y):
    return [(1 - x, y), (x, 1 - y), (1 - x, 1 - y)]


_HBM = pl.BlockSpec(memory_space=pltpu.HBM)
LOCAL_COPY_CHUNKS = 8


def _local_copies(src_ref, dst_ref, sems):
    rows = src_ref.shape[0] // LOCAL_COPY_CHUNKS
    assert rows * LOCAL_COPY_CHUNKS == src_ref.shape[0]
    return [pltpu.make_async_copy(src_ref.at[pl.ds(i * rows, rows)], dst_ref.at[pl.ds(i * rows, rows)], sems.at[i])
            for i in range(LOCAL_COPY_CHUNKS)]


def _gather_weights(src, *, name):
    R, C = src.shape
    h = R // 2

    def body(src_ref, out_ref, send_sems, recv_sems, local_sems):
        x, y, c = _position()
        me = 2 * x + y
        peers = _chip_peers(x, y)
        mine, other = pl.ds(c * h, h), pl.ds((1 - c) * h, h)

        def copy(sem, src_part, dst_part, device):
            return pltpu.make_async_remote_copy(
                src_ref=src_part, dst_ref=dst_part, send_sem=send_sems.at[sem], recv_sem=recv_sems.at[sem],
                device_id=device, device_id_type=MESH)

        sends = [copy(j, src_ref.at[mine], out_ref.at[me, mine], (px, py, c)) for j, (px, py) in enumerate(peers)]
        for cp in sends:
            cp.start()
        local = _local_copies(src_ref, out_ref.at[me], local_sems)
        for cp in local:
            cp.start()
        passed = []
        for j, (px, py) in enumerate(peers):
            landed = out_ref.at[2 * px + py, mine]
            copy(j, landed, landed, (px, py, c)).wait_recv()
            passed.append(copy(3 + j, landed, landed, (x, y, 1 - c)))
            passed[-1].start()
        for j, (px, py) in enumerate(peers):
            theirs = out_ref.at[2 * px + py, other]
            copy(3 + j, theirs, theirs, (x, y, 1 - c)).wait_recv()
        for cp in sends + passed:
            cp.wait_send()
        for cp in local:
            cp.wait()

    return pl.pallas_call(
        body, name=name, in_specs=[_HBM], out_specs=_HBM, out_shape=jax.ShapeDtypeStruct((4, R, C), src.dtype),
        scratch_shapes=[pltpu.SemaphoreType.DMA((6,)), pltpu.SemaphoreType.DMA((6,)),
                        pltpu.SemaphoreType.DMA((LOCAL_COPY_CHUNKS,))],
    )(src)


def _swap_other_half(src, *, name):
    _, R, C = src.shape
    h = R // 2

    def body(src_ref, out_ref, send_sem, recv_sem):
        x, y, c = _position()
        cp = pltpu.make_async_remote_copy(
            src_ref=src_ref.at[:, pl.ds((1 - c) * h, h)], dst_ref=out_ref, send_sem=send_sem, recv_sem=recv_sem,
            device_id=(x, y, 1 - c), device_id_type=MESH)
        cp.start()
        cp.wait()

    return pl.pallas_call(
        body, name=name, in_specs=[_HBM], out_specs=_HBM, out_shape=jax.ShapeDtypeStruct((4, h, C), src.dtype),
        scratch_shapes=[pltpu.SemaphoreType.DMA, pltpu.SemaphoreType.DMA],
    )(src)


def _scatter_chips(src, *, name):
    def body(src_ref, out_ref, send_sems, recv_sems, local_sems):
        x, y, c = _position()
        me = 2 * x + y
        peers = _chip_peers(x, y)

        def copy(j, src_block, dst_slot):
            px, py = peers[j]
            return pltpu.make_async_remote_copy(
                src_ref=src_ref.at[src_block], dst_ref=out_ref.at[dst_slot], send_sem=send_sems.at[j],
                recv_sem=recv_sems.at[j], device_id=(px, py, c), device_id_type=MESH)

        sends = [copy(j, 2 * px + py, me) for j, (px, py) in enumerate(peers)]
        for cp in sends:
            cp.start()
        local = _local_copies(src_ref.at[me], out_ref.at[me], local_sems)
        for cp in local:
            cp.start()
        for j, (px, py) in enumerate(peers):
            copy(j, me, 2 * px + py).wait_recv()
        for cp in sends:
            cp.wait_send()
        for cp in local:
            cp.wait()

    return pl.pallas_call(
        body, name=name, in_specs=[_HBM], out_specs=_HBM, out_shape=jax.ShapeDtypeStruct(src.shape, src.dtype),
        scratch_shapes=[pltpu.SemaphoreType.DMA((3,)), pltpu.SemaphoreType.DMA((3,)),
                        pltpu.SemaphoreType.DMA((LOCAL_COPY_CHUNKS,))],
    )(src)


def _join_halves(src, *, name):
    R, C = src.shape
    h = R // 2

    def body(src_ref, out_ref, send_sem, recv_sem):
        x, y, c = _position()
        mine, theirs = pl.ds(c * h, h), pl.ds((1 - c) * h, h)
        cp = pltpu.make_async_remote_copy(
            src_ref=src_ref.at[mine], dst_ref=out_ref.at[mine], send_sem=send_sem, recv_sem=recv_sem,
            device_id=(x, y, 1 - c), device_id_type=MESH)
        cp.start()
        pltpu.make_async_remote_copy(
            src_ref=src_ref.at[theirs], dst_ref=out_ref.at[theirs], send_sem=send_sem, recv_sem=recv_sem,
            device_id=(x, y, 1 - c), device_id_type=MESH).wait_recv()
        cp.wait_send()

    return pl.pallas_call(
        body, name=name, in_specs=[_HBM], out_specs=_HBM, out_shape=jax.ShapeDtypeStruct((R, C), src.dtype),
        input_output_aliases={0: 0},
        scratch_shapes=[pltpu.SemaphoreType.DMA, pltpu.SemaphoreType.DMA],
    )(src)


def _allreduce_small(vec, *, name):
    R, C = vec.shape

    def body(v_ref, o_ref, slots, send_sems, recv_sems):
        x, y, c = _position()
        me = 4 * x + 2 * y + c

        def peer(k):
            return x ^ ((k >> 2) & 1), y ^ ((k >> 1) & 1), c ^ (k & 1)

        def copy(k, slot):
            return pltpu.make_async_remote_copy(
                src_ref=v_ref, dst_ref=slots.at[slot], send_sem=send_sems.at[k - 1], recv_sem=recv_sems.at[k - 1],
                device_id=peer(k), device_id_type=MESH)

        slots[me] = v_ref[...]
        sends = [copy(k, me) for k in range(1, 8)]
        for cp in sends:
            cp.start()
        for k in range(1, 8):
            px, py, pc = peer(k)
            copy(k, 4 * px + 2 * py + pc).wait_recv()
        total = slots[0]
        for d in range(1, 8):
            total = total + slots[d]
        o_ref[...] = total
        for cp in sends:
            cp.wait_send()

    vmem = pl.BlockSpec(memory_space=pltpu.VMEM)
    return pl.pallas_call(
        body, name=name, in_specs=[vmem], out_specs=vmem, out_shape=jax.ShapeDtypeStruct((R, C), vec.dtype),
        scratch_shapes=[pltpu.VMEM((8, R, C), vec.dtype), pltpu.SemaphoreType.DMA((7,)), pltpu.SemaphoreType.DMA((7,))],
    )(vec)


def _cross_cfg(S, mem_len):
    return _Attn(T=S, Tk=mem_len, G=1, nh=X_HEADS, rep=1, dqk=X_HEAD_DIM, dv=X_HEAD_DIM, tq=512, tk=mem_len,
                 mode="none", scale=X_HEAD_DIM ** -0.5, qcol=lambda g: 0, kcol=lambda g: 0, vcol=lambda g: 1,
                 ocol=lambda g: 0, o_width=X_HEADS * X_HEAD_DIM)


def _swa_cfg(S):
    return _Attn(T=S, Tk=S, G=1, nh=SWA_HEADS, rep=SWA_HEADS // SWA_KV_HEADS, dqk=HEAD_DIM, dv=HEAD_DIM, tq=BLOCK,
                 tk=BLOCK, mode="band", max_dist=SWA_WINDOW - 1, scale=HEAD_DIM ** -0.5, qcol=lambda g: 0,
                 kcol=lambda g: 0, vcol=lambda g: 0, ocol=lambda g: 0, o_width=SWA_HEADS * HEAD_DIM)


def _mla_cfg(S):
    t = _tile(S, 512)
    return _Attn(T=S, Tk=S, G=MLA_HEADS // 2, nh=2, rep=1, dqk=LANES, dv=MLA_V, tq=t, tk=t, mode="causal",
                 scale=(MLA_NOPE + MLA_ROPE) ** -0.5, qcol=lambda g: g, kcol=lambda g: g, vcol=lambda g: g,
                 ocol=lambda g: g, o_width=MLA_HEADS * MLA_V)


def _dil_cfg(S, window, dil):
    return _Attn(T=S // dil, Tk=S // dil, G=dil, nh=DIL_HEADS, rep=1, dqk=HEAD_DIM, dv=HEAD_DIM, tq=BLOCK, tk=BLOCK,
                 mode="band", max_dist=window // dil, scale=HEAD_DIM ** -0.5, qcol=lambda g: g, kcol=lambda g: g,
                 vcol=lambda g: g, ocol=lambda g: g, o_width=dil * DIL_HEADS * HEAD_DIM)


def _rows_cfg(S):
    return _Attn(T=S, Tk=S, G=1, nh=DIL_HEADS, rep=1, dqk=HEAD_DIM, dv=HEAD_DIM, tq=BLOCK, tk=BLOCK, mode="none",
                 scale=1.0, qcol=lambda g: 0, kcol=lambda g: 0, vcol=lambda g: 0, ocol=lambda g: 0,
                 o_width=DIL_HEADS * HEAD_DIM)


def _cross_fwd(p, x, mem, W, vec):
    S = x.shape[0]
    cfg = _cross_cfg(S, mem.shape[0])
    hx = _rmsnorm(x, vec[p + "x_norm"], name=p + "x_norm")
    qx = _mm(hx, W[p + "w_xq"], mode="nn", name=p + "xq", out_dtype=MXU_DTYPE)
    memn = _rmsnorm(mem, vec[p + "mem_norm"], name=p + "mem_norm")
    kvx = _mm(memn, W[p + "w_xkv"], mode="nn", name=p + "xkv", out_dtype=MXU_DTYPE)
    ox, lse = _attn_fwd(cfg, qx, kvx, kvx, name=p + "x_attn", out_dtype=MXU_DTYPE)
    out = _mm(ox, W[p + "w_xo"], mode="nn", name=p + "xo", res=x)
    return out, (x, hx, qx, memn, kvx, ox, lse)


def _cross_bwd(p, dx, saved, mem, W, vec, dW, dvec):
    x, hx, qx, memn, kvx, ox, lse = saved
    cfg = _cross_cfg(x.shape[0], mem.shape[0])
    dox = _mm(dx, W[p + "w_xo"], mode="nt", name=p + "xo_dx", out_dtype=MXU_DTYPE)
    dW[p + "w_xo"] = _mm(ox, dx, mode="tn", name=p + "xo_dw")
    delta, _ = _attn_delta(cfg, ox, dox, name=p + "x_delta")
    dqx = _attn_dq(cfg, qx, kvx, kvx, dox, lse, delta, name=p + "x_dq", out_dtype=MXU_DTYPE)
    dkx, dvx = _attn_dkv(cfg, qx, kvx, kvx, dox, lse, delta, name=p + "x_dkv", out_dtype=MXU_DTYPE)
    dkvx = jnp.concatenate([dkx, dvx], axis=1)
    dhx = _mm(dqx, W[p + "w_xq"], mode="nt", name=p + "xq_dx")
    dW[p + "w_xq"] = _mm(hx, dqx, mode="tn", name=p + "xq_dw")
    dW[p + "w_xkv"] = _mm(memn, dkvx, mode="tn", name=p + "xkv_dw")
    dmemn = _mm(dkvx, W[p + "w_xkv"], mode="nt", name=p + "xkv_dx")
    _, dvec[p + "mem_norm"] = _rmsnorm_bwd(mem, vec[p + "mem_norm"], dmemn, name=p + "mem_norm_bwd")
    dx_in, dvec[p + "x_norm"] = _rmsnorm_bwd(x, vec[p + "x_norm"], dhx, name=p + "x_norm_bwd", dres=dx)
    return dx_in


def _ffn_fwd(p, x, W, vec):
    hf = _rmsnorm(x, vec[p + "ffn_norm"], name=p + "ffn_norm")
    gate = _mm(hf, W[p + "w_gate"], mode="nn", name=p + "gate")
    up = _mm(hf, W[p + "w_up"], mode="nn", name=p + "up")
    act = _swiglu(gate, up, name=p + "swiglu")
    out = _mm(act, W[p + "w_down"], mode="nn", name=p + "down", res=x)
    return out, (x, hf, gate, up, act)


def _ffn_bwd(p, dx, saved, W, vec, dW, dvec):
    x, hf, gate, up, act = saved
    dact = _mm(dx, W[p + "w_down"], mode="nt", name=p + "down_dx")
    dW[p + "w_down"] = _mm(act, dx, mode="tn", name=p + "down_dw")
    dgate, dup = _swiglu_bwd(gate, up, dact, name=p + "swiglu_bwd")
    dhf = _mm(dgate, W[p + "w_gate"], mode="nt", name=p + "gate_dx")
    dhf = _mm(dup, W[p + "w_up"], mode="nt", name=p + "up_dx", res=dhf)
    dW[p + "w_gate"] = _mm(hf, dgate, mode="tn", name=p + "gate_dw")
    dW[p + "w_up"] = _mm(hf, dup, mode="tn", name=p + "up_dw")
    dx_in, dvec[p + "ffn_norm"] = _rmsnorm_bwd(x, vec[p + "ffn_norm"], dhf, name=p + "ffn_norm_bwd", dres=dx)
    return dx_in


def _even_fwd(p, x, tabs, W, vec):
    S = x.shape[0]
    h = _rmsnorm(x, vec[p + "mix_norm"], name=p + "mix_norm")
    z = _mm(h, W[p + "w_in"], mode="nn", name=p + "in")
    qa, ka, va, cqn, ckvn, kr = _l0_prep(z, tabs, vec[p + "q_norm"], vec[p + "kv_norm"], name=p + "prep")
    sink = jnp.pad(vec[p + "sinks"], (0, LANES - SWA_HEADS)).reshape(1, LANES)
    oa, lse_a = _band_fwd(_swa_cfg(S), qa, ka, va, name=p + "swa", sink=sink, out_dtype=MXU_DTYPE)
    qb = _mm(cqn, W[p + "w_uq"], mode="nn", name=p + "uq")
    kvb = _mm(ckvn, W[p + "w_ukv"], mode="nn", name=p + "ukv")
    Q, K, V = _mla_prep(qb, kvb, kr, tabs, name=p + "mla_prep")
    ob, lse_b = _causal_fwd(_mla_cfg(S), Q, K, V, name=p + "mla", out_dtype=MXU_DTYPE)
    o = jnp.concatenate([oa, ob], axis=1)
    out = _mm(o, W[p + "w_out"], mode="nn", name=p + "out", res=x)
    return out, (x, h, z, qa, ka, va, cqn, ckvn, sink, oa, lse_a, Q, K, V, ob, lse_b, o)


def _even_bwd(p, dx, saved, tabs, W, vec, dW, dvec):
    x, h, z, qa, ka, va, cqn, ckvn, sink, oa, lse_a, Q, K, V, ob, lse_b, o = saved
    S = x.shape[0]
    do = _mm(dx, W[p + "w_out"], mode="nt", name=p + "out_dx", out_dtype=MXU_DTYPE)
    dW[p + "w_out"] = _mm(o, dx, mode="tn", name=p + "out_dw")
    doa, dob = do[:, :SWA_HEADS * HEAD_DIM], do[:, SWA_HEADS * HEAD_DIM:]
    cfg = _swa_cfg(S)
    delta, dsink = _attn_delta(cfg, oa, doa, name=p + "swa_delta", lse=lse_a, sink=sink)
    dvec[p + "sinks"] = dsink
    dqa, dka, dva = _band_bwd(cfg, qa, ka, va, doa, lse_a, delta, name=p + "swa_bwd")
    cfg = _mla_cfg(S)
    delta, _ = _attn_delta(cfg, ob, dob, name=p + "mla_delta")
    dQ, dK, dV = _causal_bwd(cfg, Q, K, V, dob, lse_b, delta, name=p + "mla_bwd")
    dqb, dkvb, dkr = _mla_prep_bwd(dQ, dK, dV, tabs, name=p + "mla_prep_bwd")
    dcqn = _mm(dqb, W[p + "w_uq"], mode="nt", name=p + "uq_dx")
    dW[p + "w_uq"] = _mm(cqn, dqb, mode="tn", name=p + "uq_dw")
    dckvn = _mm(dkvb, W[p + "w_ukv"], mode="nt", name=p + "ukv_dx")
    dW[p + "w_ukv"] = _mm(ckvn, dkvb, mode="tn", name=p + "ukv_dw")
    dz, dvec[p + "q_norm"], dvec[p + "kv_norm"] = _l0_prep_bwd(
        z, tabs, vec[p + "q_norm"], vec[p + "kv_norm"], dqa, dka, dva, dcqn, dckvn, dkr, name=p + "prep_bwd")
    dh = _mm(dz, W[p + "w_in"], mode="nt", name=p + "in_dx")
    dW[p + "w_in"] = _mm(h, dz, mode="tn", name=p + "in_dw")
    dx_in, dvec[p + "mix_norm"] = _rmsnorm_bwd(x, vec[p + "mix_norm"], dh, name=p + "mix_norm_bwd", dres=dx)
    return dx_in


def _odd_fwd(p, x, tabs, W, vec):
    S = x.shape[0]
    assert S % (DIL_PATTERNS[-1][1] * BLOCK) == 0, "keys past the end of the sequence are never attended"
    h = _rmsnorm(x, vec[p + "mix_norm"], name=p + "mix_norm")
    qkv = _mm(h, W[p + "w_qkv"], mode="nn", name=p + "qkv")
    q, k, v = _l1_prep(qkv, tabs, name=p + "prep")
    outs, lses = [], []
    for window, dil in DIL_PATTERNS:
        cfg = _dil_cfg(S, window, dil)
        view = lambda t: t.reshape(S // dil, dil * t.shape[1])
        o_b, lse_b = _band_fwd(cfg, view(q), view(k), view(v), name=p + "dil%d" % dil)
        outs.append(o_b.reshape(S, -1))
        lses.append(lse_b.reshape(S, LANES))
    o, w0, w1, w2 = _merge(outs, lses, name=p + "merge")
    out = _mm(o, W[p + "w_out"], mode="nn", name=p + "out", res=x)
    return out, (x, h, q, k, v, lses, (w0, w1, w2), o)


def _odd_bwd(p, dx, saved, tabs, W, vec, dW, dvec):
    x, h, q, k, v, lses, ws, o = saved
    S = x.shape[0]
    do = _mm(dx, W[p + "w_out"], mode="nt", name=p + "out_dx")
    dW[p + "w_out"] = _mm(o, dx, mode="tn", name=p + "out_dw")
    dos = _merge_bwd(do, ws, name=p + "merge_bwd")
    dq = dk = dv = None
    for b, (window, dil) in enumerate(DIL_PATTERNS):
        cfg = _dil_cfg(S, window, dil)
        view = lambda t: t.reshape(S // dil, dil * t.shape[1])
        delta, _ = _attn_delta(_rows_cfg(S), o, do, name=p + "delta%d" % dil, w=ws[b])
        args = (view(q), view(k), view(v), view(dos[b]), view(lses[b]), view(delta))
        init = None if dq is None else (view(dq), view(dk), view(dv))
        dq, dk, dv = (t.reshape(S, -1) for t in _band_bwd(cfg, *args, name=p + "dil%d_bwd" % dil, init=init))
    dqkv = _l1_prep_bwd(dq, dk, dv, tabs, name=p + "prep_bwd")
    dh = _mm(dqkv, W[p + "w_qkv"], mode="nt", name=p + "qkv_dx")
    dW[p + "w_qkv"] = _mm(h, dqkv, mode="tn", name=p + "qkv_dw")
    dx_in, dvec[p + "mix_norm"] = _rmsnorm_bwd(x, vec[p + "mix_norm"], dh, name=p + "mix_norm_bwd", dres=dx)
    return dx_in


def _local_step(x, mem, positions, target, W, vec):
    tabs = _rope_tables(positions)
    x1, s_mix0 = _even_fwd("l0_", x, tabs, W, vec)
    x2, s_x0 = _cross_fwd("l0_", x1, mem, W, vec)
    x3, s_f0 = _ffn_fwd("l0_", x2, W, vec)
    x4, s_mix1 = _odd_fwd("l1_", x3, tabs, W, vec)
    x5, s_x1 = _cross_fwd("l1_", x4, mem, W, vec)
    x6, s_f1 = _ffn_fwd("l1_", x5, W, vec)
    dW, dvec = {}, {}
    dx, dvec["final_norm"], sq = _loss_head(x6, vec["final_norm"], target, name="loss_head")
    dx = _ffn_bwd("l1_", dx, s_f1, W, vec, dW, dvec)
    dx = _cross_bwd("l1_", dx, s_x1, mem, W, vec, dW, dvec)
    dx = _odd_bwd("l1_", dx, s_mix1, tabs, W, vec, dW, dvec)
    dx = _ffn_bwd("l0_", dx, s_f0, W, vec, dW, dvec)
    dx = _cross_bwd("l0_", dx, s_x0, mem, W, vec, dW, dvec)
    dx = _even_bwd("l0_", dx, s_mix0, tabs, W, vec, dW, dvec)
    return sq, dx, dW, dvec


_LAYER_MATS = {
    0: [("w_in", "col"), ("w_uq", "col"), ("w_ukv", "col"), ("w_out", "row"), ("w_xq", "row"), ("w_xkv", "row"),
        ("w_xo", "col"), ("w_gate", "col"), ("w_up", "col"), ("w_down", "row")],
    1: [("w_qkv", "col"), ("w_out", "row"), ("w_xq", "row"), ("w_xkv", "row"), ("w_xo", "col"), ("w_gate", "col"),
        ("w_up", "col"), ("w_down", "row")],
}
MATS = [("l%d_%s" % (l, n), kind) for l in (0, 1) for n, kind in _LAYER_MATS[l]]
_LAYER_VECS = {0: ["mix_norm", "sinks", "q_norm", "kv_norm", "x_norm", "mem_norm", "ffn_norm"],
               1: ["mix_norm", "x_norm", "mem_norm", "ffn_norm"]}
VECS = ["l%d_%s" % (l, n) for l in (0, 1) for n in _LAYER_VECS[l]] + ["final_norm"]
WEIGHT_ORDER = (["l0_mix_norm", "l0_w_in", "l0_sinks", "l0_q_norm", "l0_w_uq", "l0_kv_norm", "l0_w_ukv", "l0_w_out",
                 "l0_x_norm", "l0_mem_norm", "l0_w_xq", "l0_w_xkv", "l0_w_xo", "l0_ffn_norm", "l0_w_gate", "l0_w_up",
                 "l0_w_down", "l1_mix_norm", "l1_w_qkv", "l1_w_out", "l1_x_norm", "l1_mem_norm", "l1_w_xq",
                 "l1_w_xkv", "l1_w_xo", "l1_ffn_norm", "l1_w_gate", "l1_w_up", "l1_w_down", "final_norm"])
PACK_COLS = 1024
PACK_ROW_TILE = 2 * SUM_ROW_TILE
EXCHANGE_DTYPE = jnp.bfloat16
VEC_ROWS = 16
LOSS_ROW = len(VECS)
N_CHIPS = 4


def _pack_layout(shards):
    layout, off = {}, 0
    for name, _ in MATS:
        n = shards[name].size // PACK_COLS
        assert n * PACK_COLS == shards[name].size
        layout[name] = (off, n)
        off += n
    return layout, -(-off // PACK_ROW_TILE) * PACK_ROW_TILE


def _pack_shards(shards, layout, rows, dtype):
    parts = [shards[name].astype(dtype).reshape(-1, PACK_COLS) for name, _ in MATS]
    used = sum(p.shape[0] for p in parts)
    return jnp.concatenate(parts + [jnp.zeros((rows - used, PACK_COLS), dtype)], axis=0)


def _unpack_shards(packed, layout, shards):
    return {name: packed[off:off + n].reshape(shards[name].shape) for name, (off, n) in layout.items()}


def _full_weights(gathered, layout, shards):
    W = {}
    for name, kind in MATS:
        off, n = layout[name]
        r, cw = shards[name].shape
        blocks = gathered[:, off:off + n].reshape(N_CHIPS, r, cw)
        W[name] = blocks.reshape(N_CHIPS * r, cw) if kind == "row" else (
            jnp.transpose(blocks, (1, 0, 2)).reshape(r, N_CHIPS * cw))
    W["l0_w_in"] = jnp.pad(W["l0_w_in"], ((0, 0), (0, Z_END - W["l0_w_in"].shape[1])))
    per_head = MLA_NOPE + MLA_ROPE
    uq = W["l0_w_uq"].reshape(MLA_Q_RANK, MLA_HEADS, per_head)
    W["l0_w_uq"] = jnp.pad(uq, ((0, 0), (0, 0), (0, LANES - per_head))).reshape(MLA_Q_RANK, MLA_HEADS * LANES)
    return W


def _pack_grads(dW, layout, rows, shards):
    per_head = MLA_NOPE + MLA_ROPE
    dW = dict(dW)
    dW["l0_w_in"] = dW["l0_w_in"][:, :Z_KR + MLA_ROPE]
    dW["l0_w_uq"] = dW["l0_w_uq"].reshape(MLA_Q_RANK, MLA_HEADS, LANES)[:, :, :per_head].reshape(MLA_Q_RANK, -1)
    parts = []
    for name, kind in MATS:
        r, cw = shards[name].shape
        g = dW[name]
        if kind == "col":
            g = jnp.transpose(g.reshape(r, N_CHIPS, cw), (1, 0, 2))
        parts.append(g.reshape(N_CHIPS, -1, PACK_COLS).astype(EXCHANGE_DTYPE))
    used = sum(p.shape[1] for p in parts)
    return jnp.concatenate(parts + [jnp.zeros((N_CHIPS, rows - used, PACK_COLS), EXCHANGE_DTYPE)], axis=1)


def _pack_vecs(vecs):
    rows = [jnp.pad(vecs[n].reshape(-1).astype(F32), (0, PACK_COLS - vecs[n].size)) for n in VECS]
    rows += [jnp.zeros((PACK_COLS,), F32)] * (VEC_ROWS - len(rows))
    return jnp.stack(rows)


def _unpack_vecs(packed, like):
    return {n: packed[i, :like[n].size].reshape(like[n].shape) for i, n in enumerate(VECS)}


def _step(a):
    weights = {n: a[n] for n in WEIGHT_ORDER}
    shards = {n: weights[n] for n, _ in MATS}
    vec = {n: weights[n] for n in VECS}
    layout, rows = _pack_layout(shards)

    gathered = _gather_weights(_pack_shards(shards, layout, rows, MXU_DTYPE), name="gather_weights")
    W = _full_weights(gathered, layout, shards)
    sq, grad_x, dW, dvec = _local_step(a["x"][0], a["mem"][0], a["positions"], a["loss_target"][0], W, vec)

    dvec = dict(dvec)
    dvec["l0_sinks"] = dvec["l0_sinks"][0, :SWA_HEADS]
    small = _pack_vecs(dvec)
    small = small.at[LOSS_ROW, 0].set(0.5 / a["x"].shape[-1] * jnp.sum(sq))
    small = _allreduce_small(small, name="reduce_gains")
    loss = small[LOSS_ROW, 0]
    g_s = small.at[LOSS_ROW, 0].set(0.0)
    d_s, m_s, v_s = _adamw(_pack_vecs(vec), g_s, _pack_vecs({n: a["m_" + n] for n in VECS}),
                           _pack_vecs({n: a["v_" + n] for n in VECS}), name="adamw_gains")

    grads = _pack_grads(dW, layout, rows, shards)
    half_index = lax.axis_index("c").astype(jnp.int32).reshape(1)
    chip_sum = _sum_cores(grads, _swap_other_half(grads, name="swap_other_half"), half_index, name="sum_cores")
    mine = _sum_chips(_scatter_chips(chip_sum, name="scatter_grads"), half_index, name="sum_chips")
    g_w = _join_halves(mine, name="join_halves")
    d_w, m_w, v_w = _adamw(
        _pack_shards(shards, layout, rows, F32), g_w,
        _pack_shards({n: a["m_" + n] for n, _ in MATS}, layout, rows, F32),
        _pack_shards({n: a["v_" + n] for n, _ in MATS}, layout, rows, F32), name="adamw_mats")

    out = [loss, grad_x[None]]
    for packed_w, packed_s in ((g_w, g_s), (d_w, d_s), (m_w, m_s), (v_w, v_s)):
        got = {**_unpack_shards(packed_w, layout, shards), **_unpack_vecs(packed_s, vec)}
        out += [got[n] for n in WEIGHT_ORDER]
    return tuple(out)


def kernel(x, mem, positions, l0_mix_norm, l0_w_in, l0_sinks, l0_q_norm, l0_w_uq, l0_kv_norm, l0_w_ukv, l0_w_out, l0_x_norm, l0_mem_norm, l0_w_xq, l0_w_xkv, l0_w_xo, l0_ffn_norm, l0_w_gate, l0_w_up, l0_w_down, l1_mix_norm, l1_w_qkv, l1_w_out, l1_x_norm, l1_mem_norm, l1_w_xq, l1_w_xkv, l1_w_xo, l1_ffn_norm, l1_w_gate, l1_w_up, l1_w_down, final_norm, loss_target, m_l0_mix_norm, m_l0_w_in, m_l0_sinks, m_l0_q_norm, m_l0_w_uq, m_l0_kv_norm, m_l0_w_ukv, m_l0_w_out, m_l0_x_norm, m_l0_mem_norm, m_l0_w_xq, m_l0_w_xkv, m_l0_w_xo, m_l0_ffn_norm, m_l0_w_gate, m_l0_w_up, m_l0_w_down, m_l1_mix_norm, m_l1_w_qkv, m_l1_w_out, m_l1_x_norm, m_l1_mem_norm, m_l1_w_xq, m_l1_w_xkv, m_l1_w_xo, m_l1_ffn_norm, m_l1_w_gate, m_l1_w_up, m_l1_w_down, m_final_norm, v_l0_mix_norm, v_l0_w_in, v_l0_sinks, v_l0_q_norm, v_l0_w_uq, v_l0_kv_norm, v_l0_w_ukv, v_l0_w_out, v_l0_x_norm, v_l0_mem_norm, v_l0_w_xq, v_l0_w_xkv, v_l0_w_xo, v_l0_ffn_norm, v_l0_w_gate, v_l0_w_up, v_l0_w_down, v_l1_mix_norm, v_l1_w_qkv, v_l1_w_out, v_l1_x_norm, v_l1_mem_norm, v_l1_w_xq, v_l1_w_xkv, v_l1_w_xo, v_l1_ffn_norm, v_l1_w_gate, v_l1_w_up, v_l1_w_down, v_final_norm):
    return _step(dict(locals()))
```

```python
import functools

import jax
import jax.numpy as jnp
import numpy as np
from jax import lax
from jax.experimental import pallas as pl
from jax.experimental.pallas import tpu as pltpu

F32 = jnp.float32
MXU_DTYPE = jnp.bfloat16
LANES = 128
VMEM_LIMIT_BYTES = 56 * 1024 * 1024

NORM_EPS = 1e-6
ROPE_THETA = 10000.0
BLOCK = 128
HEAD_DIM = 64
SWA_HEADS, SWA_KV_HEADS, SWA_WINDOW = 8, 2, 128
MLA_HEADS, MLA_Q_RANK, MLA_KV_RANK, MLA_NOPE, MLA_ROPE, MLA_V = 8, 384, 256, 64, 32, 64
DIL_HEADS = 16
DIL_PATTERNS = ((128, 1), (512, 4), (2048, 16))
X_HEADS, X_HEAD_DIM = 4, 128
ADAM_LR, ADAM_B1, ADAM_B2, ADAM_EPS, ADAM_WD, ADAM_STEP = 0.001, 0.9, 0.999, 1e-08, 0.01, 10
MESH = pl.DeviceIdType.MESH
NEG_BIG = -1e30

NN = (((1,), (0,)), ((), ()))
NT = (((1,), (1,)), ((), ()))


def _dot(a, b, dims=NN):
    return lax.dot_general(a.astype(MXU_DTYPE), b.astype(MXU_DTYPE), dims, preferred_element_type=F32)


def _pcall(body, *, name, dims=None, **kw):
    params = pltpu.CompilerParams(dimension_semantics=dims, vmem_limit_bytes=VMEM_LIMIT_BYTES)
    return pl.pallas_call(body, name=name, compiler_params=params, **kw)


def _tile(n, pref):
    t = (min(pref, n) // LANES) * LANES
    while t >= LANES:
        if n % t == 0:
            return t
        t -= LANES
    return n


def _lane(shape):
    return lax.broadcasted_iota(jnp.int32, shape, 1)


def _cols_to_lanes(cols, rows):
    lane = _lane((rows, LANES))
    out = jnp.zeros((rows, LANES), F32)
    for j, col in enumerate(cols):
        out = jnp.where(lane == j, col, out)
    return out


def _mm(a, b, *, mode, name, res=None, out_dtype=F32, tm=1408, tn=1536, tk=1408):
    if mode == "nn":
        (M, K), (K2, N) = a.shape, b.shape
    elif mode == "nt":
        (M, K), (N, K2) = a.shape, b.shape
    else:
        (K, M), (K2, N) = a.shape, b.shape
    assert K == K2, (a.shape, b.shape, mode)
    tm, tn, tk = _tile(M, tm), _tile(N, tn), _tile(K, tk)
    nk = K // tk
    in_place = out_dtype == F32 or nk == 1

    def body(*refs):
        refs = list(refs)
        a_ref, b_ref = refs[:2]
        r_ref = refs[2] if res is not None else None
        o_ref = refs[3 if res is not None else 2]
        acc = o_ref if in_place else refs[-1]
        k = pl.program_id(2)
        if mode == "nn":
            part = _dot(a_ref[...], b_ref[...], NN)
        elif mode == "nt":
            part = _dot(a_ref[...], b_ref[...], NT)
        else:
            part = _dot(a_ref[...].T, b_ref[...], NN)
        if nk == 1:
            o_ref[...] = (part if res is None else part + r_ref[...].astype(F32)).astype(o_ref.dtype)
            return

        @pl.when(k == 0)
        def _():
            acc[...] = part if res is None else part + r_ref[...].astype(F32)

        @pl.when(k > 0)
        def _():
            acc[...] += part

        if not in_place:
            @pl.when(k == nk - 1)
            def _():
                o_ref[...] = acc[...].astype(o_ref.dtype)

    if mode == "nn":
        a_spec = pl.BlockSpec((tm, tk), lambda i, j, k: (i, k))
        b_spec = pl.BlockSpec((tk, tn), lambda i, j, k: (k, j))
    elif mode == "nt":
        a_spec = pl.BlockSpec((tm, tk), lambda i, j, k: (i, k))
        b_spec = pl.BlockSpec((tn, tk), lambda i, j, k: (j, k))
    else:
        a_spec = pl.BlockSpec((tk, tm), lambda i, j, k: (k, i))
        b_spec = pl.BlockSpec((tk, tn), lambda i, j, k: (k, j))
    o_spec = pl.BlockSpec((tm, tn), lambda i, j, k: (i, j))
    in_specs = [a_spec, b_spec] + ([] if res is None else [o_spec])
    args = (a, b) + (() if res is None else (res,))
    return _pcall(
        body, name=name, dims=("parallel", "parallel", "arbitrary"),
        grid=(M // tm, N // tn, nk), in_specs=in_specs, out_specs=o_spec,
        out_shape=jax.ShapeDtypeStruct((M, N), out_dtype),
        scratch_shapes=[] if in_place else [pltpu.VMEM((tm, tn), F32)],
    )(*args)


def _rms_parts(xf):
    r = lax.rsqrt(jnp.mean(xf * xf, axis=-1, keepdims=True) + NORM_EPS)
    return xf * r, r


def _rms_bwd_rows(xf, g, dy):
    xhat, r = _rms_parts(xf)
    dxhat = dy * g
    dx = r * (dxhat - xhat * jnp.mean(dxhat * xhat, axis=-1, keepdims=True))
    return dx, dy * xhat


def _rmsnorm(x, g, *, name, out_dtype=MXU_DTYPE, tm=512):
    M, D = x.shape
    tm = _tile(M, tm)

    def body(x_ref, g_ref, o_ref):
        xhat, _ = _rms_parts(x_ref[...].astype(F32))
        o_ref[...] = (xhat * g_ref[...]).astype(o_ref.dtype)

    return _pcall(
        body, name=name, dims=("parallel",), grid=(M // tm,),
        in_specs=[pl.BlockSpec((tm, D), lambda i: (i, 0)), pl.BlockSpec((1, D), lambda i: (0, 0))],
        out_specs=pl.BlockSpec((tm, D), lambda i: (i, 0)),
        out_shape=jax.ShapeDtypeStruct((M, D), out_dtype),
    )(x, g.reshape(1, D))


def _rmsnorm_bwd(x, g, dy, *, name, dres=None, tm=512):
    M, D = x.shape
    tm = _tile(M, tm)

    def body(*refs):
        if dres is None:
            x_ref, g_ref, dy_ref, dx_ref, dg_ref = refs
        else:
            x_ref, g_ref, dy_ref, dr_ref, dx_ref, dg_ref = refs
        dx, dgp = _rms_bwd_rows(x_ref[...].astype(F32), g_ref[...], dy_ref[...].astype(F32))
        if dres is not None:
            dx = dx + dr_ref[...]
        dx_ref[...] = dx

        @pl.when(pl.program_id(0) == 0)
        def _():
            dg_ref[...] = jnp.zeros_like(dg_ref)

        dg_ref[...] += jnp.sum(dgp, axis=0, keepdims=True)

    row = pl.BlockSpec((tm, D), lambda i: (i, 0))
    vec = pl.BlockSpec((1, D), lambda i: (0, 0))
    in_specs = [row, vec, row] + ([] if dres is None else [row])
    args = (x, g.reshape(1, D), dy) + (() if dres is None else (dres,))
    return _pcall(
        body, name=name, dims=("arbitrary",), grid=(M // tm,), in_specs=in_specs, out_specs=[row, vec],
        out_shape=[jax.ShapeDtypeStruct((M, D), F32), jax.ShapeDtypeStruct((1, D), F32)],
    )(*args)


def _rope_chunk(t, c, s, half):
    lane = _lane(t.shape)
    swapped = jnp.where((lane % (2 * half)) < half, pltpu.roll(t, LANES - half, 1), pltpu.roll(t, half, 1))
    return t * c + swapped * s


def _rope_tables(positions):
    pos = positions.reshape(-1).astype(F32)[:, None]
    S = pos.shape[0]

    def cs(dh):
        inv_freq = ROPE_THETA ** (-jnp.arange(0, dh, 2, dtype=F32) / dh)
        ang = pos * inv_freq
        return jnp.cos(ang), jnp.sin(ang)

    c64, s64 = cs(HEAD_DIM)
    c32, s32 = cs(MLA_ROPE)
    z32, z64, z96 = (jnp.zeros((S, n), F32) for n in (32, 64, 96))
    return dict(
        c64=jnp.concatenate([c64, c64, c64, c64], 1), s64=jnp.concatenate([-s64, s64, -s64, s64], 1),
        ck=jnp.concatenate([c32, c32, z96], 1), sk=jnp.concatenate([-s32, s32, z96], 1),
        cm=jnp.concatenate([jnp.ones((S, 64), F32), c32, c32, z32], 1),
        sm=jnp.concatenate([z64, -s32, s32, z32], 1),
    )


def _attn_steps(mode, n_other, t_self, t_other):
    if mode == "band":
        assert t_self == t_other
        return 2
    return n_other


def _kv_block(mode, qi, kj):
    if mode == "band":
        return jnp.maximum(qi - 1 + kj, 0), (qi + kj) >= 1
    if mode == "causal":
        return jnp.minimum(kj, qi), kj <= qi
    return kj, None


def _q_block(mode, ki, qj, nq):
    if mode == "band":
        return jnp.minimum(ki + qj, nq - 1), (ki + qj) <= nq - 1
    if mode == "causal":
        return jnp.maximum(qj, ki), qj >= ki
    return qj, None


def _mask(mode, max_dist, qpos, kpos):
    d = qpos - kpos
    if mode == "band":
        return (d >= 0) & (d <= max_dist)
    if mode == "causal":
        return d >= 0
    return None


def _when(cond, fn):
    if cond is None:
        fn()
    else:
        pl.when(cond)(fn)


class _Attn:
    def __init__(self, *, T, Tk, G, nh, rep, dqk, dv, tq, tk, mode, scale, qcol, kcol, vcol, ocol, o_width,
                 max_dist=0):
        self.__dict__.update(locals())
        self.nkv = nh // rep
        assert T % tq == 0 and Tk % tk == 0 and nh <= LANES


def _attn_fwd(cfg, q, k, v, *, name, sink=None, out_dtype=F32):
    c = cfg
    nq, nk = c.T // c.tq, c.Tk // c.tk
    steps = _attn_steps(c.mode, nk, c.tq, c.tk)

    def body(*refs):
        if sink is None:
            q_ref, k_ref, v_ref, o_ref, lse_ref, m_scr, l_scr, acc = refs
        else:
            q_ref, k_ref, v_ref, sink_ref, o_ref, lse_ref, m_scr, l_scr, acc = refs
        qi, kj = pl.program_id(1), pl.program_id(2)
        kb, valid = _kv_block(c.mode, qi, kj)

        @pl.when(kj == 0)
        def _():
            if sink is None:
                m_scr[...] = jnp.full_like(m_scr, NEG_BIG)
                l_scr[...] = jnp.zeros_like(l_scr)
            else:
                m_scr[...] = jnp.broadcast_to(sink_ref[...], m_scr.shape)
                l_scr[...] = jnp.ones_like(l_scr)
            acc[...] = jnp.zeros_like(acc)

        def step():
            qpos = qi * c.tq + lax.broadcasted_iota(jnp.int32, (c.tq, c.tk), 0)
            kpos = kb * c.tk + lax.broadcasted_iota(jnp.int32, (c.tq, c.tk), 1)
            mask = _mask(c.mode, c.max_dist, qpos, kpos)
            for j in range(c.nh):
                g = j // c.rep
                s = _dot(q_ref[:, j * c.dqk:(j + 1) * c.dqk], k_ref[:, g * c.dqk:(g + 1) * c.dqk], NT) * c.scale
                if mask is not None:
                    s = jnp.where(mask, s, -jnp.inf)
                m_prev = m_scr[:, j:j + 1]
                m_new = jnp.maximum(m_prev, jnp.max(s, axis=1, keepdims=True))
                alpha = jnp.exp(m_prev - m_new)
                p = jnp.exp(s - m_new)
                l_scr[:, j:j + 1] = alpha * l_scr[:, j:j + 1] + jnp.sum(p, axis=1, keepdims=True)
                acc[:, j * c.dv:(j + 1) * c.dv] = (
                    alpha * acc[:, j * c.dv:(j + 1) * c.dv] + _dot(p, v_ref[:, g * c.dv:(g + 1) * c.dv], NN))
                m_scr[:, j:j + 1] = m_new

        _when(valid, step)

        @pl.when(kj == steps - 1)
        def _():
            for j in range(c.nh):
                o_ref[:, j * c.dv:(j + 1) * c.dv] = (
                    acc[:, j * c.dv:(j + 1) * c.dv] / l_scr[:, j:j + 1]).astype(o_ref.dtype)
            lane = _lane((c.tq, LANES))
            lse_ref[...] = jnp.where(lane < c.nh, m_scr[...] + jnp.log(jnp.maximum(l_scr[...], 1e-37)), 0.0)

    in_specs = [
        pl.BlockSpec((c.tq, c.nh * c.dqk), lambda g, i, j: (i, c.qcol(g))),
        pl.BlockSpec((c.tk, c.nkv * c.dqk), lambda g, i, j: (_kv_block(c.mode, i, j)[0], c.kcol(g))),
        pl.BlockSpec((c.tk, c.nkv * c.dv), lambda g, i, j: (_kv_block(c.mode, i, j)[0], c.vcol(g))),
    ]
    args = [q, k, v]
    if sink is not None:
        in_specs.append(pl.BlockSpec((1, LANES), lambda g, i, j: (0, 0)))
        args.append(sink)
    return _pcall(
        body, name=name, dims=("parallel", "parallel", "arbitrary"), grid=(c.G, nq, steps),
        in_specs=in_specs,
        out_specs=[pl.BlockSpec((c.tq, c.nh * c.dv), lambda g, i, j: (i, c.ocol(g))),
                   pl.BlockSpec((c.tq, LANES), lambda g, i, j: (i, g))],
        out_shape=[jax.ShapeDtypeStruct((c.T, c.o_width), out_dtype),
                   jax.ShapeDtypeStruct((c.T, LANES * c.G), F32)],
        scratch_shapes=[pltpu.VMEM((c.tq, LANES), F32), pltpu.VMEM((c.tq, LANES), F32),
                        pltpu.VMEM((c.tq, c.nh * c.dv), F32)],
    )(*args)


def _attn_delta(cfg, o, do, *, name, w=None, lse=None, sink=None, tm=512):
    c = cfg
    tm = _tile(c.T, tm)
    width = c.nh * c.dv

    def body(*refs):
        refs = list(refs)
        o_ref, do_ref = refs[:2]
        rest = refs[2:]
        w_ref = rest.pop(0) if w is not None else None
        lse_ref, sink_ref = (rest.pop(0), rest.pop(0)) if sink is not None else (None, None)
        d_ref = rest.pop(0)
        prod = o_ref[...].astype(F32) * do_ref[...].astype(F32)
        cols = [jnp.sum(prod[:, j * c.dv:(j + 1) * c.dv], axis=1, keepdims=True) for j in range(c.nh)]
        delta = _cols_to_lanes(cols, tm)
        if w is not None:
            delta = delta * w_ref[...]
        d_ref[...] = delta
        if sink is not None:
            ds_ref = rest.pop(0)

            @pl.when(pl.program_id(1) == 0)
            def _():
                ds_ref[...] = jnp.zeros_like(ds_ref)

            lane = _lane((tm, LANES))
            ps = jnp.where(lane < c.nh, jnp.exp(sink_ref[...] - lse_ref[...]), 0.0)
            ds_ref[...] -= jnp.sum(ps * delta, axis=0, keepdims=True)

    stat = pl.BlockSpec((tm, LANES), lambda g, i: (i, g))
    in_specs = [pl.BlockSpec((tm, width), lambda g, i: (i, c.ocol(g)))] * 2
    args = [o, do]
    out_specs, out_shape = [stat], [jax.ShapeDtypeStruct((c.T, LANES * c.G), F32)]
    if w is not None:
        in_specs.append(stat)
        args.append(w)
    if sink is not None:
        assert c.G == 1
        in_specs += [stat, pl.BlockSpec((1, LANES), lambda g, i: (0, 0))]
        args += [lse, sink]
        out_specs.append(pl.BlockSpec((1, LANES), lambda g, i: (0, 0)))
        out_shape.append(jax.ShapeDtypeStruct((1, LANES), F32))
    out = _pcall(
        body, name=name, dims=("arbitrary", "arbitrary"), grid=(c.G, c.T // tm),
        in_specs=in_specs, out_specs=out_specs, out_shape=out_shape,
    )(*args)
    return out if sink is not None else (out[0], None)


def _attn_dq(cfg, q, k, v, do, lse, delta, *, name, init=None, out_dtype=F32):
    c = cfg
    nq, nk = c.T // c.tq, c.Tk // c.tk
    steps = _attn_steps(c.mode, nk, c.tq, c.tk)
    qw = c.nh * c.dqk

    def body(*refs):
        if init is None:
            q_ref, k_ref, v_ref, do_ref, lse_ref, d_ref, dq_ref, acc = refs
        else:
            q_ref, k_ref, v_ref, do_ref, lse_ref, d_ref, init_ref, dq_ref, acc = refs
        qi, kj = pl.program_id(1), pl.program_id(2)
        kb, valid = _kv_block(c.mode, qi, kj)

        @pl.when(kj == 0)
        def _():
            acc[...] = jnp.zeros_like(acc) if init is None else init_ref[...].astype(F32)

        def step():
            qpos = qi * c.tq + lax.broadcasted_iota(jnp.int32, (c.tq, c.tk), 0)
            kpos = kb * c.tk + lax.broadcasted_iota(jnp.int32, (c.tq, c.tk), 1)
            mask = _mask(c.mode, c.max_dist, qpos, kpos)
            for j in range(c.nh):
                g = j // c.rep
                kh = k_ref[:, g * c.dqk:(g + 1) * c.dqk]
                s = _dot(q_ref[:, j * c.dqk:(j + 1) * c.dqk], kh, NT) * c.scale
                if mask is not None:
                    s = jnp.where(mask, s, -jnp.inf)
                p = jnp.exp(s - lse_ref[:, j:j + 1])
                dp = _dot(do_ref[:, j * c.dv:(j + 1) * c.dv], v_ref[:, g * c.dv:(g + 1) * c.dv], NT)
                ds = p * (dp - d_ref[:, j:j + 1]) * c.scale
                acc[:, j * c.dqk:(j + 1) * c.dqk] += _dot(ds, kh, NN)

        _when(valid, step)

        @pl.when(kj == steps - 1)
        def _():
            dq_ref[...] = acc[...].astype(dq_ref.dtype)

    kvb = lambda i, j: _kv_block(c.mode, i, j)[0]
    qspec = pl.BlockSpec((c.tq, qw), lambda g, i, j: (i, c.qcol(g)))
    stat = pl.BlockSpec((c.tq, LANES), lambda g, i, j: (i, g))
    in_specs = [
        qspec,
        pl.BlockSpec((c.tk, c.nkv * c.dqk), lambda g, i, j: (kvb(i, j), c.kcol(g))),
        pl.BlockSpec((c.tk, c.nkv * c.dv), lambda g, i, j: (kvb(i, j), c.vcol(g))),
        pl.BlockSpec((c.tq, c.nh * c.dv), lambda g, i, j: (i, c.ocol(g))),
        stat, stat,
    ]
    args = [q, k, v, do, lse, delta]
    dq_spec = pl.BlockSpec((c.tq, qw), lambda g, i, j: (i, g))
    if init is not None:
        in_specs.append(dq_spec)
        args.append(init)
    return _pcall(
        body, name=name, dims=("parallel", "parallel", "arbitrary"), grid=(c.G, nq, steps),
        in_specs=in_specs, out_specs=dq_spec,
        out_shape=jax.ShapeDtypeStruct((c.T, c.G * qw), out_dtype),
        scratch_shapes=[pltpu.VMEM((c.tq, qw), F32)],
    )(*args)


def _attn_dkv(cfg, q, k, v, do, lse, delta, *, name, init=None, out_dtype=F32):
    c = cfg
    nq, nk = c.T // c.tq, c.Tk // c.tk
    steps = _attn_steps(c.mode, nq, c.tk, c.tq)
    kw, vw = c.nkv * c.dqk, c.nkv * c.dv

    def body(*refs):
        if init is None:
            q_ref, k_ref, v_ref, do_ref, lse_ref, d_ref, dk_ref, dv_ref, dk_acc, dv_acc = refs
        else:
            q_ref, k_ref, v_ref, do_ref, lse_ref, d_ref, ik_ref, iv_ref, dk_ref, dv_ref, dk_acc, dv_acc = refs
        ki, qj = pl.program_id(1), pl.program_id(2)
        qb, valid = _q_block(c.mode, ki, qj, nq)

        @pl.when(qj == 0)
        def _():
            dk_acc[...] = jnp.zeros_like(dk_acc) if init is None else ik_ref[...].astype(F32)
            dv_acc[...] = jnp.zeros_like(dv_acc) if init is None else iv_ref[...].astype(F32)

        def step():
            kpos = ki * c.tk + lax.broadcasted_iota(jnp.int32, (c.tk, c.tq), 0)
            qpos = qb * c.tq + lax.broadcasted_iota(jnp.int32, (c.tk, c.tq), 1)
            mask = _mask(c.mode, c.max_dist, qpos, kpos)
            lse_t = lse_ref[...].T
            d_t = d_ref[...].T
            for j in range(c.nh):
                g = j // c.rep
                qh = q_ref[:, j * c.dqk:(j + 1) * c.dqk]
                doh = do_ref[:, j * c.dv:(j + 1) * c.dv]
                s_t = _dot(k_ref[:, g * c.dqk:(g + 1) * c.dqk], qh, NT) * c.scale
                if mask is not None:
                    s_t = jnp.where(mask, s_t, -jnp.inf)
                p_t = jnp.exp(s_t - lse_t[j:j + 1, :])
                dv_acc[:, g * c.dv:(g + 1) * c.dv] += _dot(p_t, doh, NN)
                dp_t = _dot(v_ref[:, g * c.dv:(g + 1) * c.dv], doh, NT)
                ds_t = p_t * (dp_t - d_t[j:j + 1, :]) * c.scale
                dk_acc[:, g * c.dqk:(g + 1) * c.dqk] += _dot(ds_t, qh, NN)

        _when(valid, step)

        @pl.when(qj == steps - 1)
        def _():
            dk_ref[...] = dk_acc[...].astype(dk_ref.dtype)
            dv_ref[...] = dv_acc[...].astype(dv_ref.dtype)

    qbi = lambda i, j: _q_block(c.mode, i, j, nq)[0]
    stat = pl.BlockSpec((c.tq, LANES), lambda g, i, j: (qbi(i, j), g))
    in_specs = [
        pl.BlockSpec((c.tq, c.nh * c.dqk), lambda g, i, j: (qbi(i, j), c.qcol(g))),
        pl.BlockSpec((c.tk, kw), lambda g, i, j: (i, c.kcol(g))),
        pl.BlockSpec((c.tk, vw), lambda g, i, j: (i, c.vcol(g))),
        pl.BlockSpec((c.tq, c.nh * c.dv), lambda g, i, j: (qbi(i, j), c.ocol(g))),
        stat, stat,
    ]
    args = [q, k, v, do, lse, delta]
    dk_spec = pl.BlockSpec((c.tk, kw), lambda g, i, j: (i, g))
    dv_spec = pl.BlockSpec((c.tk, vw), lambda g, i, j: (i, g))
    if init is not None:
        in_specs += [dk_spec, dv_spec]
        args += list(init)
    return _pcall(
        body, name=name, dims=("parallel", "parallel", "arbitrary"), grid=(c.G, nk, steps),
        in_specs=in_specs, out_specs=[dk_spec, dv_spec],
        out_shape=[jax.ShapeDtypeStruct((c.Tk, c.G * kw), out_dtype),
                   jax.ShapeDtypeStruct((c.Tk, c.G * vw), out_dtype)],
        scratch_shapes=[pltpu.VMEM((c.tk, kw), F32), pltpu.VMEM((c.tk, vw), F32)],
    )(*args)


TN = (((0,), (0,)), ((), ()))


def _band_mask(c, i):
    row = lax.broadcasted_iota(jnp.int32, (BLOCK, 2 * BLOCK), 0)
    col = lax.broadcasted_iota(jnp.int32, (BLOCK, 2 * BLOCK), 1)
    d = BLOCK + row - col
    return (d >= 0) & (d <= c.max_dist) & ((col >= BLOCK) | (i > 0))


def _band_fwd(cfg, q, k, v, *, name, sink=None, out_dtype=F32):
    c = cfg
    assert c.mode == "band" and c.tq == c.tk == BLOCK and c.T == c.Tk
    nq = c.T // BLOCK

    def body(*refs):
        if sink is None:
            q_ref, kp_ref, kc_ref, vp_ref, vc_ref, o_ref, lse_ref = refs
        else:
            q_ref, kp_ref, kc_ref, vp_ref, vc_ref, sink_ref, o_ref, lse_ref = refs
        mask = _band_mask(c, pl.program_id(1))
        k2 = jnp.concatenate([kp_ref[...], kc_ref[...]], axis=0)
        v2 = jnp.concatenate([vp_ref[...], vc_ref[...]], axis=0)
        lses = []
        for j in range(c.nh):
            g = j // c.rep
            s = _dot(q_ref[:, j * c.dqk:(j + 1) * c.dqk], k2[:, g * c.dqk:(g + 1) * c.dqk], NT) * c.scale
            s = jnp.where(mask, s, -jnp.inf)
            m = jnp.max(s, axis=1, keepdims=True)
            if sink is not None:
                sk = sink_ref[:, j:j + 1]
                m = jnp.maximum(m, sk)
            p = jnp.exp(s - m)
            l = jnp.sum(p, axis=1, keepdims=True)
            if sink is not None:
                l = l + jnp.exp(sk - m)
            o_ref[:, j * c.dv:(j + 1) * c.dv] = (_dot(p, v2[:, g * c.dv:(g + 1) * c.dv], NN) / l).astype(o_ref.dtype)
            lses.append(m + jnp.log(l))
        lse_ref[...] = _cols_to_lanes(lses, BLOCK)

    prev = lambda i: jnp.maximum(i - 1, 0)
    kw, vw = c.nkv * c.dqk, c.nkv * c.dv
    in_specs = [
        pl.BlockSpec((BLOCK, c.nh * c.dqk), lambda g, i: (i, c.qcol(g))),
        pl.BlockSpec((BLOCK, kw), lambda g, i: (prev(i), c.kcol(g))),
        pl.BlockSpec((BLOCK, kw), lambda g, i: (i, c.kcol(g))),
        pl.BlockSpec((BLOCK, vw), lambda g, i: (prev(i), c.vcol(g))),
        pl.BlockSpec((BLOCK, vw), lambda g, i: (i, c.vcol(g))),
    ]
    args = [q, k, k, v, v]
    if sink is not None:
        in_specs.append(pl.BlockSpec((1, LANES), lambda g, i: (0, 0)))
        args.append(sink)
    return _pcall(
        body, name=name, dims=("parallel", "parallel"), grid=(c.G, nq), in_specs=in_specs,
        out_specs=[pl.BlockSpec((BLOCK, c.nh * c.dv), lambda g, i: (i, c.ocol(g))),
                   pl.BlockSpec((BLOCK, LANES), lambda g, i: (i, g))],
        out_shape=[jax.ShapeDtypeStruct((c.T, c.o_width), out_dtype),
                   jax.ShapeDtypeStruct((c.T, LANES * c.G), F32)],
    )(*args)


def _band_bwd(cfg, q, k, v, do, lse, delta, *, name, init=None):
    c = cfg
    assert c.mode == "band" and c.tq == c.tk == BLOCK and c.T == c.Tk
    nq = c.T // BLOCK
    qw, kw, vw = c.nh * c.dqk, c.nkv * c.dqk, c.nkv * c.dv

    def body(*refs):
        refs = list(refs)
        q_ref, kp_ref, kc_ref, vp_ref, vc_ref, do_ref, lse_ref, d_ref = refs[:8]
        iq_ref, ik_ref, iv_ref = refs[8:11] if init is not None else (None, None, None)
        dq_ref, dk_ref, dv_ref, dk_c, dv_c = refs[-5:]
        n = pl.program_id(1)

        def plus(val, ref, sl):
            return val if ref is None else val + ref[:, sl]

        @pl.when(n == 0)
        def _():
            dk_c[...] = jnp.zeros_like(dk_c)
            dv_c[...] = jnp.zeros_like(dv_c)

        @pl.when(n < nq)
        def _():
            mask = _band_mask(c, n)
            k2 = jnp.concatenate([kp_ref[...], kc_ref[...]], axis=0)
            v2 = jnp.concatenate([vp_ref[...], vc_ref[...]], axis=0)
            dk2, dv2 = [None] * c.nkv, [None] * c.nkv
            for j in range(c.nh):
                g = j // c.rep
                qs, os_ = slice(j * c.dqk, (j + 1) * c.dqk), slice(j * c.dv, (j + 1) * c.dv)
                qh, doh = q_ref[:, qs], do_ref[:, os_]
                kh, vh = k2[:, g * c.dqk:(g + 1) * c.dqk], v2[:, g * c.dv:(g + 1) * c.dv]
                s = jnp.where(mask, _dot(qh, kh, NT) * c.scale, -jnp.inf)
                p = jnp.exp(s - lse_ref[:, j:j + 1])
                ds = p * (_dot(doh, vh, NT) - d_ref[:, j:j + 1]) * c.scale
                dq_ref[:, qs] = plus(_dot(ds, kh, NN), iq_ref, qs)
                dvh, dkh = _dot(p, doh, TN), _dot(ds, qh, TN)
                dv2[g] = dvh if dv2[g] is None else dv2[g] + dvh
                dk2[g] = dkh if dk2[g] is None else dk2[g] + dkh
            for g in range(c.nkv):
                ks, vs = slice(g * c.dqk, (g + 1) * c.dqk), slice(g * c.dv, (g + 1) * c.dv)
                dk_ref[:, ks] = plus(dk_c[:, ks] + dk2[g][:BLOCK], ik_ref, ks)
                dv_ref[:, vs] = plus(dv_c[:, vs] + dv2[g][:BLOCK], iv_ref, vs)
                dk_c[:, ks] = dk2[g][BLOCK:]
                dv_c[:, vs] = dv2[g][BLOCK:]

        @pl.when(n == nq)
        def _():
            dk_ref[...] = plus(dk_c[...], ik_ref, slice(None))
            dv_ref[...] = plus(dv_c[...], iv_ref, slice(None))

    cur = lambda n: jnp.minimum(n, nq - 1)
    prev = lambda n: jnp.maximum(cur(n) - 1, 0)
    out_blk = lambda n: jnp.maximum(n - 1, 0)
    stat = pl.BlockSpec((BLOCK, LANES), lambda g, n: (cur(n), g))
    dq_spec = pl.BlockSpec((BLOCK, qw), lambda g, n: (cur(n), g))
    dk_spec = pl.BlockSpec((BLOCK, kw), lambda g, n: (out_blk(n), g))
    dv_spec = pl.BlockSpec((BLOCK, vw), lambda g, n: (out_blk(n), g))
    in_specs = [
        pl.BlockSpec((BLOCK, qw), lambda g, n: (cur(n), c.qcol(g))),
        pl.BlockSpec((BLOCK, kw), lambda g, n: (prev(n), c.kcol(g))),
        pl.BlockSpec((BLOCK, kw), lambda g, n: (cur(n), c.kcol(g))),
        pl.BlockSpec((BLOCK, vw), lambda g, n: (prev(n), c.vcol(g))),
        pl.BlockSpec((BLOCK, vw), lambda g, n: (cur(n), c.vcol(g))),
        pl.BlockSpec((BLOCK, c.nh * c.dv), lambda g, n: (cur(n), c.ocol(g))),
        stat, stat,
    ]
    args = [q, k, k, v, v, do, lse, delta]
    if init is not None:
        in_specs += [dq_spec, dk_spec, dv_spec]
        args += list(init)
    return _pcall(
        body, name=name, dims=("parallel", "arbitrary"), grid=(c.G, nq + 1), in_specs=in_specs,
        out_specs=[dq_spec, dk_spec, dv_spec],
        out_shape=[_sds((c.T, c.G * qw)), _sds((c.T, c.G * kw)), _sds((c.T, c.G * vw))],
        scratch_shapes=[pltpu.VMEM((BLOCK, kw), F32), pltpu.VMEM((BLOCK, vw), F32)],
    )(*args)


def _causal_pairs(n, kv_major):
    pairs =[(i, j) for j in range(n) for i in range(j, n)] if kv_major else [(i, j) for i in range(n) for j in range(i + 1)]
    return jnp.asarray(np.array([p[0] for p in pairs], np.int32)), jnp.asarray(np.array([p[1] for p in pairs], np.int32))


def _causal_mask(t):
    return lax.broadcasted_iota(jnp.int32, (t, t), 0) >= lax.broadcasted_iota(jnp.int32, (t, t), 1)


def _causal_fwd(cfg, q, k, v, *, name, out_dtype=F32):
    c = cfg
    assert c.mode == "causal" and c.tq == c.tk and c.T == c.Tk
    t, n = c.tq, c.T // c.tq
    qi_tab, kj_tab = _causal_pairs(n, kv_major=False)

    def body(qi_ref, kj_ref, q_ref, k_ref, v_ref, o_ref, lse_ref, m_scr, l_scr, acc):
        pair = pl.program_id(1)
        qi, kj = qi_ref[pair], kj_ref[pair]

        @pl.when(kj == 0)
        def _():
            m_scr[...] = jnp.full_like(m_scr, NEG_BIG)
            l_scr[...] = jnp.zeros_like(l_scr)
            acc[...] = jnp.zeros_like(acc)

        def step(diagonal):
            mask = _causal_mask(t) if diagonal else None
            for j in range(c.nh):
                g = j // c.rep
                s = _dot(q_ref[:, j * c.dqk:(j + 1) * c.dqk], k_ref[:, g * c.dqk:(g + 1) * c.dqk], NT) * c.scale
                if diagonal:
                    s = jnp.where(mask, s, -jnp.inf)
                m_prev = m_scr[j]
                m_new = jnp.maximum(m_prev, jnp.max(s, axis=1, keepdims=True))
                alpha = jnp.exp(m_prev - m_new)
                p = jnp.exp(s - m_new)
                l_scr[j] = alpha * l_scr[j] + jnp.sum(p, axis=1, keepdims=True)
                acc[j] = alpha * acc[j] + _dot(p, v_ref[:, g * c.dv:(g + 1) * c.dv], NN)
                m_scr[j] = m_new

        pl.when(kj == qi)(lambda: step(True))
        pl.when(kj != qi)(lambda: step(False))

        @pl.when(kj == qi)
        def _():
            lses = []
            for j in range(c.nh):
                o_ref[:, j * c.dv:(j + 1) * c.dv] = (acc[j] / l_scr[j]).astype(o_ref.dtype)
                lses.append(m_scr[j] + jnp.log(l_scr[j]))
            lse_ref[...] = _cols_to_lanes(lses, t)

    grid_spec = pltpu.PrefetchScalarGridSpec(
        num_scalar_prefetch=2, grid=(c.G, int(qi_tab.shape[0])),
        in_specs=[pl.BlockSpec((t, c.nh * c.dqk), lambda g, p, qi, kj: (qi[p], c.qcol(g))),
                  pl.BlockSpec((t, c.nkv * c.dqk), lambda g, p, qi, kj: (kj[p], c.kcol(g))),
                  pl.BlockSpec((t, c.nkv * c.dv), lambda g, p, qi, kj: (kj[p], c.vcol(g)))],
        out_specs=[pl.BlockSpec((t, c.nh * c.dv), lambda g, p, qi, kj: (qi[p], c.ocol(g))),
                   pl.BlockSpec((t, LANES), lambda g, p, qi, kj: (qi[p], g))],
        scratch_shapes=[pltpu.VMEM((c.nh, t, 1), F32), pltpu.VMEM((c.nh, t, 1), F32), pltpu.VMEM((c.nh, t, c.dv), F32)])
    return _pcall(
        body, name=name, dims=("parallel", "arbitrary"), grid_spec=grid_spec,
        out_shape=[jax.ShapeDtypeStruct((c.T, c.o_width), out_dtype), jax.ShapeDtypeStruct((c.T, LANES * c.G), F32)],
    )(qi_tab, kj_tab, q, k, v)


def _causal_bwd(cfg, q, k, v, do, lse, delta, *, name):
    c = cfg
    assert c.mode == "causal" and c.tq == c.tk and c.T == c.Tk
    t, n = c.tq, c.T // c.tq
    qw, kw, vw = c.nh * c.dqk, c.nkv * c.dqk, c.nkv * c.dv
    qi_tab, kj_tab = _causal_pairs(n, kv_major=True)

    def body(qi_ref, kj_ref, q_ref, k_ref, v_ref, do_ref, lse_ref, d_ref, dq_ref, dk_ref, dv_ref, dk_acc, dv_acc):
        pair = pl.program_id(1)
        qi, kj = qi_ref[pair], kj_ref[pair]

        @pl.when(pair == 0)
        def _():
            dq_ref[...] = jnp.zeros_like(dq_ref)

        @pl.when(qi == kj)
        def _():
            dk_acc[...] = jnp.zeros_like(dk_acc)
            dv_acc[...] = jnp.zeros_like(dv_acc)

        rows = pl.ds(pl.multiple_of(qi * t, t), t)

        def step(diagonal):
            mask = _causal_mask(t) if diagonal else None
            for j in range(c.nh):
                g = j // c.rep
                qs, ks, vs = (slice(j * c.dqk, (j + 1) * c.dqk), slice(g * c.dqk, (g + 1) * c.dqk),
                              slice(g * c.dv, (g + 1) * c.dv))
                qh, doh, kh = q_ref[:, qs], do_ref[:, j * c.dv:(j + 1) * c.dv], k_ref[:, ks]
                s = _dot(qh, kh, NT) * c.scale
                if diagonal:
                    s = jnp.where(mask, s, -jnp.inf)
                p = jnp.exp(s - lse_ref[:, j:j + 1])
                ds = p * (_dot(doh, v_ref[:, vs], NT) - d_ref[:, j:j + 1]) * c.scale
                dq_ref[rows, qs] += _dot(ds, kh, NN)
                dv_acc[:, vs] += _dot(p, doh, TN)
                dk_acc[:, ks] += _dot(ds, qh, TN)

        pl.when(qi == kj)(lambda: step(True))
        pl.when(qi != kj)(lambda: step(False))

        @pl.when(qi == n - 1)
        def _():
            dk_ref[...] = dk_acc[...]
            dv_ref[...] = dv_acc[...]

    stat = pl.BlockSpec((t, LANES), lambda g, p, qi, kj: (qi[p], g))
    grid_spec = pltpu.PrefetchScalarGridSpec(
        num_scalar_prefetch=2, grid=(c.G, int(qi_tab.shape[0])),
        in_specs=[pl.BlockSpec((t, qw), lambda g, p, qi, kj: (qi[p], c.qcol(g))),
                  pl.BlockSpec((t, kw), lambda g, p, qi, kj: (kj[p], c.kcol(g))),
                  pl.BlockSpec((t, vw), lambda g, p, qi, kj: (kj[p], c.vcol(g))),
                  pl.BlockSpec((t, c.nh * c.dv), lambda g, p, qi, kj: (qi[p], c.ocol(g))),
                  stat, stat],
        out_specs=[pl.BlockSpec((c.T, qw), lambda g, p, qi, kj: (0, g)),
                   pl.BlockSpec((t, kw), lambda g, p, qi, kj: (kj[p], g)),
                   pl.BlockSpec((t, vw), lambda g, p, qi, kj: (kj[p], g))],
        scratch_shapes=[pltpu.VMEM((t, kw), F32), pltpu.VMEM((t, vw), F32)])
    return _pcall(
        body, name=name, dims=("parallel", "arbitrary"), grid_spec=grid_spec,
        out_shape=[_sds((c.T, c.G * qw)), _sds((c.T, c.G * kw)), _sds((c.T, c.G * vw))],
    )(qi_tab, kj_tab, q, k, v, do, lse, delta)


def _rowwise(body, ins, outs, *, name, rows, tm=512, accs=(), scratch=()):
    tm = _tile(rows, tm)

    def spec(a):
        if a.shape[0] == 1:
            return pl.BlockSpec((1, a.shape[1]), lambda i: (0, 0))
        d = rows // a.shape[0]
        assert d * a.shape[0] == rows and tm % d == 0
        return pl.BlockSpec((tm // d, a.shape[1]), lambda i: (i, 0))

    return _pcall(
        functools.partial(body, tm), name=name, dims=("arbitrary" if accs else "parallel",), grid=(rows // tm,),
        in_specs=[spec(a) for a in ins], out_specs=[spec(a) for a in outs], out_shape=list(outs),
        scratch_shapes=list(scratch),
    )(*ins)


def _sds(shape, dtype=F32):
    return jax.ShapeDtypeStruct(shape, dtype)


def _acc_rows(ref, val):
    @pl.when(pl.program_id(0) == 0)
    def _():
        ref[...] = jnp.zeros_like(ref)

    ref[...] += jnp.sum(val, axis=0, keepdims=True)


Z_QA, Z_KA, Z_VA, Z_CQ, Z_CKV, Z_KR, Z_END = 0, 512, 640, 768, 1152, 1408, 1536


def _l0_prep(z, tabs, q_norm, kv_norm, *, name):
    S = z.shape[0]

    def body(tm, z_ref, c64, s64, ck, sk, gq, gkv, qa_o, ka_o, va_o, cq_o, ckv_o, kr_o):
        for i in range(4):
            sl = slice(Z_QA + i * LANES, Z_QA + (i + 1) * LANES)
            qa_o[:, i * LANES:(i + 1) * LANES] = _rope_chunk(z_ref[:, sl], c64[...], s64[...], 32).astype(qa_o.dtype)
        ka_o[...] = _rope_chunk(z_ref[:, Z_KA:Z_VA], c64[...], s64[...], 32).astype(ka_o.dtype)
        va_o[...] = z_ref[:, Z_VA:Z_CQ].astype(va_o.dtype)
        cq_o[...] = (_rms_parts(z_ref[:, Z_CQ:Z_CKV])[0] * gq[...]).astype(cq_o.dtype)
        ckv_o[...] = (_rms_parts(z_ref[:, Z_CKV:Z_KR])[0] * gkv[...]).astype(ckv_o.dtype)
        kr_o[...] = _rope_chunk(z_ref[:, Z_KR:Z_END], ck[...], sk[...], 16)

    outs = [_sds((S, 512), MXU_DTYPE), _sds((S, 128), MXU_DTYPE), _sds((S, 128), MXU_DTYPE),
            _sds((S, MLA_Q_RANK), MXU_DTYPE), _sds((S, MLA_KV_RANK), MXU_DTYPE), _sds((S, LANES))]
    ins = [z, tabs["c64"], tabs["s64"], tabs["ck"], tabs["sk"], q_norm.reshape(1, -1), kv_norm.reshape(1, -1)]
    return _rowwise(body, ins, outs, name=name, rows=S)


def _l0_prep_bwd(z, tabs, q_norm, kv_norm, dqa, dka, dva, dcq, dckv, dkr, *, name):
    S = z.shape[0]

    def body(tm, z_ref, c64, s64, ck, sk, gq, gkv, dqa_r, dka_r, dva_r, dcq_r, dckv_r, dkr_r, dz_o, dgq_o, dgkv_o):
        for i in range(4):
            sl = slice(i * LANES, (i + 1) * LANES)
            dz_o[:, sl] = _rope_chunk(dqa_r[:, sl].astype(F32), c64[...], -s64[...], 32).astype(dz_o.dtype)
        dz_o[:, Z_KA:Z_VA] = _rope_chunk(dka_r[...].astype(F32), c64[...], -s64[...], 32).astype(dz_o.dtype)
        dz_o[:, Z_VA:Z_CQ] = dva_r[...].astype(dz_o.dtype)
        dx, dgp = _rms_bwd_rows(z_ref[:, Z_CQ:Z_CKV], gq[...], dcq_r[...].astype(F32))
        dz_o[:, Z_CQ:Z_CKV] = dx.astype(dz_o.dtype)
        _acc_rows(dgq_o, dgp)
        dx, dgp = _rms_bwd_rows(z_ref[:, Z_CKV:Z_KR], gkv[...], dckv_r[...].astype(F32))
        dz_o[:, Z_CKV:Z_KR] = dx.astype(dz_o.dtype)
        _acc_rows(dgkv_o, dgp)
        dz_o[:, Z_KR:Z_END] = _rope_chunk(dkr_r[...], ck[...], -sk[...], 16).astype(dz_o.dtype)

    outs = [_sds((S, Z_END), MXU_DTYPE), _sds((1, MLA_Q_RANK)), _sds((1, MLA_KV_RANK))]
    ins = [z, tabs["c64"], tabs["s64"], tabs["ck"], tabs["sk"], q_norm.reshape(1, -1), kv_norm.reshape(1, -1),
           dqa, dka, dva, dcq, dckv, dkr]
    return _rowwise(body, ins, outs, name=name, rows=S, accs=(1, 2))


def _mla_prep(qb, kvb, kr, tabs, *, name):
    S = qb.shape[0]

    def body(tm, qb_r, kvb_r, kr_r, cm, sm, q_o, k_o, v_o):
        lane = _lane((tm, LANES))
        kr_at_64 = pltpu.roll(kr_r[...], 64, 1)
        for h in range(MLA_HEADS):
            sl = slice(h * LANES, (h + 1) * LANES)
            q_o[:, sl] = _rope_chunk(qb_r[:, sl], cm[...], sm[...], 16).astype(q_o.dtype)
            k_o[:, sl] = jnp.where(lane < 64, kvb_r[:, sl], kr_at_64).astype(k_o.dtype)
        for p in range(MLA_HEADS // 2):
            even = pltpu.roll(kvb_r[:, (2 * p) * LANES:(2 * p + 1) * LANES], 64, 1)
            odd = kvb_r[:, (2 * p + 1) * LANES:(2 * p + 2) * LANES]
            v_o[:, p * LANES:(p + 1) * LANES] = jnp.where(lane < 64, even, odd).astype(v_o.dtype)

    outs = [_sds((S, 1024), MXU_DTYPE), _sds((S, 1024), MXU_DTYPE), _sds((S, 512), MXU_DTYPE)]
    return _rowwise(body, [qb, kvb, kr, tabs["cm"], tabs["sm"]], outs, name=name, rows=S)


def _mla_prep_bwd(dq, dk, dv, tabs, *, name):
    S = dq.shape[0]

    def body(tm, dq_r, dk_r, dv_r, cm, sm, dqb_o, dkvb_o, dkr_o):
        lane = _lane((tm, LANES))
        dkr = jnp.zeros((tm, LANES), F32)
        for h in range(MLA_HEADS):
            sl = slice(h * LANES, (h + 1) * LANES)
            dqb_o[:, sl] = _rope_chunk(dq_r[:, sl].astype(F32), cm[...], -sm[...], 16).astype(dqb_o.dtype)
            dkh = dk_r[:, sl].astype(F32)
            dvp = dv_r[:, (h // 2) * LANES:(h // 2 + 1) * LANES].astype(F32)
            dvh = pltpu.roll(dvp, 64, 1) if h % 2 == 0 else dvp
            dkvb_o[:, sl] = jnp.where(lane < 64, dkh, dvh).astype(dkvb_o.dtype)
            dkr = dkr + pltpu.roll(dkh, 64, 1)
        dkr_o[...] = jnp.where(lane < MLA_ROPE, dkr, 0.0)

    outs = [_sds((S, 1024), MXU_DTYPE), _sds((S, 1024), MXU_DTYPE), _sds((S, LANES))]
    return _rowwise(body, [dq, dk, dv, tabs["cm"], tabs["sm"]], outs, name=name, rows=S)


DILATIONS = tuple(d for _, d in DIL_PATTERNS)
QKV_CHUNKS = 8


def _to_branch(nat, c0, chunks, out_ref, d, rows):
    width = chunks * LANES
    for r in range(d):
        tok = pl.ds(r, rows // d, stride=d) if d > 1 else slice(None)
        for c in range(chunks):
            out_ref[:, r * width + c * LANES:r * width + (c + 1) * LANES] = nat[c0 + c, tok, :].astype(out_ref.dtype)


def _from_branch(in_ref, nat, c0, chunks, d, rows, add=False):
    width = chunks * LANES
    for r in range(d):
        tok = pl.ds(r, rows // d, stride=d) if d > 1 else slice(None)
        for c in range(chunks):
            val = in_ref[:, r * width + c * LANES:r * width + (c + 1) * LANES].astype(F32)
            nat[c0 + c, tok, :] = nat[c0 + c, tok, :] + val if add else val


def _branch_sds(S, width, d, dtype):
    return _sds((S // d, d * width), dtype)


def _l1_prep(qkv, tabs, *, name):
    S = qkv.shape[0]

    def body(tm, x_r, c64, s64, *rest):
        outs, nat = rest[:-1], rest[-1]
        for i in range(QKV_CHUNKS):
            sl = slice(i * LANES, (i + 1) * LANES)
            nat[i] = _rope_chunk(x_r[:, sl], c64[...], s64[...], 32)
            nat[QKV_CHUNKS + i] = _rope_chunk(x_r[:, 1024 + i * LANES:1024 + (i + 1) * LANES], c64[...], s64[...], 32)
            nat[2 * QKV_CHUNKS + i] = x_r[:, 2048 + i * LANES:2048 + (i + 1) * LANES]
        for b, d in enumerate(DILATIONS):
            for t in range(3):
                _to_branch(nat, t * QKV_CHUNKS, QKV_CHUNKS, outs[3 * b + t], d, tm)

    outs = [_branch_sds(S, 1024, d, MXU_DTYPE) for d in DILATIONS for _ in range(3)]
    got = _rowwise(body, [qkv, tabs["c64"], tabs["s64"]], outs, name=name, rows=S,
                   scratch=[pltpu.VMEM((3 * QKV_CHUNKS, _tile(S, 512), LANES), F32)])
    return {d: tuple(got[3 * b:3 * b + 3]) for b, d in enumerate(DILATIONS)}


def _l1_prep_bwd(grads, tabs, *, name):
    S = grads[1][0].shape[0]

    def body(tm, *rest):
        ins, (c64, s64, o, nat) = rest[:9], rest[9:]
        for b, d in enumerate(DILATIONS):
            for t in range(3):
                _from_branch(ins[3 * b + t], nat, t * QKV_CHUNKS, QKV_CHUNKS, d, tm, add=b > 0)
        for i in range(QKV_CHUNKS):
            sl = slice(i * LANES, (i + 1) * LANES)
            o[:, sl] = _rope_chunk(nat[i], c64[...], -s64[...], 32).astype(o.dtype)
            o[:, 1024 + i * LANES:1024 + (i + 1) * LANES] = _rope_chunk(
                nat[QKV_CHUNKS + i], c64[...], -s64[...], 32).astype(o.dtype)
            o[:, 2048 + i * LANES:2048 + (i + 1) * LANES] = nat[2 * QKV_CHUNKS + i].astype(o.dtype)

    ins = [g for d in DILATIONS for g in grads[d]] + [tabs["c64"], tabs["s64"]]
    return _rowwise(body, ins, [_sds((S, 3072), MXU_DTYPE)], name=name, rows=S, tm=256,
                    scratch=[pltpu.VMEM((3 * QKV_CHUNKS, _tile(S, 256), LANES), F32)])[0]


def _sigmoid(x):
    return 1.0 / (1.0 + jnp.exp(-x))


def _swiglu(gate, up, *, name):
    def body(tm, g_r, u_r, o):
        g = g_r[...]
        o[...] = (g * _sigmoid(g) * u_r[...]).astype(o.dtype)

    return _rowwise(body, [gate, up], [_sds(gate.shape, MXU_DTYPE)], name=name, rows=gate.shape[0], tm=256)[0]


def _swiglu_bwd(gate, up, dact, *, name):
    def body(tm, g_r, u_r, d_r, dg_o, du_o):
        g, d = g_r[...], d_r[...]
        sg = _sigmoid(g)
        dg_o[...] = (d * u_r[...] * (sg * (1.0 + g * (1.0 - sg)))).astype(dg_o.dtype)
        du_o[...] = (d * g * sg).astype(du_o.dtype)

    outs = [_sds(gate.shape, MXU_DTYPE)] * 2
    return _rowwise(body, [gate, up, dact], outs, name=name, rows=gate.shape[0], tm=256)


def _head_pair_weights(w, c, rows):
    return jnp.where(_lane((rows, LANES)) < HEAD_DIM, w[:, 2 * c:2 * c + 1], w[:, 2 * c + 1:2 * c + 2])


def _merge(outs_by_d, lses_by_d, *, name):
    S = outs_by_d[1].shape[0]
    far = DILATIONS[1:]

    def body(tm, o1, o4, o16, l1, l4, l16, o_o, w1_o, w4_o, w16_o, nat_o, nat_l):
        for b, (o_r, l_r, d) in enumerate(zip((o4, o16), (l4, l16), far)):
            _from_branch(o_r, nat_o, b * QKV_CHUNKS, QKV_CHUNKS, d, tm)
            _from_branch(l_r, nat_l, b, 1, d, tm)
        ls = [l1[...], nat_l[0], nat_l[1]]
        m = jnp.maximum(jnp.maximum(ls[0], ls[1]), ls[2])
        es = [jnp.exp(l - m) for l in ls]
        tot = es[0] + es[1] + es[2]
        ws = [e / tot for e in es]
        for w_o, w in zip((w1_o, w4_o, w16_o), ws):
            w_o[...] = w
        for c in range(QKV_CHUNKS):
            sl = slice(c * LANES, (c + 1) * LANES)
            parts = (o1[:, sl], nat_o[c], nat_o[QKV_CHUNKS + c])
            o_o[:, sl] = sum(_head_pair_weights(w, c, tm) * part for w, part in zip(ws, parts))

    ins = [outs_by_d[d] for d in DILATIONS] + [lses_by_d[d] for d in DILATIONS]
    outs = [_sds((S, 1024))] + [_sds((S, LANES))] * 3
    rows = _tile(S, 256)
    return _rowwise(body, ins, outs, name=name, rows=S, tm=256,
                    scratch=[pltpu.VMEM((2 * QKV_CHUNKS, rows, LANES), F32), pltpu.VMEM((2, rows, LANES), F32)])


def _merge_bwd(do, o, ws, *, name):
    S = do.shape[0]

    def body(tm, do_r, o_r, w1, w4, w16, d1, d4, d16, e1, e4, e16, nat, nat_l):
        prod = do_r[...] * o_r[...]
        sums = _cols_to_lanes([jnp.sum(prod[:, j * HEAD_DIM:(j + 1) * HEAD_DIM], axis=1, keepdims=True)
                               for j in range(DIL_HEADS)], tm)
        for w_r, d_o, e_o, d in zip((w1, w4, w16), (d1, d4, d16), (e1, e4, e16), DILATIONS):
            w = w_r[...]
            nat_l[0] = w * sums
            _to_branch(nat_l, 0, 1, e_o, d, tm)
            for c in range(QKV_CHUNKS):
                nat[c] = _head_pair_weights(w, c, tm) * do_r[:, c * LANES:(c + 1) * LANES]
            _to_branch(nat, 0, QKV_CHUNKS, d_o, d, tm)

    outs = [_branch_sds(S, 1024, d, MXU_DTYPE) for d in DILATIONS] + [_branch_sds(S, LANES, d, F32) for d in DILATIONS]
    rows = _tile(S, 256)
    got = _rowwise(body, [do, o] + [ws[d] for d in DILATIONS], outs, name=name, rows=S, tm=256,
                   scratch=[pltpu.VMEM((QKV_CHUNKS, rows, LANES), F32), pltpu.VMEM((1, rows, LANES), F32)])
    return dict(zip(DILATIONS, got[:3])), dict(zip(DILATIONS, got[3:]))


def _loss_head(x, g, target, *, name):
    S, D = x.shape

    def body(tm, x_r, g_r, t_r, dx_o, dg_o, sq_o):
        xf = x_r[...]
        xhat, _ = _rms_parts(xf)
        err = xhat * g_r[...] - t_r[...]
        dx, dgp = _rms_bwd_rows(xf, g_r[...], err * (1.0 / D))
        dx_o[...] = dx
        _acc_rows(dg_o, dgp)
        _acc_rows(sq_o, err * err)

    return _rowwise(body, [x, g.reshape(1, D), target], [_sds((S, D)), _sds((1, D)), _sds((1, D))],
                    name=name, rows=S, accs=(1, 2))


def _adamw(w, g, m, v, *, name):
    c1 = 1.0 - ADAM_B1 ** ADAM_STEP
    c2 = 1.0 - ADAM_B2 ** ADAM_STEP

    def body(tm, w_r, g_r, m_r, v_r, d_o, m_o, v_o):
        g = g_r[...]
        m_new = ADAM_B1 * m_r[...] + (1.0 - ADAM_B1) * g
        v_new = ADAM_B2 * v_r[...] + (1.0 - ADAM_B2) * (g * g)
        m_o[...] = m_new
        v_o[...] = v_new
        d_o[...] = -ADAM_LR * ((m_new / c1) / (jnp.sqrt(v_new / c2) + ADAM_EPS) + ADAM_WD * w_r[...])

    return _rowwise(body, [w, g, m, v], [_sds(w.shape)] * 3, name=name, rows=w.shape[0], tm=256)


SUM_ROW_TILE = 256


def _sum_cores(grads, theirs, half_index, *, name):
    _, R, C = grads.shape
    h = R // 2
    nb = h // SUM_ROW_TILE

    def body(c_ref, g_ref, t_ref, o_ref):
        o_ref[...] = (g_ref[...].astype(F32) + t_ref[...].astype(F32)).astype(o_ref.dtype)

    grid_spec = pltpu.PrefetchScalarGridSpec(
        num_scalar_prefetch=1, grid=(4, nb),
        in_specs=[pl.BlockSpec((1, SUM_ROW_TILE, C), lambda k, i, c_ref: (k, c_ref[0] * nb + i, 0)),
                  pl.BlockSpec((1, SUM_ROW_TILE, C), lambda k, i, c_ref: (k, i, 0))],
        out_specs=pl.BlockSpec((1, SUM_ROW_TILE, C), lambda k, i, c_ref: (k, i, 0)))
    return _pcall(body, name=name, dims=("parallel", "parallel"), grid_spec=grid_spec,
                  out_shape=_sds((4, h, C), grads.dtype))(half_index, grads, theirs)


def _sum_chips(parts, half_index, *, name):
    _, h, C = parts.shape
    nb = h // SUM_ROW_TILE

    def body(c_ref, p_ref, o_ref):
        p = [p_ref[k].astype(F32) for k in range(4)]
        o_ref[...] = ((p[0] + p[1]) + p[2]) + p[3]

    grid_spec = pltpu.PrefetchScalarGridSpec(
        num_scalar_prefetch=1, grid=(nb,),
        in_specs=[pl.BlockSpec((4, SUM_ROW_TILE, C), lambda i, c_ref: (0, i, 0))],
        out_specs=pl.BlockSpec((SUM_ROW_TILE, C), lambda i, c_ref: (c_ref[0] * nb + i, 0)))
    return _pcall(body, name=name, dims=("parallel",), grid_spec=grid_spec,
                  out_shape=_sds((2 * h, C)))(half_index, parts)


def _position():
    return lax.axis_index("x"), lax.axis_index("y"), lax.axis_index("c")


def _chip_peers(x, y):
    return [(1 - x, y), (x, 1 - y), (1 - x, 1 - y)]


_HBM = pl.BlockSpec(memory_space=pltpu.HBM)
LOCAL_COPY_CHUNKS = 8


def _local_copies(src_ref, dst_ref, sems):
    rows = src_ref.shape[0] // LOCAL_COPY_CHUNKS
    assert rows * LOCAL_COPY_CHUNKS == src_ref.shape[0]
    return [pltpu.make_async_copy(src_ref.at[pl.ds(i * rows, rows)], dst_ref.at[pl.ds(i * rows, rows)], sems.at[i])
            for i in range(LOCAL_COPY_CHUNKS)]


def _gather_weights(src, *, name):
    R, C = src.shape
    h = R // 2

    def body(src_ref, out_ref, send_sems, recv_sems, local_sems):
        x, y, c = _position()
        me = 2 * x + y
        peers = _chip_peers(x, y)
        mine, other = pl.ds(c * h, h), pl.ds((1 - c) * h, h)

        def copy(sem, src_part, dst_part, device):
            return pltpu.make_async_remote_copy(
                src_ref=src_part, dst_ref=dst_part, send_sem=send_sems.at[sem], recv_sem=recv_sems.at[sem],
                device_id=device, device_id_type=MESH)

        sends = [copy(j, src_ref.at[mine], out_ref.at[me, mine], (px, py, c)) for j, (px, py) in enumerate(peers)]
        for cp in sends:
            cp.start()
        local = _local_copies(src_ref, out_ref.at[me], local_sems)
        for cp in local:
            cp.start()
        passed = []
        for j, (px, py) in enumerate(peers):
            landed = out_ref.at[2 * px + py, mine]
            copy(j, landed, landed, (px, py, c)).wait_recv()
            passed.append(copy(3 + j, landed, landed, (x, y, 1 - c)))
            passed[-1].start()
        for j, (px, py) in enumerate(peers):
            theirs = out_ref.at[2 * px + py, other]
            copy(3 + j, theirs, theirs, (x, y, 1 - c)).wait_recv()
        for cp in sends + passed:
            cp.wait_send()
        for cp in local:
            cp.wait()

    return pl.pallas_call(
        body, name=name, in_specs=[_HBM], out_specs=_HBM, out_shape=jax.ShapeDtypeStruct((4, R, C), src.dtype),
        scratch_shapes=[pltpu.SemaphoreType.DMA((6,)), pltpu.SemaphoreType.DMA((6,)),
                        pltpu.SemaphoreType.DMA((LOCAL_COPY_CHUNKS,))],
    )(src)


def _swap_other_half(src, *, name):
    _, R, C = src.shape
    h = R // 2

    def body(src_ref, out_ref, send_sem, recv_sem):
        x, y, c = _position()
        cp = pltpu.make_async_remote_copy(
            src_ref=src_ref.at[:, pl.ds((1 - c) * h, h)], dst_ref=out_ref, send_sem=send_sem, recv_sem=recv_sem,
            device_id=(x, y, 1 - c), device_id_type=MESH)
        cp.start()
        cp.wait()

    return pl.pallas_call(
        body, name=name, in_specs=[_HBM], out_specs=_HBM, out_shape=jax.ShapeDtypeStruct((4, h, C), src.dtype),
        scratch_shapes=[pltpu.SemaphoreType.DMA, pltpu.SemaphoreType.DMA],
    )(src)


def _scatter_chips(src, *, name):
    def body(src_ref, out_ref, send_sems, recv_sems, local_sems):
        x, y, c = _position()
        me = 2 * x + y
        peers = _chip_peers(x, y)

        def copy(j, src_block, dst_slot):
            px, py = peers[j]
            return pltpu.make_async_remote_copy(
                src_ref=src_ref.at[src_block], dst_ref=out_ref.at[dst_slot], send_sem=send_sems.at[j],
                recv_sem=recv_sems.at[j], device_id=(px, py, c), device_id_type=MESH)

        sends = [copy(j, 2 * px + py, me) for j, (px, py) in enumerate(peers)]
        for cp in sends:
            cp.start()
        local = _local_copies(src_ref.at[me], out_ref.at[me], local_sems)
        for cp in local:
            cp.start()
        for j, (px, py) in enumerate(peers):
            copy(j, me, 2 * px + py).wait_recv()
        for cp in sends:
            cp.wait_send()
        for cp in local:
            cp.wait()

    return pl.pallas_call(
        body, name=name, in_specs=[_HBM], out_specs=_HBM, out_shape=jax.ShapeDtypeStruct(src.shape, src.dtype),
        scratch_shapes=[pltpu.SemaphoreType.DMA((3,)), pltpu.SemaphoreType.DMA((3,)),
                        pltpu.SemaphoreType.DMA((LOCAL_COPY_CHUNKS,))],
    )(src)


def _join_halves(src, *, name):
    R, C = src.shape
    h = R // 2

    def body(src_ref, out_ref, send_sem, recv_sem):
        x, y, c = _position()
        mine, theirs = pl.ds(c * h, h), pl.ds((1 - c) * h, h)
        cp = pltpu.make_async_remote_copy(
            src_ref=src_ref.at[mine], dst_ref=out_ref.at[mine], send_sem=send_sem, recv_sem=recv_sem,
            device_id=(x, y, 1 - c), device_id_type=MESH)
        cp.start()
        pltpu.make_async_remote_copy(
            src_ref=src_ref.at[theirs], dst_ref=out_ref.at[theirs], send_sem=send_sem, recv_sem=recv_sem,
            device_id=(x, y, 1 - c), device_id_type=MESH).wait_recv()
        cp.wait_send()

    return pl.pallas_call(
        body, name=name, in_specs=[_HBM], out_specs=_HBM, out_shape=jax.ShapeDtypeStruct((R, C), src.dtype),
        input_output_aliases={0: 0},
        scratch_shapes=[pltpu.SemaphoreType.DMA, pltpu.SemaphoreType.DMA],
    )(src)


def _allreduce_small(vec, *, name):
    R, C = vec.shape

    def body(v_ref, o_ref, slots, send_sems, recv_sems):
        x, y, c = _position()
        me = 4 * x + 2 * y + c

        def peer(k):
            return x ^ ((k >> 2) & 1), y ^ ((k >> 1) & 1), c ^ (k & 1)

        def copy(k, slot):
            return pltpu.make_async_remote_copy(
                src_ref=v_ref, dst_ref=slots.at[slot], send_sem=send_sems.at[k - 1], recv_sem=recv_sems.at[k - 1],
                device_id=peer(k), device_id_type=MESH)

        slots[me] = v_ref[...]
        sends = [copy(k, me) for k in range(1, 8)]
        for cp in sends:
            cp.start()
        for k in range(1, 8):
            px, py, pc = peer(k)
            copy(k, 4 * px + 2 * py + pc).wait_recv()
        total = slots[0]
        for d in range(1, 8):
            total = total + slots[d]
        o_ref[...] = total
        for cp in sends:
            cp.wait_send()

    vmem = pl.BlockSpec(memory_space=pltpu.VMEM)
    return pl.pallas_call(
        body, name=name, in_specs=[vmem], out_specs=vmem, out_shape=jax.ShapeDtypeStruct((R, C), vec.dtype),
        scratch_shapes=[pltpu.VMEM((8, R, C), vec.dtype), pltpu.SemaphoreType.DMA((7,)), pltpu.SemaphoreType.DMA((7,))],
    )(vec)


def _cross_cfg(S, mem_len):
    return _Attn(T=S, Tk=mem_len, G=1, nh=X_HEADS, rep=1, dqk=X_HEAD_DIM, dv=X_HEAD_DIM, tq=512, tk=mem_len,
                 mode="none", scale=X_HEAD_DIM ** -0.5, qcol=lambda g: 0, kcol=lambda g: 0, vcol=lambda g: 1,
                 ocol=lambda g: 0, o_width=X_HEADS * X_HEAD_DIM)


def _swa_cfg(S):
    return _Attn(T=S, Tk=S, G=1, nh=SWA_HEADS, rep=SWA_HEADS // SWA_KV_HEADS, dqk=HEAD_DIM, dv=HEAD_DIM, tq=BLOCK,
                 tk=BLOCK, mode="band", max_dist=SWA_WINDOW - 1, scale=HEAD_DIM ** -0.5, qcol=lambda g: 0,
                 kcol=lambda g: 0, vcol=lambda g: 0, ocol=lambda g: 0, o_width=SWA_HEADS * HEAD_DIM)


def _mla_cfg(S):
    t = _tile(S, 512)
    return _Attn(T=S, Tk=S, G=MLA_HEADS // 2, nh=2, rep=1, dqk=LANES, dv=MLA_V, tq=t, tk=t, mode="causal",
                 scale=(MLA_NOPE + MLA_ROPE) ** -0.5, qcol=lambda g: g, kcol=lambda g: g, vcol=lambda g: g,
                 ocol=lambda g: g, o_width=MLA_HEADS * MLA_V)


def _dil_cfg(S, window, dil):
    return _Attn(T=S // dil, Tk=S // dil, G=dil, nh=DIL_HEADS, rep=1, dqk=HEAD_DIM, dv=HEAD_DIM, tq=BLOCK, tk=BLOCK,
                 mode="band", max_dist=window // dil, scale=HEAD_DIM ** -0.5, qcol=lambda g: g, kcol=lambda g: g,
                 vcol=lambda g: g, ocol=lambda g: g, o_width=dil * DIL_HEADS * HEAD_DIM)


def _rows_cfg(S):
    return _Attn(T=S, Tk=S, G=1, nh=DIL_HEADS, rep=1, dqk=HEAD_DIM, dv=HEAD_DIM, tq=BLOCK, tk=BLOCK, mode="none",
                 scale=1.0, qcol=lambda g: 0, kcol=lambda g: 0, vcol=lambda g: 0, ocol=lambda g: 0,
                 o_width=DIL_HEADS * HEAD_DIM)


def _cross_fwd(p, x, mem, W, vec):
    S = x.shape[0]
    cfg = _cross_cfg(S, mem.shape[0])
    hx = _rmsnorm(x, vec[p + "x_norm"], name=p + "x_norm")
    qx = _mm(hx, W[p + "w_xq"], mode="nn", name=p + "xq", out_dtype=MXU_DTYPE)
    memn = _rmsnorm(mem, vec[p + "mem_norm"], name=p + "mem_norm")
    kvx = _mm(memn, W[p + "w_xkv"], mode="nn", name=p + "xkv", out_dtype=MXU_DTYPE)
    ox, lse = _attn_fwd(cfg, qx, kvx, kvx, name=p + "x_attn", out_dtype=MXU_DTYPE)
    out = _mm(ox, W[p + "w_xo"], mode="nn", name=p + "xo", res=x)
    return out, (x, hx, qx, memn, kvx, ox, lse)


def _cross_bwd(p, dx, saved, mem, W, vec, dW, dvec):
    x, hx, qx, memn, kvx, ox, lse = saved
    cfg = _cross_cfg(x.shape[0], mem.shape[0])
    dox = _mm(dx, W[p + "w_xo"], mode="nt", name=p + "xo_dx", out_dtype=MXU_DTYPE)
    dW[p + "w_xo"] = _mm(ox, dx, mode="tn", name=p + "xo_dw")
    delta, _ = _attn_delta(cfg, ox, dox, name=p + "x_delta")
    dqx = _attn_dq(cfg, qx, kvx, kvx, dox, lse, delta, name=p + "x_dq", out_dtype=MXU_DTYPE)
    dkx, dvx = _attn_dkv(cfg, qx, kvx, kvx, dox, lse, delta, name=p + "x_dkv", out_dtype=MXU_DTYPE)
    dkvx = jnp.concatenate([dkx, dvx], axis=1)
    dhx = _mm(dqx, W[p + "w_xq"], mode="nt", name=p + "xq_dx")
    dW[p + "w_xq"] = _mm(hx, dqx, mode="tn", name=p + "xq_dw")
    dW[p + "w_xkv"] = _mm(memn, dkvx, mode="tn", name=p + "xkv_dw")
    dmemn = _mm(dkvx, W[p + "w_xkv"], mode="nt", name=p + "xkv_dx")
    _, dvec[p + "mem_norm"] = _rmsnorm_bwd(mem, vec[p + "mem_norm"], dmemn, name=p + "mem_norm_bwd")
    dx_in, dvec[p + "x_norm"] = _rmsnorm_bwd(x, vec[p + "x_norm"], dhx, name=p + "x_norm_bwd", dres=dx)
    return dx_in


def _ffn_fwd(p, x, W, vec):
    hf = _rmsnorm(x, vec[p + "ffn_norm"], name=p + "ffn_norm")
    gate = _mm(hf, W[p + "w_gate"], mode="nn", name=p + "gate")
    up = _mm(hf, W[p + "w_up"], mode="nn", name=p + "up")
    act = _swiglu(gate, up, name=p + "swiglu")
    out = _mm(act, W[p + "w_down"], mode="nn", name=p + "down", res=x)
    return out, (x, hf, gate, up, act)


def _ffn_bwd(p, dx, saved, W, vec, dW, dvec):
    x, hf, gate, up, act = saved
    dact = _mm(dx, W[p + "w_down"], mode="nt", name=p + "down_dx")
    dW[p + "w_down"] = _mm(act, dx, mode="tn", name=p + "down_dw")
    dgate, dup = _swiglu_bwd(gate, up, dact, name=p + "swiglu_bwd")
    dhf = _mm(dgate, W[p + "w_gate"], mode="nt", name=p + "gate_dx")
    dhf = _mm(dup, W[p + "w_up"], mode="nt", name=p + "up_dx", res=dhf)
    dW[p + "w_gate"] = _mm(hf, dgate, mode="tn", name=p + "gate_dw")
    dW[p + "w_up"] = _mm(hf, dup, mode="tn", name=p + "up_dw")
    dx_in, dvec[p + "ffn_norm"] = _rmsnorm_bwd(x, vec[p + "ffn_norm"], dhf, name=p + "ffn_norm_bwd", dres=dx)
    return dx_in


def _even_fwd(p, x, tabs, W, vec):
    S = x.shape[0]
    h = _rmsnorm(x, vec[p + "mix_norm"], name=p + "mix_norm")
    z = _mm(h, W[p + "w_in"], mode="nn", name=p + "in")
    qa, ka, va, cqn, ckvn, kr = _l0_prep(z, tabs, vec[p + "q_norm"], vec[p + "kv_norm"], name=p + "prep")
    sink = jnp.pad(vec[p + "sinks"], (0, LANES - SWA_HEADS)).reshape(1, LANES)
    oa, lse_a = _band_fwd(_swa_cfg(S), qa, ka, va, name=p + "swa", sink=sink, out_dtype=MXU_DTYPE)
    qb = _mm(cqn, W[p + "w_uq"], mode="nn", name=p + "uq")
    kvb = _mm(ckvn, W[p + "w_ukv"], mode="nn", name=p + "ukv")
    Q, K, V = _mla_prep(qb, kvb, kr, tabs, name=p + "mla_prep")
    ob, lse_b = _causal_fwd(_mla_cfg(S), Q, K, V, name=p + "mla", out_dtype=MXU_DTYPE)
    o = jnp.concatenate([oa, ob], axis=1)
    out = _mm(o, W[p + "w_out"], mode="nn", name=p + "out", res=x)
    return out, (x, h, z, qa, ka, va, cqn, ckvn, sink, oa, lse_a, Q, K, V, ob, lse_b, o)


def _even_bwd(p, dx, saved, tabs, W, vec, dW, dvec):
    x, h, z, qa, ka, va, cqn, ckvn, sink, oa, lse_a, Q, K, V, ob, lse_b, o = saved
    S = x.shape[0]
    do = _mm(dx, W[p + "w_out"], mode="nt", name=p + "out_dx", out_dtype=MXU_DTYPE)
    dW[p + "w_out"] = _mm(o, dx, mode="tn", name=p + "out_dw")
    doa, dob = do[:, :SWA_HEADS * HEAD_DIM], do[:, SWA_HEADS * HEAD_DIM:]
    cfg = _swa_cfg(S)
    delta, dsink = _attn_delta(cfg, oa, doa, name=p + "swa_delta", lse=lse_a, sink=sink)
    dvec[p + "sinks"] = dsink
    dqa, dka, dva = _band_bwd(cfg, qa, ka, va, doa, lse_a, delta, name=p + "swa_bwd")
    cfg = _mla_cfg(S)
    delta, _ = _attn_delta(cfg, ob, dob, name=p + "mla_delta")
    dQ, dK, dV = _causal_bwd(cfg, Q, K, V, dob, lse_b, delta, name=p + "mla_bwd")
    dqb, dkvb, dkr = _mla_prep_bwd(dQ, dK, dV, tabs, name=p + "mla_prep_bwd")
    dcqn = _mm(dqb, W[p + "w_uq"], mode="nt", name=p + "uq_dx")
    dW[p + "w_uq"] = _mm(cqn, dqb, mode="tn", name=p + "uq_dw")
    dckvn = _mm(dkvb, W[p + "w_ukv"], mode="nt", name=p + "ukv_dx")
    dW[p + "w_ukv"] = _mm(ckvn, dkvb, mode="tn", name=p + "ukv_dw")
    dz, dvec[p + "q_norm"], dvec[p + "kv_norm"] = _l0_prep_bwd(
        z, tabs, vec[p + "q_norm"], vec[p + "kv_norm"], dqa, dka, dva, dcqn, dckvn, dkr, name=p + "prep_bwd")
    dh = _mm(dz, W[p + "w_in"], mode="nt", name=p + "in_dx")
    dW[p + "w_in"] = _mm(h, dz, mode="tn", name=p + "in_dw")
    dx_in, dvec[p + "mix_norm"] = _rmsnorm_bwd(x, vec[p + "mix_norm"], dh, name=p + "mix_norm_bwd", dres=dx)
    return dx_in


def _odd_fwd(p, x, tabs, W, vec):
    S = x.shape[0]
    assert S % (DIL_PATTERNS[-1][1] * BLOCK) == 0, "keys past the end of the sequence are never attended"
    h = _rmsnorm(x, vec[p + "mix_norm"], name=p + "mix_norm")
    qkv = _mm(h, W[p + "w_qkv"], mode="nn", name=p + "qkv")
    qkv_by_d = _l1_prep(qkv, tabs, name=p + "prep")
    outs, lses = {}, {}
    for window, dil in DIL_PATTERNS:
        outs[dil], lses[dil] = _band_fwd(_dil_cfg(S, window, dil), *qkv_by_d[dil], name=p + "dil%d" % dil)
    o, w1, w4, w16 = _merge(outs, lses, name=p + "merge")
    out = _mm(o, W[p + "w_out"], mode="nn", name=p + "out", res=x)
    return out, (x, h, qkv_by_d, lses, dict(zip(DILATIONS, (w1, w4, w16))), o)


def _odd_bwd(p, dx, saved, tabs, W, vec, dW, dvec):
    x, h, qkv_by_d, lses, ws, o = saved
    S = x.shape[0]
    do = _mm(dx, W[p + "w_out"], mode="nt", name=p + "out_dx")
    dW[p + "w_out"] = _mm(o, dx, mode="tn", name=p + "out_dw")
    dos, deltas = _merge_bwd(do, o, ws, name=p + "merge_bwd")
    grads = {}
    for window, dil in DIL_PATTERNS:
        grads[dil] = _band_bwd(_dil_cfg(S, window, dil), *qkv_by_d[dil], dos[dil], lses[dil], deltas[dil],
                               name=p + "dil%d_bwd" % dil)
    dqkv = _l1_prep_bwd(grads, tabs, name=p + "prep_bwd")
    dh = _mm(dqkv, W[p + "w_qkv"], mode="nt", name=p + "qkv_dx")
    dW[p + "w_qkv"] = _mm(h, dqkv, mode="tn", name=p + "qkv_dw")
    dx_in, dvec[p + "mix_norm"] = _rmsnorm_bwd(x, vec[p + "mix_norm"], dh, name=p + "mix_norm_bwd", dres=dx)
    return dx_in


def _local_step(x, mem, positions, target, W, vec):
    tabs = _rope_tables(positions)
    x1, s_mix0 = _even_fwd("l0_", x, tabs, W, vec)
    x2, s_x0 = _cross_fwd("l0_", x1, mem, W, vec)
    x3, s_f0 = _ffn_fwd("l0_", x2, W, vec)
    x4, s_mix1 = _odd_fwd("l1_", x3, tabs, W, vec)
    x5, s_x1 = _cross_fwd("l1_", x4, mem, W, vec)
    x6, s_f1 = _ffn_fwd("l1_", x5, W, vec)
    dW, dvec = {}, {}
    dx, dvec["final_norm"], sq = _loss_head(x6, vec["final_norm"], target, name="loss_head")
    dx = _ffn_bwd("l1_", dx, s_f1, W, vec, dW, dvec)
    dx = _cross_bwd("l1_", dx, s_x1, mem, W, vec, dW, dvec)
    dx = _odd_bwd("l1_", dx, s_mix1, tabs, W, vec, dW, dvec)
    dx = _ffn_bwd("l0_", dx, s_f0, W, vec, dW, dvec)
    dx = _cross_bwd("l0_", dx, s_x0, mem, W, vec, dW, dvec)
    dx = _even_bwd("l0_", dx, s_mix0, tabs, W, vec, dW, dvec)
    return sq, dx, dW, dvec


_LAYER_MATS = {
    0: [("w_in", "col"), ("w_uq", "col"), ("w_ukv", "col"), ("w_out", "row"), ("w_xq", "row"), ("w_xkv", "row"),
        ("w_xo", "col"), ("w_gate", "col"), ("w_up", "col"), ("w_down", "row")],
    1: [("w_qkv", "col"), ("w_out", "row"), ("w_xq", "row"), ("w_xkv", "row"), ("w_xo", "col"), ("w_gate", "col"),
        ("w_up", "col"), ("w_down", "row")],
}
MATS = [("l%d_%s" % (l, n), kind) for l in (0, 1) for n, kind in _LAYER_MATS[l]]
_LAYER_VECS = {0: ["mix_norm", "sinks", "q_norm", "kv_norm", "x_norm", "mem_norm", "ffn_norm"],
               1: ["mix_norm", "x_norm", "mem_norm", "ffn_norm"]}
VECS = ["l%d_%s" % (l, n) for l in (0, 1) for n in _LAYER_VECS[l]] + ["final_norm"]
WEIGHT_ORDER = (["l0_mix_norm", "l0_w_in", "l0_sinks", "l0_q_norm", "l0_w_uq", "l0_kv_norm", "l0_w_ukv", "l0_w_out",
                 "l0_x_norm", "l0_mem_norm", "l0_w_xq", "l0_w_xkv", "l0_w_xo", "l0_ffn_norm", "l0_w_gate", "l0_w_up",
                 "l0_w_down", "l1_mix_norm", "l1_w_qkv", "l1_w_out", "l1_x_norm", "l1_mem_norm", "l1_w_xq",
                 "l1_w_xkv", "l1_w_xo", "l1_ffn_norm", "l1_w_gate", "l1_w_up", "l1_w_down", "final_norm"])
PACK_COLS = 1024
PACK_ROW_TILE = 2 * SUM_ROW_TILE
EXCHANGE_DTYPE = jnp.bfloat16
VEC_ROWS = 16
LOSS_ROW = len(VECS)
N_CHIPS = 4


def _pack_layout(shards):
    layout, off = {}, 0
    for name, _ in MATS:
        n = shards[name].size // PACK_COLS
        assert n * PACK_COLS == shards[name].size
        layout[name] = (off, n)
        off += n
    return layout, -(-off // PACK_ROW_TILE) * PACK_ROW_TILE


def _pack_shards(shards, layout, rows, dtype):
    parts = [shards[name].astype(dtype).reshape(-1, PACK_COLS) for name, _ in MATS]
    used = sum(p.shape[0] for p in parts)
    return jnp.concatenate(parts + [jnp.zeros((rows - used, PACK_COLS), dtype)], axis=0)


def _unpack_shards(packed, layout, shards):
    return {name: packed[off:off + n].reshape(shards[name].shape) for name, (off, n) in layout.items()}


def _full_weights(gathered, layout, shards):
    W = {}
    for name, kind in MATS:
        off, n = layout[name]
        r, cw = shards[name].shape
        blocks = gathered[:, off:off + n].reshape(N_CHIPS, r, cw)
        W[name] = blocks.reshape(N_CHIPS * r, cw) if kind == "row" else (
            jnp.transpose(blocks, (1, 0, 2)).reshape(r, N_CHIPS * cw))
    W["l0_w_in"] = jnp.pad(W["l0_w_in"], ((0, 0), (0, Z_END - W["l0_w_in"].shape[1])))
    per_head = MLA_NOPE + MLA_ROPE
    uq = W["l0_w_uq"].reshape(MLA_Q_RANK, MLA_HEADS, per_head)
    W["l0_w_uq"] = jnp.pad(uq, ((0, 0), (0, 0), (0, LANES - per_head))).reshape(MLA_Q_RANK, MLA_HEADS * LANES)
    return W


def _pack_grads(dW, layout, rows, shards):
    per_head = MLA_NOPE + MLA_ROPE
    dW = dict(dW)
    dW["l0_w_in"] = dW["l0_w_in"][:, :Z_KR + MLA_ROPE]
    dW["l0_w_uq"] = dW["l0_w_uq"].reshape(MLA_Q_RANK, MLA_HEADS, LANES)[:, :, :per_head].reshape(MLA_Q_RANK, -1)
    parts = []
    for name, kind in MATS:
        r, cw = shards[name].shape
        g = dW[name]
        if kind == "col":
            g = jnp.transpose(g.reshape(r, N_CHIPS, cw), (1, 0, 2))
        parts.append(g.reshape(N_CHIPS, -1, PACK_COLS).astype(EXCHANGE_DTYPE))
    used = sum(p.shape[1] for p in parts)
    return jnp.concatenate(parts + [jnp.zeros((N_CHIPS, rows - used, PACK_COLS), EXCHANGE_DTYPE)], axis=1)


def _pack_vecs(vecs):
    rows = [jnp.pad(vecs[n].reshape(-1).astype(F32), (0, PACK_COLS - vecs[n].size)) for n in VECS]
    rows += [jnp.zeros((PACK_COLS,), F32)] * (VEC_ROWS - len(rows))
    return jnp.stack(rows)


def _unpack_vecs(packed, like):
    return {n: packed[i, :like[n].size].reshape(like[n].shape) for i, n in enumerate(VECS)}


def _step(a):
    weights = {n: a[n] for n in WEIGHT_ORDER}
    shards = {n: weights[n] for n, _ in MATS}
    vec = {n: weights[n] for n in VECS}
    layout, rows = _pack_layout(shards)

    gathered = _gather_weights(_pack_shards(shards, layout, rows, MXU_DTYPE), name="gather_weights")
    W = _full_weights(gathered, layout, shards)
    sq, grad_x, dW, dvec = _local_step(a["x"][0], a["mem"][0], a["positions"], a["loss_target"][0], W, vec)

    dvec = dict(dvec)
    dvec["l0_sinks"] = dvec["l0_sinks"][0, :SWA_HEADS]
    small = _pack_vecs(dvec)
    small = small.at[LOSS_ROW, 0].set(0.5 / a["x"].shape[-1] * jnp.sum(sq))
    small = _allreduce_small(small, name="reduce_gains")
    loss = small[LOSS_ROW, 0]
    g_s = small.at[LOSS_ROW, 0].set(0.0)
    d_s, m_s, v_s = _adamw(_pack_vecs(vec), g_s, _pack_vecs({n: a["m_" + n] for n in VECS}),
                           _pack_vecs({n: a["v_" + n] for n in VECS}), name="adamw_gains")

    grads = _pack_grads(dW, layout, rows, shards)
    half_index = lax.axis_index("c").astype(jnp.int32).reshape(1)
    chip_sum = _sum_cores(grads, _swap_other_half(grads, name="swap_other_half"), half_index, name="sum_cores")
    mine = _sum_chips(_scatter_chips(chip_sum, name="scatter_grads"), half_index, name="sum_chips")
    g_w = _join_halves(mine, name="join_halves")
    d_w, m_w, v_w = _adamw(
        _pack_shards(shards, layout, rows, F32), g_w,
        _pack_shards({n: a["m_" + n] for n, _ in MATS}, layout, rows, F32),
        _pack_shards({n: a["v_" + n] for n, _ in MATS}, layout, rows, F32), name="adamw_mats")

    out = [loss, grad_x[None]]
    for packed_w, packed_s in ((g_w, g_s), (d_w, d_s), (m_w, m_s), (v_w, v_s)):
        got = {**_unpack_shards(packed_w, layout, shards), **_unpack_vecs(packed_s, vec)}
        out += [got[n] for n in WEIGHT_ORDER]
    return tuple(out)


def kernel(x, mem, positions, l0_mix_norm, l0_w_in, l0_sinks, l0_q_norm, l0_w_uq, l0_kv_norm, l0_w_ukv, l0_w_out, l0_x_norm, l0_mem_norm, l0_w_xq, l0_w_xkv, l0_w_xo, l0_ffn_norm, l0_w_gate, l0_w_up, l0_w_down, l1_mix_norm, l1_w_qkv, l1_w_out, l1_x_norm, l1_mem_norm, l1_w_xq, l1_w_xkv, l1_w_xo, l1_ffn_norm, l1_w_gate, l1_w_up, l1_w_down, final_norm, loss_target, m_l0_mix_norm, m_l0_w_in, m_l0_sinks, m_l0_q_norm, m_l0_w_uq, m_l0_kv_norm, m_l0_w_ukv, m_l0_w_out, m_l0_x_norm, m_l0_mem_norm, m_l0_w_xq, m_l0_w_xkv, m_l0_w_xo, m_l0_ffn_norm, m_l0_w_gate, m_l0_w_up, m_l0_w_down, m_l1_mix_norm, m_l1_w_qkv, m_l1_w_out, m_l1_x_norm, m_l1_mem_norm, m_l1_w_xq, m_l1_w_xkv, m_l1_w_xo, m_l1_ffn_norm, m_l1_w_gate, m_l1_w_up, m_l1_w_down, m_final_norm, v_l0_mix_norm, v_l0_w_in, v_l0_sinks, v_l0_q_norm, v_l0_w_uq, v_l0_kv_norm, v_l0_w_ukv, v_l0_w_out, v_l0_x_norm, v_l0_mem_norm, v_l0_w_xq, v_l0_w_xkv, v_l0_w_xo, v_l0_ffn_norm, v_l0_w_gate, v_l0_w_up, v_l0_w_down, v_l1_mix_norm, v_l1_w_qkv, v_l1_w_out, v_l1_x_norm, v_l1_mem_norm, v_l1_w_xq, v_l1_w_xkv, v_l1_w_xo, v_l1_ffn_norm, v_l1_w_gate, v_l1_w_up, v_l1_w_down, v_final_norm):
    return _step(dict(locals()))
```

```python
import functools

import jax
import jax.numpy as jnp
import numpy as np
from jax import lax
from jax.experimental import pallas as pl
from jax.experimental.pallas import tpu as pltpu

F32 = jnp.float32
MXU_DTYPE = jnp.bfloat16
LANES = 128
VMEM_LIMIT_BYTES = 56 * 1024 * 1024

NORM_EPS = 1e-6
ROPE_THETA = 10000.0
BLOCK = 128
HEAD_DIM = 64
SWA_HEADS, SWA_KV_HEADS, SWA_WINDOW = 8, 2, 128
MLA_HEADS, MLA_Q_RANK, MLA_KV_RANK, MLA_NOPE, MLA_ROPE, MLA_V = 8, 384, 256, 64, 32, 64
DIL_HEADS = 16
DIL_PATTERNS = ((128, 1), (512, 4), (2048, 16))
X_HEADS, X_HEAD_DIM = 4, 128
ADAM_LR, ADAM_B1, ADAM_B2, ADAM_EPS, ADAM_WD, ADAM_STEP = 0.001, 0.9, 0.999, 1e-08, 0.01, 10
MESH = pl.DeviceIdType.MESH
NEG_BIG = -1e30

NN = (((1,), (0,)), ((), ()))
NT = (((1,), (1,)), ((), ()))


def _dot(a, b, dims=NN):
    return lax.dot_general(a.astype(MXU_DTYPE), b.astype(MXU_DTYPE), dims, preferred_element_type=F32)


def _pcall(body, *, name, dims=None, **kw):
    params = pltpu.CompilerParams(dimension_semantics=dims, vmem_limit_bytes=VMEM_LIMIT_BYTES)
    return pl.pallas_call(body, name=name, compiler_params=params, **kw)


def _tile(n, pref):
    t = (min(pref, n) // LANES) * LANES
    while t >= LANES:
        if n % t == 0:
            return t
        t -= LANES
    return n


def _lane(shape):
    return lax.broadcasted_iota(jnp.int32, shape, 1)


def _cols_to_lanes(cols, rows):
    lane = _lane((rows, LANES))
    out = jnp.zeros((rows, LANES), F32)
    for j, col in enumerate(cols):
        out = jnp.where(lane == j, col, out)
    return out


def _mm(a, b, *, mode, name, res=None, out_dtype=F32, tm=1408, tn=1536, tk=1408):
    if mode == "nn":
        (M, K), (K2, N) = a.shape, b.shape
    elif mode == "nt":
        (M, K), (N, K2) = a.shape, b.shape
    else:
        (K, M), (K2, N) = a.shape, b.shape
    assert K == K2, (a.shape, b.shape, mode)
    tm, tn, tk = _tile(M, tm), _tile(N, tn), _tile(K, tk)
    nk = K // tk
    in_place = out_dtype == F32 or nk == 1

    def body(*refs):
        refs = list(refs)
        a_ref, b_ref = refs[:2]
        r_ref = refs[2] if res is not None else None
        o_ref = refs[3 if res is not None else 2]
        acc = o_ref if in_place else refs[-1]
        k = pl.program_id(2)
        if mode == "nn":
            part = _dot(a_ref[...], b_ref[...], NN)
        elif mode == "nt":
            part = _dot(a_ref[...], b_ref[...], NT)
        else:
            part = _dot(a_ref[...].T, b_ref[...], NN)
        if nk == 1:
            o_ref[...] = (part if res is None else part + r_ref[...].astype(F32)).astype(o_ref.dtype)
            return

        @pl.when(k == 0)
        def _():
            acc[...] = part if res is None else part + r_ref[...].astype(F32)

        @pl.when(k > 0)
        def _():
            acc[...] += part

        if not in_place:
            @pl.when(k == nk - 1)
            def _():
                o_ref[...] = acc[...].astype(o_ref.dtype)

    if mode == "nn":
        a_spec = pl.BlockSpec((tm, tk), lambda i, j, k: (i, k))
        b_spec = pl.BlockSpec((tk, tn), lambda i, j, k: (k, j))
    elif mode == "nt":
        a_spec = pl.BlockSpec((tm, tk), lambda i, j, k: (i, k))
        b_spec = pl.BlockSpec((tn, tk), lambda i, j, k: (j, k))
    else:
        a_spec = pl.BlockSpec((tk, tm), lambda i, j, k: (k, i))
        b_spec = pl.BlockSpec((tk, tn), lambda i, j, k: (k, j))
    o_spec = pl.BlockSpec((tm, tn), lambda i, j, k: (i, j))
    in_specs = [a_spec, b_spec] + ([] if res is None else [o_spec])
    args = (a, b) + (() if res is None else (res,))
    return _pcall(
        body, name=name, dims=("parallel", "parallel", "arbitrary"),
        grid=(M // tm, N // tn, nk), in_specs=in_specs, out_specs=o_spec,
        out_shape=jax.ShapeDtypeStruct((M, N), out_dtype),
        scratch_shapes=[] if in_place else [pltpu.VMEM((tm, tn), F32)],
    )(*args)


def _rms_parts(xf):
    r = lax.rsqrt(jnp.mean(xf * xf, axis=-1, keepdims=True) + NORM_EPS)
    return xf * r, r


def _rms_bwd_rows(xf, g, dy):
    xhat, r = _rms_parts(xf)
    dxhat = dy * g
    dx = r * (dxhat - xhat * jnp.mean(dxhat * xhat, axis=-1, keepdims=True))
    return dx, dy * xhat


def _rmsnorm(x, g, *, name, out_dtype=MXU_DTYPE, tm=512):
    M, D = x.shape
    tm = _tile(M, tm)

    def body(x_ref, g_ref, o_ref):
        xhat, _ = _rms_parts(x_ref[...].astype(F32))
        o_ref[...] = (xhat * g_ref[...]).astype(o_ref.dtype)

    return _pcall(
        body, name=name, dims=("parallel",), grid=(M // tm,),
        in_specs=[pl.BlockSpec((tm, D), lambda i: (i, 0)), pl.BlockSpec((1, D), lambda i: (0, 0))],
        out_specs=pl.BlockSpec((tm, D), lambda i: (i, 0)),
        out_shape=jax.ShapeDtypeStruct((M, D), out_dtype),
    )(x, g.reshape(1, D))


def _rmsnorm_bwd(x, g, dy, *, name, dres=None, tm=512):
    M, D = x.shape
    tm = _tile(M, tm)

    def body(*refs):
        if dres is None:
            x_ref, g_ref, dy_ref, dx_ref, dg_ref = refs
        else:
            x_ref, g_ref, dy_ref, dr_ref, dx_ref, dg_ref = refs
        dx, dgp = _rms_bwd_rows(x_ref[...].astype(F32), g_ref[...], dy_ref[...].astype(F32))
        if dres is not None:
            dx = dx + dr_ref[...]
        dx_ref[...] = dx

        @pl.when(pl.program_id(0) == 0)
        def _():
            dg_ref[...] = jnp.zeros_like(dg_ref)

        dg_ref[...] += jnp.sum(dgp, axis=0, keepdims=True)

    row = pl.BlockSpec((tm, D), lambda i: (i, 0))
    vec = pl.BlockSpec((1, D), lambda i: (0, 0))
    in_specs = [row, vec, row] + ([] if dres is None else [row])
    args = (x, g.reshape(1, D), dy) + (() if dres is None else (dres,))
    return _pcall(
        body, name=name, dims=("arbitrary",), grid=(M // tm,), in_specs=in_specs, out_specs=[row, vec],
        out_shape=[jax.ShapeDtypeStruct((M, D), F32), jax.ShapeDtypeStruct((1, D), F32)],
    )(*args)


def _rope_chunk(t, c, s, half):
    lane = _lane(t.shape)
    swapped = jnp.where((lane % (2 * half)) < half, pltpu.roll(t, LANES - half, 1), pltpu.roll(t, half, 1))
    return t * c + swapped * s


def _rope_tables(positions):
    pos = positions.reshape(-1).astype(F32)[:, None]
    S = pos.shape[0]

    def cs(dh):
        inv_freq = ROPE_THETA ** (-jnp.arange(0, dh, 2, dtype=F32) / dh)
        ang = pos * inv_freq
        return jnp.cos(ang), jnp.sin(ang)

    c64, s64 = cs(HEAD_DIM)
    c32, s32 = cs(MLA_ROPE)
    z32, z64, z96 = (jnp.zeros((S, n), F32) for n in (32, 64, 96))
    return dict(
        c64=jnp.concatenate([c64, c64, c64, c64], 1), s64=jnp.concatenate([-s64, s64, -s64, s64], 1),
        ck=jnp.concatenate([c32, c32, z96], 1), sk=jnp.concatenate([-s32, s32, z96], 1),
        cm=jnp.concatenate([jnp.ones((S, 64), F32), c32, c32, z32], 1),
        sm=jnp.concatenate([z64, -s32, s32, z32], 1),
    )


def _attn_steps(mode, n_other, t_self, t_other):
    if mode == "band":
        assert t_self == t_other
        return 2
    return n_other


def _kv_block(mode, qi, kj):
    if mode == "band":
        return jnp.maximum(qi - 1 + kj, 0), (qi + kj) >= 1
    if mode == "causal":
        return jnp.minimum(kj, qi), kj <= qi
    return kj, None


def _q_block(mode, ki, qj, nq):
    if mode == "band":
        return jnp.minimum(ki + qj, nq - 1), (ki + qj) <= nq - 1
    if mode == "causal":
        return jnp.maximum(qj, ki), qj >= ki
    return qj, None


def _mask(mode, max_dist, qpos, kpos):
    d = qpos - kpos
    if mode == "band":
        return (d >= 0) & (d <= max_dist)
    if mode == "causal":
        return d >= 0
    return None


def _when(cond, fn):
    if cond is None:
        fn()
    else:
        pl.when(cond)(fn)


class _Attn:
    def __init__(self, *, T, Tk, G, nh, rep, dqk, dv, tq, tk, mode, scale, qcol, kcol, vcol, ocol, o_width,
                 max_dist=0):
        self.__dict__.update(locals())
        self.nkv = nh // rep
        assert T % tq == 0 and Tk % tk == 0 and nh <= LANES


def _attn_fwd(cfg, q, k, v, *, name, sink=None, out_dtype=F32):
    c = cfg
    nq, nk = c.T // c.tq, c.Tk // c.tk
    steps = _attn_steps(c.mode, nk, c.tq, c.tk)

    def body(*refs):
        if sink is None:
            q_ref, k_ref, v_ref, o_ref, lse_ref, m_scr, l_scr, acc = refs
        else:
            q_ref, k_ref, v_ref, sink_ref, o_ref, lse_ref, m_scr, l_scr, acc = refs
        qi, kj = pl.program_id(1), pl.program_id(2)
        kb, valid = _kv_block(c.mode, qi, kj)

        @pl.when(kj == 0)
        def _():
            if sink is None:
                m_scr[...] = jnp.full_like(m_scr, NEG_BIG)
                l_scr[...] = jnp.zeros_like(l_scr)
            else:
                m_scr[...] = jnp.broadcast_to(sink_ref[...], m_scr.shape)
                l_scr[...] = jnp.ones_like(l_scr)
            acc[...] = jnp.zeros_like(acc)

        def step():
            qpos = qi * c.tq + lax.broadcasted_iota(jnp.int32, (c.tq, c.tk), 0)
            kpos = kb * c.tk + lax.broadcasted_iota(jnp.int32, (c.tq, c.tk), 1)
            mask = _mask(c.mode, c.max_dist, qpos, kpos)
            for j in range(c.nh):
                g = j // c.rep
                s = _dot(q_ref[:, j * c.dqk:(j + 1) * c.dqk], k_ref[:, g * c.dqk:(g + 1) * c.dqk], NT) * c.scale
                if mask is not None:
                    s = jnp.where(mask, s, -jnp.inf)
                m_prev = m_scr[:, j:j + 1]
                m_new = jnp.maximum(m_prev, jnp.max(s, axis=1, keepdims=True))
                alpha = jnp.exp(m_prev - m_new)
                p = jnp.exp(s - m_new)
                l_scr[:, j:j + 1] = alpha * l_scr[:, j:j + 1] + jnp.sum(p, axis=1, keepdims=True)
                acc[:, j * c.dv:(j + 1) * c.dv] = (
                    alpha * acc[:, j * c.dv:(j + 1) * c.dv] + _dot(p, v_ref[:, g * c.dv:(g + 1) * c.dv], NN))
                m_scr[:, j:j + 1] = m_new

        _when(valid, step)

        @pl.when(kj == steps - 1)
        def _():
            for j in range(c.nh):
                o_ref[:, j * c.dv:(j + 1) * c.dv] = (
                    acc[:, j * c.dv:(j + 1) * c.dv] / l_scr[:, j:j + 1]).astype(o_ref.dtype)
            lane = _lane((c.tq, LANES))
            lse_ref[...] = jnp.where(lane < c.nh, m_scr[...] + jnp.log(jnp.maximum(l_scr[...], 1e-37)), 0.0)

    in_specs = [
        pl.BlockSpec((c.tq, c.nh * c.dqk), lambda g, i, j: (i, c.qcol(g))),
        pl.BlockSpec((c.tk, c.nkv * c.dqk), lambda g, i, j: (_kv_block(c.mode, i, j)[0], c.kcol(g))),
        pl.BlockSpec((c.tk, c.nkv * c.dv), lambda g, i, j: (_kv_block(c.mode, i, j)[0], c.vcol(g))),
    ]
    args = [q, k, v]
    if sink is not None:
        in_specs.append(pl.BlockSpec((1, LANES), lambda g, i, j: (0, 0)))
        args.append(sink)
    return _pcall(
        body, name=name, dims=("parallel", "parallel", "arbitrary"), grid=(c.G, nq, steps),
        in_specs=in_specs,
        out_specs=[pl.BlockSpec((c.tq, c.nh * c.dv), lambda g, i, j: (i, c.ocol(g))),
                   pl.BlockSpec((c.tq, LANES), lambda g, i, j: (i, g))],
        out_shape=[jax.ShapeDtypeStruct((c.T, c.o_width), out_dtype),
                   jax.ShapeDtypeStruct((c.T, LANES * c.G), F32)],
        scratch_shapes=[pltpu.VMEM((c.tq, LANES), F32), pltpu.VMEM((c.tq, LANES), F32),
                        pltpu.VMEM((c.tq, c.nh * c.dv), F32)],
    )(*args)


def _attn_delta(cfg, o, do, *, name, w=None, lse=None, sink=None, tm=512):
    c = cfg
    tm = _tile(c.T, tm)
    width = c.nh * c.dv

    def body(*refs):
        refs = list(refs)
        o_ref, do_ref = refs[:2]
        rest = refs[2:]
        w_ref = rest.pop(0) if w is not None else None
        lse_ref, sink_ref = (rest.pop(0), rest.pop(0)) if sink is not None else (None, None)
        d_ref = rest.pop(0)
        prod = o_ref[...].astype(F32) * do_ref[...].astype(F32)
        cols = [jnp.sum(prod[:, j * c.dv:(j + 1) * c.dv], axis=1, keepdims=True) for j in range(c.nh)]
        delta = _cols_to_lanes(cols, tm)
        if w is not None:
            delta = delta * w_ref[...]
        d_ref[...] = delta
        if sink is not None:
            ds_ref = rest.pop(0)

            @pl.when(pl.program_id(1) == 0)
            def _():
                ds_ref[...] = jnp.zeros_like(ds_ref)

            lane = _lane((tm, LANES))
            ps = jnp.where(lane < c.nh, jnp.exp(sink_ref[...] - lse_ref[...]), 0.0)
            ds_ref[...] -= jnp.sum(ps * delta, axis=0, keepdims=True)

    stat = pl.BlockSpec((tm, LANES), lambda g, i: (i, g))
    in_specs = [pl.BlockSpec((tm, width), lambda g, i: (i, c.ocol(g)))] * 2
    args = [o, do]
    out_specs, out_shape = [stat], [jax.ShapeDtypeStruct((c.T, LANES * c.G), F32)]
    if w is not None:
        in_specs.append(stat)
        args.append(w)
    if sink is not None:
        assert c.G == 1
        in_specs += [stat, pl.BlockSpec((1, LANES), lambda g, i: (0, 0))]
        args += [lse, sink]
        out_specs.append(pl.BlockSpec((1, LANES), lambda g, i: (0, 0)))
        out_shape.append(jax.ShapeDtypeStruct((1, LANES), F32))
    out = _pcall(
        body, name=name, dims=("arbitrary", "arbitrary"), grid=(c.G, c.T // tm),
        in_specs=in_specs, out_specs=out_specs, out_shape=out_shape,
    )(*args)
    return out if sink is not None else (out[0], None)


def _attn_dq(cfg, q, k, v, do, lse, delta, *, name, init=None, out_dtype=F32):
    c = cfg
    nq, nk = c.T // c.tq, c.Tk // c.tk
    steps = _attn_steps(c.mode, nk, c.tq, c.tk)
    qw = c.nh * c.dqk

    def body(*refs):
        if init is None:
            q_ref, k_ref, v_ref, do_ref, lse_ref, d_ref, dq_ref, acc = refs
        else:
            q_ref, k_ref, v_ref, do_ref, lse_ref, d_ref, init_ref, dq_ref, acc = refs
        qi, kj = pl.program_id(1), pl.program_id(2)
        kb, valid = _kv_block(c.mode, qi, kj)

        @pl.when(kj == 0)
        def _():
            acc[...] = jnp.zeros_like(acc) if init is None else init_ref[...].astype(F32)

        def step():
            qpos = qi * c.tq + lax.broadcasted_iota(jnp.int32, (c.tq, c.tk), 0)
            kpos = kb * c.tk + lax.broadcasted_iota(jnp.int32, (c.tq, c.tk), 1)
            mask = _mask(c.mode, c.max_dist, qpos, kpos)
            for j in range(c.nh):
                g = j // c.rep
                kh = k_ref[:, g * c.dqk:(g + 1) * c.dqk]
                s = _dot(q_ref[:, j * c.dqk:(j + 1) * c.dqk], kh, NT) * c.scale
                if mask is not None:
                    s = jnp.where(mask, s, -jnp.inf)
                p = jnp.exp(s - lse_ref[:, j:j + 1])
                dp = _dot(do_ref[:, j * c.dv:(j + 1) * c.dv], v_ref[:, g * c.dv:(g + 1) * c.dv], NT)
                ds = p * (dp - d_ref[:, j:j + 1]) * c.scale
                acc[:, j * c.dqk:(j + 1) * c.dqk] += _dot(ds, kh, NN)

        _when(valid, step)

        @pl.when(kj == steps - 1)
        def _():
            dq_ref[...] = acc[...].astype(dq_ref.dtype)

    kvb = lambda i, j: _kv_block(c.mode, i, j)[0]
    qspec = pl.BlockSpec((c.tq, qw), lambda g, i, j: (i, c.qcol(g)))
    stat = pl.BlockSpec((c.tq, LANES), lambda g, i, j: (i, g))
    in_specs = [
        qspec,
        pl.BlockSpec((c.tk, c.nkv * c.dqk), lambda g, i, j: (kvb(i, j), c.kcol(g))),
        pl.BlockSpec((c.tk, c.nkv * c.dv), lambda g, i, j: (kvb(i, j), c.vcol(g))),
        pl.BlockSpec((c.tq, c.nh * c.dv), lambda g, i, j: (i, c.ocol(g))),
        stat, stat,
    ]
    args = [q, k, v, do, lse, delta]
    dq_spec = pl.BlockSpec((c.tq, qw), lambda g, i, j: (i, g))
    if init is not None:
        in_specs.append(dq_spec)
        args.append(init)
    return _pcall(
        body, name=name, dims=("parallel", "parallel", "arbitrary"), grid=(c.G, nq, steps),
        in_specs=in_specs, out_specs=dq_spec,
        out_shape=jax.ShapeDtypeStruct((c.T, c.G * qw), out_dtype),
        scratch_shapes=[pltpu.VMEM((c.tq, qw), F32)],
    )(*args)


def _attn_dkv(cfg, q, k, v, do, lse, delta, *, name, init=None, out_dtype=F32):
    c = cfg
    nq, nk = c.T // c.tq, c.Tk // c.tk
    steps = _attn_steps(c.mode, nq, c.tk, c.tq)
    kw, vw = c.nkv * c.dqk, c.nkv * c.dv

    def body(*refs):
        if init is None:
            q_ref, k_ref, v_ref, do_ref, lse_ref, d_ref, dk_ref, dv_ref, dk_acc, dv_acc = refs
        else:
            q_ref, k_ref, v_ref, do_ref, lse_ref, d_ref, ik_ref, iv_ref, dk_ref, dv_ref, dk_acc, dv_acc = refs
        ki, qj = pl.program_id(1), pl.program_id(2)
        qb, valid = _q_block(c.mode, ki, qj, nq)

        @pl.when(qj == 0)
        def _():
            dk_acc[...] = jnp.zeros_like(dk_acc) if init is None else ik_ref[...].astype(F32)
            dv_acc[...] = jnp.zeros_like(dv_acc) if init is None else iv_ref[...].astype(F32)

        def step():
            kpos = ki * c.tk + lax.broadcasted_iota(jnp.int32, (c.tk, c.tq), 0)
            qpos = qb * c.tq + lax.broadcasted_iota(jnp.int32, (c.tk, c.tq), 1)
            mask = _mask(c.mode, c.max_dist, qpos, kpos)
            lse_t = lse_ref[...].T
            d_t = d_ref[...].T
            for j in range(c.nh):
                g = j // c.rep
                qh = q_ref[:, j * c.dqk:(j + 1) * c.dqk]
                doh = do_ref[:, j * c.dv:(j + 1) * c.dv]
                s_t = _dot(k_ref[:, g * c.dqk:(g + 1) * c.dqk], qh, NT) * c.scale
                if mask is not None:
                    s_t = jnp.where(mask, s_t, -jnp.inf)
                p_t = jnp.exp(s_t - lse_t[j:j + 1, :])
                dv_acc[:, g * c.dv:(g + 1) * c.dv] += _dot(p_t, doh, NN)
                dp_t = _dot(v_ref[:, g * c.dv:(g + 1) * c.dv], doh, NT)
                ds_t = p_t * (dp_t - d_t[j:j + 1, :]) * c.scale
                dk_acc[:, g * c.dqk:(g + 1) * c.dqk] += _dot(ds_t, qh, NN)

        _when(valid, step)

        @pl.when(qj == steps - 1)
        def _():
            dk_ref[...] = dk_acc[...].astype(dk_ref.dtype)
            dv_ref[...] = dv_acc[...].astype(dv_ref.dtype)

    qbi = lambda i, j: _q_block(c.mode, i, j, nq)[0]
    stat = pl.BlockSpec((c.tq, LANES), lambda g, i, j: (qbi(i, j), g))
    in_specs = [
        pl.BlockSpec((c.tq, c.nh * c.dqk), lambda g, i, j: (qbi(i, j), c.qcol(g))),
        pl.BlockSpec((c.tk, kw), lambda g, i, j: (i, c.kcol(g))),
        pl.BlockSpec((c.tk, vw), lambda g, i, j: (i, c.vcol(g))),
        pl.BlockSpec((c.tq, c.nh * c.dv), lambda g, i, j: (qbi(i, j), c.ocol(g))),
        stat, stat,
    ]
    args = [q, k, v, do, lse, delta]
    dk_spec = pl.BlockSpec((c.tk, kw), lambda g, i, j: (i, g))
    dv_spec = pl.BlockSpec((c.tk, vw), lambda g, i, j: (i, g))
    if init is not None:
        in_specs += [dk_spec, dv_spec]
        args += list(init)
    return _pcall(
        body, name=name, dims=("parallel", "parallel", "arbitrary"), grid=(c.G, nk, steps),
        in_specs=in_specs, out_specs=[dk_spec, dv_spec],
        out_shape=[jax.ShapeDtypeStruct((c.Tk, c.G * kw), out_dtype),
                   jax.ShapeDtypeStruct((c.Tk, c.G * vw), out_dtype)],
        scratch_shapes=[pltpu.VMEM((c.tk, kw), F32), pltpu.VMEM((c.tk, vw), F32)],
    )(*args)


TN = (((0,), (0,)), ((), ()))


def _band_mask(c, i):
    row = lax.broadcasted_iota(jnp.int32, (BLOCK, 2 * BLOCK), 0)
    col = lax.broadcasted_iota(jnp.int32, (BLOCK, 2 * BLOCK), 1)
    d = BLOCK + row - col
    return (d >= 0) & (d <= c.max_dist) & ((col >= BLOCK) | (i > 0))


def _band_fwd(cfg, q, k, v, *, name, sink=None, out_dtype=F32):
    c = cfg
    assert c.mode == "band" and c.tq == c.tk == BLOCK and c.T == c.Tk
    nq = c.T // BLOCK

    def body(*refs):
        if sink is None:
            q_ref, kp_ref, kc_ref, vp_ref, vc_ref, o_ref, lse_ref = refs
        else:
            q_ref, kp_ref, kc_ref, vp_ref, vc_ref, sink_ref, o_ref, lse_ref = refs
        mask = _band_mask(c, pl.program_id(1))
        k2 = jnp.concatenate([kp_ref[...], kc_ref[...]], axis=0)
        v2 = jnp.concatenate([vp_ref[...], vc_ref[...]], axis=0)
        lses = []
        for j in range(c.nh):
            g = j // c.rep
            s = _dot(q_ref[:, j * c.dqk:(j + 1) * c.dqk], k2[:, g * c.dqk:(g + 1) * c.dqk], NT) * c.scale
            s = jnp.where(mask, s, -jnp.inf)
            m = jnp.max(s, axis=1, keepdims=True)
            if sink is not None:
                sk = sink_ref[:, j:j + 1]
                m = jnp.maximum(m, sk)
            p = jnp.exp(s - m)
            l = jnp.sum(p, axis=1, keepdims=True)
            if sink is not None:
                l = l + jnp.exp(sk - m)
            o_ref[:, j * c.dv:(j + 1) * c.dv] = (_dot(p, v2[:, g * c.dv:(g + 1) * c.dv], NN) / l).astype(o_ref.dtype)
            lses.append(m + jnp.log(l))
        lse_ref[...] = _cols_to_lanes(lses, BLOCK)

    prev = lambda i: jnp.maximum(i - 1, 0)
    kw, vw = c.nkv * c.dqk, c.nkv * c.dv
    in_specs = [
        pl.BlockSpec((BLOCK, c.nh * c.dqk), lambda g, i: (i, c.qcol(g))),
        pl.BlockSpec((BLOCK, kw), lambda g, i: (prev(i), c.kcol(g))),
        pl.BlockSpec((BLOCK, kw), lambda g, i: (i, c.kcol(g))),
        pl.BlockSpec((BLOCK, vw), lambda g, i: (prev(i), c.vcol(g))),
        pl.BlockSpec((BLOCK, vw), lambda g, i: (i, c.vcol(g))),
    ]
    args = [q, k, k, v, v]
    if sink is not None:
        in_specs.append(pl.BlockSpec((1, LANES), lambda g, i: (0, 0)))
        args.append(sink)
    return _pcall(
        body, name=name, dims=("parallel", "parallel"), grid=(c.G, nq), in_specs=in_specs,
        out_specs=[pl.BlockSpec((BLOCK, c.nh * c.dv), lambda g, i: (i, c.ocol(g))),
                   pl.BlockSpec((BLOCK, LANES), lambda g, i: (i, g))],
        out_shape=[jax.ShapeDtypeStruct((c.T, c.o_width), out_dtype),
                   jax.ShapeDtypeStruct((c.T, LANES * c.G), F32)],
    )(*args)


def _band_bwd(cfg, q, k, v, do, lse, delta, *, name, init=None):
    c = cfg
    assert c.mode == "band" and c.tq == c.tk == BLOCK and c.T == c.Tk
    nq = c.T // BLOCK
    qw, kw, vw = c.nh * c.dqk, c.nkv * c.dqk, c.nkv * c.dv

    def body(*refs):
        refs = list(refs)
        q_ref, kp_ref, kc_ref, vp_ref, vc_ref, do_ref, lse_ref, d_ref = refs[:8]
        iq_ref, ik_ref, iv_ref = refs[8:11] if init is not None else (None, None, None)
        dq_ref, dk_ref, dv_ref, dk_c, dv_c = refs[-5:]
        n = pl.program_id(1)

        def plus(val, ref, sl):
            return val if ref is None else val + ref[:, sl]

        @pl.when(n == 0)
        def _():
            dk_c[...] = jnp.zeros_like(dk_c)
            dv_c[...] = jnp.zeros_like(dv_c)

        @pl.when(n < nq)
        def _():
            mask = _band_mask(c, n)
            k2 = jnp.concatenate([kp_ref[...], kc_ref[...]], axis=0)
            v2 = jnp.concatenate([vp_ref[...], vc_ref[...]], axis=0)
            dk2, dv2 = [None] * c.nkv, [None] * c.nkv
            for j in range(c.nh):
                g = j // c.rep
                qs, os_ = slice(j * c.dqk, (j + 1) * c.dqk), slice(j * c.dv, (j + 1) * c.dv)
                qh, doh = q_ref[:, qs], do_ref[:, os_]
                kh, vh = k2[:, g * c.dqk:(g + 1) * c.dqk], v2[:, g * c.dv:(g + 1) * c.dv]
                s = jnp.where(mask, _dot(qh, kh, NT) * c.scale, -jnp.inf)
                p = jnp.exp(s - lse_ref[:, j:j + 1])
                ds = p * (_dot(doh, vh, NT) - d_ref[:, j:j + 1]) * c.scale
                dq_ref[:, qs] = plus(_dot(ds, kh, NN), iq_ref, qs)
                dvh, dkh = _dot(p, doh, TN), _dot(ds, qh, TN)
                dv2[g] = dvh if dv2[g] is None else dv2[g] + dvh
                dk2[g] = dkh if dk2[g] is None else dk2[g] + dkh
            for g in range(c.nkv):
                ks, vs = slice(g * c.dqk, (g + 1) * c.dqk), slice(g * c.dv, (g + 1) * c.dv)
                dk_ref[:, ks] = plus(dk_c[:, ks] + dk2[g][:BLOCK], ik_ref, ks)
                dv_ref[:, vs] = plus(dv_c[:, vs] + dv2[g][:BLOCK], iv_ref, vs)
                dk_c[:, ks] = dk2[g][BLOCK:]
                dv_c[:, vs] = dv2[g][BLOCK:]

        @pl.when(n == nq)
        def _():
            dk_ref[...] = plus(dk_c[...], ik_ref, slice(None))
            dv_ref[...] = plus(dv_c[...], iv_ref, slice(None))

    cur = lambda n: jnp.minimum(n, nq - 1)
    prev = lambda n: jnp.maximum(cur(n) - 1, 0)
    out_blk = lambda n: jnp.maximum(n - 1, 0)
    stat = pl.BlockSpec((BLOCK, LANES), lambda g, n: (cur(n), g))
    dq_spec = pl.BlockSpec((BLOCK, qw), lambda g, n: (cur(n), g))
    dk_spec = pl.BlockSpec((BLOCK, kw), lambda g, n: (out_blk(n), g))
    dv_spec = pl.BlockSpec((BLOCK, vw), lambda g, n: (out_blk(n), g))
    in_specs = [
        pl.BlockSpec((BLOCK, qw), lambda g, n: (cur(n), c.qcol(g))),
        pl.BlockSpec((BLOCK, kw), lambda g, n: (prev(n), c.kcol(g))),
        pl.BlockSpec((BLOCK, kw), lambda g, n: (cur(n), c.kcol(g))),
        pl.BlockSpec((BLOCK, vw), lambda g, n: (prev(n), c.vcol(g))),
        pl.BlockSpec((BLOCK, vw), lambda g, n: (cur(n), c.vcol(g))),
        pl.BlockSpec((BLOCK, c.nh * c.dv), lambda g, n: (cur(n), c.ocol(g))),
        stat, stat,
    ]
    args = [q, k, k, v, v, do, lse, delta]
    if init is not None:
        in_specs += [dq_spec, dk_spec, dv_spec]
        args += list(init)
    return _pcall(
        body, name=name, dims=("parallel", "arbitrary"), grid=(c.G, nq + 1), in_specs=in_specs,
        out_specs=[dq_spec, dk_spec, dv_spec],
        out_shape=[_sds((c.T, c.G * qw)), _sds((c.T, c.G * kw)), _sds((c.T, c.G * vw))],
        scratch_shapes=[pltpu.VMEM((BLOCK, kw), F32), pltpu.VMEM((BLOCK, vw), F32)],
    )(*args)


def _causal_pairs(n, kv_major):
    pairs =[(i, j) for j in range(n) for i in range(j, n)] if kv_major else [(i, j) for i in range(n) for j in range(i + 1)]
    return jnp.asarray(np.array([p[0] for p in pairs], np.int32)), jnp.asarray(np.array([p[1] for p in pairs], np.int32))


def _causal_mask(t):
    return lax.broadcasted_iota(jnp.int32, (t, t), 0) >= lax.broadcasted_iota(jnp.int32, (t, t), 1)


def _carrying(body, n_in, n_out, n_scratch, grid, carry):
    if carry is None:
        return body
    G, P = grid

    def wrapped(*refs):
        refs = list(refs)
        prefetch, refs = refs[:2], refs[2:]
        ins, src = refs[:n_in], refs[n_in]
        outs, out = refs[n_in + 1:n_in + 1 + n_out], refs[n_in + 1 + n_out]
        scratch, sems = refs[n_in + 2 + n_out:n_in + 2 + n_out + n_scratch], refs[n_in + 2 + n_out + n_scratch:]
        step = pl.program_id(0) * P + pl.program_id(1)
        carry.run([src, out] + sems, step, G * P, at_end=False)
        body(*prefetch, *ins, *outs, *scratch)
        carry.run([src, out] + sems, step, G * P, at_end=True)

    return wrapped


def _carry_specs(carry):
    if carry is None:
        return [], [], [], [], []
    any_space = pl.BlockSpec(memory_space=pl.ANY)
    return [any_space], [any_space], [carry.out_shape], list(carry.sems), [carry.src]


def _causal_fwd(cfg, q, k, v, *, name, out_dtype=F32, carry=None):
    c = cfg
    assert c.mode == "causal" and c.tq == c.tk and c.T == c.Tk
    t, n = c.tq, c.T // c.tq
    qi_tab, kj_tab = _causal_pairs(n, kv_major=False)
    n_pairs = int(qi_tab.shape[0])

    def body(qi_ref, kj_ref, q_ref, k_ref, v_ref, o_ref, lse_ref, m_scr, l_scr, acc):
        pair = pl.program_id(1)
        qi, kj = qi_ref[pair], kj_ref[pair]

        @pl.when(kj == 0)
        def _():
            m_scr[...] = jnp.full_like(m_scr, NEG_BIG)
            l_scr[...] = jnp.zeros_like(l_scr)
            acc[...] = jnp.zeros_like(acc)

        def step(diagonal):
            mask = _causal_mask(t) if diagonal else None
            for j in range(c.nh):
                g = j // c.rep
                s = _dot(q_ref[:, j * c.dqk:(j + 1) * c.dqk], k_ref[:, g * c.dqk:(g + 1) * c.dqk], NT) * c.scale
                if diagonal:
                    s = jnp.where(mask, s, -jnp.inf)
                m_prev = m_scr[j]
                m_new = jnp.maximum(m_prev, jnp.max(s, axis=1, keepdims=True))
                alpha = jnp.exp(m_prev - m_new)
                p = jnp.exp(s - m_new)
                l_scr[j] = alpha * l_scr[j] + jnp.sum(p, axis=1, keepdims=True)
                acc[j] = alpha * acc[j] + _dot(p, v_ref[:, g * c.dv:(g + 1) * c.dv], NN)
                m_scr[j] = m_new

        pl.when(kj == qi)(lambda: step(True))
        pl.when(kj != qi)(lambda: step(False))

        @pl.when(kj == qi)
        def _():
            lses = []
            for j in range(c.nh):
                o_ref[:, j * c.dv:(j + 1) * c.dv] = (acc[j] / l_scr[j]).astype(o_ref.dtype)
                lses.append(m_scr[j] + jnp.log(l_scr[j]))
            lse_ref[...] = _cols_to_lanes(lses, t)

    x_in, x_out, x_shapes, x_scratch, x_args = _carry_specs(carry)
    grid_spec = pltpu.PrefetchScalarGridSpec(
        num_scalar_prefetch=2, grid=(c.G, n_pairs),
        in_specs=[pl.BlockSpec((t, c.nh * c.dqk), lambda g, p, qi, kj: (qi[p], c.qcol(g))),
                  pl.BlockSpec((t, c.nkv * c.dqk), lambda g, p, qi, kj: (kj[p], c.kcol(g))),
                  pl.BlockSpec((t, c.nkv * c.dv), lambda g, p, qi, kj: (kj[p], c.vcol(g)))] + x_in,
        out_specs=[pl.BlockSpec((t, c.nh * c.dv), lambda g, p, qi, kj: (qi[p], c.ocol(g))),
                   pl.BlockSpec((t, LANES), lambda g, p, qi, kj: (qi[p], g))] + x_out,
        scratch_shapes=[pltpu.VMEM((c.nh, t, 1), F32), pltpu.VMEM((c.nh, t, 1), F32),
                        pltpu.VMEM((c.nh, t, c.dv), F32)] + x_scratch)
    return _pcall(
        _carrying(body, 3, 2, 3, (c.G, n_pairs), carry), name=name,
        dims=("arbitrary", "arbitrary") if carry is not None else ("parallel", "arbitrary"), grid_spec=grid_spec,
        out_shape=[jax.ShapeDtypeStruct((c.T, c.o_width), out_dtype),
                   jax.ShapeDtypeStruct((c.T, LANES * c.G), F32)] + x_shapes,
    )(qi_tab, kj_tab, q, k, v, *x_args)


def _causal_bwd(cfg, q, k, v, do, lse, delta, *, name, carry=None):
    c = cfg
    assert c.mode == "causal" and c.tq == c.tk and c.T == c.Tk
    t, n = c.tq, c.T // c.tq
    qw, kw, vw = c.nh * c.dqk, c.nkv * c.dqk, c.nkv * c.dv
    qi_tab, kj_tab = _causal_pairs(n, kv_major=True)

    def body(qi_ref, kj_ref, q_ref, k_ref, v_ref, do_ref, lse_ref, d_ref, dq_ref, dk_ref, dv_ref, dk_acc, dv_acc):
        pair = pl.program_id(1)
        qi, kj = qi_ref[pair], kj_ref[pair]

        @pl.when(pair == 0)
        def _():
            dq_ref[...] = jnp.zeros_like(dq_ref)

        @pl.when(qi == kj)
        def _():
            dk_acc[...] = jnp.zeros_like(dk_acc)
            dv_acc[...] = jnp.zeros_like(dv_acc)

        rows = pl.ds(pl.multiple_of(qi * t, t), t)

        def step(diagonal):
            mask = _causal_mask(t) if diagonal else None
            for j in range(c.nh):
                g = j // c.rep
                qs, ks, vs = (slice(j * c.dqk, (j + 1) * c.dqk), slice(g * c.dqk, (g + 1) * c.dqk),
                              slice(g * c.dv, (g + 1) * c.dv))
                qh, doh, kh = q_ref[:, qs], do_ref[:, j * c.dv:(j + 1) * c.dv], k_ref[:, ks]
                s = _dot(qh, kh, NT) * c.scale
                if diagonal:
                    s = jnp.where(mask, s, -jnp.inf)
                p = jnp.exp(s - lse_ref[:, j:j + 1])
                ds = p * (_dot(doh, v_ref[:, vs], NT) - d_ref[:, j:j + 1]) * c.scale
                dq_ref[rows, qs] += _dot(ds, kh, NN)
                dv_acc[:, vs] += _dot(p, doh, TN)
                dk_acc[:, ks] += _dot(ds, qh, TN)

        pl.when(qi == kj)(lambda: step(True))
        pl.when(qi != kj)(lambda: step(False))

        @pl.when(qi == n - 1)
        def _():
            dk_ref[...] = dk_acc[...]
            dv_ref[...] = dv_acc[...]

    stat = pl.BlockSpec((t, LANES), lambda g, p, qi, kj: (qi[p], g))
    n_pairs = int(qi_tab.shape[0])
    x_in, x_out, x_shapes, x_scratch, x_args = _carry_specs(carry)
    grid_spec = pltpu.PrefetchScalarGridSpec(
        num_scalar_prefetch=2, grid=(c.G, n_pairs),
        in_specs=[pl.BlockSpec((t, qw), lambda g, p, qi, kj: (qi[p], c.qcol(g))),
                  pl.BlockSpec((t, kw), lambda g, p, qi, kj: (kj[p], c.kcol(g))),
                  pl.BlockSpec((t, vw), lambda g, p, qi, kj: (kj[p], c.vcol(g))),
                  pl.BlockSpec((t, c.nh * c.dv), lambda g, p, qi, kj: (qi[p], c.ocol(g))),
                  stat, stat] + x_in,
        out_specs=[pl.BlockSpec((c.T, qw), lambda g, p, qi, kj: (0, g)),
                   pl.BlockSpec((t, kw), lambda g, p, qi, kj: (kj[p], g)),
                   pl.BlockSpec((t, vw), lambda g, p, qi, kj: (kj[p], g))] + x_out,
        scratch_shapes=[pltpu.VMEM((t, kw), F32), pltpu.VMEM((t, vw), F32)] + x_scratch)
    return _pcall(
        _carrying(body, 6, 3, 2, (c.G, n_pairs), carry), name=name,
        dims=("arbitrary", "arbitrary") if carry is not None else ("parallel", "arbitrary"), grid_spec=grid_spec,
        out_shape=[_sds((c.T, c.G * qw)), _sds((c.T, c.G * kw)), _sds((c.T, c.G * vw))] + x_shapes,
    )(qi_tab, kj_tab, q, k, v, do, lse, delta, *x_args)


def _rowwise(body, ins, outs, *, name, rows, tm=512, accs=(), scratch=()):
    tm = _tile(rows, tm)

    def spec(a):
        if a.shape[0] == 1:
            return pl.BlockSpec((1, a.shape[1]), lambda i: (0, 0))
        d = rows // a.shape[0]
        assert d * a.shape[0] == rows and tm % d == 0
        return pl.BlockSpec((tm // d, a.shape[1]), lambda i: (i, 0))

    return _pcall(
        functools.partial(body, tm), name=name, dims=("arbitrary" if accs else "parallel",), grid=(rows // tm,),
        in_specs=[spec(a) for a in ins], out_specs=[spec(a) for a in outs], out_shape=list(outs),
        scratch_shapes=list(scratch),
    )(*ins)


def _sds(shape, dtype=F32):
    return jax.ShapeDtypeStruct(shape, dtype)


def _acc_rows(ref, val):
    @pl.when(pl.program_id(0) == 0)
    def _():
        ref[...] = jnp.zeros_like(ref)

    ref[...] += jnp.sum(val, axis=0, keepdims=True)


Z_QA, Z_KA, Z_VA, Z_CQ, Z_CKV, Z_KR, Z_END = 0, 512, 640, 768, 1152, 1408, 1536


def _l0_prep(z, tabs, q_norm, kv_norm, *, name):
    S = z.shape[0]

    def body(tm, z_ref, c64, s64, ck, sk, gq, gkv, qa_o, ka_o, va_o, cq_o, ckv_o, kr_o):
        for i in range(4):
            sl = slice(Z_QA + i * LANES, Z_QA + (i + 1) * LANES)
            qa_o[:, i * LANES:(i + 1) * LANES] = _rope_chunk(z_ref[:, sl], c64[...], s64[...], 32).astype(qa_o.dtype)
        ka_o[...] = _rope_chunk(z_ref[:, Z_KA:Z_VA], c64[...], s64[...], 32).astype(ka_o.dtype)
        va_o[...] = z_ref[:, Z_VA:Z_CQ].astype(va_o.dtype)
        cq_o[...] = (_rms_parts(z_ref[:, Z_CQ:Z_CKV])[0] * gq[...]).astype(cq_o.dtype)
        ckv_o[...] = (_rms_parts(z_ref[:, Z_CKV:Z_KR])[0] * gkv[...]).astype(ckv_o.dtype)
        kr_o[...] = _rope_chunk(z_ref[:, Z_KR:Z_END], ck[...], sk[...], 16)

    outs = [_sds((S, 512), MXU_DTYPE), _sds((S, 128), MXU_DTYPE), _sds((S, 128), MXU_DTYPE),
            _sds((S, MLA_Q_RANK), MXU_DTYPE), _sds((S, MLA_KV_RANK), MXU_DTYPE), _sds((S, LANES))]
    ins = [z, tabs["c64"], tabs["s64"], tabs["ck"], tabs["sk"], q_norm.reshape(1, -1), kv_norm.reshape(1, -1)]
    return _rowwise(body, ins, outs, name=name, rows=S)


def _l0_prep_bwd(z, tabs, q_norm, kv_norm, dqa, dka, dva, dcq, dckv, dkr, *, name):
    S = z.shape[0]

    def body(tm, z_ref, c64, s64, ck, sk, gq, gkv, dqa_r, dka_r, dva_r, dcq_r, dckv_r, dkr_r, dz_o, dgq_o, dgkv_o):
        for i in range(4):
            sl = slice(i * LANES, (i + 1) * LANES)
            dz_o[:, sl] = _rope_chunk(dqa_r[:, sl].astype(F32), c64[...], -s64[...], 32).astype(dz_o.dtype)
        dz_o[:, Z_KA:Z_VA] = _rope_chunk(dka_r[...].astype(F32), c64[...], -s64[...], 32).astype(dz_o.dtype)
        dz_o[:, Z_VA:Z_CQ] = dva_r[...].astype(dz_o.dtype)
        dx, dgp = _rms_bwd_rows(z_ref[:, Z_CQ:Z_CKV], gq[...], dcq_r[...].astype(F32))
        dz_o[:, Z_CQ:Z_CKV] = dx.astype(dz_o.dtype)
        _acc_rows(dgq_o, dgp)
        dx, dgp = _rms_bwd_rows(z_ref[:, Z_CKV:Z_KR], gkv[...], dckv_r[...].astype(F32))
        dz_o[:, Z_CKV:Z_KR] = dx.astype(dz_o.dtype)
        _acc_rows(dgkv_o, dgp)
        dz_o[:, Z_KR:Z_END] = _rope_chunk(dkr_r[...], ck[...], -sk[...], 16).astype(dz_o.dtype)

    outs = [_sds((S, Z_END), MXU_DTYPE), _sds((1, MLA_Q_RANK)), _sds((1, MLA_KV_RANK))]
    ins = [z, tabs["c64"], tabs["s64"], tabs["ck"], tabs["sk"], q_norm.reshape(1, -1), kv_norm.reshape(1, -1),
           dqa, dka, dva, dcq, dckv, dkr]
    return _rowwise(body, ins, outs, name=name, rows=S, accs=(1, 2))


def _mla_prep(qb, kvb, kr, tabs, *, name):
    S = qb.shape[0]

    def body(tm, qb_r, kvb_r, kr_r, cm, sm, q_o, k_o, v_o):
        lane = _lane((tm, LANES))
        kr_at_64 = pltpu.roll(kr_r[...], 64, 1)
        for h in range(MLA_HEADS):
            sl = slice(h * LANES, (h + 1) * LANES)
            q_o[:, sl] = _rope_chunk(qb_r[:, sl], cm[...], sm[...], 16).astype(q_o.dtype)
            k_o[:, sl] = jnp.where(lane < 64, kvb_r[:, sl], kr_at_64).astype(k_o.dtype)
        for p in range(MLA_HEADS // 2):
            even = pltpu.roll(kvb_r[:, (2 * p) * LANES:(2 * p + 1) * LANES], 64, 1)
            odd = kvb_r[:, (2 * p + 1) * LANES:(2 * p + 2) * LANES]
            v_o[:, p * LANES:(p + 1) * LANES] = jnp.where(lane < 64, even, odd).astype(v_o.dtype)

    outs = [_sds((S, 1024), MXU_DTYPE), _sds((S, 1024), MXU_DTYPE), _sds((S, 512), MXU_DTYPE)]
    return _rowwise(body, [qb, kvb, kr, tabs["cm"], tabs["sm"]], outs, name=name, rows=S)


def _mla_prep_bwd(dq, dk, dv, tabs, *, name):
    S = dq.shape[0]

    def body(tm, dq_r, dk_r, dv_r, cm, sm, dqb_o, dkvb_o, dkr_o):
        lane = _lane((tm, LANES))
        dkr = jnp.zeros((tm, LANES), F32)
        for h in range(MLA_HEADS):
            sl = slice(h * LANES, (h + 1) * LANES)
            dqb_o[:, sl] = _rope_chunk(dq_r[:, sl].astype(F32), cm[...], -sm[...], 16).astype(dqb_o.dtype)
            dkh = dk_r[:, sl].astype(F32)
            dvp = dv_r[:, (h // 2) * LANES:(h // 2 + 1) * LANES].astype(F32)
            dvh = pltpu.roll(dvp, 64, 1) if h % 2 == 0 else dvp
            dkvb_o[:, sl] = jnp.where(lane < 64, dkh, dvh).astype(dkvb_o.dtype)
            dkr = dkr + pltpu.roll(dkh, 64, 1)
        dkr_o[...] = jnp.where(lane < MLA_ROPE, dkr, 0.0)

    outs = [_sds((S, 1024), MXU_DTYPE), _sds((S, 1024), MXU_DTYPE), _sds((S, LANES))]
    return _rowwise(body, [dq, dk, dv, tabs["cm"], tabs["sm"]], outs, name=name, rows=S)


DILATIONS = tuple(d for _, d in DIL_PATTERNS)
QKV_CHUNKS = 8


def _to_branch(nat, c0, chunks, out_ref, d, rows):
    width = chunks * LANES
    for r in range(d):
        tok = pl.ds(r, rows // d, stride=d) if d > 1 else slice(None)
        for c in range(chunks):
            out_ref[:, r * width + c * LANES:r * width + (c + 1) * LANES] = nat[c0 + c, tok, :].astype(out_ref.dtype)


def _from_branch(in_ref, nat, c0, chunks, d, rows, add=False):
    width = chunks * LANES
    for r in range(d):
        tok = pl.ds(r, rows // d, stride=d) if d > 1 else slice(None)
        for c in range(chunks):
            val = in_ref[:, r * width + c * LANES:r * width + (c + 1) * LANES].astype(F32)
            nat[c0 + c, tok, :] = nat[c0 + c, tok, :] + val if add else val


def _branch_sds(S, width, d, dtype):
    return _sds((S // d, d * width), dtype)


def _l1_prep(qkv, tabs, *, name):
    S = qkv.shape[0]

    def body(tm, x_r, c64, s64, *rest):
        outs, nat = rest[:-1], rest[-1]
        for i in range(QKV_CHUNKS):
            sl = slice(i * LANES, (i + 1) * LANES)
            nat[i] = _rope_chunk(x_r[:, sl], c64[...], s64[...], 32)
            nat[QKV_CHUNKS + i] = _rope_chunk(x_r[:, 1024 + i * LANES:1024 + (i + 1) * LANES], c64[...], s64[...], 32)
            nat[2 * QKV_CHUNKS + i] = x_r[:, 2048 + i * LANES:2048 + (i + 1) * LANES]
        for b, d in enumerate(DILATIONS):
            for t in range(3):
                _to_branch(nat, t * QKV_CHUNKS, QKV_CHUNKS, outs[3 * b + t], d, tm)

    outs = [_branch_sds(S, 1024, d, MXU_DTYPE) for d in DILATIONS for _ in range(3)]
    got = _rowwise(body, [qkv, tabs["c64"], tabs["s64"]], outs, name=name, rows=S,
                   scratch=[pltpu.VMEM((3 * QKV_CHUNKS, _tile(S, 512), LANES), F32)])
    return {d: tuple(got[3 * b:3 * b + 3]) for b, d in enumerate(DILATIONS)}


def _l1_prep_bwd(grads, tabs, *, name):
    S = grads[1][0].shape[0]

    def body(tm, *rest):
        ins, (c64, s64, o, nat) = rest[:9], rest[9:]
        for b, d in enumerate(DILATIONS):
            for t in range(3):
                _from_branch(ins[3 * b + t], nat, t * QKV_CHUNKS, QKV_CHUNKS, d, tm, add=b > 0)
        for i in range(QKV_CHUNKS):
            sl = slice(i * LANES, (i + 1) * LANES)
            o[:, sl] = _rope_chunk(nat[i], c64[...], -s64[...], 32).astype(o.dtype)
            o[:, 1024 + i * LANES:1024 + (i + 1) * LANES] = _rope_chunk(
                nat[QKV_CHUNKS + i], c64[...], -s64[...], 32).astype(o.dtype)
            o[:, 2048 + i * LANES:2048 + (i + 1) * LANES] = nat[2 * QKV_CHUNKS + i].astype(o.dtype)

    ins = [g for d in DILATIONS for g in grads[d]] + [tabs["c64"], tabs["s64"]]
    return _rowwise(body, ins, [_sds((S, 3072), MXU_DTYPE)], name=name, rows=S, tm=256,
                    scratch=[pltpu.VMEM((3 * QKV_CHUNKS, _tile(S, 256), LANES), F32)])[0]


def _sigmoid(x):
    return 1.0 / (1.0 + jnp.exp(-x))


def _swiglu(gate, up, *, name):
    def body(tm, g_r, u_r, o):
        g = g_r[...]
        o[...] = (g * _sigmoid(g) * u_r[...]).astype(o.dtype)

    return _rowwise(body, [gate, up], [_sds(gate.shape, MXU_DTYPE)], name=name, rows=gate.shape[0], tm=256)[0]


def _swiglu_bwd(gate, up, dact, *, name):
    def body(tm, g_r, u_r, d_r, dg_o, du_o):
        g, d = g_r[...], d_r[...]
        sg = _sigmoid(g)
        dg_o[...] = (d * u_r[...] * (sg * (1.0 + g * (1.0 - sg)))).astype(dg_o.dtype)
        du_o[...] = (d * g * sg).astype(du_o.dtype)

    outs = [_sds(gate.shape, MXU_DTYPE)] * 2
    return _rowwise(body, [gate, up, dact], outs, name=name, rows=gate.shape[0], tm=256)


def _head_pair_weights(w, c, rows):
    return jnp.where(_lane((rows, LANES)) < HEAD_DIM, w[:, 2 * c:2 * c + 1], w[:, 2 * c + 1:2 * c + 2])


def _merge(outs_by_d, lses_by_d, *, name):
    S = outs_by_d[1].shape[0]
    far = DILATIONS[1:]

    def body(tm, o1, o4, o16, l1, l4, l16, o_o, w1_o, w4_o, w16_o, nat_o, nat_l):
        for b, (o_r, l_r, d) in enumerate(zip((o4, o16), (l4, l16), far)):
            _from_branch(o_r, nat_o, b * QKV_CHUNKS, QKV_CHUNKS, d, tm)
            _from_branch(l_r, nat_l, b, 1, d, tm)
        ls = [l1[...], nat_l[0], nat_l[1]]
        m = jnp.maximum(jnp.maximum(ls[0], ls[1]), ls[2])
        es = [jnp.exp(l - m) for l in ls]
        tot = es[0] + es[1] + es[2]
        ws = [e / tot for e in es]
        for w_o, w in zip((w1_o, w4_o, w16_o), ws):
            w_o[...] = w
        for c in range(QKV_CHUNKS):
            sl = slice(c * LANES, (c + 1) * LANES)
            parts = (o1[:, sl], nat_o[c], nat_o[QKV_CHUNKS + c])
            o_o[:, sl] = sum(_head_pair_weights(w, c, tm) * part for w, part in zip(ws, parts))

    ins = [outs_by_d[d] for d in DILATIONS] + [lses_by_d[d] for d in DILATIONS]
    outs = [_sds((S, 1024))] + [_sds((S, LANES))] * 3
    rows = _tile(S, 256)
    return _rowwise(body, ins, outs, name=name, rows=S, tm=256,
                    scratch=[pltpu.VMEM((2 * QKV_CHUNKS, rows, LANES), F32), pltpu.VMEM((2, rows, LANES), F32)])


def _merge_bwd(do, o, ws, *, name):
    S = do.shape[0]

    def body(tm, do_r, o_r, w1, w4, w16, d1, d4, d16, e1, e4, e16, nat, nat_l):
        prod = do_r[...] * o_r[...]
        sums = _cols_to_lanes([jnp.sum(prod[:, j * HEAD_DIM:(j + 1) * HEAD_DIM], axis=1, keepdims=True)
                               for j in range(DIL_HEADS)], tm)
        for w_r, d_o, e_o, d in zip((w1, w4, w16), (d1, d4, d16), (e1, e4, e16), DILATIONS):
            w = w_r[...]
            nat_l[0] = w * sums
            _to_branch(nat_l, 0, 1, e_o, d, tm)
            for c in range(QKV_CHUNKS):
                nat[c] = _head_pair_weights(w, c, tm) * do_r[:, c * LANES:(c + 1) * LANES]
            _to_branch(nat, 0, QKV_CHUNKS, d_o, d, tm)

    outs = [_branch_sds(S, 1024, d, MXU_DTYPE) for d in DILATIONS] + [_branch_sds(S, LANES, d, F32) for d in DILATIONS]
    rows = _tile(S, 256)
    got = _rowwise(body, [do, o] + [ws[d] for d in DILATIONS], outs, name=name, rows=S, tm=256,
                   scratch=[pltpu.VMEM((QKV_CHUNKS, rows, LANES), F32), pltpu.VMEM((1, rows, LANES), F32)])
    return dict(zip(DILATIONS, got[:3])), dict(zip(DILATIONS, got[3:]))


def _loss_head(x, g, target, *, name):
    S, D = x.shape

    def body(tm, x_r, g_r, t_r, dx_o, dg_o, sq_o):
        xf = x_r[...]
        xhat, _ = _rms_parts(xf)
        err = xhat * g_r[...] - t_r[...]
        dx, dgp = _rms_bwd_rows(xf, g_r[...], err * (1.0 / D))
        dx_o[...] = dx
        _acc_rows(dg_o, dgp)
        _acc_rows(sq_o, err * err)

    return _rowwise(body, [x, g.reshape(1, D), target], [_sds((S, D)), _sds((1, D)), _sds((1, D))],
                    name=name, rows=S, accs=(1, 2))


def _adamw(w, g, m, v, *, name):
    c1 = 1.0 - ADAM_B1 ** ADAM_STEP
    c2 = 1.0 - ADAM_B2 ** ADAM_STEP

    def body(tm, w_r, g_r, m_r, v_r, d_o, m_o, v_o):
        g = g_r[...]
        m_new = ADAM_B1 * m_r[...] + (1.0 - ADAM_B1) * g
        v_new = ADAM_B2 * v_r[...] + (1.0 - ADAM_B2) * (g * g)
        m_o[...] = m_new
        v_o[...] = v_new
        d_o[...] = -ADAM_LR * ((m_new / c1) / (jnp.sqrt(v_new / c2) + ADAM_EPS) + ADAM_WD * w_r[...])

    return _rowwise(body, [w, g, m, v], [_sds(w.shape)] * 3, name=name, rows=w.shape[0], tm=256)


SUM_ROW_TILE = 256


def _sum_cores(grads, theirs, half_index, *, name):
    _, R, C = grads.shape
    h = R // 2
    nb = h // SUM_ROW_TILE

    def body(c_ref, g_ref, t_ref, o_ref):
        o_ref[...] = (g_ref[...].astype(F32) + t_ref[...].astype(F32)).astype(o_ref.dtype)

    grid_spec = pltpu.PrefetchScalarGridSpec(
        num_scalar_prefetch=1, grid=(4, nb),
        in_specs=[pl.BlockSpec((1, SUM_ROW_TILE, C), lambda k, i, c_ref: (k, c_ref[0] * nb + i, 0)),
                  pl.BlockSpec((1, SUM_ROW_TILE, C), lambda k, i, c_ref: (k, i, 0))],
        out_specs=pl.BlockSpec((1, SUM_ROW_TILE, C), lambda k, i, c_ref: (k, i, 0)))
    return _pcall(body, name=name, dims=("parallel", "parallel"), grid_spec=grid_spec,
                  out_shape=_sds((4, h, C), grads.dtype))(half_index, grads, theirs)


def _sum_chips(parts, half_index, *, name):
    _, h, C = parts.shape
    nb = h // SUM_ROW_TILE

    def body(c_ref, p_ref, o_ref):
        p = [p_ref[k].astype(F32) for k in range(4)]
        o_ref[...] = ((p[0] + p[1]) + p[2]) + p[3]

    grid_spec = pltpu.PrefetchScalarGridSpec(
        num_scalar_prefetch=1, grid=(nb,),
        in_specs=[pl.BlockSpec((4, SUM_ROW_TILE, C), lambda i, c_ref: (0, i, 0))],
        out_specs=pl.BlockSpec((SUM_ROW_TILE, C), lambda i, c_ref: (c_ref[0] * nb + i, 0)))
    return _pcall(body, name=name, dims=("parallel",), grid_spec=grid_spec,
                  out_shape=_sds((2 * h, C)))(half_index, parts)


def _position():
    return lax.axis_index("x"), lax.axis_index("y"), lax.axis_index("c")


def _chip_peers(x, y):
    return [(1 - x, y), (x, 1 - y), (1 - x, 1 - y)]


_HBM = pl.BlockSpec(memory_space=pltpu.HBM)
LOCAL_COPY_CHUNKS = 8


def _local_copies(src_ref, dst_ref, sems):
    rows = src_ref.shape[0] // LOCAL_COPY_CHUNKS
    assert rows * LOCAL_COPY_CHUNKS == src_ref.shape[0]
    return [pltpu.make_async_copy(src_ref.at[pl.ds(i * rows, rows)], dst_ref.at[pl.ds(i * rows, rows)], sems.at[i])
            for i in range(LOCAL_COPY_CHUNKS)]


class _Exchange:
    def __init__(self, src, out_shape, sems, stages):
        self.src, self.out_shape, self.sems, self.stages = src, out_shape, sems, stages

    def run(self, refs, step, n_steps, at_end):
        for fraction, fn in self.stages:
            if (fraction == 1.0) == at_end:
                pl.when(step == int(round(fraction * (n_steps - 1))))(functools.partial(fn, *refs))


def _run_exchange(ex, *, name):
    def body(*refs):
        for _, fn in ex.stages:
            fn(*refs)

    return pl.pallas_call(
        body, name=name, in_specs=[_HBM], out_specs=_HBM, out_shape=ex.out_shape, scratch_shapes=list(ex.sems),
    )(ex.src)


def _gather_exchange(src):
    R, C = src.shape
    h = R // 2

    def plan(src_ref, out_ref, send_sems, recv_sems, local_sems):
        x, y, c = _position()
        me = 2 * x + y
        peers = _chip_peers(x, y)
        mine, other = pl.ds(c * h, h), pl.ds((1 - c) * h, h)

        def copy(sem, src_part, dst_part, device):
            return pltpu.make_async_remote_copy(
                src_ref=src_part, dst_ref=dst_part, send_sem=send_sems.at[sem], recv_sem=recv_sems.at[sem],
                device_id=device, device_id_type=MESH)

        landed = [out_ref.at[2 * px + py, mine] for px, py in peers]
        theirs = [out_ref.at[2 * px + py, other] for px, py in peers]
        return dict(
            sends=lambda: [copy(j, src_ref.at[mine], out_ref.at[me, mine], (px, py, c))
                           for j, (px, py) in enumerate(peers)],
            local=lambda: _local_copies(src_ref, out_ref.at[me], local_sems),
            arrivals=lambda: [copy(j, landed[j], landed[j], (px, py, c)) for j, (px, py) in enumerate(peers)],
            passed=lambda: [copy(3 + j, landed[j], landed[j], (x, y, 1 - c)) for j in range(3)],
            from_sibling=lambda: [copy(3 + j, theirs[j], theirs[j], (x, y, 1 - c)) for j in range(3)])

    def start(*refs):
        p = plan(*refs)
        for cp in p["sends"]() + p["local"]():
            cp.start()

    def pass_on(*refs):
        p = plan(*refs)
        for arrival, forward in zip(p["arrivals"](), p["passed"]()):
            arrival.wait_recv()
            forward.start()

    def finish(*refs):
        p = plan(*refs)
        for cp in p["from_sibling"]():
            cp.wait_recv()
        for cp in p["sends"]() + p["passed"]():
            cp.wait_send()
        for cp in p["local"]():
            cp.wait()

    sems = [pltpu.SemaphoreType.DMA((6,)), pltpu.SemaphoreType.DMA((6,)), pltpu.SemaphoreType.DMA((LOCAL_COPY_CHUNKS,))]
    return _Exchange(src, jax.ShapeDtypeStruct((4, R, C), src.dtype), sems, [(0.0, start), (0.6, pass_on), (1.0, finish)])


def _swap_other_half(src, *, name):
    _, R, C = src.shape
    h = R // 2

    def body(src_ref, out_ref, send_sem, recv_sem):
        x, y, c = _position()
        cp = pltpu.make_async_remote_copy(
            src_ref=src_ref.at[:, pl.ds((1 - c) * h, h)], dst_ref=out_ref, send_sem=send_sem, recv_sem=recv_sem,
            device_id=(x, y, 1 - c), device_id_type=MESH)
        cp.start()
        cp.wait()

    return pl.pallas_call(
        body, name=name, in_specs=[_HBM], out_specs=_HBM, out_shape=jax.ShapeDtypeStruct((4, h, C), src.dtype),
        scratch_shapes=[pltpu.SemaphoreType.DMA, pltpu.SemaphoreType.DMA],
    )(src)


def _scatter_exchange(src):
    def plan(src_ref, out_ref, send_sems, recv_sems, local_sems):
        x, y, c = _position()
        me = 2 * x + y
        peers = _chip_peers(x, y)

        def copy(j, src_block, dst_slot):
            px, py = peers[j]
            return pltpu.make_async_remote_copy(
                src_ref=src_ref.at[src_block], dst_ref=out_ref.at[dst_slot], send_sem=send_sems.at[j],
                recv_sem=recv_sems.at[j], device_id=(px, py, c), device_id_type=MESH)

        return dict(sends=lambda: [copy(j, 2 * px + py, me) for j, (px, py) in enumerate(peers)],
                    arrivals=lambda: [copy(j, me, 2 * px + py) for j, (px, py) in enumerate(peers)],
                    local=lambda: _local_copies(src_ref.at[me], out_ref.at[me], local_sems))

    def start(*refs):
        p = plan(*refs)
        for cp in p["sends"]() + p["local"]():
            cp.start()

    def finish(*refs):
        p = plan(*refs)
        for cp in p["arrivals"]():
            cp.wait_recv()
        for cp in p["sends"]():
            cp.wait_send()
        for cp in p["local"]():
            cp.wait()

    sems = [pltpu.SemaphoreType.DMA((3,)), pltpu.SemaphoreType.DMA((3,)), pltpu.SemaphoreType.DMA((LOCAL_COPY_CHUNKS,))]
    return _Exchange(src, jax.ShapeDtypeStruct(src.shape, src.dtype), sems, [(0.0, start), (1.0, finish)])


def _join_halves(src, *, name):
    R, C = src.shape
    h = R // 2

    def body(src_ref, out_ref, send_sem, recv_sem):
        x, y, c = _position()
        mine, theirs = pl.ds(c * h, h), pl.ds((1 - c) * h, h)
        cp = pltpu.make_async_remote_copy(
            src_ref=src_ref.at[mine], dst_ref=out_ref.at[mine], send_sem=send_sem, recv_sem=recv_sem,
            device_id=(x, y, 1 - c), device_id_type=MESH)
        cp.start()
        pltpu.make_async_remote_copy(
            src_ref=src_ref.at[theirs], dst_ref=out_ref.at[theirs], send_sem=send_sem, recv_sem=recv_sem,
            device_id=(x, y, 1 - c), device_id_type=MESH).wait_recv()
        cp.wait_send()

    return pl.pallas_call(
        body, name=name, in_specs=[_HBM], out_specs=_HBM, out_shape=jax.ShapeDtypeStruct((R, C), src.dtype),
        input_output_aliases={0: 0},
        scratch_shapes=[pltpu.SemaphoreType.DMA, pltpu.SemaphoreType.DMA],
    )(src)


def _allreduce_small(vec, *, name):
    R, C = vec.shape

    def body(v_ref, o_ref, slots, send_sems, recv_sems):
        x, y, c = _position()
        me = 4 * x + 2 * y + c

        def peer(k):
            return x ^ ((k >> 2) & 1), y ^ ((k >> 1) & 1), c ^ (k & 1)

        def copy(k, slot):
            return pltpu.make_async_remote_copy(
                src_ref=v_ref, dst_ref=slots.at[slot], send_sem=send_sems.at[k - 1], recv_sem=recv_sems.at[k - 1],
                device_id=peer(k), device_id_type=MESH)

        slots[me] = v_ref[...]
        sends = [copy(k, me) for k in range(1, 8)]
        for cp in sends:
            cp.start()
        for k in range(1, 8):
            px, py, pc = peer(k)
            copy(k, 4 * px + 2 * py + pc).wait_recv()
        total = slots[0]
        for d in range(1, 8):
            total = total + slots[d]
        o_ref[...] = total
        for cp in sends:
            cp.wait_send()

    vmem = pl.BlockSpec(memory_space=pltpu.VMEM)
    return pl.pallas_call(
        body, name=name, in_specs=[vmem], out_specs=vmem, out_shape=jax.ShapeDtypeStruct((R, C), vec.dtype),
        scratch_shapes=[pltpu.VMEM((8, R, C), vec.dtype), pltpu.SemaphoreType.DMA((7,)), pltpu.SemaphoreType.DMA((7,))],
    )(vec)


def _cross_cfg(S, mem_len):
    return _Attn(T=S, Tk=mem_len, G=1, nh=X_HEADS, rep=1, dqk=X_HEAD_DIM, dv=X_HEAD_DIM, tq=512, tk=mem_len,
                 mode="none", scale=X_HEAD_DIM ** -0.5, qcol=lambda g: 0, kcol=lambda g: 0, vcol=lambda g: 1,
                 ocol=lambda g: 0, o_width=X_HEADS * X_HEAD_DIM)


def _swa_cfg(S):
    return _Attn(T=S, Tk=S, G=1, nh=SWA_HEADS, rep=SWA_HEADS // SWA_KV_HEADS, dqk=HEAD_DIM, dv=HEAD_DIM, tq=BLOCK,
                 tk=BLOCK, mode="band", max_dist=SWA_WINDOW - 1, scale=HEAD_DIM ** -0.5, qcol=lambda g: 0,
                 kcol=lambda g: 0, vcol=lambda g: 0, ocol=lambda g: 0, o_width=SWA_HEADS * HEAD_DIM)


def _mla_cfg(S):
    t = _tile(S, 512)
    return _Attn(T=S, Tk=S, G=MLA_HEADS // 2, nh=2, rep=1, dqk=LANES, dv=MLA_V, tq=t, tk=t, mode="causal",
                 scale=(MLA_NOPE + MLA_ROPE) ** -0.5, qcol=lambda g: g, kcol=lambda g: g, vcol=lambda g: g,
                 ocol=lambda g: g, o_width=MLA_HEADS * MLA_V)


def _dil_cfg(S, window, dil):
    return _Attn(T=S // dil, Tk=S // dil, G=dil, nh=DIL_HEADS, rep=1, dqk=HEAD_DIM, dv=HEAD_DIM, tq=BLOCK, tk=BLOCK,
                 mode="band", max_dist=window // dil, scale=HEAD_DIM ** -0.5, qcol=lambda g: g, kcol=lambda g: g,
                 vcol=lambda g: g, ocol=lambda g: g, o_width=dil * DIL_HEADS * HEAD_DIM)


def _rows_cfg(S):
    return _Attn(T=S, Tk=S, G=1, nh=DIL_HEADS, rep=1, dqk=HEAD_DIM, dv=HEAD_DIM, tq=BLOCK, tk=BLOCK, mode="none",
                 scale=1.0, qcol=lambda g: 0, kcol=lambda g: 0, vcol=lambda g: 0, ocol=lambda g: 0,
                 o_width=DIL_HEADS * HEAD_DIM)


def _cross_fwd(p, x, mem, W, vec):
    S = x.shape[0]
    cfg = _cross_cfg(S, mem.shape[0])
    hx = _rmsnorm(x, vec[p + "x_norm"], name=p + "x_norm")
    qx = _mm(hx, W[p + "w_xq"], mode="nn", name=p + "xq", out_dtype=MXU_DTYPE)
    memn = _rmsnorm(mem, vec[p + "mem_norm"], name=p + "mem_norm")
    kvx = _mm(memn, W[p + "w_xkv"], mode="nn", name=p + "xkv", out_dtype=MXU_DTYPE)
    ox, lse = _attn_fwd(cfg, qx, kvx, kvx, name=p + "x_attn", out_dtype=MXU_DTYPE)
    out = _mm(ox, W[p + "w_xo"], mode="nn", name=p + "xo", res=x)
    return out, (x, hx, qx, memn, kvx, ox, lse)


def _cross_bwd(p, dx, saved, mem, W, vec, dW, dvec):
    x, hx, qx, memn, kvx, ox, lse = saved
    cfg = _cross_cfg(x.shape[0], mem.shape[0])
    dox = _mm(dx, W[p + "w_xo"], mode="nt", name=p + "xo_dx", out_dtype=MXU_DTYPE)
    dW[p + "w_xo"] = _mm(ox, dx, mode="tn", name=p + "xo_dw")
    delta, _ = _attn_delta(cfg, ox, dox, name=p + "x_delta")
    dqx = _attn_dq(cfg, qx, kvx, kvx, dox, lse, delta, name=p + "x_dq", out_dtype=MXU_DTYPE)
    dkx, dvx = _attn_dkv(cfg, qx, kvx, kvx, dox, lse, delta, name=p + "x_dkv", out_dtype=MXU_DTYPE)
    dkvx = jnp.concatenate([dkx, dvx], axis=1)
    dhx = _mm(dqx, W[p + "w_xq"], mode="nt", name=p + "xq_dx")
    dW[p + "w_xq"] = _mm(hx, dqx, mode="tn", name=p + "xq_dw")
    dW[p + "w_xkv"] = _mm(memn, dkvx, mode="tn", name=p + "xkv_dw")
    dmemn = _mm(dkvx, W[p + "w_xkv"], mode="nt", name=p + "xkv_dx")
    _, dvec[p + "mem_norm"] = _rmsnorm_bwd(mem, vec[p + "mem_norm"], dmemn, name=p + "mem_norm_bwd")
    dx_in, dvec[p + "x_norm"] = _rmsnorm_bwd(x, vec[p + "x_norm"], dhx, name=p + "x_norm_bwd", dres=dx)
    return dx_in


def _ffn_fwd(p, x, W, vec):
    hf = _rmsnorm(x, vec[p + "ffn_norm"], name=p + "ffn_norm")
    gate = _mm(hf, W[p + "w_gate"], mode="nn", name=p + "gate")
    up = _mm(hf, W[p + "w_up"], mode="nn", name=p + "up")
    act = _swiglu(gate, up, name=p + "swiglu")
    out = _mm(act, W[p + "w_down"], mode="nn", name=p + "down", res=x)
    return out, (x, hf, gate, up, act)


def _ffn_bwd(p, dx, saved, W, vec, dW, dvec):
    x, hf, gate, up, act = saved
    dact = _mm(dx, W[p + "w_down"], mode="nt", name=p + "down_dx")
    dW[p + "w_down"] = _mm(act, dx, mode="tn", name=p + "down_dw")
    dgate, dup = _swiglu_bwd(gate, up, dact, name=p + "swiglu_bwd")
    dhf = _mm(dgate, W[p + "w_gate"], mode="nt", name=p + "gate_dx")
    dhf = _mm(dup, W[p + "w_up"], mode="nt", name=p + "up_dx", res=dhf)
    dW[p + "w_gate"] = _mm(hf, dgate, mode="tn", name=p + "gate_dw")
    dW[p + "w_up"] = _mm(hf, dup, mode="tn", name=p + "up_dw")
    dx_in, dvec[p + "ffn_norm"] = _rmsnorm_bwd(x, vec[p + "ffn_norm"], dhf, name=p + "ffn_norm_bwd", dres=dx)
    return dx_in


def _even_fwd(p, x, tabs, W, vec, comm=None):
    S = x.shape[0]
    h = _rmsnorm(x, vec[p + "mix_norm"], name=p + "mix_norm")
    z = _mm(h, W[p + "w_in"], mode="nn", name=p + "in")
    qa, ka, va, cqn, ckvn, kr = _l0_prep(z, tabs, vec[p + "q_norm"], vec[p + "kv_norm"], name=p + "prep")
    sink = jnp.pad(vec[p + "sinks"], (0, LANES - SWA_HEADS)).reshape(1, LANES)
    oa, lse_a = _band_fwd(_swa_cfg(S), qa, ka, va, name=p + "swa", sink=sink, out_dtype=MXU_DTYPE)
    qb = _mm(cqn, W[p + "w_uq"], mode="nn", name=p + "uq")
    kvb = _mm(ckvn, W[p + "w_ukv"], mode="nn", name=p + "ukv")
    Q, K, V = _mla_prep(qb, kvb, kr, tabs, name=p + "mla_prep")
    if comm is None:
        ob, lse_b = _causal_fwd(_mla_cfg(S), Q, K, V, name=p + "mla", out_dtype=MXU_DTYPE)
    else:
        ob, lse_b, gathered = _causal_fwd(_mla_cfg(S), Q, K, V, name=p + "mla", out_dtype=MXU_DTYPE,
                                          carry=comm.late_weights_exchange())
        W = {**W, **comm.late_weights(gathered)}
    o = jnp.concatenate([oa, ob], axis=1)
    out = _mm(o, W[p + "w_out"], mode="nn", name=p + "out", res=x)
    return out, (x, h, z, qa, ka, va, cqn, ckvn, sink, oa, lse_a, Q, K, V, ob, lse_b, o), W


def _even_bwd(p, dx, saved, tabs, W, vec, dW, dvec, comm=None):
    x, h, z, qa, ka, va, cqn, ckvn, sink, oa, lse_a, Q, K, V, ob, lse_b, o = saved
    S = x.shape[0]
    do = _mm(dx, W[p + "w_out"], mode="nt", name=p + "out_dx", out_dtype=MXU_DTYPE)
    dW[p + "w_out"] = _mm(o, dx, mode="tn", name=p + "out_dw")
    doa, dob = do[:, :SWA_HEADS * HEAD_DIM], do[:, SWA_HEADS * HEAD_DIM:]
    cfg = _swa_cfg(S)
    delta, dsink = _attn_delta(cfg, oa, doa, name=p + "swa_delta", lse=lse_a, sink=sink)
    dvec[p + "sinks"] = dsink
    dqa, dka, dva = _band_bwd(cfg, qa, ka, va, doa, lse_a, delta, name=p + "swa_bwd")
    cfg = _mla_cfg(S)
    delta, _ = _attn_delta(cfg, ob, dob, name=p + "mla_delta")
    if comm is None:
        dQ, dK, dV = _causal_bwd(cfg, Q, K, V, dob, lse_b, delta, name=p + "mla_bwd")
    else:
        dQ, dK, dV, landed = _causal_bwd(cfg, Q, K, V, dob, lse_b, delta, name=p + "mla_bwd",
                                         carry=comm.late_grads_exchange(dW))
        comm.late_grads_landed(landed)
    dqb, dkvb, dkr = _mla_prep_bwd(dQ, dK, dV, tabs, name=p + "mla_prep_bwd")
    dcqn = _mm(dqb, W[p + "w_uq"], mode="nt", name=p + "uq_dx")
    dW[p + "w_uq"] = _mm(cqn, dqb, mode="tn", name=p + "uq_dw")
    dckvn = _mm(dkvb, W[p + "w_ukv"], mode="nt", name=p + "ukv_dx")
    dW[p + "w_ukv"] = _mm(ckvn, dkvb, mode="tn", name=p + "ukv_dw")
    dz, dvec[p + "q_norm"], dvec[p + "kv_norm"] = _l0_prep_bwd(
        z, tabs, vec[p + "q_norm"], vec[p + "kv_norm"], dqa, dka, dva, dcqn, dckvn, dkr, name=p + "prep_bwd")
    dh = _mm(dz, W[p + "w_in"], mode="nt", name=p + "in_dx")
    dW[p + "w_in"] = _mm(h, dz, mode="tn", name=p + "in_dw")
    dx_in, dvec[p + "mix_norm"] = _rmsnorm_bwd(x, vec[p + "mix_norm"], dh, name=p + "mix_norm_bwd", dres=dx)
    return dx_in


def _odd_fwd(p, x, tabs, W, vec):
    S = x.shape[0]
    assert S % (DIL_PATTERNS[-1][1] * BLOCK) == 0, "keys past the end of the sequence are never attended"
    h = _rmsnorm(x, vec[p + "mix_norm"], name=p + "mix_norm")
    qkv = _mm(h, W[p + "w_qkv"], mode="nn", name=p + "qkv")
    qkv_by_d = _l1_prep(qkv, tabs, name=p + "prep")
    outs, lses = {}, {}
    for window, dil in DIL_PATTERNS:
        outs[dil], lses[dil] = _band_fwd(_dil_cfg(S, window, dil), *qkv_by_d[dil], name=p + "dil%d" % dil)
    o, w1, w4, w16 = _merge(outs, lses, name=p + "merge")
    out = _mm(o, W[p + "w_out"], mode="nn", name=p + "out", res=x)
    return out, (x, h, qkv_by_d, lses, dict(zip(DILATIONS, (w1, w4, w16))), o)


def _odd_bwd(p, dx, saved, tabs, W, vec, dW, dvec):
    x, h, qkv_by_d, lses, ws, o = saved
    S = x.shape[0]
    do = _mm(dx, W[p + "w_out"], mode="nt", name=p + "out_dx")
    dW[p + "w_out"] = _mm(o, dx, mode="tn", name=p + "out_dw")
    dos, deltas = _merge_bwd(do, o, ws, name=p + "merge_bwd")
    grads = {}
    for window, dil in DIL_PATTERNS:
        grads[dil] = _band_bwd(_dil_cfg(S, window, dil), *qkv_by_d[dil], dos[dil], lses[dil], deltas[dil],
                               name=p + "dil%d_bwd" % dil)
    dqkv = _l1_prep_bwd(grads, tabs, name=p + "prep_bwd")
    dh = _mm(dqkv, W[p + "w_qkv"], mode="nt", name=p + "qkv_dx")
    dW[p + "w_qkv"] = _mm(h, dqkv, mode="tn", name=p + "qkv_dw")
    dx_in, dvec[p + "mix_norm"] = _rmsnorm_bwd(x, vec[p + "mix_norm"], dh, name=p + "mix_norm_bwd", dres=dx)
    return dx_in


def _local_step(x, mem, positions, target, W, vec, comm=None):
    tabs = _rope_tables(positions)
    x1, s_mix0, W = _even_fwd("l0_", x, tabs, W, vec, comm)
    x2, s_x0 = _cross_fwd("l0_", x1, mem, W, vec)
    x3, s_f0 = _ffn_fwd("l0_", x2, W, vec)
    x4, s_mix1 = _odd_fwd("l1_", x3, tabs, W, vec)
    x5, s_x1 = _cross_fwd("l1_", x4, mem, W, vec)
    x6, s_f1 = _ffn_fwd("l1_", x5, W, vec)
    dW, dvec = {}, {}
    dx, dvec["final_norm"], sq = _loss_head(x6, vec["final_norm"], target, name="loss_head")
    dx = _ffn_bwd("l1_", dx, s_f1, W, vec, dW, dvec)
    dx = _cross_bwd("l1_", dx, s_x1, mem, W, vec, dW, dvec)
    dx = _odd_bwd("l1_", dx, s_mix1, tabs, W, vec, dW, dvec)
    dx = _ffn_bwd("l0_", dx, s_f0, W, vec, dW, dvec)
    dx = _cross_bwd("l0_", dx, s_x0, mem, W, vec, dW, dvec)
    dx = _even_bwd("l0_", dx, s_mix0, tabs, W, vec, dW, dvec, comm)
    return sq, dx, dW, dvec


_LAYER_MATS = {
    0: [("w_in", "col"), ("w_uq", "col"), ("w_ukv", "col"), ("w_out", "row"), ("w_xq", "row"), ("w_xkv", "row"),
        ("w_xo", "col"), ("w_gate", "col"), ("w_up", "col"), ("w_down", "row")],
    1: [("w_qkv", "col"), ("w_out", "row"), ("w_xq", "row"), ("w_xkv", "row"), ("w_xo", "col"), ("w_gate", "col"),
        ("w_up", "col"), ("w_down", "row")],
}
MATS = [("l%d_%s" % (l, n), kind) for l in (0, 1) for n, kind in _LAYER_MATS[l]]
_LAYER_VECS = {0: ["mix_norm", "sinks", "q_norm", "kv_norm", "x_norm", "mem_norm", "ffn_norm"],
               1: ["mix_norm", "x_norm", "mem_norm", "ffn_norm"]}
VECS = ["l%d_%s" % (l, n) for l in (0, 1) for n in _LAYER_VECS[l]] + ["final_norm"]
WEIGHT_ORDER = (["l0_mix_norm", "l0_w_in", "l0_sinks", "l0_q_norm", "l0_w_uq", "l0_kv_norm", "l0_w_ukv", "l0_w_out",
                 "l0_x_norm", "l0_mem_norm", "l0_w_xq", "l0_w_xkv", "l0_w_xo", "l0_ffn_norm", "l0_w_gate", "l0_w_up",
                 "l0_w_down", "l1_mix_norm", "l1_w_qkv", "l1_w_out", "l1_x_norm", "l1_mem_norm", "l1_w_xq",
                 "l1_w_xkv", "l1_w_xo", "l1_ffn_norm", "l1_w_gate", "l1_w_up", "l1_w_down", "final_norm"])
PACK_COLS = 1024
PACK_ROW_TILE = 2 * SUM_ROW_TILE
EXCHANGE_DTYPE = jnp.bfloat16
VEC_ROWS = 16
LOSS_ROW = len(VECS)
N_CHIPS = 4


class _Group:
    def __init__(self, mats, shards):
        self.mats, self.shards = mats, shards
        self.layout, off = {}, 0
        for name, _ in mats:
            n = shards[name].size // PACK_COLS
            assert n * PACK_COLS == shards[name].size
            self.layout[name] = (off, n)
            off += n
        self.used = off
        self.rows = -(-off // PACK_ROW_TILE) * PACK_ROW_TILE

    def pack(self, tensors, dtype):
        parts = [tensors[name].astype(dtype).reshape(-1, PACK_COLS) for name, _ in self.mats]
        return jnp.concatenate(parts + [jnp.zeros((self.rows - self.used, PACK_COLS), dtype)], axis=0)

    def unpack(self, packed):
        return {name: packed[off:off + n].reshape(self.shards[name].shape) for name, (off, n) in self.layout.items()}

    def full_weights(self, gathered):
        W = {}
        for name, kind in self.mats:
            off, n = self.layout[name]
            r, cw = self.shards[name].shape
            blocks = gathered[:, off:off + n].reshape(N_CHIPS, r, cw)
            W[name] = blocks.reshape(N_CHIPS * r, cw) if kind == "row" else (
                jnp.transpose(blocks, (1, 0, 2)).reshape(r, N_CHIPS * cw))
        if "l0_w_in" in W:
            W["l0_w_in"] = jnp.pad(W["l0_w_in"], ((0, 0), (0, Z_END - W["l0_w_in"].shape[1])))
        if "l0_w_uq" in W:
            uq = W["l0_w_uq"].reshape(MLA_Q_RANK, MLA_HEADS, MLA_NOPE + MLA_ROPE)
            uq = jnp.pad(uq, ((0, 0), (0, 0), (0, LANES - MLA_NOPE - MLA_ROPE)))
            W["l0_w_uq"] = uq.reshape(MLA_Q_RANK, MLA_HEADS * LANES)
        return W

    def pack_grads(self, dW):
        parts = []
        for name, kind in self.mats:
            r, cw = self.shards[name].shape
            g = dW[name]
            if name == "l0_w_in":
                g = g[:, :Z_KR + MLA_ROPE]
            if name == "l0_w_uq":
                g = g.reshape(MLA_Q_RANK, MLA_HEADS, LANES)[:, :, :MLA_NOPE + MLA_ROPE].reshape(MLA_Q_RANK, -1)
            if kind == "col":
                g = jnp.transpose(g.reshape(r, N_CHIPS, cw), (1, 0, 2))
            parts.append(g.reshape(N_CHIPS, -1, PACK_COLS).astype(EXCHANGE_DTYPE))
        pad = jnp.zeros((N_CHIPS, self.rows - self.used, PACK_COLS), EXCHANGE_DTYPE)
        return jnp.concatenate(parts + [pad], axis=1)


def _pack_vecs(vecs):
    rows = [jnp.pad(vecs[n].reshape(-1).astype(F32), (0, PACK_COLS - vecs[n].size)) for n in VECS]
    rows += [jnp.zeros((PACK_COLS,), F32)] * (VEC_ROWS - len(rows))
    return jnp.stack(rows)


def _unpack_vecs(packed, like):
    return {n: packed[i, :like[n].size].reshape(like[n].shape) for i, n in enumerate(VECS)}


EARLY_MATS = [m for m in MATS if m[0] in ("l0_w_in", "l0_w_uq", "l0_w_ukv")]
LATE_MATS = [m for m in MATS if m not in EARLY_MATS]


class _StepComm:
    def __init__(self, shards):
        self.early, self.late = _Group(EARLY_MATS, shards), _Group(LATE_MATS, shards)
        self.half_index = lax.axis_index("c").astype(jnp.int32).reshape(1)
        self.late_grads = None

    def early_weights(self):
        src = self.early.pack(self.early.shards, MXU_DTYPE)
        return self.early.full_weights(_run_exchange(_gather_exchange(src), name="gather_early"))

    def late_weights_exchange(self):
        return _gather_exchange(self.late.pack(self.late.shards, MXU_DTYPE))

    def late_weights(self, gathered):
        return self.late.full_weights(gathered)

    def _chip_sum(self, group, dW, tag):
        grads = group.pack_grads(dW)
        theirs = _swap_other_half(grads, name="swap_other_half_" + tag)
        return _sum_cores(grads, theirs, self.half_index, name="sum_cores_" + tag)

    def _finish(self, parts, tag):
        return _join_halves(_sum_chips(parts, self.half_index, name="sum_chips_" + tag), name="join_halves_" + tag)

    def late_grads_exchange(self, dW):
        return _scatter_exchange(self._chip_sum(self.late, dW, "late"))

    def late_grads_landed(self, parts):
        self.late_grads = self._finish(parts, "late")

    def early_grads(self, dW):
        parts = _run_exchange(_scatter_exchange(self._chip_sum(self.early, dW, "early")), name="scatter_early")
        return self._finish(parts, "early")


def _step(a):
    weights = {n: a[n] for n in WEIGHT_ORDER}
    shards = {n: weights[n] for n, _ in MATS}
    vec = {n: weights[n] for n in VECS}
    comm = _StepComm(shards)
    sq, grad_x, dW, dvec = _local_step(a["x"][0], a["mem"][0], a["positions"], a["loss_target"][0],
                                       comm.early_weights(), vec, comm)

    dvec = dict(dvec)
    dvec["l0_sinks"] = dvec["l0_sinks"][0, :SWA_HEADS]
    small = _pack_vecs(dvec)
    small = small.at[LOSS_ROW, 0].set(0.5 / a["x"].shape[-1] * jnp.sum(sq))
    small = _allreduce_small(small, name="reduce_gains")
    loss = small[LOSS_ROW, 0]
    g_s = small.at[LOSS_ROW, 0].set(0.0)
    d_s, m_s, v_s = _adamw(_pack_vecs(vec), g_s, _pack_vecs({n: a["m_" + n] for n in VECS}),
                           _pack_vecs({n: a["v_" + n] for n in VECS}), name="adamw_gains")
    got = [_unpack_vecs(packed, vec) for packed in (g_s, d_s, m_s, v_s)]

    for group, g_w, tag in ((comm.late, comm.late_grads, "late"), (comm.early, comm.early_grads(dW), "early")):
        moments = [group.pack({n: a[pre + n] for n, _ in group.mats}, F32) for pre in ("m_", "v_")]
        d_w, m_w, v_w = _adamw(group.pack(shards, F32), g_w, *moments, name="adamw_" + tag)
        for kind, packed in zip(got, (g_w, d_w, m_w, v_w)):
            kind.update(group.unpack(packed))

    out = [loss, grad_x[None]]
    for kind in got:
        out += [kind[n] for n in WEIGHT_ORDER]
    return tuple(out)


def kernel(x, mem, positions, l0_mix_norm, l0_w_in, l0_sinks, l0_q_norm, l0_w_uq, l0_kv_norm, l0_w_ukv, l0_w_out, l0_x_norm, l0_mem_norm, l0_w_xq, l0_w_xkv, l0_w_xo, l0_ffn_norm, l0_w_gate, l0_w_up, l0_w_down, l1_mix_norm, l1_w_qkv, l1_w_out, l1_x_norm, l1_mem_norm, l1_w_xq, l1_w_xkv, l1_w_xo, l1_ffn_norm, l1_w_gate, l1_w_up, l1_w_down, final_norm, loss_target, m_l0_mix_norm, m_l0_w_in, m_l0_sinks, m_l0_q_norm, m_l0_w_uq, m_l0_kv_norm, m_l0_w_ukv, m_l0_w_out, m_l0_x_norm, m_l0_mem_norm, m_l0_w_xq, m_l0_w_xkv, m_l0_w_xo, m_l0_ffn_norm, m_l0_w_gate, m_l0_w_up, m_l0_w_down, m_l1_mix_norm, m_l1_w_qkv, m_l1_w_out, m_l1_x_norm, m_l1_mem_norm, m_l1_w_xq, m_l1_w_xkv, m_l1_w_xo, m_l1_ffn_norm, m_l1_w_gate, m_l1_w_up, m_l1_w_down, m_final_norm, v_l0_mix_norm, v_l0_w_in, v_l0_sinks, v_l0_q_norm, v_l0_w_uq, v_l0_kv_norm, v_l0_w_ukv, v_l0_w_out, v_l0_x_norm, v_l0_mem_norm, v_l0_w_xq, v_l0_w_xkv, v_l0_w_xo, v_l0_ffn_norm, v_l0_w_gate, v_l0_w_up, v_l0_w_down, v_l1_mix_norm, v_l1_w_qkv, v_l1_w_out, v_l1_x_norm, v_l1_mem_norm, v_l1_w_xq, v_l1_w_xkv, v_l1_w_xo, v_l1_ffn_norm, v_l1_w_gate, v_l1_w_up, v_l1_w_down, v_final_norm):
    return _step(dict(locals()))
```

```python
import functools

import jax
import jax.numpy as jnp
import numpy as np
from jax import lax
from jax.experimental import pallas as pl
from jax.experimental.pallas import tpu as pltpu

F32 = jnp.float32
MXU_DTYPE = jnp.bfloat16
LANES = 128
VMEM_LIMIT_BYTES = 56 * 1024 * 1024

NORM_EPS = 1e-6
ROPE_THETA = 10000.0
BLOCK = 128
HEAD_DIM = 64
SWA_HEADS, SWA_KV_HEADS, SWA_WINDOW = 8, 2, 128
MLA_HEADS, MLA_Q_RANK, MLA_KV_RANK, MLA_NOPE, MLA_ROPE, MLA_V = 8, 384, 256, 64, 32, 64
DIL_HEADS = 16
DIL_PATTERNS = ((128, 1), (512, 4), (2048, 16))
X_HEADS, X_HEAD_DIM = 4, 128
ADAM_LR, ADAM_B1, ADAM_B2, ADAM_EPS, ADAM_WD, ADAM_STEP = 0.001, 0.9, 0.999, 1e-08, 0.01, 10
MESH = pl.DeviceIdType.MESH
NEG_BIG = -1e30

NN = (((1,), (0,)), ((), ()))
NT = (((1,), (1,)), ((), ()))


def _dot(a, b, dims=NN):
    return lax.dot_general(a.astype(MXU_DTYPE), b.astype(MXU_DTYPE), dims, preferred_element_type=F32)


def _pcall(body, *, name, dims=None, **kw):
    params = pltpu.CompilerParams(dimension_semantics=dims, vmem_limit_bytes=VMEM_LIMIT_BYTES)
    return pl.pallas_call(body, name=name, compiler_params=params, **kw)


def _tile(n, pref):
    t = (min(pref, n) // LANES) * LANES
    while t >= LANES:
        if n % t == 0:
            return t
        t -= LANES
    return n


SUBLANES_PACKED = 16


def _row_tile(n, pref):
    t = (min(pref, n) // SUBLANES_PACKED) * SUBLANES_PACKED
    while t >= SUBLANES_PACKED:
        if n % t == 0:
            return t
        t -= SUBLANES_PACKED
    return n


def _lane(shape):
    return lax.broadcasted_iota(jnp.int32, shape, 1)


def _cols_to_lanes(cols, rows):
    lane = _lane((rows, LANES))
    out = jnp.zeros((rows, LANES), F32)
    for j, col in enumerate(cols):
        out = jnp.where(lane == j, col, out)
    return out


def _mm(a, b, *, mode, name, res=None, out_dtype=F32, tm=1408, tn=1536, tk=1408):
    if mode == "nn":
        (M, K), (K2, N) = a.shape, b.shape
    elif mode == "nt":
        (M, K), (N, K2) = a.shape, b.shape
    else:
        (K, M), (K2, N) = a.shape, b.shape
    assert K == K2, (a.shape, b.shape, mode)
    tm, tn, tk = _tile(M, tm), _tile(N, tn), _tile(K, tk)
    nk = K // tk
    in_place = out_dtype == F32 or nk == 1

    def body(*refs):
        refs = list(refs)
        a_ref, b_ref = refs[:2]
        r_ref = refs[2] if res is not None else None
        o_ref = refs[3 if res is not None else 2]
        acc = o_ref if in_place else refs[-1]
        k = pl.program_id(2)
        if mode == "nn":
            part = _dot(a_ref[...], b_ref[...], NN)
        elif mode == "nt":
            part = _dot(a_ref[...], b_ref[...], NT)
        else:
            part = _dot(a_ref[...].T, b_ref[...], NN)
        if nk == 1:
            o_ref[...] = (part if res is None else part + r_ref[...].astype(F32)).astype(o_ref.dtype)
            return

        @pl.when(k == 0)
        def _():
            acc[...] = part if res is None else part + r_ref[...].astype(F32)

        @pl.when(k > 0)
        def _():
            acc[...] += part

        if not in_place:
            @pl.when(k == nk - 1)
            def _():
                o_ref[...] = acc[...].astype(o_ref.dtype)

    if mode == "nn":
        a_spec = pl.BlockSpec((tm, tk), lambda i, j, k: (i, k))
        b_spec = pl.BlockSpec((tk, tn), lambda i, j, k: (k, j))
    elif mode == "nt":
        a_spec = pl.BlockSpec((tm, tk), lambda i, j, k: (i, k))
        b_spec = pl.BlockSpec((tn, tk), lambda i, j, k: (j, k))
    else:
        a_spec = pl.BlockSpec((tk, tm), lambda i, j, k: (k, i))
        b_spec = pl.BlockSpec((tk, tn), lambda i, j, k: (k, j))
    o_spec = pl.BlockSpec((tm, tn), lambda i, j, k: (i, j))
    in_specs = [a_spec, b_spec] + ([] if res is None else [o_spec])
    args = (a, b) + (() if res is None else (res,))
    return _pcall(
        body, name=name, dims=("parallel", "parallel", "arbitrary"),
        grid=(M // tm, N // tn, nk), in_specs=in_specs, out_specs=o_spec,
        out_shape=jax.ShapeDtypeStruct((M, N), out_dtype),
        scratch_shapes=[] if in_place else [pltpu.VMEM((tm, tn), F32)],
    )(*args)


def _rms_parts(xf):
    r = lax.rsqrt(jnp.mean(xf * xf, axis=-1, keepdims=True) + NORM_EPS)
    return xf * r, r


def _rms_bwd_rows(xf, g, dy):
    xhat, r = _rms_parts(xf)
    dxhat = dy * g
    dx = r * (dxhat - xhat * jnp.mean(dxhat * xhat, axis=-1, keepdims=True))
    return dx, dy * xhat


def _rmsnorm(x, g, *, name, out_dtype=MXU_DTYPE, tm=512):
    M, D = x.shape
    tm = _tile(M, tm)

    def body(x_ref, g_ref, o_ref):
        xhat, _ = _rms_parts(x_ref[...].astype(F32))
        o_ref[...] = (xhat * g_ref[...]).astype(o_ref.dtype)

    return _pcall(
        body, name=name, dims=("parallel",), grid=(M // tm,),
        in_specs=[pl.BlockSpec((tm, D), lambda i: (i, 0)), pl.BlockSpec((1, D), lambda i: (0, 0))],
        out_specs=pl.BlockSpec((tm, D), lambda i: (i, 0)),
        out_shape=jax.ShapeDtypeStruct((M, D), out_dtype),
    )(x, g.reshape(1, D))


def _rmsnorm_bwd(x, g, dy, *, name, dres=None, tm=512):
    M, D = x.shape
    tm = _tile(M, tm)

    def body(*refs):
        if dres is None:
            x_ref, g_ref, dy_ref, dx_ref, dg_ref = refs
        else:
            x_ref, g_ref, dy_ref, dr_ref, dx_ref, dg_ref = refs
        dx, dgp = _rms_bwd_rows(x_ref[...].astype(F32), g_ref[...], dy_ref[...].astype(F32))
        if dres is not None:
            dx = dx + dr_ref[...]
        dx_ref[...] = dx

        @pl.when(pl.program_id(0) == 0)
        def _():
            dg_ref[...] = jnp.zeros_like(dg_ref)

        dg_ref[...] += jnp.sum(dgp, axis=0, keepdims=True)

    row = pl.BlockSpec((tm, D), lambda i: (i, 0))
    vec = pl.BlockSpec((1, D), lambda i: (0, 0))
    in_specs = [row, vec, row] + ([] if dres is None else [row])
    args = (x, g.reshape(1, D), dy) + (() if dres is None else (dres,))
    return _pcall(
        body, name=name, dims=("arbitrary",), grid=(M // tm,), in_specs=in_specs, out_specs=[row, vec],
        out_shape=[jax.ShapeDtypeStruct((M, D), F32), jax.ShapeDtypeStruct((1, D), F32)],
    )(*args)


def _rope_chunk(t, c, s, half):
    lane = _lane(t.shape)
    swapped = jnp.where((lane % (2 * half)) < half, pltpu.roll(t, LANES - half, 1), pltpu.roll(t, half, 1))
    return t * c + swapped * s


def _rope_tables(positions):
    pos = positions.reshape(-1).astype(F32)[:, None]
    S = pos.shape[0]

    def cs(dh):
        inv_freq = ROPE_THETA ** (-jnp.arange(0, dh, 2, dtype=F32) / dh)
        ang = pos * inv_freq
        return jnp.cos(ang), jnp.sin(ang)

    c64, s64 = cs(HEAD_DIM)
    c32, s32 = cs(MLA_ROPE)
    z32, z64, z96 = (jnp.zeros((S, n), F32) for n in (32, 64, 96))
    return dict(
        c64=jnp.concatenate([c64, c64, c64, c64], 1), s64=jnp.concatenate([-s64, s64, -s64, s64], 1),
        ck=jnp.concatenate([c32, c32, z96], 1), sk=jnp.concatenate([-s32, s32, z96], 1),
        cm=jnp.concatenate([jnp.ones((S, 64), F32), c32, c32, z32], 1),
        sm=jnp.concatenate([z64, -s32, s32, z32], 1),
    )


def _attn_steps(mode, n_other, t_self, t_other):
    if mode == "band":
        assert t_self == t_other
        return 2
    return n_other


def _kv_block(mode, qi, kj):
    if mode == "band":
        return jnp.maximum(qi - 1 + kj, 0), (qi + kj) >= 1
    if mode == "causal":
        return jnp.minimum(kj, qi), kj <= qi
    return kj, None


def _q_block(mode, ki, qj, nq):
    if mode == "band":
        return jnp.minimum(ki + qj, nq - 1), (ki + qj) <= nq - 1
    if mode == "causal":
        return jnp.maximum(qj, ki), qj >= ki
    return qj, None


def _mask(mode, max_dist, qpos, kpos):
    d = qpos - kpos
    if mode == "band":
        return (d >= 0) & (d <= max_dist)
    if mode == "causal":
        return d >= 0
    return None


def _when(cond, fn):
    if cond is None:
        fn()
    else:
        pl.when(cond)(fn)


class _Attn:
    def __init__(self, *, T, Tk, G, nh, rep, dqk, dv, tq, tk, mode, scale, qcol, kcol, vcol, ocol, o_width,
                 max_dist=0):
        self.__dict__.update(locals())
        self.nkv = nh // rep
        assert T % tq == 0 and Tk % tk == 0 and nh <= LANES


def _attn_fwd(cfg, q, k, v, *, name, sink=None, out_dtype=F32):
    c = cfg
    nq, nk = c.T // c.tq, c.Tk // c.tk
    steps = _attn_steps(c.mode, nk, c.tq, c.tk)

    def body(*refs):
        if sink is None:
            q_ref, k_ref, v_ref, o_ref, lse_ref, m_scr, l_scr, acc = refs
        else:
            q_ref, k_ref, v_ref, sink_ref, o_ref, lse_ref, m_scr, l_scr, acc = refs
        qi, kj = pl.program_id(1), pl.program_id(2)
        kb, valid = _kv_block(c.mode, qi, kj)

        @pl.when(kj == 0)
        def _():
            if sink is None:
                m_scr[...] = jnp.full_like(m_scr, NEG_BIG)
                l_scr[...] = jnp.zeros_like(l_scr)
            else:
                m_scr[...] = jnp.broadcast_to(sink_ref[...], m_scr.shape)
                l_scr[...] = jnp.ones_like(l_scr)
            acc[...] = jnp.zeros_like(acc)

        def step():
            qpos = qi * c.tq + lax.broadcasted_iota(jnp.int32, (c.tq, c.tk), 0)
            kpos = kb * c.tk + lax.broadcasted_iota(jnp.int32, (c.tq, c.tk), 1)
            mask = _mask(c.mode, c.max_dist, qpos, kpos)
            for j in range(c.nh):
                g = j // c.rep
                s = _dot(q_ref[:, j * c.dqk:(j + 1) * c.dqk], k_ref[:, g * c.dqk:(g + 1) * c.dqk], NT) * c.scale
                if mask is not None:
                    s = jnp.where(mask, s, -jnp.inf)
                m_prev = m_scr[:, j:j + 1]
                m_new = jnp.maximum(m_prev, jnp.max(s, axis=1, keepdims=True))
                alpha = jnp.exp(m_prev - m_new)
                p = jnp.exp(s - m_new)
                l_scr[:, j:j + 1] = alpha * l_scr[:, j:j + 1] + jnp.sum(p, axis=1, keepdims=True)
                acc[:, j * c.dv:(j + 1) * c.dv] = (
                    alpha * acc[:, j * c.dv:(j + 1) * c.dv] + _dot(p, v_ref[:, g * c.dv:(g + 1) * c.dv], NN))
                m_scr[:, j:j + 1] = m_new

        _when(valid, step)

        @pl.when(kj == steps - 1)
        def _():
            for j in range(c.nh):
                o_ref[:, j * c.dv:(j + 1) * c.dv] = (
                    acc[:, j * c.dv:(j + 1) * c.dv] / l_scr[:, j:j + 1]).astype(o_ref.dtype)
            lane = _lane((c.tq, LANES))
            lse_ref[...] = jnp.where(lane < c.nh, m_scr[...] + jnp.log(jnp.maximum(l_scr[...], 1e-37)), 0.0)

    in_specs = [
        pl.BlockSpec((c.tq, c.nh * c.dqk), lambda g, i, j: (i, c.qcol(g))),
        pl.BlockSpec((c.tk, c.nkv * c.dqk), lambda g, i, j: (_kv_block(c.mode, i, j)[0], c.kcol(g))),
        pl.BlockSpec((c.tk, c.nkv * c.dv), lambda g, i, j: (_kv_block(c.mode, i, j)[0], c.vcol(g))),
    ]
    args = [q, k, v]
    if sink is not None:
        in_specs.append(pl.BlockSpec((1, LANES), lambda g, i, j: (0, 0)))
        args.append(sink)
    return _pcall(
        body, name=name, dims=("parallel", "parallel", "arbitrary"), grid=(c.G, nq, steps),
        in_specs=in_specs,
        out_specs=[pl.BlockSpec((c.tq, c.nh * c.dv), lambda g, i, j: (i, c.ocol(g))),
                   pl.BlockSpec((c.tq, LANES), lambda g, i, j: (i, g))],
        out_shape=[jax.ShapeDtypeStruct((c.T, c.o_width), out_dtype),
                   jax.ShapeDtypeStruct((c.T, LANES * c.G), F32)],
        scratch_shapes=[pltpu.VMEM((c.tq, LANES), F32), pltpu.VMEM((c.tq, LANES), F32),
                        pltpu.VMEM((c.tq, c.nh * c.dv), F32)],
    )(*args)


def _attn_delta(cfg, o, do, *, name, w=None, lse=None, sink=None, tm=512):
    c = cfg
    tm = _tile(c.T, tm)
    width = c.nh * c.dv

    def body(*refs):
        refs = list(refs)
        o_ref, do_ref = refs[:2]
        rest = refs[2:]
        w_ref = rest.pop(0) if w is not None else None
        lse_ref, sink_ref = (rest.pop(0), rest.pop(0)) if sink is not None else (None, None)
        d_ref = rest.pop(0)
        prod = o_ref[...].astype(F32) * do_ref[...].astype(F32)
        cols = [jnp.sum(prod[:, j * c.dv:(j + 1) * c.dv], axis=1, keepdims=True) for j in range(c.nh)]
        delta = _cols_to_lanes(cols, tm)
        if w is not None:
            delta = delta * w_ref[...]
        d_ref[...] = delta
        if sink is not None:
            ds_ref = rest.pop(0)

            @pl.when(pl.program_id(1) == 0)
            def _():
                ds_ref[...] = jnp.zeros_like(ds_ref)

            lane = _lane((tm, LANES))
            ps = jnp.where(lane < c.nh, jnp.exp(sink_ref[...] - lse_ref[...]), 0.0)
            ds_ref[...] -= jnp.sum(ps * delta, axis=0, keepdims=True)

    stat = pl.BlockSpec((tm, LANES), lambda g, i: (i, g))
    in_specs = [pl.BlockSpec((tm, width), lambda g, i: (i, c.ocol(g)))] * 2
    args = [o, do]
    out_specs, out_shape = [stat], [jax.ShapeDtypeStruct((c.T, LANES * c.G), F32)]
    if w is not None:
        in_specs.append(stat)
        args.append(w)
    if sink is not None:
        assert c.G == 1
        in_specs += [stat, pl.BlockSpec((1, LANES), lambda g, i: (0, 0))]
        args += [lse, sink]
        out_specs.append(pl.BlockSpec((1, LANES), lambda g, i: (0, 0)))
        out_shape.append(jax.ShapeDtypeStruct((1, LANES), F32))
    out = _pcall(
        body, name=name, dims=("arbitrary", "arbitrary"), grid=(c.G, c.T // tm),
        in_specs=in_specs, out_specs=out_specs, out_shape=out_shape,
    )(*args)
    return out if sink is not None else (out[0], None)


def _attn_dq(cfg, q, k, v, do, lse, delta, *, name, init=None, out_dtype=F32):
    c = cfg
    nq, nk = c.T // c.tq, c.Tk // c.tk
    steps = _attn_steps(c.mode, nk, c.tq, c.tk)
    qw = c.nh * c.dqk

    def body(*refs):
        if init is None:
            q_ref, k_ref, v_ref, do_ref, lse_ref, d_ref, dq_ref, acc = refs
        else:
            q_ref, k_ref, v_ref, do_ref, lse_ref, d_ref, init_ref, dq_ref, acc = refs
        qi, kj = pl.program_id(1), pl.program_id(2)
        kb, valid = _kv_block(c.mode, qi, kj)

        @pl.when(kj == 0)
        def _():
            acc[...] = jnp.zeros_like(acc) if init is None else init_ref[...].astype(F32)

        def step():
            qpos = qi * c.tq + lax.broadcasted_iota(jnp.int32, (c.tq, c.tk), 0)
            kpos = kb * c.tk + lax.broadcasted_iota(jnp.int32, (c.tq, c.tk), 1)
            mask = _mask(c.mode, c.max_dist, qpos, kpos)
            for j in range(c.nh):
                g = j // c.rep
                kh = k_ref[:, g * c.dqk:(g + 1) * c.dqk]
                s = _dot(q_ref[:, j * c.dqk:(j + 1) * c.dqk], kh, NT) * c.scale
                if mask is not None:
                    s = jnp.where(mask, s, -jnp.inf)
                p = jnp.exp(s - lse_ref[:, j:j + 1])
                dp = _dot(do_ref[:, j * c.dv:(j + 1) * c.dv], v_ref[:, g * c.dv:(g + 1) * c.dv], NT)
                ds = p * (dp - d_ref[:, j:j + 1]) * c.scale
                acc[:, j * c.dqk:(j + 1) * c.dqk] += _dot(ds, kh, NN)

        _when(valid, step)

        @pl.when(kj == steps - 1)
        def _():
            dq_ref[...] = acc[...].astype(dq_ref.dtype)

    kvb = lambda i, j: _kv_block(c.mode, i, j)[0]
    qspec = pl.BlockSpec((c.tq, qw), lambda g, i, j: (i, c.qcol(g)))
    stat = pl.BlockSpec((c.tq, LANES), lambda g, i, j: (i, g))
    in_specs = [
        qspec,
        pl.BlockSpec((c.tk, c.nkv * c.dqk), lambda g, i, j: (kvb(i, j), c.kcol(g))),
        pl.BlockSpec((c.tk, c.nkv * c.dv), lambda g, i, j: (kvb(i, j), c.vcol(g))),
        pl.BlockSpec((c.tq, c.nh * c.dv), lambda g, i, j: (i, c.ocol(g))),
        stat, stat,
    ]
    args = [q, k, v, do, lse, delta]
    dq_spec = pl.BlockSpec((c.tq, qw), lambda g, i, j: (i, g))
    if init is not None:
        in_specs.append(dq_spec)
        args.append(init)
    return _pcall(
        body, name=name, dims=("parallel", "parallel", "arbitrary"), grid=(c.G, nq, steps),
        in_specs=in_specs, out_specs=dq_spec,
        out_shape=jax.ShapeDtypeStruct((c.T, c.G * qw), out_dtype),
        scratch_shapes=[pltpu.VMEM((c.tq, qw), F32)],
    )(*args)


def _attn_dkv(cfg, q, k, v, do, lse, delta, *, name, init=None, out_dtype=F32):
    c = cfg
    nq, nk = c.T // c.tq, c.Tk // c.tk
    steps = _attn_steps(c.mode, nq, c.tk, c.tq)
    kw, vw = c.nkv * c.dqk, c.nkv * c.dv

    def body(*refs):
        if init is None:
            q_ref, k_ref, v_ref, do_ref, lse_ref, d_ref, dk_ref, dv_ref, dk_acc, dv_acc = refs
        else:
            q_ref, k_ref, v_ref, do_ref, lse_ref, d_ref, ik_ref, iv_ref, dk_ref, dv_ref, dk_acc, dv_acc = refs
        ki, qj = pl.program_id(1), pl.program_id(2)
        qb, valid = _q_block(c.mode, ki, qj, nq)

        @pl.when(qj == 0)
        def _():
            dk_acc[...] = jnp.zeros_like(dk_acc) if init is None else ik_ref[...].astype(F32)
            dv_acc[...] = jnp.zeros_like(dv_acc) if init is None else iv_ref[...].astype(F32)

        def step():
            kpos = ki * c.tk + lax.broadcasted_iota(jnp.int32, (c.tk, c.tq), 0)
            qpos = qb * c.tq + lax.broadcasted_iota(jnp.int32, (c.tk, c.tq), 1)
            mask = _mask(c.mode, c.max_dist, qpos, kpos)
            lse_t = lse_ref[...].T
            d_t = d_ref[...].T
            for j in range(c.nh):
                g = j // c.rep
                qh = q_ref[:, j * c.dqk:(j + 1) * c.dqk]
                doh = do_ref[:, j * c.dv:(j + 1) * c.dv]
                s_t = _dot(k_ref[:, g * c.dqk:(g + 1) * c.dqk], qh, NT) * c.scale
                if mask is not None:
                    s_t = jnp.where(mask, s_t, -jnp.inf)
                p_t = jnp.exp(s_t - lse_t[j:j + 1, :])
                dv_acc[:, g * c.dv:(g + 1) * c.dv] += _dot(p_t, doh, NN)
                dp_t = _dot(v_ref[:, g * c.dv:(g + 1) * c.dv], doh, NT)
                ds_t = p_t * (dp_t - d_t[j:j + 1, :]) * c.scale
                dk_acc[:, g * c.dqk:(g + 1) * c.dqk] += _dot(ds_t, qh, NN)

        _when(valid, step)

        @pl.when(qj == steps - 1)
        def _():
            dk_ref[...] = dk_acc[...].astype(dk_ref.dtype)
            dv_ref[...] = dv_acc[...].astype(dv_ref.dtype)

    qbi = lambda i, j: _q_block(c.mode, i, j, nq)[0]
    stat = pl.BlockSpec((c.tq, LANES), lambda g, i, j: (qbi(i, j), g))
    in_specs = [
        pl.BlockSpec((c.tq, c.nh * c.dqk), lambda g, i, j: (qbi(i, j), c.qcol(g))),
        pl.BlockSpec((c.tk, kw), lambda g, i, j: (i, c.kcol(g))),
        pl.BlockSpec((c.tk, vw), lambda g, i, j: (i, c.vcol(g))),
        pl.BlockSpec((c.tq, c.nh * c.dv), lambda g, i, j: (qbi(i, j), c.ocol(g))),
        stat, stat,
    ]
    args = [q, k, v, do, lse, delta]
    dk_spec = pl.BlockSpec((c.tk, kw), lambda g, i, j: (i, g))
    dv_spec = pl.BlockSpec((c.tk, vw), lambda g, i, j: (i, g))
    if init is not None:
        in_specs += [dk_spec, dv_spec]
        args += list(init)
    return _pcall(
        body, name=name, dims=("parallel", "parallel", "arbitrary"), grid=(c.G, nk, steps),
        in_specs=in_specs, out_specs=[dk_spec, dv_spec],
        out_shape=[jax.ShapeDtypeStruct((c.Tk, c.G * kw), out_dtype),
                   jax.ShapeDtypeStruct((c.Tk, c.G * vw), out_dtype)],
        scratch_shapes=[pltpu.VMEM((c.tk, kw), F32), pltpu.VMEM((c.tk, vw), F32)],
    )(*args)


TN = (((0,), (0,)), ((), ()))


def _band_mask(c, i):
    row = lax.broadcasted_iota(jnp.int32, (BLOCK, 2 * BLOCK), 0)
    col = lax.broadcasted_iota(jnp.int32, (BLOCK, 2 * BLOCK), 1)
    d = BLOCK + row - col
    return (d >= 0) & (d <= c.max_dist) & ((col >= BLOCK) | (i > 0))


def _band_fwd(cfg, q, k, v, *, name, sink=None, out_dtype=F32):
    c = cfg
    assert c.mode == "band" and c.tq == c.tk == BLOCK and c.T == c.Tk
    nq = c.T // BLOCK

    def body(*refs):
        if sink is None:
            q_ref, kp_ref, kc_ref, vp_ref, vc_ref, o_ref, lse_ref = refs
        else:
            q_ref, kp_ref, kc_ref, vp_ref, vc_ref, sink_ref, o_ref, lse_ref = refs
        mask = _band_mask(c, pl.program_id(1))
        k2 = jnp.concatenate([kp_ref[...], kc_ref[...]], axis=0)
        v2 = jnp.concatenate([vp_ref[...], vc_ref[...]], axis=0)
        lses = []
        for j in range(c.nh):
            g = j // c.rep
            s = _dot(q_ref[:, j * c.dqk:(j + 1) * c.dqk], k2[:, g * c.dqk:(g + 1) * c.dqk], NT) * c.scale
            s = jnp.where(mask, s, -jnp.inf)
            m = jnp.max(s, axis=1, keepdims=True)
            if sink is not None:
                sk = sink_ref[:, j:j + 1]
                m = jnp.maximum(m, sk)
            p = jnp.exp(s - m)
            l = jnp.sum(p, axis=1, keepdims=True)
            if sink is not None:
                l = l + jnp.exp(sk - m)
            o_ref[:, j * c.dv:(j + 1) * c.dv] = (_dot(p, v2[:, g * c.dv:(g + 1) * c.dv], NN) / l).astype(o_ref.dtype)
            lses.append(m + jnp.log(l))
        lse_ref[...] = _cols_to_lanes(lses, BLOCK)

    prev = lambda i: jnp.maximum(i - 1, 0)
    kw, vw = c.nkv * c.dqk, c.nkv * c.dv
    in_specs = [
        pl.BlockSpec((BLOCK, c.nh * c.dqk), lambda g, i: (i, c.qcol(g))),
        pl.BlockSpec((BLOCK, kw), lambda g, i: (prev(i), c.kcol(g))),
        pl.BlockSpec((BLOCK, kw), lambda g, i: (i, c.kcol(g))),
        pl.BlockSpec((BLOCK, vw), lambda g, i: (prev(i), c.vcol(g))),
        pl.BlockSpec((BLOCK, vw), lambda g, i: (i, c.vcol(g))),
    ]
    args = [q, k, k, v, v]
    if sink is not None:
        in_specs.append(pl.BlockSpec((1, LANES), lambda g, i: (0, 0)))
        args.append(sink)
    return _pcall(
        body, name=name, dims=("parallel", "parallel"), grid=(c.G, nq), in_specs=in_specs,
        out_specs=[pl.BlockSpec((BLOCK, c.nh * c.dv), lambda g, i: (i, c.ocol(g))),
                   pl.BlockSpec((BLOCK, LANES), lambda g, i: (i, g))],
        out_shape=[jax.ShapeDtypeStruct((c.T, c.o_width), out_dtype),
                   jax.ShapeDtypeStruct((c.T, LANES * c.G), F32)],
    )(*args)


def _band_bwd(cfg, q, k, v, do, lse, delta, *, name, init=None):
    c = cfg
    assert c.mode == "band" and c.tq == c.tk == BLOCK and c.T == c.Tk
    nq = c.T // BLOCK
    qw, kw, vw = c.nh * c.dqk, c.nkv * c.dqk, c.nkv * c.dv

    def body(*refs):
        refs = list(refs)
        q_ref, kp_ref, kc_ref, vp_ref, vc_ref, do_ref, lse_ref, d_ref = refs[:8]
        iq_ref, ik_ref, iv_ref = refs[8:11] if init is not None else (None, None, None)
        dq_ref, dk_ref, dv_ref, dk_c, dv_c = refs[-5:]
        n = pl.program_id(1)

        def plus(val, ref, sl):
            return val if ref is None else val + ref[:, sl]

        @pl.when(n == 0)
        def _():
            dk_c[...] = jnp.zeros_like(dk_c)
            dv_c[...] = jnp.zeros_like(dv_c)

        @pl.when(n < nq)
        def _():
            mask = _band_mask(c, n)
            k2 = jnp.concatenate([kp_ref[...], kc_ref[...]], axis=0)
            v2 = jnp.concatenate([vp_ref[...], vc_ref[...]], axis=0)
            dk2, dv2 = [None] * c.nkv, [None] * c.nkv
            for j in range(c.nh):
                g = j // c.rep
                qs, os_ = slice(j * c.dqk, (j + 1) * c.dqk), slice(j * c.dv, (j + 1) * c.dv)
                qh, doh = q_ref[:, qs], do_ref[:, os_]
                kh, vh = k2[:, g * c.dqk:(g + 1) * c.dqk], v2[:, g * c.dv:(g + 1) * c.dv]
                s = jnp.where(mask, _dot(qh, kh, NT) * c.scale, -jnp.inf)
                p = jnp.exp(s - lse_ref[:, j:j + 1])
                ds = p * (_dot(doh, vh, NT) - d_ref[:, j:j + 1]) * c.scale
                dq_ref[:, qs] = plus(_dot(ds, kh, NN), iq_ref, qs)
                dvh, dkh = _dot(p, doh, TN), _dot(ds, qh, TN)
                dv2[g] = dvh if dv2[g] is None else dv2[g] + dvh
                dk2[g] = dkh if dk2[g] is None else dk2[g] + dkh
            for g in range(c.nkv):
                ks, vs = slice(g * c.dqk, (g + 1) * c.dqk), slice(g * c.dv, (g + 1) * c.dv)
                dk_ref[:, ks] = plus(dk_c[:, ks] + dk2[g][:BLOCK], ik_ref, ks)
                dv_ref[:, vs] = plus(dv_c[:, vs] + dv2[g][:BLOCK], iv_ref, vs)
                dk_c[:, ks] = dk2[g][BLOCK:]
                dv_c[:, vs] = dv2[g][BLOCK:]

        @pl.when(n == nq)
        def _():
            dk_ref[...] = plus(dk_c[...], ik_ref, slice(None))
            dv_ref[...] = plus(dv_c[...], iv_ref, slice(None))

    cur = lambda n: jnp.minimum(n, nq - 1)
    prev = lambda n: jnp.maximum(cur(n) - 1, 0)
    out_blk = lambda n: jnp.maximum(n - 1, 0)
    stat = pl.BlockSpec((BLOCK, LANES), lambda g, n: (cur(n), g))
    dq_spec = pl.BlockSpec((BLOCK, qw), lambda g, n: (cur(n), g))
    dk_spec = pl.BlockSpec((BLOCK, kw), lambda g, n: (out_blk(n), g))
    dv_spec = pl.BlockSpec((BLOCK, vw), lambda g, n: (out_blk(n), g))
    in_specs = [
        pl.BlockSpec((BLOCK, qw), lambda g, n: (cur(n), c.qcol(g))),
        pl.BlockSpec((BLOCK, kw), lambda g, n: (prev(n), c.kcol(g))),
        pl.BlockSpec((BLOCK, kw), lambda g, n: (cur(n), c.kcol(g))),
        pl.BlockSpec((BLOCK, vw), lambda g, n: (prev(n), c.vcol(g))),
        pl.BlockSpec((BLOCK, vw), lambda g, n: (cur(n), c.vcol(g))),
        pl.BlockSpec((BLOCK, c.nh * c.dv), lambda g, n: (cur(n), c.ocol(g))),
        stat, stat,
    ]
    args = [q, k, k, v, v, do, lse, delta]
    if init is not None:
        in_specs += [dq_spec, dk_spec, dv_spec]
        args += list(init)
    return _pcall(
        body, name=name, dims=("parallel", "arbitrary"), grid=(c.G, nq + 1), in_specs=in_specs,
        out_specs=[dq_spec, dk_spec, dv_spec],
        out_shape=[_sds((c.T, c.G * qw)), _sds((c.T, c.G * kw)), _sds((c.T, c.G * vw))],
        scratch_shapes=[pltpu.VMEM((BLOCK, kw), F32), pltpu.VMEM((BLOCK, vw), F32)],
    )(*args)


def _causal_pairs(n, kv_major):
    pairs =[(i, j) for j in range(n) for i in range(j, n)] if kv_major else [(i, j) for i in range(n) for j in range(i + 1)]
    return jnp.asarray(np.array([p[0] for p in pairs], np.int32)), jnp.asarray(np.array([p[1] for p in pairs], np.int32))


def _causal_mask(t):
    return lax.broadcasted_iota(jnp.int32, (t, t), 0) >= lax.broadcasted_iota(jnp.int32, (t, t), 1)


def _carrying(body, n_in, n_out, n_scratch, grid, carry):
    if carry is None:
        return body
    G, P = grid

    def wrapped(*refs):
        refs = list(refs)
        prefetch, refs = refs[:2], refs[2:]
        ins, src = refs[:n_in], refs[n_in]
        outs, out = refs[n_in + 1:n_in + 1 + n_out], refs[n_in + 1 + n_out]
        scratch, sems = refs[n_in + 2 + n_out:n_in + 2 + n_out + n_scratch], refs[n_in + 2 + n_out + n_scratch:]
        step = pl.program_id(0) * P + pl.program_id(1)
        carry.run([src, out] + sems, step, G * P, at_end=False)
        body(*prefetch, *ins, *outs, *scratch)
        carry.run([src, out] + sems, step, G * P, at_end=True)

    return wrapped


def _carry_specs(carry):
    if carry is None:
        return [], [], [], [], []
    any_space = pl.BlockSpec(memory_space=pl.ANY)
    return [any_space], [any_space], [carry.out_shape], list(carry.sems), [carry.src]


def _causal_fwd(cfg, q, k, v, *, name, out_dtype=F32, carry=None):
    c = cfg
    assert c.mode == "causal" and c.tq == c.tk and c.T == c.Tk
    t, n = c.tq, c.T // c.tq
    qi_tab, kj_tab = _causal_pairs(n, kv_major=False)
    n_pairs = int(qi_tab.shape[0])

    def body(qi_ref, kj_ref, q_ref, k_ref, v_ref, o_ref, lse_ref, m_scr, l_scr, acc):
        pair = pl.program_id(1)
        qi, kj = qi_ref[pair], kj_ref[pair]

        @pl.when(kj == 0)
        def _():
            m_scr[...] = jnp.full_like(m_scr, NEG_BIG)
            l_scr[...] = jnp.zeros_like(l_scr)
            acc[...] = jnp.zeros_like(acc)

        def step(diagonal):
            mask = None
            if diagonal:
                mask = lax.broadcasted_iota(jnp.int32, (t, t), 1) >= lax.broadcasted_iota(jnp.int32, (t, t), 0)
            for j in range(c.nh):
                g = j // c.rep
                s = _dot(k_ref[:, g * c.dqk:(g + 1) * c.dqk], q_ref[:, j * c.dqk:(j + 1) * c.dqk], NT) * c.scale
                if diagonal:
                    s = jnp.where(mask, s, -jnp.inf)
                m_prev = m_scr[j]
                m_new = jnp.maximum(m_prev, jnp.max(s, axis=0, keepdims=True))
                alpha = jnp.exp(m_prev - m_new)
                p = jnp.exp(s - m_new)
                l_scr[j] = alpha * l_scr[j] + jnp.sum(p, axis=0, keepdims=True)
                acc[j] = alpha * acc[j] + _dot(v_ref[:, g * c.dv:(g + 1) * c.dv], p, TN)
                m_scr[j] = m_new

        pl.when(kj == qi)(lambda: step(True))
        pl.when(kj != qi)(lambda: step(False))

        @pl.when(kj == qi)
        def _():
            rows = []
            for j in range(c.nh):
                o_ref[:, j * c.dv:(j + 1) * c.dv] = (acc[j] / l_scr[j]).T.astype(o_ref.dtype)
                rows.append(m_scr[j] + jnp.log(l_scr[j]))
            rows.append(jnp.zeros((LANES - c.nh, t), F32))
            lse_ref[...] = jnp.concatenate(rows, axis=0).T

    x_in, x_out, x_shapes, x_scratch, x_args = _carry_specs(carry)
    grid_spec = pltpu.PrefetchScalarGridSpec(
        num_scalar_prefetch=2, grid=(c.G, n_pairs),
        in_specs=[pl.BlockSpec((t, c.nh * c.dqk), lambda g, p, qi, kj: (qi[p], c.qcol(g))),
                  pl.BlockSpec((t, c.nkv * c.dqk), lambda g, p, qi, kj: (kj[p], c.kcol(g))),
                  pl.BlockSpec((t, c.nkv * c.dv), lambda g, p, qi, kj: (kj[p], c.vcol(g)))] + x_in,
        out_specs=[pl.BlockSpec((t, c.nh * c.dv), lambda g, p, qi, kj: (qi[p], c.ocol(g))),
                   pl.BlockSpec((t, LANES), lambda g, p, qi, kj: (qi[p], g))] + x_out,
        scratch_shapes=[pltpu.VMEM((c.nh, 1, t), F32), pltpu.VMEM((c.nh, 1, t), F32),
                        pltpu.VMEM((c.nh, c.dv, t), F32)] + x_scratch)
    return _pcall(
        _carrying(body, 3, 2, 3, (c.G, n_pairs), carry), name=name,
        dims=("arbitrary", "arbitrary") if carry is not None else ("parallel", "arbitrary"), grid_spec=grid_spec,
        out_shape=[jax.ShapeDtypeStruct((c.T, c.o_width), out_dtype),
                   jax.ShapeDtypeStruct((c.T, LANES * c.G), F32)] + x_shapes,
    )(qi_tab, kj_tab, q, k, v, *x_args)


def _causal_bwd(cfg, q, k, v, do, lse, delta, *, name, carry=None):
    c = cfg
    assert c.mode == "causal" and c.tq == c.tk and c.T == c.Tk
    t, n = c.tq, c.T // c.tq
    qw, kw, vw = c.nh * c.dqk, c.nkv * c.dqk, c.nkv * c.dv
    qi_tab, kj_tab = _causal_pairs(n, kv_major=True)

    def body(qi_ref, kj_ref, q_ref, k_ref, v_ref, do_ref, lse_ref, d_ref, dq_ref, dk_ref, dv_ref, dk_acc, dv_acc):
        pair = pl.program_id(1)
        qi, kj = qi_ref[pair], kj_ref[pair]

        @pl.when(pair == 0)
        def _():
            dq_ref[...] = jnp.zeros_like(dq_ref)

        @pl.when(qi == kj)
        def _():
            dk_acc[...] = jnp.zeros_like(dk_acc)
            dv_acc[...] = jnp.zeros_like(dv_acc)

        rows = pl.ds(pl.multiple_of(qi * t, t), t)

        def step(diagonal):
            mask = _causal_mask(t) if diagonal else None
            for j in range(c.nh):
                g = j // c.rep
                qs, ks, vs = (slice(j * c.dqk, (j + 1) * c.dqk), slice(g * c.dqk, (g + 1) * c.dqk),
                              slice(g * c.dv, (g + 1) * c.dv))
                qh, doh, kh = q_ref[:, qs], do_ref[:, j * c.dv:(j + 1) * c.dv], k_ref[:, ks]
                s = _dot(qh, kh, NT) * c.scale
                if diagonal:
                    s = jnp.where(mask, s, -jnp.inf)
                p = jnp.exp(s - lse_ref[:, j:j + 1])
                ds = p * (_dot(doh, v_ref[:, vs], NT) - d_ref[:, j:j + 1]) * c.scale
                dq_ref[rows, qs] += _dot(ds, kh, NN)
                dv_acc[:, vs] += _dot(p, doh, TN)
                dk_acc[:, ks] += _dot(ds, qh, TN)

        pl.when(qi == kj)(lambda: step(True))
        pl.when(qi != kj)(lambda: step(False))

        @pl.when(qi == n - 1)
        def _():
            dk_ref[...] = dk_acc[...]
            dv_ref[...] = dv_acc[...]

    stat = pl.BlockSpec((t, LANES), lambda g, p, qi, kj: (qi[p], g))
    n_pairs = int(qi_tab.shape[0])
    x_in, x_out, x_shapes, x_scratch, x_args = _carry_specs(carry)
    grid_spec = pltpu.PrefetchScalarGridSpec(
        num_scalar_prefetch=2, grid=(c.G, n_pairs),
        in_specs=[pl.BlockSpec((t, qw), lambda g, p, qi, kj: (qi[p], c.qcol(g))),
                  pl.BlockSpec((t, kw), lambda g, p, qi, kj: (kj[p], c.kcol(g))),
                  pl.BlockSpec((t, vw), lambda g, p, qi, kj: (kj[p], c.vcol(g))),
                  pl.BlockSpec((t, c.nh * c.dv), lambda g, p, qi, kj: (qi[p], c.ocol(g))),
                  stat, stat] + x_in,
        out_specs=[pl.BlockSpec((c.T, qw), lambda g, p, qi, kj: (0, g)),
                   pl.BlockSpec((t, kw), lambda g, p, qi, kj: (kj[p], g)),
                   pl.BlockSpec((t, vw), lambda g, p, qi, kj: (kj[p], g))] + x_out,
        scratch_shapes=[pltpu.VMEM((t, kw), F32), pltpu.VMEM((t, vw), F32)] + x_scratch)
    return _pcall(
        _carrying(body, 6, 3, 2, (c.G, n_pairs), carry), name=name,
        dims=("arbitrary", "arbitrary") if carry is not None else ("parallel", "arbitrary"), grid_spec=grid_spec,
        out_shape=[_sds((c.T, c.G * qw)), _sds((c.T, c.G * kw)), _sds((c.T, c.G * vw))] + x_shapes,
    )(qi_tab, kj_tab, q, k, v, do, lse, delta, *x_args)


def _rowwise(body, ins, outs, *, name, rows, tm=512, accs=(), scratch=()):
    tm = _row_tile(rows, tm)

    def spec(a):
        if a.shape[0] == 1:
            return pl.BlockSpec((1, a.shape[1]), lambda i: (0, 0))
        d = rows // a.shape[0]
        assert d * a.shape[0] == rows and tm % d == 0
        return pl.BlockSpec((tm // d, a.shape[1]), lambda i: (i, 0))

    return _pcall(
        functools.partial(body, tm), name=name, dims=("arbitrary" if accs else "parallel",), grid=(rows // tm,),
        in_specs=[spec(a) for a in ins], out_specs=[spec(a) for a in outs], out_shape=list(outs),
        scratch_shapes=list(scratch),
    )(*ins)


def _sds(shape, dtype=F32):
    return jax.ShapeDtypeStruct(shape, dtype)


def _acc_rows(ref, val):
    @pl.when(pl.program_id(0) == 0)
    def _():
        ref[...] = jnp.zeros_like(ref)

    ref[...] += jnp.sum(val, axis=0, keepdims=True)


Z_QA, Z_KA, Z_VA, Z_CQ, Z_CKV, Z_KR, Z_END = 0, 512, 640, 768, 1152, 1408, 1536


def _l0_prep(z, tabs, q_norm, kv_norm, *, name):
    S = z.shape[0]

    def body(tm, z_ref, c64, s64, ck, sk, gq, gkv, qa_o, ka_o, va_o, cq_o, ckv_o, kr_o):
        for i in range(4):
            sl = slice(Z_QA + i * LANES, Z_QA + (i + 1) * LANES)
            qa_o[:, i * LANES:(i + 1) * LANES] = _rope_chunk(z_ref[:, sl], c64[...], s64[...], 32).astype(qa_o.dtype)
        ka_o[...] = _rope_chunk(z_ref[:, Z_KA:Z_VA], c64[...], s64[...], 32).astype(ka_o.dtype)
        va_o[...] = z_ref[:, Z_VA:Z_CQ].astype(va_o.dtype)
        cq_o[...] = (_rms_parts(z_ref[:, Z_CQ:Z_CKV])[0] * gq[...]).astype(cq_o.dtype)
        ckv_o[...] = (_rms_parts(z_ref[:, Z_CKV:Z_KR])[0] * gkv[...]).astype(ckv_o.dtype)
        kr_o[...] = _rope_chunk(z_ref[:, Z_KR:Z_END], ck[...], sk[...], 16)

    outs = [_sds((S, 512), MXU_DTYPE), _sds((S, 128), MXU_DTYPE), _sds((S, 128), MXU_DTYPE),
            _sds((S, MLA_Q_RANK), MXU_DTYPE), _sds((S, MLA_KV_RANK), MXU_DTYPE), _sds((S, LANES))]
    ins = [z, tabs["c64"], tabs["s64"], tabs["ck"], tabs["sk"], q_norm.reshape(1, -1), kv_norm.reshape(1, -1)]
    return _rowwise(body, ins, outs, name=name, rows=S)


def _l0_prep_bwd(z, tabs, q_norm, kv_norm, dqa, dka, dva, dcq, dckv, dkr, *, name):
    S = z.shape[0]

    def body(tm, z_ref, c64, s64, ck, sk, gq, gkv, dqa_r, dka_r, dva_r, dcq_r, dckv_r, dkr_r, dz_o, dgq_o, dgkv_o):
        for i in range(4):
            sl = slice(i * LANES, (i + 1) * LANES)
            dz_o[:, sl] = _rope_chunk(dqa_r[:, sl].astype(F32), c64[...], -s64[...], 32).astype(dz_o.dtype)
        dz_o[:, Z_KA:Z_VA] = _rope_chunk(dka_r[...].astype(F32), c64[...], -s64[...], 32).astype(dz_o.dtype)
        dz_o[:, Z_VA:Z_CQ] = dva_r[...].astype(dz_o.dtype)
        dx, dgp = _rms_bwd_rows(z_ref[:, Z_CQ:Z_CKV], gq[...], dcq_r[...].astype(F32))
        dz_o[:, Z_CQ:Z_CKV] = dx.astype(dz_o.dtype)
        _acc_rows(dgq_o, dgp)
        dx, dgp = _rms_bwd_rows(z_ref[:, Z_CKV:Z_KR], gkv[...], dckv_r[...].astype(F32))
        dz_o[:, Z_CKV:Z_KR] = dx.astype(dz_o.dtype)
        _acc_rows(dgkv_o, dgp)
        dz_o[:, Z_KR:Z_END] = _rope_chunk(dkr_r[...], ck[...], -sk[...], 16).astype(dz_o.dtype)

    outs = [_sds((S, Z_END), MXU_DTYPE), _sds((1, MLA_Q_RANK)), _sds((1, MLA_KV_RANK))]
    ins = [z, tabs["c64"], tabs["s64"], tabs["ck"], tabs["sk"], q_norm.reshape(1, -1), kv_norm.reshape(1, -1),
           dqa, dka, dva, dcq, dckv, dkr]
    return _rowwise(body, ins, outs, name=name, rows=S, accs=(1, 2))


def _mla_prep(qb, kvb, kr, tabs, *, name):
    S = qb.shape[0]

    def body(tm, qb_r, kvb_r, kr_r, cm, sm, q_o, k_o, v_o):
        lane = _lane((tm, LANES))
        kr_at_64 = pltpu.roll(kr_r[...], 64, 1)
        for h in range(MLA_HEADS):
            sl = slice(h * LANES, (h + 1) * LANES)
            q_o[:, sl] = _rope_chunk(qb_r[:, sl], cm[...], sm[...], 16).astype(q_o.dtype)
            k_o[:, sl] = jnp.where(lane < 64, kvb_r[:, sl], kr_at_64).astype(k_o.dtype)
        for p in range(MLA_HEADS // 2):
            even = pltpu.roll(kvb_r[:, (2 * p) * LANES:(2 * p + 1) * LANES], 64, 1)
            odd = kvb_r[:, (2 * p + 1) * LANES:(2 * p + 2) * LANES]
            v_o[:, p * LANES:(p + 1) * LANES] = jnp.where(lane < 64, even, odd).astype(v_o.dtype)

    outs = [_sds((S, 1024), MXU_DTYPE), _sds((S, 1024), MXU_DTYPE), _sds((S, 512), MXU_DTYPE)]
    return _rowwise(body, [qb, kvb, kr, tabs["cm"], tabs["sm"]], outs, name=name, rows=S)


def _mla_prep_bwd(dq, dk, dv, tabs, *, name):
    S = dq.shape[0]

    def body(tm, dq_r, dk_r, dv_r, cm, sm, dqb_o, dkvb_o, dkr_o):
        lane = _lane((tm, LANES))
        dkr = jnp.zeros((tm, LANES), F32)
        for h in range(MLA_HEADS):
            sl = slice(h * LANES, (h + 1) * LANES)
            dqb_o[:, sl] = _rope_chunk(dq_r[:, sl].astype(F32), cm[...], -sm[...], 16).astype(dqb_o.dtype)
            dkh = dk_r[:, sl].astype(F32)
            dvp = dv_r[:, (h // 2) * LANES:(h // 2 + 1) * LANES].astype(F32)
            dvh = pltpu.roll(dvp, 64, 1) if h % 2 == 0 else dvp
            dkvb_o[:, sl] = jnp.where(lane < 64, dkh, dvh).astype(dkvb_o.dtype)
            dkr = dkr + pltpu.roll(dkh, 64, 1)
        dkr_o[...] = jnp.where(lane < MLA_ROPE, dkr, 0.0)

    outs = [_sds((S, 1024), MXU_DTYPE), _sds((S, 1024), MXU_DTYPE), _sds((S, LANES))]
    return _rowwise(body, [dq, dk, dv, tabs["cm"], tabs["sm"]], outs, name=name, rows=S)


DILATIONS = tuple(d for _, d in DIL_PATTERNS)
QKV_CHUNKS = 8


def _to_branch(nat, c0, chunks, out_ref, d, rows):
    width = chunks * LANES
    for r in range(d):
        tok = pl.ds(r, rows // d, stride=d) if d > 1 else slice(None)
        for c in range(chunks):
            out_ref[:, r * width + c * LANES:r * width + (c + 1) * LANES] = nat[c0 + c, tok, :].astype(out_ref.dtype)


def _from_branch(in_ref, nat, c0, chunks, d, rows, add=False):
    width = chunks * LANES
    for r in range(d):
        tok = pl.ds(r, rows // d, stride=d) if d > 1 else slice(None)
        for c in range(chunks):
            val = in_ref[:, r * width + c * LANES:r * width + (c + 1) * LANES].astype(F32)
            nat[c0 + c, tok, :] = nat[c0 + c, tok, :] + val if add else val


def _branch_sds(S, width, d, dtype):
    return _sds((S // d, d * width), dtype)


def _l1_prep(qkv, tabs, *, name):
    S = qkv.shape[0]

    def body(tm, x_r, c64, s64, *rest):
        outs, nat = rest[:-1], rest[-1]
        for i in range(QKV_CHUNKS):
            sl = slice(i * LANES, (i + 1) * LANES)
            nat[i] = _rope_chunk(x_r[:, sl], c64[...], s64[...], 32)
            nat[QKV_CHUNKS + i] = _rope_chunk(x_r[:, 1024 + i * LANES:1024 + (i + 1) * LANES], c64[...], s64[...], 32)
            nat[2 * QKV_CHUNKS + i] = x_r[:, 2048 + i * LANES:2048 + (i + 1) * LANES]
        for b, d in enumerate(DILATIONS):
            for t in range(3):
                _to_branch(nat, t * QKV_CHUNKS, QKV_CHUNKS, outs[3 * b + t], d, tm)

    outs = [_branch_sds(S, 1024, d, MXU_DTYPE) for d in DILATIONS for _ in range(3)]
    got = _rowwise(body, [qkv, tabs["c64"], tabs["s64"]], outs, name=name, rows=S,
                   scratch=[pltpu.VMEM((3 * QKV_CHUNKS, _row_tile(S, 512), LANES), F32)])
    return {d: tuple(got[3 * b:3 * b + 3]) for b, d in enumerate(DILATIONS)}


def _l1_prep_bwd(grads, tabs, *, name):
    S = grads[1][0].shape[0]

    def body(tm, *rest):
        ins, (c64, s64, o, nat) = rest[:9], rest[9:]
        for b, d in enumerate(DILATIONS):
            for t in range(3):
                _from_branch(ins[3 * b + t], nat, t * QKV_CHUNKS, QKV_CHUNKS, d, tm, add=b > 0)
        for i in range(QKV_CHUNKS):
            sl = slice(i * LANES, (i + 1) * LANES)
            o[:, sl] = _rope_chunk(nat[i], c64[...], -s64[...], 32).astype(o.dtype)
            o[:, 1024 + i * LANES:1024 + (i + 1) * LANES] = _rope_chunk(
                nat[QKV_CHUNKS + i], c64[...], -s64[...], 32).astype(o.dtype)
            o[:, 2048 + i * LANES:2048 + (i + 1) * LANES] = nat[2 * QKV_CHUNKS + i].astype(o.dtype)

    ins = [g for d in DILATIONS for g in grads[d]] + [tabs["c64"], tabs["s64"]]
    return _rowwise(body, ins, [_sds((S, 3072), MXU_DTYPE)], name=name, rows=S, tm=256,
                    scratch=[pltpu.VMEM((3 * QKV_CHUNKS, _row_tile(S, 256), LANES), F32)])[0]


def _sigmoid(x):
    return 1.0 / (1.0 + jnp.exp(-x))


FFN_ROW_TILE, FFN_COL_TILE = 512, 1408


def _gate_up(h, w_gate, w_up, *, name):
    (M, K), N = h.shape, w_gate.shape[1]
    tm, tn = _tile(M, FFN_ROW_TILE), _tile(N, FFN_COL_TILE)

    def body(h_ref, wg_ref, wu_ref, g_ref, u_ref, a_ref):
        g = _dot(h_ref[...], wg_ref[...], NN)
        u = _dot(h_ref[...], wu_ref[...], NN)
        g_ref[...] = g
        u_ref[...] = u
        a_ref[...] = (g * _sigmoid(g) * u).astype(a_ref.dtype)

    w_spec = pl.BlockSpec((K, tn), lambda j, i: (0, j))
    o_spec = pl.BlockSpec((tm, tn), lambda j, i: (i, j))
    return _pcall(
        body, name=name, dims=("parallel", "parallel"), grid=(N // tn, M // tm),
        in_specs=[pl.BlockSpec((tm, K), lambda j, i: (i, 0)), w_spec, w_spec], out_specs=[o_spec] * 3,
        out_shape=[_sds((M, N)), _sds((M, N)), _sds((M, N), MXU_DTYPE)],
    )(h, w_gate, w_up)


def _gate_up_bwd(dx, w_down, gate, up, *, name):
    (M, K), N = dx.shape, w_down.shape[0]
    tm, tn = _tile(M, FFN_ROW_TILE), _tile(N, FFN_COL_TILE)

    def body(dx_ref, w_ref, g_ref, u_ref, dg_ref, du_ref):
        d = _dot(dx_ref[...], w_ref[...], NT)
        g = g_ref[...]
        sg = _sigmoid(g)
        dg_ref[...] = (d * u_ref[...] * (sg * (1.0 + g * (1.0 - sg)))).astype(dg_ref.dtype)
        du_ref[...] = (d * g * sg).astype(du_ref.dtype)

    o_spec = pl.BlockSpec((tm, tn), lambda j, i: (i, j))
    return _pcall(
        body, name=name, dims=("parallel", "parallel"), grid=(N // tn, M // tm),
        in_specs=[pl.BlockSpec((tm, K), lambda j, i: (i, 0)), pl.BlockSpec((tn, K), lambda j, i: (j, 0)),
                  o_spec, o_spec],
        out_specs=[o_spec] * 2, out_shape=[_sds((M, N), MXU_DTYPE)] * 2,
    )(dx, w_down, gate, up)


def _head_pair_weights(w, c, rows):
    return jnp.where(_lane((rows, LANES)) < HEAD_DIM, w[:, 2 * c:2 * c + 1], w[:, 2 * c + 1:2 * c + 2])


def _merge(outs_by_d, lses_by_d, *, name):
    S = outs_by_d[1].shape[0]
    far = DILATIONS[1:]

    def body(tm, o1, o4, o16, l1, l4, l16, o_o, w1_o, w4_o, w16_o, nat_o, nat_l):
        for b, (o_r, l_r, d) in enumerate(zip((o4, o16), (l4, l16), far)):
            _from_branch(o_r, nat_o, b * QKV_CHUNKS, QKV_CHUNKS, d, tm)
            _from_branch(l_r, nat_l, b, 1, d, tm)
        ls = [l1[...], nat_l[0], nat_l[1]]
        m = jnp.maximum(jnp.maximum(ls[0], ls[1]), ls[2])
        es = [jnp.exp(l - m) for l in ls]
        tot = es[0] + es[1] + es[2]
        ws = [e / tot for e in es]
        for w_o, w in zip((w1_o, w4_o, w16_o), ws):
            w_o[...] = w
        for c in range(QKV_CHUNKS):
            sl = slice(c * LANES, (c + 1) * LANES)
            parts = (o1[:, sl], nat_o[c], nat_o[QKV_CHUNKS + c])
            o_o[:, sl] = sum(_head_pair_weights(w, c, tm) * part for w, part in zip(ws, parts))

    ins = [outs_by_d[d] for d in DILATIONS] + [lses_by_d[d] for d in DILATIONS]
    outs = [_sds((S, 1024))] + [_sds((S, LANES))] * 3
    rows = _row_tile(S, 256)
    return _rowwise(body, ins, outs, name=name, rows=S, tm=256,
                    scratch=[pltpu.VMEM((2 * QKV_CHUNKS, rows, LANES), F32), pltpu.VMEM((2, rows, LANES), F32)])


def _merge_bwd(do, o, ws, *, name):
    S = do.shape[0]

    def body(tm, do_r, o_r, w1, w4, w16, d1, d4, d16, e1, e4, e16, nat, nat_l):
        prod = do_r[...] * o_r[...]
        sums = _cols_to_lanes([jnp.sum(prod[:, j * HEAD_DIM:(j + 1) * HEAD_DIM], axis=1, keepdims=True)
                               for j in range(DIL_HEADS)], tm)
        for w_r, d_o, e_o, d in zip((w1, w4, w16), (d1, d4, d16), (e1, e4, e16), DILATIONS):
            w = w_r[...]
            nat_l[0] = w * sums
            _to_branch(nat_l, 0, 1, e_o, d, tm)
            for c in range(QKV_CHUNKS):
                nat[c] = _head_pair_weights(w, c, tm) * do_r[:, c * LANES:(c + 1) * LANES]
            _to_branch(nat, 0, QKV_CHUNKS, d_o, d, tm)

    outs = [_branch_sds(S, 1024, d, MXU_DTYPE) for d in DILATIONS] + [_branch_sds(S, LANES, d, F32) for d in DILATIONS]
    rows = _row_tile(S, 256)
    got = _rowwise(body, [do, o] + [ws[d] for d in DILATIONS], outs, name=name, rows=S, tm=256,
                   scratch=[pltpu.VMEM((QKV_CHUNKS, rows, LANES), F32), pltpu.VMEM((1, rows, LANES), F32)])
    return dict(zip(DILATIONS, got[:3])), dict(zip(DILATIONS, got[3:]))


def _loss_head(x, g, target, *, name):
    S, D = x.shape

    def body(tm, x_r, g_r, t_r, dx_o, dg_o, sq_o):
        xf = x_r[...]
        xhat, _ = _rms_parts(xf)
        err = xhat * g_r[...] - t_r[...]
        dx, dgp = _rms_bwd_rows(xf, g_r[...], err * (1.0 / D))
        dx_o[...] = dx
        _acc_rows(dg_o, dgp)
        _acc_rows(sq_o, err * err)

    return _rowwise(body, [x, g.reshape(1, D), target], [_sds((S, D)), _sds((1, D)), _sds((1, D))],
                    name=name, rows=S, accs=(1, 2))


def _adamw(w, g, m, v, *, name):
    c1 = 1.0 - ADAM_B1 ** ADAM_STEP
    c2 = 1.0 - ADAM_B2 ** ADAM_STEP

    def body(tm, w_r, g_r, m_r, v_r, d_o, m_o, v_o):
        g = g_r[...]
        m_new = ADAM_B1 * m_r[...] + (1.0 - ADAM_B1) * g
        v_new = ADAM_B2 * v_r[...] + (1.0 - ADAM_B2) * (g * g)
        m_o[...] = m_new
        v_o[...] = v_new
        d_o[...] = -ADAM_LR * ((m_new / c1) / (jnp.sqrt(v_new / c2) + ADAM_EPS) + ADAM_WD * w_r[...])

    return _rowwise(body, [w, g, m, v], [_sds(w.shape)] * 3, name=name, rows=w.shape[0], tm=256)


SUM_ROW_TILE = 256


def _sum_cores(grads, theirs, half_index, *, name):
    _, R, C = grads.shape
    h = R // 2
    nb = h // SUM_ROW_TILE

    def body(c_ref, g_ref, t_ref, o_ref):
        o_ref[...] = (g_ref[...].astype(F32) + t_ref[...].astype(F32)).astype(o_ref.dtype)

    grid_spec = pltpu.PrefetchScalarGridSpec(
        num_scalar_prefetch=1, grid=(4, nb),
        in_specs=[pl.BlockSpec((1, SUM_ROW_TILE, C), lambda k, i, c_ref: (k, c_ref[0] * nb + i, 0)),
                  pl.BlockSpec((1, SUM_ROW_TILE, C), lambda k, i, c_ref: (k, i, 0))],
        out_specs=pl.BlockSpec((1, SUM_ROW_TILE, C), lambda k, i, c_ref: (k, i, 0)))
    return _pcall(body, name=name, dims=("parallel", "parallel"), grid_spec=grid_spec,
                  out_shape=_sds((4, h, C), grads.dtype))(half_index, grads, theirs)


def _sum_chips(parts, half_index, *, name):
    _, h, C = parts.shape
    nb = h // SUM_ROW_TILE

    def body(c_ref, p_ref, o_ref):
        p = [p_ref[k].astype(F32) for k in range(4)]
        o_ref[...] = ((p[0] + p[1]) + p[2]) + p[3]

    grid_spec = pltpu.PrefetchScalarGridSpec(
        num_scalar_prefetch=1, grid=(nb,),
        in_specs=[pl.BlockSpec((4, SUM_ROW_TILE, C), lambda i, c_ref: (0, i, 0))],
        out_specs=pl.BlockSpec((SUM_ROW_TILE, C), lambda i, c_ref: (c_ref[0] * nb + i, 0)))
    return _pcall(body, name=name, dims=("parallel",), grid_spec=grid_spec,
                  out_shape=_sds((2 * h, C)))(half_index, parts)


def _position():
    return lax.axis_index("x"), lax.axis_index("y"), lax.axis_index("c")


def _chip_peers(x, y):
    return [(1 - x, y), (x, 1 - y), (1 - x, 1 - y)]


_HBM = pl.BlockSpec(memory_space=pltpu.HBM)
LOCAL_COPY_CHUNKS = 8


def _local_copies(src_ref, dst_ref, sems):
    rows = src_ref.shape[0] // LOCAL_COPY_CHUNKS
    assert rows * LOCAL_COPY_CHUNKS == src_ref.shape[0]
    return [pltpu.make_async_copy(src_ref.at[pl.ds(i * rows, rows)], dst_ref.at[pl.ds(i * rows, rows)], sems.at[i])
            for i in range(LOCAL_COPY_CHUNKS)]


class _Exchange:
    def __init__(self, src, out_shape, sems, stages):
        self.src, self.out_shape, self.sems, self.stages = src, out_shape, sems, stages

    def run(self, refs, step, n_steps, at_end):
        for fraction, fn in self.stages:
            if (fraction == 1.0) == at_end:
                pl.when(step == int(round(fraction * (n_steps - 1))))(functools.partial(fn, *refs))


def _run_exchange(ex, *, name):
    def body(*refs):
        for _, fn in ex.stages:
            fn(*refs)

    return pl.pallas_call(
        body, name=name, in_specs=[_HBM], out_specs=_HBM, out_shape=ex.out_shape, scratch_shapes=list(ex.sems),
    )(ex.src)


def _gather_exchange(src):
    R, C = src.shape
    h = R // 2

    def plan(src_ref, out_ref, send_sems, recv_sems, local_sems):
        x, y, c = _position()
        me = 2 * x + y
        peers = _chip_peers(x, y)
        mine, other = pl.ds(c * h, h), pl.ds((1 - c) * h, h)

        def copy(sem, src_part, dst_part, device):
            return pltpu.make_async_remote_copy(
                src_ref=src_part, dst_ref=dst_part, send_sem=send_sems.at[sem], recv_sem=recv_sems.at[sem],
                device_id=device, device_id_type=MESH)

        landed = [out_ref.at[2 * px + py, mine] for px, py in peers]
        theirs = [out_ref.at[2 * px + py, other] for px, py in peers]
        return dict(
            sends=lambda: [copy(j, src_ref.at[mine], out_ref.at[me, mine], (px, py, c))
                           for j, (px, py) in enumerate(peers)],
            local=lambda: _local_copies(src_ref, out_ref.at[me], local_sems),
            arrivals=lambda: [copy(j, landed[j], landed[j], (px, py, c)) for j, (px, py) in enumerate(peers)],
            passed=lambda: [copy(3 + j, landed[j], landed[j], (x, y, 1 - c)) for j in range(3)],
            from_sibling=lambda: [copy(3 + j, theirs[j], theirs[j], (x, y, 1 - c)) for j in range(3)])

    def start(*refs):
        p = plan(*refs)
        for cp in p["sends"]() + p["local"]():
            cp.start()

    def pass_on(*refs):
        p = plan(*refs)
        for arrival, forward in zip(p["arrivals"](), p["passed"]()):
            arrival.wait_recv()
            forward.start()

    def finish(*refs):
        p = plan(*refs)
        for cp in p["from_sibling"]():
            cp.wait_recv()
        for cp in p["sends"]() + p["passed"]():
            cp.wait_send()
        for cp in p["local"]():
            cp.wait()

    sems = [pltpu.SemaphoreType.DMA((6,)), pltpu.SemaphoreType.DMA((6,)), pltpu.SemaphoreType.DMA((LOCAL_COPY_CHUNKS,))]
    return _Exchange(src, jax.ShapeDtypeStruct((4, R, C), src.dtype), sems, [(0.0, start), (0.6, pass_on), (1.0, finish)])


def _swap_other_half(src, *, name):
    _, R, C = src.shape
    h = R // 2

    def body(src_ref, out_ref, send_sem, recv_sem):
        x, y, c = _position()
        cp = pltpu.make_async_remote_copy(
            src_ref=src_ref.at[:, pl.ds((1 - c) * h, h)], dst_ref=out_ref, send_sem=send_sem, recv_sem=recv_sem,
            device_id=(x, y, 1 - c), device_id_type=MESH)
        cp.start()
        cp.wait()

    return pl.pallas_call(
        body, name=name, in_specs=[_HBM], out_specs=_HBM, out_shape=jax.ShapeDtypeStruct((4, h, C), src.dtype),
        scratch_shapes=[pltpu.SemaphoreType.DMA, pltpu.SemaphoreType.DMA],
    )(src)


def _scatter_exchange(src):
    def plan(src_ref, out_ref, send_sems, recv_sems, local_sems):
        x, y, c = _position()
        me = 2 * x + y
        peers = _chip_peers(x, y)

        def copy(j, src_block, dst_slot):
            px, py = peers[j]
            return pltpu.make_async_remote_copy(
                src_ref=src_ref.at[src_block], dst_ref=out_ref.at[dst_slot], send_sem=send_sems.at[j],
                recv_sem=recv_sems.at[j], device_id=(px, py, c), device_id_type=MESH)

        return dict(sends=lambda: [copy(j, 2 * px + py, me) for j, (px, py) in enumerate(peers)],
                    arrivals=lambda: [copy(j, me, 2 * px + py) for j, (px, py) in enumerate(peers)],
                    local=lambda: _local_copies(src_ref.at[me], out_ref.at[me], local_sems))

    def start(*refs):
        p = plan(*refs)
        for cp in p["sends"]() + p["local"]():
            cp.start()

    def finish(*refs):
        p = plan(*refs)
        for cp in p["arrivals"]():
            cp.wait_recv()
        for cp in p["sends"]():
            cp.wait_send()
        for cp in p["local"]():
            cp.wait()

    sems = [pltpu.SemaphoreType.DMA((3,)), pltpu.SemaphoreType.DMA((3,)), pltpu.SemaphoreType.DMA((LOCAL_COPY_CHUNKS,))]
    return _Exchange(src, jax.ShapeDtypeStruct(src.shape, src.dtype), sems, [(0.0, start), (1.0, finish)])


def _join_halves(src, *, name):
    R, C = src.shape
    h = R // 2

    def body(src_ref, out_ref, send_sem, recv_sem):
        x, y, c = _position()
        mine, theirs = pl.ds(c * h, h), pl.ds((1 - c) * h, h)
        cp = pltpu.make_async_remote_copy(
            src_ref=src_ref.at[mine], dst_ref=out_ref.at[mine], send_sem=send_sem, recv_sem=recv_sem,
            device_id=(x, y, 1 - c), device_id_type=MESH)
        cp.start()
        pltpu.make_async_remote_copy(
            src_ref=src_ref.at[theirs], dst_ref=out_ref.at[theirs], send_sem=send_sem, recv_sem=recv_sem,
            device_id=(x, y, 1 - c), device_id_type=MESH).wait_recv()
        cp.wait_send()

    return pl.pallas_call(
        body, name=name, in_specs=[_HBM], out_specs=_HBM, out_shape=jax.ShapeDtypeStruct((R, C), src.dtype),
        input_output_aliases={0: 0},
        scratch_shapes=[pltpu.SemaphoreType.DMA, pltpu.SemaphoreType.DMA],
    )(src)


def _allreduce_small(vec, *, name):
    R, C = vec.shape

    def body(v_ref, o_ref, slots, send_sems, recv_sems):
        x, y, c = _position()
        me = 4 * x + 2 * y + c

        def peer(k):
            return x ^ ((k >> 2) & 1), y ^ ((k >> 1) & 1), c ^ (k & 1)

        def copy(k, slot):
            return pltpu.make_async_remote_copy(
                src_ref=v_ref, dst_ref=slots.at[slot], send_sem=send_sems.at[k - 1], recv_sem=recv_sems.at[k - 1],
                device_id=peer(k), device_id_type=MESH)

        slots[me] = v_ref[...]
        sends = [copy(k, me) for k in range(1, 8)]
        for cp in sends:
            cp.start()
        for k in range(1, 8):
            px, py, pc = peer(k)
            copy(k, 4 * px + 2 * py + pc).wait_recv()
        total = slots[0]
        for d in range(1, 8):
            total = total + slots[d]
        o_ref[...] = total
        for cp in sends:
            cp.wait_send()

    vmem = pl.BlockSpec(memory_space=pltpu.VMEM)
    return pl.pallas_call(
        body, name=name, in_specs=[vmem], out_specs=vmem, out_shape=jax.ShapeDtypeStruct((R, C), vec.dtype),
        scratch_shapes=[pltpu.VMEM((8, R, C), vec.dtype), pltpu.SemaphoreType.DMA((7,)), pltpu.SemaphoreType.DMA((7,))],
    )(vec)


def _cross_cfg(S, mem_len):
    return _Attn(T=S, Tk=mem_len, G=1, nh=X_HEADS, rep=1, dqk=X_HEAD_DIM, dv=X_HEAD_DIM, tq=512, tk=mem_len,
                 mode="none", scale=X_HEAD_DIM ** -0.5, qcol=lambda g: 0, kcol=lambda g: 0, vcol=lambda g: 1,
                 ocol=lambda g: 0, o_width=X_HEADS * X_HEAD_DIM)


def _swa_cfg(S):
    return _Attn(T=S, Tk=S, G=1, nh=SWA_HEADS, rep=SWA_HEADS // SWA_KV_HEADS, dqk=HEAD_DIM, dv=HEAD_DIM, tq=BLOCK,
                 tk=BLOCK, mode="band", max_dist=SWA_WINDOW - 1, scale=HEAD_DIM ** -0.5, qcol=lambda g: 0,
                 kcol=lambda g: 0, vcol=lambda g: 0, ocol=lambda g: 0, o_width=SWA_HEADS * HEAD_DIM)


def _mla_cfg(S):
    t = _tile(S, 512)
    return _Attn(T=S, Tk=S, G=MLA_HEADS // 2, nh=2, rep=1, dqk=LANES, dv=MLA_V, tq=t, tk=t, mode="causal",
                 scale=(MLA_NOPE + MLA_ROPE) ** -0.5, qcol=lambda g: g, kcol=lambda g: g, vcol=lambda g: g,
                 ocol=lambda g: g, o_width=MLA_HEADS * MLA_V)


def _dil_cfg(S, window, dil):
    return _Attn(T=S // dil, Tk=S // dil, G=dil, nh=DIL_HEADS, rep=1, dqk=HEAD_DIM, dv=HEAD_DIM, tq=BLOCK, tk=BLOCK,
                 mode="band", max_dist=window // dil, scale=HEAD_DIM ** -0.5, qcol=lambda g: g, kcol=lambda g: g,
                 vcol=lambda g: g, ocol=lambda g: g, o_width=dil * DIL_HEADS * HEAD_DIM)


def _rows_cfg(S):
    return _Attn(T=S, Tk=S, G=1, nh=DIL_HEADS, rep=1, dqk=HEAD_DIM, dv=HEAD_DIM, tq=BLOCK, tk=BLOCK, mode="none",
                 scale=1.0, qcol=lambda g: 0, kcol=lambda g: 0, vcol=lambda g: 0, ocol=lambda g: 0,
                 o_width=DIL_HEADS * HEAD_DIM)


def _cross_fwd(p, x, mem, W, vec):
    S = x.shape[0]
    cfg = _cross_cfg(S, mem.shape[0])
    hx = _rmsnorm(x, vec[p + "x_norm"], name=p + "x_norm")
    qx = _mm(hx, W[p + "w_xq"], mode="nn", name=p + "xq", out_dtype=MXU_DTYPE)
    memn = _rmsnorm(mem, vec[p + "mem_norm"], name=p + "mem_norm")
    kvx = _mm(memn, W[p + "w_xkv"], mode="nn", name=p + "xkv", out_dtype=MXU_DTYPE)
    ox, lse = _attn_fwd(cfg, qx, kvx, kvx, name=p + "x_attn", out_dtype=MXU_DTYPE)
    out = _mm(ox, W[p + "w_xo"], mode="nn", name=p + "xo", res=x)
    return out, (x, hx, qx, memn, kvx, ox, lse)


def _cross_bwd(p, dx, saved, mem, W, vec, dW, dvec):
    x, hx, qx, memn, kvx, ox, lse = saved
    cfg = _cross_cfg(x.shape[0], mem.shape[0])
    dox = _mm(dx, W[p + "w_xo"], mode="nt", name=p + "xo_dx", out_dtype=MXU_DTYPE)
    dW[p + "w_xo"] = _mm(ox, dx, mode="tn", name=p + "xo_dw")
    delta, _ = _attn_delta(cfg, ox, dox, name=p + "x_delta")
    dqx = _attn_dq(cfg, qx, kvx, kvx, dox, lse, delta, name=p + "x_dq", out_dtype=MXU_DTYPE)
    dkx, dvx = _attn_dkv(cfg, qx, kvx, kvx, dox, lse, delta, name=p + "x_dkv", out_dtype=MXU_DTYPE)
    dkvx = jnp.concatenate([dkx, dvx], axis=1)
    dhx = _mm(dqx, W[p + "w_xq"], mode="nt", name=p + "xq_dx")
    dW[p + "w_xq"] = _mm(hx, dqx, mode="tn", name=p + "xq_dw")
    dW[p + "w_xkv"] = _mm(memn, dkvx, mode="tn", name=p + "xkv_dw")
    dmemn = _mm(dkvx, W[p + "w_xkv"], mode="nt", name=p + "xkv_dx")
    _, dvec[p + "mem_norm"] = _rmsnorm_bwd(mem, vec[p + "mem_norm"], dmemn, name=p + "mem_norm_bwd")
    dx_in, dvec[p + "x_norm"] = _rmsnorm_bwd(x, vec[p + "x_norm"], dhx, name=p + "x_norm_bwd", dres=dx)
    return dx_in


def _ffn_fwd(p, x, W, vec):
    hf = _rmsnorm(x, vec[p + "ffn_norm"], name=p + "ffn_norm")
    gate, up, act = _gate_up(hf, W[p + "w_gate"], W[p + "w_up"], name=p + "gate_up")
    out = _mm(act, W[p + "w_down"], mode="nn", name=p + "down", res=x)
    return out, (x, hf, gate, up, act)


def _ffn_bwd(p, dx, saved, W, vec, dW, dvec):
    x, hf, gate, up, act = saved
    dW[p + "w_down"] = _mm(act, dx, mode="tn", name=p + "down_dw")
    dgate, dup = _gate_up_bwd(dx, W[p + "w_down"], gate, up, name=p + "gate_up_bwd")
    dhf = _mm(dgate, W[p + "w_gate"], mode="nt", name=p + "gate_dx")
    dhf = _mm(dup, W[p + "w_up"], mode="nt", name=p + "up_dx", res=dhf)
    dW[p + "w_gate"] = _mm(hf, dgate, mode="tn", name=p + "gate_dw")
    dW[p + "w_up"] = _mm(hf, dup, mode="tn", name=p + "up_dw")
    dx_in, dvec[p + "ffn_norm"] = _rmsnorm_bwd(x, vec[p + "ffn_norm"], dhf, name=p + "ffn_norm_bwd", dres=dx)
    return dx_in


def _even_fwd(p, x, tabs, W, vec, comm=None):
    S = x.shape[0]
    h = _rmsnorm(x, vec[p + "mix_norm"], name=p + "mix_norm")
    z = _mm(h, W[p + "w_in"], mode="nn", name=p + "in")
    qa, ka, va, cqn, ckvn, kr = _l0_prep(z, tabs, vec[p + "q_norm"], vec[p + "kv_norm"], name=p + "prep")
    sink = jnp.pad(vec[p + "sinks"], (0, LANES - SWA_HEADS)).reshape(1, LANES)
    oa, lse_a = _band_fwd(_swa_cfg(S), qa, ka, va, name=p + "swa", sink=sink, out_dtype=MXU_DTYPE)
    qb = _mm(cqn, W[p + "w_uq"], mode="nn", name=p + "uq")
    kvb = _mm(ckvn, W[p + "w_ukv"], mode="nn", name=p + "ukv")
    Q, K, V = _mla_prep(qb, kvb, kr, tabs, name=p + "mla_prep")
    if comm is None:
        ob, lse_b = _causal_fwd(_mla_cfg(S), Q, K, V, name=p + "mla", out_dtype=MXU_DTYPE)
    else:
        ob, lse_b, gathered = _causal_fwd(_mla_cfg(S), Q, K, V, name=p + "mla", out_dtype=MXU_DTYPE,
                                          carry=comm.late_weights_exchange())
        W = {**W, **comm.late_weights(gathered)}
    o = jnp.concatenate([oa, ob], axis=1)
    out = _mm(o, W[p + "w_out"], mode="nn", name=p + "out", res=x)
    return out, (x, h, z, qa, ka, va, cqn, ckvn, sink, oa, lse_a, Q, K, V, ob, lse_b, o), W


def _even_bwd(p, dx, saved, tabs, W, vec, dW, dvec, comm=None):
    x, h, z, qa, ka, va, cqn, ckvn, sink, oa, lse_a, Q, K, V, ob, lse_b, o = saved
    S = x.shape[0]
    do = _mm(dx, W[p + "w_out"], mode="nt", name=p + "out_dx", out_dtype=MXU_DTYPE)
    dW[p + "w_out"] = _mm(o, dx, mode="tn", name=p + "out_dw")
    doa, dob = do[:, :SWA_HEADS * HEAD_DIM], do[:, SWA_HEADS * HEAD_DIM:]
    cfg = _swa_cfg(S)
    delta, dsink = _attn_delta(cfg, oa, doa, name=p + "swa_delta", lse=lse_a, sink=sink)
    dvec[p + "sinks"] = dsink
    dqa, dka, dva = _band_bwd(cfg, qa, ka, va, doa, lse_a, delta, name=p + "swa_bwd")
    cfg = _mla_cfg(S)
    delta, _ = _attn_delta(cfg, ob, dob, name=p + "mla_delta")
    if comm is None:
        dQ, dK, dV = _causal_bwd(cfg, Q, K, V, dob, lse_b, delta, name=p + "mla_bwd")
    else:
        dQ, dK, dV, landed = _causal_bwd(cfg, Q, K, V, dob, lse_b, delta, name=p + "mla_bwd",
                                         carry=comm.late_grads_exchange(dW))
        comm.late_grads_landed(landed)
    dqb, dkvb, dkr = _mla_prep_bwd(dQ, dK, dV, tabs, name=p + "mla_prep_bwd")
    dcqn = _mm(dqb, W[p + "w_uq"], mode="nt", name=p + "uq_dx")
    dW[p + "w_uq"] = _mm(cqn, dqb, mode="tn", name=p + "uq_dw")
    dckvn = _mm(dkvb, W[p + "w_ukv"], mode="nt", name=p + "ukv_dx")
    dW[p + "w_ukv"] = _mm(ckvn, dkvb, mode="tn", name=p + "ukv_dw")
    dz, dvec[p + "q_norm"], dvec[p + "kv_norm"] = _l0_prep_bwd(
        z, tabs, vec[p + "q_norm"], vec[p + "kv_norm"], dqa, dka, dva, dcqn, dckvn, dkr, name=p + "prep_bwd")
    dh = _mm(dz, W[p + "w_in"], mode="nt", name=p + "in_dx")
    dW[p + "w_in"] = _mm(h, dz, mode="tn", name=p + "in_dw")
    dx_in, dvec[p + "mix_norm"] = _rmsnorm_bwd(x, vec[p + "mix_norm"], dh, name=p + "mix_norm_bwd", dres=dx)
    return dx_in


def _odd_fwd(p, x, tabs, W, vec):
    S = x.shape[0]
    assert S % (DIL_PATTERNS[-1][1] * BLOCK) == 0, "keys past the end of the sequence are never attended"
    h = _rmsnorm(x, vec[p + "mix_norm"], name=p + "mix_norm")
    qkv = _mm(h, W[p + "w_qkv"], mode="nn", name=p + "qkv")
    qkv_by_d = _l1_prep(qkv, tabs, name=p + "prep")
    outs, lses = {}, {}
    for window, dil in DIL_PATTERNS:
        outs[dil], lses[dil] = _band_fwd(_dil_cfg(S, window, dil), *qkv_by_d[dil], name=p + "dil%d" % dil)
    o, w1, w4, w16 = _merge(outs, lses, name=p + "merge")
    out = _mm(o, W[p + "w_out"], mode="nn", name=p + "out", res=x)
    return out, (x, h, qkv_by_d, lses, dict(zip(DILATIONS, (w1, w4, w16))), o)


def _odd_bwd(p, dx, saved, tabs, W, vec, dW, dvec):
    x, h, qkv_by_d, lses, ws, o = saved
    S = x.shape[0]
    do = _mm(dx, W[p + "w_out"], mode="nt", name=p + "out_dx")
    dW[p + "w_out"] = _mm(o, dx, mode="tn", name=p + "out_dw")
    dos, deltas = _merge_bwd(do, o, ws, name=p + "merge_bwd")
    grads = {}
    for window, dil in DIL_PATTERNS:
        grads[dil] = _band_bwd(_dil_cfg(S, window, dil), *qkv_by_d[dil], dos[dil], lses[dil], deltas[dil],
                               name=p + "dil%d_bwd" % dil)
    dqkv = _l1_prep_bwd(grads, tabs, name=p + "prep_bwd")
    dh = _mm(dqkv, W[p + "w_qkv"], mode="nt", name=p + "qkv_dx")
    dW[p + "w_qkv"] = _mm(h, dqkv, mode="tn", name=p + "qkv_dw")
    dx_in, dvec[p + "mix_norm"] = _rmsnorm_bwd(x, vec[p + "mix_norm"], dh, name=p + "mix_norm_bwd", dres=dx)
    return dx_in


def _local_step(x, mem, positions, target, W, vec, comm=None):
    tabs = _rope_tables(positions)
    x1, s_mix0, W = _even_fwd("l0_", x, tabs, W, vec, comm)
    x2, s_x0 = _cross_fwd("l0_", x1, mem, W, vec)
    x3, s_f0 = _ffn_fwd("l0_", x2, W, vec)
    x4, s_mix1 = _odd_fwd("l1_", x3, tabs, W, vec)
    x5, s_x1 = _cross_fwd("l1_", x4, mem, W, vec)
    x6, s_f1 = _ffn_fwd("l1_", x5, W, vec)
    dW, dvec = {}, {}
    dx, dvec["final_norm"], sq = _loss_head(x6, vec["final_norm"], target, name="loss_head")
    dx = _ffn_bwd("l1_", dx, s_f1, W, vec, dW, dvec)
    dx = _cross_bwd("l1_", dx, s_x1, mem, W, vec, dW, dvec)
    dx = _odd_bwd("l1_", dx, s_mix1, tabs, W, vec, dW, dvec)
    dx = _ffn_bwd("l0_", dx, s_f0, W, vec, dW, dvec)
    dx = _cross_bwd("l0_", dx, s_x0, mem, W, vec, dW, dvec)
    dx = _even_bwd("l0_", dx, s_mix0, tabs, W, vec, dW, dvec, comm)
    return sq, dx, dW, dvec


_LAYER_MATS = {
    0: [("w_in", "col"), ("w_uq", "col"), ("w_ukv", "col"), ("w_out", "row"), ("w_xq", "row"), ("w_xkv", "row"),
        ("w_xo", "col"), ("w_gate", "col"), ("w_up", "col"), ("w_down", "row")],
    1: [("w_qkv", "col"), ("w_out", "row"), ("w_xq", "row"), ("w_xkv", "row"), ("w_xo", "col"), ("w_gate", "col"),
        ("w_up", "col"), ("w_down", "row")],
}
MATS = [("l%d_%s" % (l, n), kind) for l in (0, 1) for n, kind in _LAYER_MATS[l]]
_LAYER_VECS = {0: ["mix_norm", "sinks", "q_norm", "kv_norm", "x_norm", "mem_norm", "ffn_norm"],
               1: ["mix_norm", "x_norm", "mem_norm", "ffn_norm"]}
VECS = ["l%d_%s" % (l, n) for l in (0, 1) for n in _LAYER_VECS[l]] + ["final_norm"]
WEIGHT_ORDER = (["l0_mix_norm", "l0_w_in", "l0_sinks", "l0_q_norm", "l0_w_uq", "l0_kv_norm", "l0_w_ukv", "l0_w_out",
                 "l0_x_norm", "l0_mem_norm", "l0_w_xq", "l0_w_xkv", "l0_w_xo", "l0_ffn_norm", "l0_w_gate", "l0_w_up",
                 "l0_w_down", "l1_mix_norm", "l1_w_qkv", "l1_w_out", "l1_x_norm", "l1_mem_norm", "l1_w_xq",
                 "l1_w_xkv", "l1_w_xo", "l1_ffn_norm", "l1_w_gate", "l1_w_up", "l1_w_down", "final_norm"])
PACK_COLS = 1024
PACK_ROW_TILE = 2 * SUM_ROW_TILE
EXCHANGE_DTYPE = jnp.bfloat16
VEC_ROWS = 16
LOSS_ROW = len(VECS)
N_CHIPS = 4


class _Group:
    def __init__(self, mats, shards):
        self.mats, self.shards = mats, shards
        self.layout, off = {}, 0
        for name, _ in mats:
            n = shards[name].size // PACK_COLS
            assert n * PACK_COLS == shards[name].size
            self.layout[name] = (off, n)
            off += n
        self.used = off
        self.rows = -(-off // PACK_ROW_TILE) * PACK_ROW_TILE

    def pack(self, tensors, dtype):
        parts = [tensors[name].astype(dtype).reshape(-1, PACK_COLS) for name, _ in self.mats]
        return jnp.concatenate(parts + [jnp.zeros((self.rows - self.used, PACK_COLS), dtype)], axis=0)

    def unpack(self, packed):
        return {name: packed[off:off + n].reshape(self.shards[name].shape) for name, (off, n) in self.layout.items()}

    def full_weights(self, gathered):
        W = {}
        for name, kind in self.mats:
            off, n = self.layout[name]
            r, cw = self.shards[name].shape
            blocks = gathered[:, off:off + n].reshape(N_CHIPS, r, cw)
            W[name] = blocks.reshape(N_CHIPS * r, cw) if kind == "row" else (
                jnp.transpose(blocks, (1, 0, 2)).reshape(r, N_CHIPS * cw))
        if "l0_w_in" in W:
            W["l0_w_in"] = jnp.pad(W["l0_w_in"], ((0, 0), (0, Z_END - W["l0_w_in"].shape[1])))
        if "l0_w_uq" in W:
            uq = W["l0_w_uq"].reshape(MLA_Q_RANK, MLA_HEADS, MLA_NOPE + MLA_ROPE)
            uq = jnp.pad(uq, ((0, 0), (0, 0), (0, LANES - MLA_NOPE - MLA_ROPE)))
            W["l0_w_uq"] = uq.reshape(MLA_Q_RANK, MLA_HEADS * LANES)
        return W

    def pack_grads(self, dW):
        parts = []
        for name, kind in self.mats:
            r, cw = self.shards[name].shape
            g = dW[name]
            if name == "l0_w_in":
                g = g[:, :Z_KR + MLA_ROPE]
            if name == "l0_w_uq":
                g = g.reshape(MLA_Q_RANK, MLA_HEADS, LANES)[:, :, :MLA_NOPE + MLA_ROPE].reshape(MLA_Q_RANK, -1)
            if kind == "col":
                g = jnp.transpose(g.reshape(r, N_CHIPS, cw), (1, 0, 2))
            parts.append(g.reshape(N_CHIPS, -1, PACK_COLS).astype(EXCHANGE_DTYPE))
        pad = jnp.zeros((N_CHIPS, self.rows - self.used, PACK_COLS), EXCHANGE_DTYPE)
        return jnp.concatenate(parts + [pad], axis=1)


def _pack_vecs(vecs):
    rows = [jnp.pad(vecs[n].reshape(-1).astype(F32), (0, PACK_COLS - vecs[n].size)) for n in VECS]
    rows += [jnp.zeros((PACK_COLS,), F32)] * (VEC_ROWS - len(rows))
    return jnp.stack(rows)


def _unpack_vecs(packed, like):
    return {n: packed[i, :like[n].size].reshape(like[n].shape) for i, n in enumerate(VECS)}


EARLY_MATS = [m for m in MATS if m[0] in ("l0_w_in", "l0_w_uq", "l0_w_ukv")]
LATE_MATS = [m for m in MATS if m not in EARLY_MATS]


class _StepComm:
    def __init__(self, shards):
        self.early, self.late = _Group(EARLY_MATS, shards), _Group(LATE_MATS, shards)
        self.half_index = lax.axis_index("c").astype(jnp.int32).reshape(1)
        self.late_grads = None

    def early_weights(self):
        src = self.early.pack(self.early.shards, MXU_DTYPE)
        return self.early.full_weights(_run_exchange(_gather_exchange(src), name="gather_early"))

    def late_weights_exchange(self):
        return _gather_exchange(self.late.pack(self.late.shards, MXU_DTYPE))

    def late_weights(self, gathered):
        return self.late.full_weights(gathered)

    def _chip_sum(self, group, dW, tag):
        grads = group.pack_grads(dW)
        theirs = _swap_other_half(grads, name="swap_other_half_" + tag)
        return _sum_cores(grads, theirs, self.half_index, name="sum_cores_" + tag)

    def _finish(self, parts, tag):
        return _join_halves(_sum_chips(parts, self.half_index, name="sum_chips_" + tag), name="join_halves_" + tag)

    def late_grads_exchange(self, dW):
        return _scatter_exchange(self._chip_sum(self.late, dW, "late"))

    def late_grads_landed(self, parts):
        self.late_grads = self._finish(parts, "late")

    def early_grads(self, dW):
        parts = _run_exchange(_scatter_exchange(self._chip_sum(self.early, dW, "early")), name="scatter_early")
        return self._finish(parts, "early")


def _step(a):
    weights = {n: a[n] for n in WEIGHT_ORDER}
    shards = {n: weights[n] for n, _ in MATS}
    vec = {n: weights[n] for n in VECS}
    comm = _StepComm(shards)
    sq, grad_x, dW, dvec = _local_step(a["x"][0], a["mem"][0], a["positions"], a["loss_target"][0],
                                       comm.early_weights(), vec, comm)

    dvec = dict(dvec)
    dvec["l0_sinks"] = dvec["l0_sinks"][0, :SWA_HEADS]
    small = _pack_vecs(dvec)
    small = small.at[LOSS_ROW, 0].set(0.5 / a["x"].shape[-1] * jnp.sum(sq))
    small = _allreduce_small(small, name="reduce_gains")
    loss = small[LOSS_ROW, 0]
    g_s = small.at[LOSS_ROW, 0].set(0.0)
    d_s, m_s, v_s = _adamw(_pack_vecs(vec), g_s, _pack_vecs({n: a["m_" + n] for n in VECS}),
                           _pack_vecs({n: a["v_" + n] for n in VECS}), name="adamw_gains")
    got = [_unpack_vecs(packed, vec) for packed in (g_s, d_s, m_s, v_s)]

    for group, g_w in ((comm.late, comm.late_grads), (comm.early, comm.early_grads(dW))):
        for n, g in group.unpack(g_w).items():
            results = (g,) + tuple(_adamw(shards[n], g, a["m_" + n], a["v_" + n], name="adamw_" + n))
            for kind, value in zip(got, results):
                kind[n] = value

    out = [loss, grad_x[None]]
    for kind in got:
        out += [kind[n] for n in WEIGHT_ORDER]
    return tuple(out)


def kernel(x, mem, positions, l0_mix_norm, l0_w_in, l0_sinks, l0_q_norm, l0_w_uq, l0_kv_norm, l0_w_ukv, l0_w_out, l0_x_norm, l0_mem_norm, l0_w_xq, l0_w_xkv, l0_w_xo, l0_ffn_norm, l0_w_gate, l0_w_up, l0_w_down, l1_mix_norm, l1_w_qkv, l1_w_out, l1_x_norm, l1_mem_norm, l1_w_xq, l1_w_xkv, l1_w_xo, l1_ffn_norm, l1_w_gate, l1_w_up, l1_w_down, final_norm, loss_target, m_l0_mix_norm, m_l0_w_in, m_l0_sinks, m_l0_q_norm, m_l0_w_uq, m_l0_kv_norm, m_l0_w_ukv, m_l0_w_out, m_l0_x_norm, m_l0_mem_norm, m_l0_w_xq, m_l0_w_xkv, m_l0_w_xo, m_l0_ffn_norm, m_l0_w_gate, m_l0_w_up, m_l0_w_down, m_l1_mix_norm, m_l1_w_qkv, m_l1_w_out, m_l1_x_norm, m_l1_mem_norm, m_l1_w_xq, m_l1_w_xkv, m_l1_w_xo, m_l1_ffn_norm, m_l1_w_gate, m_l1_w_up, m_l1_w_down, m_final_norm, v_l0_mix_norm, v_l0_w_in, v_l0_sinks, v_l0_q_norm, v_l0_w_uq, v_l0_kv_norm, v_l0_w_ukv, v_l0_w_out, v_l0_x_norm, v_l0_mem_norm, v_l0_w_xq, v_l0_w_xkv, v_l0_w_xo, v_l0_ffn_norm, v_l0_w_gate, v_l0_w_up, v_l0_w_down, v_l1_mix_norm, v_l1_w_qkv, v_l1_w_out, v_l1_x_norm, v_l1_mem_norm, v_l1_w_xq, v_l1_w_xkv, v_l1_w_xo, v_l1_ffn_norm, v_l1_w_gate, v_l1_w_up, v_l1_w_down, v_final_norm):
    return _step(dict(locals()))
```

```python
import functools

import jax
import jax.numpy as jnp
import numpy as np
from jax import lax
from jax.experimental import pallas as pl
from jax.experimental.pallas import tpu as pltpu

F32 = jnp.float32
MXU_DTYPE = jnp.bfloat16
LANES = 128
VMEM_LIMIT_BYTES = 56 * 1024 * 1024

NORM_EPS = 1e-6
ROPE_THETA = 10000.0
BLOCK = 128
HEAD_DIM = 64
SWA_HEADS, SWA_KV_HEADS, SWA_WINDOW = 8, 2, 128
MLA_HEADS, MLA_Q_RANK, MLA_KV_RANK, MLA_NOPE, MLA_ROPE, MLA_V = 8, 384, 256, 64, 32, 64
DIL_HEADS = 16
DIL_PATTERNS = ((128, 1), (512, 4), (2048, 16))
X_HEADS, X_HEAD_DIM = 4, 128
ADAM_LR, ADAM_B1, ADAM_B2, ADAM_EPS, ADAM_WD, ADAM_STEP = 0.001, 0.9, 0.999, 1e-08, 0.01, 10
MESH = pl.DeviceIdType.MESH
NEG_BIG = -1e30

NN = (((1,), (0,)), ((), ()))
NT = (((1,), (1,)), ((), ()))


def _dot(a, b, dims=NN):
    return lax.dot_general(a.astype(MXU_DTYPE), b.astype(MXU_DTYPE), dims, preferred_element_type=F32)


def _pcall(body, *, name, dims=None, **kw):
    params = pltpu.CompilerParams(dimension_semantics=dims, vmem_limit_bytes=VMEM_LIMIT_BYTES)
    return pl.pallas_call(body, name=name, compiler_params=params, **kw)


def _tile(n, pref):
    t = (min(pref, n) // LANES) * LANES
    while t >= LANES:
        if n % t == 0:
            return t
        t -= LANES
    return n


SUBLANES_PACKED = 16


def _row_tile(n, pref):
    t = (min(pref, n) // SUBLANES_PACKED) * SUBLANES_PACKED
    while t >= SUBLANES_PACKED:
        if n % t == 0:
            return t
        t -= SUBLANES_PACKED
    return n


def _lane(shape):
    return lax.broadcasted_iota(jnp.int32, shape, 1)


def _cols_to_lanes(cols, rows):
    lane = _lane((rows, LANES))
    out = jnp.zeros((rows, LANES), F32)
    for j, col in enumerate(cols):
        out = jnp.where(lane == j, col, out)
    return out


def _mm(a, b, *, mode, name, res=None, out_dtype=F32, tm=1408, tn=1536, tk=1408):
    if mode == "nn":
        (M, K), (K2, N) = a.shape, b.shape
    elif mode == "nt":
        (M, K), (N, K2) = a.shape, b.shape
    else:
        (K, M), (K2, N) = a.shape, b.shape
    assert K == K2, (a.shape, b.shape, mode)
    tm, tn, tk = _tile(M, tm), _tile(N, tn), _tile(K, tk)
    nk = K // tk
    in_place = out_dtype == F32 or nk == 1

    def body(*refs):
        refs = list(refs)
        a_ref, b_ref = refs[:2]
        r_ref = refs[2] if res is not None else None
        o_ref = refs[3 if res is not None else 2]
        acc = o_ref if in_place else refs[-1]
        k = pl.program_id(2)
        if mode == "nn":
            part = _dot(a_ref[...], b_ref[...], NN)
        elif mode == "nt":
            part = _dot(a_ref[...], b_ref[...], NT)
        else:
            part = _dot(a_ref[...].T, b_ref[...], NN)
        if nk == 1:
            o_ref[...] = (part if res is None else part + r_ref[...].astype(F32)).astype(o_ref.dtype)
            return

        @pl.when(k == 0)
        def _():
            acc[...] = part if res is None else part + r_ref[...].astype(F32)

        @pl.when(k > 0)
        def _():
            acc[...] += part

        if not in_place:
            @pl.when(k == nk - 1)
            def _():
                o_ref[...] = acc[...].astype(o_ref.dtype)

    if mode == "nn":
        a_spec = pl.BlockSpec((tm, tk), lambda i, j, k: (i, k))
        b_spec = pl.BlockSpec((tk, tn), lambda i, j, k: (k, j))
    elif mode == "nt":
        a_spec = pl.BlockSpec((tm, tk), lambda i, j, k: (i, k))
        b_spec = pl.BlockSpec((tn, tk), lambda i, j, k: (j, k))
    else:
        a_spec = pl.BlockSpec((tk, tm), lambda i, j, k: (k, i))
        b_spec = pl.BlockSpec((tk, tn), lambda i, j, k: (k, j))
    o_spec = pl.BlockSpec((tm, tn), lambda i, j, k: (i, j))
    in_specs = [a_spec, b_spec] + ([] if res is None else [o_spec])
    args = (a, b) + (() if res is None else (res,))
    return _pcall(
        body, name=name, dims=("parallel", "parallel", "arbitrary"),
        grid=(M // tm, N // tn, nk), in_specs=in_specs, out_specs=o_spec,
        out_shape=jax.ShapeDtypeStruct((M, N), out_dtype),
        scratch_shapes=[] if in_place else [pltpu.VMEM((tm, tn), F32)],
    )(*args)


def _rms_parts(xf):
    r = lax.rsqrt(jnp.mean(xf * xf, axis=-1, keepdims=True) + NORM_EPS)
    return xf * r, r


def _rms_bwd_rows(xf, g, dy):
    xhat, r = _rms_parts(xf)
    dxhat = dy * g
    dx = r * (dxhat - xhat * jnp.mean(dxhat * xhat, axis=-1, keepdims=True))
    return dx, dy * xhat


def _rmsnorm(x, g, *, name, out_dtype=MXU_DTYPE, tm=512):
    M, D = x.shape
    tm = _tile(M, tm)

    def body(x_ref, g_ref, o_ref):
        xhat, _ = _rms_parts(x_ref[...].astype(F32))
        o_ref[...] = (xhat * g_ref[...]).astype(o_ref.dtype)

    return _pcall(
        body, name=name, dims=("parallel",), grid=(M // tm,),
        in_specs=[pl.BlockSpec((tm, D), lambda i: (i, 0)), pl.BlockSpec((1, D), lambda i: (0, 0))],
        out_specs=pl.BlockSpec((tm, D), lambda i: (i, 0)),
        out_shape=jax.ShapeDtypeStruct((M, D), out_dtype),
    )(x, g.reshape(1, D))


def _rmsnorm_bwd(x, g, dy, *, name, dres=None, tm=512):
    M, D = x.shape
    tm = _tile(M, tm)

    def body(*refs):
        if dres is None:
            x_ref, g_ref, dy_ref, dx_ref, dg_ref = refs
        else:
            x_ref, g_ref, dy_ref, dr_ref, dx_ref, dg_ref = refs
        dx, dgp = _rms_bwd_rows(x_ref[...].astype(F32), g_ref[...], dy_ref[...].astype(F32))
        if dres is not None:
            dx = dx + dr_ref[...]
        dx_ref[...] = dx

        @pl.when(pl.program_id(0) == 0)
        def _():
            dg_ref[...] = jnp.zeros_like(dg_ref)

        dg_ref[...] += jnp.sum(dgp, axis=0, keepdims=True)

    row = pl.BlockSpec((tm, D), lambda i: (i, 0))
    vec = pl.BlockSpec((1, D), lambda i: (0, 0))
    in_specs = [row, vec, row] + ([] if dres is None else [row])
    args = (x, g.reshape(1, D), dy) + (() if dres is None else (dres,))
    return _pcall(
        body, name=name, dims=("arbitrary",), grid=(M // tm,), in_specs=in_specs, out_specs=[row, vec],
        out_shape=[jax.ShapeDtypeStruct((M, D), F32), jax.ShapeDtypeStruct((1, D), F32)],
    )(*args)


def _rope_chunk(t, c, s, half):
    lane = _lane(t.shape)
    swapped = jnp.where((lane % (2 * half)) < half, pltpu.roll(t, LANES - half, 1), pltpu.roll(t, half, 1))
    return t * c + swapped * s


def _rope_tables(positions):
    pos = positions.reshape(-1).astype(F32)[:, None]
    S = pos.shape[0]

    def cs(dh):
        inv_freq = ROPE_THETA ** (-jnp.arange(0, dh, 2, dtype=F32) / dh)
        ang = pos * inv_freq
        return jnp.cos(ang), jnp.sin(ang)

    c64, s64 = cs(HEAD_DIM)
    c32, s32 = cs(MLA_ROPE)
    z32, z64, z96 = (jnp.zeros((S, n), F32) for n in (32, 64, 96))
    return dict(
        c64=jnp.concatenate([c64, c64, c64, c64], 1), s64=jnp.concatenate([-s64, s64, -s64, s64], 1),
        ck=jnp.concatenate([c32, c32, z96], 1), sk=jnp.concatenate([-s32, s32, z96], 1),
        cm=jnp.concatenate([jnp.ones((S, 64), F32), c32, c32, z32], 1),
        sm=jnp.concatenate([z64, -s32, s32, z32], 1),
    )


def _attn_steps(mode, n_other, t_self, t_other):
    if mode == "band":
        assert t_self == t_other
        return 2
    return n_other


def _kv_block(mode, qi, kj):
    if mode == "band":
        return jnp.maximum(qi - 1 + kj, 0), (qi + kj) >= 1
    if mode == "causal":
        return jnp.minimum(kj, qi), kj <= qi
    return kj, None


def _q_block(mode, ki, qj, nq):
    if mode == "band":
        return jnp.minimum(ki + qj, nq - 1), (ki + qj) <= nq - 1
    if mode == "causal":
        return jnp.maximum(qj, ki), qj >= ki
    return qj, None


def _mask(mode, max_dist, qpos, kpos):
    d = qpos - kpos
    if mode == "band":
        return (d >= 0) & (d <= max_dist)
    if mode == "causal":
        return d >= 0
    return None


def _when(cond, fn):
    if cond is None:
        fn()
    else:
        pl.when(cond)(fn)


class _Attn:
    def __init__(self, *, T, Tk, G, nh, rep, dqk, dv, tq, tk, mode, scale, qcol, kcol, vcol, ocol, o_width,
                 max_dist=0):
        self.__dict__.update(locals())
        self.nkv = nh // rep
        assert T % tq == 0 and Tk % tk == 0 and nh <= LANES


def _attn_fwd(cfg, q, k, v, *, name, sink=None, out_dtype=F32):
    c = cfg
    nq, nk = c.T // c.tq, c.Tk // c.tk
    steps = _attn_steps(c.mode, nk, c.tq, c.tk)

    def body(*refs):
        if sink is None:
            q_ref, k_ref, v_ref, o_ref, lse_ref, m_scr, l_scr, acc = refs
        else:
            q_ref, k_ref, v_ref, sink_ref, o_ref, lse_ref, m_scr, l_scr, acc = refs
        qi, kj = pl.program_id(1), pl.program_id(2)
        kb, valid = _kv_block(c.mode, qi, kj)

        @pl.when(kj == 0)
        def _():
            if sink is None:
                m_scr[...] = jnp.full_like(m_scr, NEG_BIG)
                l_scr[...] = jnp.zeros_like(l_scr)
            else:
                m_scr[...] = jnp.broadcast_to(sink_ref[...], m_scr.shape)
                l_scr[...] = jnp.ones_like(l_scr)
            acc[...] = jnp.zeros_like(acc)

        def step():
            qpos = qi * c.tq + lax.broadcasted_iota(jnp.int32, (c.tq, c.tk), 0)
            kpos = kb * c.tk + lax.broadcasted_iota(jnp.int32, (c.tq, c.tk), 1)
            mask = _mask(c.mode, c.max_dist, qpos, kpos)
            for j in range(c.nh):
                g = j // c.rep
                s = _dot(q_ref[:, j * c.dqk:(j + 1) * c.dqk], k_ref[:, g * c.dqk:(g + 1) * c.dqk], NT) * c.scale
                if mask is not None:
                    s = jnp.where(mask, s, -jnp.inf)
                m_prev = m_scr[:, j:j + 1]
                m_new = jnp.maximum(m_prev, jnp.max(s, axis=1, keepdims=True))
                alpha = jnp.exp(m_prev - m_new)
                p = jnp.exp(s - m_new)
                l_scr[:, j:j + 1] = alpha * l_scr[:, j:j + 1] + jnp.sum(p, axis=1, keepdims=True)
                acc[:, j * c.dv:(j + 1) * c.dv] = (
                    alpha * acc[:, j * c.dv:(j + 1) * c.dv] + _dot(p, v_ref[:, g * c.dv:(g + 1) * c.dv], NN))
                m_scr[:, j:j + 1] = m_new

        _when(valid, step)

        @pl.when(kj == steps - 1)
        def _():
            for j in range(c.nh):
                o_ref[:, j * c.dv:(j + 1) * c.dv] = (
                    acc[:, j * c.dv:(j + 1) * c.dv] / l_scr[:, j:j + 1]).astype(o_ref.dtype)
            lane = _lane((c.tq, LANES))
            lse_ref[...] = jnp.where(lane < c.nh, m_scr[...] + jnp.log(jnp.maximum(l_scr[...], 1e-37)), 0.0)

    in_specs = [
        pl.BlockSpec((c.tq, c.nh * c.dqk), lambda g, i, j: (i, c.qcol(g))),
        pl.BlockSpec((c.tk, c.nkv * c.dqk), lambda g, i, j: (_kv_block(c.mode, i, j)[0], c.kcol(g))),
        pl.BlockSpec((c.tk, c.nkv * c.dv), lambda g, i, j: (_kv_block(c.mode, i, j)[0], c.vcol(g))),
    ]
    args = [q, k, v]
    if sink is not None:
        in_specs.append(pl.BlockSpec((1, LANES), lambda g, i, j: (0, 0)))
        args.append(sink)
    return _pcall(
        body, name=name, dims=("parallel", "parallel", "arbitrary"), grid=(c.G, nq, steps),
        in_specs=in_specs,
        out_specs=[pl.BlockSpec((c.tq, c.nh * c.dv), lambda g, i, j: (i, c.ocol(g))),
                   pl.BlockSpec((c.tq, LANES), lambda g, i, j: (i, g))],
        out_shape=[jax.ShapeDtypeStruct((c.T, c.o_width), out_dtype),
                   jax.ShapeDtypeStruct((c.T, LANES * c.G), F32)],
        scratch_shapes=[pltpu.VMEM((c.tq, LANES), F32), pltpu.VMEM((c.tq, LANES), F32),
                        pltpu.VMEM((c.tq, c.nh * c.dv), F32)],
    )(*args)


def _attn_delta(cfg, o, do, *, name, w=None, lse=None, sink=None, tm=512):
    c = cfg
    tm = _tile(c.T, tm)
    width = c.nh * c.dv

    def body(*refs):
        refs = list(refs)
        o_ref, do_ref = refs[:2]
        rest = refs[2:]
        w_ref = rest.pop(0) if w is not None else None
        lse_ref, sink_ref = (rest.pop(0), rest.pop(0)) if sink is not None else (None, None)
        d_ref = rest.pop(0)
        prod = o_ref[...].astype(F32) * do_ref[...].astype(F32)
        cols = [jnp.sum(prod[:, j * c.dv:(j + 1) * c.dv], axis=1, keepdims=True) for j in range(c.nh)]
        delta = _cols_to_lanes(cols, tm)
        if w is not None:
            delta = delta * w_ref[...]
        d_ref[...] = delta
        if sink is not None:
            ds_ref = rest.pop(0)

            @pl.when(pl.program_id(1) == 0)
            def _():
                ds_ref[...] = jnp.zeros_like(ds_ref)

            lane = _lane((tm, LANES))
            ps = jnp.where(lane < c.nh, jnp.exp(sink_ref[...] - lse_ref[...]), 0.0)
            ds_ref[...] -= jnp.sum(ps * delta, axis=0, keepdims=True)

    stat = pl.BlockSpec((tm, LANES), lambda g, i: (i, g))
    in_specs = [pl.BlockSpec((tm, width), lambda g, i: (i, c.ocol(g)))] * 2
    args = [o, do]
    out_specs, out_shape = [stat], [jax.ShapeDtypeStruct((c.T, LANES * c.G), F32)]
    if w is not None:
        in_specs.append(stat)
        args.append(w)
    if sink is not None:
        assert c.G == 1
        in_specs += [stat, pl.BlockSpec((1, LANES), lambda g, i: (0, 0))]
        args += [lse, sink]
        out_specs.append(pl.BlockSpec((1, LANES), lambda g, i: (0, 0)))
        out_shape.append(jax.ShapeDtypeStruct((1, LANES), F32))
    out = _pcall(
        body, name=name, dims=("arbitrary", "arbitrary"), grid=(c.G, c.T // tm),
        in_specs=in_specs, out_specs=out_specs, out_shape=out_shape,
    )(*args)
    return out if sink is not None else (out[0], None)


def _attn_dq(cfg, q, k, v, do, lse, delta, *, name, init=None, out_dtype=F32):
    c = cfg
    nq, nk = c.T // c.tq, c.Tk // c.tk
    steps = _attn_steps(c.mode, nk, c.tq, c.tk)
    qw = c.nh * c.dqk

    def body(*refs):
        if init is None:
            q_ref, k_ref, v_ref, do_ref, lse_ref, d_ref, dq_ref, acc = refs
        else:
            q_ref, k_ref, v_ref, do_ref, lse_ref, d_ref, init_ref, dq_ref, acc = refs
        qi, kj = pl.program_id(1), pl.program_id(2)
        kb, valid = _kv_block(c.mode, qi, kj)

        @pl.when(kj == 0)
        def _():
            acc[...] = jnp.zeros_like(acc) if init is None else init_ref[...].astype(F32)

        def step():
            qpos = qi * c.tq + lax.broadcasted_iota(jnp.int32, (c.tq, c.tk), 0)
            kpos = kb * c.tk + lax.broadcasted_iota(jnp.int32, (c.tq, c.tk), 1)
            mask = _mask(c.mode, c.max_dist, qpos, kpos)
            for j in range(c.nh):
                g = j // c.rep
                kh = k_ref[:, g * c.dqk:(g + 1) * c.dqk]
                s = _dot(q_ref[:, j * c.dqk:(j + 1) * c.dqk], kh, NT) * c.scale
                if mask is not None:
                    s = jnp.where(mask, s, -jnp.inf)
                p = jnp.exp(s - lse_ref[:, j:j + 1])
                dp = _dot(do_ref[:, j * c.dv:(j + 1) * c.dv], v_ref[:, g * c.dv:(g + 1) * c.dv], NT)
                ds = p * (dp - d_ref[:, j:j + 1]) * c.scale
                acc[:, j * c.dqk:(j + 1) * c.dqk] += _dot(ds, kh, NN)

        _when(valid, step)

        @pl.when(kj == steps - 1)
        def _():
            dq_ref[...] = acc[...].astype(dq_ref.dtype)

    kvb = lambda i, j: _kv_block(c.mode, i, j)[0]
    qspec = pl.BlockSpec((c.tq, qw), lambda g, i, j: (i, c.qcol(g)))
    stat = pl.BlockSpec((c.tq, LANES), lambda g, i, j: (i, g))
    in_specs = [
        qspec,
        pl.BlockSpec((c.tk, c.nkv * c.dqk), lambda g, i, j: (kvb(i, j), c.kcol(g))),
        pl.BlockSpec((c.tk, c.nkv * c.dv), lambda g, i, j: (kvb(i, j), c.vcol(g))),
        pl.BlockSpec((c.tq, c.nh * c.dv), lambda g, i, j: (i, c.ocol(g))),
        stat, stat,
    ]
    args = [q, k, v, do, lse, delta]
    dq_spec = pl.BlockSpec((c.tq, qw), lambda g, i, j: (i, g))
    if init is not None:
        in_specs.append(dq_spec)
        args.append(init)
    return _pcall(
        body, name=name, dims=("parallel", "parallel", "arbitrary"), grid=(c.G, nq, steps),
        in_specs=in_specs, out_specs=dq_spec,
        out_shape=jax.ShapeDtypeStruct((c.T, c.G * qw), out_dtype),
        scratch_shapes=[pltpu.VMEM((c.tq, qw), F32)],
    )(*args)


def _attn_dkv(cfg, q, k, v, do, lse, delta, *, name, init=None, out_dtype=F32):
    c = cfg
    nq, nk = c.T // c.tq, c.Tk // c.tk
    steps = _attn_steps(c.mode, nq, c.tk, c.tq)
    kw, vw = c.nkv * c.dqk, c.nkv * c.dv

    def body(*refs):
        if init is None:
            q_ref, k_ref, v_ref, do_ref, lse_ref, d_ref, dk_ref, dv_ref, dk_acc, dv_acc = refs
        else:
            q_ref, k_ref, v_ref, do_ref, lse_ref, d_ref, ik_ref, iv_ref, dk_ref, dv_ref, dk_acc, dv_acc = refs
        ki, qj = pl.program_id(1), pl.program_id(2)
        qb, valid = _q_block(c.mode, ki, qj, nq)

        @pl.when(qj == 0)
        def _():
            dk_acc[...] = jnp.zeros_like(dk_acc) if init is None else ik_ref[...].astype(F32)
            dv_acc[...] = jnp.zeros_like(dv_acc) if init is None else iv_ref[...].astype(F32)

        def step():
            kpos = ki * c.tk + lax.broadcasted_iota(jnp.int32, (c.tk, c.tq), 0)
            qpos = qb * c.tq + lax.broadcasted_iota(jnp.int32, (c.tk, c.tq), 1)
            mask = _mask(c.mode, c.max_dist, qpos, kpos)
            lse_t = lse_ref[...].T
            d_t = d_ref[...].T
            for j in range(c.nh):
                g = j // c.rep
                qh = q_ref[:, j * c.dqk:(j + 1) * c.dqk]
                doh = do_ref[:, j * c.dv:(j + 1) * c.dv]
                s_t = _dot(k_ref[:, g * c.dqk:(g + 1) * c.dqk], qh, NT) * c.scale
                if mask is not None:
                    s_t = jnp.where(mask, s_t, -jnp.inf)
                p_t = jnp.exp(s_t - lse_t[j:j + 1, :])
                dv_acc[:, g * c.dv:(g + 1) * c.dv] += _dot(p_t, doh, NN)
                dp_t = _dot(v_ref[:, g * c.dv:(g + 1) * c.dv], doh, NT)
                ds_t = p_t * (dp_t - d_t[j:j + 1, :]) * c.scale
                dk_acc[:, g * c.dqk:(g + 1) * c.dqk] += _dot(ds_t, qh, NN)

        _when(valid, step)

        @pl.when(qj == steps - 1)
        def _():
            dk_ref[...] = dk_acc[...].astype(dk_ref.dtype)
            dv_ref[...] = dv_acc[...].astype(dv_ref.dtype)

    qbi = lambda i, j: _q_block(c.mode, i, j, nq)[0]
    stat = pl.BlockSpec((c.tq, LANES), lambda g, i, j: (qbi(i, j), g))
    in_specs = [
        pl.BlockSpec((c.tq, c.nh * c.dqk), lambda g, i, j: (qbi(i, j), c.qcol(g))),
        pl.BlockSpec((c.tk, kw), lambda g, i, j: (i, c.kcol(g))),
        pl.BlockSpec((c.tk, vw), lambda g, i, j: (i, c.vcol(g))),
        pl.BlockSpec((c.tq, c.nh * c.dv), lambda g, i, j: (qbi(i, j), c.ocol(g))),
        stat, stat,
    ]
    args = [q, k, v, do, lse, delta]
    dk_spec = pl.BlockSpec((c.tk, kw), lambda g, i, j: (i, g))
    dv_spec = pl.BlockSpec((c.tk, vw), lambda g, i, j: (i, g))
    if init is not None:
        in_specs += [dk_spec, dv_spec]
        args += list(init)
    return _pcall(
        body, name=name, dims=("parallel", "parallel", "arbitrary"), grid=(c.G, nk, steps),
        in_specs=in_specs, out_specs=[dk_spec, dv_spec],
        out_shape=[jax.ShapeDtypeStruct((c.Tk, c.G * kw), out_dtype),
                   jax.ShapeDtypeStruct((c.Tk, c.G * vw), out_dtype)],
        scratch_shapes=[pltpu.VMEM((c.tk, kw), F32), pltpu.VMEM((c.tk, vw), F32)],
    )(*args)


TN = (((0,), (0,)), ((), ()))


def _band_mask(c, i):
    key = lax.broadcasted_iota(jnp.int32, (2 * BLOCK, BLOCK), 0)
    qry = lax.broadcasted_iota(jnp.int32, (2 * BLOCK, BLOCK), 1)
    d = BLOCK + qry - key
    return (d >= 0) & (d <= c.max_dist) & ((key >= BLOCK) | (i > 0))


def _head_pairs(c):
    return c.rep == 1 and c.dqk == c.dv == LANES // 2 and c.nh % 2 == 0


def _block_diagonal(pair):
    lane = _lane(pair.shape)
    zero = jnp.zeros_like(pair)
    return jnp.concatenate([jnp.where(lane < LANES // 2, pair, zero), jnp.where(lane >= LANES // 2, pair, zero)], axis=0)


def _own_blocks(t):
    n = t.shape[1] // 2
    rows = lax.broadcasted_iota(jnp.int32, (LANES, n), 0)
    return jnp.where(rows < LANES // 2, t[:, :n], t[:, n:])


def _rows_to_stats(rows, n):
    return jnp.concatenate(rows + [jnp.zeros((LANES - len(rows), n), F32)], axis=0).T


def _band_fwd(cfg, q, k, v, *, name, sink=None, out_dtype=F32):
    c = cfg
    assert c.mode == "band" and c.tq == c.tk == BLOCK and c.T == c.Tk
    nq = c.T // BLOCK

    def body(*refs):
        if sink is None:
            q_ref, kp_ref, kc_ref, vp_ref, vc_ref, o_ref, lse_ref = refs
        else:
            q_ref, kp_ref, kc_ref, vp_ref, vc_ref, sink_ref, o_ref, lse_ref = refs
        mask = _band_mask(c, pl.program_id(1))
        k2 = jnp.concatenate([kp_ref[...], kc_ref[...]], axis=0)
        v2 = jnp.concatenate([vp_ref[...], vc_ref[...]], axis=0)
        lses = []
        if _head_pairs(c):
            mask2 = jnp.concatenate([mask, mask], axis=1)
            pair_lanes = [slice(pc * LANES, (pc + 1) * LANES) for pc in range(c.nh // 2)]
            score = lambda sl: _dot(k2[:, sl], _block_diagonal(q_ref[:, sl]), NT)
            ahead, behind = score(pair_lanes[0]), None

            def finish(entry):
                sl, o_t, l = entry
                o_ref[:, sl] = _own_blocks(o_t / l).T.astype(o_ref.dtype)

            for pc, sl in enumerate(pair_lanes):
                s = ahead * c.scale
                if pc + 1 < len(pair_lanes):
                    ahead = score(pair_lanes[pc + 1])
                s = jnp.where(mask2, s, -jnp.inf)
                m = jnp.max(s, axis=0, keepdims=True)
                p = jnp.exp(s - m)
                l = jnp.sum(p, axis=0, keepdims=True)
                if behind is not None:
                    finish(behind)
                behind = (sl, _dot(v2[:, sl], p, TN), l)
                lse = m + jnp.log(l)
                lses += [lse[:, :BLOCK], lse[:, BLOCK:]]
            finish(behind)
        heads = [] if _head_pairs(c) else list(range(c.nh))
        score_of = lambda j: _dot(k2[:, (j // c.rep) * c.dqk:(j // c.rep + 1) * c.dqk],
                                  q_ref[:, j * c.dqk:(j + 1) * c.dqk], NT)
        ahead = score_of(0) if heads else None
        for j in heads:
            g = j // c.rep
            s = ahead * c.scale
            if j + 1 < c.nh:
                ahead = score_of(j + 1)
            s = jnp.where(mask, s, -jnp.inf)
            m = jnp.max(s, axis=0, keepdims=True)
            if sink is not None:
                sk = sink_ref[:, j:j + 1]
                m = jnp.maximum(m, sk)
            p = jnp.exp(s - m)
            l = jnp.sum(p, axis=0, keepdims=True)
            if sink is not None:
                l = l + jnp.exp(sk - m)
            o_t = _dot(v2[:, g * c.dv:(g + 1) * c.dv], p, TN)
            o_ref[:, j * c.dv:(j + 1) * c.dv] = (o_t / l).T.astype(o_ref.dtype)
            lses.append(m + jnp.log(l))
        lse_ref[...] = _rows_to_stats(lses, BLOCK)

    prev = lambda i: jnp.maximum(i - 1, 0)
    kw, vw = c.nkv * c.dqk, c.nkv * c.dv
    in_specs = [
        pl.BlockSpec((BLOCK, c.nh * c.dqk), lambda g, i: (i, c.qcol(g))),
        pl.BlockSpec((BLOCK, kw), lambda g, i: (prev(i), c.kcol(g))),
        pl.BlockSpec((BLOCK, kw), lambda g, i: (i, c.kcol(g))),
        pl.BlockSpec((BLOCK, vw), lambda g, i: (prev(i), c.vcol(g))),
        pl.BlockSpec((BLOCK, vw), lambda g, i: (i, c.vcol(g))),
    ]
    args = [q, k, k, v, v]
    if sink is not None:
        in_specs.append(pl.BlockSpec((1, LANES), lambda g, i: (0, 0)))
        args.append(sink)
    return _pcall(
        body, name=name, dims=("parallel", "parallel"), grid=(c.G, nq), in_specs=in_specs,
        out_specs=[pl.BlockSpec((BLOCK, c.nh * c.dv), lambda g, i: (i, c.ocol(g))),
                   pl.BlockSpec((BLOCK, LANES), lambda g, i: (i, g))],
        out_shape=[jax.ShapeDtypeStruct((c.T, c.o_width), out_dtype),
                   jax.ShapeDtypeStruct((c.T, LANES * c.G), F32)],
    )(*args)


def _band_bwd(cfg, q, k, v, do, lse, delta, *, name):
    c = cfg
    assert c.mode == "band" and c.tq == c.tk == BLOCK and c.T == c.Tk
    nq = c.T // BLOCK
    qw, kw, vw = c.nh * c.dqk, c.nkv * c.dqk, c.nkv * c.dv

    def body(q_ref, kp_ref, kc_ref, vp_ref, vc_ref, do_ref, lse_ref, d_ref, dq_ref, dk_ref, dv_ref, dk_c, dv_c):
        n = pl.program_id(1)

        @pl.when(n == 0)
        def _():
            dk_c[...] = jnp.zeros_like(dk_c)
            dv_c[...] = jnp.zeros_like(dv_c)

        @pl.when(n < nq)
        def _():
            mask = _band_mask(c, n)
            k2 = jnp.concatenate([kp_ref[...], kc_ref[...]], axis=0)
            v2 = jnp.concatenate([vp_ref[...], vc_ref[...]], axis=0)
            lse_t, d_t = lse_ref[...].T, d_ref[...].T
            if _head_pairs(c):
                mask2 = jnp.concatenate([mask, mask], axis=1)
                pair_lanes = [slice(pc * LANES, (pc + 1) * LANES) for pc in range(c.nh // 2)]

                def first(sl):
                    q_bd, do_bd = _block_diagonal(q_ref[:, sl]), _block_diagonal(do_ref[:, sl])
                    return q_bd, do_bd, k2[:, sl], _dot(k2[:, sl], q_bd, NT), _dot(v2[:, sl], do_bd, NT)

                def finish(entry):
                    sl, dq_t, dv_pair, dk_pair = entry
                    dq_ref[:, sl] = _own_blocks(dq_t).T
                    dk_ref[:, sl] = dk_c[:, sl] + dk_pair[:BLOCK]
                    dv_ref[:, sl] = dv_c[:, sl] + dv_pair[:BLOCK]
                    dk_c[:, sl] = dk_pair[BLOCK:]
                    dv_c[:, sl] = dv_pair[BLOCK:]

                ahead, behind = first(pair_lanes[0]), None
                for pc, sl in enumerate(pair_lanes):
                    q_bd, do_bd, kp, s, dp = ahead
                    if pc + 1 < len(pair_lanes):
                        ahead = first(pair_lanes[pc + 1])
                    both = lambda t: jnp.concatenate([t[2 * pc:2 * pc + 1, :], t[2 * pc + 1:2 * pc + 2, :]], axis=1)
                    p = jnp.exp(jnp.where(mask2, s * c.scale, -jnp.inf) - both(lse_t))
                    ds = p * (dp - both(d_t)) * c.scale
                    entry = (sl, _dot(kp, ds, TN), _dot(p, do_bd, NN), _dot(ds, q_bd, NN))
                    if behind is not None:
                        finish(behind)
                    behind = entry
                finish(behind)
                return
            dk2, dv2 = [None] * c.nkv, [None] * c.nkv
            for j in range(c.nh):
                g = j // c.rep
                qs, os_ = slice(j * c.dqk, (j + 1) * c.dqk), slice(j * c.dv, (j + 1) * c.dv)
                qh, doh = q_ref[:, qs], do_ref[:, os_]
                kh, vh = k2[:, g * c.dqk:(g + 1) * c.dqk], v2[:, g * c.dv:(g + 1) * c.dv]
                s = jnp.where(mask, _dot(kh, qh, NT) * c.scale, -jnp.inf)
                p = jnp.exp(s - lse_t[j:j + 1, :])
                ds = p * (_dot(vh, doh, NT) - d_t[j:j + 1, :]) * c.scale
                dq_ref[:, qs] = _dot(kh, ds, TN).T
                dvh, dkh = _dot(p, doh, NN), _dot(ds, qh, NN)
                dv2[g] = dvh if dv2[g] is None else dv2[g] + dvh
                dk2[g] = dkh if dk2[g] is None else dk2[g] + dkh
            for g in range(c.nkv):
                ks, vs = slice(g * c.dqk, (g + 1) * c.dqk), slice(g * c.dv, (g + 1) * c.dv)
                dk_ref[:, ks] = dk_c[:, ks] + dk2[g][:BLOCK]
                dv_ref[:, vs] = dv_c[:, vs] + dv2[g][:BLOCK]
                dk_c[:, ks] = dk2[g][BLOCK:]
                dv_c[:, vs] = dv2[g][BLOCK:]

        @pl.when(n == nq)
        def _():
            dk_ref[...] = dk_c[...]
            dv_ref[...] = dv_c[...]

    cur = lambda n: jnp.minimum(n, nq - 1)
    prev = lambda n: jnp.maximum(cur(n) - 1, 0)
    out_blk = lambda n: jnp.maximum(n - 1, 0)
    stat = pl.BlockSpec((BLOCK, LANES), lambda g, n: (cur(n), g))
    dq_spec = pl.BlockSpec((BLOCK, qw), lambda g, n: (cur(n), g))
    dk_spec = pl.BlockSpec((BLOCK, kw), lambda g, n: (out_blk(n), g))
    dv_spec = pl.BlockSpec((BLOCK, vw), lambda g, n: (out_blk(n), g))
    in_specs = [
        pl.BlockSpec((BLOCK, qw), lambda g, n: (cur(n), c.qcol(g))),
        pl.BlockSpec((BLOCK, kw), lambda g, n: (prev(n), c.kcol(g))),
        pl.BlockSpec((BLOCK, kw), lambda g, n: (cur(n), c.kcol(g))),
        pl.BlockSpec((BLOCK, vw), lambda g, n: (prev(n), c.vcol(g))),
        pl.BlockSpec((BLOCK, vw), lambda g, n: (cur(n), c.vcol(g))),
        pl.BlockSpec((BLOCK, c.nh * c.dv), lambda g, n: (cur(n), c.ocol(g))),
        stat, stat,
    ]
    return _pcall(
        body, name=name, dims=("parallel", "arbitrary"), grid=(c.G, nq + 1), in_specs=in_specs,
        out_specs=[dq_spec, dk_spec, dv_spec],
        out_shape=[_sds((c.T, c.G * qw)), _sds((c.T, c.G * kw)), _sds((c.T, c.G * vw))],
        scratch_shapes=[pltpu.VMEM((BLOCK, kw), F32), pltpu.VMEM((BLOCK, vw), F32)],
    )(q, k, k, v, v, do, lse, delta)


def _causal_pairs(n, kv_major):
    pairs =[(i, j) for j in range(n) for i in range(j, n)] if kv_major else [(i, j) for i in range(n) for j in range(i + 1)]
    return jnp.asarray(np.array([p[0] for p in pairs], np.int32)), jnp.asarray(np.array([p[1] for p in pairs], np.int32))


def _causal_mask(t):
    return lax.broadcasted_iota(jnp.int32, (t, t), 0) >= lax.broadcasted_iota(jnp.int32, (t, t), 1)


def _carrying(body, n_in, n_out, n_scratch, grid, carry):
    if carry is None:
        return body
    G, P = grid

    def wrapped(*refs):
        refs = list(refs)
        prefetch, refs = refs[:2], refs[2:]
        ins, src = refs[:n_in], refs[n_in]
        outs, out = refs[n_in + 1:n_in + 1 + n_out], refs[n_in + 1 + n_out]
        scratch, sems = refs[n_in + 2 + n_out:n_in + 2 + n_out + n_scratch], refs[n_in + 2 + n_out + n_scratch:]
        step = pl.program_id(0) * P + pl.program_id(1)
        carry.run([src, out] + sems, step, G * P, at_end=False)
        body(*prefetch, *ins, *outs, *scratch)
        carry.run([src, out] + sems, step, G * P, at_end=True)

    return wrapped


def _carry_specs(carry):
    if carry is None:
        return [], [], [], [], []
    any_space = pl.BlockSpec(memory_space=pl.ANY)
    return [any_space], [any_space], [carry.out_shape], list(carry.sems), [carry.src]


def _causal_fwd(cfg, q, k, v, *, name, out_dtype=F32, carry=None):
    c = cfg
    assert c.mode == "causal" and c.tq == c.tk and c.T == c.Tk
    t, n = c.tq, c.T // c.tq
    qi_tab, kj_tab = _causal_pairs(n, kv_major=False)
    n_pairs = int(qi_tab.shape[0])

    def body(qi_ref, kj_ref, q_ref, k_ref, v_ref, o_ref, lse_ref, m_scr, l_scr, acc):
        pair = pl.program_id(1)
        qi, kj = qi_ref[pair], kj_ref[pair]

        @pl.when(kj == 0)
        def _():
            m_scr[...] = jnp.full_like(m_scr, NEG_BIG)
            l_scr[...] = jnp.zeros_like(l_scr)
            acc[...] = jnp.zeros_like(acc)

        def step(diagonal):
            mask = None
            if diagonal:
                mask = lax.broadcasted_iota(jnp.int32, (t, t), 1) >= lax.broadcasted_iota(jnp.int32, (t, t), 0)
            scores = [_dot(k_ref[:, (j // c.rep) * c.dqk:(j // c.rep + 1) * c.dqk],
                           q_ref[:, j * c.dqk:(j + 1) * c.dqk], NT) for j in range(c.nh)]
            for j in range(c.nh):
                g = j // c.rep
                s = scores[j] * c.scale
                if diagonal:
                    s = jnp.where(mask, s, -jnp.inf)
                m_prev = m_scr[j]
                m_new = jnp.maximum(m_prev, jnp.max(s, axis=0, keepdims=True))
                alpha = jnp.exp(m_prev - m_new)
                p = jnp.exp(s - m_new)
                l_scr[j] = alpha * l_scr[j] + jnp.sum(p, axis=0, keepdims=True)
                acc[j] = alpha * acc[j] + _dot(v_ref[:, g * c.dv:(g + 1) * c.dv], p, TN)
                m_scr[j] = m_new

        pl.when(kj == qi)(lambda: step(True))
        pl.when(kj != qi)(lambda: step(False))

        @pl.when(kj == qi)
        def _():
            rows = []
            for j in range(c.nh):
                o_ref[:, j * c.dv:(j + 1) * c.dv] = (acc[j] / l_scr[j]).T.astype(o_ref.dtype)
                rows.append(m_scr[j] + jnp.log(l_scr[j]))
            rows.append(jnp.zeros((LANES - c.nh, t), F32))
            lse_ref[...] = jnp.concatenate(rows, axis=0).T

    x_in, x_out, x_shapes, x_scratch, x_args = _carry_specs(carry)
    grid_spec = pltpu.PrefetchScalarGridSpec(
        num_scalar_prefetch=2, grid=(c.G, n_pairs),
        in_specs=[pl.BlockSpec((t, c.nh * c.dqk), lambda g, p, qi, kj: (qi[p], c.qcol(g))),
                  pl.BlockSpec((t, c.nkv * c.dqk), lambda g, p, qi, kj: (kj[p], c.kcol(g))),
                  pl.BlockSpec((t, c.nkv * c.dv), lambda g, p, qi, kj: (kj[p], c.vcol(g)))] + x_in,
        out_specs=[pl.BlockSpec((t, c.nh * c.dv), lambda g, p, qi, kj: (qi[p], c.ocol(g))),
                   pl.BlockSpec((t, LANES), lambda g, p, qi, kj: (qi[p], g))] + x_out,
        scratch_shapes=[pltpu.VMEM((c.nh, 1, t), F32), pltpu.VMEM((c.nh, 1, t), F32),
                        pltpu.VMEM((c.nh, c.dv, t), F32)] + x_scratch)
    return _pcall(
        _carrying(body, 3, 2, 3, (c.G, n_pairs), carry), name=name,
        dims=("arbitrary", "arbitrary") if carry is not None else ("parallel", "arbitrary"), grid_spec=grid_spec,
        out_shape=[jax.ShapeDtypeStruct((c.T, c.o_width), out_dtype),
                   jax.ShapeDtypeStruct((c.T, LANES * c.G), F32)] + x_shapes,
    )(qi_tab, kj_tab, q, k, v, *x_args)


def _causal_bwd(cfg, q, k, v, do, lse, delta, *, name, carry=None):
    c = cfg
    assert c.mode == "causal" and c.tq == c.tk and c.T == c.Tk
    t, n = c.tq, c.T // c.tq
    qw, kw, vw = c.nh * c.dqk, c.nkv * c.dqk, c.nkv * c.dv
    qi_tab, kj_tab = _causal_pairs(n, kv_major=True)

    def body(qi_ref, kj_ref, q_ref, k_ref, v_ref, do_ref, lse_ref, d_ref, dq_ref, dk_ref, dv_ref, dk_acc, dv_acc):
        pair = pl.program_id(1)
        qi, kj = qi_ref[pair], kj_ref[pair]

        @pl.when(pair == 0)
        def _():
            dq_ref[...] = jnp.zeros_like(dq_ref)

        @pl.when(qi == kj)
        def _():
            dk_acc[...] = jnp.zeros_like(dk_acc)
            dv_acc[...] = jnp.zeros_like(dv_acc)

        rows = pl.ds(pl.multiple_of(qi * t, t), t)

        def step(diagonal):
            mask = _causal_mask(t) if diagonal else None
            for j in range(c.nh):
                g = j // c.rep
                qs, ks, vs = (slice(j * c.dqk, (j + 1) * c.dqk), slice(g * c.dqk, (g + 1) * c.dqk),
                              slice(g * c.dv, (g + 1) * c.dv))
                qh, doh, kh = q_ref[:, qs], do_ref[:, j * c.dv:(j + 1) * c.dv], k_ref[:, ks]
                s = _dot(qh, kh, NT) * c.scale
                if diagonal:
                    s = jnp.where(mask, s, -jnp.inf)
                p = jnp.exp(s - lse_ref[:, j:j + 1])
                ds = p * (_dot(doh, v_ref[:, vs], NT) - d_ref[:, j:j + 1]) * c.scale
                dq_ref[rows, qs] += _dot(ds, kh, NN)
                dv_acc[g] += _dot(doh, p, TN)
                dk_acc[g] += _dot(qh, ds, TN)

        pl.when(qi == kj)(lambda: step(True))
        pl.when(qi != kj)(lambda: step(False))

        @pl.when(qi == n - 1)
        def _():
            for g in range(c.nkv):
                dk_ref[:, g * c.dqk:(g + 1) * c.dqk] = dk_acc[g].T
                dv_ref[:, g * c.dv:(g + 1) * c.dv] = dv_acc[g].T

    stat = pl.BlockSpec((t, LANES), lambda g, p, qi, kj: (qi[p], g))
    n_pairs = int(qi_tab.shape[0])
    x_in, x_out, x_shapes, x_scratch, x_args = _carry_specs(carry)
    grid_spec = pltpu.PrefetchScalarGridSpec(
        num_scalar_prefetch=2, grid=(c.G, n_pairs),
        in_specs=[pl.BlockSpec((t, qw), lambda g, p, qi, kj: (qi[p], c.qcol(g))),
                  pl.BlockSpec((t, kw), lambda g, p, qi, kj: (kj[p], c.kcol(g))),
                  pl.BlockSpec((t, vw), lambda g, p, qi, kj: (kj[p], c.vcol(g))),
                  pl.BlockSpec((t, c.nh * c.dv), lambda g, p, qi, kj: (qi[p], c.ocol(g))),
                  stat, stat] + x_in,
        out_specs=[pl.BlockSpec((c.T, qw), lambda g, p, qi, kj: (0, g)),
                   pl.BlockSpec((t, kw), lambda g, p, qi, kj: (kj[p], g)),
                   pl.BlockSpec((t, vw), lambda g, p, qi, kj: (kj[p], g))] + x_out,
        scratch_shapes=[pltpu.VMEM((c.nkv, c.dqk, t), F32), pltpu.VMEM((c.nkv, c.dv, t), F32)] + x_scratch)
    return _pcall(
        _carrying(body, 6, 3, 2, (c.G, n_pairs), carry), name=name,
        dims=("arbitrary", "arbitrary") if carry is not None else ("parallel", "arbitrary"), grid_spec=grid_spec,
        out_shape=[_sds((c.T, c.G * qw)), _sds((c.T, c.G * kw)), _sds((c.T, c.G * vw))] + x_shapes,
    )(qi_tab, kj_tab, q, k, v, do, lse, delta, *x_args)


def _rowwise(body, ins, outs, *, name, rows, tm=512, accs=(), scratch=()):
    tm = _row_tile(rows, tm)

    def spec(a):
        if a.shape[0] == 1:
            return pl.BlockSpec((1, a.shape[1]), lambda i: (0, 0))
        d = rows // a.shape[0]
        assert d * a.shape[0] == rows and tm % d == 0
        return pl.BlockSpec((tm // d, a.shape[1]), lambda i: (i, 0))

    return _pcall(
        functools.partial(body, tm), name=name, dims=("arbitrary" if accs else "parallel",), grid=(rows // tm,),
        in_specs=[spec(a) for a in ins], out_specs=[spec(a) for a in outs], out_shape=list(outs),
        scratch_shapes=list(scratch),
    )(*ins)


def _sds(shape, dtype=F32):
    return jax.ShapeDtypeStruct(shape, dtype)


def _acc_rows(ref, val):
    @pl.when(pl.program_id(0) == 0)
    def _():
        ref[...] = jnp.zeros_like(ref)

    ref[...] += jnp.sum(val, axis=0, keepdims=True)


Z_QA, Z_KA, Z_VA, Z_CQ, Z_CKV, Z_KR, Z_END = 0, 512, 640, 768, 1152, 1408, 1536


def _l0_prep(z, tabs, q_norm, kv_norm, *, name):
    S = z.shape[0]

    def body(tm, z_ref, c64, s64, ck, sk, gq, gkv, qa_o, ka_o, va_o, cq_o, ckv_o, kr_o):
        for i in range(4):
            sl = slice(Z_QA + i * LANES, Z_QA + (i + 1) * LANES)
            qa_o[:, i * LANES:(i + 1) * LANES] = _rope_chunk(z_ref[:, sl], c64[...], s64[...], 32).astype(qa_o.dtype)
        ka_o[...] = _rope_chunk(z_ref[:, Z_KA:Z_VA], c64[...], s64[...], 32).astype(ka_o.dtype)
        va_o[...] = z_ref[:, Z_VA:Z_CQ].astype(va_o.dtype)
        cq_o[...] = (_rms_parts(z_ref[:, Z_CQ:Z_CKV])[0] * gq[...]).astype(cq_o.dtype)
        ckv_o[...] = (_rms_parts(z_ref[:, Z_CKV:Z_KR])[0] * gkv[...]).astype(ckv_o.dtype)
        kr_o[...] = _rope_chunk(z_ref[:, Z_KR:Z_END], ck[...], sk[...], 16)

    outs = [_sds((S, 512), MXU_DTYPE), _sds((S, 128), MXU_DTYPE), _sds((S, 128), MXU_DTYPE),
            _sds((S, MLA_Q_RANK), MXU_DTYPE), _sds((S, MLA_KV_RANK), MXU_DTYPE), _sds((S, LANES))]
    ins = [z, tabs["c64"], tabs["s64"], tabs["ck"], tabs["sk"], q_norm.reshape(1, -1), kv_norm.reshape(1, -1)]
    return _rowwise(body, ins, outs, name=name, rows=S)


def _l0_prep_bwd(z, tabs, q_norm, kv_norm, dqa, dka, dva, dcq, dckv, dkr, *, name):
    S = z.shape[0]

    def body(tm, z_ref, c64, s64, ck, sk, gq, gkv, dqa_r, dka_r, dva_r, dcq_r, dckv_r, dkr_r, dz_o, dgq_o, dgkv_o):
        for i in range(4):
            sl = slice(i * LANES, (i + 1) * LANES)
            dz_o[:, sl] = _rope_chunk(dqa_r[:, sl].astype(F32), c64[...], -s64[...], 32).astype(dz_o.dtype)
        dz_o[:, Z_KA:Z_VA] = _rope_chunk(dka_r[...].astype(F32), c64[...], -s64[...], 32).astype(dz_o.dtype)
        dz_o[:, Z_VA:Z_CQ] = dva_r[...].astype(dz_o.dtype)
        dx, dgp = _rms_bwd_rows(z_ref[:, Z_CQ:Z_CKV], gq[...], dcq_r[...].astype(F32))
        dz_o[:, Z_CQ:Z_CKV] = dx.astype(dz_o.dtype)
        _acc_rows(dgq_o, dgp)
        dx, dgp = _rms_bwd_rows(z_ref[:, Z_CKV:Z_KR], gkv[...], dckv_r[...].astype(F32))
        dz_o[:, Z_CKV:Z_KR] = dx.astype(dz_o.dtype)
        _acc_rows(dgkv_o, dgp)
        dz_o[:, Z_KR:Z_END] = _rope_chunk(dkr_r[...], ck[...], -sk[...], 16).astype(dz_o.dtype)

    outs = [_sds((S, Z_END), MXU_DTYPE), _sds((1, MLA_Q_RANK)), _sds((1, MLA_KV_RANK))]
    ins = [z, tabs["c64"], tabs["s64"], tabs["ck"], tabs["sk"], q_norm.reshape(1, -1), kv_norm.reshape(1, -1),
           dqa, dka, dva, dcq, dckv, dkr]
    return _rowwise(body, ins, outs, name=name, rows=S, accs=(1, 2))


def _mla_prep(qb, kvb, kr, tabs, *, name):
    S = qb.shape[0]

    def body(tm, qb_r, kvb_r, kr_r, cm, sm, q_o, k_o, v_o):
        lane = _lane((tm, LANES))
        kr_at_64 = pltpu.roll(kr_r[...], 64, 1)
        for h in range(MLA_HEADS):
            sl = slice(h * LANES, (h + 1) * LANES)
            q_o[:, sl] = _rope_chunk(qb_r[:, sl], cm[...], sm[...], 16).astype(q_o.dtype)
            k_o[:, sl] = jnp.where(lane < 64, kvb_r[:, sl], kr_at_64).astype(k_o.dtype)
        for p in range(MLA_HEADS // 2):
            even = pltpu.roll(kvb_r[:, (2 * p) * LANES:(2 * p + 1) * LANES], 64, 1)
            odd = kvb_r[:, (2 * p + 1) * LANES:(2 * p + 2) * LANES]
            v_o[:, p * LANES:(p + 1) * LANES] = jnp.where(lane < 64, even, odd).astype(v_o.dtype)

    outs = [_sds((S, 1024), MXU_DTYPE), _sds((S, 1024), MXU_DTYPE), _sds((S, 512), MXU_DTYPE)]
    return _rowwise(body, [qb, kvb, kr, tabs["cm"], tabs["sm"]], outs, name=name, rows=S)


def _mla_prep_bwd(dq, dk, dv, tabs, *, name):
    S = dq.shape[0]

    def body(tm, dq_r, dk_r, dv_r, cm, sm, dqb_o, dkvb_o, dkr_o):
        lane = _lane((tm, LANES))
        dkr = jnp.zeros((tm, LANES), F32)
        for h in range(MLA_HEADS):
            sl = slice(h * LANES, (h + 1) * LANES)
            dqb_o[:, sl] = _rope_chunk(dq_r[:, sl].astype(F32), cm[...], -sm[...], 16).astype(dqb_o.dtype)
            dkh = dk_r[:, sl].astype(F32)
            dvp = dv_r[:, (h // 2) * LANES:(h // 2 + 1) * LANES].astype(F32)
            dvh = pltpu.roll(dvp, 64, 1) if h % 2 == 0 else dvp
            dkvb_o[:, sl] = jnp.where(lane < 64, dkh, dvh).astype(dkvb_o.dtype)
            dkr = dkr + pltpu.roll(dkh, 64, 1)
        dkr_o[...] = jnp.where(lane < MLA_ROPE, dkr, 0.0)

    outs = [_sds((S, 1024), MXU_DTYPE), _sds((S, 1024), MXU_DTYPE), _sds((S, LANES))]
    return _rowwise(body, [dq, dk, dv, tabs["cm"], tabs["sm"]], outs, name=name, rows=S)


DILATIONS = tuple(d for _, d in DIL_PATTERNS)
QKV_CHUNKS = 8


def _to_branch(nat, c0, chunks, out_ref, d, rows):
    width = chunks * LANES
    for r in range(d):
        tok = pl.ds(r, rows // d, stride=d) if d > 1 else slice(None)
        for c in range(chunks):
            out_ref[:, r * width + c * LANES:r * width + (c + 1) * LANES] = nat[c0 + c, tok, :].astype(out_ref.dtype)


def _from_branch(in_ref, nat, c0, chunks, d, rows, add=False):
    width = chunks * LANES
    for r in range(d):
        tok = pl.ds(r, rows // d, stride=d) if d > 1 else slice(None)
        for c in range(chunks):
            val = in_ref[:, r * width + c * LANES:r * width + (c + 1) * LANES].astype(F32)
            nat[c0 + c, tok, :] = nat[c0 + c, tok, :] + val if add else val


def _branch_sds(S, width, d, dtype):
    return _sds((S // d, d * width), dtype)


def _l1_prep(qkv, tabs, *, name):
    S = qkv.shape[0]

    def body(tm, x_r, c64, s64, *rest):
        outs, nat = rest[:-1], rest[-1]
        for i in range(QKV_CHUNKS):
            sl = slice(i * LANES, (i + 1) * LANES)
            nat[i] = _rope_chunk(x_r[:, sl], c64[...], s64[...], 32)
            nat[QKV_CHUNKS + i] = _rope_chunk(x_r[:, 1024 + i * LANES:1024 + (i + 1) * LANES], c64[...], s64[...], 32)
            nat[2 * QKV_CHUNKS + i] = x_r[:, 2048 + i * LANES:2048 + (i + 1) * LANES]
        for b, d in enumerate(DILATIONS):
            for t in range(3):
                _to_branch(nat, t * QKV_CHUNKS, QKV_CHUNKS, outs[3 * b + t], d, tm)

    outs = [_branch_sds(S, 1024, d, MXU_DTYPE) for d in DILATIONS for _ in range(3)]
    got = _rowwise(body, [qkv, tabs["c64"], tabs["s64"]], outs, name=name, rows=S,
                   scratch=[pltpu.VMEM((3 * QKV_CHUNKS, _row_tile(S, 512), LANES), F32)])
    return {d: tuple(got[3 * b:3 * b + 3]) for b, d in enumerate(DILATIONS)}


def _l1_prep_bwd(grads, tabs, *, name):
    S = grads[1][0].shape[0]

    def body(tm, *rest):
        ins, (c64, s64, o, nat) = rest[:9], rest[9:]
        for b, d in enumerate(DILATIONS):
            for t in range(3):
                _from_branch(ins[3 * b + t], nat, t * QKV_CHUNKS, QKV_CHUNKS, d, tm, add=b > 0)
        for i in range(QKV_CHUNKS):
            sl = slice(i * LANES, (i + 1) * LANES)
            o[:, sl] = _rope_chunk(nat[i], c64[...], -s64[...], 32).astype(o.dtype)
            o[:, 1024 + i * LANES:1024 + (i + 1) * LANES] = _rope_chunk(
                nat[QKV_CHUNKS + i], c64[...], -s64[...], 32).astype(o.dtype)
            o[:, 2048 + i * LANES:2048 + (i + 1) * LANES] = nat[2 * QKV_CHUNKS + i].astype(o.dtype)

    ins = [g for d in DILATIONS for g in grads[d]] + [tabs["c64"], tabs["s64"]]
    return _rowwise(body, ins, [_sds((S, 3072), MXU_DTYPE)], name=name, rows=S, tm=256,
                    scratch=[pltpu.VMEM((3 * QKV_CHUNKS, _row_tile(S, 256), LANES), F32)])[0]


def _sigmoid(x):
    return 1.0 / (1.0 + jnp.exp(-x))


FFN_ROW_TILE, FFN_COL_TILE = 512, 1408


def _gate_up(h, w_gate, w_up, *, name):
    (M, K), N = h.shape, w_gate.shape[1]
    tm, tn = _tile(M, FFN_ROW_TILE), _tile(N, FFN_COL_TILE)

    def body(h_ref, wg_ref, wu_ref, g_ref, u_ref, a_ref):
        g = _dot(h_ref[...], wg_ref[...], NN)
        u = _dot(h_ref[...], wu_ref[...], NN)
        g_ref[...] = g
        u_ref[...] = u
        a_ref[...] = (g * _sigmoid(g) * u).astype(a_ref.dtype)

    w_spec = pl.BlockSpec((K, tn), lambda j, i: (0, j))
    o_spec = pl.BlockSpec((tm, tn), lambda j, i: (i, j))
    return _pcall(
        body, name=name, dims=("parallel", "parallel"), grid=(N // tn, M // tm),
        in_specs=[pl.BlockSpec((tm, K), lambda j, i: (i, 0)), w_spec, w_spec], out_specs=[o_spec] * 3,
        out_shape=[_sds((M, N)), _sds((M, N)), _sds((M, N), MXU_DTYPE)],
    )(h, w_gate, w_up)


def _gate_up_bwd(dx, w_down, gate, up, *, name):
    (M, K), N = dx.shape, w_down.shape[0]
    tm, tn = _tile(M, FFN_ROW_TILE), _tile(N, FFN_COL_TILE)

    def body(dx_ref, w_ref, g_ref, u_ref, dg_ref, du_ref):
        d = _dot(dx_ref[...], w_ref[...], NT)
        g = g_ref[...]
        sg = _sigmoid(g)
        dg_ref[...] = (d * u_ref[...] * (sg * (1.0 + g * (1.0 - sg)))).astype(dg_ref.dtype)
        du_ref[...] = (d * g * sg).astype(du_ref.dtype)

    o_spec = pl.BlockSpec((tm, tn), lambda j, i: (i, j))
    return _pcall(
        body, name=name, dims=("parallel", "parallel"), grid=(N // tn, M // tm),
        in_specs=[pl.BlockSpec((tm, K), lambda j, i: (i, 0)), pl.BlockSpec((tn, K), lambda j, i: (j, 0)),
                  o_spec, o_spec],
        out_specs=[o_spec] * 2, out_shape=[_sds((M, N), MXU_DTYPE)] * 2,
    )(dx, w_down, gate, up)


def _head_pair_weights(w, c, rows):
    return jnp.where(_lane((rows, LANES)) < HEAD_DIM, w[:, 2 * c:2 * c + 1], w[:, 2 * c + 1:2 * c + 2])


def _merge(outs_by_d, lses_by_d, *, name):
    S = outs_by_d[1].shape[0]
    far = DILATIONS[1:]

    def body(tm, o1, o4, o16, l1, l4, l16, o_o, w1_o, w4_o, w16_o, nat_o, nat_l):
        for b, (o_r, l_r, d) in enumerate(zip((o4, o16), (l4, l16), far)):
            _from_branch(o_r, nat_o, b * QKV_CHUNKS, QKV_CHUNKS, d, tm)
            _from_branch(l_r, nat_l, b, 1, d, tm)
        ls = [l1[...], nat_l[0], nat_l[1]]
        m = jnp.maximum(jnp.maximum(ls[0], ls[1]), ls[2])
        es = [jnp.exp(l - m) for l in ls]
        tot = es[0] + es[1] + es[2]
        ws = [e / tot for e in es]
        for w_o, w in zip((w1_o, w4_o, w16_o), ws):
            w_o[...] = w
        for c in range(QKV_CHUNKS):
            sl = slice(c * LANES, (c + 1) * LANES)
            parts = (o1[:, sl], nat_o[c], nat_o[QKV_CHUNKS + c])
            o_o[:, sl] = sum(_head_pair_weights(w, c, tm) * part for w, part in zip(ws, parts))

    ins = [outs_by_d[d] for d in DILATIONS] + [lses_by_d[d] for d in DILATIONS]
    outs = [_sds((S, 1024))] + [_sds((S, LANES))] * 3
    rows = _row_tile(S, 256)
    return _rowwise(body, ins, outs, name=name, rows=S, tm=256,
                    scratch=[pltpu.VMEM((2 * QKV_CHUNKS, rows, LANES), F32), pltpu.VMEM((2, rows, LANES), F32)])


def _merge_bwd(do, o, ws, *, name):
    S = do.shape[0]

    def body(tm, do_r, o_r, w1, w4, w16, d1, d4, d16, e1, e4, e16, nat, nat_l):
        prod = do_r[...] * o_r[...]
        sums = _cols_to_lanes([jnp.sum(prod[:, j * HEAD_DIM:(j + 1) * HEAD_DIM], axis=1, keepdims=True)
                               for j in range(DIL_HEADS)], tm)
        for w_r, d_o, e_o, d in zip((w1, w4, w16), (d1, d4, d16), (e1, e4, e16), DILATIONS):
            w = w_r[...]
            nat_l[0] = w * sums
            _to_branch(nat_l, 0, 1, e_o, d, tm)
            for c in range(QKV_CHUNKS):
                nat[c] = _head_pair_weights(w, c, tm) * do_r[:, c * LANES:(c + 1) * LANES]
            _to_branch(nat, 0, QKV_CHUNKS, d_o, d, tm)

    outs = [_branch_sds(S, 1024, d, MXU_DTYPE) for d in DILATIONS] + [_branch_sds(S, LANES, d, F32) for d in DILATIONS]
    rows = _row_tile(S, 256)
    got = _rowwise(body, [do, o] + [ws[d] for d in DILATIONS], outs, name=name, rows=S, tm=256,
                   scratch=[pltpu.VMEM((QKV_CHUNKS, rows, LANES), F32), pltpu.VMEM((1, rows, LANES), F32)])
    return dict(zip(DILATIONS, got[:3])), dict(zip(DILATIONS, got[3:]))


def _loss_head(x, g, target, *, name):
    S, D = x.shape

    def body(tm, x_r, g_r, t_r, dx_o, dg_o, sq_o):
        xf = x_r[...]
        xhat, _ = _rms_parts(xf)
        err = xhat * g_r[...] - t_r[...]
        dx, dgp = _rms_bwd_rows(xf, g_r[...], err * (1.0 / D))
        dx_o[...] = dx
        _acc_rows(dg_o, dgp)
        _acc_rows(sq_o, err * err)

    return _rowwise(body, [x, g.reshape(1, D), target], [_sds((S, D)), _sds((1, D)), _sds((1, D))],
                    name=name, rows=S, accs=(1, 2))


def _adamw(w, g, m, v, *, name):
    c1 = 1.0 - ADAM_B1 ** ADAM_STEP
    c2 = 1.0 - ADAM_B2 ** ADAM_STEP

    def body(tm, w_r, g_r, m_r, v_r, d_o, m_o, v_o):
        g = g_r[...]
        m_new = ADAM_B1 * m_r[...] + (1.0 - ADAM_B1) * g
        v_new = ADAM_B2 * v_r[...] + (1.0 - ADAM_B2) * (g * g)
        m_o[...] = m_new
        v_o[...] = v_new
        d_o[...] = -ADAM_LR * ((m_new / c1) / (jnp.sqrt(v_new / c2) + ADAM_EPS) + ADAM_WD * w_r[...])

    return _rowwise(body, [w, g, m, v], [_sds(w.shape)] * 3, name=name, rows=w.shape[0], tm=256)


SUM_ROW_TILE = 256


def _sum_cores(grads, theirs, half_index, *, name):
    _, R, C = grads.shape
    h = R // 2
    nb = h // SUM_ROW_TILE

    def body(c_ref, g_ref, t_ref, o_ref):
        o_ref[...] = (g_ref[...].astype(F32) + t_ref[...].astype(F32)).astype(o_ref.dtype)

    grid_spec = pltpu.PrefetchScalarGridSpec(
        num_scalar_prefetch=1, grid=(4, nb),
        in_specs=[pl.BlockSpec((1, SUM_ROW_TILE, C), lambda k, i, c_ref: (k, c_ref[0] * nb + i, 0)),
                  pl.BlockSpec((1, SUM_ROW_TILE, C), lambda k, i, c_ref: (k, i, 0))],
        out_specs=pl.BlockSpec((1, SUM_ROW_TILE, C), lambda k, i, c_ref: (k, i, 0)))
    return _pcall(body, name=name, dims=("parallel", "parallel"), grid_spec=grid_spec,
                  out_shape=_sds((4, h, C), grads.dtype))(half_index, grads, theirs)


def _sum_chips(parts, half_index, *, name):
    _, h, C = parts.shape
    nb = h // SUM_ROW_TILE

    def body(c_ref, p_ref, o_ref):
        p = [p_ref[k].astype(F32) for k in range(4)]
        o_ref[...] = ((p[0] + p[1]) + p[2]) + p[3]

    grid_spec = pltpu.PrefetchScalarGridSpec(
        num_scalar_prefetch=1, grid=(nb,),
        in_specs=[pl.BlockSpec((4, SUM_ROW_TILE, C), lambda i, c_ref: (0, i, 0))],
        out_specs=pl.BlockSpec((SUM_ROW_TILE, C), lambda i, c_ref: (c_ref[0] * nb + i, 0)))
    return _pcall(body, name=name, dims=("parallel",), grid_spec=grid_spec,
                  out_shape=_sds((2 * h, C)))(half_index, parts)


def _position():
    return lax.axis_index("x"), lax.axis_index("y"), lax.axis_index("c")


def _chip_peers(x, y):
    return [(1 - x, y), (x, 1 - y), (1 - x, 1 - y)]


_HBM = pl.BlockSpec(memory_space=pltpu.HBM)
LOCAL_COPY_CHUNKS = 8


def _local_copies(src_ref, dst_ref, sems):
    rows = src_ref.shape[0] // LOCAL_COPY_CHUNKS
    assert rows * LOCAL_COPY_CHUNKS == src_ref.shape[0]
    return [pltpu.make_async_copy(src_ref.at[pl.ds(i * rows, rows)], dst_ref.at[pl.ds(i * rows, rows)], sems.at[i])
            for i in range(LOCAL_COPY_CHUNKS)]


class _Exchange:
    def __init__(self, src, out_shape, sems, stages):
        self.src, self.out_shape, self.sems, self.stages = src, out_shape, sems, stages

    def run(self, refs, step, n_steps, at_end):
        for fraction, fn in self.stages:
            if (fraction == 1.0) == at_end:
                pl.when(step == int(round(fraction * (n_steps - 1))))(functools.partial(fn, *refs))


def _run_exchange(ex, *, name):
    def body(*refs):
        for _, fn in ex.stages:
            fn(*refs)

    return pl.pallas_call(
        body, name=name, in_specs=[_HBM], out_specs=_HBM, out_shape=ex.out_shape, scratch_shapes=list(ex.sems),
    )(ex.src)


def _gather_exchange(src):
    R, C = src.shape
    h = R // 2

    def plan(src_ref, out_ref, send_sems, recv_sems, local_sems):
        x, y, c = _position()
        me = 2 * x + y
        peers = _chip_peers(x, y)
        mine, other = pl.ds(c * h, h), pl.ds((1 - c) * h, h)

        def copy(sem, src_part, dst_part, device):
            return pltpu.make_async_remote_copy(
                src_ref=src_part, dst_ref=dst_part, send_sem=send_sems.at[sem], recv_sem=recv_sems.at[sem],
                device_id=device, device_id_type=MESH)

        landed = [out_ref.at[2 * px + py, mine] for px, py in peers]
        theirs = [out_ref.at[2 * px + py, other] for px, py in peers]
        return dict(
            sends=lambda: [copy(j, src_ref.at[mine], out_ref.at[me, mine], (px, py, c))
                           for j, (px, py) in enumerate(peers)],
            local=lambda: _local_copies(src_ref, out_ref.at[me], local_sems),
            arrivals=lambda: [copy(j, landed[j], landed[j], (px, py, c)) for j, (px, py) in enumerate(peers)],
            passed=lambda: [copy(3 + j, landed[j], landed[j], (x, y, 1 - c)) for j in range(3)],
            from_sibling=lambda: [copy(3 + j, theirs[j], theirs[j], (x, y, 1 - c)) for j in range(3)])

    def start(*refs):
        p = plan(*refs)
        for cp in p["sends"]() + p["local"]():
            cp.start()

    def pass_on(*refs):
        p = plan(*refs)
        for arrival, forward in zip(p["arrivals"](), p["passed"]()):
            arrival.wait_recv()
            forward.start()

    def finish(*refs):
        p = plan(*refs)
        for cp in p["from_sibling"]():
            cp.wait_recv()
        for cp in p["sends"]() + p["passed"]():
            cp.wait_send()
        for cp in p["local"]():
            cp.wait()

    sems = [pltpu.SemaphoreType.DMA((6,)), pltpu.SemaphoreType.DMA((6,)), pltpu.SemaphoreType.DMA((LOCAL_COPY_CHUNKS,))]
    return _Exchange(src, jax.ShapeDtypeStruct((4, R, C), src.dtype), sems, [(0.0, start), (0.6, pass_on), (1.0, finish)])


def _swap_other_half(src, *, name):
    _, R, C = src.shape
    h = R // 2

    def body(src_ref, out_ref, send_sem, recv_sem):
        x, y, c = _position()
        cp = pltpu.make_async_remote_copy(
            src_ref=src_ref.at[:, pl.ds((1 - c) * h, h)], dst_ref=out_ref, send_sem=send_sem, recv_sem=recv_sem,
            device_id=(x, y, 1 - c), device_id_type=MESH)
        cp.start()
        cp.wait()

    return pl.pallas_call(
        body, name=name, in_specs=[_HBM], out_specs=_HBM, out_shape=jax.ShapeDtypeStruct((4, h, C), src.dtype),
        scratch_shapes=[pltpu.SemaphoreType.DMA, pltpu.SemaphoreType.DMA],
    )(src)


def _scatter_exchange(src):
    def plan(src_ref, out_ref, send_sems, recv_sems, local_sems):
        x, y, c = _position()
        me = 2 * x + y
        peers = _chip_peers(x, y)

        def copy(j, src_block, dst_slot):
            px, py = peers[j]
            return pltpu.make_async_remote_copy(
                src_ref=src_ref.at[src_block], dst_ref=out_ref.at[dst_slot], send_sem=send_sems.at[j],
                recv_sem=recv_sems.at[j], device_id=(px, py, c), device_id_type=MESH)

        return dict(sends=lambda: [copy(j, 2 * px + py, me) for j, (px, py) in enumerate(peers)],
                    arrivals=lambda: [copy(j, me, 2 * px + py) for j, (px, py) in enumerate(peers)],
                    local=lambda: _local_copies(src_ref.at[me], out_ref.at[me], local_sems))

    def start(*refs):
        p = plan(*refs)
        for cp in p["sends"]() + p["local"]():
            cp.start()

    def finish(*refs):
        p = plan(*refs)
        for cp in p["arrivals"]():
            cp.wait_recv()
        for cp in p["sends"]():
            cp.wait_send()
        for cp in p["local"]():
            cp.wait()

    sems = [pltpu.SemaphoreType.DMA((3,)), pltpu.SemaphoreType.DMA((3,)), pltpu.SemaphoreType.DMA((LOCAL_COPY_CHUNKS,))]
    return _Exchange(src, jax.ShapeDtypeStruct(src.shape, src.dtype), sems, [(0.0, start), (1.0, finish)])


def _join_halves(src, *, name):
    R, C = src.shape
    h = R // 2

    def body(src_ref, out_ref, send_sem, recv_sem):
        x, y, c = _position()
        mine, theirs = pl.ds(c * h, h), pl.ds((1 - c) * h, h)
        cp = pltpu.make_async_remote_copy(
            src_ref=src_ref.at[mine], dst_ref=out_ref.at[mine], send_sem=send_sem, recv_sem=recv_sem,
            device_id=(x, y, 1 - c), device_id_type=MESH)
        cp.start()
        pltpu.make_async_remote_copy(
            src_ref=src_ref.at[theirs], dst_ref=out_ref.at[theirs], send_sem=send_sem, recv_sem=recv_sem,
            device_id=(x, y, 1 - c), device_id_type=MESH).wait_recv()
        cp.wait_send()

    return pl.pallas_call(
        body, name=name, in_specs=[_HBM], out_specs=_HBM, out_shape=jax.ShapeDtypeStruct((R, C), src.dtype),
        input_output_aliases={0: 0},
        scratch_shapes=[pltpu.SemaphoreType.DMA, pltpu.SemaphoreType.DMA],
    )(src)


def _allreduce_small(vec, *, name):
    R, C = vec.shape

    def body(v_ref, o_ref, slots, send_sems, recv_sems):
        x, y, c = _position()
        me = 4 * x + 2 * y + c

        def peer(k):
            return x ^ ((k >> 2) & 1), y ^ ((k >> 1) & 1), c ^ (k & 1)

        def copy(k, slot):
            return pltpu.make_async_remote_copy(
                src_ref=v_ref, dst_ref=slots.at[slot], send_sem=send_sems.at[k - 1], recv_sem=recv_sems.at[k - 1],
                device_id=peer(k), device_id_type=MESH)

        slots[me] = v_ref[...]
        sends = [copy(k, me) for k in range(1, 8)]
        for cp in sends:
            cp.start()
        for k in range(1, 8):
            px, py, pc = peer(k)
            copy(k, 4 * px + 2 * py + pc).wait_recv()
        total = slots[0]
        for d in range(1, 8):
            total = total + slots[d]
        o_ref[...] = total
        for cp in sends:
            cp.wait_send()

    vmem = pl.BlockSpec(memory_space=pltpu.VMEM)
    return pl.pallas_call(
        body, name=name, in_specs=[vmem], out_specs=vmem, out_shape=jax.ShapeDtypeStruct((R, C), vec.dtype),
        scratch_shapes=[pltpu.VMEM((8, R, C), vec.dtype), pltpu.SemaphoreType.DMA((7,)), pltpu.SemaphoreType.DMA((7,))],
    )(vec)


def _cross_cfg(S, mem_len):
    return _Attn(T=S, Tk=mem_len, G=1, nh=X_HEADS, rep=1, dqk=X_HEAD_DIM, dv=X_HEAD_DIM, tq=512, tk=mem_len,
                 mode="none", scale=X_HEAD_DIM ** -0.5, qcol=lambda g: 0, kcol=lambda g: 0, vcol=lambda g: 1,
                 ocol=lambda g: 0, o_width=X_HEADS * X_HEAD_DIM)


def _swa_cfg(S):
    return _Attn(T=S, Tk=S, G=1, nh=SWA_HEADS, rep=SWA_HEADS // SWA_KV_HEADS, dqk=HEAD_DIM, dv=HEAD_DIM, tq=BLOCK,
                 tk=BLOCK, mode="band", max_dist=SWA_WINDOW - 1, scale=HEAD_DIM ** -0.5, qcol=lambda g: 0,
                 kcol=lambda g: 0, vcol=lambda g: 0, ocol=lambda g: 0, o_width=SWA_HEADS * HEAD_DIM)


def _mla_cfg(S):
    t = _tile(S, 512)
    return _Attn(T=S, Tk=S, G=MLA_HEADS // 2, nh=2, rep=1, dqk=LANES, dv=MLA_V, tq=t, tk=t, mode="causal",
                 scale=(MLA_NOPE + MLA_ROPE) ** -0.5, qcol=lambda g: g, kcol=lambda g: g, vcol=lambda g: g,
                 ocol=lambda g: g, o_width=MLA_HEADS * MLA_V)


def _dil_cfg(S, window, dil):
    return _Attn(T=S // dil, Tk=S // dil, G=dil, nh=DIL_HEADS, rep=1, dqk=HEAD_DIM, dv=HEAD_DIM, tq=BLOCK, tk=BLOCK,
                 mode="band", max_dist=window // dil, scale=HEAD_DIM ** -0.5, qcol=lambda g: g, kcol=lambda g: g,
                 vcol=lambda g: g, ocol=lambda g: g, o_width=dil * DIL_HEADS * HEAD_DIM)


def _cross_fwd(p, x, mem, W, vec):
    S = x.shape[0]
    cfg = _cross_cfg(S, mem.shape[0])
    hx = _rmsnorm(x, vec[p + "x_norm"], name=p + "x_norm")
    qx = _mm(hx, W[p + "w_xq"], mode="nn", name=p + "xq", out_dtype=MXU_DTYPE)
    memn = _rmsnorm(mem, vec[p + "mem_norm"], name=p + "mem_norm")
    kvx = _mm(memn, W[p + "w_xkv"], mode="nn", name=p + "xkv", out_dtype=MXU_DTYPE)
    ox, lse = _attn_fwd(cfg, qx, kvx, kvx, name=p + "x_attn", out_dtype=MXU_DTYPE)
    out = _mm(ox, W[p + "w_xo"], mode="nn", name=p + "xo", res=x)
    return out, (x, hx, qx, memn, kvx, ox, lse)


def _cross_bwd(p, dx, saved, mem, W, vec, dW, dvec):
    x, hx, qx, memn, kvx, ox, lse = saved
    cfg = _cross_cfg(x.shape[0], mem.shape[0])
    dox = _mm(dx, W[p + "w_xo"], mode="nt", name=p + "xo_dx", out_dtype=MXU_DTYPE)
    dW[p + "w_xo"] = _mm(ox, dx, mode="tn", name=p + "xo_dw")
    delta, _ = _attn_delta(cfg, ox, dox, name=p + "x_delta")
    dqx = _attn_dq(cfg, qx, kvx, kvx, dox, lse, delta, name=p + "x_dq", out_dtype=MXU_DTYPE)
    dkx, dvx = _attn_dkv(cfg, qx, kvx, kvx, dox, lse, delta, name=p + "x_dkv", out_dtype=MXU_DTYPE)
    dkvx = jnp.concatenate([dkx, dvx], axis=1)
    dhx = _mm(dqx, W[p + "w_xq"], mode="nt", name=p + "xq_dx")
    dW[p + "w_xq"] = _mm(hx, dqx, mode="tn", name=p + "xq_dw")
    dW[p + "w_xkv"] = _mm(memn, dkvx, mode="tn", name=p + "xkv_dw")
    dmemn = _mm(dkvx, W[p + "w_xkv"], mode="nt", name=p + "xkv_dx")
    _, dvec[p + "mem_norm"] = _rmsnorm_bwd(mem, vec[p + "mem_norm"], dmemn, name=p + "mem_norm_bwd")
    dx_in, dvec[p + "x_norm"] = _rmsnorm_bwd(x, vec[p + "x_norm"], dhx, name=p + "x_norm_bwd", dres=dx)
    return dx_in


def _ffn_fwd(p, x, W, vec):
    hf = _rmsnorm(x, vec[p + "ffn_norm"], name=p + "ffn_norm")
    gate, up, act = _gate_up(hf, W[p + "w_gate"], W[p + "w_up"], name=p + "gate_up")
    out = _mm(act, W[p + "w_down"], mode="nn", name=p + "down", res=x)
    return out, (x, hf, gate, up, act)


def _ffn_bwd(p, dx, saved, W, vec, dW, dvec):
    x, hf, gate, up, act = saved
    dW[p + "w_down"] = _mm(act, dx, mode="tn", name=p + "down_dw")
    dgate, dup = _gate_up_bwd(dx, W[p + "w_down"], gate, up, name=p + "gate_up_bwd")
    dhf = _mm(dgate, W[p + "w_gate"], mode="nt", name=p + "gate_dx")
    dhf = _mm(dup, W[p + "w_up"], mode="nt", name=p + "up_dx", res=dhf)
    dW[p + "w_gate"] = _mm(hf, dgate, mode="tn", name=p + "gate_dw")
    dW[p + "w_up"] = _mm(hf, dup, mode="tn", name=p + "up_dw")
    dx_in, dvec[p + "ffn_norm"] = _rmsnorm_bwd(x, vec[p + "ffn_norm"], dhf, name=p + "ffn_norm_bwd", dres=dx)
    return dx_in


def _even_fwd(p, x, tabs, W, vec, comm=None):
    S = x.shape[0]
    h = _rmsnorm(x, vec[p + "mix_norm"], name=p + "mix_norm")
    z = _mm(h, W[p + "w_in"], mode="nn", name=p + "in")
    qa, ka, va, cqn, ckvn, kr = _l0_prep(z, tabs, vec[p + "q_norm"], vec[p + "kv_norm"], name=p + "prep")
    sink = jnp.pad(vec[p + "sinks"], (0, LANES - SWA_HEADS)).reshape(1, LANES)
    oa, lse_a = _band_fwd(_swa_cfg(S), qa, ka, va, name=p + "swa", sink=sink, out_dtype=MXU_DTYPE)
    qb = _mm(cqn, W[p + "w_uq"], mode="nn", name=p + "uq")
    kvb = _mm(ckvn, W[p + "w_ukv"], mode="nn", name=p + "ukv")
    Q, K, V = _mla_prep(qb, kvb, kr, tabs, name=p + "mla_prep")
    if comm is None:
        ob, lse_b = _causal_fwd(_mla_cfg(S), Q, K, V, name=p + "mla", out_dtype=MXU_DTYPE)
    else:
        ob, lse_b, gathered = _causal_fwd(_mla_cfg(S), Q, K, V, name=p + "mla", out_dtype=MXU_DTYPE,
                                          carry=comm.late_weights_exchange())
        W = {**W, **comm.late_weights(gathered)}
    o = jnp.concatenate([oa, ob], axis=1)
    out = _mm(o, W[p + "w_out"], mode="nn", name=p + "out", res=x)
    return out, (x, h, z, qa, ka, va, cqn, ckvn, sink, oa, lse_a, Q, K, V, ob, lse_b, o), W


def _even_bwd(p, dx, saved, tabs, W, vec, dW, dvec, comm=None):
    x, h, z, qa, ka, va, cqn, ckvn, sink, oa, lse_a, Q, K, V, ob, lse_b, o = saved
    S = x.shape[0]
    do = _mm(dx, W[p + "w_out"], mode="nt", name=p + "out_dx", out_dtype=MXU_DTYPE)
    dW[p + "w_out"] = _mm(o, dx, mode="tn", name=p + "out_dw")
    doa, dob = do[:, :SWA_HEADS * HEAD_DIM], do[:, SWA_HEADS * HEAD_DIM:]
    cfg = _swa_cfg(S)
    delta, dsink = _attn_delta(cfg, oa, doa, name=p + "swa_delta", lse=lse_a, sink=sink)
    dvec[p + "sinks"] = dsink
    dqa, dka, dva = _band_bwd(cfg, qa, ka, va, doa, lse_a, delta, name=p + "swa_bwd")
    cfg = _mla_cfg(S)
    delta, _ = _attn_delta(cfg, ob, dob, name=p + "mla_delta")
    if comm is None:
        dQ, dK, dV = _causal_bwd(cfg, Q, K, V, dob, lse_b, delta, name=p + "mla_bwd")
    else:
        dQ, dK, dV, landed = _causal_bwd(cfg, Q, K, V, dob, lse_b, delta, name=p + "mla_bwd",
                                         carry=comm.late_grads_exchange(dW))
        comm.late_grads_landed(landed)
    dqb, dkvb, dkr = _mla_prep_bwd(dQ, dK, dV, tabs, name=p + "mla_prep_bwd")
    dcqn = _mm(dqb, W[p + "w_uq"], mode="nt", name=p + "uq_dx")
    dW[p + "w_uq"] = _mm(cqn, dqb, mode="tn", name=p + "uq_dw")
    dckvn = _mm(dkvb, W[p + "w_ukv"], mode="nt", name=p + "ukv_dx")
    dW[p + "w_ukv"] = _mm(ckvn, dkvb, mode="tn", name=p + "ukv_dw")
    dz, dvec[p + "q_norm"], dvec[p + "kv_norm"] = _l0_prep_bwd(
        z, tabs, vec[p + "q_norm"], vec[p + "kv_norm"], dqa, dka, dva, dcqn, dckvn, dkr, name=p + "prep_bwd")
    dh = _mm(dz, W[p + "w_in"], mode="nt", name=p + "in_dx")
    dW[p + "w_in"] = _mm(h, dz, mode="tn", name=p + "in_dw")
    dx_in, dvec[p + "mix_norm"] = _rmsnorm_bwd(x, vec[p + "mix_norm"], dh, name=p + "mix_norm_bwd", dres=dx)
    return dx_in


def _odd_fwd(p, x, tabs, W, vec):
    S = x.shape[0]
    assert S % (DIL_PATTERNS[-1][1] * BLOCK) == 0, "keys past the end of the sequence are never attended"
    h = _rmsnorm(x, vec[p + "mix_norm"], name=p + "mix_norm")
    qkv = _mm(h, W[p + "w_qkv"], mode="nn", name=p + "qkv")
    qkv_by_d = _l1_prep(qkv, tabs, name=p + "prep")
    outs, lses = {}, {}
    for window, dil in DIL_PATTERNS:
        outs[dil], lses[dil] = _band_fwd(_dil_cfg(S, window, dil), *qkv_by_d[dil], name=p + "dil%d" % dil)
    o, w1, w4, w16 = _merge(outs, lses, name=p + "merge")
    out = _mm(o, W[p + "w_out"], mode="nn", name=p + "out", res=x)
    return out, (x, h, qkv_by_d, lses, dict(zip(DILATIONS, (w1, w4, w16))), o)


def _odd_bwd(p, dx, saved, tabs, W, vec, dW, dvec):
    x, h, qkv_by_d, lses, ws, o = saved
    S = x.shape[0]
    do = _mm(dx, W[p + "w_out"], mode="nt", name=p + "out_dx")
    dW[p + "w_out"] = _mm(o, dx, mode="tn", name=p + "out_dw")
    dos, deltas = _merge_bwd(do, o, ws, name=p + "merge_bwd")
    grads = {}
    for window, dil in DIL_PATTERNS:
        grads[dil] = _band_bwd(_dil_cfg(S, window, dil), *qkv_by_d[dil], dos[dil], lses[dil], deltas[dil],
                               name=p + "dil%d_bwd" % dil)
    dqkv = _l1_prep_bwd(grads, tabs, name=p + "prep_bwd")
    dh = _mm(dqkv, W[p + "w_qkv"], mode="nt", name=p + "qkv_dx")
    dW[p + "w_qkv"] = _mm(h, dqkv, mode="tn", name=p + "qkv_dw")
    dx_in, dvec[p + "mix_norm"] = _rmsnorm_bwd(x, vec[p + "mix_norm"], dh, name=p + "mix_norm_bwd", dres=dx)
    return dx_in


def _local_step(x, mem, positions, target, W, vec, comm=None):
    tabs = _rope_tables(positions)
    x1, s_mix0, W = _even_fwd("l0_", x, tabs, W, vec, comm)
    x2, s_x0 = _cross_fwd("l0_", x1, mem, W, vec)
    x3, s_f0 = _ffn_fwd("l0_", x2, W, vec)
    x4, s_mix1 = _odd_fwd("l1_", x3, tabs, W, vec)
    x5, s_x1 = _cross_fwd("l1_", x4, mem, W, vec)
    x6, s_f1 = _ffn_fwd("l1_", x5, W, vec)
    dW, dvec = {}, {}
    dx, dvec["final_norm"], sq = _loss_head(x6, vec["final_norm"], target, name="loss_head")
    dx = _ffn_bwd("l1_", dx, s_f1, W, vec, dW, dvec)
    dx = _cross_bwd("l1_", dx, s_x1, mem, W, vec, dW, dvec)
    dx = _odd_bwd("l1_", dx, s_mix1, tabs, W, vec, dW, dvec)
    dx = _ffn_bwd("l0_", dx, s_f0, W, vec, dW, dvec)
    dx = _cross_bwd("l0_", dx, s_x0, mem, W, vec, dW, dvec)
    dx = _even_bwd("l0_", dx, s_mix0, tabs, W, vec, dW, dvec, comm)
    return sq, dx, dW, dvec


_LAYER_MATS = {
    0: [("w_in", "col"), ("w_uq", "col"), ("w_ukv", "col"), ("w_out", "row"), ("w_xq", "row"), ("w_xkv", "row"),
        ("w_xo", "col"), ("w_gate", "col"), ("w_up", "col"), ("w_down", "row")],
    1: [("w_qkv", "col"), ("w_out", "row"), ("w_xq", "row"), ("w_xkv", "row"), ("w_xo", "col"), ("w_gate", "col"),
        ("w_up", "col"), ("w_down", "row")],
}
MATS = [("l%d_%s" % (l, n), kind) for l in (0, 1) for n, kind in _LAYER_MATS[l]]
_LAYER_VECS = {0: ["mix_norm", "sinks", "q_norm", "kv_norm", "x_norm", "mem_norm", "ffn_norm"],
               1: ["mix_norm", "x_norm", "mem_norm", "ffn_norm"]}
VECS = ["l%d_%s" % (l, n) for l in (0, 1) for n in _LAYER_VECS[l]] + ["final_norm"]
WEIGHT_ORDER = (["l0_mix_norm", "l0_w_in", "l0_sinks", "l0_q_norm", "l0_w_uq", "l0_kv_norm", "l0_w_ukv", "l0_w_out",
                 "l0_x_norm", "l0_mem_norm", "l0_w_xq", "l0_w_xkv", "l0_w_xo", "l0_ffn_norm", "l0_w_gate", "l0_w_up",
                 "l0_w_down", "l1_mix_norm", "l1_w_qkv", "l1_w_out", "l1_x_norm", "l1_mem_norm", "l1_w_xq",
                 "l1_w_xkv", "l1_w_xo", "l1_ffn_norm", "l1_w_gate", "l1_w_up", "l1_w_down", "final_norm"])
PACK_COLS = 1024
PACK_ROW_TILE = 2 * SUM_ROW_TILE
EXCHANGE_DTYPE = jnp.bfloat16
VEC_ROWS = 16
LOSS_ROW = len(VECS)
N_CHIPS = 4


class _Group:
    def __init__(self, mats, shards):
        self.mats, self.shards = mats, shards
        self.layout, off = {}, 0
        for name, _ in mats:
            n = shards[name].size // PACK_COLS
            assert n * PACK_COLS == shards[name].size
            self.layout[name] = (off, n)
            off += n
        self.used = off
        self.rows = -(-off // PACK_ROW_TILE) * PACK_ROW_TILE

    def pack(self, tensors, dtype):
        parts = [tensors[name].astype(dtype).reshape(-1, PACK_COLS) for name, _ in self.mats]
        return jnp.concatenate(parts + [jnp.zeros((self.rows - self.used, PACK_COLS), dtype)], axis=0)

    def unpack(self, packed):
        return {name: packed[off:off + n].reshape(self.shards[name].shape) for name, (off, n) in self.layout.items()}

    def full_weights(self, gathered):
        W = {}
        for name, kind in self.mats:
            off, n = self.layout[name]
            r, cw = self.shards[name].shape
            blocks = gathered[:, off:off + n].reshape(N_CHIPS, r, cw)
            W[name] = blocks.reshape(N_CHIPS * r, cw) if kind == "row" else (
                jnp.transpose(blocks, (1, 0, 2)).reshape(r, N_CHIPS * cw))
        if "l0_w_in" in W:
            W["l0_w_in"] = jnp.pad(W["l0_w_in"], ((0, 0), (0, Z_END - W["l0_w_in"].shape[1])))
        if "l0_w_uq" in W:
            uq = W["l0_w_uq"].reshape(MLA_Q_RANK, MLA_HEADS, MLA_NOPE + MLA_ROPE)
            uq = jnp.pad(uq, ((0, 0), (0, 0), (0, LANES - MLA_NOPE - MLA_ROPE)))
            W["l0_w_uq"] = uq.reshape(MLA_Q_RANK, MLA_HEADS * LANES)
        return W

    def pack_grads(self, dW):
        parts = []
        for name, kind in self.mats:
            r, cw = self.shards[name].shape
            g = dW[name]
            if name == "l0_w_in":
                g = g[:, :Z_KR + MLA_ROPE]
            if name == "l0_w_uq":
                g = g.reshape(MLA_Q_RANK, MLA_HEADS, LANES)[:, :, :MLA_NOPE + MLA_ROPE].reshape(MLA_Q_RANK, -1)
            if kind == "col":
                g = jnp.transpose(g.reshape(r, N_CHIPS, cw), (1, 0, 2))
            parts.append(g.reshape(N_CHIPS, -1, PACK_COLS).astype(EXCHANGE_DTYPE))
        pad = jnp.zeros((N_CHIPS, self.rows - self.used, PACK_COLS), EXCHANGE_DTYPE)
        return jnp.concatenate(parts + [pad], axis=1)


def _pack_vecs(vecs):
    rows = [jnp.pad(vecs[n].reshape(-1).astype(F32), (0, PACK_COLS - vecs[n].size)) for n in VECS]
    rows += [jnp.zeros((PACK_COLS,), F32)] * (VEC_ROWS - len(rows))
    return jnp.stack(rows)


def _unpack_vecs(packed, like):
    return {n: packed[i, :like[n].size].reshape(like[n].shape) for i, n in enumerate(VECS)}


EARLY_MATS = [m for m in MATS if m[0] in ("l0_w_in", "l0_w_uq", "l0_w_ukv")]
LATE_MATS = [m for m in MATS if m not in EARLY_MATS]


class _StepComm:
    def __init__(self, shards):
        self.early, self.late = _Group(EARLY_MATS, shards), _Group(LATE_MATS, shards)
        self.half_index = lax.axis_index("c").astype(jnp.int32).reshape(1)
        self.late_grads = None

    def early_weights(self):
        src = self.early.pack(self.early.shards, MXU_DTYPE)
        return self.early.full_weights(_run_exchange(_gather_exchange(src), name="gather_early"))

    def late_weights_exchange(self):
        return _gather_exchange(self.late.pack(self.late.shards, MXU_DTYPE))

    def late_weights(self, gathered):
        return self.late.full_weights(gathered)

    def _chip_sum(self, group, dW, tag):
        grads = group.pack_grads(dW)
        theirs = _swap_other_half(grads, name="swap_other_half_" + tag)
        return _sum_cores(grads, theirs, self.half_index, name="sum_cores_" + tag)

    def _finish(self, parts, tag):
        return _join_halves(_sum_chips(parts, self.half_index, name="sum_chips_" + tag), name="join_halves_" + tag)

    def late_grads_exchange(self, dW):
        return _scatter_exchange(self._chip_sum(self.late, dW, "late"))

    def late_grads_landed(self, parts):
        self.late_grads = self._finish(parts, "late")

    def early_grads(self, dW):
        parts = _run_exchange(_scatter_exchange(self._chip_sum(self.early, dW, "early")), name="scatter_early")
        return self._finish(parts, "early")


def _step(a):
    weights = {n: a[n] for n in WEIGHT_ORDER}
    shards = {n: weights[n] for n, _ in MATS}
    vec = {n: weights[n] for n in VECS}
    comm = _StepComm(shards)
    sq, grad_x, dW, dvec = _local_step(a["x"][0], a["mem"][0], a["positions"], a["loss_target"][0],
                                       comm.early_weights(), vec, comm)

    dvec = dict(dvec)
    dvec["l0_sinks"] = dvec["l0_sinks"][0, :SWA_HEADS]
    small = _pack_vecs(dvec)
    small = small.at[LOSS_ROW, 0].set(0.5 / a["x"].shape[-1] * jnp.sum(sq))
    small = _allreduce_small(small, name="reduce_gains")
    loss = small[LOSS_ROW, 0]
    g_s = small.at[LOSS_ROW, 0].set(0.0)
    d_s, m_s, v_s = _adamw(_pack_vecs(vec), g_s, _pack_vecs({n: a["m_" + n] for n in VECS}),
                           _pack_vecs({n: a["v_" + n] for n in VECS}), name="adamw_gains")
    got = [_unpack_vecs(packed, vec) for packed in (g_s, d_s, m_s, v_s)]

    for group, g_w in ((comm.late, comm.late_grads), (comm.early, comm.early_grads(dW))):
        for n, g in group.unpack(g_w).items():
            results = (g,) + tuple(_adamw(shards[n], g, a["m_" + n], a["v_" + n], name="adamw_" + n))
            for kind, value in zip(got, results):
                kind[n] = value

    out = [loss, grad_x[None]]
    for kind in got:
        out += [kind[n] for n in WEIGHT_ORDER]
    return tuple(out)


def kernel(x, mem, positions, l0_mix_norm, l0_w_in, l0_sinks, l0_q_norm, l0_w_uq, l0_kv_norm, l0_w_ukv, l0_w_out, l0_x_norm, l0_mem_norm, l0_w_xq, l0_w_xkv, l0_w_xo, l0_ffn_norm, l0_w_gate, l0_w_up, l0_w_down, l1_mix_norm, l1_w_qkv, l1_w_out, l1_x_norm, l1_mem_norm, l1_w_xq, l1_w_xkv, l1_w_xo, l1_ffn_norm, l1_w_gate, l1_w_up, l1_w_down, final_norm, loss_target, m_l0_mix_norm, m_l0_w_in, m_l0_sinks, m_l0_q_norm, m_l0_w_uq, m_l0_kv_norm, m_l0_w_ukv, m_l0_w_out, m_l0_x_norm, m_l0_mem_norm, m_l0_w_xq, m_l0_w_xkv, m_l0_w_xo, m_l0_ffn_norm, m_l0_w_gate, m_l0_w_up, m_l0_w_down, m_l1_mix_norm, m_l1_w_qkv, m_l1_w_out, m_l1_x_norm, m_l1_mem_norm, m_l1_w_xq, m_l1_w_xkv, m_l1_w_xo, m_l1_ffn_norm, m_l1_w_gate, m_l1_w_up, m_l1_w_down, m_final_norm, v_l0_mix_norm, v_l0_w_in, v_l0_sinks, v_l0_q_norm, v_l0_w_uq, v_l0_kv_norm, v_l0_w_ukv, v_l0_w_out, v_l0_x_norm, v_l0_mem_norm, v_l0_w_xq, v_l0_w_xkv, v_l0_w_xo, v_l0_ffn_norm, v_l0_w_gate, v_l0_w_up, v_l0_w_down, v_l1_mix_norm, v_l1_w_qkv, v_l1_w_out, v_l1_x_norm, v_l1_mem_norm, v_l1_w_xq, v_l1_w_xkv, v_l1_w_xo, v_l1_ffn_norm, v_l1_w_gate, v_l1_w_up, v_l1_w_down, v_final_norm):
    return _step(dict(locals()))
```

```python
import functools

import jax
import jax.numpy as jnp
import numpy as np
from jax import lax
from jax.experimental import pallas as pl
from jax.experimental.pallas import tpu as pltpu

F32 = jnp.float32
MXU_DTYPE = jnp.bfloat16
LANES = 128
VMEM_LIMIT_BYTES = 56 * 1024 * 1024

NORM_EPS = 1e-6
ROPE_THETA = 10000.0
BLOCK = 128
HEAD_DIM = 64
SWA_HEADS, SWA_KV_HEADS, SWA_WINDOW = 8, 2, 128
MLA_HEADS, MLA_Q_RANK, MLA_KV_RANK, MLA_NOPE, MLA_ROPE, MLA_V = 8, 384, 256, 64, 32, 64
DIL_HEADS = 16
DIL_PATTERNS = ((128, 1), (512, 4), (2048, 16))
X_HEADS, X_HEAD_DIM = 4, 128
ADAM_LR, ADAM_B1, ADAM_B2, ADAM_EPS, ADAM_WD, ADAM_STEP = 0.001, 0.9, 0.999, 1e-08, 0.01, 10
MESH = pl.DeviceIdType.MESH
NEG_BIG = -1e30

NN = (((1,), (0,)), ((), ()))
NT = (((1,), (1,)), ((), ()))


def _dot(a, b, dims=NN):
    return lax.dot_general(a.astype(MXU_DTYPE), b.astype(MXU_DTYPE), dims, preferred_element_type=F32)


def _pcall(body, *, name, dims=None, **kw):
    params = pltpu.CompilerParams(dimension_semantics=dims, vmem_limit_bytes=VMEM_LIMIT_BYTES)
    return pl.pallas_call(body, name=name, compiler_params=params, **kw)


def _tile(n, pref):
    t = (min(pref, n) // LANES) * LANES
    while t >= LANES:
        if n % t == 0:
            return t
        t -= LANES
    return n


SUBLANES_PACKED = 16


def _row_tile(n, pref):
    t = (min(pref, n) // SUBLANES_PACKED) * SUBLANES_PACKED
    while t >= SUBLANES_PACKED:
        if n % t == 0:
            return t
        t -= SUBLANES_PACKED
    return n


def _lane(shape):
    return lax.broadcasted_iota(jnp.int32, shape, 1)


def _cols_to_lanes(cols, rows):
    lane = _lane((rows, LANES))
    out = jnp.zeros((rows, LANES), F32)
    for j, col in enumerate(cols):
        out = jnp.where(lane == j, col, out)
    return out


def _mm(a, b, *, mode, name, res=None, out_dtype=F32, tm=1408, tn=1536, tk=1408):
    if mode == "nn":
        (M, K), (K2, N) = a.shape, b.shape
    elif mode == "nt":
        (M, K), (N, K2) = a.shape, b.shape
    else:
        (K, M), (K2, N) = a.shape, b.shape
    assert K == K2, (a.shape, b.shape, mode)
    tm, tn, tk = _tile(M, tm), _tile(N, tn), _tile(K, tk)
    nk = K // tk
    in_place = out_dtype == F32 or nk == 1

    def body(*refs):
        refs = list(refs)
        a_ref, b_ref = refs[:2]
        r_ref = refs[2] if res is not None else None
        o_ref = refs[3 if res is not None else 2]
        acc = o_ref if in_place else refs[-1]
        k = pl.program_id(2)
        if mode == "nn":
            part = _dot(a_ref[...], b_ref[...], NN)
        elif mode == "nt":
            part = _dot(a_ref[...], b_ref[...], NT)
        else:
            part = _dot(a_ref[...].T, b_ref[...], NN)
        if nk == 1:
            o_ref[...] = (part if res is None else part + r_ref[...].astype(F32)).astype(o_ref.dtype)
            return

        @pl.when(k == 0)
        def _():
            acc[...] = part if res is None else part + r_ref[...].astype(F32)

        @pl.when(k > 0)
        def _():
            acc[...] += part

        if not in_place:
            @pl.when(k == nk - 1)
            def _():
                o_ref[...] = acc[...].astype(o_ref.dtype)

    if mode == "nn":
        a_spec = pl.BlockSpec((tm, tk), lambda i, j, k: (i, k))
        b_spec = pl.BlockSpec((tk, tn), lambda i, j, k: (k, j))
    elif mode == "nt":
        a_spec = pl.BlockSpec((tm, tk), lambda i, j, k: (i, k))
        b_spec = pl.BlockSpec((tn, tk), lambda i, j, k: (j, k))
    else:
        a_spec = pl.BlockSpec((tk, tm), lambda i, j, k: (k, i))
        b_spec = pl.BlockSpec((tk, tn), lambda i, j, k: (k, j))
    o_spec = pl.BlockSpec((tm, tn), lambda i, j, k: (i, j))
    in_specs = [a_spec, b_spec] + ([] if res is None else [o_spec])
    args = (a, b) + (() if res is None else (res,))
    return _pcall(
        body, name=name, dims=("parallel", "parallel", "arbitrary"),
        grid=(M // tm, N // tn, nk), in_specs=in_specs, out_specs=o_spec,
        out_shape=jax.ShapeDtypeStruct((M, N), out_dtype),
        scratch_shapes=[] if in_place else [pltpu.VMEM((tm, tn), F32)],
    )(*args)


def _rms_parts(xf):
    r = lax.rsqrt(jnp.mean(xf * xf, axis=-1, keepdims=True) + NORM_EPS)
    return xf * r, r


def _rms_bwd_rows(xf, g, dy):
    xhat, r = _rms_parts(xf)
    dxhat = dy * g
    dx = r * (dxhat - xhat * jnp.mean(dxhat * xhat, axis=-1, keepdims=True))
    return dx, dy * xhat


def _rmsnorm(x, g, *, name, out_dtype=MXU_DTYPE, tm=512):
    M, D = x.shape
    tm = _tile(M, tm)

    def body(x_ref, g_ref, o_ref):
        xhat, _ = _rms_parts(x_ref[...].astype(F32))
        o_ref[...] = (xhat * g_ref[...]).astype(o_ref.dtype)

    return _pcall(
        body, name=name, dims=("parallel",), grid=(M // tm,),
        in_specs=[pl.BlockSpec((tm, D), lambda i: (i, 0)), pl.BlockSpec((1, D), lambda i: (0, 0))],
        out_specs=pl.BlockSpec((tm, D), lambda i: (i, 0)),
        out_shape=jax.ShapeDtypeStruct((M, D), out_dtype),
    )(x, g.reshape(1, D))


def _rmsnorm_bwd(x, g, dy, *, name, dres=None, tm=512):
    M, D = x.shape
    tm = _tile(M, tm)

    def body(*refs):
        if dres is None:
            x_ref, g_ref, dy_ref, dx_ref, dg_ref = refs
        else:
            x_ref, g_ref, dy_ref, dr_ref, dx_ref, dg_ref = refs
        dx, dgp = _rms_bwd_rows(x_ref[...].astype(F32), g_ref[...], dy_ref[...].astype(F32))
        if dres is not None:
            dx = dx + dr_ref[...]
        dx_ref[...] = dx

        @pl.when(pl.program_id(0) == 0)
        def _():
            dg_ref[...] = jnp.zeros_like(dg_ref)

        dg_ref[...] += jnp.sum(dgp, axis=0, keepdims=True)

    row = pl.BlockSpec((tm, D), lambda i: (i, 0))
    vec = pl.BlockSpec((1, D), lambda i: (0, 0))
    in_specs = [row, vec, row] + ([] if dres is None else [row])
    args = (x, g.reshape(1, D), dy) + (() if dres is None else (dres,))
    return _pcall(
        body, name=name, dims=("arbitrary",), grid=(M // tm,), in_specs=in_specs, out_specs=[row, vec],
        out_shape=[jax.ShapeDtypeStruct((M, D), F32), jax.ShapeDtypeStruct((1, D), F32)],
    )(*args)


def _rope_chunk(t, c, s, half):
    lane = _lane(t.shape)
    swapped = jnp.where((lane % (2 * half)) < half, pltpu.roll(t, LANES - half, 1), pltpu.roll(t, half, 1))
    return t * c + swapped * s


def _rope_tables(positions):
    pos = positions.reshape(-1).astype(F32)[:, None]
    S = pos.shape[0]

    def cs(dh):
        inv_freq = ROPE_THETA ** (-jnp.arange(0, dh, 2, dtype=F32) / dh)
        ang = pos * inv_freq
        return jnp.cos(ang), jnp.sin(ang)

    c64, s64 = cs(HEAD_DIM)
    c32, s32 = cs(MLA_ROPE)
    z32, z64, z96 = (jnp.zeros((S, n), F32) for n in (32, 64, 96))
    return dict(
        c64=jnp.concatenate([c64, c64, c64, c64], 1), s64=jnp.concatenate([-s64, s64, -s64, s64], 1),
        ck=jnp.concatenate([c32, c32, z96], 1), sk=jnp.concatenate([-s32, s32, z96], 1),
        cm=jnp.concatenate([jnp.ones((S, 64), F32), c32, c32, z32], 1),
        sm=jnp.concatenate([z64, -s32, s32, z32], 1),
    )


def _attn_steps(mode, n_other, t_self, t_other):
    if mode == "band":
        assert t_self == t_other
        return 2
    return n_other


def _kv_block(mode, qi, kj):
    if mode == "band":
        return jnp.maximum(qi - 1 + kj, 0), (qi + kj) >= 1
    if mode == "causal":
        return jnp.minimum(kj, qi), kj <= qi
    return kj, None


def _q_block(mode, ki, qj, nq):
    if mode == "band":
        return jnp.minimum(ki + qj, nq - 1), (ki + qj) <= nq - 1
    if mode == "causal":
        return jnp.maximum(qj, ki), qj >= ki
    return qj, None


def _mask(mode, max_dist, qpos, kpos):
    d = qpos - kpos
    if mode == "band":
        return (d >= 0) & (d <= max_dist)
    if mode == "causal":
        return d >= 0
    return None


def _when(cond, fn):
    if cond is None:
        fn()
    else:
        pl.when(cond)(fn)


class _Attn:
    def __init__(self, *, T, Tk, G, nh, rep, dqk, dv, tq, tk, mode, scale, qcol, kcol, vcol, ocol, o_width,
                 max_dist=0):
        self.__dict__.update(locals())
        self.nkv = nh // rep
        assert T % tq == 0 and Tk % tk == 0 and nh <= LANES


def _attn_fwd(cfg, q, k, v, *, name, sink=None, out_dtype=F32):
    c = cfg
    nq, nk = c.T // c.tq, c.Tk // c.tk
    steps = _attn_steps(c.mode, nk, c.tq, c.tk)

    def body(*refs):
        if sink is None:
            q_ref, k_ref, v_ref, o_ref, lse_ref, m_scr, l_scr, acc = refs
        else:
            q_ref, k_ref, v_ref, sink_ref, o_ref, lse_ref, m_scr, l_scr, acc = refs
        qi, kj = pl.program_id(1), pl.program_id(2)
        kb, valid = _kv_block(c.mode, qi, kj)

        @pl.when(kj == 0)
        def _():
            if sink is None:
                m_scr[...] = jnp.full_like(m_scr, NEG_BIG)
                l_scr[...] = jnp.zeros_like(l_scr)
            else:
                m_scr[...] = jnp.broadcast_to(sink_ref[...], m_scr.shape)
                l_scr[...] = jnp.ones_like(l_scr)
            acc[...] = jnp.zeros_like(acc)

        def step():
            qpos = qi * c.tq + lax.broadcasted_iota(jnp.int32, (c.tq, c.tk), 0)
            kpos = kb * c.tk + lax.broadcasted_iota(jnp.int32, (c.tq, c.tk), 1)
            mask = _mask(c.mode, c.max_dist, qpos, kpos)
            for j in range(c.nh):
                g = j // c.rep
                s = _dot(q_ref[:, j * c.dqk:(j + 1) * c.dqk], k_ref[:, g * c.dqk:(g + 1) * c.dqk], NT) * c.scale
                if mask is not None:
                    s = jnp.where(mask, s, -jnp.inf)
                m_prev = m_scr[:, j:j + 1]
                m_new = jnp.maximum(m_prev, jnp.max(s, axis=1, keepdims=True))
                alpha = jnp.exp(m_prev - m_new)
                p = jnp.exp(s - m_new)
                l_scr[:, j:j + 1] = alpha * l_scr[:, j:j + 1] + jnp.sum(p, axis=1, keepdims=True)
                acc[:, j * c.dv:(j + 1) * c.dv] = (
                    alpha * acc[:, j * c.dv:(j + 1) * c.dv] + _dot(p, v_ref[:, g * c.dv:(g + 1) * c.dv], NN))
                m_scr[:, j:j + 1] = m_new

        _when(valid, step)

        @pl.when(kj == steps - 1)
        def _():
            for j in range(c.nh):
                o_ref[:, j * c.dv:(j + 1) * c.dv] = (
                    acc[:, j * c.dv:(j + 1) * c.dv] / l_scr[:, j:j + 1]).astype(o_ref.dtype)
            lane = _lane((c.tq, LANES))
            lse_ref[...] = jnp.where(lane < c.nh, m_scr[...] + jnp.log(jnp.maximum(l_scr[...], 1e-37)), 0.0)

    in_specs = [
        pl.BlockSpec((c.tq, c.nh * c.dqk), lambda g, i, j: (i, c.qcol(g))),
        pl.BlockSpec((c.tk, c.nkv * c.dqk), lambda g, i, j: (_kv_block(c.mode, i, j)[0], c.kcol(g))),
        pl.BlockSpec((c.tk, c.nkv * c.dv), lambda g, i, j: (_kv_block(c.mode, i, j)[0], c.vcol(g))),
    ]
    args = [q, k, v]
    if sink is not None:
        in_specs.append(pl.BlockSpec((1, LANES), lambda g, i, j: (0, 0)))
        args.append(sink)
    return _pcall(
        body, name=name, dims=("parallel", "parallel", "arbitrary"), grid=(c.G, nq, steps),
        in_specs=in_specs,
        out_specs=[pl.BlockSpec((c.tq, c.nh * c.dv), lambda g, i, j: (i, c.ocol(g))),
                   pl.BlockSpec((c.tq, LANES), lambda g, i, j: (i, g))],
        out_shape=[jax.ShapeDtypeStruct((c.T, c.o_width), out_dtype),
                   jax.ShapeDtypeStruct((c.T, LANES * c.G), F32)],
        scratch_shapes=[pltpu.VMEM((c.tq, LANES), F32), pltpu.VMEM((c.tq, LANES), F32),
                        pltpu.VMEM((c.tq, c.nh * c.dv), F32)],
    )(*args)


def _attn_delta(cfg, o, do, *, name, w=None, lse=None, sink=None, tm=512):
    c = cfg
    tm = _tile(c.T, tm)
    width = c.nh * c.dv

    def body(*refs):
        refs = list(refs)
        o_ref, do_ref = refs[:2]
        rest = refs[2:]
        w_ref = rest.pop(0) if w is not None else None
        lse_ref, sink_ref = (rest.pop(0), rest.pop(0)) if sink is not None else (None, None)
        d_ref = rest.pop(0)
        prod = o_ref[...].astype(F32) * do_ref[...].astype(F32)
        cols = [jnp.sum(prod[:, j * c.dv:(j + 1) * c.dv], axis=1, keepdims=True) for j in range(c.nh)]
        delta = _cols_to_lanes(cols, tm)
        if w is not None:
            delta = delta * w_ref[...]
        d_ref[...] = delta
        if sink is not None:
            ds_ref = rest.pop(0)

            @pl.when(pl.program_id(1) == 0)
            def _():
                ds_ref[...] = jnp.zeros_like(ds_ref)

            lane = _lane((tm, LANES))
            ps = jnp.where(lane < c.nh, jnp.exp(sink_ref[...] - lse_ref[...]), 0.0)
            ds_ref[...] -= jnp.sum(ps * delta, axis=0, keepdims=True)

    stat = pl.BlockSpec((tm, LANES), lambda g, i: (i, g))
    in_specs = [pl.BlockSpec((tm, width), lambda g, i: (i, c.ocol(g)))] * 2
    args = [o, do]
    out_specs, out_shape = [stat], [jax.ShapeDtypeStruct((c.T, LANES * c.G), F32)]
    if w is not None:
        in_specs.append(stat)
        args.append(w)
    if sink is not None:
        assert c.G == 1
        in_specs += [stat, pl.BlockSpec((1, LANES), lambda g, i: (0, 0))]
        args += [lse, sink]
        out_specs.append(pl.BlockSpec((1, LANES), lambda g, i: (0, 0)))
        out_shape.append(jax.ShapeDtypeStruct((1, LANES), F32))
    out = _pcall(
        body, name=name, dims=("arbitrary", "arbitrary"), grid=(c.G, c.T // tm),
        in_specs=in_specs, out_specs=out_specs, out_shape=out_shape,
    )(*args)
    return out if sink is not None else (out[0], None)


def _attn_dq(cfg, q, k, v, do, lse, delta, *, name, init=None, out_dtype=F32):
    c = cfg
    nq, nk = c.T // c.tq, c.Tk // c.tk
    steps = _attn_steps(c.mode, nk, c.tq, c.tk)
    qw = c.nh * c.dqk

    def body(*refs):
        if init is None:
            q_ref, k_ref, v_ref, do_ref, lse_ref, d_ref, dq_ref, acc = refs
        else:
            q_ref, k_ref, v_ref, do_ref, lse_ref, d_ref, init_ref, dq_ref, acc = refs
        qi, kj = pl.program_id(1), pl.program_id(2)
        kb, valid = _kv_block(c.mode, qi, kj)

        @pl.when(kj == 0)
        def _():
            acc[...] = jnp.zeros_like(acc) if init is None else init_ref[...].astype(F32)

        def step():
            qpos = qi * c.tq + lax.broadcasted_iota(jnp.int32, (c.tq, c.tk), 0)
            kpos = kb * c.tk + lax.broadcasted_iota(jnp.int32, (c.tq, c.tk), 1)
            mask = _mask(c.mode, c.max_dist, qpos, kpos)
            for j in range(c.nh):
                g = j // c.rep
                kh = k_ref[:, g * c.dqk:(g + 1) * c.dqk]
                s = _dot(q_ref[:, j * c.dqk:(j + 1) * c.dqk], kh, NT) * c.scale
                if mask is not None:
                    s = jnp.where(mask, s, -jnp.inf)
                p = jnp.exp(s - lse_ref[:, j:j + 1])
                dp = _dot(do_ref[:, j * c.dv:(j + 1) * c.dv], v_ref[:, g * c.dv:(g + 1) * c.dv], NT)
                ds = p * (dp - d_ref[:, j:j + 1]) * c.scale
                acc[:, j * c.dqk:(j + 1) * c.dqk] += _dot(ds, kh, NN)

        _when(valid, step)

        @pl.when(kj == steps - 1)
        def _():
            dq_ref[...] = acc[...].astype(dq_ref.dtype)

    kvb = lambda i, j: _kv_block(c.mode, i, j)[0]
    qspec = pl.BlockSpec((c.tq, qw), lambda g, i, j: (i, c.qcol(g)))
    stat = pl.BlockSpec((c.tq, LANES), lambda g, i, j: (i, g))
    in_specs = [
        qspec,
        pl.BlockSpec((c.tk, c.nkv * c.dqk), lambda g, i, j: (kvb(i, j), c.kcol(g))),
        pl.BlockSpec((c.tk, c.nkv * c.dv), lambda g, i, j: (kvb(i, j), c.vcol(g))),
        pl.BlockSpec((c.tq, c.nh * c.dv), lambda g, i, j: (i, c.ocol(g))),
        stat, stat,
    ]
    args = [q, k, v, do, lse, delta]
    dq_spec = pl.BlockSpec((c.tq, qw), lambda g, i, j: (i, g))
    if init is not None:
        in_specs.append(dq_spec)
        args.append(init)
    return _pcall(
        body, name=name, dims=("parallel", "parallel", "arbitrary"), grid=(c.G, nq, steps),
        in_specs=in_specs, out_specs=dq_spec,
        out_shape=jax.ShapeDtypeStruct((c.T, c.G * qw), out_dtype),
        scratch_shapes=[pltpu.VMEM((c.tq, qw), F32)],
    )(*args)


def _attn_dkv(cfg, q, k, v, do, lse, delta, *, name, init=None, out_dtype=F32):
    c = cfg
    nq, nk = c.T // c.tq, c.Tk // c.tk
    steps = _attn_steps(c.mode, nq, c.tk, c.tq)
    kw, vw = c.nkv * c.dqk, c.nkv * c.dv

    def body(*refs):
        if init is None:
            q_ref, k_ref, v_ref, do_ref, lse_ref, d_ref, dk_ref, dv_ref, dk_acc, dv_acc = refs
        else:
            q_ref, k_ref, v_ref, do_ref, lse_ref, d_ref, ik_ref, iv_ref, dk_ref, dv_ref, dk_acc, dv_acc = refs
        ki, qj = pl.program_id(1), pl.program_id(2)
        qb, valid = _q_block(c.mode, ki, qj, nq)

        @pl.when(qj == 0)
        def _():
            dk_acc[...] = jnp.zeros_like(dk_acc) if init is None else ik_ref[...].astype(F32)
            dv_acc[...] = jnp.zeros_like(dv_acc) if init is None else iv_ref[...].astype(F32)

        def step():
            kpos = ki * c.tk + lax.broadcasted_iota(jnp.int32, (c.tk, c.tq), 0)
            qpos = qb * c.tq + lax.broadcasted_iota(jnp.int32, (c.tk, c.tq), 1)
            mask = _mask(c.mode, c.max_dist, qpos, kpos)
            lse_t = lse_ref[...].T
            d_t = d_ref[...].T
            for j in range(c.nh):
                g = j // c.rep
                qh = q_ref[:, j * c.dqk:(j + 1) * c.dqk]
                doh = do_ref[:, j * c.dv:(j + 1) * c.dv]
                s_t = _dot(k_ref[:, g * c.dqk:(g + 1) * c.dqk], qh, NT) * c.scale
                if mask is not None:
                    s_t = jnp.where(mask, s_t, -jnp.inf)
                p_t = jnp.exp(s_t - lse_t[j:j + 1, :])
                dv_acc[:, g * c.dv:(g + 1) * c.dv] += _dot(p_t, doh, NN)
                dp_t = _dot(v_ref[:, g * c.dv:(g + 1) * c.dv], doh, NT)
                ds_t = p_t * (dp_t - d_t[j:j + 1, :]) * c.scale
                dk_acc[:, g * c.dqk:(g + 1) * c.dqk] += _dot(ds_t, qh, NN)

        _when(valid, step)

        @pl.when(qj == steps - 1)
        def _():
            dk_ref[...] = dk_acc[...].astype(dk_ref.dtype)
            dv_ref[...] = dv_acc[...].astype(dv_ref.dtype)

    qbi = lambda i, j: _q_block(c.mode, i, j, nq)[0]
    stat = pl.BlockSpec((c.tq, LANES), lambda g, i, j: (qbi(i, j), g))
    in_specs = [
        pl.BlockSpec((c.tq, c.nh * c.dqk), lambda g, i, j: (qbi(i, j), c.qcol(g))),
        pl.BlockSpec((c.tk, kw), lambda g, i, j: (i, c.kcol(g))),
        pl.BlockSpec((c.tk, vw), lambda g, i, j: (i, c.vcol(g))),
        pl.BlockSpec((c.tq, c.nh * c.dv), lambda g, i, j: (qbi(i, j), c.ocol(g))),
        stat, stat,
    ]
    args = [q, k, v, do, lse, delta]
    dk_spec = pl.BlockSpec((c.tk, kw), lambda g, i, j: (i, g))
    dv_spec = pl.BlockSpec((c.tk, vw), lambda g, i, j: (i, g))
    if init is not None:
        in_specs += [dk_spec, dv_spec]
        args += list(init)
    return _pcall(
        body, name=name, dims=("parallel", "parallel", "arbitrary"), grid=(c.G, nk, steps),
        in_specs=in_specs, out_specs=[dk_spec, dv_spec],
        out_shape=[jax.ShapeDtypeStruct((c.Tk, c.G * kw), out_dtype),
                   jax.ShapeDtypeStruct((c.Tk, c.G * vw), out_dtype)],
        scratch_shapes=[pltpu.VMEM((c.tk, kw), F32), pltpu.VMEM((c.tk, vw), F32)],
    )(*args)


TN = (((0,), (0,)), ((), ()))


def _band_mask(c, i):
    key = lax.broadcasted_iota(jnp.int32, (2 * BLOCK, BLOCK), 0)
    qry = lax.broadcasted_iota(jnp.int32, (2 * BLOCK, BLOCK), 1)
    d = BLOCK + qry - key
    return (d >= 0) & (d <= c.max_dist) & ((key >= BLOCK) | (i > 0))


def _head_pairs(c):
    return c.rep == 1 and c.dqk == c.dv == LANES // 2 and c.nh % 2 == 0


def _block_diagonal(pair):
    lane = _lane(pair.shape)
    zero = jnp.zeros_like(pair)
    return jnp.concatenate([jnp.where(lane < LANES // 2, pair, zero), jnp.where(lane >= LANES // 2, pair, zero)], axis=0)


def _own_blocks(t):
    n = t.shape[1] // 2
    rows = lax.broadcasted_iota(jnp.int32, (LANES, n), 0)
    return jnp.where(rows < LANES // 2, t[:, :n], t[:, n:])


def _rows_to_stats(rows, n):
    return jnp.concatenate(rows + [jnp.zeros((LANES - len(rows), n), F32)], axis=0).T


def _band_fwd(cfg, q, k, v, *, name, sink=None, out_dtype=F32):
    c = cfg
    assert c.mode == "band" and c.tq == c.tk == BLOCK and c.T == c.Tk
    nq = c.T // BLOCK

    def body(*refs):
        if sink is None:
            q_ref, kp_ref, kc_ref, vp_ref, vc_ref, o_ref, lse_ref = refs
        else:
            q_ref, kp_ref, kc_ref, vp_ref, vc_ref, sink_ref, o_ref, lse_ref = refs
        mask = _band_mask(c, pl.program_id(1))
        k2 = jnp.concatenate([kp_ref[...], kc_ref[...]], axis=0)
        v2 = jnp.concatenate([vp_ref[...], vc_ref[...]], axis=0)
        lses = []
        if _head_pairs(c):
            mask2 = jnp.concatenate([mask, mask], axis=1)
            pair_lanes = [slice(pc * LANES, (pc + 1) * LANES) for pc in range(c.nh // 2)]
            score = lambda sl: _dot(k2[:, sl], _block_diagonal(q_ref[:, sl]), NT)
            ahead, behind = score(pair_lanes[0]), None

            def finish(entry):
                sl, o_t, l = entry
                o_ref[:, sl] = _own_blocks(o_t / l).T.astype(o_ref.dtype)

            for pc, sl in enumerate(pair_lanes):
                s = ahead * c.scale
                if pc + 1 < len(pair_lanes):
                    ahead = score(pair_lanes[pc + 1])
                s = jnp.where(mask2, s, -jnp.inf)
                m = jnp.max(s, axis=0, keepdims=True)
                p = jnp.exp(s - m)
                l = jnp.sum(p, axis=0, keepdims=True)
                if behind is not None:
                    finish(behind)
                behind = (sl, _dot(v2[:, sl], p, TN), l)
                lse = m + jnp.log(l)
                lses += [lse[:, :BLOCK], lse[:, BLOCK:]]
            finish(behind)
        heads = [] if _head_pairs(c) else list(range(c.nh))
        score_of = lambda j: _dot(k2[:, (j // c.rep) * c.dqk:(j // c.rep + 1) * c.dqk],
                                  q_ref[:, j * c.dqk:(j + 1) * c.dqk], NT)
        ahead = score_of(0) if heads else None
        for j in heads:
            g = j // c.rep
            s = ahead * c.scale
            if j + 1 < c.nh:
                ahead = score_of(j + 1)
            s = jnp.where(mask, s, -jnp.inf)
            m = jnp.max(s, axis=0, keepdims=True)
            if sink is not None:
                sk = sink_ref[:, j:j + 1]
                m = jnp.maximum(m, sk)
            p = jnp.exp(s - m)
            l = jnp.sum(p, axis=0, keepdims=True)
            if sink is not None:
                l = l + jnp.exp(sk - m)
            o_t = _dot(v2[:, g * c.dv:(g + 1) * c.dv], p, TN)
            o_ref[:, j * c.dv:(j + 1) * c.dv] = (o_t / l).T.astype(o_ref.dtype)
            lses.append(m + jnp.log(l))
        lse_ref[...] = _rows_to_stats(lses, BLOCK)

    prev = lambda i: jnp.maximum(i - 1, 0)
    kw, vw = c.nkv * c.dqk, c.nkv * c.dv
    in_specs = [
        pl.BlockSpec((BLOCK, c.nh * c.dqk), lambda g, i: (i, c.qcol(g))),
        pl.BlockSpec((BLOCK, kw), lambda g, i: (prev(i), c.kcol(g))),
        pl.BlockSpec((BLOCK, kw), lambda g, i: (i, c.kcol(g))),
        pl.BlockSpec((BLOCK, vw), lambda g, i: (prev(i), c.vcol(g))),
        pl.BlockSpec((BLOCK, vw), lambda g, i: (i, c.vcol(g))),
    ]
    args = [q, k, k, v, v]
    if sink is not None:
        in_specs.append(pl.BlockSpec((1, LANES), lambda g, i: (0, 0)))
        args.append(sink)
    return _pcall(
        body, name=name, dims=("parallel", "parallel"), grid=(c.G, nq), in_specs=in_specs,
        out_specs=[pl.BlockSpec((BLOCK, c.nh * c.dv), lambda g, i: (i, c.ocol(g))),
                   pl.BlockSpec((BLOCK, LANES), lambda g, i: (i, g))],
        out_shape=[jax.ShapeDtypeStruct((c.T, c.o_width), out_dtype),
                   jax.ShapeDtypeStruct((c.T, LANES * c.G), F32)],
    )(*args)


def _band_bwd(cfg, q, k, v, do, lse, delta, *, name):
    c = cfg
    assert c.mode == "band" and c.tq == c.tk == BLOCK and c.T == c.Tk
    nq = c.T // BLOCK
    qw, kw, vw = c.nh * c.dqk, c.nkv * c.dqk, c.nkv * c.dv

    def body(q_ref, kp_ref, kc_ref, vp_ref, vc_ref, do_ref, lse_ref, d_ref, dq_ref, dk_ref, dv_ref, dk_c, dv_c):
        n = pl.program_id(1)

        @pl.when(n == 0)
        def _():
            dk_c[...] = jnp.zeros_like(dk_c)
            dv_c[...] = jnp.zeros_like(dv_c)

        @pl.when(n < nq)
        def _():
            mask = _band_mask(c, n)
            k2 = jnp.concatenate([kp_ref[...], kc_ref[...]], axis=0)
            v2 = jnp.concatenate([vp_ref[...], vc_ref[...]], axis=0)
            lse_t, d_t = lse_ref[...].T, d_ref[...].T
            if _head_pairs(c):
                mask2 = jnp.concatenate([mask, mask], axis=1)
                pair_lanes = [slice(pc * LANES, (pc + 1) * LANES) for pc in range(c.nh // 2)]

                def first(sl):
                    q_bd, do_bd = _block_diagonal(q_ref[:, sl]), _block_diagonal(do_ref[:, sl])
                    return q_bd, do_bd, k2[:, sl], _dot(k2[:, sl], q_bd, NT), _dot(v2[:, sl], do_bd, NT)

                def finish(entry):
                    sl, dq_t, dv_pair, dk_pair = entry
                    dq_ref[:, sl] = _own_blocks(dq_t).T
                    dk_ref[:, sl] = dk_c[:, sl] + dk_pair[:BLOCK]
                    dv_ref[:, sl] = dv_c[:, sl] + dv_pair[:BLOCK]
                    dk_c[:, sl] = dk_pair[BLOCK:]
                    dv_c[:, sl] = dv_pair[BLOCK:]

                ahead, behind = first(pair_lanes[0]), None
                for pc, sl in enumerate(pair_lanes):
                    q_bd, do_bd, kp, s, dp = ahead
                    if pc + 1 < len(pair_lanes):
                        ahead = first(pair_lanes[pc + 1])
                    both = lambda t: jnp.concatenate([t[2 * pc:2 * pc + 1, :], t[2 * pc + 1:2 * pc + 2, :]], axis=1)
                    p = jnp.exp(jnp.where(mask2, s * c.scale, -jnp.inf) - both(lse_t))
                    ds = p * (dp - both(d_t)) * c.scale
                    entry = (sl, _dot(kp, ds, TN), _dot(p, do_bd, NN), _dot(ds, q_bd, NN))
                    if behind is not None:
                        finish(behind)
                    behind = entry
                finish(behind)
                return
            dk2, dv2 = [None] * c.nkv, [None] * c.nkv

            def first_of(j):
                g = j // c.rep
                qh, doh = q_ref[:, j * c.dqk:(j + 1) * c.dqk], do_ref[:, j * c.dv:(j + 1) * c.dv]
                kh = k2[:, g * c.dqk:(g + 1) * c.dqk]
                return qh, doh, kh, _dot(kh, qh, NT), _dot(v2[:, g * c.dv:(g + 1) * c.dv], doh, NT)

            ahead = first_of(0)
            for j in range(c.nh):
                g = j // c.rep
                qh, doh, kh, s, dp = ahead
                if j + 1 < c.nh:
                    ahead = first_of(j + 1)
                p = jnp.exp(jnp.where(mask, s * c.scale, -jnp.inf) - lse_t[j:j + 1, :])
                ds = p * (dp - d_t[j:j + 1, :]) * c.scale
                dq_ref[:, j * c.dqk:(j + 1) * c.dqk] = _dot(kh, ds, TN).T
                dvh, dkh = _dot(p, doh, NN), _dot(ds, qh, NN)
                dv2[g] = dvh if dv2[g] is None else dv2[g] + dvh
                dk2[g] = dkh if dk2[g] is None else dk2[g] + dkh
            for g in range(c.nkv):
                ks, vs = slice(g * c.dqk, (g + 1) * c.dqk), slice(g * c.dv, (g + 1) * c.dv)
                dk_ref[:, ks] = dk_c[:, ks] + dk2[g][:BLOCK]
                dv_ref[:, vs] = dv_c[:, vs] + dv2[g][:BLOCK]
                dk_c[:, ks] = dk2[g][BLOCK:]
                dv_c[:, vs] = dv2[g][BLOCK:]

        @pl.when(n == nq)
        def _():
            dk_ref[...] = dk_c[...]
            dv_ref[...] = dv_c[...]

    cur = lambda n: jnp.minimum(n, nq - 1)
    prev = lambda n: jnp.maximum(cur(n) - 1, 0)
    out_blk = lambda n: jnp.maximum(n - 1, 0)
    stat = pl.BlockSpec((BLOCK, LANES), lambda g, n: (cur(n), g))
    dq_spec = pl.BlockSpec((BLOCK, qw), lambda g, n: (cur(n), g))
    dk_spec = pl.BlockSpec((BLOCK, kw), lambda g, n: (out_blk(n), g))
    dv_spec = pl.BlockSpec((BLOCK, vw), lambda g, n: (out_blk(n), g))
    in_specs = [
        pl.BlockSpec((BLOCK, qw), lambda g, n: (cur(n), c.qcol(g))),
        pl.BlockSpec((BLOCK, kw), lambda g, n: (prev(n), c.kcol(g))),
        pl.BlockSpec((BLOCK, kw), lambda g, n: (cur(n), c.kcol(g))),
        pl.BlockSpec((BLOCK, vw), lambda g, n: (prev(n), c.vcol(g))),
        pl.BlockSpec((BLOCK, vw), lambda g, n: (cur(n), c.vcol(g))),
        pl.BlockSpec((BLOCK, c.nh * c.dv), lambda g, n: (cur(n), c.ocol(g))),
        stat, stat,
    ]
    return _pcall(
        body, name=name, dims=("parallel", "arbitrary"), grid=(c.G, nq + 1), in_specs=in_specs,
        out_specs=[dq_spec, dk_spec, dv_spec],
        out_shape=[_sds((c.T, c.G * qw)), _sds((c.T, c.G * kw)), _sds((c.T, c.G * vw))],
        scratch_shapes=[pltpu.VMEM((BLOCK, kw), F32), pltpu.VMEM((BLOCK, vw), F32)],
    )(q, k, k, v, v, do, lse, delta)


def _causal_pairs(n, kv_major):
    pairs =[(i, j) for j in range(n) for i in range(j, n)] if kv_major else [(i, j) for i in range(n) for j in range(i + 1)]
    return jnp.asarray(np.array([p[0] for p in pairs], np.int32)), jnp.asarray(np.array([p[1] for p in pairs], np.int32))


def _causal_mask(t):
    return lax.broadcasted_iota(jnp.int32, (t, t), 0) >= lax.broadcasted_iota(jnp.int32, (t, t), 1)


def _carrying(body, n_in, n_out, n_scratch, grid, carry):
    if carry is None:
        return body
    G, P = grid

    def wrapped(*refs):
        refs = list(refs)
        prefetch, refs = refs[:2], refs[2:]
        ins, src = refs[:n_in], refs[n_in]
        outs, out = refs[n_in + 1:n_in + 1 + n_out], refs[n_in + 1 + n_out]
        scratch, sems = refs[n_in + 2 + n_out:n_in + 2 + n_out + n_scratch], refs[n_in + 2 + n_out + n_scratch:]
        step = pl.program_id(0) * P + pl.program_id(1)
        carry.run([src, out] + sems, step, G * P, at_end=False)
        body(*prefetch, *ins, *outs, *scratch)
        carry.run([src, out] + sems, step, G * P, at_end=True)

    return wrapped


def _carry_specs(carry):
    if carry is None:
        return [], [], [], [], []
    any_space = pl.BlockSpec(memory_space=pl.ANY)
    return [any_space], [any_space], [carry.out_shape], list(carry.sems), [carry.src]


def _causal_fwd(cfg, q, k, v, *, name, out_dtype=F32, carry=None):
    c = cfg
    assert c.mode == "causal" and c.tq == c.tk and c.T == c.Tk
    t, n = c.tq, c.T // c.tq
    qi_tab, kj_tab = _causal_pairs(n, kv_major=False)
    n_pairs = int(qi_tab.shape[0])

    def body(qi_ref, kj_ref, q_ref, k_ref, v_ref, o_ref, lse_ref, m_scr, l_scr, acc):
        pair = pl.program_id(1)
        qi, kj = qi_ref[pair], kj_ref[pair]

        @pl.when(kj == 0)
        def _():
            m_scr[...] = jnp.full_like(m_scr, NEG_BIG)
            l_scr[...] = jnp.zeros_like(l_scr)
            acc[...] = jnp.zeros_like(acc)

        def step(diagonal):
            mask = None
            if diagonal:
                mask = lax.broadcasted_iota(jnp.int32, (t, t), 1) >= lax.broadcasted_iota(jnp.int32, (t, t), 0)
            scores = [_dot(k_ref[:, (j // c.rep) * c.dqk:(j // c.rep + 1) * c.dqk],
                           q_ref[:, j * c.dqk:(j + 1) * c.dqk], NT) for j in range(c.nh)]
            for j in range(c.nh):
                g = j // c.rep
                s = scores[j] * c.scale
                if diagonal:
                    s = jnp.where(mask, s, -jnp.inf)
                m_prev = m_scr[j]
                m_new = jnp.maximum(m_prev, jnp.max(s, axis=0, keepdims=True))
                alpha = jnp.exp(m_prev - m_new)
                p = jnp.exp(s - m_new)
                l_scr[j] = alpha * l_scr[j] + jnp.sum(p, axis=0, keepdims=True)
                acc[j] = alpha * acc[j] + _dot(v_ref[:, g * c.dv:(g + 1) * c.dv], p, TN)
                m_scr[j] = m_new

        pl.when(kj == qi)(lambda: step(True))
        pl.when(kj != qi)(lambda: step(False))

        @pl.when(kj == qi)
        def _():
            rows = []
            for j in range(c.nh):
                o_ref[:, j * c.dv:(j + 1) * c.dv] = (acc[j] / l_scr[j]).T.astype(o_ref.dtype)
                rows.append(m_scr[j] + jnp.log(l_scr[j]))
            rows.append(jnp.zeros((LANES - c.nh, t), F32))
            lse_ref[...] = jnp.concatenate(rows, axis=0).T

    x_in, x_out, x_shapes, x_scratch, x_args = _carry_specs(carry)
    grid_spec = pltpu.PrefetchScalarGridSpec(
        num_scalar_prefetch=2, grid=(c.G, n_pairs),
        in_specs=[pl.BlockSpec((t, c.nh * c.dqk), lambda g, p, qi, kj: (qi[p], c.qcol(g))),
                  pl.BlockSpec((t, c.nkv * c.dqk), lambda g, p, qi, kj: (kj[p], c.kcol(g))),
                  pl.BlockSpec((t, c.nkv * c.dv), lambda g, p, qi, kj: (kj[p], c.vcol(g)))] + x_in,
        out_specs=[pl.BlockSpec((t, c.nh * c.dv), lambda g, p, qi, kj: (qi[p], c.ocol(g))),
                   pl.BlockSpec((t, LANES), lambda g, p, qi, kj: (qi[p], g))] + x_out,
        scratch_shapes=[pltpu.VMEM((c.nh, 1, t), F32), pltpu.VMEM((c.nh, 1, t), F32),
                        pltpu.VMEM((c.nh, c.dv, t), F32)] + x_scratch)
    return _pcall(
        _carrying(body, 3, 2, 3, (c.G, n_pairs), carry), name=name,
        dims=("arbitrary", "arbitrary") if carry is not None else ("parallel", "arbitrary"), grid_spec=grid_spec,
        out_shape=[jax.ShapeDtypeStruct((c.T, c.o_width), out_dtype),
                   jax.ShapeDtypeStruct((c.T, LANES * c.G), F32)] + x_shapes,
    )(qi_tab, kj_tab, q, k, v, *x_args)


def _causal_bwd(cfg, q, k, v, o, do, lse, *, name, carry=None):
    c = cfg
    assert c.mode == "causal" and c.tq == c.tk and c.T == c.Tk
    t, n = c.tq, c.T // c.tq
    qw, kw, vw = c.nh * c.dqk, c.nkv * c.dqk, c.nkv * c.dv
    qi_tab, kj_tab = _causal_pairs(n, kv_major=True)

    def body(qi_ref, kj_ref, q_ref, k_ref, v_ref, o_ref, do_ref, lse_ref, dq_ref, dk_ref, dv_ref, dk_acc, dv_acc):
        pair = pl.program_id(1)
        qi, kj = qi_ref[pair], kj_ref[pair]

        @pl.when(pair == 0)
        def _():
            dq_ref[...] = jnp.zeros_like(dq_ref)

        @pl.when(qi == kj)
        def _():
            dk_acc[...] = jnp.zeros_like(dk_acc)
            dv_acc[...] = jnp.zeros_like(dv_acc)

        rows = pl.ds(pl.multiple_of(qi * t, t), t)

        def step(diagonal):
            mask = _causal_mask(t) if diagonal else None
            for j in range(c.nh):
                g = j // c.rep
                qs, ks, vs = (slice(j * c.dqk, (j + 1) * c.dqk), slice(g * c.dqk, (g + 1) * c.dqk),
                              slice(g * c.dv, (g + 1) * c.dv))
                qh, doh, kh = q_ref[:, qs], do_ref[:, j * c.dv:(j + 1) * c.dv], k_ref[:, ks]
                s = _dot(qh, kh, NT) * c.scale
                if diagonal:
                    s = jnp.where(mask, s, -jnp.inf)
                p = jnp.exp(s - lse_ref[:, j:j + 1])
                delta = jnp.sum(doh.astype(F32) * o_ref[:, j * c.dv:(j + 1) * c.dv].astype(F32), axis=1, keepdims=True)
                ds = p * (_dot(doh, v_ref[:, vs], NT) - delta) * c.scale
                dq_ref[rows, qs] += _dot(ds, kh, NN)
                dv_acc[g] += _dot(doh, p, TN)
                dk_acc[g] += _dot(qh, ds, TN)

        pl.when(qi == kj)(lambda: step(True))
        pl.when(qi != kj)(lambda: step(False))

        @pl.when(qi == n - 1)
        def _():
            for g in range(c.nkv):
                dk_ref[:, g * c.dqk:(g + 1) * c.dqk] = dk_acc[g].T
                dv_ref[:, g * c.dv:(g + 1) * c.dv] = dv_acc[g].T

    stat = pl.BlockSpec((t, LANES), lambda g, p, qi, kj: (qi[p], g))
    o_spec = pl.BlockSpec((t, c.nh * c.dv), lambda g, p, qi, kj: (qi[p], c.ocol(g)))
    n_pairs = int(qi_tab.shape[0])
    x_in, x_out, x_shapes, x_scratch, x_args = _carry_specs(carry)
    grid_spec = pltpu.PrefetchScalarGridSpec(
        num_scalar_prefetch=2, grid=(c.G, n_pairs),
        in_specs=[pl.BlockSpec((t, qw), lambda g, p, qi, kj: (qi[p], c.qcol(g))),
                  pl.BlockSpec((t, kw), lambda g, p, qi, kj: (kj[p], c.kcol(g))),
                  pl.BlockSpec((t, vw), lambda g, p, qi, kj: (kj[p], c.vcol(g))),
                  o_spec, o_spec, stat] + x_in,
        out_specs=[pl.BlockSpec((c.T, qw), lambda g, p, qi, kj: (0, g)),
                   pl.BlockSpec((t, kw), lambda g, p, qi, kj: (kj[p], g)),
                   pl.BlockSpec((t, vw), lambda g, p, qi, kj: (kj[p], g))] + x_out,
        scratch_shapes=[pltpu.VMEM((c.nkv, c.dqk, t), F32), pltpu.VMEM((c.nkv, c.dv, t), F32)] + x_scratch)
    return _pcall(
        _carrying(body, 6, 3, 2, (c.G, n_pairs), carry), name=name,
        dims=("arbitrary", "arbitrary") if carry is not None else ("parallel", "arbitrary"), grid_spec=grid_spec,
        out_shape=[_sds((c.T, c.G * qw)), _sds((c.T, c.G * kw)), _sds((c.T, c.G * vw))] + x_shapes,
    )(qi_tab, kj_tab, q, k, v, o, do, lse, *x_args)


def _rowwise(body, ins, outs, *, name, rows, tm=512, accs=(), scratch=()):
    tm = _row_tile(rows, tm)

    def spec(a):
        if a.shape[0] == 1:
            return pl.BlockSpec((1, a.shape[1]), lambda i: (0, 0))
        d = rows // a.shape[0]
        assert d * a.shape[0] == rows and tm % d == 0
        return pl.BlockSpec((tm // d, a.shape[1]), lambda i: (i, 0))

    return _pcall(
        functools.partial(body, tm), name=name, dims=("arbitrary" if accs else "parallel",), grid=(rows // tm,),
        in_specs=[spec(a) for a in ins], out_specs=[spec(a) for a in outs], out_shape=list(outs),
        scratch_shapes=list(scratch),
    )(*ins)


def _sds(shape, dtype=F32):
    return jax.ShapeDtypeStruct(shape, dtype)


def _acc_rows(ref, val):
    @pl.when(pl.program_id(0) == 0)
    def _():
        ref[...] = jnp.zeros_like(ref)

    ref[...] += jnp.sum(val, axis=0, keepdims=True)


Z_QA, Z_KA, Z_VA, Z_CQ, Z_CKV, Z_KR, Z_END = 0, 512, 640, 768, 1152, 1408, 1536


def _l0_prep(z, tabs, q_norm, kv_norm, *, name):
    S = z.shape[0]

    def body(tm, z_ref, c64, s64, ck, sk, gq, gkv, qa_o, ka_o, va_o, cq_o, ckv_o, kr_o):
        for i in range(4):
            sl = slice(Z_QA + i * LANES, Z_QA + (i + 1) * LANES)
            qa_o[:, i * LANES:(i + 1) * LANES] = _rope_chunk(z_ref[:, sl], c64[...], s64[...], 32).astype(qa_o.dtype)
        ka_o[...] = _rope_chunk(z_ref[:, Z_KA:Z_VA], c64[...], s64[...], 32).astype(ka_o.dtype)
        va_o[...] = z_ref[:, Z_VA:Z_CQ].astype(va_o.dtype)
        cq_o[...] = (_rms_parts(z_ref[:, Z_CQ:Z_CKV])[0] * gq[...]).astype(cq_o.dtype)
        ckv_o[...] = (_rms_parts(z_ref[:, Z_CKV:Z_KR])[0] * gkv[...]).astype(ckv_o.dtype)
        kr_o[...] = _rope_chunk(z_ref[:, Z_KR:Z_END], ck[...], sk[...], 16)

    outs = [_sds((S, 512), MXU_DTYPE), _sds((S, 128), MXU_DTYPE), _sds((S, 128), MXU_DTYPE),
            _sds((S, MLA_Q_RANK), MXU_DTYPE), _sds((S, MLA_KV_RANK), MXU_DTYPE), _sds((S, LANES))]
    ins = [z, tabs["c64"], tabs["s64"], tabs["ck"], tabs["sk"], q_norm.reshape(1, -1), kv_norm.reshape(1, -1)]
    return _rowwise(body, ins, outs, name=name, rows=S)


def _l0_prep_bwd(z, tabs, q_norm, kv_norm, dqa, dka, dva, dcq, dckv, dkr, *, name):
    S = z.shape[0]

    def body(tm, z_ref, c64, s64, ck, sk, gq, gkv, dqa_r, dka_r, dva_r, dcq_r, dckv_r, dkr_r, dz_o, dgq_o, dgkv_o):
        for i in range(4):
            sl = slice(i * LANES, (i + 1) * LANES)
            dz_o[:, sl] = _rope_chunk(dqa_r[:, sl].astype(F32), c64[...], -s64[...], 32).astype(dz_o.dtype)
        dz_o[:, Z_KA:Z_VA] = _rope_chunk(dka_r[...].astype(F32), c64[...], -s64[...], 32).astype(dz_o.dtype)
        dz_o[:, Z_VA:Z_CQ] = dva_r[...].astype(dz_o.dtype)
        dx, dgp = _rms_bwd_rows(z_ref[:, Z_CQ:Z_CKV], gq[...], dcq_r[...].astype(F32))
        dz_o[:, Z_CQ:Z_CKV] = dx.astype(dz_o.dtype)
        _acc_rows(dgq_o, dgp)
        dx, dgp = _rms_bwd_rows(z_ref[:, Z_CKV:Z_KR], gkv[...], dckv_r[...].astype(F32))
        dz_o[:, Z_CKV:Z_KR] = dx.astype(dz_o.dtype)
        _acc_rows(dgkv_o, dgp)
        dz_o[:, Z_KR:Z_END] = _rope_chunk(dkr_r[...], ck[...], -sk[...], 16).astype(dz_o.dtype)

    outs = [_sds((S, Z_END), MXU_DTYPE), _sds((1, MLA_Q_RANK)), _sds((1, MLA_KV_RANK))]
    ins = [z, tabs["c64"], tabs["s64"], tabs["ck"], tabs["sk"], q_norm.reshape(1, -1), kv_norm.reshape(1, -1),
           dqa, dka, dva, dcq, dckv, dkr]
    return _rowwise(body, ins, outs, name=name, rows=S, accs=(1, 2))


def _mla_prep(qb, kvb, kr, tabs, *, name):
    S = qb.shape[0]

    def body(tm, qb_r, kvb_r, kr_r, cm, sm, q_o, k_o, v_o):
        lane = _lane((tm, LANES))
        kr_at_64 = pltpu.roll(kr_r[...], 64, 1)
        for h in range(MLA_HEADS):
            sl = slice(h * LANES, (h + 1) * LANES)
            q_o[:, sl] = _rope_chunk(qb_r[:, sl], cm[...], sm[...], 16).astype(q_o.dtype)
            k_o[:, sl] = jnp.where(lane < 64, kvb_r[:, sl], kr_at_64).astype(k_o.dtype)
        for p in range(MLA_HEADS // 2):
            even = pltpu.roll(kvb_r[:, (2 * p) * LANES:(2 * p + 1) * LANES], 64, 1)
            odd = kvb_r[:, (2 * p + 1) * LANES:(2 * p + 2) * LANES]
            v_o[:, p * LANES:(p + 1) * LANES] = jnp.where(lane < 64, even, odd).astype(v_o.dtype)

    outs = [_sds((S, 1024), MXU_DTYPE), _sds((S, 1024), MXU_DTYPE), _sds((S, 512), MXU_DTYPE)]
    return _rowwise(body, [qb, kvb, kr, tabs["cm"], tabs["sm"]], outs, name=name, rows=S)


def _mla_prep_bwd(dq, dk, dv, tabs, *, name):
    S = dq.shape[0]

    def body(tm, dq_r, dk_r, dv_r, cm, sm, dqb_o, dkvb_o, dkr_o):
        lane = _lane((tm, LANES))
        dkr = jnp.zeros((tm, LANES), F32)
        for h in range(MLA_HEADS):
            sl = slice(h * LANES, (h + 1) * LANES)
            dqb_o[:, sl] = _rope_chunk(dq_r[:, sl].astype(F32), cm[...], -sm[...], 16).astype(dqb_o.dtype)
            dkh = dk_r[:, sl].astype(F32)
            dvp = dv_r[:, (h // 2) * LANES:(h // 2 + 1) * LANES].astype(F32)
            dvh = pltpu.roll(dvp, 64, 1) if h % 2 == 0 else dvp
            dkvb_o[:, sl] = jnp.where(lane < 64, dkh, dvh).astype(dkvb_o.dtype)
            dkr = dkr + pltpu.roll(dkh, 64, 1)
        dkr_o[...] = jnp.where(lane < MLA_ROPE, dkr, 0.0)

    outs = [_sds((S, 1024), MXU_DTYPE), _sds((S, 1024), MXU_DTYPE), _sds((S, LANES))]
    return _rowwise(body, [dq, dk, dv, tabs["cm"], tabs["sm"]], outs, name=name, rows=S)


DILATIONS = tuple(d for _, d in DIL_PATTERNS)
QKV_CHUNKS = 8


def _to_branch(nat, c0, chunks, out_ref, d, rows):
    width = chunks * LANES
    for r in range(d):
        tok = pl.ds(r, rows // d, stride=d) if d > 1 else slice(None)
        for c in range(chunks):
            out_ref[:, r * width + c * LANES:r * width + (c + 1) * LANES] = nat[c0 + c, tok, :].astype(out_ref.dtype)


def _from_branch(in_ref, nat, c0, chunks, d, rows, add=False):
    width = chunks * LANES
    for r in range(d):
        tok = pl.ds(r, rows // d, stride=d) if d > 1 else slice(None)
        for c in range(chunks):
            val = in_ref[:, r * width + c * LANES:r * width + (c + 1) * LANES].astype(F32)
            nat[c0 + c, tok, :] = nat[c0 + c, tok, :] + val if add else val


def _branch_sds(S, width, d, dtype):
    return _sds((S // d, d * width), dtype)


def _l1_prep(qkv, tabs, *, name):
    S = qkv.shape[0]

    def body(tm, x_r, c64, s64, *rest):
        outs, nat = rest[:-1], rest[-1]
        for i in range(QKV_CHUNKS):
            sl = slice(i * LANES, (i + 1) * LANES)
            nat[i] = _rope_chunk(x_r[:, sl], c64[...], s64[...], 32)
            nat[QKV_CHUNKS + i] = _rope_chunk(x_r[:, 1024 + i * LANES:1024 + (i + 1) * LANES], c64[...], s64[...], 32)
            nat[2 * QKV_CHUNKS + i] = x_r[:, 2048 + i * LANES:2048 + (i + 1) * LANES]
        for b, d in enumerate(DILATIONS):
            for t in range(3):
                _to_branch(nat, t * QKV_CHUNKS, QKV_CHUNKS, outs[3 * b + t], d, tm)

    outs = [_branch_sds(S, 1024, d, MXU_DTYPE) for d in DILATIONS for _ in range(3)]
    got = _rowwise(body, [qkv, tabs["c64"], tabs["s64"]], outs, name=name, rows=S,
                   scratch=[pltpu.VMEM((3 * QKV_CHUNKS, _row_tile(S, 512), LANES), F32)])
    return {d: tuple(got[3 * b:3 * b + 3]) for b, d in enumerate(DILATIONS)}


def _l1_prep_bwd(grads, tabs, *, name):
    S = grads[1][0].shape[0]

    def body(tm, *rest):
        ins, (c64, s64, o, nat) = rest[:9], rest[9:]
        for b, d in enumerate(DILATIONS):
            for t in range(3):
                _from_branch(ins[3 * b + t], nat, t * QKV_CHUNKS, QKV_CHUNKS, d, tm, add=b > 0)
        for i in range(QKV_CHUNKS):
            sl = slice(i * LANES, (i + 1) * LANES)
            o[:, sl] = _rope_chunk(nat[i], c64[...], -s64[...], 32).astype(o.dtype)
            o[:, 1024 + i * LANES:1024 + (i + 1) * LANES] = _rope_chunk(
                nat[QKV_CHUNKS + i], c64[...], -s64[...], 32).astype(o.dtype)
            o[:, 2048 + i * LANES:2048 + (i + 1) * LANES] = nat[2 * QKV_CHUNKS + i].astype(o.dtype)

    ins = [g for d in DILATIONS for g in grads[d]] + [tabs["c64"], tabs["s64"]]
    return _rowwise(body, ins, [_sds((S, 3072), MXU_DTYPE)], name=name, rows=S, tm=256,
                    scratch=[pltpu.VMEM((3 * QKV_CHUNKS, _row_tile(S, 256), LANES), F32)])[0]


def _sigmoid(x):
    return 1.0 / (1.0 + jnp.exp(-x))


FFN_ROW_TILE, FFN_COL_TILE = 512, 1408


def _gate_up(h, w_gate, w_up, *, name):
    (M, K), N = h.shape, w_gate.shape[1]
    tm, tn = _tile(M, FFN_ROW_TILE), _tile(N, FFN_COL_TILE)

    def body(h_ref, wg_ref, wu_ref, g_ref, u_ref, a_ref):
        g = _dot(h_ref[...], wg_ref[...], NN)
        u = _dot(h_ref[...], wu_ref[...], NN)
        g_ref[...] = g
        u_ref[...] = u
        a_ref[...] = (g * _sigmoid(g) * u).astype(a_ref.dtype)

    w_spec = pl.BlockSpec((K, tn), lambda j, i: (0, j))
    o_spec = pl.BlockSpec((tm, tn), lambda j, i: (i, j))
    return _pcall(
        body, name=name, dims=("parallel", "parallel"), grid=(N // tn, M // tm),
        in_specs=[pl.BlockSpec((tm, K), lambda j, i: (i, 0)), w_spec, w_spec], out_specs=[o_spec] * 3,
        out_shape=[_sds((M, N)), _sds((M, N)), _sds((M, N), MXU_DTYPE)],
    )(h, w_gate, w_up)


def _gate_up_bwd(dx, w_down, gate, up, *, name):
    (M, K), N = dx.shape, w_down.shape[0]
    tm, tn = _tile(M, FFN_ROW_TILE), _tile(N, FFN_COL_TILE)

    def body(dx_ref, w_ref, g_ref, u_ref, dg_ref, du_ref):
        d = _dot(dx_ref[...], w_ref[...], NT)
        g = g_ref[...]
        sg = _sigmoid(g)
        dg_ref[...] = (d * u_ref[...] * (sg * (1.0 + g * (1.0 - sg)))).astype(dg_ref.dtype)
        du_ref[...] = (d * g * sg).astype(du_ref.dtype)

    o_spec = pl.BlockSpec((tm, tn), lambda j, i: (i, j))
    return _pcall(
        body, name=name, dims=("parallel", "parallel"), grid=(N // tn, M // tm),
        in_specs=[pl.BlockSpec((tm, K), lambda j, i: (i, 0)), pl.BlockSpec((tn, K), lambda j, i: (j, 0)),
                  o_spec, o_spec],
        out_specs=[o_spec] * 2, out_shape=[_sds((M, N), MXU_DTYPE)] * 2,
    )(dx, w_down, gate, up)


def _head_pair_weights(w, c, rows):
    return jnp.where(_lane((rows, LANES)) < HEAD_DIM, w[:, 2 * c:2 * c + 1], w[:, 2 * c + 1:2 * c + 2])


def _merge(outs_by_d, lses_by_d, *, name):
    S = outs_by_d[1].shape[0]
    far = DILATIONS[1:]

    def body(tm, o1, o4, o16, l1, l4, l16, o_o, w1_o, w4_o, w16_o, nat_o, nat_l):
        for b, (o_r, l_r, d) in enumerate(zip((o4, o16), (l4, l16), far)):
            _from_branch(o_r, nat_o, b * QKV_CHUNKS, QKV_CHUNKS, d, tm)
            _from_branch(l_r, nat_l, b, 1, d, tm)
        ls = [l1[...], nat_l[0], nat_l[1]]
        m = jnp.maximum(jnp.maximum(ls[0], ls[1]), ls[2])
        es = [jnp.exp(l - m) for l in ls]
        tot = es[0] + es[1] + es[2]
        ws = [e / tot for e in es]
        for w_o, w in zip((w1_o, w4_o, w16_o), ws):
            w_o[...] = w
        for c in range(QKV_CHUNKS):
            sl = slice(c * LANES, (c + 1) * LANES)
            parts = (o1[:, sl], nat_o[c], nat_o[QKV_CHUNKS + c])
            o_o[:, sl] = sum(_head_pair_weights(w, c, tm) * part for w, part in zip(ws, parts))

    ins = [outs_by_d[d] for d in DILATIONS] + [lses_by_d[d] for d in DILATIONS]
    outs = [_sds((S, 1024))] + [_sds((S, LANES))] * 3
    rows = _row_tile(S, 256)
    return _rowwise(body, ins, outs, name=name, rows=S, tm=256,
                    scratch=[pltpu.VMEM((2 * QKV_CHUNKS, rows, LANES), F32), pltpu.VMEM((2, rows, LANES), F32)])


def _merge_bwd(do, o, ws, *, name):
    S = do.shape[0]

    def body(tm, do_r, o_r, w1, w4, w16, d1, d4, d16, e1, e4, e16, nat, nat_l):
        prod = do_r[...] * o_r[...]
        sums = _cols_to_lanes([jnp.sum(prod[:, j * HEAD_DIM:(j + 1) * HEAD_DIM], axis=1, keepdims=True)
                               for j in range(DIL_HEADS)], tm)
        for w_r, d_o, e_o, d in zip((w1, w4, w16), (d1, d4, d16), (e1, e4, e16), DILATIONS):
            w = w_r[...]
            nat_l[0] = w * sums
            _to_branch(nat_l, 0, 1, e_o, d, tm)
            for c in range(QKV_CHUNKS):
                nat[c] = _head_pair_weights(w, c, tm) * do_r[:, c * LANES:(c + 1) * LANES]
            _to_branch(nat, 0, QKV_CHUNKS, d_o, d, tm)

    outs = [_branch_sds(S, 1024, d, MXU_DTYPE) for d in DILATIONS] + [_branch_sds(S, LANES, d, F32) for d in DILATIONS]
    rows = _row_tile(S, 256)
    got = _rowwise(body, [do, o] + [ws[d] for d in DILATIONS], outs, name=name, rows=S, tm=256,
                   scratch=[pltpu.VMEM((QKV_CHUNKS, rows, LANES), F32), pltpu.VMEM((1, rows, LANES), F32)])
    return dict(zip(DILATIONS, got[:3])), dict(zip(DILATIONS, got[3:]))


def _loss_head(x, g, target, *, name):
    S, D = x.shape

    def body(tm, x_r, g_r, t_r, dx_o, dg_o, sq_o):
        xf = x_r[...]
        xhat, _ = _rms_parts(xf)
        err = xhat * g_r[...] - t_r[...]
        dx, dgp = _rms_bwd_rows(xf, g_r[...], err * (1.0 / D))
        dx_o[...] = dx
        _acc_rows(dg_o, dgp)
        _acc_rows(sq_o, err * err)

    return _rowwise(body, [x, g.reshape(1, D), target], [_sds((S, D)), _sds((1, D)), _sds((1, D))],
                    name=name, rows=S, accs=(1, 2))


def _adamw(w, g, m, v, *, name):
    c1 = 1.0 - ADAM_B1 ** ADAM_STEP
    c2 = 1.0 - ADAM_B2 ** ADAM_STEP

    def body(tm, w_r, g_r, m_r, v_r, d_o, m_o, v_o):
        g = g_r[...]
        m_new = ADAM_B1 * m_r[...] + (1.0 - ADAM_B1) * g
        v_new = ADAM_B2 * v_r[...] + (1.0 - ADAM_B2) * (g * g)
        m_o[...] = m_new
        v_o[...] = v_new
        d_o[...] = -ADAM_LR * ((m_new / c1) / (jnp.sqrt(v_new / c2) + ADAM_EPS) + ADAM_WD * w_r[...])

    return _rowwise(body, [w, g, m, v], [_sds(w.shape)] * 3, name=name, rows=w.shape[0], tm=256)


SUM_ROW_TILE = 256


def _sum_cores(grads, theirs, half_index, *, name):
    _, R, C = grads.shape
    h = R // 2
    nb = h // SUM_ROW_TILE

    def body(c_ref, g_ref, t_ref, o_ref):
        o_ref[...] = (g_ref[...].astype(F32) + t_ref[...].astype(F32)).astype(o_ref.dtype)

    grid_spec = pltpu.PrefetchScalarGridSpec(
        num_scalar_prefetch=1, grid=(4, nb),
        in_specs=[pl.BlockSpec((1, SUM_ROW_TILE, C), lambda k, i, c_ref: (k, c_ref[0] * nb + i, 0)),
                  pl.BlockSpec((1, SUM_ROW_TILE, C), lambda k, i, c_ref: (k, i, 0))],
        out_specs=pl.BlockSpec((1, SUM_ROW_TILE, C), lambda k, i, c_ref: (k, i, 0)))
    return _pcall(body, name=name, dims=("parallel", "parallel"), grid_spec=grid_spec,
                  out_shape=_sds((4, h, C), grads.dtype))(half_index, grads, theirs)


def _sum_chips(parts, half_index, *, name):
    _, h, C = parts.shape
    nb = h // SUM_ROW_TILE

    def body(c_ref, p_ref, o_ref):
        p = [p_ref[k].astype(F32) for k in range(4)]
        o_ref[...] = ((p[0] + p[1]) + p[2]) + p[3]

    grid_spec = pltpu.PrefetchScalarGridSpec(
        num_scalar_prefetch=1, grid=(nb,),
        in_specs=[pl.BlockSpec((4, SUM_ROW_TILE, C), lambda i, c_ref: (0, i, 0))],
        out_specs=pl.BlockSpec((SUM_ROW_TILE, C), lambda i, c_ref: (c_ref[0] * nb + i, 0)))
    return _pcall(body, name=name, dims=("parallel",), grid_spec=grid_spec,
                  out_shape=_sds((2 * h, C)))(half_index, parts)


def _position():
    return lax.axis_index("x"), lax.axis_index("y"), lax.axis_index("c")


def _chip_peers(x, y):
    return [(1 - x, y), (x, 1 - y), (1 - x, 1 - y)]


_HBM = pl.BlockSpec(memory_space=pltpu.HBM)
LOCAL_COPY_CHUNKS = 8


def _local_copies(src_ref, dst_ref, sems):
    rows = src_ref.shape[0] // LOCAL_COPY_CHUNKS
    assert rows * LOCAL_COPY_CHUNKS == src_ref.shape[0]
    return [pltpu.make_async_copy(src_ref.at[pl.ds(i * rows, rows)], dst_ref.at[pl.ds(i * rows, rows)], sems.at[i])
            for i in range(LOCAL_COPY_CHUNKS)]


class _Exchange:
    def __init__(self, src, out_shape, sems, stages):
        self.src, self.out_shape, self.sems, self.stages = src, out_shape, sems, stages

    def run(self, refs, step, n_steps, at_end):
        for fraction, fn in self.stages:
            if (fraction == 1.0) == at_end:
                pl.when(step == int(round(fraction * (n_steps - 1))))(functools.partial(fn, *refs))


def _run_exchange(ex, *, name):
    def body(*refs):
        for _, fn in ex.stages:
            fn(*refs)

    return pl.pallas_call(
        body, name=name, in_specs=[_HBM], out_specs=_HBM, out_shape=ex.out_shape, scratch_shapes=list(ex.sems),
    )(ex.src)


def _gather_exchange(src):
    R, C = src.shape
    h = R // 2

    def plan(src_ref, out_ref, send_sems, recv_sems, local_sems):
        x, y, c = _position()
        me = 2 * x + y
        peers = _chip_peers(x, y)
        mine, other = pl.ds(c * h, h), pl.ds((1 - c) * h, h)

        def copy(sem, src_part, dst_part, device):
            return pltpu.make_async_remote_copy(
                src_ref=src_part, dst_ref=dst_part, send_sem=send_sems.at[sem], recv_sem=recv_sems.at[sem],
                device_id=device, device_id_type=MESH)

        landed = [out_ref.at[2 * px + py, mine] for px, py in peers]
        theirs = [out_ref.at[2 * px + py, other] for px, py in peers]
        return dict(
            sends=lambda: [copy(j, src_ref.at[mine], out_ref.at[me, mine], (px, py, c))
                           for j, (px, py) in enumerate(peers)],
            local=lambda: _local_copies(src_ref, out_ref.at[me], local_sems),
            arrivals=lambda: [copy(j, landed[j], landed[j], (px, py, c)) for j, (px, py) in enumerate(peers)],
            passed=lambda: [copy(3 + j, landed[j], landed[j], (x, y, 1 - c)) for j in range(3)],
            from_sibling=lambda: [copy(3 + j, theirs[j], theirs[j], (x, y, 1 - c)) for j in range(3)])

    def start(*refs):
        p = plan(*refs)
        for cp in p["sends"]() + p["local"]():
            cp.start()

    def pass_on(*refs):
        p = plan(*refs)
        for arrival, forward in zip(p["arrivals"](), p["passed"]()):
            arrival.wait_recv()
            forward.start()

    def finish(*refs):
        p = plan(*refs)
        for cp in p["from_sibling"]():
            cp.wait_recv()
        for cp in p["sends"]() + p["passed"]():
            cp.wait_send()
        for cp in p["local"]():
            cp.wait()

    sems = [pltpu.SemaphoreType.DMA((6,)), pltpu.SemaphoreType.DMA((6,)), pltpu.SemaphoreType.DMA((LOCAL_COPY_CHUNKS,))]
    return _Exchange(src, jax.ShapeDtypeStruct((4, R, C), src.dtype), sems, [(0.0, start), (0.6, pass_on), (1.0, finish)])


def _swap_other_half(src, *, name):
    _, R, C = src.shape
    h = R // 2

    def body(src_ref, out_ref, send_sem, recv_sem):
        x, y, c = _position()
        cp = pltpu.make_async_remote_copy(
            src_ref=src_ref.at[:, pl.ds((1 - c) * h, h)], dst_ref=out_ref, send_sem=send_sem, recv_sem=recv_sem,
            device_id=(x, y, 1 - c), device_id_type=MESH)
        cp.start()
        cp.wait()

    return pl.pallas_call(
        body, name=name, in_specs=[_HBM], out_specs=_HBM, out_shape=jax.ShapeDtypeStruct((4, h, C), src.dtype),
        scratch_shapes=[pltpu.SemaphoreType.DMA, pltpu.SemaphoreType.DMA],
    )(src)


def _scatter_exchange(src):
    def plan(src_ref, out_ref, send_sems, recv_sems, local_sems):
        x, y, c = _position()
        me = 2 * x + y
        peers = _chip_peers(x, y)

        def copy(j, src_block, dst_slot):
            px, py = peers[j]
            return pltpu.make_async_remote_copy(
                src_ref=src_ref.at[src_block], dst_ref=out_ref.at[dst_slot], send_sem=send_sems.at[j],
                recv_sem=recv_sems.at[j], device_id=(px, py, c), device_id_type=MESH)

        return dict(sends=lambda: [copy(j, 2 * px + py, me) for j, (px, py) in enumerate(peers)],
                    arrivals=lambda: [copy(j, me, 2 * px + py) for j, (px, py) in enumerate(peers)],
                    local=lambda: _local_copies(src_ref.at[me], out_ref.at[me], local_sems))

    def start(*refs):
        p = plan(*refs)
        for cp in p["sends"]() + p["local"]():
            cp.start()

    def finish(*refs):
        p = plan(*refs)
        for cp in p["arrivals"]():
            cp.wait_recv()
        for cp in p["sends"]():
            cp.wait_send()
        for cp in p["local"]():
            cp.wait()

    sems = [pltpu.SemaphoreType.DMA((3,)), pltpu.SemaphoreType.DMA((3,)), pltpu.SemaphoreType.DMA((LOCAL_COPY_CHUNKS,))]
    return _Exchange(src, jax.ShapeDtypeStruct(src.shape, src.dtype), sems, [(0.0, start), (1.0, finish)])


def _join_halves(src, *, name):
    R, C = src.shape
    h = R // 2

    def body(src_ref, out_ref, send_sem, recv_sem):
        x, y, c = _position()
        mine, theirs = pl.ds(c * h, h), pl.ds((1 - c) * h, h)
        cp = pltpu.make_async_remote_copy(
            src_ref=src_ref.at[mine], dst_ref=out_ref.at[mine], send_sem=send_sem, recv_sem=recv_sem,
            device_id=(x, y, 1 - c), device_id_type=MESH)
        cp.start()
        pltpu.make_async_remote_copy(
            src_ref=src_ref.at[theirs], dst_ref=out_ref.at[theirs], send_sem=send_sem, recv_sem=recv_sem,
            device_id=(x, y, 1 - c), device_id_type=MESH).wait_recv()
        cp.wait_send()

    return pl.pallas_call(
        body, name=name, in_specs=[_HBM], out_specs=_HBM, out_shape=jax.ShapeDtypeStruct((R, C), src.dtype),
        input_output_aliases={0: 0},
        scratch_shapes=[pltpu.SemaphoreType.DMA, pltpu.SemaphoreType.DMA],
    )(src)


def _allreduce_small(vec, *, name):
    R, C = vec.shape

    def body(v_ref, o_ref, slots, send_sems, recv_sems):
        x, y, c = _position()
        me = 4 * x + 2 * y + c

        def peer(k):
            return x ^ ((k >> 2) & 1), y ^ ((k >> 1) & 1), c ^ (k & 1)

        def copy(k, slot):
            return pltpu.make_async_remote_copy(
                src_ref=v_ref, dst_ref=slots.at[slot], send_sem=send_sems.at[k - 1], recv_sem=recv_sems.at[k - 1],
                device_id=peer(k), device_id_type=MESH)

        slots[me] = v_ref[...]
        sends = [copy(k, me) for k in range(1, 8)]
        for cp in sends:
            cp.start()
        for k in range(1, 8):
            px, py, pc = peer(k)
            copy(k, 4 * px + 2 * py + pc).wait_recv()
        total = slots[0]
        for d in range(1, 8):
            total = total + slots[d]
        o_ref[...] = total
        for cp in sends:
            cp.wait_send()

    vmem = pl.BlockSpec(memory_space=pltpu.VMEM)
    return pl.pallas_call(
        body, name=name, in_specs=[vmem], out_specs=vmem, out_shape=jax.ShapeDtypeStruct((R, C), vec.dtype),
        scratch_shapes=[pltpu.VMEM((8, R, C), vec.dtype), pltpu.SemaphoreType.DMA((7,)), pltpu.SemaphoreType.DMA((7,))],
    )(vec)


def _cross_cfg(S, mem_len):
    return _Attn(T=S, Tk=mem_len, G=1, nh=X_HEADS, rep=1, dqk=X_HEAD_DIM, dv=X_HEAD_DIM, tq=512, tk=mem_len,
                 mode="none", scale=X_HEAD_DIM ** -0.5, qcol=lambda g: 0, kcol=lambda g: 0, vcol=lambda g: 1,
                 ocol=lambda g: 0, o_width=X_HEADS * X_HEAD_DIM)


def _swa_cfg(S):
    return _Attn(T=S, Tk=S, G=1, nh=SWA_HEADS, rep=SWA_HEADS // SWA_KV_HEADS, dqk=HEAD_DIM, dv=HEAD_DIM, tq=BLOCK,
                 tk=BLOCK, mode="band", max_dist=SWA_WINDOW - 1, scale=HEAD_DIM ** -0.5, qcol=lambda g: 0,
                 kcol=lambda g: 0, vcol=lambda g: 0, ocol=lambda g: 0, o_width=SWA_HEADS * HEAD_DIM)


MLA_GROUP = 4


def _mla_cfg(S):
    t = _tile(S, 512)
    return _Attn(T=S, Tk=S, G=MLA_HEADS // MLA_GROUP, nh=MLA_GROUP, rep=1, dqk=LANES, dv=MLA_V, tq=t, tk=t, mode="causal",
                 scale=(MLA_NOPE + MLA_ROPE) ** -0.5, qcol=lambda g: g, kcol=lambda g: g, vcol=lambda g: g,
                 ocol=lambda g: g, o_width=MLA_HEADS * MLA_V)


def _dil_cfg(S, window, dil):
    return _Attn(T=S // dil, Tk=S // dil, G=dil, nh=DIL_HEADS, rep=1, dqk=HEAD_DIM, dv=HEAD_DIM, tq=BLOCK, tk=BLOCK,
                 mode="band", max_dist=window // dil, scale=HEAD_DIM ** -0.5, qcol=lambda g: g, kcol=lambda g: g,
                 vcol=lambda g: g, ocol=lambda g: g, o_width=dil * DIL_HEADS * HEAD_DIM)


def _cross_fwd(p, x, mem, W, vec):
    S = x.shape[0]
    cfg = _cross_cfg(S, mem.shape[0])
    hx = _rmsnorm(x, vec[p + "x_norm"], name=p + "x_norm")
    qx = _mm(hx, W[p + "w_xq"], mode="nn", name=p + "xq", out_dtype=MXU_DTYPE)
    memn = _rmsnorm(mem, vec[p + "mem_norm"], name=p + "mem_norm")
    kvx = _mm(memn, W[p + "w_xkv"], mode="nn", name=p + "xkv", out_dtype=MXU_DTYPE)
    ox, lse = _attn_fwd(cfg, qx, kvx, kvx, name=p + "x_attn", out_dtype=MXU_DTYPE)
    out = _mm(ox, W[p + "w_xo"], mode="nn", name=p + "xo", res=x)
    return out, (x, hx, qx, memn, kvx, ox, lse)


def _cross_bwd(p, dx, saved, mem, W, vec, dW, dvec):
    x, hx, qx, memn, kvx, ox, lse = saved
    cfg = _cross_cfg(x.shape[0], mem.shape[0])
    dox = _mm(dx, W[p + "w_xo"], mode="nt", name=p + "xo_dx", out_dtype=MXU_DTYPE)
    dW[p + "w_xo"] = _mm(ox, dx, mode="tn", name=p + "xo_dw")
    delta, _ = _attn_delta(cfg, ox, dox, name=p + "x_delta")
    dqx = _attn_dq(cfg, qx, kvx, kvx, dox, lse, delta, name=p + "x_dq", out_dtype=MXU_DTYPE)
    dkx, dvx = _attn_dkv(cfg, qx, kvx, kvx, dox, lse, delta, name=p + "x_dkv", out_dtype=MXU_DTYPE)
    dkvx = jnp.concatenate([dkx, dvx], axis=1)
    dhx = _mm(dqx, W[p + "w_xq"], mode="nt", name=p + "xq_dx")
    dW[p + "w_xq"] = _mm(hx, dqx, mode="tn", name=p + "xq_dw")
    dW[p + "w_xkv"] = _mm(memn, dkvx, mode="tn", name=p + "xkv_dw")
    dmemn = _mm(dkvx, W[p + "w_xkv"], mode="nt", name=p + "xkv_dx")
    _, dvec[p + "mem_norm"] = _rmsnorm_bwd(mem, vec[p + "mem_norm"], dmemn, name=p + "mem_norm_bwd")
    dx_in, dvec[p + "x_norm"] = _rmsnorm_bwd(x, vec[p + "x_norm"], dhx, name=p + "x_norm_bwd", dres=dx)
    return dx_in


def _ffn_fwd(p, x, W, vec):
    hf = _rmsnorm(x, vec[p + "ffn_norm"], name=p + "ffn_norm")
    gate, up, act = _gate_up(hf, W[p + "w_gate"], W[p + "w_up"], name=p + "gate_up")
    out = _mm(act, W[p + "w_down"], mode="nn", name=p + "down", res=x)
    return out, (x, hf, gate, up, act)


def _ffn_bwd(p, dx, saved, W, vec, dW, dvec):
    x, hf, gate, up, act = saved
    dW[p + "w_down"] = _mm(act, dx, mode="tn", name=p + "down_dw")
    dgate, dup = _gate_up_bwd(dx, W[p + "w_down"], gate, up, name=p + "gate_up_bwd")
    dhf = _mm(dgate, W[p + "w_gate"], mode="nt", name=p + "gate_dx")
    dhf = _mm(dup, W[p + "w_up"], mode="nt", name=p + "up_dx", res=dhf)
    dW[p + "w_gate"] = _mm(hf, dgate, mode="tn", name=p + "gate_dw")
    dW[p + "w_up"] = _mm(hf, dup, mode="tn", name=p + "up_dw")
    dx_in, dvec[p + "ffn_norm"] = _rmsnorm_bwd(x, vec[p + "ffn_norm"], dhf, name=p + "ffn_norm_bwd", dres=dx)
    return dx_in


def _even_fwd(p, x, tabs, W, vec, comm=None):
    S = x.shape[0]
    h = _rmsnorm(x, vec[p + "mix_norm"], name=p + "mix_norm")
    z = _mm(h, W[p + "w_in"], mode="nn", name=p + "in")
    qa, ka, va, cqn, ckvn, kr = _l0_prep(z, tabs, vec[p + "q_norm"], vec[p + "kv_norm"], name=p + "prep")
    sink = jnp.pad(vec[p + "sinks"], (0, LANES - SWA_HEADS)).reshape(1, LANES)
    oa, lse_a = _band_fwd(_swa_cfg(S), qa, ka, va, name=p + "swa", sink=sink, out_dtype=MXU_DTYPE)
    qb = _mm(cqn, W[p + "w_uq"], mode="nn", name=p + "uq")
    kvb = _mm(ckvn, W[p + "w_ukv"], mode="nn", name=p + "ukv")
    Q, K, V = _mla_prep(qb, kvb, kr, tabs, name=p + "mla_prep")
    if comm is None:
        ob, lse_b = _causal_fwd(_mla_cfg(S), Q, K, V, name=p + "mla", out_dtype=MXU_DTYPE)
    else:
        ob, lse_b, gathered = _causal_fwd(_mla_cfg(S), Q, K, V, name=p + "mla", out_dtype=MXU_DTYPE,
                                          carry=comm.late_weights_exchange())
        W = {**W, **comm.late_weights(gathered)}
    o = jnp.concatenate([oa, ob], axis=1)
    out = _mm(o, W[p + "w_out"], mode="nn", name=p + "out", res=x)
    return out, (x, h, z, qa, ka, va, cqn, ckvn, sink, oa, lse_a, Q, K, V, ob, lse_b, o), W


def _even_bwd(p, dx, saved, tabs, W, vec, dW, dvec, comm=None):
    x, h, z, qa, ka, va, cqn, ckvn, sink, oa, lse_a, Q, K, V, ob, lse_b, o = saved
    S = x.shape[0]
    do = _mm(dx, W[p + "w_out"], mode="nt", name=p + "out_dx", out_dtype=MXU_DTYPE)
    dW[p + "w_out"] = _mm(o, dx, mode="tn", name=p + "out_dw")
    doa, dob = do[:, :SWA_HEADS * HEAD_DIM], do[:, SWA_HEADS * HEAD_DIM:]
    cfg = _swa_cfg(S)
    delta, dsink = _attn_delta(cfg, oa, doa, name=p + "swa_delta", lse=lse_a, sink=sink)
    dvec[p + "sinks"] = dsink
    dqa, dka, dva = _band_bwd(cfg, qa, ka, va, doa, lse_a, delta, name=p + "swa_bwd")
    cfg = _mla_cfg(S)
    if comm is None:
        dQ, dK, dV = _causal_bwd(cfg, Q, K, V, ob, dob, lse_b, name=p + "mla_bwd")
    else:
        dQ, dK, dV, landed = _causal_bwd(cfg, Q, K, V, ob, dob, lse_b, name=p + "mla_bwd",
                                         carry=comm.late_grads_exchange(dW))
        comm.late_grads_landed(landed)
    dqb, dkvb, dkr = _mla_prep_bwd(dQ, dK, dV, tabs, name=p + "mla_prep_bwd")
    dcqn = _mm(dqb, W[p + "w_uq"], mode="nt", name=p + "uq_dx")
    dW[p + "w_uq"] = _mm(cqn, dqb, mode="tn", name=p + "uq_dw")
    dckvn = _mm(dkvb, W[p + "w_ukv"], mode="nt", name=p + "ukv_dx")
    dW[p + "w_ukv"] = _mm(ckvn, dkvb, mode="tn", name=p + "ukv_dw")
    dz, dvec[p + "q_norm"], dvec[p + "kv_norm"] = _l0_prep_bwd(
        z, tabs, vec[p + "q_norm"], vec[p + "kv_norm"], dqa, dka, dva, dcqn, dckvn, dkr, name=p + "prep_bwd")
    dh = _mm(dz, W[p + "w_in"], mode="nt", name=p + "in_dx")
    dW[p + "w_in"] = _mm(h, dz, mode="tn", name=p + "in_dw")
    dx_in, dvec[p + "mix_norm"] = _rmsnorm_bwd(x, vec[p + "mix_norm"], dh, name=p + "mix_norm_bwd", dres=dx)
    return dx_in


def _odd_fwd(p, x, tabs, W, vec):
    S = x.shape[0]
    assert S % (DIL_PATTERNS[-1][1] * BLOCK) == 0, "keys past the end of the sequence are never attended"
    h = _rmsnorm(x, vec[p + "mix_norm"], name=p + "mix_norm")
    qkv = _mm(h, W[p + "w_qkv"], mode="nn", name=p + "qkv")
    qkv_by_d = _l1_prep(qkv, tabs, name=p + "prep")
    outs, lses = {}, {}
    for window, dil in DIL_PATTERNS:
        outs[dil], lses[dil] = _band_fwd(_dil_cfg(S, window, dil), *qkv_by_d[dil], name=p + "dil%d" % dil)
    o, w1, w4, w16 = _merge(outs, lses, name=p + "merge")
    out = _mm(o, W[p + "w_out"], mode="nn", name=p + "out", res=x)
    return out, (x, h, qkv_by_d, lses, dict(zip(DILATIONS, (w1, w4, w16))), o)


def _odd_bwd(p, dx, saved, tabs, W, vec, dW, dvec):
    x, h, qkv_by_d, lses, ws, o = saved
    S = x.shape[0]
    do = _mm(dx, W[p + "w_out"], mode="nt", name=p + "out_dx")
    dW[p + "w_out"] = _mm(o, dx, mode="tn", name=p + "out_dw")
    dos, deltas = _merge_bwd(do, o, ws, name=p + "merge_bwd")
    grads = {}
    for window, dil in DIL_PATTERNS:
        grads[dil] = _band_bwd(_dil_cfg(S, window, dil), *qkv_by_d[dil], dos[dil], lses[dil], deltas[dil],
                               name=p + "dil%d_bwd" % dil)
    dqkv = _l1_prep_bwd(grads, tabs, name=p + "prep_bwd")
    dh = _mm(dqkv, W[p + "w_qkv"], mode="nt", name=p + "qkv_dx")
    dW[p + "w_qkv"] = _mm(h, dqkv, mode="tn", name=p + "qkv_dw")
    dx_in, dvec[p + "mix_norm"] = _rmsnorm_bwd(x, vec[p + "mix_norm"], dh, name=p + "mix_norm_bwd", dres=dx)
    return dx_in


def _local_step(x, mem, positions, target, W, vec, comm=None):
    tabs = _rope_tables(positions)
    x1, s_mix0, W = _even_fwd("l0_", x, tabs, W, vec, comm)
    x2, s_x0 = _cross_fwd("l0_", x1, mem, W, vec)
    x3, s_f0 = _ffn_fwd("l0_", x2, W, vec)
    x4, s_mix1 = _odd_fwd("l1_", x3, tabs, W, vec)
    x5, s_x1 = _cross_fwd("l1_", x4, mem, W, vec)
    x6, s_f1 = _ffn_fwd("l1_", x5, W, vec)
    dW, dvec = {}, {}
    dx, dvec["final_norm"], sq = _loss_head(x6, vec["final_norm"], target, name="loss_head")
    dx = _ffn_bwd("l1_", dx, s_f1, W, vec, dW, dvec)
    dx = _cross_bwd("l1_", dx, s_x1, mem, W, vec, dW, dvec)
    dx = _odd_bwd("l1_", dx, s_mix1, tabs, W, vec, dW, dvec)
    dx = _ffn_bwd("l0_", dx, s_f0, W, vec, dW, dvec)
    dx = _cross_bwd("l0_", dx, s_x0, mem, W, vec, dW, dvec)
    dx = _even_bwd("l0_", dx, s_mix0, tabs, W, vec, dW, dvec, comm)
    return sq, dx, dW, dvec


_LAYER_MATS = {
    0: [("w_in", "col"), ("w_uq", "col"), ("w_ukv", "col"), ("w_out", "row"), ("w_xq", "row"), ("w_xkv", "row"),
        ("w_xo", "col"), ("w_gate", "col"), ("w_up", "col"), ("w_down", "row")],
    1: [("w_qkv", "col"), ("w_out", "row"), ("w_xq", "row"), ("w_xkv", "row"), ("w_xo", "col"), ("w_gate", "col"),
        ("w_up", "col"), ("w_down", "row")],
}
MATS = [("l%d_%s" % (l, n), kind) for l in (0, 1) for n, kind in _LAYER_MATS[l]]
_LAYER_VECS = {0: ["mix_norm", "sinks", "q_norm", "kv_norm", "x_norm", "mem_norm", "ffn_norm"],
               1: ["mix_norm", "x_norm", "mem_norm", "ffn_norm"]}
VECS = ["l%d_%s" % (l, n) for l in (0, 1) for n in _LAYER_VECS[l]] + ["final_norm"]
WEIGHT_ORDER = (["l0_mix_norm", "l0_w_in", "l0_sinks", "l0_q_norm", "l0_w_uq", "l0_kv_norm", "l0_w_ukv", "l0_w_out",
                 "l0_x_norm", "l0_mem_norm", "l0_w_xq", "l0_w_xkv", "l0_w_xo", "l0_ffn_norm", "l0_w_gate", "l0_w_up",
                 "l0_w_down", "l1_mix_norm", "l1_w_qkv", "l1_w_out", "l1_x_norm", "l1_mem_norm", "l1_w_xq",
                 "l1_w_xkv", "l1_w_xo", "l1_ffn_norm", "l1_w_gate", "l1_w_up", "l1_w_down", "final_norm"])
PACK_COLS = 1024
PACK_ROW_TILE = 2 * SUM_ROW_TILE
EXCHANGE_DTYPE = jnp.bfloat16
VEC_ROWS = 16
LOSS_ROW = len(VECS)
N_CHIPS = 4


class _Group:
    def __init__(self, mats, shards):
        self.mats, self.shards = mats, shards
        self.layout, off = {}, 0
        for name, _ in mats:
            n = shards[name].size // PACK_COLS
            assert n * PACK_COLS == shards[name].size
            self.layout[name] = (off, n)
            off += n
        self.used = off
        self.rows = -(-off // PACK_ROW_TILE) * PACK_ROW_TILE

    def pack(self, tensors, dtype):
        parts = [tensors[name].astype(dtype).reshape(-1, PACK_COLS) for name, _ in self.mats]
        return jnp.concatenate(parts + [jnp.zeros((self.rows - self.used, PACK_COLS), dtype)], axis=0)

    def unpack(self, packed):
        return {name: packed[off:off + n].reshape(self.shards[name].shape) for name, (off, n) in self.layout.items()}

    def full_weights(self, gathered):
        W = {}
        for name, kind in self.mats:
            off, n = self.layout[name]
            r, cw = self.shards[name].shape
            blocks = gathered[:, off:off + n].reshape(N_CHIPS, r, cw)
            W[name] = blocks.reshape(N_CHIPS * r, cw) if kind == "row" else (
                jnp.transpose(blocks, (1, 0, 2)).reshape(r, N_CHIPS * cw))
        if "l0_w_in" in W:
            W["l0_w_in"] = jnp.pad(W["l0_w_in"], ((0, 0), (0, Z_END - W["l0_w_in"].shape[1])))
        if "l0_w_uq" in W:
            uq = W["l0_w_uq"].reshape(MLA_Q_RANK, MLA_HEADS, MLA_NOPE + MLA_ROPE)
            uq = jnp.pad(uq, ((0, 0), (0, 0), (0, LANES - MLA_NOPE - MLA_ROPE)))
            W["l0_w_uq"] = uq.reshape(MLA_Q_RANK, MLA_HEADS * LANES)
        return W

    def pack_grads(self, dW):
        parts = []
        for name, kind in self.mats:
            r, cw = self.shards[name].shape
            g = dW[name]
            if name == "l0_w_in":
                g = g[:, :Z_KR + MLA_ROPE]
            if name == "l0_w_uq":
                g = g.reshape(MLA_Q_RANK, MLA_HEADS, LANES)[:, :, :MLA_NOPE + MLA_ROPE].reshape(MLA_Q_RANK, -1)
            if kind == "col":
                g = jnp.transpose(g.reshape(r, N_CHIPS, cw), (1, 0, 2))
            parts.append(g.reshape(N_CHIPS, -1, PACK_COLS).astype(EXCHANGE_DTYPE))
        pad = jnp.zeros((N_CHIPS, self.rows - self.used, PACK_COLS), EXCHANGE_DTYPE)
        return jnp.concatenate(parts + [pad], axis=1)


def _pack_vecs(vecs):
    rows = [jnp.pad(vecs[n].reshape(-1).astype(F32), (0, PACK_COLS - vecs[n].size)) for n in VECS]
    rows += [jnp.zeros((PACK_COLS,), F32)] * (VEC_ROWS - len(rows))
    return jnp.stack(rows)


def _unpack_vecs(packed, like):
    return {n: packed[i, :like[n].size].reshape(like[n].shape) for i, n in enumerate(VECS)}


EARLY_MATS = [m for m in MATS if m[0] in ("l0_w_in", "l0_w_uq", "l0_w_ukv")]
LATE_MATS = [m for m in MATS if m not in EARLY_MATS]


class _StepComm:
    def __init__(self, shards):
        self.early, self.late = _Group(EARLY_MATS, shards), _Group(LATE_MATS, shards)
        self.half_index = lax.axis_index("c").astype(jnp.int32).reshape(1)
        self.late_grads = None

    def early_weights(self):
        src = self.early.pack(self.early.shards, MXU_DTYPE)
        return self.early.full_weights(_run_exchange(_gather_exchange(src), name="gather_early"))

    def late_weights_exchange(self):
        return _gather_exchange(self.late.pack(self.late.shards, MXU_DTYPE))

    def late_weights(self, gathered):
        return self.late.full_weights(gathered)

    def _chip_sum(self, group, dW, tag):
        grads = group.pack_grads(dW)
        theirs = _swap_other_half(grads, name="swap_other_half_" + tag)
        return _sum_cores(grads, theirs, self.half_index, name="sum_cores_" + tag)

    def _finish(self, parts, tag):
        return _join_halves(_sum_chips(parts, self.half_index, name="sum_chips_" + tag), name="join_halves_" + tag)

    def late_grads_exchange(self, dW):
        return _scatter_exchange(self._chip_sum(self.late, dW, "late"))

    def late_grads_landed(self, parts):
        self.late_grads = self._finish(parts, "late")

    def early_grads(self, dW):
        parts = _run_exchange(_scatter_exchange(self._chip_sum(self.early, dW, "early")), name="scatter_early")
        return self._finish(parts, "early")


def _step(a):
    weights = {n: a[n] for n in WEIGHT_ORDER}
    shards = {n: weights[n] for n, _ in MATS}
    vec = {n: weights[n] for n in VECS}
    comm = _StepComm(shards)
    sq, grad_x, dW, dvec = _local_step(a["x"][0], a["mem"][0], a["positions"], a["loss_target"][0],
                                       comm.early_weights(), vec, comm)

    dvec = dict(dvec)
    dvec["l0_sinks"] = dvec["l0_sinks"][0, :SWA_HEADS]
    small = _pack_vecs(dvec)
    small = small.at[LOSS_ROW, 0].set(0.5 / a["x"].shape[-1] * jnp.sum(sq))
    small = _allreduce_small(small, name="reduce_gains")
    loss = small[LOSS_ROW, 0]
    g_s = small.at[LOSS_ROW, 0].set(0.0)
    d_s, m_s, v_s = _adamw(_pack_vecs(vec), g_s, _pack_vecs({n: a["m_" + n] for n in VECS}),
                           _pack_vecs({n: a["v_" + n] for n in VECS}), name="adamw_gains")
    got = [_unpack_vecs(packed, vec) for packed in (g_s, d_s, m_s, v_s)]

    for group, g_w in ((comm.late, comm.late_grads), (comm.early, comm.early_grads(dW))):
        for n, g in group.unpack(g_w).items():
            results = (g,) + tuple(_adamw(shards[n], g, a["m_" + n], a["v_" + n], name="adamw_" + n))
            for kind, value in zip(got, results):
                kind[n] = value

    out = [loss, grad_x[None]]
    for kind in got:
        out += [kind[n] for n in WEIGHT_ORDER]
    return tuple(out)


def kernel(x, mem, positions, l0_mix_norm, l0_w_in, l0_sinks, l0_q_norm, l0_w_uq, l0_kv_norm, l0_w_ukv, l0_w_out, l0_x_norm, l0_mem_norm, l0_w_xq, l0_w_xkv, l0_w_xo, l0_ffn_norm, l0_w_gate, l0_w_up, l0_w_down, l1_mix_norm, l1_w_qkv, l1_w_out, l1_x_norm, l1_mem_norm, l1_w_xq, l1_w_xkv, l1_w_xo, l1_ffn_norm, l1_w_gate, l1_w_up, l1_w_down, final_norm, loss_target, m_l0_mix_norm, m_l0_w_in, m_l0_sinks, m_l0_q_norm, m_l0_w_uq, m_l0_kv_norm, m_l0_w_ukv, m_l0_w_out, m_l0_x_norm, m_l0_mem_norm, m_l0_w_xq, m_l0_w_xkv, m_l0_w_xo, m_l0_ffn_norm, m_l0_w_gate, m_l0_w_up, m_l0_w_down, m_l1_mix_norm, m_l1_w_qkv, m_l1_w_out, m_l1_x_norm, m_l1_mem_norm, m_l1_w_xq, m_l1_w_xkv, m_l1_w_xo, m_l1_ffn_norm, m_l1_w_gate, m_l1_w_up, m_l1_w_down, m_final_norm, v_l0_mix_norm, v_l0_w_in, v_l0_sinks, v_l0_q_norm, v_l0_w_uq, v_l0_kv_norm, v_l0_w_ukv, v_l0_w_out, v_l0_x_norm, v_l0_mem_norm, v_l0_w_xq, v_l0_w_xkv, v_l0_w_xo, v_l0_ffn_norm, v_l0_w_gate, v_l0_w_up, v_l0_w_down, v_l1_mix_norm, v_l1_w_qkv, v_l1_w_out, v_l1_x_norm, v_l1_mem_norm, v_l1_w_xq, v_l1_w_xkv, v_l1_w_xo, v_l1_ffn_norm, v_l1_w_gate, v_l1_w_up, v_l1_w_down, v_final_norm):
    return _step(dict(locals()))
```

```python
import functools

import jax
import jax.numpy as jnp
import numpy as np
from jax import lax
from jax.experimental import pallas as pl
from jax.experimental.pallas import tpu as pltpu

F32 = jnp.float32
MXU_DTYPE = jnp.bfloat16
LANES = 128
VMEM_LIMIT_BYTES = 56 * 1024 * 1024

NORM_EPS = 1e-6
ROPE_THETA = 10000.0
BLOCK = 128
HEAD_DIM = 64
SWA_HEADS, SWA_KV_HEADS, SWA_WINDOW = 8, 2, 128
MLA_HEADS, MLA_Q_RANK, MLA_KV_RANK, MLA_NOPE, MLA_ROPE, MLA_V = 8, 384, 256, 64, 32, 64
DIL_HEADS = 16
DIL_PATTERNS = ((128, 1), (512, 4), (2048, 16))
X_HEADS, X_HEAD_DIM = 4, 128
ADAM_LR, ADAM_B1, ADAM_B2, ADAM_EPS, ADAM_WD, ADAM_STEP = 0.001, 0.9, 0.999, 1e-08, 0.01, 10
MESH = pl.DeviceIdType.MESH
NEG_BIG = -1e30

NN = (((1,), (0,)), ((), ()))
NT = (((1,), (1,)), ((), ()))


def _dot(a, b, dims=NN):
    return lax.dot_general(a.astype(MXU_DTYPE), b.astype(MXU_DTYPE), dims, preferred_element_type=F32)


def _pcall(body, *, name, dims=None, **kw):
    params = pltpu.CompilerParams(dimension_semantics=dims, vmem_limit_bytes=VMEM_LIMIT_BYTES)
    return pl.pallas_call(body, name=name, compiler_params=params, **kw)


def _tile(n, pref):
    t = (min(pref, n) // LANES) * LANES
    while t >= LANES:
        if n % t == 0:
            return t
        t -= LANES
    return n


SUBLANES_PACKED = 16


def _row_tile(n, pref):
    t = (min(pref, n) // SUBLANES_PACKED) * SUBLANES_PACKED
    while t >= SUBLANES_PACKED:
        if n % t == 0:
            return t
        t -= SUBLANES_PACKED
    return n


def _lane(shape):
    return lax.broadcasted_iota(jnp.int32, shape, 1)


def _cols_to_lanes(cols, rows):
    lane = _lane((rows, LANES))
    out = jnp.zeros((rows, LANES), F32)
    for j, col in enumerate(cols):
        out = jnp.where(lane == j, col, out)
    return out


def _mm(a, b, *, mode, name, res=None, out_dtype=F32, tm=1408, tn=1536, tk=1408):
    if mode == "nn":
        (M, K), (K2, N) = a.shape, b.shape
    elif mode == "nt":
        (M, K), (N, K2) = a.shape, b.shape
    else:
        (K, M), (K2, N) = a.shape, b.shape
    assert K == K2, (a.shape, b.shape, mode)
    tm, tn, tk = _tile(M, tm), _tile(N, tn), _tile(K, tk)
    nk = K // tk
    in_place = out_dtype == F32 or nk == 1

    def body(*refs):
        refs = list(refs)
        a_ref, b_ref = refs[:2]
        r_ref = refs[2] if res is not None else None
        o_ref = refs[3 if res is not None else 2]
        acc = o_ref if in_place else refs[-1]
        k = pl.program_id(2)
        if mode == "nn":
            part = _dot(a_ref[...], b_ref[...], NN)
        elif mode == "nt":
            part = _dot(a_ref[...], b_ref[...], NT)
        else:
            part = _dot(a_ref[...].T, b_ref[...], NN)
        if nk == 1:
            o_ref[...] = (part if res is None else part + r_ref[...].astype(F32)).astype(o_ref.dtype)
            return

        @pl.when(k == 0)
        def _():
            acc[...] = part if res is None else part + r_ref[...].astype(F32)

        @pl.when(k > 0)
        def _():
            acc[...] += part

        if not in_place:
            @pl.when(k == nk - 1)
            def _():
                o_ref[...] = acc[...].astype(o_ref.dtype)

    if mode == "nn":
        a_spec = pl.BlockSpec((tm, tk), lambda i, j, k: (i, k))
        b_spec = pl.BlockSpec((tk, tn), lambda i, j, k: (k, j))
    elif mode == "nt":
        a_spec = pl.BlockSpec((tm, tk), lambda i, j, k: (i, k))
        b_spec = pl.BlockSpec((tn, tk), lambda i, j, k: (j, k))
    else:
        a_spec = pl.BlockSpec((tk, tm), lambda i, j, k: (k, i))
        b_spec = pl.BlockSpec((tk, tn), lambda i, j, k: (k, j))
    o_spec = pl.BlockSpec((tm, tn), lambda i, j, k: (i, j))
    in_specs = [a_spec, b_spec] + ([] if res is None else [o_spec])
    args = (a, b) + (() if res is None else (res,))
    return _pcall(
        body, name=name, dims=("parallel", "parallel", "arbitrary"),
        grid=(M // tm, N // tn, nk), in_specs=in_specs, out_specs=o_spec,
        out_shape=jax.ShapeDtypeStruct((M, N), out_dtype),
        scratch_shapes=[] if in_place else [pltpu.VMEM((tm, tn), F32)],
    )(*args)


EXCHANGE_DTYPE = jnp.bfloat16


def _dw(a, b, *, name):
    return _mm(a, b, mode="tn", name=name, out_dtype=EXCHANGE_DTYPE)


def _rms_parts(xf):
    r = lax.rsqrt(jnp.mean(xf * xf, axis=-1, keepdims=True) + NORM_EPS)
    return xf * r, r


def _rms_bwd_rows(xf, g, dy):
    xhat, r = _rms_parts(xf)
    dxhat = dy * g
    dx = r * (dxhat - xhat * jnp.mean(dxhat * xhat, axis=-1, keepdims=True))
    return dx, dy * xhat


def _rmsnorm(x, g, *, name, out_dtype=MXU_DTYPE, tm=512):
    M, D = x.shape
    tm = _tile(M, tm)

    def body(x_ref, g_ref, o_ref):
        xhat, _ = _rms_parts(x_ref[...].astype(F32))
        o_ref[...] = (xhat * g_ref[...]).astype(o_ref.dtype)

    return _pcall(
        body, name=name, dims=("parallel",), grid=(M // tm,),
        in_specs=[pl.BlockSpec((tm, D), lambda i: (i, 0)), pl.BlockSpec((1, D), lambda i: (0, 0))],
        out_specs=pl.BlockSpec((tm, D), lambda i: (i, 0)),
        out_shape=jax.ShapeDtypeStruct((M, D), out_dtype),
    )(x, g.reshape(1, D))


def _rmsnorm_bwd(x, g, dy, *, name, dres=None, tm=512):
    M, D = x.shape
    tm = _tile(M, tm)

    def body(*refs):
        if dres is None:
            x_ref, g_ref, dy_ref, dx_ref, dg_ref = refs
        else:
            x_ref, g_ref, dy_ref, dr_ref, dx_ref, dg_ref = refs
        dx, dgp = _rms_bwd_rows(x_ref[...].astype(F32), g_ref[...], dy_ref[...].astype(F32))
        if dres is not None:
            dx = dx + dr_ref[...]
        dx_ref[...] = dx

        @pl.when(pl.program_id(0) == 0)
        def _():
            dg_ref[...] = jnp.zeros_like(dg_ref)

        dg_ref[...] += jnp.sum(dgp, axis=0, keepdims=True)

    row = pl.BlockSpec((tm, D), lambda i: (i, 0))
    vec = pl.BlockSpec((1, D), lambda i: (0, 0))
    in_specs = [row, vec, row] + ([] if dres is None else [row])
    args = (x, g.reshape(1, D), dy) + (() if dres is None else (dres,))
    return _pcall(
        body, name=name, dims=("arbitrary",), grid=(M // tm,), in_specs=in_specs, out_specs=[row, vec],
        out_shape=[jax.ShapeDtypeStruct((M, D), F32), jax.ShapeDtypeStruct((1, D), F32)],
    )(*args)


def _rope_chunk(t, c, s, half):
    lane = _lane(t.shape)
    swapped = jnp.where((lane % (2 * half)) < half, pltpu.roll(t, LANES - half, 1), pltpu.roll(t, half, 1))
    return t * c + swapped * s


def _rope_tables(positions):
    pos = positions.reshape(-1).astype(F32)[:, None]

    def table(dh, first, copies, sine, fill=0.0):
        half = dh // 2
        inv_freq = ROPE_THETA ** (-jnp.arange(0, dh, 2, dtype=F32) / dh)
        lane = np.arange(LANES)
        inside = (lane >= first) & (lane < first + copies * dh)
        idx = np.where(inside, (lane - first) % half, 0)
        sign = np.where((lane - first) % dh < half, -1.0, 1.0) if sine else np.ones(LANES)
        ang = pos * inv_freq[idx][None, :]
        val = (jnp.sin(ang) if sine else jnp.cos(ang)) * jnp.asarray(sign, F32)[None, :]
        return jnp.where(jnp.asarray(inside)[None, :], val, fill)

    return dict(
        c64=table(HEAD_DIM, 0, 2, False), s64=table(HEAD_DIM, 0, 2, True),
        ck=table(MLA_ROPE, 0, 1, False), sk=table(MLA_ROPE, 0, 1, True),
        cm=jnp.where(jnp.asarray(np.arange(LANES) < MLA_NOPE)[None, :], 1.0, table(MLA_ROPE, MLA_NOPE, 1, False)),
        sm=table(MLA_ROPE, MLA_NOPE, 1, True),
    )


def _attn_steps(mode, n_other, t_self, t_other):
    if mode == "band":
        assert t_self == t_other
        return 2
    return n_other


def _kv_block(mode, qi, kj):
    if mode == "band":
        return jnp.maximum(qi - 1 + kj, 0), (qi + kj) >= 1
    if mode == "causal":
        return jnp.minimum(kj, qi), kj <= qi
    return kj, None


def _q_block(mode, ki, qj, nq):
    if mode == "band":
        return jnp.minimum(ki + qj, nq - 1), (ki + qj) <= nq - 1
    if mode == "causal":
        return jnp.maximum(qj, ki), qj >= ki
    return qj, None


def _mask(mode, max_dist, qpos, kpos):
    d = qpos - kpos
    if mode == "band":
        return (d >= 0) & (d <= max_dist)
    if mode == "causal":
        return d >= 0
    return None


def _when(cond, fn):
    if cond is None:
        fn()
    else:
        pl.when(cond)(fn)


class _Attn:
    def __init__(self, *, T, Tk, G, nh, rep, dqk, dv, tq, tk, mode, scale, qcol, kcol, vcol, ocol, o_width,
                 max_dist=0):
        self.__dict__.update(locals())
        self.nkv = nh // rep
        assert T % tq == 0 and Tk % tk == 0 and nh <= LANES


def _attn_fwd(cfg, q, k, v, *, name, sink=None, out_dtype=F32):
    c = cfg
    nq, nk = c.T // c.tq, c.Tk // c.tk
    steps = _attn_steps(c.mode, nk, c.tq, c.tk)

    def body(*refs):
        if sink is None:
            q_ref, k_ref, v_ref, o_ref, lse_ref, m_scr, l_scr, acc = refs
        else:
            q_ref, k_ref, v_ref, sink_ref, o_ref, lse_ref, m_scr, l_scr, acc = refs
        qi, kj = pl.program_id(1), pl.program_id(2)
        kb, valid = _kv_block(c.mode, qi, kj)

        @pl.when(kj == 0)
        def _():
            if sink is None:
                m_scr[...] = jnp.full_like(m_scr, NEG_BIG)
                l_scr[...] = jnp.zeros_like(l_scr)
            else:
                m_scr[...] = jnp.broadcast_to(sink_ref[...], m_scr.shape)
                l_scr[...] = jnp.ones_like(l_scr)
            acc[...] = jnp.zeros_like(acc)

        def step():
            qpos = qi * c.tq + lax.broadcasted_iota(jnp.int32, (c.tq, c.tk), 0)
            kpos = kb * c.tk + lax.broadcasted_iota(jnp.int32, (c.tq, c.tk), 1)
            mask = _mask(c.mode, c.max_dist, qpos, kpos)
            for j in range(c.nh):
                g = j // c.rep
                s = _dot(q_ref[:, j * c.dqk:(j + 1) * c.dqk], k_ref[:, g * c.dqk:(g + 1) * c.dqk], NT) * c.scale
                if mask is not None:
                    s = jnp.where(mask, s, -jnp.inf)
                m_prev = m_scr[:, j:j + 1]
                m_new = jnp.maximum(m_prev, jnp.max(s, axis=1, keepdims=True))
                alpha = jnp.exp(m_prev - m_new)
                p = jnp.exp(s - m_new)
                l_scr[:, j:j + 1] = alpha * l_scr[:, j:j + 1] + jnp.sum(p, axis=1, keepdims=True)
                acc[:, j * c.dv:(j + 1) * c.dv] = (
                    alpha * acc[:, j * c.dv:(j + 1) * c.dv] + _dot(p, v_ref[:, g * c.dv:(g + 1) * c.dv], NN))
                m_scr[:, j:j + 1] = m_new

        _when(valid, step)

        @pl.when(kj == steps - 1)
        def _():
            for j in range(c.nh):
                o_ref[:, j * c.dv:(j + 1) * c.dv] = (
                    acc[:, j * c.dv:(j + 1) * c.dv] / l_scr[:, j:j + 1]).astype(o_ref.dtype)
            lane = _lane((c.tq, LANES))
            lse_ref[...] = jnp.where(lane < c.nh, m_scr[...] + jnp.log(jnp.maximum(l_scr[...], 1e-37)), 0.0)

    in_specs = [
        pl.BlockSpec((c.tq, c.nh * c.dqk), lambda g, i, j: (i, c.qcol(g))),
        pl.BlockSpec((c.tk, c.nkv * c.dqk), lambda g, i, j: (_kv_block(c.mode, i, j)[0], c.kcol(g))),
        pl.BlockSpec((c.tk, c.nkv * c.dv), lambda g, i, j: (_kv_block(c.mode, i, j)[0], c.vcol(g))),
    ]
    args = [q, k, v]
    if sink is not None:
        in_specs.append(pl.BlockSpec((1, LANES), lambda g, i, j: (0, 0)))
        args.append(sink)
    return _pcall(
        body, name=name, dims=("parallel", "parallel", "arbitrary"), grid=(c.G, nq, steps),
        in_specs=in_specs,
        out_specs=[pl.BlockSpec((c.tq, c.nh * c.dv), lambda g, i, j: (i, c.ocol(g))),
                   pl.BlockSpec((c.tq, LANES), lambda g, i, j: (i, g))],
        out_shape=[jax.ShapeDtypeStruct((c.T, c.o_width), out_dtype),
                   jax.ShapeDtypeStruct((c.T, LANES * c.G), F32)],
        scratch_shapes=[pltpu.VMEM((c.tq, LANES), F32), pltpu.VMEM((c.tq, LANES), F32),
                        pltpu.VMEM((c.tq, c.nh * c.dv), F32)],
    )(*args)


def _attn_delta(cfg, o, do, *, name, w=None, lse=None, sink=None, tm=512):
    c = cfg
    tm = _tile(c.T, tm)
    width = c.nh * c.dv

    def body(*refs):
        refs = list(refs)
        o_ref, do_ref = refs[:2]
        rest = refs[2:]
        w_ref = rest.pop(0) if w is not None else None
        lse_ref, sink_ref = (rest.pop(0), rest.pop(0)) if sink is not None else (None, None)
        d_ref = rest.pop(0)
        prod = o_ref[...].astype(F32) * do_ref[...].astype(F32)
        cols = [jnp.sum(prod[:, j * c.dv:(j + 1) * c.dv], axis=1, keepdims=True) for j in range(c.nh)]
        delta = _cols_to_lanes(cols, tm)
        if w is not None:
            delta = delta * w_ref[...]
        d_ref[...] = delta
        if sink is not None:
            ds_ref = rest.pop(0)

            @pl.when(pl.program_id(1) == 0)
            def _():
                ds_ref[...] = jnp.zeros_like(ds_ref)

            lane = _lane((tm, LANES))
            ps = jnp.where(lane < c.nh, jnp.exp(sink_ref[...] - lse_ref[...]), 0.0)
            ds_ref[...] -= jnp.sum(ps * delta, axis=0, keepdims=True)

    stat = pl.BlockSpec((tm, LANES), lambda g, i: (i, g))
    in_specs = [pl.BlockSpec((tm, width), lambda g, i: (i, c.ocol(g)))] * 2
    args = [o, do]
    out_specs, out_shape = [stat], [jax.ShapeDtypeStruct((c.T, LANES * c.G), F32)]
    if w is not None:
        in_specs.append(stat)
        args.append(w)
    if sink is not None:
        assert c.G == 1
        in_specs += [stat, pl.BlockSpec((1, LANES), lambda g, i: (0, 0))]
        args += [lse, sink]
        out_specs.append(pl.BlockSpec((1, LANES), lambda g, i: (0, 0)))
        out_shape.append(jax.ShapeDtypeStruct((1, LANES), F32))
    out = _pcall(
        body, name=name, dims=("arbitrary", "arbitrary"), grid=(c.G, c.T // tm),
        in_specs=in_specs, out_specs=out_specs, out_shape=out_shape,
    )(*args)
    return out if sink is not None else (out[0], None)


def _attn_dq(cfg, q, k, v, do, lse, delta, *, name, init=None, out_dtype=F32):
    c = cfg
    nq, nk = c.T // c.tq, c.Tk // c.tk
    steps = _attn_steps(c.mode, nk, c.tq, c.tk)
    qw = c.nh * c.dqk

    def body(*refs):
        if init is None:
            q_ref, k_ref, v_ref, do_ref, lse_ref, d_ref, dq_ref, acc = refs
        else:
            q_ref, k_ref, v_ref, do_ref, lse_ref, d_ref, init_ref, dq_ref, acc = refs
        qi, kj = pl.program_id(1), pl.program_id(2)
        kb, valid = _kv_block(c.mode, qi, kj)

        @pl.when(kj == 0)
        def _():
            acc[...] = jnp.zeros_like(acc) if init is None else init_ref[...].astype(F32)

        def step():
            qpos = qi * c.tq + lax.broadcasted_iota(jnp.int32, (c.tq, c.tk), 0)
            kpos = kb * c.tk + lax.broadcasted_iota(jnp.int32, (c.tq, c.tk), 1)
            mask = _mask(c.mode, c.max_dist, qpos, kpos)
            for j in range(c.nh):
                g = j // c.rep
                kh = k_ref[:, g * c.dqk:(g + 1) * c.dqk]
                s = _dot(q_ref[:, j * c.dqk:(j + 1) * c.dqk], kh, NT) * c.scale
                if mask is not None:
                    s = jnp.where(mask, s, -jnp.inf)
                p = jnp.exp(s - lse_ref[:, j:j + 1])
                dp = _dot(do_ref[:, j * c.dv:(j + 1) * c.dv], v_ref[:, g * c.dv:(g + 1) * c.dv], NT)
                ds = p * (dp - d_ref[:, j:j + 1]) * c.scale
                acc[:, j * c.dqk:(j + 1) * c.dqk] += _dot(ds, kh, NN)

        _when(valid, step)

        @pl.when(kj == steps - 1)
        def _():
            dq_ref[...] = acc[...].astype(dq_ref.dtype)

    kvb = lambda i, j: _kv_block(c.mode, i, j)[0]
    qspec = pl.BlockSpec((c.tq, qw), lambda g, i, j: (i, c.qcol(g)))
    stat = pl.BlockSpec((c.tq, LANES), lambda g, i, j: (i, g))
    in_specs = [
        qspec,
        pl.BlockSpec((c.tk, c.nkv * c.dqk), lambda g, i, j: (kvb(i, j), c.kcol(g))),
        pl.BlockSpec((c.tk, c.nkv * c.dv), lambda g, i, j: (kvb(i, j), c.vcol(g))),
        pl.BlockSpec((c.tq, c.nh * c.dv), lambda g, i, j: (i, c.ocol(g))),
        stat, stat,
    ]
    args = [q, k, v, do, lse, delta]
    dq_spec = pl.BlockSpec((c.tq, qw), lambda g, i, j: (i, g))
    if init is not None:
        in_specs.append(dq_spec)
        args.append(init)
    return _pcall(
        body, name=name, dims=("parallel", "parallel", "arbitrary"), grid=(c.G, nq, steps),
        in_specs=in_specs, out_specs=dq_spec,
        out_shape=jax.ShapeDtypeStruct((c.T, c.G * qw), out_dtype),
        scratch_shapes=[pltpu.VMEM((c.tq, qw), F32)],
    )(*args)


def _attn_dkv(cfg, q, k, v, do, lse, delta, *, name, init=None, out_dtype=F32):
    c = cfg
    nq, nk = c.T // c.tq, c.Tk // c.tk
    steps = _attn_steps(c.mode, nq, c.tk, c.tq)
    kw, vw = c.nkv * c.dqk, c.nkv * c.dv

    def body(*refs):
        if init is None:
            q_ref, k_ref, v_ref, do_ref, lse_ref, d_ref, dk_ref, dv_ref, dk_acc, dv_acc = refs
        else:
            q_ref, k_ref, v_ref, do_ref, lse_ref, d_ref, ik_ref, iv_ref, dk_ref, dv_ref, dk_acc, dv_acc = refs
        ki, qj = pl.program_id(1), pl.program_id(2)
        qb, valid = _q_block(c.mode, ki, qj, nq)

        @pl.when(qj == 0)
        def _():
            dk_acc[...] = jnp.zeros_like(dk_acc) if init is None else ik_ref[...].astype(F32)
            dv_acc[...] = jnp.zeros_like(dv_acc) if init is None else iv_ref[...].astype(F32)

        def step():
            kpos = ki * c.tk + lax.broadcasted_iota(jnp.int32, (c.tk, c.tq), 0)
            qpos = qb * c.tq + lax.broadcasted_iota(jnp.int32, (c.tk, c.tq), 1)
            mask = _mask(c.mode, c.max_dist, qpos, kpos)
            lse_t = lse_ref[...].T
            d_t = d_ref[...].T
            for j in range(c.nh):
                g = j // c.rep
                qh = q_ref[:, j * c.dqk:(j + 1) * c.dqk]
                doh = do_ref[:, j * c.dv:(j + 1) * c.dv]
                s_t = _dot(k_ref[:, g * c.dqk:(g + 1) * c.dqk], qh, NT) * c.scale
                if mask is not None:
                    s_t = jnp.where(mask, s_t, -jnp.inf)
                p_t = jnp.exp(s_t - lse_t[j:j + 1, :])
                dv_acc[:, g * c.dv:(g + 1) * c.dv] += _dot(p_t, doh, NN)
                dp_t = _dot(v_ref[:, g * c.dv:(g + 1) * c.dv], doh, NT)
                ds_t = p_t * (dp_t - d_t[j:j + 1, :]) * c.scale
                dk_acc[:, g * c.dqk:(g + 1) * c.dqk] += _dot(ds_t, qh, NN)

        _when(valid, step)

        @pl.when(qj == steps - 1)
        def _():
            dk_ref[...] = dk_acc[...].astype(dk_ref.dtype)
            dv_ref[...] = dv_acc[...].astype(dv_ref.dtype)

    qbi = lambda i, j: _q_block(c.mode, i, j, nq)[0]
    stat = pl.BlockSpec((c.tq, LANES), lambda g, i, j: (qbi(i, j), g))
    in_specs = [
        pl.BlockSpec((c.tq, c.nh * c.dqk), lambda g, i, j: (qbi(i, j), c.qcol(g))),
        pl.BlockSpec((c.tk, kw), lambda g, i, j: (i, c.kcol(g))),
        pl.BlockSpec((c.tk, vw), lambda g, i, j: (i, c.vcol(g))),
        pl.BlockSpec((c.tq, c.nh * c.dv), lambda g, i, j: (qbi(i, j), c.ocol(g))),
        stat, stat,
    ]
    args = [q, k, v, do, lse, delta]
    dk_spec = pl.BlockSpec((c.tk, kw), lambda g, i, j: (i, g))
    dv_spec = pl.BlockSpec((c.tk, vw), lambda g, i, j: (i, g))
    if init is not None:
        in_specs += [dk_spec, dv_spec]
        args += list(init)
    return _pcall(
        body, name=name, dims=("parallel", "parallel", "arbitrary"), grid=(c.G, nk, steps),
        in_specs=in_specs, out_specs=[dk_spec, dv_spec],
        out_shape=[jax.ShapeDtypeStruct((c.Tk, c.G * kw), out_dtype),
                   jax.ShapeDtypeStruct((c.Tk, c.G * vw), out_dtype)],
        scratch_shapes=[pltpu.VMEM((c.tk, kw), F32), pltpu.VMEM((c.tk, vw), F32)],
    )(*args)


TN = (((0,), (0,)), ((), ()))


def _band_mask(c, i):
    key = lax.broadcasted_iota(jnp.int32, (2 * BLOCK, BLOCK), 0)
    qry = lax.broadcasted_iota(jnp.int32, (2 * BLOCK, BLOCK), 1)
    d = BLOCK + qry - key
    return (d >= 0) & (d <= c.max_dist) & ((key >= BLOCK) | (i > 0))


def _head_pairs(c):
    return c.rep == 1 and c.dqk == c.dv == LANES // 2 and c.nh % 2 == 0


def _block_diagonal(pair):
    lane = _lane(pair.shape)
    zero = jnp.zeros_like(pair)
    return jnp.concatenate([jnp.where(lane < LANES // 2, pair, zero), jnp.where(lane >= LANES // 2, pair, zero)], axis=0)


def _own_blocks(t):
    n = t.shape[1] // 2
    rows = lax.broadcasted_iota(jnp.int32, (LANES, n), 0)
    return jnp.where(rows < LANES // 2, t[:, :n], t[:, n:])


def _rows_to_stats(rows, n):
    return jnp.concatenate(rows + [jnp.zeros((LANES - len(rows), n), F32)], axis=0).T


def _band_fwd(cfg, q, k, v, *, name, sink=None, out_dtype=F32):
    c = cfg
    assert c.mode == "band" and c.tq == c.tk == BLOCK and c.T == c.Tk
    nq = c.T // BLOCK

    def body(*refs):
        if sink is None:
            q_ref, kp_ref, kc_ref, vp_ref, vc_ref, o_ref, lse_ref = refs
        else:
            q_ref, kp_ref, kc_ref, vp_ref, vc_ref, sink_ref, o_ref, lse_ref = refs
        mask = _band_mask(c, pl.program_id(1))
        k2 = jnp.concatenate([kp_ref[...], kc_ref[...]], axis=0)
        v2 = jnp.concatenate([vp_ref[...], vc_ref[...]], axis=0)
        lses = []
        if _head_pairs(c):
            mask2 = jnp.concatenate([mask, mask], axis=1)
            pair_lanes = [slice(pc * LANES, (pc + 1) * LANES) for pc in range(c.nh // 2)]
            score = lambda sl: _dot(k2[:, sl], _block_diagonal(q_ref[:, sl]), NT)
            ahead, behind = score(pair_lanes[0]), None

            def finish(entry):
                sl, o_t, l = entry
                o_ref[:, sl] = _own_blocks(o_t / l).T.astype(o_ref.dtype)

            for pc, sl in enumerate(pair_lanes):
                s = ahead * c.scale
                if pc + 1 < len(pair_lanes):
                    ahead = score(pair_lanes[pc + 1])
                s = jnp.where(mask2, s, -jnp.inf)
                m = jnp.max(s, axis=0, keepdims=True)
                p = jnp.exp(s - m)
                l = jnp.sum(p, axis=0, keepdims=True)
                if behind is not None:
                    finish(behind)
                behind = (sl, _dot(v2[:, sl], p, TN), l)
                lse = m + jnp.log(l)
                lses += [lse[:, :BLOCK], lse[:, BLOCK:]]
            finish(behind)
        heads = [] if _head_pairs(c) else list(range(c.nh))
        score_of = lambda j: _dot(k2[:, (j // c.rep) * c.dqk:(j // c.rep + 1) * c.dqk],
                                  q_ref[:, j * c.dqk:(j + 1) * c.dqk], NT)
        ahead = score_of(0) if heads else None
        for j in heads:
            g = j // c.rep
            s = ahead * c.scale
            if j + 1 < c.nh:
                ahead = score_of(j + 1)
            s = jnp.where(mask, s, -jnp.inf)
            m = jnp.max(s, axis=0, keepdims=True)
            if sink is not None:
                sk = sink_ref[:, j:j + 1]
                m = jnp.maximum(m, sk)
            p = jnp.exp(s - m)
            l = jnp.sum(p, axis=0, keepdims=True)
            if sink is not None:
                l = l + jnp.exp(sk - m)
            o_t = _dot(v2[:, g * c.dv:(g + 1) * c.dv], p, TN)
            o_ref[:, j * c.dv:(j + 1) * c.dv] = (o_t / l).T.astype(o_ref.dtype)
            lses.append(m + jnp.log(l))
        lse_ref[...] = _rows_to_stats(lses, BLOCK)

    prev = lambda i: jnp.maximum(i - 1, 0)
    kw, vw = c.nkv * c.dqk, c.nkv * c.dv
    in_specs = [
        pl.BlockSpec((BLOCK, c.nh * c.dqk), lambda g, i: (i, c.qcol(g))),
        pl.BlockSpec((BLOCK, kw), lambda g, i: (prev(i), c.kcol(g))),
        pl.BlockSpec((BLOCK, kw), lambda g, i: (i, c.kcol(g))),
        pl.BlockSpec((BLOCK, vw), lambda g, i: (prev(i), c.vcol(g))),
        pl.BlockSpec((BLOCK, vw), lambda g, i: (i, c.vcol(g))),
    ]
    args = [q, k, k, v, v]
    if sink is not None:
        in_specs.append(pl.BlockSpec((1, LANES), lambda g, i: (0, 0)))
        args.append(sink)
    return _pcall(
        body, name=name, dims=("parallel", "parallel"), grid=(c.G, nq), in_specs=in_specs,
        out_specs=[pl.BlockSpec((BLOCK, c.nh * c.dv), lambda g, i: (i, c.ocol(g))),
                   pl.BlockSpec((BLOCK, LANES), lambda g, i: (i, g))],
        out_shape=[jax.ShapeDtypeStruct((c.T, c.o_width), out_dtype),
                   jax.ShapeDtypeStruct((c.T, LANES * c.G), F32)],
    )(*args)


def _band_bwd(cfg, q, k, v, do, lse, delta, *, name):
    c = cfg
    assert c.mode == "band" and c.tq == c.tk == BLOCK and c.T == c.Tk
    nq = c.T // BLOCK
    qw, kw, vw = c.nh * c.dqk, c.nkv * c.dqk, c.nkv * c.dv

    def body(q_ref, kp_ref, kc_ref, vp_ref, vc_ref, do_ref, lse_ref, d_ref, dq_ref, dk_ref, dv_ref, dk_c, dv_c):
        n = pl.program_id(1)

        @pl.when(n == 0)
        def _():
            dk_c[...] = jnp.zeros_like(dk_c)
            dv_c[...] = jnp.zeros_like(dv_c)

        @pl.when(n < nq)
        def _():
            mask = _band_mask(c, n)
            k2 = jnp.concatenate([kp_ref[...], kc_ref[...]], axis=0)
            v2 = jnp.concatenate([vp_ref[...], vc_ref[...]], axis=0)
            lse_t, d_t = lse_ref[...].T, d_ref[...].T
            if _head_pairs(c):
                mask2 = jnp.concatenate([mask, mask], axis=1)
                pair_lanes = [slice(pc * LANES, (pc + 1) * LANES) for pc in range(c.nh // 2)]

                def first(sl):
                    q_bd, do_bd = _block_diagonal(q_ref[:, sl]), _block_diagonal(do_ref[:, sl])
                    return q_bd, do_bd, k2[:, sl], _dot(k2[:, sl], q_bd, NT), _dot(v2[:, sl], do_bd, NT)

                def finish(entry):
                    sl, dq_t, dv_pair, dk_pair = entry
                    dq_ref[:, sl] = _own_blocks(dq_t).T
                    dk_ref[:, sl] = dk_c[:, sl] + dk_pair[:BLOCK]
                    dv_ref[:, sl] = dv_c[:, sl] + dv_pair[:BLOCK]
                    dk_c[:, sl] = dk_pair[BLOCK:]
                    dv_c[:, sl] = dv_pair[BLOCK:]

                ahead, behind = first(pair_lanes[0]), None
                for pc, sl in enumerate(pair_lanes):
                    q_bd, do_bd, kp, s, dp = ahead
                    if pc + 1 < len(pair_lanes):
                        ahead = first(pair_lanes[pc + 1])
                    both = lambda t: jnp.concatenate([t[2 * pc:2 * pc + 1, :], t[2 * pc + 1:2 * pc + 2, :]], axis=1)
                    p = jnp.exp(jnp.where(mask2, s * c.scale, -jnp.inf) - both(lse_t))
                    ds = p * (dp - both(d_t)) * c.scale
                    entry = (sl, _dot(kp, ds, TN), _dot(p, do_bd, NN), _dot(ds, q_bd, NN))
                    if behind is not None:
                        finish(behind)
                    behind = entry
                finish(behind)
                return
            dk2, dv2 = [None] * c.nkv, [None] * c.nkv

            def first_of(j):
                g = j // c.rep
                qh, doh = q_ref[:, j * c.dqk:(j + 1) * c.dqk], do_ref[:, j * c.dv:(j + 1) * c.dv]
                kh = k2[:, g * c.dqk:(g + 1) * c.dqk]
                return qh, doh, kh, _dot(kh, qh, NT), _dot(v2[:, g * c.dv:(g + 1) * c.dv], doh, NT)

            ahead = first_of(0)
            for j in range(c.nh):
                g = j // c.rep
                qh, doh, kh, s, dp = ahead
                if j + 1 < c.nh:
                    ahead = first_of(j + 1)
                p = jnp.exp(jnp.where(mask, s * c.scale, -jnp.inf) - lse_t[j:j + 1, :])
                ds = p * (dp - d_t[j:j + 1, :]) * c.scale
                dq_ref[:, j * c.dqk:(j + 1) * c.dqk] = _dot(kh, ds, TN).T
                dvh, dkh = _dot(p, doh, NN), _dot(ds, qh, NN)
                dv2[g] = dvh if dv2[g] is None else dv2[g] + dvh
                dk2[g] = dkh if dk2[g] is None else dk2[g] + dkh
            for g in range(c.nkv):
                ks, vs = slice(g * c.dqk, (g + 1) * c.dqk), slice(g * c.dv, (g + 1) * c.dv)
                dk_ref[:, ks] = dk_c[:, ks] + dk2[g][:BLOCK]
                dv_ref[:, vs] = dv_c[:, vs] + dv2[g][:BLOCK]
                dk_c[:, ks] = dk2[g][BLOCK:]
                dv_c[:, vs] = dv2[g][BLOCK:]

        @pl.when(n == nq)
        def _():
            dk_ref[...] = dk_c[...]
            dv_ref[...] = dv_c[...]

    cur = lambda n: jnp.minimum(n, nq - 1)
    prev = lambda n: jnp.maximum(cur(n) - 1, 0)
    out_blk = lambda n: jnp.maximum(n - 1, 0)
    stat = pl.BlockSpec((BLOCK, LANES), lambda g, n: (cur(n), g))
    dq_spec = pl.BlockSpec((BLOCK, qw), lambda g, n: (cur(n), g))
    dk_spec = pl.BlockSpec((BLOCK, kw), lambda g, n: (out_blk(n), g))
    dv_spec = pl.BlockSpec((BLOCK, vw), lambda g, n: (out_blk(n), g))
    in_specs = [
        pl.BlockSpec((BLOCK, qw), lambda g, n: (cur(n), c.qcol(g))),
        pl.BlockSpec((BLOCK, kw), lambda g, n: (prev(n), c.kcol(g))),
        pl.BlockSpec((BLOCK, kw), lambda g, n: (cur(n), c.kcol(g))),
        pl.BlockSpec((BLOCK, vw), lambda g, n: (prev(n), c.vcol(g))),
        pl.BlockSpec((BLOCK, vw), lambda g, n: (cur(n), c.vcol(g))),
        pl.BlockSpec((BLOCK, c.nh * c.dv), lambda g, n: (cur(n), c.ocol(g))),
        stat, stat,
    ]
    return _pcall(
        body, name=name, dims=("parallel", "arbitrary"), grid=(c.G, nq + 1), in_specs=in_specs,
        out_specs=[dq_spec, dk_spec, dv_spec],
        out_shape=[_sds((c.T, c.G * qw)), _sds((c.T, c.G * kw)), _sds((c.T, c.G * vw))],
        scratch_shapes=[pltpu.VMEM((BLOCK, kw), F32), pltpu.VMEM((BLOCK, vw), F32)],
    )(q, k, k, v, v, do, lse, delta)


def _causal_pairs(n, kv_major):
    pairs =[(i, j) for j in range(n) for i in range(j, n)] if kv_major else [(i, j) for i in range(n) for j in range(i + 1)]
    return jnp.asarray(np.array([p[0] for p in pairs], np.int32)), jnp.asarray(np.array([p[1] for p in pairs], np.int32))


def _causal_mask(t):
    return lax.broadcasted_iota(jnp.int32, (t, t), 0) >= lax.broadcasted_iota(jnp.int32, (t, t), 1)


def _carrying(body, n_in, n_out, n_scratch, grid, carry):
    if carry is None:
        return body
    G, P = grid

    def wrapped(*refs):
        refs = list(refs)
        prefetch, refs = refs[:2], refs[2:]
        ins, src = refs[:n_in], refs[n_in]
        outs, out = refs[n_in + 1:n_in + 1 + n_out], refs[n_in + 1 + n_out]
        scratch, sems = refs[n_in + 2 + n_out:n_in + 2 + n_out + n_scratch], refs[n_in + 2 + n_out + n_scratch:]
        step = pl.program_id(0) * P + pl.program_id(1)
        carry.run([src, out] + sems, step, G * P, at_end=False)
        body(*prefetch, *ins, *outs, *scratch)
        carry.run([src, out] + sems, step, G * P, at_end=True)

    return wrapped


def _carry_specs(carry):
    if carry is None:
        return [], [], [], [], []
    any_space = pl.BlockSpec(memory_space=pl.ANY)
    return [any_space], [any_space], [carry.out_shape], list(carry.sems), [carry.src]


def _causal_fwd(cfg, q, k, v, *, name, out_dtype=F32, carry=None):
    c = cfg
    assert c.mode == "causal" and c.tq == c.tk and c.T == c.Tk
    t, n = c.tq, c.T // c.tq
    qi_tab, kj_tab = _causal_pairs(n, kv_major=False)
    n_pairs = int(qi_tab.shape[0])

    def body(qi_ref, kj_ref, q_ref, k_ref, v_ref, o_ref, lse_ref, m_scr, l_scr, acc):
        pair = pl.program_id(1)
        qi, kj = qi_ref[pair], kj_ref[pair]

        @pl.when(kj == 0)
        def _():
            m_scr[...] = jnp.full_like(m_scr, NEG_BIG)
            l_scr[...] = jnp.zeros_like(l_scr)
            acc[...] = jnp.zeros_like(acc)

        def step(diagonal):
            mask = None
            if diagonal:
                mask = lax.broadcasted_iota(jnp.int32, (t, t), 1) >= lax.broadcasted_iota(jnp.int32, (t, t), 0)
            scores = [_dot(k_ref[:, (j // c.rep) * c.dqk:(j // c.rep + 1) * c.dqk],
                           q_ref[:, j * c.dqk:(j + 1) * c.dqk], NT) for j in range(c.nh)]
            for j in range(c.nh):
                g = j // c.rep
                s = scores[j] * c.scale
                if diagonal:
                    s = jnp.where(mask, s, -jnp.inf)
                m_prev = m_scr[j]
                m_new = jnp.maximum(m_prev, jnp.max(s, axis=0, keepdims=True))
                alpha = jnp.exp(m_prev - m_new)
                p = jnp.exp(s - m_new)
                l_scr[j] = alpha * l_scr[j] + jnp.sum(p, axis=0, keepdims=True)
                acc[j] = alpha * acc[j] + _dot(v_ref[:, g * c.dv:(g + 1) * c.dv], p, TN)
                m_scr[j] = m_new

        pl.when(kj == qi)(lambda: step(True))
        pl.when(kj != qi)(lambda: step(False))

        @pl.when(kj == qi)
        def _():
            rows = []
            for j in range(c.nh):
                o_ref[:, j * c.dv:(j + 1) * c.dv] = (acc[j] / l_scr[j]).T.astype(o_ref.dtype)
                rows.append(m_scr[j] + jnp.log(l_scr[j]))
            rows.append(jnp.zeros((LANES - c.nh, t), F32))
            lse_ref[...] = jnp.concatenate(rows, axis=0).T

    x_in, x_out, x_shapes, x_scratch, x_args = _carry_specs(carry)
    grid_spec = pltpu.PrefetchScalarGridSpec(
        num_scalar_prefetch=2, grid=(c.G, n_pairs),
        in_specs=[pl.BlockSpec((t, c.nh * c.dqk), lambda g, p, qi, kj: (qi[p], c.qcol(g))),
                  pl.BlockSpec((t, c.nkv * c.dqk), lambda g, p, qi, kj: (kj[p], c.kcol(g))),
                  pl.BlockSpec((t, c.nkv * c.dv), lambda g, p, qi, kj: (kj[p], c.vcol(g)))] + x_in,
        out_specs=[pl.BlockSpec((t, c.nh * c.dv), lambda g, p, qi, kj: (qi[p], c.ocol(g))),
                   pl.BlockSpec((t, LANES), lambda g, p, qi, kj: (qi[p], g))] + x_out,
        scratch_shapes=[pltpu.VMEM((c.nh, 1, t), F32), pltpu.VMEM((c.nh, 1, t), F32),
                        pltpu.VMEM((c.nh, c.dv, t), F32)] + x_scratch)
    return _pcall(
        _carrying(body, 3, 2, 3, (c.G, n_pairs), carry), name=name,
        dims=("arbitrary", "arbitrary") if carry is not None else ("parallel", "arbitrary"), grid_spec=grid_spec,
        out_shape=[jax.ShapeDtypeStruct((c.T, c.o_width), out_dtype),
                   jax.ShapeDtypeStruct((c.T, LANES * c.G), F32)] + x_shapes,
    )(qi_tab, kj_tab, q, k, v, *x_args)


def _causal_bwd(cfg, q, k, v, o, do, lse, *, name, stat_heads=None, carry=None):
    c = cfg
    assert c.mode == "causal" and c.tq == c.tk and c.T == c.Tk
    t, n = c.tq, c.T // c.tq
    qw, kw, vw = c.nh * c.dqk, c.nkv * c.dqk, c.nkv * c.dv
    qi_tab, kj_tab = _causal_pairs(n, kv_major=True)
    share = (stat_heads or c.nh) // c.nh
    assert share * c.nh == (stat_heads or c.nh)

    def body(qi_ref, kj_ref, q_ref, k_ref, v_ref, o_ref, do_ref, lse_ref, dq_ref, dk_ref, dv_ref, dk_acc, dv_acc):
        pair = pl.program_id(1)
        qi, kj = qi_ref[pair], kj_ref[pair]

        @pl.when(pair == 0)
        def _():
            dq_ref[...] = jnp.zeros_like(dq_ref)

        @pl.when(qi == kj)
        def _():
            dk_acc[...] = jnp.zeros_like(dk_acc)
            dv_acc[...] = jnp.zeros_like(dv_acc)

        rows = pl.ds(pl.multiple_of(qi * t, t), t)

        def lse_col(j):
            col = lse_ref[:, j:j + 1]
            for b in range(1, share):
                col = jnp.where(pl.program_id(0) % share == b, lse_ref[:, b * c.nh + j:b * c.nh + j + 1], col)
            return col

        def step(diagonal):
            mask = _causal_mask(t) if diagonal else None
            for j in range(c.nh):
                g = j // c.rep
                qs, ks, vs = (slice(j * c.dqk, (j + 1) * c.dqk), slice(g * c.dqk, (g + 1) * c.dqk),
                              slice(g * c.dv, (g + 1) * c.dv))
                qh, doh, kh = q_ref[:, qs], do_ref[:, j * c.dv:(j + 1) * c.dv], k_ref[:, ks]
                s = _dot(qh, kh, NT) * c.scale
                if diagonal:
                    s = jnp.where(mask, s, -jnp.inf)
                p = jnp.exp(s - lse_col(j))
                delta = jnp.sum(doh.astype(F32) * o_ref[:, j * c.dv:(j + 1) * c.dv].astype(F32), axis=1, keepdims=True)
                ds = p * (_dot(doh, v_ref[:, vs], NT) - delta) * c.scale
                dq_ref[rows, qs] += _dot(ds, kh, NN)
                dv_acc[g] += _dot(doh, p, TN)
                dk_acc[g] += _dot(qh, ds, TN)

        pl.when(qi == kj)(lambda: step(True))
        pl.when(qi != kj)(lambda: step(False))

        @pl.when(qi == n - 1)
        def _():
            for g in range(c.nkv):
                dk_ref[:, g * c.dqk:(g + 1) * c.dqk] = dk_acc[g].T
                dv_ref[:, g * c.dv:(g + 1) * c.dv] = dv_acc[g].T

    stat = pl.BlockSpec((t, LANES), lambda g, p, qi, kj: (qi[p], g // share))
    o_spec = pl.BlockSpec((t, c.nh * c.dv), lambda g, p, qi, kj: (qi[p], c.ocol(g)))
    n_pairs = int(qi_tab.shape[0])
    x_in, x_out, x_shapes, x_scratch, x_args = _carry_specs(carry)
    grid_spec = pltpu.PrefetchScalarGridSpec(
        num_scalar_prefetch=2, grid=(c.G, n_pairs),
        in_specs=[pl.BlockSpec((t, qw), lambda g, p, qi, kj: (qi[p], c.qcol(g))),
                  pl.BlockSpec((t, kw), lambda g, p, qi, kj: (kj[p], c.kcol(g))),
                  pl.BlockSpec((t, vw), lambda g, p, qi, kj: (kj[p], c.vcol(g))),
                  o_spec, o_spec, stat] + x_in,
        out_specs=[pl.BlockSpec((c.T, qw), lambda g, p, qi, kj: (0, g)),
                   pl.BlockSpec((t, kw), lambda g, p, qi, kj: (kj[p], g)),
                   pl.BlockSpec((t, vw), lambda g, p, qi, kj: (kj[p], g))] + x_out,
        scratch_shapes=[pltpu.VMEM((c.nkv, c.dqk, t), F32), pltpu.VMEM((c.nkv, c.dv, t), F32)] + x_scratch)
    return _pcall(
        _carrying(body, 6, 3, 2, (c.G, n_pairs), carry), name=name,
        dims=("arbitrary", "arbitrary") if carry is not None else ("parallel", "arbitrary"), grid_spec=grid_spec,
        out_shape=[_sds((c.T, c.G * qw)), _sds((c.T, c.G * kw)), _sds((c.T, c.G * vw))] + x_shapes,
    )(qi_tab, kj_tab, q, k, v, o, do, lse, *x_args)


def _rowwise(body, ins, outs, *, name, rows, tm=512, accs=(), scratch=()):
    tm = _row_tile(rows, tm)

    def spec(a):
        if a.shape[0] == 1:
            return pl.BlockSpec((1, a.shape[1]), lambda i: (0, 0))
        d = rows // a.shape[0]
        assert d * a.shape[0] == rows and tm % d == 0
        return pl.BlockSpec((tm // d, a.shape[1]), lambda i: (i, 0))

    return _pcall(
        functools.partial(body, tm), name=name, dims=("arbitrary" if accs else "parallel",), grid=(rows // tm,),
        in_specs=[spec(a) for a in ins], out_specs=[spec(a) for a in outs], out_shape=list(outs),
        scratch_shapes=list(scratch),
    )(*ins)


def _sds(shape, dtype=F32):
    return jax.ShapeDtypeStruct(shape, dtype)


def _acc_rows(ref, val):
    @pl.when(pl.program_id(0) == 0)
    def _():
        ref[...] = jnp.zeros_like(ref)

    ref[...] += jnp.sum(val, axis=0, keepdims=True)


Z_QA, Z_KA, Z_VA, Z_CQ, Z_CKV, Z_KR, Z_END = 0, 512, 640, 768, 1152, 1408, 1536


def _l0_prep(z, tabs, q_norm, kv_norm, *, name):
    S = z.shape[0]

    def body(tm, z_ref, c64, s64, ck, sk, gq, gkv, qa_o, ka_o, va_o, cq_o, ckv_o, kr_o):
        for i in range(4):
            sl = slice(Z_QA + i * LANES, Z_QA + (i + 1) * LANES)
            qa_o[:, i * LANES:(i + 1) * LANES] = _rope_chunk(z_ref[:, sl], c64[...], s64[...], 32).astype(qa_o.dtype)
        ka_o[...] = _rope_chunk(z_ref[:, Z_KA:Z_VA], c64[...], s64[...], 32).astype(ka_o.dtype)
        va_o[...] = z_ref[:, Z_VA:Z_CQ].astype(va_o.dtype)
        cq_o[...] = (_rms_parts(z_ref[:, Z_CQ:Z_CKV])[0] * gq[...]).astype(cq_o.dtype)
        ckv_o[...] = (_rms_parts(z_ref[:, Z_CKV:Z_KR])[0] * gkv[...]).astype(ckv_o.dtype)
        kr_o[...] = _rope_chunk(z_ref[:, Z_KR:Z_END], ck[...], sk[...], 16)

    outs = [_sds((S, 512), MXU_DTYPE), _sds((S, 128), MXU_DTYPE), _sds((S, 128), MXU_DTYPE),
            _sds((S, MLA_Q_RANK), MXU_DTYPE), _sds((S, MLA_KV_RANK), MXU_DTYPE), _sds((S, LANES))]
    ins = [z, tabs["c64"], tabs["s64"], tabs["ck"], tabs["sk"], q_norm.reshape(1, -1), kv_norm.reshape(1, -1)]
    return _rowwise(body, ins, outs, name=name, rows=S)


def _l0_prep_bwd(z, tabs, q_norm, kv_norm, dqa, dka, dva, dcq, dckv, dkr, *, name):
    S = z.shape[0]

    def body(tm, z_ref, c64, s64, ck, sk, gq, gkv, dqa_r, dka_r, dva_r, dcq_r, dckv_r, dkr_r, dz_o, dgq_o, dgkv_o):
        for i in range(4):
            sl = slice(i * LANES, (i + 1) * LANES)
            dz_o[:, sl] = _rope_chunk(dqa_r[:, sl].astype(F32), c64[...], -s64[...], 32).astype(dz_o.dtype)
        dz_o[:, Z_KA:Z_VA] = _rope_chunk(dka_r[...].astype(F32), c64[...], -s64[...], 32).astype(dz_o.dtype)
        dz_o[:, Z_VA:Z_CQ] = dva_r[...].astype(dz_o.dtype)
        dx, dgp = _rms_bwd_rows(z_ref[:, Z_CQ:Z_CKV], gq[...], dcq_r[...].astype(F32))
        dz_o[:, Z_CQ:Z_CKV] = dx.astype(dz_o.dtype)
        _acc_rows(dgq_o, dgp)
        dx, dgp = _rms_bwd_rows(z_ref[:, Z_CKV:Z_KR], gkv[...], dckv_r[...].astype(F32))
        dz_o[:, Z_CKV:Z_KR] = dx.astype(dz_o.dtype)
        _acc_rows(dgkv_o, dgp)
        dz_o[:, Z_KR:Z_END] = _rope_chunk(dkr_r[...], ck[...], -sk[...], 16).astype(dz_o.dtype)

    outs = [_sds((S, Z_END), MXU_DTYPE), _sds((1, MLA_Q_RANK)), _sds((1, MLA_KV_RANK))]
    ins = [z, tabs["c64"], tabs["s64"], tabs["ck"], tabs["sk"], q_norm.reshape(1, -1), kv_norm.reshape(1, -1),
           dqa, dka, dva, dcq, dckv, dkr]
    return _rowwise(body, ins, outs, name=name, rows=S, accs=(1, 2))


def _mla_prep(qb, kvb, kr, tabs, *, name):
    S = qb.shape[0]

    def body(tm, qb_r, kvb_r, kr_r, cm, sm, q_o, k_o, v_o):
        lane = _lane((tm, LANES))
        kr_at_64 = pltpu.roll(kr_r[...], 64, 1)
        for h in range(MLA_HEADS):
            sl = slice(h * LANES, (h + 1) * LANES)
            q_o[:, sl] = _rope_chunk(qb_r[:, sl], cm[...], sm[...], 16).astype(q_o.dtype)
            k_o[:, sl] = jnp.where(lane < 64, kvb_r[:, sl], kr_at_64).astype(k_o.dtype)
        for p in range(MLA_HEADS // 2):
            even = pltpu.roll(kvb_r[:, (2 * p) * LANES:(2 * p + 1) * LANES], 64, 1)
            odd = kvb_r[:, (2 * p + 1) * LANES:(2 * p + 2) * LANES]
            v_o[:, p * LANES:(p + 1) * LANES] = jnp.where(lane < 64, even, odd).astype(v_o.dtype)

    outs = [_sds((S, 1024), MXU_DTYPE), _sds((S, 1024), MXU_DTYPE), _sds((S, 512), MXU_DTYPE)]
    return _rowwise(body, [qb, kvb, kr, tabs["cm"], tabs["sm"]], outs, name=name, rows=S)


def _mla_prep_bwd(dq, dk, dv, tabs, *, name):
    S = dq.shape[0]

    def body(tm, dq_r, dk_r, dv_r, cm, sm, dqb_o, dkvb_o, dkr_o):
        lane = _lane((tm, LANES))
        dkr = jnp.zeros((tm, LANES), F32)
        for h in range(MLA_HEADS):
            sl = slice(h * LANES, (h + 1) * LANES)
            dqb_o[:, sl] = _rope_chunk(dq_r[:, sl].astype(F32), cm[...], -sm[...], 16).astype(dqb_o.dtype)
            dkh = dk_r[:, sl].astype(F32)
            dvp = dv_r[:, (h // 2) * LANES:(h // 2 + 1) * LANES].astype(F32)
            dvh = pltpu.roll(dvp, 64, 1) if h % 2 == 0 else dvp
            dkvb_o[:, sl] = jnp.where(lane < 64, dkh, dvh).astype(dkvb_o.dtype)
            dkr = dkr + pltpu.roll(dkh, 64, 1)
        dkr_o[...] = jnp.where(lane < MLA_ROPE, dkr, 0.0)

    outs = [_sds((S, 1024), MXU_DTYPE), _sds((S, 1024), MXU_DTYPE), _sds((S, LANES))]
    return _rowwise(body, [dq, dk, dv, tabs["cm"], tabs["sm"]], outs, name=name, rows=S)


DILATIONS = tuple(d for _, d in DIL_PATTERNS)
QKV_CHUNKS = 8


def _to_branch(nat, c0, chunks, out_ref, d, rows):
    width = chunks * LANES
    for r in range(d):
        tok = pl.ds(r, rows // d, stride=d) if d > 1 else slice(None)
        for c in range(chunks):
            out_ref[:, r * width + c * LANES:r * width + (c + 1) * LANES] = nat[c0 + c, tok, :].astype(out_ref.dtype)


def _from_branch(in_ref, nat, c0, chunks, d, rows, add=False):
    width = chunks * LANES
    for r in range(d):
        tok = pl.ds(r, rows // d, stride=d) if d > 1 else slice(None)
        for c in range(chunks):
            val = in_ref[:, r * width + c * LANES:r * width + (c + 1) * LANES].astype(F32)
            nat[c0 + c, tok, :] = nat[c0 + c, tok, :] + val if add else val


def _branch_sds(S, width, d, dtype):
    return _sds((S // d, d * width), dtype)


def _l1_prep(qkv, tabs, *, name):
    S = qkv.shape[0]

    def body(tm, x_r, c64, s64, *rest):
        outs, nat = rest[:-1], rest[-1]
        for i in range(QKV_CHUNKS):
            sl = slice(i * LANES, (i + 1) * LANES)
            nat[i] = _rope_chunk(x_r[:, sl], c64[...], s64[...], 32)
            nat[QKV_CHUNKS + i] = _rope_chunk(x_r[:, 1024 + i * LANES:1024 + (i + 1) * LANES], c64[...], s64[...], 32)
            nat[2 * QKV_CHUNKS + i] = x_r[:, 2048 + i * LANES:2048 + (i + 1) * LANES]
        for b, d in enumerate(DILATIONS):
            for t in range(3):
                _to_branch(nat, t * QKV_CHUNKS, QKV_CHUNKS, outs[3 * b + t], d, tm)

    outs = [_branch_sds(S, 1024, d, MXU_DTYPE) for d in DILATIONS for _ in range(3)]
    got = _rowwise(body, [qkv, tabs["c64"], tabs["s64"]], outs, name=name, rows=S,
                   scratch=[pltpu.VMEM((3 * QKV_CHUNKS, _row_tile(S, 512), LANES), F32)])
    return {d: tuple(got[3 * b:3 * b + 3]) for b, d in enumerate(DILATIONS)}


def _l1_prep_bwd(grads, tabs, *, name):
    S = grads[1][0].shape[0]

    def body(tm, *rest):
        ins, (c64, s64, o, nat) = rest[:9], rest[9:]
        for b, d in enumerate(DILATIONS):
            for t in range(3):
                _from_branch(ins[3 * b + t], nat, t * QKV_CHUNKS, QKV_CHUNKS, d, tm, add=b > 0)
        for i in range(QKV_CHUNKS):
            sl = slice(i * LANES, (i + 1) * LANES)
            o[:, sl] = _rope_chunk(nat[i], c64[...], -s64[...], 32).astype(o.dtype)
            o[:, 1024 + i * LANES:1024 + (i + 1) * LANES] = _rope_chunk(
                nat[QKV_CHUNKS + i], c64[...], -s64[...], 32).astype(o.dtype)
            o[:, 2048 + i * LANES:2048 + (i + 1) * LANES] = nat[2 * QKV_CHUNKS + i].astype(o.dtype)

    ins = [g for d in DILATIONS for g in grads[d]] + [tabs["c64"], tabs["s64"]]
    return _rowwise(body, ins, [_sds((S, 3072), MXU_DTYPE)], name=name, rows=S, tm=256,
                    scratch=[pltpu.VMEM((3 * QKV_CHUNKS, _row_tile(S, 256), LANES), F32)])[0]


def _sigmoid(x):
    return 1.0 / (1.0 + jnp.exp(-x))


FFN_ROW_TILE, FFN_COL_TILE = 512, 1408


def _gate_up(h, w_gate, w_up, *, name):
    (M, K), N = h.shape, w_gate.shape[1]
    tm, tn = _tile(M, FFN_ROW_TILE), _tile(N, FFN_COL_TILE)

    def body(h_ref, wg_ref, wu_ref, g_ref, u_ref, a_ref):
        g = _dot(h_ref[...], wg_ref[...], NN)
        u = _dot(h_ref[...], wu_ref[...], NN)
        g_ref[...] = g
        u_ref[...] = u
        a_ref[...] = (g * _sigmoid(g) * u).astype(a_ref.dtype)

    w_spec = pl.BlockSpec((K, tn), lambda j, i: (0, j))
    o_spec = pl.BlockSpec((tm, tn), lambda j, i: (i, j))
    return _pcall(
        body, name=name, dims=("parallel", "parallel"), grid=(N // tn, M // tm),
        in_specs=[pl.BlockSpec((tm, K), lambda j, i: (i, 0)), w_spec, w_spec], out_specs=[o_spec] * 3,
        out_shape=[_sds((M, N)), _sds((M, N)), _sds((M, N), MXU_DTYPE)],
    )(h, w_gate, w_up)


def _gate_up_bwd(dx, w_down, gate, up, *, name):
    (M, K), N = dx.shape, w_down.shape[0]
    tm, tn = _tile(M, FFN_ROW_TILE), _tile(N, FFN_COL_TILE)

    def body(dx_ref, w_ref, g_ref, u_ref, dg_ref, du_ref):
        d = _dot(dx_ref[...], w_ref[...], NT)
        g = g_ref[...]
        sg = _sigmoid(g)
        dg_ref[...] = (d * u_ref[...] * (sg * (1.0 + g * (1.0 - sg)))).astype(dg_ref.dtype)
        du_ref[...] = (d * g * sg).astype(du_ref.dtype)

    o_spec = pl.BlockSpec((tm, tn), lambda j, i: (i, j))
    return _pcall(
        body, name=name, dims=("parallel", "parallel"), grid=(N // tn, M // tm),
        in_specs=[pl.BlockSpec((tm, K), lambda j, i: (i, 0)), pl.BlockSpec((tn, K), lambda j, i: (j, 0)),
                  o_spec, o_spec],
        out_specs=[o_spec] * 2, out_shape=[_sds((M, N), MXU_DTYPE)] * 2,
    )(dx, w_down, gate, up)


def _head_pair_weights(w, c, rows):
    return jnp.where(_lane((rows, LANES)) < HEAD_DIM, w[:, 2 * c:2 * c + 1], w[:, 2 * c + 1:2 * c + 2])


def _merge(outs_by_d, lses_by_d, *, name):
    S = outs_by_d[1].shape[0]
    far = DILATIONS[1:]

    def body(tm, o1, o4, o16, l1, l4, l16, o_o, w1_o, w4_o, w16_o, nat_o, nat_l):
        for b, (o_r, l_r, d) in enumerate(zip((o4, o16), (l4, l16), far)):
            _from_branch(o_r, nat_o, b * QKV_CHUNKS, QKV_CHUNKS, d, tm)
            _from_branch(l_r, nat_l, b, 1, d, tm)
        ls = [l1[...], nat_l[0], nat_l[1]]
        m = jnp.maximum(jnp.maximum(ls[0], ls[1]), ls[2])
        es = [jnp.exp(l - m) for l in ls]
        tot = es[0] + es[1] + es[2]
        ws = [e / tot for e in es]
        for w_o, w in zip((w1_o, w4_o, w16_o), ws):
            w_o[...] = w
        for c in range(QKV_CHUNKS):
            sl = slice(c * LANES, (c + 1) * LANES)
            parts = (o1[:, sl], nat_o[c], nat_o[QKV_CHUNKS + c])
            o_o[:, sl] = sum(_head_pair_weights(w, c, tm) * part for w, part in zip(ws, parts))

    ins = [outs_by_d[d] for d in DILATIONS] + [lses_by_d[d] for d in DILATIONS]
    outs = [_sds((S, 1024))] + [_sds((S, LANES))] * 3
    rows = _row_tile(S, 256)
    return _rowwise(body, ins, outs, name=name, rows=S, tm=256,
                    scratch=[pltpu.VMEM((2 * QKV_CHUNKS, rows, LANES), F32), pltpu.VMEM((2, rows, LANES), F32)])


def _merge_bwd(do, o, ws, *, name):
    S = do.shape[0]

    def body(tm, do_r, o_r, w1, w4, w16, d1, d4, d16, e1, e4, e16, nat, nat_l):
        prod = do_r[...] * o_r[...]
        sums = _cols_to_lanes([jnp.sum(prod[:, j * HEAD_DIM:(j + 1) * HEAD_DIM], axis=1, keepdims=True)
                               for j in range(DIL_HEADS)], tm)
        for w_r, d_o, e_o, d in zip((w1, w4, w16), (d1, d4, d16), (e1, e4, e16), DILATIONS):
            w = w_r[...]
            nat_l[0] = w * sums
            _to_branch(nat_l, 0, 1, e_o, d, tm)
            for c in range(QKV_CHUNKS):
                nat[c] = _head_pair_weights(w, c, tm) * do_r[:, c * LANES:(c + 1) * LANES]
            _to_branch(nat, 0, QKV_CHUNKS, d_o, d, tm)

    outs = [_branch_sds(S, 1024, d, MXU_DTYPE) for d in DILATIONS] + [_branch_sds(S, LANES, d, F32) for d in DILATIONS]
    rows = _row_tile(S, 256)
    got = _rowwise(body, [do, o] + [ws[d] for d in DILATIONS], outs, name=name, rows=S, tm=256,
                   scratch=[pltpu.VMEM((QKV_CHUNKS, rows, LANES), F32), pltpu.VMEM((1, rows, LANES), F32)])
    return dict(zip(DILATIONS, got[:3])), dict(zip(DILATIONS, got[3:]))


def _loss_head(x, g, target, *, name):
    S, D = x.shape

    def body(tm, x_r, g_r, t_r, dx_o, dg_o, sq_o):
        xf = x_r[...]
        xhat, _ = _rms_parts(xf)
        err = xhat * g_r[...] - t_r[...]
        dx, dgp = _rms_bwd_rows(xf, g_r[...], err * (1.0 / D))
        dx_o[...] = dx
        _acc_rows(dg_o, dgp)
        _acc_rows(sq_o, err * err)

    return _rowwise(body, [x, g.reshape(1, D), target], [_sds((S, D)), _sds((1, D)), _sds((1, D))],
                    name=name, rows=S, accs=(1, 2))


def _adamw(w, g, m, v, *, name):
    c1 = 1.0 - ADAM_B1 ** ADAM_STEP
    c2 = 1.0 - ADAM_B2 ** ADAM_STEP

    def body(tm, w_r, g_r, m_r, v_r, d_o, m_o, v_o):
        g = g_r[...]
        m_new = ADAM_B1 * m_r[...] + (1.0 - ADAM_B1) * g
        v_new = ADAM_B2 * v_r[...] + (1.0 - ADAM_B2) * (g * g)
        m_o[...] = m_new
        v_o[...] = v_new
        d_o[...] = -ADAM_LR * ((m_new / c1) / (jnp.sqrt(v_new / c2) + ADAM_EPS) + ADAM_WD * w_r[...])

    return _rowwise(body, [w, g, m, v], [_sds(w.shape)] * 3, name=name, rows=w.shape[0], tm=256)


SUM_ROW_TILE = 256


def _sum_cores(grads, theirs, half_index, *, name):
    _, R, C = grads.shape
    h = R // 2
    nb = h // SUM_ROW_TILE

    def body(c_ref, g_ref, t_ref, o_ref):
        o_ref[...] = (g_ref[...].astype(F32) + t_ref[...].astype(F32)).astype(o_ref.dtype)

    grid_spec = pltpu.PrefetchScalarGridSpec(
        num_scalar_prefetch=1, grid=(4, nb),
        in_specs=[pl.BlockSpec((1, SUM_ROW_TILE, C), lambda k, i, c_ref: (k, c_ref[0] * nb + i, 0)),
                  pl.BlockSpec((1, SUM_ROW_TILE, C), lambda k, i, c_ref: (k, i, 0))],
        out_specs=pl.BlockSpec((1, SUM_ROW_TILE, C), lambda k, i, c_ref: (k, i, 0)))
    return _pcall(body, name=name, dims=("parallel", "parallel"), grid_spec=grid_spec,
                  out_shape=_sds((4, h, C), grads.dtype))(half_index, grads, theirs)


def _sum_chips(parts, half_index, *, name):
    _, h, C = parts.shape
    nb = h // SUM_ROW_TILE

    def body(c_ref, p_ref, o_ref):
        p = [p_ref[k].astype(F32) for k in range(4)]
        o_ref[...] = ((p[0] + p[1]) + p[2]) + p[3]

    grid_spec = pltpu.PrefetchScalarGridSpec(
        num_scalar_prefetch=1, grid=(nb,),
        in_specs=[pl.BlockSpec((4, SUM_ROW_TILE, C), lambda i, c_ref: (0, i, 0))],
        out_specs=pl.BlockSpec((SUM_ROW_TILE, C), lambda i, c_ref: (c_ref[0] * nb + i, 0)))
    return _pcall(body, name=name, dims=("parallel",), grid_spec=grid_spec,
                  out_shape=_sds((2 * h, C)))(half_index, parts)


def _position():
    return lax.axis_index("x"), lax.axis_index("y"), lax.axis_index("c")


def _chip_peers(x, y):
    return [(1 - x, y), (x, 1 - y), (1 - x, 1 - y)]


_HBM = pl.BlockSpec(memory_space=pltpu.HBM)
LOCAL_COPY_CHUNKS = 8


def _local_copies(src_ref, dst_ref, sems):
    rows = src_ref.shape[0] // LOCAL_COPY_CHUNKS
    assert rows * LOCAL_COPY_CHUNKS == src_ref.shape[0]
    return [pltpu.make_async_copy(src_ref.at[pl.ds(i * rows, rows)], dst_ref.at[pl.ds(i * rows, rows)], sems.at[i])
            for i in range(LOCAL_COPY_CHUNKS)]


class _Exchange:
    def __init__(self, src, out_shape, sems, stages):
        self.src, self.out_shape, self.sems, self.stages = src, out_shape, sems, stages

    def run(self, refs, step, n_steps, at_end):
        for fraction, fn in self.stages:
            if (fraction == 1.0) == at_end:
                pl.when(step == int(round(fraction * (n_steps - 1))))(functools.partial(fn, *refs))


def _run_exchange(ex, *, name):
    def body(*refs):
        for _, fn in ex.stages:
            fn(*refs)

    return pl.pallas_call(
        body, name=name, in_specs=[_HBM], out_specs=_HBM, out_shape=ex.out_shape, scratch_shapes=list(ex.sems),
    )(ex.src)


def _gather_exchange(src):
    R, C = src.shape
    h = R // 2

    def plan(src_ref, out_ref, send_sems, recv_sems, local_sems):
        x, y, c = _position()
        me = 2 * x + y
        peers = _chip_peers(x, y)
        mine, other = pl.ds(c * h, h), pl.ds((1 - c) * h, h)

        def copy(sem, src_part, dst_part, device):
            return pltpu.make_async_remote_copy(
                src_ref=src_part, dst_ref=dst_part, send_sem=send_sems.at[sem], recv_sem=recv_sems.at[sem],
                device_id=device, device_id_type=MESH)

        landed = [out_ref.at[2 * px + py, mine] for px, py in peers]
        theirs = [out_ref.at[2 * px + py, other] for px, py in peers]
        return dict(
            sends=lambda: [copy(j, src_ref.at[mine], out_ref.at[me, mine], (px, py, c))
                           for j, (px, py) in enumerate(peers)],
            local=lambda: _local_copies(src_ref, out_ref.at[me], local_sems),
            arrivals=lambda: [copy(j, landed[j], landed[j], (px, py, c)) for j, (px, py) in enumerate(peers)],
            passed=lambda: [copy(3 + j, landed[j], landed[j], (x, y, 1 - c)) for j in range(3)],
            from_sibling=lambda: [copy(3 + j, theirs[j], theirs[j], (x, y, 1 - c)) for j in range(3)])

    def start(*refs):
        p = plan(*refs)
        for cp in p["sends"]() + p["local"]():
            cp.start()

    def pass_on(*refs):
        p = plan(*refs)
        for arrival, forward in zip(p["arrivals"](), p["passed"]()):
            arrival.wait_recv()
            forward.start()

    def finish(*refs):
        p = plan(*refs)
        for cp in p["from_sibling"]():
            cp.wait_recv()
        for cp in p["sends"]() + p["passed"]():
            cp.wait_send()
        for cp in p["local"]():
            cp.wait()

    sems = [pltpu.SemaphoreType.DMA((6,)), pltpu.SemaphoreType.DMA((6,)), pltpu.SemaphoreType.DMA((LOCAL_COPY_CHUNKS,))]
    return _Exchange(src, jax.ShapeDtypeStruct((4, R, C), src.dtype), sems, [(0.0, start), (0.6, pass_on), (1.0, finish)])


def _swap_other_half(src, *, name):
    _, R, C = src.shape
    h = R // 2

    def body(src_ref, out_ref, send_sem, recv_sem):
        x, y, c = _position()
        cp = pltpu.make_async_remote_copy(
            src_ref=src_ref.at[:, pl.ds((1 - c) * h, h)], dst_ref=out_ref, send_sem=send_sem, recv_sem=recv_sem,
            device_id=(x, y, 1 - c), device_id_type=MESH)
        cp.start()
        cp.wait()

    return pl.pallas_call(
        body, name=name, in_specs=[_HBM], out_specs=_HBM, out_shape=jax.ShapeDtypeStruct((4, h, C), src.dtype),
        scratch_shapes=[pltpu.SemaphoreType.DMA, pltpu.SemaphoreType.DMA],
    )(src)


def _scatter_exchange(src):
    def plan(src_ref, out_ref, send_sems, recv_sems, local_sems):
        x, y, c = _position()
        me = 2 * x + y
        peers = _chip_peers(x, y)

        def copy(j, src_block, dst_slot):
            px, py = peers[j]
            return pltpu.make_async_remote_copy(
                src_ref=src_ref.at[src_block], dst_ref=out_ref.at[dst_slot], send_sem=send_sems.at[j],
                recv_sem=recv_sems.at[j], device_id=(px, py, c), device_id_type=MESH)

        return dict(sends=lambda: [copy(j, 2 * px + py, me) for j, (px, py) in enumerate(peers)],
                    arrivals=lambda: [copy(j, me, 2 * px + py) for j, (px, py) in enumerate(peers)],
                    local=lambda: _local_copies(src_ref.at[me], out_ref.at[me], local_sems))

    def start(*refs):
        p = plan(*refs)
        for cp in p["sends"]() + p["local"]():
            cp.start()

    def finish(*refs):
        p = plan(*refs)
        for cp in p["arrivals"]():
            cp.wait_recv()
        for cp in p["sends"]():
            cp.wait_send()
        for cp in p["local"]():
            cp.wait()

    sems = [pltpu.SemaphoreType.DMA((3,)), pltpu.SemaphoreType.DMA((3,)), pltpu.SemaphoreType.DMA((LOCAL_COPY_CHUNKS,))]
    return _Exchange(src, jax.ShapeDtypeStruct(src.shape, src.dtype), sems, [(0.0, start), (1.0, finish)])


def _join_halves(src, *, name):
    R, C = src.shape
    h = R // 2

    def body(src_ref, out_ref, send_sem, recv_sem):
        x, y, c = _position()
        mine, theirs = pl.ds(c * h, h), pl.ds((1 - c) * h, h)
        cp = pltpu.make_async_remote_copy(
            src_ref=src_ref.at[mine], dst_ref=out_ref.at[mine], send_sem=send_sem, recv_sem=recv_sem,
            device_id=(x, y, 1 - c), device_id_type=MESH)
        cp.start()
        pltpu.make_async_remote_copy(
            src_ref=src_ref.at[theirs], dst_ref=out_ref.at[theirs], send_sem=send_sem, recv_sem=recv_sem,
            device_id=(x, y, 1 - c), device_id_type=MESH).wait_recv()
        cp.wait_send()

    return pl.pallas_call(
        body, name=name, in_specs=[_HBM], out_specs=_HBM, out_shape=jax.ShapeDtypeStruct((R, C), src.dtype),
        input_output_aliases={0: 0},
        scratch_shapes=[pltpu.SemaphoreType.DMA, pltpu.SemaphoreType.DMA],
    )(src)


def _allreduce_small(vec, *, name):
    R, C = vec.shape

    def body(v_ref, o_ref, slots, send_sems, recv_sems):
        x, y, c = _position()
        me = 4 * x + 2 * y + c

        def peer(k):
            return x ^ ((k >> 2) & 1), y ^ ((k >> 1) & 1), c ^ (k & 1)

        def copy(k, slot):
            return pltpu.make_async_remote_copy(
                src_ref=v_ref, dst_ref=slots.at[slot], send_sem=send_sems.at[k - 1], recv_sem=recv_sems.at[k - 1],
                device_id=peer(k), device_id_type=MESH)

        slots[me] = v_ref[...]
        sends = [copy(k, me) for k in range(1, 8)]
        for cp in sends:
            cp.start()
        for k in range(1, 8):
            px, py, pc = peer(k)
            copy(k, 4 * px + 2 * py + pc).wait_recv()
        total = slots[0]
        for d in range(1, 8):
            total = total + slots[d]
        o_ref[...] = total
        for cp in sends:
            cp.wait_send()

    vmem = pl.BlockSpec(memory_space=pltpu.VMEM)
    return pl.pallas_call(
        body, name=name, in_specs=[vmem], out_specs=vmem, out_shape=jax.ShapeDtypeStruct((R, C), vec.dtype),
        scratch_shapes=[pltpu.VMEM((8, R, C), vec.dtype), pltpu.SemaphoreType.DMA((7,)), pltpu.SemaphoreType.DMA((7,))],
    )(vec)


def _cross_cfg(S, mem_len):
    return _Attn(T=S, Tk=mem_len, G=1, nh=X_HEADS, rep=1, dqk=X_HEAD_DIM, dv=X_HEAD_DIM, tq=512, tk=mem_len,
                 mode="none", scale=X_HEAD_DIM ** -0.5, qcol=lambda g: 0, kcol=lambda g: 0, vcol=lambda g: 1,
                 ocol=lambda g: 0, o_width=X_HEADS * X_HEAD_DIM)


def _swa_cfg(S):
    return _Attn(T=S, Tk=S, G=1, nh=SWA_HEADS, rep=SWA_HEADS // SWA_KV_HEADS, dqk=HEAD_DIM, dv=HEAD_DIM, tq=BLOCK,
                 tk=BLOCK, mode="band", max_dist=SWA_WINDOW - 1, scale=HEAD_DIM ** -0.5, qcol=lambda g: 0,
                 kcol=lambda g: 0, vcol=lambda g: 0, ocol=lambda g: 0, o_width=SWA_HEADS * HEAD_DIM)


MLA_FWD_GROUP = 8
MLA_BWD_GROUP = 2


def _mla_cfg(S, group):
    t = _tile(S, 512)
    return _Attn(T=S, Tk=S, G=MLA_HEADS // group, nh=group, rep=1, dqk=LANES, dv=MLA_V, tq=t, tk=t, mode="causal",
                 scale=(MLA_NOPE + MLA_ROPE) ** -0.5, qcol=lambda g: g, kcol=lambda g: g, vcol=lambda g: g,
                 ocol=lambda g: g, o_width=MLA_HEADS * MLA_V)


def _dil_cfg(S, window, dil):
    return _Attn(T=S // dil, Tk=S // dil, G=dil, nh=DIL_HEADS, rep=1, dqk=HEAD_DIM, dv=HEAD_DIM, tq=BLOCK, tk=BLOCK,
                 mode="band", max_dist=window // dil, scale=HEAD_DIM ** -0.5, qcol=lambda g: g, kcol=lambda g: g,
                 vcol=lambda g: g, ocol=lambda g: g, o_width=dil * DIL_HEADS * HEAD_DIM)


def _cross_fwd(p, x, mem, W, vec):
    S = x.shape[0]
    cfg = _cross_cfg(S, mem.shape[0])
    hx = _rmsnorm(x, vec[p + "x_norm"], name=p + "x_norm")
    qx = _mm(hx, W[p + "w_xq"], mode="nn", name=p + "xq", out_dtype=MXU_DTYPE)
    memn = _rmsnorm(mem, vec[p + "mem_norm"], name=p + "mem_norm")
    kvx = _mm(memn, W[p + "w_xkv"], mode="nn", name=p + "xkv", out_dtype=MXU_DTYPE)
    ox, lse = _attn_fwd(cfg, qx, kvx, kvx, name=p + "x_attn", out_dtype=MXU_DTYPE)
    out = _mm(ox, W[p + "w_xo"], mode="nn", name=p + "xo", res=x)
    return out, (x, hx, qx, memn, kvx, ox, lse)


def _cross_bwd(p, dx, saved, mem, W, vec, dW, dvec):
    x, hx, qx, memn, kvx, ox, lse = saved
    cfg = _cross_cfg(x.shape[0], mem.shape[0])
    dox = _mm(dx, W[p + "w_xo"], mode="nt", name=p + "xo_dx", out_dtype=MXU_DTYPE)
    dW[p + "w_xo"] = _dw(ox, dx, name=p + "xo_dw")
    delta, _ = _attn_delta(cfg, ox, dox, name=p + "x_delta")
    dqx = _attn_dq(cfg, qx, kvx, kvx, dox, lse, delta, name=p + "x_dq", out_dtype=MXU_DTYPE)
    dkx, dvx = _attn_dkv(cfg, qx, kvx, kvx, dox, lse, delta, name=p + "x_dkv", out_dtype=MXU_DTYPE)
    dkvx = jnp.concatenate([dkx, dvx], axis=1)
    dhx = _mm(dqx, W[p + "w_xq"], mode="nt", name=p + "xq_dx")
    dW[p + "w_xq"] = _dw(hx, dqx, name=p + "xq_dw")
    dW[p + "w_xkv"] = _dw(memn, dkvx, name=p + "xkv_dw")
    dmemn = _mm(dkvx, W[p + "w_xkv"], mode="nt", name=p + "xkv_dx")
    _, dvec[p + "mem_norm"] = _rmsnorm_bwd(mem, vec[p + "mem_norm"], dmemn, name=p + "mem_norm_bwd")
    dx_in, dvec[p + "x_norm"] = _rmsnorm_bwd(x, vec[p + "x_norm"], dhx, name=p + "x_norm_bwd", dres=dx)
    return dx_in


def _ffn_fwd(p, x, W, vec):
    hf = _rmsnorm(x, vec[p + "ffn_norm"], name=p + "ffn_norm")
    gate, up, act = _gate_up(hf, W[p + "w_gate"], W[p + "w_up"], name=p + "gate_up")
    out = _mm(act, W[p + "w_down"], mode="nn", name=p + "down", res=x)
    return out, (x, hf, gate, up, act)


def _ffn_bwd(p, dx, saved, W, vec, dW, dvec):
    x, hf, gate, up, act = saved
    dW[p + "w_down"] = _dw(act, dx, name=p + "down_dw")
    dgate, dup = _gate_up_bwd(dx, W[p + "w_down"], gate, up, name=p + "gate_up_bwd")
    dhf = _mm(dgate, W[p + "w_gate"], mode="nt", name=p + "gate_dx")
    dhf = _mm(dup, W[p + "w_up"], mode="nt", name=p + "up_dx", res=dhf)
    dW[p + "w_gate"] = _dw(hf, dgate, name=p + "gate_dw")
    dW[p + "w_up"] = _dw(hf, dup, name=p + "up_dw")
    dx_in, dvec[p + "ffn_norm"] = _rmsnorm_bwd(x, vec[p + "ffn_norm"], dhf, name=p + "ffn_norm_bwd", dres=dx)
    return dx_in


def _even_fwd(p, x, tabs, W, vec, comm=None):
    S = x.shape[0]
    h = _rmsnorm(x, vec[p + "mix_norm"], name=p + "mix_norm")
    z = _mm(h, W[p + "w_in"], mode="nn", name=p + "in")
    qa, ka, va, cqn, ckvn, kr = _l0_prep(z, tabs, vec[p + "q_norm"], vec[p + "kv_norm"], name=p + "prep")
    sink = jnp.pad(vec[p + "sinks"], (0, LANES - SWA_HEADS)).reshape(1, LANES)
    oa, lse_a = _band_fwd(_swa_cfg(S), qa, ka, va, name=p + "swa", sink=sink, out_dtype=MXU_DTYPE)
    qb = _mm(cqn, W[p + "w_uq"], mode="nn", name=p + "uq")
    kvb = _mm(ckvn, W[p + "w_ukv"], mode="nn", name=p + "ukv")
    Q, K, V = _mla_prep(qb, kvb, kr, tabs, name=p + "mla_prep")
    if comm is None:
        ob, lse_b = _causal_fwd(_mla_cfg(S, MLA_FWD_GROUP), Q, K, V, name=p + "mla", out_dtype=MXU_DTYPE)
    else:
        ob, lse_b, gathered = _causal_fwd(_mla_cfg(S, MLA_FWD_GROUP), Q, K, V, name=p + "mla", out_dtype=MXU_DTYPE,
                                          carry=comm.late_weights_exchange())
        W = {**W, **comm.late_weights(gathered)}
    o = jnp.concatenate([oa, ob], axis=1)
    out = _mm(o, W[p + "w_out"], mode="nn", name=p + "out", res=x)
    return out, (x, h, z, qa, ka, va, cqn, ckvn, sink, oa, lse_a, Q, K, V, ob, lse_b, o), W


def _even_bwd(p, dx, saved, tabs, W, vec, dW, dvec, comm=None):
    x, h, z, qa, ka, va, cqn, ckvn, sink, oa, lse_a, Q, K, V, ob, lse_b, o = saved
    S = x.shape[0]
    do = _mm(dx, W[p + "w_out"], mode="nt", name=p + "out_dx", out_dtype=MXU_DTYPE)
    dW[p + "w_out"] = _dw(o, dx, name=p + "out_dw")
    doa, dob = do[:, :SWA_HEADS * HEAD_DIM], do[:, SWA_HEADS * HEAD_DIM:]
    cfg = _swa_cfg(S)
    delta, dsink = _attn_delta(cfg, oa, doa, name=p + "swa_delta", lse=lse_a, sink=sink)
    dvec[p + "sinks"] = dsink
    dqa, dka, dva = _band_bwd(cfg, qa, ka, va, doa, lse_a, delta, name=p + "swa_bwd")
    cfg = _mla_cfg(S, MLA_BWD_GROUP)
    if comm is None:
        dQ, dK, dV = _causal_bwd(cfg, Q, K, V, ob, dob, lse_b, name=p + "mla_bwd", stat_heads=MLA_FWD_GROUP)
    else:
        dQ, dK, dV, landed = _causal_bwd(cfg, Q, K, V, ob, dob, lse_b, name=p + "mla_bwd", stat_heads=MLA_FWD_GROUP,
                                         carry=comm.late_grads_exchange(dW))
        comm.late_grads_landed(landed)
    dqb, dkvb, dkr = _mla_prep_bwd(dQ, dK, dV, tabs, name=p + "mla_prep_bwd")
    dcqn = _mm(dqb, W[p + "w_uq"], mode="nt", name=p + "uq_dx")
    dW[p + "w_uq"] = _dw(cqn, dqb, name=p + "uq_dw")
    dckvn = _mm(dkvb, W[p + "w_ukv"], mode="nt", name=p + "ukv_dx")
    dW[p + "w_ukv"] = _dw(ckvn, dkvb, name=p + "ukv_dw")
    dz, dvec[p + "q_norm"], dvec[p + "kv_norm"] = _l0_prep_bwd(
        z, tabs, vec[p + "q_norm"], vec[p + "kv_norm"], dqa, dka, dva, dcqn, dckvn, dkr, name=p + "prep_bwd")
    dh = _mm(dz, W[p + "w_in"], mode="nt", name=p + "in_dx")
    dW[p + "w_in"] = _dw(h, dz, name=p + "in_dw")
    dx_in, dvec[p + "mix_norm"] = _rmsnorm_bwd(x, vec[p + "mix_norm"], dh, name=p + "mix_norm_bwd", dres=dx)
    return dx_in


def _odd_fwd(p, x, tabs, W, vec):
    S = x.shape[0]
    assert S % (DIL_PATTERNS[-1][1] * BLOCK) == 0, "keys past the end of the sequence are never attended"
    h = _rmsnorm(x, vec[p + "mix_norm"], name=p + "mix_norm")
    qkv = _mm(h, W[p + "w_qkv"], mode="nn", name=p + "qkv")
    qkv_by_d = _l1_prep(qkv, tabs, name=p + "prep")
    outs, lses = {}, {}
    for window, dil in DIL_PATTERNS:
        outs[dil], lses[dil] = _band_fwd(_dil_cfg(S, window, dil), *qkv_by_d[dil], name=p + "dil%d" % dil)
    o, w1, w4, w16 = _merge(outs, lses, name=p + "merge")
    out = _mm(o, W[p + "w_out"], mode="nn", name=p + "out", res=x)
    return out, (x, h, qkv_by_d, lses, dict(zip(DILATIONS, (w1, w4, w16))), o)


def _odd_bwd(p, dx, saved, tabs, W, vec, dW, dvec):
    x, h, qkv_by_d, lses, ws, o = saved
    S = x.shape[0]
    do = _mm(dx, W[p + "w_out"], mode="nt", name=p + "out_dx")
    dW[p + "w_out"] = _dw(o, dx, name=p + "out_dw")
    dos, deltas = _merge_bwd(do, o, ws, name=p + "merge_bwd")
    grads = {}
    for window, dil in DIL_PATTERNS:
        grads[dil] = _band_bwd(_dil_cfg(S, window, dil), *qkv_by_d[dil], dos[dil], lses[dil], deltas[dil],
                               name=p + "dil%d_bwd" % dil)
    dqkv = _l1_prep_bwd(grads, tabs, name=p + "prep_bwd")
    dh = _mm(dqkv, W[p + "w_qkv"], mode="nt", name=p + "qkv_dx")
    dW[p + "w_qkv"] = _dw(h, dqkv, name=p + "qkv_dw")
    dx_in, dvec[p + "mix_norm"] = _rmsnorm_bwd(x, vec[p + "mix_norm"], dh, name=p + "mix_norm_bwd", dres=dx)
    return dx_in


def _local_step(x, mem, positions, target, W, vec, comm=None):
    tabs = _rope_tables(positions)
    x1, s_mix0, W = _even_fwd("l0_", x, tabs, W, vec, comm)
    x2, s_x0 = _cross_fwd("l0_", x1, mem, W, vec)
    x3, s_f0 = _ffn_fwd("l0_", x2, W, vec)
    x4, s_mix1 = _odd_fwd("l1_", x3, tabs, W, vec)
    x5, s_x1 = _cross_fwd("l1_", x4, mem, W, vec)
    x6, s_f1 = _ffn_fwd("l1_", x5, W, vec)
    dW, dvec = {}, {}
    dx, dvec["final_norm"], sq = _loss_head(x6, vec["final_norm"], target, name="loss_head")
    dx = _ffn_bwd("l1_", dx, s_f1, W, vec, dW, dvec)
    dx = _cross_bwd("l1_", dx, s_x1, mem, W, vec, dW, dvec)
    dx = _odd_bwd("l1_", dx, s_mix1, tabs, W, vec, dW, dvec)
    dx = _ffn_bwd("l0_", dx, s_f0, W, vec, dW, dvec)
    dx = _cross_bwd("l0_", dx, s_x0, mem, W, vec, dW, dvec)
    dx = _even_bwd("l0_", dx, s_mix0, tabs, W, vec, dW, dvec, comm)
    return sq, dx, dW, dvec


_LAYER_MATS = {
    0: [("w_in", "col"), ("w_uq", "col"), ("w_ukv", "col"), ("w_out", "row"), ("w_xq", "row"), ("w_xkv", "row"),
        ("w_xo", "col"), ("w_gate", "col"), ("w_up", "col"), ("w_down", "row")],
    1: [("w_qkv", "col"), ("w_out", "row"), ("w_xq", "row"), ("w_xkv", "row"), ("w_xo", "col"), ("w_gate", "col"),
        ("w_up", "col"), ("w_down", "row")],
}
MATS = [("l%d_%s" % (l, n), kind) for l in (0, 1) for n, kind in _LAYER_MATS[l]]
_LAYER_VECS = {0: ["mix_norm", "sinks", "q_norm", "kv_norm", "x_norm", "mem_norm", "ffn_norm"],
               1: ["mix_norm", "x_norm", "mem_norm", "ffn_norm"]}
VECS = ["l%d_%s" % (l, n) for l in (0, 1) for n in _LAYER_VECS[l]] + ["final_norm"]
WEIGHT_ORDER = (["l0_mix_norm", "l0_w_in", "l0_sinks", "l0_q_norm", "l0_w_uq", "l0_kv_norm", "l0_w_ukv", "l0_w_out",
                 "l0_x_norm", "l0_mem_norm", "l0_w_xq", "l0_w_xkv", "l0_w_xo", "l0_ffn_norm", "l0_w_gate", "l0_w_up",
                 "l0_w_down", "l1_mix_norm", "l1_w_qkv", "l1_w_out", "l1_x_norm", "l1_mem_norm", "l1_w_xq",
                 "l1_w_xkv", "l1_w_xo", "l1_ffn_norm", "l1_w_gate", "l1_w_up", "l1_w_down", "final_norm"])
PACK_COLS = 1024
PACK_ROW_TILE = 2 * SUM_ROW_TILE
VEC_ROWS = 16
LOSS_ROW = len(VECS)
N_CHIPS = 4


class _Group:
    def __init__(self, mats, shards):
        self.mats, self.shards = mats, shards
        self.layout, off = {}, 0
        for name, _ in mats:
            n = shards[name].size // PACK_COLS
            assert n * PACK_COLS == shards[name].size
            self.layout[name] = (off, n)
            off += n
        self.used = off
        self.rows = -(-off // PACK_ROW_TILE) * PACK_ROW_TILE

    def pack(self, tensors, dtype):
        parts = [tensors[name].astype(dtype).reshape(-1, PACK_COLS) for name, _ in self.mats]
        return jnp.concatenate(parts + [jnp.zeros((self.rows - self.used, PACK_COLS), dtype)], axis=0)

    def unpack(self, packed):
        return {name: packed[off:off + n].reshape(self.shards[name].shape) for name, (off, n) in self.layout.items()}

    def full_weights(self, gathered):
        W = {}
        for name, kind in self.mats:
            off, n = self.layout[name]
            r, cw = self.shards[name].shape
            blocks = gathered[:, off:off + n].reshape(N_CHIPS, r, cw)
            W[name] = blocks.reshape(N_CHIPS * r, cw) if kind == "row" else (
                jnp.transpose(blocks, (1, 0, 2)).reshape(r, N_CHIPS * cw))
        if "l0_w_in" in W:
            W["l0_w_in"] = jnp.pad(W["l0_w_in"], ((0, 0), (0, Z_END - W["l0_w_in"].shape[1])))
        if "l0_w_uq" in W:
            uq = W["l0_w_uq"].reshape(MLA_Q_RANK, MLA_HEADS, MLA_NOPE + MLA_ROPE)
            uq = jnp.pad(uq, ((0, 0), (0, 0), (0, LANES - MLA_NOPE - MLA_ROPE)))
            W["l0_w_uq"] = uq.reshape(MLA_Q_RANK, MLA_HEADS * LANES)
        return W

    def pack_grads(self, dW):
        parts = []
        for name, kind in self.mats:
            r, cw = self.shards[name].shape
            g = dW[name]
            if name == "l0_w_in":
                g = g[:, :Z_KR + MLA_ROPE]
            if name == "l0_w_uq":
                g = g.reshape(MLA_Q_RANK, MLA_HEADS, LANES)[:, :, :MLA_NOPE + MLA_ROPE].reshape(MLA_Q_RANK, -1)
            if kind == "col":
                g = jnp.transpose(g.reshape(r, N_CHIPS, cw), (1, 0, 2))
            parts.append(g.reshape(N_CHIPS, -1, PACK_COLS).astype(EXCHANGE_DTYPE))
        pad = jnp.zeros((N_CHIPS, self.rows - self.used, PACK_COLS), EXCHANGE_DTYPE)
        return jnp.concatenate(parts + [pad], axis=1)


def _pack_vecs(vecs):
    rows = [jnp.pad(vecs[n].reshape(-1).astype(F32), (0, PACK_COLS - vecs[n].size)) for n in VECS]
    rows += [jnp.zeros((PACK_COLS,), F32)] * (VEC_ROWS - len(rows))
    return jnp.stack(rows)


def _unpack_vecs(packed, like):
    return {n: packed[i, :like[n].size].reshape(like[n].shape) for i, n in enumerate(VECS)}


EARLY_MATS = [m for m in MATS if m[0] in ("l0_w_in", "l0_w_uq", "l0_w_ukv")]
LATE_MATS = [m for m in MATS if m not in EARLY_MATS]


class _StepComm:
    def __init__(self, shards):
        self.early, self.late = _Group(EARLY_MATS, shards), _Group(LATE_MATS, shards)
        self.half_index = lax.axis_index("c").astype(jnp.int32).reshape(1)
        self.late_grads = None

    def early_weights(self):
        src = self.early.pack(self.early.shards, MXU_DTYPE)
        return self.early.full_weights(_run_exchange(_gather_exchange(src), name="gather_early"))

    def late_weights_exchange(self):
        return _gather_exchange(self.late.pack(self.late.shards, MXU_DTYPE))

    def late_weights(self, gathered):
        return self.late.full_weights(gathered)

    def _chip_sum(self, group, dW, tag):
        grads = group.pack_grads(dW)
        theirs = _swap_other_half(grads, name="swap_other_half_" + tag)
        return _sum_cores(grads, theirs, self.half_index, name="sum_cores_" + tag)

    def _finish(self, parts, tag):
        return _join_halves(_sum_chips(parts, self.half_index, name="sum_chips_" + tag), name="join_halves_" + tag)

    def late_grads_exchange(self, dW):
        return _scatter_exchange(self._chip_sum(self.late, dW, "late"))

    def late_grads_landed(self, parts):
        self.late_grads = self._finish(parts, "late")

    def early_grads(self, dW):
        parts = _run_exchange(_scatter_exchange(self._chip_sum(self.early, dW, "early")), name="scatter_early")
        return self._finish(parts, "early")


def _step(a):
    weights = {n: a[n] for n in WEIGHT_ORDER}
    shards = {n: weights[n] for n, _ in MATS}
    vec = {n: weights[n] for n in VECS}
    comm = _StepComm(shards)
    sq, grad_x, dW, dvec = _local_step(a["x"][0], a["mem"][0], a["positions"], a["loss_target"][0],
                                       comm.early_weights(), vec, comm)

    dvec = dict(dvec)
    dvec["l0_sinks"] = dvec["l0_sinks"][0, :SWA_HEADS]
    small = _pack_vecs(dvec)
    small = small.at[LOSS_ROW, 0].set(0.5 / a["x"].shape[-1] * jnp.sum(sq))
    small = _allreduce_small(small, name="reduce_gains")
    loss = small[LOSS_ROW, 0]
    g_s = small.at[LOSS_ROW, 0].set(0.0)
    d_s, m_s, v_s = _adamw(_pack_vecs(vec), g_s, _pack_vecs({n: a["m_" + n] for n in VECS}),
                           _pack_vecs({n: a["v_" + n] for n in VECS}), name="adamw_gains")
    got = [_unpack_vecs(packed, vec) for packed in (g_s, d_s, m_s, v_s)]

    for group, g_w in ((comm.late, comm.late_grads), (comm.early, comm.early_grads(dW))):
        for n, g in group.unpack(g_w).items():
            results = (g,) + tuple(_adamw(shards[n], g, a["m_" + n], a["v_" + n], name="adamw_" + n))
            for kind, value in zip(got, results):
                kind[n] = value

    out = [loss, grad_x[None]]
    for kind in got:
        out += [kind[n] for n in WEIGHT_ORDER]
    return tuple(out)


def kernel(x, mem, positions, l0_mix_norm, l0_w_in, l0_sinks, l0_q_norm, l0_w_uq, l0_kv_norm, l0_w_ukv, l0_w_out, l0_x_norm, l0_mem_norm, l0_w_xq, l0_w_xkv, l0_w_xo, l0_ffn_norm, l0_w_gate, l0_w_up, l0_w_down, l1_mix_norm, l1_w_qkv, l1_w_out, l1_x_norm, l1_mem_norm, l1_w_xq, l1_w_xkv, l1_w_xo, l1_ffn_norm, l1_w_gate, l1_w_up, l1_w_down, final_norm, loss_target, m_l0_mix_norm, m_l0_w_in, m_l0_sinks, m_l0_q_norm, m_l0_w_uq, m_l0_kv_norm, m_l0_w_ukv, m_l0_w_out, m_l0_x_norm, m_l0_mem_norm, m_l0_w_xq, m_l0_w_xkv, m_l0_w_xo, m_l0_ffn_norm, m_l0_w_gate, m_l0_w_up, m_l0_w_down, m_l1_mix_norm, m_l1_w_qkv, m_l1_w_out, m_l1_x_norm, m_l1_mem_norm, m_l1_w_xq, m_l1_w_xkv, m_l1_w_xo, m_l1_ffn_norm, m_l1_w_gate, m_l1_w_up, m_l1_w_down, m_final_norm, v_l0_mix_norm, v_l0_w_in, v_l0_sinks, v_l0_q_norm, v_l0_w_uq, v_l0_kv_norm, v_l0_w_ukv, v_l0_w_out, v_l0_x_norm, v_l0_mem_norm, v_l0_w_xq, v_l0_w_xkv, v_l0_w_xo, v_l0_ffn_norm, v_l0_w_gate, v_l0_w_up, v_l0_w_down, v_l1_mix_norm, v_l1_w_qkv, v_l1_w_out, v_l1_x_norm, v_l1_mem_norm, v_l1_w_xq, v_l1_w_xkv, v_l1_w_xo, v_l1_ffn_norm, v_l1_w_gate, v_l1_w_up, v_l1_w_down, v_final_norm):
    return _step(dict(locals()))
```

```python
import functools

import jax
import jax.numpy as jnp
import numpy as np
from jax import lax
from jax.experimental import pallas as pl
from jax.experimental.pallas import tpu as pltpu

F32 = jnp.float32
MXU_DTYPE = jnp.bfloat16
LANES = 128
VMEM_LIMIT_BYTES = 56 * 1024 * 1024

NORM_EPS = 1e-6
ROPE_THETA = 10000.0
BLOCK = 128
HEAD_DIM = 64
SWA_HEADS, SWA_KV_HEADS, SWA_WINDOW = 8, 2, 128
MLA_HEADS, MLA_Q_RANK, MLA_KV_RANK, MLA_NOPE, MLA_ROPE, MLA_V = 8, 384, 256, 64, 32, 64
DIL_HEADS = 16
DIL_PATTERNS = ((128, 1), (512, 4), (2048, 16))
X_HEADS, X_HEAD_DIM = 4, 128
ADAM_LR, ADAM_B1, ADAM_B2, ADAM_EPS, ADAM_WD, ADAM_STEP = 0.001, 0.9, 0.999, 1e-08, 0.01, 10
MESH = pl.DeviceIdType.MESH
NEG_BIG = -1e30

NN = (((1,), (0,)), ((), ()))
NT = (((1,), (1,)), ((), ()))


def _dot(a, b, dims=NN):
    return lax.dot_general(a.astype(MXU_DTYPE), b.astype(MXU_DTYPE), dims, preferred_element_type=F32)


def _pcall(body, *, name, dims=None, **kw):
    params = pltpu.CompilerParams(dimension_semantics=dims, vmem_limit_bytes=VMEM_LIMIT_BYTES)
    return pl.pallas_call(body, name=name, compiler_params=params, **kw)


def _tile(n, pref):
    t = (min(pref, n) // LANES) * LANES
    while t >= LANES:
        if n % t == 0:
            return t
        t -= LANES
    return n


SUBLANES_PACKED = 16


def _row_tile(n, pref):
    t = (min(pref, n) // SUBLANES_PACKED) * SUBLANES_PACKED
    while t >= SUBLANES_PACKED:
        if n % t == 0:
            return t
        t -= SUBLANES_PACKED
    return n


def _lane(shape):
    return lax.broadcasted_iota(jnp.int32, shape, 1)


def _cols_to_lanes(cols, rows):
    lane = _lane((rows, LANES))
    out = jnp.zeros((rows, LANES), F32)
    for j, col in enumerate(cols):
        out = jnp.where(lane == j, col, out)
    return out


def _mm(a, b, *, mode, name, res=None, out_dtype=F32, tm=1408, tn=1536, tk=1408):
    if mode == "nn":
        (M, K), (K2, N) = a.shape, b.shape
    elif mode == "nt":
        (M, K), (N, K2) = a.shape, b.shape
    else:
        (K, M), (K2, N) = a.shape, b.shape
    assert K == K2, (a.shape, b.shape, mode)
    tm, tn, tk = _tile(M, tm), _tile(N, tn), _tile(K, tk)
    nk = K // tk
    in_place = out_dtype == F32 or nk == 1

    def body(*refs):
        refs = list(refs)
        a_ref, b_ref = refs[:2]
        r_ref = refs[2] if res is not None else None
        o_ref = refs[3 if res is not None else 2]
        acc = o_ref if in_place else refs[-1]
        k = pl.program_id(2)
        if mode == "nn":
            part = _dot(a_ref[...], b_ref[...], NN)
        elif mode == "nt":
            part = _dot(a_ref[...], b_ref[...], NT)
        else:
            part = _dot(a_ref[...].T, b_ref[...], NN)
        if nk == 1:
            o_ref[...] = (part if res is None else part + r_ref[...].astype(F32)).astype(o_ref.dtype)
            return

        @pl.when(k == 0)
        def _():
            acc[...] = part if res is None else part + r_ref[...].astype(F32)

        @pl.when(k > 0)
        def _():
            acc[...] += part

        if not in_place:
            @pl.when(k == nk - 1)
            def _():
                o_ref[...] = acc[...].astype(o_ref.dtype)

    if mode == "nn":
        a_spec = pl.BlockSpec((tm, tk), lambda i, j, k: (i, k))
        b_spec = pl.BlockSpec((tk, tn), lambda i, j, k: (k, j))
    elif mode == "nt":
        a_spec = pl.BlockSpec((tm, tk), lambda i, j, k: (i, k))
        b_spec = pl.BlockSpec((tn, tk), lambda i, j, k: (j, k))
    else:
        a_spec = pl.BlockSpec((tk, tm), lambda i, j, k: (k, i))
        b_spec = pl.BlockSpec((tk, tn), lambda i, j, k: (k, j))
    o_spec = pl.BlockSpec((tm, tn), lambda i, j, k: (i, j))
    in_specs = [a_spec, b_spec] + ([] if res is None else [o_spec])
    args = (a, b) + (() if res is None else (res,))
    return _pcall(
        body, name=name, dims=("parallel", "parallel", "arbitrary"),
        grid=(M // tm, N // tn, nk), in_specs=in_specs, out_specs=o_spec,
        out_shape=jax.ShapeDtypeStruct((M, N), out_dtype),
        scratch_shapes=[] if in_place else [pltpu.VMEM((tm, tn), F32)],
    )(*args)


EXCHANGE_DTYPE = jnp.bfloat16


def _dw(a, b, *, name):
    return _mm(a, b, mode="tn", name=name, out_dtype=EXCHANGE_DTYPE)


def _rms_parts(xf):
    r = lax.rsqrt(jnp.mean(xf * xf, axis=-1, keepdims=True) + NORM_EPS)
    return xf * r, r


def _rms_bwd_rows(xf, g, dy):
    xhat, r = _rms_parts(xf)
    dxhat = dy * g
    dx = r * (dxhat - xhat * jnp.mean(dxhat * xhat, axis=-1, keepdims=True))
    return dx, dy * xhat


def _rmsnorm(x, g, *, name, out_dtype=MXU_DTYPE, tm=512):
    M, D = x.shape
    tm = _tile(M, tm)

    def body(x_ref, g_ref, o_ref):
        xhat, _ = _rms_parts(x_ref[...].astype(F32))
        o_ref[...] = (xhat * g_ref[...]).astype(o_ref.dtype)

    return _pcall(
        body, name=name, dims=("parallel",), grid=(M // tm,),
        in_specs=[pl.BlockSpec((tm, D), lambda i: (i, 0)), pl.BlockSpec((1, D), lambda i: (0, 0))],
        out_specs=pl.BlockSpec((tm, D), lambda i: (i, 0)),
        out_shape=jax.ShapeDtypeStruct((M, D), out_dtype),
    )(x, g.reshape(1, D))


def _rmsnorm_bwd(x, g, dy, *, name, dres=None, tm=512):
    M, D = x.shape
    tm = _tile(M, tm)

    def body(*refs):
        if dres is None:
            x_ref, g_ref, dy_ref, dx_ref, dg_ref = refs
        else:
            x_ref, g_ref, dy_ref, dr_ref, dx_ref, dg_ref = refs
        dx, dgp = _rms_bwd_rows(x_ref[...].astype(F32), g_ref[...], dy_ref[...].astype(F32))
        if dres is not None:
            dx = dx + dr_ref[...]
        dx_ref[...] = dx

        @pl.when(pl.program_id(0) == 0)
        def _():
            dg_ref[...] = jnp.zeros_like(dg_ref)

        dg_ref[...] += jnp.sum(dgp, axis=0, keepdims=True)

    row = pl.BlockSpec((tm, D), lambda i: (i, 0))
    vec = pl.BlockSpec((1, D), lambda i: (0, 0))
    in_specs = [row, vec, row] + ([] if dres is None else [row])
    args = (x, g.reshape(1, D), dy) + (() if dres is None else (dres,))
    return _pcall(
        body, name=name, dims=("arbitrary",), grid=(M // tm,), in_specs=in_specs, out_specs=[row, vec],
        out_shape=[jax.ShapeDtypeStruct((M, D), F32), jax.ShapeDtypeStruct((1, D), F32)],
    )(*args)


def _rope_chunk(t, c, s, half):
    lane = _lane(t.shape)
    swapped = jnp.where((lane % (2 * half)) < half, pltpu.roll(t, LANES - half, 1), pltpu.roll(t, half, 1))
    return t * c + swapped * s


def _rope_tables(positions):
    pos = positions.reshape(-1).astype(F32)[:, None]

    def table(dh, first, copies, sine, fill=0.0):
        half = dh // 2
        inv_freq = ROPE_THETA ** (-jnp.arange(0, dh, 2, dtype=F32) / dh)
        lane = np.arange(LANES)
        inside = (lane >= first) & (lane < first + copies * dh)
        idx = np.where(inside, (lane - first) % half, 0)
        sign = np.where((lane - first) % dh < half, -1.0, 1.0) if sine else np.ones(LANES)
        ang = pos * inv_freq[idx][None, :]
        val = (jnp.sin(ang) if sine else jnp.cos(ang)) * jnp.asarray(sign, F32)[None, :]
        return jnp.where(jnp.asarray(inside)[None, :], val, fill)

    return dict(
        c64=table(HEAD_DIM, 0, 2, False), s64=table(HEAD_DIM, 0, 2, True),
        ck=table(MLA_ROPE, 0, 1, False), sk=table(MLA_ROPE, 0, 1, True),
        cm=jnp.where(jnp.asarray(np.arange(LANES) < MLA_NOPE)[None, :], 1.0, table(MLA_ROPE, MLA_NOPE, 1, False)),
        sm=table(MLA_ROPE, MLA_NOPE, 1, True),
    )


def _attn_steps(mode, n_other, t_self, t_other):
    if mode == "band":
        assert t_self == t_other
        return 2
    return n_other


def _kv_block(mode, qi, kj):
    if mode == "band":
        return jnp.maximum(qi - 1 + kj, 0), (qi + kj) >= 1
    if mode == "causal":
        return jnp.minimum(kj, qi), kj <= qi
    return kj, None


def _q_block(mode, ki, qj, nq):
    if mode == "band":
        return jnp.minimum(ki + qj, nq - 1), (ki + qj) <= nq - 1
    if mode == "causal":
        return jnp.maximum(qj, ki), qj >= ki
    return qj, None


def _mask(mode, max_dist, qpos, kpos):
    d = qpos - kpos
    if mode == "band":
        return (d >= 0) & (d <= max_dist)
    if mode == "causal":
        return d >= 0
    return None


def _when(cond, fn):
    if cond is None:
        fn()
    else:
        pl.when(cond)(fn)


class _Attn:
    def __init__(self, *, T, Tk, G, nh, rep, dqk, dv, tq, tk, mode, scale, qcol, kcol, vcol, ocol, o_width,
                 max_dist=0):
        self.__dict__.update(locals())
        self.nkv = nh // rep
        assert T % tq == 0 and Tk % tk == 0 and nh <= LANES


def _attn_fwd(cfg, q, k, v, *, name, sink=None, out_dtype=F32):
    c = cfg
    nq, nk = c.T // c.tq, c.Tk // c.tk
    steps = _attn_steps(c.mode, nk, c.tq, c.tk)

    def body(*refs):
        if sink is None:
            q_ref, k_ref, v_ref, o_ref, lse_ref, m_scr, l_scr, acc = refs
        else:
            q_ref, k_ref, v_ref, sink_ref, o_ref, lse_ref, m_scr, l_scr, acc = refs
        qi, kj = pl.program_id(1), pl.program_id(2)
        kb, valid = _kv_block(c.mode, qi, kj)

        @pl.when(kj == 0)
        def _():
            if sink is None:
                m_scr[...] = jnp.full_like(m_scr, NEG_BIG)
                l_scr[...] = jnp.zeros_like(l_scr)
            else:
                m_scr[...] = jnp.broadcast_to(sink_ref[...], m_scr.shape)
                l_scr[...] = jnp.ones_like(l_scr)
            acc[...] = jnp.zeros_like(acc)

        def step():
            qpos = qi * c.tq + lax.broadcasted_iota(jnp.int32, (c.tq, c.tk), 0)
            kpos = kb * c.tk + lax.broadcasted_iota(jnp.int32, (c.tq, c.tk), 1)
            mask = _mask(c.mode, c.max_dist, qpos, kpos)
            for j in range(c.nh):
                g = j // c.rep
                s = _dot(q_ref[:, j * c.dqk:(j + 1) * c.dqk], k_ref[:, g * c.dqk:(g + 1) * c.dqk], NT) * c.scale
                if mask is not None:
                    s = jnp.where(mask, s, -jnp.inf)
                m_prev = m_scr[:, j:j + 1]
                m_new = jnp.maximum(m_prev, jnp.max(s, axis=1, keepdims=True))
                alpha = jnp.exp(m_prev - m_new)
                p = jnp.exp(s - m_new)
                l_scr[:, j:j + 1] = alpha * l_scr[:, j:j + 1] + jnp.sum(p, axis=1, keepdims=True)
                acc[:, j * c.dv:(j + 1) * c.dv] = (
                    alpha * acc[:, j * c.dv:(j + 1) * c.dv] + _dot(p, v_ref[:, g * c.dv:(g + 1) * c.dv], NN))
                m_scr[:, j:j + 1] = m_new

        _when(valid, step)

        @pl.when(kj == steps - 1)
        def _():
            for j in range(c.nh):
                o_ref[:, j * c.dv:(j + 1) * c.dv] = (
                    acc[:, j * c.dv:(j + 1) * c.dv] / l_scr[:, j:j + 1]).astype(o_ref.dtype)
            lane = _lane((c.tq, LANES))
            lse_ref[...] = jnp.where(lane < c.nh, m_scr[...] + jnp.log(jnp.maximum(l_scr[...], 1e-37)), 0.0)

    in_specs = [
        pl.BlockSpec((c.tq, c.nh * c.dqk), lambda g, i, j: (i, c.qcol(g))),
        pl.BlockSpec((c.tk, c.nkv * c.dqk), lambda g, i, j: (_kv_block(c.mode, i, j)[0], c.kcol(g))),
        pl.BlockSpec((c.tk, c.nkv * c.dv), lambda g, i, j: (_kv_block(c.mode, i, j)[0], c.vcol(g))),
    ]
    args = [q, k, v]
    if sink is not None:
        in_specs.append(pl.BlockSpec((1, LANES), lambda g, i, j: (0, 0)))
        args.append(sink)
    return _pcall(
        body, name=name, dims=("parallel", "parallel", "arbitrary"), grid=(c.G, nq, steps),
        in_specs=in_specs,
        out_specs=[pl.BlockSpec((c.tq, c.nh * c.dv), lambda g, i, j: (i, c.ocol(g))),
                   pl.BlockSpec((c.tq, LANES), lambda g, i, j: (i, g))],
        out_shape=[jax.ShapeDtypeStruct((c.T, c.o_width), out_dtype),
                   jax.ShapeDtypeStruct((c.T, LANES * c.G), F32)],
        scratch_shapes=[pltpu.VMEM((c.tq, LANES), F32), pltpu.VMEM((c.tq, LANES), F32),
                        pltpu.VMEM((c.tq, c.nh * c.dv), F32)],
    )(*args)


def _attn_delta(cfg, o, do, *, name, w=None, lse=None, sink=None, tm=512):
    c = cfg
    tm = _tile(c.T, tm)
    width = c.nh * c.dv

    def body(*refs):
        refs = list(refs)
        o_ref, do_ref = refs[:2]
        rest = refs[2:]
        w_ref = rest.pop(0) if w is not None else None
        lse_ref, sink_ref = (rest.pop(0), rest.pop(0)) if sink is not None else (None, None)
        d_ref = rest.pop(0)
        prod = o_ref[...].astype(F32) * do_ref[...].astype(F32)
        cols = [jnp.sum(prod[:, j * c.dv:(j + 1) * c.dv], axis=1, keepdims=True) for j in range(c.nh)]
        delta = _cols_to_lanes(cols, tm)
        if w is not None:
            delta = delta * w_ref[...]
        d_ref[...] = delta
        if sink is not None:
            ds_ref = rest.pop(0)

            @pl.when(pl.program_id(1) == 0)
            def _():
                ds_ref[...] = jnp.zeros_like(ds_ref)

            lane = _lane((tm, LANES))
            ps = jnp.where(lane < c.nh, jnp.exp(sink_ref[...] - lse_ref[...]), 0.0)
            ds_ref[...] -= jnp.sum(ps * delta, axis=0, keepdims=True)

    stat = pl.BlockSpec((tm, LANES), lambda g, i: (i, g))
    in_specs = [pl.BlockSpec((tm, width), lambda g, i: (i, c.ocol(g)))] * 2
    args = [o, do]
    out_specs, out_shape = [stat], [jax.ShapeDtypeStruct((c.T, LANES * c.G), F32)]
    if w is not None:
        in_specs.append(stat)
        args.append(w)
    if sink is not None:
        assert c.G == 1
        in_specs += [stat, pl.BlockSpec((1, LANES), lambda g, i: (0, 0))]
        args += [lse, sink]
        out_specs.append(pl.BlockSpec((1, LANES), lambda g, i: (0, 0)))
        out_shape.append(jax.ShapeDtypeStruct((1, LANES), F32))
    out = _pcall(
        body, name=name, dims=("arbitrary", "arbitrary"), grid=(c.G, c.T // tm),
        in_specs=in_specs, out_specs=out_specs, out_shape=out_shape,
    )(*args)
    return out if sink is not None else (out[0], None)


def _attn_dq(cfg, q, k, v, do, lse, delta, *, name, init=None, out_dtype=F32):
    c = cfg
    nq, nk = c.T // c.tq, c.Tk // c.tk
    steps = _attn_steps(c.mode, nk, c.tq, c.tk)
    qw = c.nh * c.dqk

    def body(*refs):
        if init is None:
            q_ref, k_ref, v_ref, do_ref, lse_ref, d_ref, dq_ref, acc = refs
        else:
            q_ref, k_ref, v_ref, do_ref, lse_ref, d_ref, init_ref, dq_ref, acc = refs
        qi, kj = pl.program_id(1), pl.program_id(2)
        kb, valid = _kv_block(c.mode, qi, kj)

        @pl.when(kj == 0)
        def _():
            acc[...] = jnp.zeros_like(acc) if init is None else init_ref[...].astype(F32)

        def step():
            qpos = qi * c.tq + lax.broadcasted_iota(jnp.int32, (c.tq, c.tk), 0)
            kpos = kb * c.tk + lax.broadcasted_iota(jnp.int32, (c.tq, c.tk), 1)
            mask = _mask(c.mode, c.max_dist, qpos, kpos)
            for j in range(c.nh):
                g = j // c.rep
                kh = k_ref[:, g * c.dqk:(g + 1) * c.dqk]
                s = _dot(q_ref[:, j * c.dqk:(j + 1) * c.dqk], kh, NT) * c.scale
                if mask is not None:
                    s = jnp.where(mask, s, -jnp.inf)
                p = jnp.exp(s - lse_ref[:, j:j + 1])
                dp = _dot(do_ref[:, j * c.dv:(j + 1) * c.dv], v_ref[:, g * c.dv:(g + 1) * c.dv], NT)
                ds = p * (dp - d_ref[:, j:j + 1]) * c.scale
                acc[:, j * c.dqk:(j + 1) * c.dqk] += _dot(ds, kh, NN)

        _when(valid, step)

        @pl.when(kj == steps - 1)
        def _():
            dq_ref[...] = acc[...].astype(dq_ref.dtype)

    kvb = lambda i, j: _kv_block(c.mode, i, j)[0]
    qspec = pl.BlockSpec((c.tq, qw), lambda g, i, j: (i, c.qcol(g)))
    stat = pl.BlockSpec((c.tq, LANES), lambda g, i, j: (i, g))
    in_specs = [
        qspec,
        pl.BlockSpec((c.tk, c.nkv * c.dqk), lambda g, i, j: (kvb(i, j), c.kcol(g))),
        pl.BlockSpec((c.tk, c.nkv * c.dv), lambda g, i, j: (kvb(i, j), c.vcol(g))),
        pl.BlockSpec((c.tq, c.nh * c.dv), lambda g, i, j: (i, c.ocol(g))),
        stat, stat,
    ]
    args = [q, k, v, do, lse, delta]
    dq_spec = pl.BlockSpec((c.tq, qw), lambda g, i, j: (i, g))
    if init is not None:
        in_specs.append(dq_spec)
        args.append(init)
    return _pcall(
        body, name=name, dims=("parallel", "parallel", "arbitrary"), grid=(c.G, nq, steps),
        in_specs=in_specs, out_specs=dq_spec,
        out_shape=jax.ShapeDtypeStruct((c.T, c.G * qw), out_dtype),
        scratch_shapes=[pltpu.VMEM((c.tq, qw), F32)],
    )(*args)


def _attn_dkv(cfg, q, k, v, do, lse, delta, *, name, init=None, out_dtype=F32):
    c = cfg
    nq, nk = c.T // c.tq, c.Tk // c.tk
    steps = _attn_steps(c.mode, nq, c.tk, c.tq)
    kw, vw = c.nkv * c.dqk, c.nkv * c.dv

    def body(*refs):
        if init is None:
            q_ref, k_ref, v_ref, do_ref, lse_ref, d_ref, dk_ref, dv_ref, dk_acc, dv_acc = refs
        else:
            q_ref, k_ref, v_ref, do_ref, lse_ref, d_ref, ik_ref, iv_ref, dk_ref, dv_ref, dk_acc, dv_acc = refs
        ki, qj = pl.program_id(1), pl.program_id(2)
        qb, valid = _q_block(c.mode, ki, qj, nq)

        @pl.when(qj == 0)
        def _():
            dk_acc[...] = jnp.zeros_like(dk_acc) if init is None else ik_ref[...].astype(F32)
            dv_acc[...] = jnp.zeros_like(dv_acc) if init is None else iv_ref[...].astype(F32)

        def step():
            kpos = ki * c.tk + lax.broadcasted_iota(jnp.int32, (c.tk, c.tq), 0)
            qpos = qb * c.tq + lax.broadcasted_iota(jnp.int32, (c.tk, c.tq), 1)
            mask = _mask(c.mode, c.max_dist, qpos, kpos)
            lse_t = lse_ref[...].T
            d_t = d_ref[...].T
            for j in range(c.nh):
                g = j // c.rep
                qh = q_ref[:, j * c.dqk:(j + 1) * c.dqk]
                doh = do_ref[:, j * c.dv:(j + 1) * c.dv]
                s_t = _dot(k_ref[:, g * c.dqk:(g + 1) * c.dqk], qh, NT) * c.scale
                if mask is not None:
                    s_t = jnp.where(mask, s_t, -jnp.inf)
                p_t = jnp.exp(s_t - lse_t[j:j + 1, :])
                dv_acc[:, g * c.dv:(g + 1) * c.dv] += _dot(p_t, doh, NN)
                dp_t = _dot(v_ref[:, g * c.dv:(g + 1) * c.dv], doh, NT)
                ds_t = p_t * (dp_t - d_t[j:j + 1, :]) * c.scale
                dk_acc[:, g * c.dqk:(g + 1) * c.dqk] += _dot(ds_t, qh, NN)

        _when(valid, step)

        @pl.when(qj == steps - 1)
        def _():
            dk_ref[...] = dk_acc[...].astype(dk_ref.dtype)
            dv_ref[...] = dv_acc[...].astype(dv_ref.dtype)

    qbi = lambda i, j: _q_block(c.mode, i, j, nq)[0]
    stat = pl.BlockSpec((c.tq, LANES), lambda g, i, j: (qbi(i, j), g))
    in_specs = [
        pl.BlockSpec((c.tq, c.nh * c.dqk), lambda g, i, j: (qbi(i, j), c.qcol(g))),
        pl.BlockSpec((c.tk, kw), lambda g, i, j: (i, c.kcol(g))),
        pl.BlockSpec((c.tk, vw), lambda g, i, j: (i, c.vcol(g))),
        pl.BlockSpec((c.tq, c.nh * c.dv), lambda g, i, j: (qbi(i, j), c.ocol(g))),
        stat, stat,
    ]
    args = [q, k, v, do, lse, delta]
    dk_spec = pl.BlockSpec((c.tk, kw), lambda g, i, j: (i, g))
    dv_spec = pl.BlockSpec((c.tk, vw), lambda g, i, j: (i, g))
    if init is not None:
        in_specs += [dk_spec, dv_spec]
        args += list(init)
    return _pcall(
        body, name=name, dims=("parallel", "parallel", "arbitrary"), grid=(c.G, nk, steps),
        in_specs=in_specs, out_specs=[dk_spec, dv_spec],
        out_shape=[jax.ShapeDtypeStruct((c.Tk, c.G * kw), out_dtype),
                   jax.ShapeDtypeStruct((c.Tk, c.G * vw), out_dtype)],
        scratch_shapes=[pltpu.VMEM((c.tk, kw), F32), pltpu.VMEM((c.tk, vw), F32)],
    )(*args)


TN = (((0,), (0,)), ((), ()))


def _band_mask(c, i):
    key = lax.broadcasted_iota(jnp.int32, (2 * BLOCK, BLOCK), 0)
    qry = lax.broadcasted_iota(jnp.int32, (2 * BLOCK, BLOCK), 1)
    d = BLOCK + qry - key
    return (d >= 0) & (d <= c.max_dist) & ((key >= BLOCK) | (i > 0))


def _head_pairs(c):
    return c.rep == 1 and c.dqk == c.dv == LANES // 2 and c.nh % 2 == 0


def _block_diagonal(pair):
    lane = _lane(pair.shape)
    zero = jnp.zeros_like(pair)
    return jnp.concatenate([jnp.where(lane < LANES // 2, pair, zero), jnp.where(lane >= LANES // 2, pair, zero)], axis=0)


def _own_blocks(t):
    n = t.shape[1] // 2
    rows = lax.broadcasted_iota(jnp.int32, (LANES, n), 0)
    return jnp.where(rows < LANES // 2, t[:, :n], t[:, n:])


def _rows_to_stats(rows, n):
    return jnp.concatenate(rows + [jnp.zeros((LANES - len(rows), n), F32)], axis=0).T


def _band_fwd(cfg, q, k, v, *, name, sink=None, out_dtype=F32):
    c = cfg
    assert c.mode == "band" and c.tq == c.tk == BLOCK and c.T == c.Tk
    nq = c.T // BLOCK

    def body(*refs):
        if sink is None:
            q_ref, kp_ref, kc_ref, vp_ref, vc_ref, o_ref, lse_ref = refs
        else:
            q_ref, kp_ref, kc_ref, vp_ref, vc_ref, sink_ref, o_ref, lse_ref = refs
        mask = _band_mask(c, pl.program_id(1))
        k2 = jnp.concatenate([kp_ref[...], kc_ref[...]], axis=0)
        v2 = jnp.concatenate([vp_ref[...], vc_ref[...]], axis=0)
        lses = []
        if _head_pairs(c):
            mask2 = jnp.concatenate([mask, mask], axis=1)
            pair_lanes = [slice(pc * LANES, (pc + 1) * LANES) for pc in range(c.nh // 2)]
            score = lambda sl: _dot(k2[:, sl], _block_diagonal(q_ref[:, sl]), NT)
            ahead, behind = score(pair_lanes[0]), None

            def finish(entry):
                sl, o_t, l = entry
                o_ref[:, sl] = _own_blocks(o_t / l).T.astype(o_ref.dtype)

            for pc, sl in enumerate(pair_lanes):
                s = ahead * c.scale
                if pc + 1 < len(pair_lanes):
                    ahead = score(pair_lanes[pc + 1])
                s = jnp.where(mask2, s, -jnp.inf)
                m = jnp.max(s, axis=0, keepdims=True)
                p = jnp.exp(s - m)
                l = jnp.sum(p, axis=0, keepdims=True)
                if behind is not None:
                    finish(behind)
                behind = (sl, _dot(v2[:, sl], p, TN), l)
                lse = m + jnp.log(l)
                lses += [lse[:, :BLOCK], lse[:, BLOCK:]]
            finish(behind)
        heads = [] if _head_pairs(c) else list(range(c.nh))
        score_of = lambda j: _dot(k2[:, (j // c.rep) * c.dqk:(j // c.rep + 1) * c.dqk],
                                  q_ref[:, j * c.dqk:(j + 1) * c.dqk], NT)
        ahead = score_of(0) if heads else None
        for j in heads:
            g = j // c.rep
            s = ahead * c.scale
            if j + 1 < c.nh:
                ahead = score_of(j + 1)
            s = jnp.where(mask, s, -jnp.inf)
            m = jnp.max(s, axis=0, keepdims=True)
            if sink is not None:
                sk = sink_ref[:, j:j + 1]
                m = jnp.maximum(m, sk)
            p = jnp.exp(s - m)
            l = jnp.sum(p, axis=0, keepdims=True)
            if sink is not None:
                l = l + jnp.exp(sk - m)
            o_t = _dot(v2[:, g * c.dv:(g + 1) * c.dv], p, TN)
            o_ref[:, j * c.dv:(j + 1) * c.dv] = (o_t / l).T.astype(o_ref.dtype)
            lses.append(m + jnp.log(l))
        lse_ref[...] = _rows_to_stats(lses, BLOCK)

    prev = lambda i: jnp.maximum(i - 1, 0)
    kw, vw = c.nkv * c.dqk, c.nkv * c.dv
    in_specs = [
        pl.BlockSpec((BLOCK, c.nh * c.dqk), lambda g, i: (i, c.qcol(g))),
        pl.BlockSpec((BLOCK, kw), lambda g, i: (prev(i), c.kcol(g))),
        pl.BlockSpec((BLOCK, kw), lambda g, i: (i, c.kcol(g))),
        pl.BlockSpec((BLOCK, vw), lambda g, i: (prev(i), c.vcol(g))),
        pl.BlockSpec((BLOCK, vw), lambda g, i: (i, c.vcol(g))),
    ]
    args = [q, k, k, v, v]
    if sink is not None:
        in_specs.append(pl.BlockSpec((1, LANES), lambda g, i: (0, 0)))
        args.append(sink)
    return _pcall(
        body, name=name, dims=("parallel", "parallel"), grid=(c.G, nq), in_specs=in_specs,
        out_specs=[pl.BlockSpec((BLOCK, c.nh * c.dv), lambda g, i: (i, c.ocol(g))),
                   pl.BlockSpec((BLOCK, LANES), lambda g, i: (i, g))],
        out_shape=[jax.ShapeDtypeStruct((c.T, c.o_width), out_dtype),
                   jax.ShapeDtypeStruct((c.T, LANES * c.G), F32)],
    )(*args)


def _band_bwd(cfg, q, k, v, do, lse, delta, *, name):
    c = cfg
    assert c.mode == "band" and c.tq == c.tk == BLOCK and c.T == c.Tk
    nq = c.T // BLOCK
    qw, kw, vw = c.nh * c.dqk, c.nkv * c.dqk, c.nkv * c.dv

    def body(q_ref, kp_ref, kc_ref, vp_ref, vc_ref, do_ref, lse_ref, d_ref, dq_ref, dk_ref, dv_ref, dk_c, dv_c):
        n = pl.program_id(1)

        @pl.when(n == 0)
        def _():
            dk_c[...] = jnp.zeros_like(dk_c)
            dv_c[...] = jnp.zeros_like(dv_c)

        @pl.when(n < nq)
        def _():
            mask = _band_mask(c, n)
            k2 = jnp.concatenate([kp_ref[...], kc_ref[...]], axis=0)
            v2 = jnp.concatenate([vp_ref[...], vc_ref[...]], axis=0)
            lse_t, d_t = lse_ref[...].T, d_ref[...].T
            if _head_pairs(c):
                mask2 = jnp.concatenate([mask, mask], axis=1)
                pair_lanes = [slice(pc * LANES, (pc + 1) * LANES) for pc in range(c.nh // 2)]

                def first(sl):
                    q_bd, do_bd = _block_diagonal(q_ref[:, sl]), _block_diagonal(do_ref[:, sl])
                    return q_bd, do_bd, k2[:, sl], _dot(k2[:, sl], q_bd, NT), _dot(v2[:, sl], do_bd, NT)

                def finish(entry):
                    sl, dq_t, dv_pair, dk_pair = entry
                    dq_ref[:, sl] = _own_blocks(dq_t).T
                    dk_ref[:, sl] = dk_c[:, sl] + dk_pair[:BLOCK]
                    dv_ref[:, sl] = dv_c[:, sl] + dv_pair[:BLOCK]
                    dk_c[:, sl] = dk_pair[BLOCK:]
                    dv_c[:, sl] = dv_pair[BLOCK:]

                ahead, behind = first(pair_lanes[0]), None
                for pc, sl in enumerate(pair_lanes):
                    q_bd, do_bd, kp, s, dp = ahead
                    if pc + 1 < len(pair_lanes):
                        ahead = first(pair_lanes[pc + 1])
                    both = lambda t: jnp.concatenate([t[2 * pc:2 * pc + 1, :], t[2 * pc + 1:2 * pc + 2, :]], axis=1)
                    p = jnp.exp(jnp.where(mask2, s * c.scale, -jnp.inf) - both(lse_t))
                    ds = p * (dp - both(d_t)) * c.scale
                    entry = (sl, _dot(kp, ds, TN), _dot(p, do_bd, NN), _dot(ds, q_bd, NN))
                    if behind is not None:
                        finish(behind)
                    behind = entry
                finish(behind)
                return
            dk2, dv2 = [None] * c.nkv, [None] * c.nkv

            def first_of(j):
                g = j // c.rep
                qh, doh = q_ref[:, j * c.dqk:(j + 1) * c.dqk], do_ref[:, j * c.dv:(j + 1) * c.dv]
                kh = k2[:, g * c.dqk:(g + 1) * c.dqk]
                return qh, doh, kh, _dot(kh, qh, NT), _dot(v2[:, g * c.dv:(g + 1) * c.dv], doh, NT)

            ahead = first_of(0)
            for j in range(c.nh):
                g = j // c.rep
                qh, doh, kh, s, dp = ahead
                if j + 1 < c.nh:
                    ahead = first_of(j + 1)
                p = jnp.exp(jnp.where(mask, s * c.scale, -jnp.inf) - lse_t[j:j + 1, :])
                ds = p * (dp - d_t[j:j + 1, :]) * c.scale
                dq_ref[:, j * c.dqk:(j + 1) * c.dqk] = _dot(kh, ds, TN).T
                dvh, dkh = _dot(p, doh, NN), _dot(ds, qh, NN)
                dv2[g] = dvh if dv2[g] is None else dv2[g] + dvh
                dk2[g] = dkh if dk2[g] is None else dk2[g] + dkh
            for g in range(c.nkv):
                ks, vs = slice(g * c.dqk, (g + 1) * c.dqk), slice(g * c.dv, (g + 1) * c.dv)
                dk_ref[:, ks] = dk_c[:, ks] + dk2[g][:BLOCK]
                dv_ref[:, vs] = dv_c[:, vs] + dv2[g][:BLOCK]
                dk_c[:, ks] = dk2[g][BLOCK:]
                dv_c[:, vs] = dv2[g][BLOCK:]

        @pl.when(n == nq)
        def _():
            dk_ref[...] = dk_c[...]
            dv_ref[...] = dv_c[...]

    cur = lambda n: jnp.minimum(n, nq - 1)
    prev = lambda n: jnp.maximum(cur(n) - 1, 0)
    out_blk = lambda n: jnp.maximum(n - 1, 0)
    stat = pl.BlockSpec((BLOCK, LANES), lambda g, n: (cur(n), g))
    dq_spec = pl.BlockSpec((BLOCK, qw), lambda g, n: (cur(n), g))
    dk_spec = pl.BlockSpec((BLOCK, kw), lambda g, n: (out_blk(n), g))
    dv_spec = pl.BlockSpec((BLOCK, vw), lambda g, n: (out_blk(n), g))
    in_specs = [
        pl.BlockSpec((BLOCK, qw), lambda g, n: (cur(n), c.qcol(g))),
        pl.BlockSpec((BLOCK, kw), lambda g, n: (prev(n), c.kcol(g))),
        pl.BlockSpec((BLOCK, kw), lambda g, n: (cur(n), c.kcol(g))),
        pl.BlockSpec((BLOCK, vw), lambda g, n: (prev(n), c.vcol(g))),
        pl.BlockSpec((BLOCK, vw), lambda g, n: (cur(n), c.vcol(g))),
        pl.BlockSpec((BLOCK, c.nh * c.dv), lambda g, n: (cur(n), c.ocol(g))),
        stat, stat,
    ]
    return _pcall(
        body, name=name, dims=("parallel", "arbitrary"), grid=(c.G, nq + 1), in_specs=in_specs,
        out_specs=[dq_spec, dk_spec, dv_spec],
        out_shape=[_sds((c.T, c.G * qw)), _sds((c.T, c.G * kw)), _sds((c.T, c.G * vw))],
        scratch_shapes=[pltpu.VMEM((BLOCK, kw), F32), pltpu.VMEM((BLOCK, vw), F32)],
    )(q, k, k, v, v, do, lse, delta)


def _causal_pairs(n, kv_major):
    pairs =[(i, j) for j in range(n) for i in range(j, n)] if kv_major else [(i, j) for i in range(n) for j in range(i + 1)]
    return jnp.asarray(np.array([p[0] for p in pairs], np.int32)), jnp.asarray(np.array([p[1] for p in pairs], np.int32))


def _causal_mask(t):
    return lax.broadcasted_iota(jnp.int32, (t, t), 0) >= lax.broadcasted_iota(jnp.int32, (t, t), 1)


def _carrying(body, n_in, n_out, n_scratch, grid, carry):
    if carry is None:
        return body
    G, P = grid

    def wrapped(*refs):
        refs = list(refs)
        prefetch, refs = refs[:2], refs[2:]
        ins, src = refs[:n_in], refs[n_in]
        outs, out = refs[n_in + 1:n_in + 1 + n_out], refs[n_in + 1 + n_out]
        scratch, sems = refs[n_in + 2 + n_out:n_in + 2 + n_out + n_scratch], refs[n_in + 2 + n_out + n_scratch:]
        step = pl.program_id(0) * P + pl.program_id(1)
        carry.run([src, out] + sems, step, G * P, at_end=False)
        body(*prefetch, *ins, *outs, *scratch)
        carry.run([src, out] + sems, step, G * P, at_end=True)

    return wrapped


def _carry_specs(carry):
    if carry is None:
        return [], [], [], [], []
    any_space = pl.BlockSpec(memory_space=pl.ANY)
    return [any_space], [any_space], [carry.out_shape], list(carry.sems), [carry.src]


def _causal_fwd(cfg, q, k, v, *, name, out_dtype=F32, stat_heads=None, carry=None):
    c = cfg
    assert c.mode == "causal" and c.tq == c.tk and c.T == c.Tk
    t, n = c.tq, c.T // c.tq
    stat_heads = stat_heads or c.nh
    stat_blocks = c.nh // stat_heads
    assert stat_blocks * stat_heads == c.nh
    qi_tab, kj_tab = _causal_pairs(n, kv_major=False)
    n_pairs = int(qi_tab.shape[0])

    def body(qi_ref, kj_ref, q_ref, k_ref, v_ref, o_ref, lse_ref, m_scr, l_scr, acc):
        pair = pl.program_id(1)
        qi, kj = qi_ref[pair], kj_ref[pair]

        @pl.when(kj == 0)
        def _():
            m_scr[...] = jnp.full_like(m_scr, NEG_BIG)
            l_scr[...] = jnp.zeros_like(l_scr)
            acc[...] = jnp.zeros_like(acc)

        def step(diagonal):
            mask = None
            if diagonal:
                mask = lax.broadcasted_iota(jnp.int32, (t, t), 1) >= lax.broadcasted_iota(jnp.int32, (t, t), 0)
            scores = [_dot(k_ref[:, (j // c.rep) * c.dqk:(j // c.rep + 1) * c.dqk],
                           q_ref[:, j * c.dqk:(j + 1) * c.dqk], NT) for j in range(c.nh)]
            for j in range(c.nh):
                g = j // c.rep
                s = scores[j] * c.scale
                if diagonal:
                    s = jnp.where(mask, s, -jnp.inf)
                m_prev = m_scr[j]
                m_new = jnp.maximum(m_prev, jnp.max(s, axis=0, keepdims=True))
                alpha = jnp.exp(m_prev - m_new)
                p = jnp.exp(s - m_new)
                l_scr[j] = alpha * l_scr[j] + jnp.sum(p, axis=0, keepdims=True)
                acc[j] = alpha * acc[j] + _dot(v_ref[:, g * c.dv:(g + 1) * c.dv], p, TN)
                m_scr[j] = m_new

        pl.when(kj == qi)(lambda: step(True))
        pl.when(kj != qi)(lambda: step(False))

        @pl.when(kj == qi)
        def _():
            rows = []
            for j in range(c.nh):
                o_ref[:, j * c.dv:(j + 1) * c.dv] = (acc[j] / l_scr[j]).T.astype(o_ref.dtype)
                rows.append(m_scr[j] + jnp.log(l_scr[j]))
            for b in range(stat_blocks):
                lse_ref[:, b * LANES:(b + 1) * LANES] = _rows_to_stats(rows[b * stat_heads:(b + 1) * stat_heads], t)

    x_in, x_out, x_shapes, x_scratch, x_args = _carry_specs(carry)
    grid_spec = pltpu.PrefetchScalarGridSpec(
        num_scalar_prefetch=2, grid=(c.G, n_pairs),
        in_specs=[pl.BlockSpec((t, c.nh * c.dqk), lambda g, p, qi, kj: (qi[p], c.qcol(g))),
                  pl.BlockSpec((t, c.nkv * c.dqk), lambda g, p, qi, kj: (kj[p], c.kcol(g))),
                  pl.BlockSpec((t, c.nkv * c.dv), lambda g, p, qi, kj: (kj[p], c.vcol(g)))] + x_in,
        out_specs=[pl.BlockSpec((t, c.nh * c.dv), lambda g, p, qi, kj: (qi[p], c.ocol(g))),
                   pl.BlockSpec((t, LANES * stat_blocks), lambda g, p, qi, kj: (qi[p], g))] + x_out,
        scratch_shapes=[pltpu.VMEM((c.nh, 1, t), F32), pltpu.VMEM((c.nh, 1, t), F32),
                        pltpu.VMEM((c.nh, c.dv, t), F32)] + x_scratch)
    return _pcall(
        _carrying(body, 3, 2, 3, (c.G, n_pairs), carry), name=name,
        dims=("arbitrary", "arbitrary") if carry is not None else ("parallel", "arbitrary"), grid_spec=grid_spec,
        out_shape=[jax.ShapeDtypeStruct((c.T, c.o_width), out_dtype),
                   jax.ShapeDtypeStruct((c.T, LANES * c.G * stat_blocks), F32)] + x_shapes,
    )(qi_tab, kj_tab, q, k, v, *x_args)


def _causal_bwd(cfg, q, k, v, o, do, lse, *, name, carry=None):
    c = cfg
    assert c.mode == "causal" and c.tq == c.tk and c.T == c.Tk
    t, n = c.tq, c.T // c.tq
    qw, kw, vw = c.nh * c.dqk, c.nkv * c.dqk, c.nkv * c.dv
    qi_tab, kj_tab = _causal_pairs(n, kv_major=True)

    def body(qi_ref, kj_ref, q_ref, k_ref, v_ref, o_ref, do_ref, lse_ref, dq_ref, dk_ref, dv_ref, dk_acc, dv_acc):
        pair = pl.program_id(1)
        qi, kj = qi_ref[pair], kj_ref[pair]

        @pl.when(pair == 0)
        def _():
            dq_ref[...] = jnp.zeros_like(dq_ref)

        @pl.when(qi == kj)
        def _():
            dk_acc[...] = jnp.zeros_like(dk_acc)
            dv_acc[...] = jnp.zeros_like(dv_acc)

        rows = pl.ds(pl.multiple_of(qi * t, t), t)

        def step(diagonal):
            mask = _causal_mask(t) if diagonal else None
            for j in range(c.nh):
                g = j // c.rep
                qs, ks, vs = (slice(j * c.dqk, (j + 1) * c.dqk), slice(g * c.dqk, (g + 1) * c.dqk),
                              slice(g * c.dv, (g + 1) * c.dv))
                qh, doh, kh = q_ref[:, qs], do_ref[:, j * c.dv:(j + 1) * c.dv], k_ref[:, ks]
                s = _dot(qh, kh, NT) * c.scale
                if diagonal:
                    s = jnp.where(mask, s, -jnp.inf)
                p = jnp.exp(s - lse_ref[:, j:j + 1])
                delta = jnp.sum(doh.astype(F32) * o_ref[:, j * c.dv:(j + 1) * c.dv].astype(F32), axis=1, keepdims=True)
                ds = p * (_dot(doh, v_ref[:, vs], NT) - delta) * c.scale
                dq_ref[rows, qs] += _dot(ds, kh, NN)
                dv_acc[g] += _dot(doh, p, TN)
                dk_acc[g] += _dot(qh, ds, TN)

        pl.when(qi == kj)(lambda: step(True))
        pl.when(qi != kj)(lambda: step(False))

        @pl.when(qi == n - 1)
        def _():
            for g in range(c.nkv):
                dk_ref[:, g * c.dqk:(g + 1) * c.dqk] = dk_acc[g].T
                dv_ref[:, g * c.dv:(g + 1) * c.dv] = dv_acc[g].T

    stat = pl.BlockSpec((t, LANES), lambda g, p, qi, kj: (qi[p], g))
    o_spec = pl.BlockSpec((t, c.nh * c.dv), lambda g, p, qi, kj: (qi[p], c.ocol(g)))
    n_pairs = int(qi_tab.shape[0])
    x_in, x_out, x_shapes, x_scratch, x_args = _carry_specs(carry)
    grid_spec = pltpu.PrefetchScalarGridSpec(
        num_scalar_prefetch=2, grid=(c.G, n_pairs),
        in_specs=[pl.BlockSpec((t, qw), lambda g, p, qi, kj: (qi[p], c.qcol(g))),
                  pl.BlockSpec((t, kw), lambda g, p, qi, kj: (kj[p], c.kcol(g))),
                  pl.BlockSpec((t, vw), lambda g, p, qi, kj: (kj[p], c.vcol(g))),
                  o_spec, o_spec, stat] + x_in,
        out_specs=[pl.BlockSpec((c.T, qw), lambda g, p, qi, kj: (0, g)),
                   pl.BlockSpec((t, kw), lambda g, p, qi, kj: (kj[p], g)),
                   pl.BlockSpec((t, vw), lambda g, p, qi, kj: (kj[p], g))] + x_out,
        scratch_shapes=[pltpu.VMEM((c.nkv, c.dqk, t), F32), pltpu.VMEM((c.nkv, c.dv, t), F32)] + x_scratch)
    return _pcall(
        _carrying(body, 6, 3, 2, (c.G, n_pairs), carry), name=name,
        dims=("arbitrary", "arbitrary") if carry is not None else ("parallel", "arbitrary"), grid_spec=grid_spec,
        out_shape=[_sds((c.T, c.G * qw)), _sds((c.T, c.G * kw)), _sds((c.T, c.G * vw))] + x_shapes,
    )(qi_tab, kj_tab, q, k, v, o, do, lse, *x_args)


def _rowwise(body, ins, outs, *, name, rows, tm=512, accs=(), scratch=()):
    tm = _row_tile(rows, tm)

    def spec(a):
        if a.shape[0] == 1:
            return pl.BlockSpec((1, a.shape[1]), lambda i: (0, 0))
        d = rows // a.shape[0]
        assert d * a.shape[0] == rows and tm % d == 0
        return pl.BlockSpec((tm // d, a.shape[1]), lambda i: (i, 0))

    return _pcall(
        functools.partial(body, tm), name=name, dims=("arbitrary" if accs else "parallel",), grid=(rows // tm,),
        in_specs=[spec(a) for a in ins], out_specs=[spec(a) for a in outs], out_shape=list(outs),
        scratch_shapes=list(scratch),
    )(*ins)


def _sds(shape, dtype=F32):
    return jax.ShapeDtypeStruct(shape, dtype)


def _acc_rows(ref, val):
    @pl.when(pl.program_id(0) == 0)
    def _():
        ref[...] = jnp.zeros_like(ref)

    ref[...] += jnp.sum(val, axis=0, keepdims=True)


Z_QA, Z_KA, Z_VA, Z_CQ, Z_CKV, Z_KR, Z_END = 0, 512, 640, 768, 1152, 1408, 1536


def _l0_prep(z, tabs, q_norm, kv_norm, *, name):
    S = z.shape[0]

    def body(tm, z_ref, c64, s64, ck, sk, gq, gkv, qa_o, ka_o, va_o, cq_o, ckv_o, kr_o):
        for i in range(4):
            sl = slice(Z_QA + i * LANES, Z_QA + (i + 1) * LANES)
            qa_o[:, i * LANES:(i + 1) * LANES] = _rope_chunk(z_ref[:, sl], c64[...], s64[...], 32).astype(qa_o.dtype)
        ka_o[...] = _rope_chunk(z_ref[:, Z_KA:Z_VA], c64[...], s64[...], 32).astype(ka_o.dtype)
        va_o[...] = z_ref[:, Z_VA:Z_CQ].astype(va_o.dtype)
        cq_o[...] = (_rms_parts(z_ref[:, Z_CQ:Z_CKV])[0] * gq[...]).astype(cq_o.dtype)
        ckv_o[...] = (_rms_parts(z_ref[:, Z_CKV:Z_KR])[0] * gkv[...]).astype(ckv_o.dtype)
        kr_o[...] = _rope_chunk(z_ref[:, Z_KR:Z_END], ck[...], sk[...], 16)

    outs = [_sds((S, 512), MXU_DTYPE), _sds((S, 128), MXU_DTYPE), _sds((S, 128), MXU_DTYPE),
            _sds((S, MLA_Q_RANK), MXU_DTYPE), _sds((S, MLA_KV_RANK), MXU_DTYPE), _sds((S, LANES))]
    ins = [z, tabs["c64"], tabs["s64"], tabs["ck"], tabs["sk"], q_norm.reshape(1, -1), kv_norm.reshape(1, -1)]
    return _rowwise(body, ins, outs, name=name, rows=S)


def _l0_prep_bwd(z, tabs, q_norm, kv_norm, dqa, dka, dva, dcq, dckv, dkr, *, name):
    S = z.shape[0]

    def body(tm, z_ref, c64, s64, ck, sk, gq, gkv, dqa_r, dka_r, dva_r, dcq_r, dckv_r, dkr_r, dz_o, dgq_o, dgkv_o):
        for i in range(4):
            sl = slice(i * LANES, (i + 1) * LANES)
            dz_o[:, sl] = _rope_chunk(dqa_r[:, sl].astype(F32), c64[...], -s64[...], 32).astype(dz_o.dtype)
        dz_o[:, Z_KA:Z_VA] = _rope_chunk(dka_r[...].astype(F32), c64[...], -s64[...], 32).astype(dz_o.dtype)
        dz_o[:, Z_VA:Z_CQ] = dva_r[...].astype(dz_o.dtype)
        dx, dgp = _rms_bwd_rows(z_ref[:, Z_CQ:Z_CKV], gq[...], dcq_r[...].astype(F32))
        dz_o[:, Z_CQ:Z_CKV] = dx.astype(dz_o.dtype)
        _acc_rows(dgq_o, dgp)
        dx, dgp = _rms_bwd_rows(z_ref[:, Z_CKV:Z_KR], gkv[...], dckv_r[...].astype(F32))
        dz_o[:, Z_CKV:Z_KR] = dx.astype(dz_o.dtype)
        _acc_rows(dgkv_o, dgp)
        dz_o[:, Z_KR:Z_END] = _rope_chunk(dkr_r[...], ck[...], -sk[...], 16).astype(dz_o.dtype)

    outs = [_sds((S, Z_END), MXU_DTYPE), _sds((1, MLA_Q_RANK)), _sds((1, MLA_KV_RANK))]
    ins = [z, tabs["c64"], tabs["s64"], tabs["ck"], tabs["sk"], q_norm.reshape(1, -1), kv_norm.reshape(1, -1),
           dqa, dka, dva, dcq, dckv, dkr]
    return _rowwise(body, ins, outs, name=name, rows=S, accs=(1, 2))


def _mla_prep(qb, kvb, kr, tabs, *, name):
    S = qb.shape[0]

    def body(tm, qb_r, kvb_r, kr_r, cm, sm, q_o, k_o, v_o):
        lane = _lane((tm, LANES))
        kr_at_64 = pltpu.roll(kr_r[...], 64, 1)
        for h in range(MLA_HEADS):
            sl = slice(h * LANES, (h + 1) * LANES)
            q_o[:, sl] = _rope_chunk(qb_r[:, sl], cm[...], sm[...], 16).astype(q_o.dtype)
            k_o[:, sl] = jnp.where(lane < 64, kvb_r[:, sl], kr_at_64).astype(k_o.dtype)
        for p in range(MLA_HEADS // 2):
            even = pltpu.roll(kvb_r[:, (2 * p) * LANES:(2 * p + 1) * LANES], 64, 1)
            odd = kvb_r[:, (2 * p + 1) * LANES:(2 * p + 2) * LANES]
            v_o[:, p * LANES:(p + 1) * LANES] = jnp.where(lane < 64, even, odd).astype(v_o.dtype)

    outs = [_sds((S, 1024), MXU_DTYPE), _sds((S, 1024), MXU_DTYPE), _sds((S, 512), MXU_DTYPE)]
    return _rowwise(body, [qb, kvb, kr, tabs["cm"], tabs["sm"]], outs, name=name, rows=S)


def _mla_prep_bwd(dq, dk, dv, tabs, *, name):
    S = dq.shape[0]

    def body(tm, dq_r, dk_r, dv_r, cm, sm, dqb_o, dkvb_o, dkr_o):
        lane = _lane((tm, LANES))
        dkr = jnp.zeros((tm, LANES), F32)
        for h in range(MLA_HEADS):
            sl = slice(h * LANES, (h + 1) * LANES)
            dqb_o[:, sl] = _rope_chunk(dq_r[:, sl].astype(F32), cm[...], -sm[...], 16).astype(dqb_o.dtype)
            dkh = dk_r[:, sl].astype(F32)
            dvp = dv_r[:, (h // 2) * LANES:(h // 2 + 1) * LANES].astype(F32)
            dvh = pltpu.roll(dvp, 64, 1) if h % 2 == 0 else dvp
            dkvb_o[:, sl] = jnp.where(lane < 64, dkh, dvh).astype(dkvb_o.dtype)
            dkr = dkr + pltpu.roll(dkh, 64, 1)
        dkr_o[...] = jnp.where(lane < MLA_ROPE, dkr, 0.0)

    outs = [_sds((S, 1024), MXU_DTYPE), _sds((S, 1024), MXU_DTYPE), _sds((S, LANES))]
    return _rowwise(body, [dq, dk, dv, tabs["cm"], tabs["sm"]], outs, name=name, rows=S)


DILATIONS = tuple(d for _, d in DIL_PATTERNS)
QKV_CHUNKS = 8


def _to_branch(nat, c0, chunks, out_ref, d, rows):
    width = chunks * LANES
    for r in range(d):
        tok = pl.ds(r, rows // d, stride=d) if d > 1 else slice(None)
        for c in range(chunks):
            out_ref[:, r * width + c * LANES:r * width + (c + 1) * LANES] = nat[c0 + c, tok, :].astype(out_ref.dtype)


def _from_branch(in_ref, nat, c0, chunks, d, rows, add=False):
    width = chunks * LANES
    for r in range(d):
        tok = pl.ds(r, rows // d, stride=d) if d > 1 else slice(None)
        for c in range(chunks):
            val = in_ref[:, r * width + c * LANES:r * width + (c + 1) * LANES].astype(F32)
            nat[c0 + c, tok, :] = nat[c0 + c, tok, :] + val if add else val


def _branch_sds(S, width, d, dtype):
    return _sds((S // d, d * width), dtype)


def _l1_prep(qkv, tabs, *, name):
    S = qkv.shape[0]

    def body(tm, x_r, c64, s64, *rest):
        outs, nat = rest[:-1], rest[-1]
        for i in range(QKV_CHUNKS):
            sl = slice(i * LANES, (i + 1) * LANES)
            nat[i] = _rope_chunk(x_r[:, sl], c64[...], s64[...], 32)
            nat[QKV_CHUNKS + i] = _rope_chunk(x_r[:, 1024 + i * LANES:1024 + (i + 1) * LANES], c64[...], s64[...], 32)
            nat[2 * QKV_CHUNKS + i] = x_r[:, 2048 + i * LANES:2048 + (i + 1) * LANES]
        for b, d in enumerate(DILATIONS):
            for t in range(3):
                _to_branch(nat, t * QKV_CHUNKS, QKV_CHUNKS, outs[3 * b + t], d, tm)

    outs = [_branch_sds(S, 1024, d, MXU_DTYPE) for d in DILATIONS for _ in range(3)]
    got = _rowwise(body, [qkv, tabs["c64"], tabs["s64"]], outs, name=name, rows=S,
                   scratch=[pltpu.VMEM((3 * QKV_CHUNKS, _row_tile(S, 512), LANES), F32)])
    return {d: tuple(got[3 * b:3 * b + 3]) for b, d in enumerate(DILATIONS)}


def _l1_prep_bwd(grads, tabs, *, name):
    S = grads[1][0].shape[0]

    def body(tm, *rest):
        ins, (c64, s64, o, nat) = rest[:9], rest[9:]
        for b, d in enumerate(DILATIONS):
            for t in range(3):
                _from_branch(ins[3 * b + t], nat, t * QKV_CHUNKS, QKV_CHUNKS, d, tm, add=b > 0)
        for i in range(QKV_CHUNKS):
            sl = slice(i * LANES, (i + 1) * LANES)
            o[:, sl] = _rope_chunk(nat[i], c64[...], -s64[...], 32).astype(o.dtype)
            o[:, 1024 + i * LANES:1024 + (i + 1) * LANES] = _rope_chunk(
                nat[QKV_CHUNKS + i], c64[...], -s64[...], 32).astype(o.dtype)
            o[:, 2048 + i * LANES:2048 + (i + 1) * LANES] = nat[2 * QKV_CHUNKS + i].astype(o.dtype)

    ins = [g for d in DILATIONS for g in grads[d]] + [tabs["c64"], tabs["s64"]]
    return _rowwise(body, ins, [_sds((S, 3072), MXU_DTYPE)], name=name, rows=S, tm=256,
                    scratch=[pltpu.VMEM((3 * QKV_CHUNKS, _row_tile(S, 256), LANES), F32)])[0]


def _sigmoid(x):
    return 1.0 / (1.0 + jnp.exp(-x))


FFN_ROW_TILE, FFN_COL_TILE = 512, 1408


def _gate_up(h, w_gate, w_up, *, name):
    (M, K), N = h.shape, w_gate.shape[1]
    tm, tn = _tile(M, FFN_ROW_TILE), _tile(N, FFN_COL_TILE)

    def body(h_ref, wg_ref, wu_ref, g_ref, u_ref, a_ref):
        g = _dot(h_ref[...], wg_ref[...], NN)
        u = _dot(h_ref[...], wu_ref[...], NN)
        g_ref[...] = g
        u_ref[...] = u
        a_ref[...] = (g * _sigmoid(g) * u).astype(a_ref.dtype)

    w_spec = pl.BlockSpec((K, tn), lambda j, i: (0, j))
    o_spec = pl.BlockSpec((tm, tn), lambda j, i: (i, j))
    return _pcall(
        body, name=name, dims=("parallel", "parallel"), grid=(N // tn, M // tm),
        in_specs=[pl.BlockSpec((tm, K), lambda j, i: (i, 0)), w_spec, w_spec], out_specs=[o_spec] * 3,
        out_shape=[_sds((M, N)), _sds((M, N)), _sds((M, N), MXU_DTYPE)],
    )(h, w_gate, w_up)


def _gate_up_bwd(dx, w_down, gate, up, *, name):
    (M, K), N = dx.shape, w_down.shape[0]
    tm, tn = _tile(M, FFN_ROW_TILE), _tile(N, FFN_COL_TILE)

    def body(dx_ref, w_ref, g_ref, u_ref, dg_ref, du_ref):
        d = _dot(dx_ref[...], w_ref[...], NT)
        g = g_ref[...]
        sg = _sigmoid(g)
        dg_ref[...] = (d * u_ref[...] * (sg * (1.0 + g * (1.0 - sg)))).astype(dg_ref.dtype)
        du_ref[...] = (d * g * sg).astype(du_ref.dtype)

    o_spec = pl.BlockSpec((tm, tn), lambda j, i: (i, j))
    return _pcall(
        body, name=name, dims=("parallel", "parallel"), grid=(N // tn, M // tm),
        in_specs=[pl.BlockSpec((tm, K), lambda j, i: (i, 0)), pl.BlockSpec((tn, K), lambda j, i: (j, 0)),
                  o_spec, o_spec],
        out_specs=[o_spec] * 2, out_shape=[_sds((M, N), MXU_DTYPE)] * 2,
    )(dx, w_down, gate, up)


def _head_pair_weights(w, c, rows):
    return jnp.where(_lane((rows, LANES)) < HEAD_DIM, w[:, 2 * c:2 * c + 1], w[:, 2 * c + 1:2 * c + 2])


def _merge(outs_by_d, lses_by_d, *, name):
    S = outs_by_d[1].shape[0]
    far = DILATIONS[1:]

    def body(tm, o1, o4, o16, l1, l4, l16, o_o, w1_o, w4_o, w16_o, nat_o, nat_l):
        for b, (o_r, l_r, d) in enumerate(zip((o4, o16), (l4, l16), far)):
            _from_branch(o_r, nat_o, b * QKV_CHUNKS, QKV_CHUNKS, d, tm)
            _from_branch(l_r, nat_l, b, 1, d, tm)
        ls = [l1[...], nat_l[0], nat_l[1]]
        m = jnp.maximum(jnp.maximum(ls[0], ls[1]), ls[2])
        es = [jnp.exp(l - m) for l in ls]
        tot = es[0] + es[1] + es[2]
        ws = [e / tot for e in es]
        for w_o, w in zip((w1_o, w4_o, w16_o), ws):
            w_o[...] = w
        for c in range(QKV_CHUNKS):
            sl = slice(c * LANES, (c + 1) * LANES)
            parts = (o1[:, sl], nat_o[c], nat_o[QKV_CHUNKS + c])
            o_o[:, sl] = sum(_head_pair_weights(w, c, tm) * part for w, part in zip(ws, parts))

    ins = [outs_by_d[d] for d in DILATIONS] + [lses_by_d[d] for d in DILATIONS]
    outs = [_sds((S, 1024))] + [_sds((S, LANES))] * 3
    rows = _row_tile(S, 256)
    return _rowwise(body, ins, outs, name=name, rows=S, tm=256,
                    scratch=[pltpu.VMEM((2 * QKV_CHUNKS, rows, LANES), F32), pltpu.VMEM((2, rows, LANES), F32)])


def _merge_bwd(do, o, ws, *, name):
    S = do.shape[0]

    def body(tm, do_r, o_r, w1, w4, w16, d1, d4, d16, e1, e4, e16, nat, nat_l):
        prod = do_r[...] * o_r[...]
        sums = _cols_to_lanes([jnp.sum(prod[:, j * HEAD_DIM:(j + 1) * HEAD_DIM], axis=1, keepdims=True)
                               for j in range(DIL_HEADS)], tm)
        for w_r, d_o, e_o, d in zip((w1, w4, w16), (d1, d4, d16), (e1, e4, e16), DILATIONS):
            w = w_r[...]
            nat_l[0] = w * sums
            _to_branch(nat_l, 0, 1, e_o, d, tm)
            for c in range(QKV_CHUNKS):
                nat[c] = _head_pair_weights(w, c, tm) * do_r[:, c * LANES:(c + 1) * LANES]
            _to_branch(nat, 0, QKV_CHUNKS, d_o, d, tm)

    outs = [_branch_sds(S, 1024, d, MXU_DTYPE) for d in DILATIONS] + [_branch_sds(S, LANES, d, F32) for d in DILATIONS]
    rows = _row_tile(S, 256)
    got = _rowwise(body, [do, o] + [ws[d] for d in DILATIONS], outs, name=name, rows=S, tm=256,
                   scratch=[pltpu.VMEM((QKV_CHUNKS, rows, LANES), F32), pltpu.VMEM((1, rows, LANES), F32)])
    return dict(zip(DILATIONS, got[:3])), dict(zip(DILATIONS, got[3:]))


def _loss_head(x, g, target, *, name):
    S, D = x.shape

    def body(tm, x_r, g_r, t_r, dx_o, dg_o, sq_o):
        xf = x_r[...]
        xhat, _ = _rms_parts(xf)
        err = xhat * g_r[...] - t_r[...]
        dx, dgp = _rms_bwd_rows(xf, g_r[...], err * (1.0 / D))
        dx_o[...] = dx
        _acc_rows(dg_o, dgp)
        _acc_rows(sq_o, err * err)

    return _rowwise(body, [x, g.reshape(1, D), target], [_sds((S, D)), _sds((1, D)), _sds((1, D))],
                    name=name, rows=S, accs=(1, 2))


def _adamw(w, g, m, v, *, name):
    c1 = 1.0 - ADAM_B1 ** ADAM_STEP
    c2 = 1.0 - ADAM_B2 ** ADAM_STEP

    def body(tm, w_r, g_r, m_r, v_r, d_o, m_o, v_o):
        g = g_r[...]
        m_new = ADAM_B1 * m_r[...] + (1.0 - ADAM_B1) * g
        v_new = ADAM_B2 * v_r[...] + (1.0 - ADAM_B2) * (g * g)
        m_o[...] = m_new
        v_o[...] = v_new
        d_o[...] = -ADAM_LR * ((m_new / c1) / (jnp.sqrt(v_new / c2) + ADAM_EPS) + ADAM_WD * w_r[...])

    return _rowwise(body, [w, g, m, v], [_sds(w.shape)] * 3, name=name, rows=w.shape[0], tm=256)


SUM_ROW_TILE = 256


def _sum_cores(grads, theirs, half_index, *, name):
    _, R, C = grads.shape
    h = R // 2
    nb = h // SUM_ROW_TILE

    def body(c_ref, g_ref, t_ref, o_ref):
        o_ref[...] = (g_ref[...].astype(F32) + t_ref[...].astype(F32)).astype(o_ref.dtype)

    grid_spec = pltpu.PrefetchScalarGridSpec(
        num_scalar_prefetch=1, grid=(4, nb),
        in_specs=[pl.BlockSpec((1, SUM_ROW_TILE, C), lambda k, i, c_ref: (k, c_ref[0] * nb + i, 0)),
                  pl.BlockSpec((1, SUM_ROW_TILE, C), lambda k, i, c_ref: (k, i, 0))],
        out_specs=pl.BlockSpec((1, SUM_ROW_TILE, C), lambda k, i, c_ref: (k, i, 0)))
    return _pcall(body, name=name, dims=("parallel", "parallel"), grid_spec=grid_spec,
                  out_shape=_sds((4, h, C), grads.dtype))(half_index, grads, theirs)


def _sum_chips(parts, half_index, *, name):
    _, h, C = parts.shape
    nb = h // SUM_ROW_TILE

    def body(c_ref, p_ref, o_ref):
        p = [p_ref[k].astype(F32) for k in range(4)]
        o_ref[...] = ((p[0] + p[1]) + p[2]) + p[3]

    grid_spec = pltpu.PrefetchScalarGridSpec(
        num_scalar_prefetch=1, grid=(nb,),
        in_specs=[pl.BlockSpec((4, SUM_ROW_TILE, C), lambda i, c_ref: (0, i, 0))],
        out_specs=pl.BlockSpec((SUM_ROW_TILE, C), lambda i, c_ref: (c_ref[0] * nb + i, 0)))
    return _pcall(body, name=name, dims=("parallel",), grid_spec=grid_spec,
                  out_shape=_sds((2 * h, C)))(half_index, parts)


def _position():
    return lax.axis_index("x"), lax.axis_index("y"), lax.axis_index("c")


def _chip_peers(x, y):
    return [(1 - x, y), (x, 1 - y), (1 - x, 1 - y)]


_HBM = pl.BlockSpec(memory_space=pltpu.HBM)
LOCAL_COPY_CHUNKS = 8


def _local_copies(src_ref, dst_ref, sems):
    rows = src_ref.shape[0] // LOCAL_COPY_CHUNKS
    assert rows * LOCAL_COPY_CHUNKS == src_ref.shape[0]
    return [pltpu.make_async_copy(src_ref.at[pl.ds(i * rows, rows)], dst_ref.at[pl.ds(i * rows, rows)], sems.at[i])
            for i in range(LOCAL_COPY_CHUNKS)]


class _Exchange:
    def __init__(self, src, out_shape, sems, stages):
        self.src, self.out_shape, self.sems, self.stages = src, out_shape, sems, stages

    def run(self, refs, step, n_steps, at_end):
        for fraction, fn in self.stages:
            if (fraction == 1.0) == at_end:
                pl.when(step == int(round(fraction * (n_steps - 1))))(functools.partial(fn, *refs))


def _run_exchange(ex, *, name):
    def body(*refs):
        for _, fn in ex.stages:
            fn(*refs)

    return pl.pallas_call(
        body, name=name, in_specs=[_HBM], out_specs=_HBM, out_shape=ex.out_shape, scratch_shapes=list(ex.sems),
    )(ex.src)


def _gather_exchange(src):
    R, C = src.shape
    h = R // 2

    def plan(src_ref, out_ref, send_sems, recv_sems, local_sems):
        x, y, c = _position()
        me = 2 * x + y
        peers = _chip_peers(x, y)
        mine, other = pl.ds(c * h, h), pl.ds((1 - c) * h, h)

        def copy(sem, src_part, dst_part, device):
            return pltpu.make_async_remote_copy(
                src_ref=src_part, dst_ref=dst_part, send_sem=send_sems.at[sem], recv_sem=recv_sems.at[sem],
                device_id=device, device_id_type=MESH)

        landed = [out_ref.at[2 * px + py, mine] for px, py in peers]
        theirs = [out_ref.at[2 * px + py, other] for px, py in peers]
        return dict(
            sends=lambda: [copy(j, src_ref.at[mine], out_ref.at[me, mine], (px, py, c))
                           for j, (px, py) in enumerate(peers)],
            local=lambda: _local_copies(src_ref, out_ref.at[me], local_sems),
            arrivals=lambda: [copy(j, landed[j], landed[j], (px, py, c)) for j, (px, py) in enumerate(peers)],
            passed=lambda: [copy(3 + j, landed[j], landed[j], (x, y, 1 - c)) for j in range(3)],
            from_sibling=lambda: [copy(3 + j, theirs[j], theirs[j], (x, y, 1 - c)) for j in range(3)])

    def start(*refs):
        p = plan(*refs)
        for cp in p["sends"]() + p["local"]():
            cp.start()

    def pass_on(*refs):
        p = plan(*refs)
        for arrival, forward in zip(p["arrivals"](), p["passed"]()):
            arrival.wait_recv()
            forward.start()

    def finish(*refs):
        p = plan(*refs)
        for cp in p["from_sibling"]():
            cp.wait_recv()
        for cp in p["sends"]() + p["passed"]():
            cp.wait_send()
        for cp in p["local"]():
            cp.wait()

    sems = [pltpu.SemaphoreType.DMA((6,)), pltpu.SemaphoreType.DMA((6,)), pltpu.SemaphoreType.DMA((LOCAL_COPY_CHUNKS,))]
    return _Exchange(src, jax.ShapeDtypeStruct((4, R, C), src.dtype), sems, [(0.0, start), (0.6, pass_on), (1.0, finish)])


def _swap_other_half(src, *, name):
    _, R, C = src.shape
    h = R // 2

    def body(src_ref, out_ref, send_sem, recv_sem):
        x, y, c = _position()
        cp = pltpu.make_async_remote_copy(
            src_ref=src_ref.at[:, pl.ds((1 - c) * h, h)], dst_ref=out_ref, send_sem=send_sem, recv_sem=recv_sem,
            device_id=(x, y, 1 - c), device_id_type=MESH)
        cp.start()
        cp.wait()

    return pl.pallas_call(
        body, name=name, in_specs=[_HBM], out_specs=_HBM, out_shape=jax.ShapeDtypeStruct((4, h, C), src.dtype),
        scratch_shapes=[pltpu.SemaphoreType.DMA, pltpu.SemaphoreType.DMA],
    )(src)


def _scatter_exchange(src):
    def plan(src_ref, out_ref, send_sems, recv_sems, local_sems):
        x, y, c = _position()
        me = 2 * x + y
        peers = _chip_peers(x, y)

        def copy(j, src_block, dst_slot):
            px, py = peers[j]
            return pltpu.make_async_remote_copy(
                src_ref=src_ref.at[src_block], dst_ref=out_ref.at[dst_slot], send_sem=send_sems.at[j],
                recv_sem=recv_sems.at[j], device_id=(px, py, c), device_id_type=MESH)

        return dict(sends=lambda: [copy(j, 2 * px + py, me) for j, (px, py) in enumerate(peers)],
                    arrivals=lambda: [copy(j, me, 2 * px + py) for j, (px, py) in enumerate(peers)],
                    local=lambda: _local_copies(src_ref.at[me], out_ref.at[me], local_sems))

    def start(*refs):
        p = plan(*refs)
        for cp in p["sends"]() + p["local"]():
            cp.start()

    def finish(*refs):
        p = plan(*refs)
        for cp in p["arrivals"]():
            cp.wait_recv()
        for cp in p["sends"]():
            cp.wait_send()
        for cp in p["local"]():
            cp.wait()

    sems = [pltpu.SemaphoreType.DMA((3,)), pltpu.SemaphoreType.DMA((3,)), pltpu.SemaphoreType.DMA((LOCAL_COPY_CHUNKS,))]
    return _Exchange(src, jax.ShapeDtypeStruct(src.shape, src.dtype), sems, [(0.0, start), (1.0, finish)])


def _join_halves(src, *, name):
    R, C = src.shape
    h = R // 2

    def body(src_ref, out_ref, send_sem, recv_sem):
        x, y, c = _position()
        mine, theirs = pl.ds(c * h, h), pl.ds((1 - c) * h, h)
        cp = pltpu.make_async_remote_copy(
            src_ref=src_ref.at[mine], dst_ref=out_ref.at[mine], send_sem=send_sem, recv_sem=recv_sem,
            device_id=(x, y, 1 - c), device_id_type=MESH)
        cp.start()
        pltpu.make_async_remote_copy(
            src_ref=src_ref.at[theirs], dst_ref=out_ref.at[theirs], send_sem=send_sem, recv_sem=recv_sem,
            device_id=(x, y, 1 - c), device_id_type=MESH).wait_recv()
        cp.wait_send()

    return pl.pallas_call(
        body, name=name, in_specs=[_HBM], out_specs=_HBM, out_shape=jax.ShapeDtypeStruct((R, C), src.dtype),
        input_output_aliases={0: 0},
        scratch_shapes=[pltpu.SemaphoreType.DMA, pltpu.SemaphoreType.DMA],
    )(src)


def _allreduce_small(vec, *, name):
    R, C = vec.shape

    def body(v_ref, o_ref, slots, send_sems, recv_sems):
        x, y, c = _position()
        me = 4 * x + 2 * y + c

        def peer(k):
            return x ^ ((k >> 2) & 1), y ^ ((k >> 1) & 1), c ^ (k & 1)

        def copy(k, slot):
            return pltpu.make_async_remote_copy(
                src_ref=v_ref, dst_ref=slots.at[slot], send_sem=send_sems.at[k - 1], recv_sem=recv_sems.at[k - 1],
                device_id=peer(k), device_id_type=MESH)

        slots[me] = v_ref[...]
        sends = [copy(k, me) for k in range(1, 8)]
        for cp in sends:
            cp.start()
        for k in range(1, 8):
            px, py, pc = peer(k)
            copy(k, 4 * px + 2 * py + pc).wait_recv()
        total = slots[0]
        for d in range(1, 8):
            total = total + slots[d]
        o_ref[...] = total
        for cp in sends:
            cp.wait_send()

    vmem = pl.BlockSpec(memory_space=pltpu.VMEM)
    return pl.pallas_call(
        body, name=name, in_specs=[vmem], out_specs=vmem, out_shape=jax.ShapeDtypeStruct((R, C), vec.dtype),
        scratch_shapes=[pltpu.VMEM((8, R, C), vec.dtype), pltpu.SemaphoreType.DMA((7,)), pltpu.SemaphoreType.DMA((7,))],
    )(vec)


def _cross_cfg(S, mem_len):
    return _Attn(T=S, Tk=mem_len, G=1, nh=X_HEADS, rep=1, dqk=X_HEAD_DIM, dv=X_HEAD_DIM, tq=512, tk=mem_len,
                 mode="none", scale=X_HEAD_DIM ** -0.5, qcol=lambda g: 0, kcol=lambda g: 0, vcol=lambda g: 1,
                 ocol=lambda g: 0, o_width=X_HEADS * X_HEAD_DIM)


def _swa_cfg(S):
    return _Attn(T=S, Tk=S, G=1, nh=SWA_HEADS, rep=SWA_HEADS // SWA_KV_HEADS, dqk=HEAD_DIM, dv=HEAD_DIM, tq=BLOCK,
                 tk=BLOCK, mode="band", max_dist=SWA_WINDOW - 1, scale=HEAD_DIM ** -0.5, qcol=lambda g: 0,
                 kcol=lambda g: 0, vcol=lambda g: 0, ocol=lambda g: 0, o_width=SWA_HEADS * HEAD_DIM)


MLA_FWD_GROUP = 8
MLA_BWD_GROUP = 2


def _mla_cfg(S, group):
    t = _tile(S, 512)
    return _Attn(T=S, Tk=S, G=MLA_HEADS // group, nh=group, rep=1, dqk=LANES, dv=MLA_V, tq=t, tk=t, mode="causal",
                 scale=(MLA_NOPE + MLA_ROPE) ** -0.5, qcol=lambda g: g, kcol=lambda g: g, vcol=lambda g: g,
                 ocol=lambda g: g, o_width=MLA_HEADS * MLA_V)


def _dil_cfg(S, window, dil):
    return _Attn(T=S // dil, Tk=S // dil, G=dil, nh=DIL_HEADS, rep=1, dqk=HEAD_DIM, dv=HEAD_DIM, tq=BLOCK, tk=BLOCK,
                 mode="band", max_dist=window // dil, scale=HEAD_DIM ** -0.5, qcol=lambda g: g, kcol=lambda g: g,
                 vcol=lambda g: g, ocol=lambda g: g, o_width=dil * DIL_HEADS * HEAD_DIM)


def _cross_fwd(p, x, mem, W, vec):
    S = x.shape[0]
    cfg = _cross_cfg(S, mem.shape[0])
    hx = _rmsnorm(x, vec[p + "x_norm"], name=p + "x_norm")
    qx = _mm(hx, W[p + "w_xq"], mode="nn", name=p + "xq", out_dtype=MXU_DTYPE)
    memn = _rmsnorm(mem, vec[p + "mem_norm"], name=p + "mem_norm")
    kvx = _mm(memn, W[p + "w_xkv"], mode="nn", name=p + "xkv", out_dtype=MXU_DTYPE)
    ox, lse = _attn_fwd(cfg, qx, kvx, kvx, name=p + "x_attn", out_dtype=MXU_DTYPE)
    out = _mm(ox, W[p + "w_xo"], mode="nn", name=p + "xo", res=x)
    return out, (x, hx, qx, memn, kvx, ox, lse)


def _cross_bwd(p, dx, saved, mem, W, vec, dW, dvec):
    x, hx, qx, memn, kvx, ox, lse = saved
    cfg = _cross_cfg(x.shape[0], mem.shape[0])
    dox = _mm(dx, W[p + "w_xo"], mode="nt", name=p + "xo_dx", out_dtype=MXU_DTYPE)
    dW[p + "w_xo"] = _dw(ox, dx, name=p + "xo_dw")
    delta, _ = _attn_delta(cfg, ox, dox, name=p + "x_delta")
    dqx = _attn_dq(cfg, qx, kvx, kvx, dox, lse, delta, name=p + "x_dq", out_dtype=MXU_DTYPE)
    dkx, dvx = _attn_dkv(cfg, qx, kvx, kvx, dox, lse, delta, name=p + "x_dkv", out_dtype=MXU_DTYPE)
    dkvx = jnp.concatenate([dkx, dvx], axis=1)
    dhx = _mm(dqx, W[p + "w_xq"], mode="nt", name=p + "xq_dx")
    dW[p + "w_xq"] = _dw(hx, dqx, name=p + "xq_dw")
    dW[p + "w_xkv"] = _dw(memn, dkvx, name=p + "xkv_dw")
    dmemn = _mm(dkvx, W[p + "w_xkv"], mode="nt", name=p + "xkv_dx")
    _, dvec[p + "mem_norm"] = _rmsnorm_bwd(mem, vec[p + "mem_norm"], dmemn, name=p + "mem_norm_bwd")
    dx_in, dvec[p + "x_norm"] = _rmsnorm_bwd(x, vec[p + "x_norm"], dhx, name=p + "x_norm_bwd", dres=dx)
    return dx_in


def _ffn_fwd(p, x, W, vec):
    hf = _rmsnorm(x, vec[p + "ffn_norm"], name=p + "ffn_norm")
    gate, up, act = _gate_up(hf, W[p + "w_gate"], W[p + "w_up"], name=p + "gate_up")
    out = _mm(act, W[p + "w_down"], mode="nn", name=p + "down", res=x)
    return out, (x, hf, gate, up, act)


def _ffn_bwd(p, dx, saved, W, vec, dW, dvec):
    x, hf, gate, up, act = saved
    dW[p + "w_down"] = _dw(act, dx, name=p + "down_dw")
    dgate, dup = _gate_up_bwd(dx, W[p + "w_down"], gate, up, name=p + "gate_up_bwd")
    dhf = _mm(dgate, W[p + "w_gate"], mode="nt", name=p + "gate_dx")
    dhf = _mm(dup, W[p + "w_up"], mode="nt", name=p + "up_dx", res=dhf)
    dW[p + "w_gate"] = _dw(hf, dgate, name=p + "gate_dw")
    dW[p + "w_up"] = _dw(hf, dup, name=p + "up_dw")
    dx_in, dvec[p + "ffn_norm"] = _rmsnorm_bwd(x, vec[p + "ffn_norm"], dhf, name=p + "ffn_norm_bwd", dres=dx)
    return dx_in


def _even_fwd(p, x, tabs, W, vec, comm=None):
    S = x.shape[0]
    h = _rmsnorm(x, vec[p + "mix_norm"], name=p + "mix_norm")
    z = _mm(h, W[p + "w_in"], mode="nn", name=p + "in")
    qa, ka, va, cqn, ckvn, kr = _l0_prep(z, tabs, vec[p + "q_norm"], vec[p + "kv_norm"], name=p + "prep")
    sink = jnp.pad(vec[p + "sinks"], (0, LANES - SWA_HEADS)).reshape(1, LANES)
    oa, lse_a = _band_fwd(_swa_cfg(S), qa, ka, va, name=p + "swa", sink=sink, out_dtype=MXU_DTYPE)
    qb = _mm(cqn, W[p + "w_uq"], mode="nn", name=p + "uq")
    kvb = _mm(ckvn, W[p + "w_ukv"], mode="nn", name=p + "ukv")
    Q, K, V = _mla_prep(qb, kvb, kr, tabs, name=p + "mla_prep")
    if comm is None:
        ob, lse_b = _causal_fwd(_mla_cfg(S, MLA_FWD_GROUP), Q, K, V, name=p + "mla", out_dtype=MXU_DTYPE, stat_heads=MLA_BWD_GROUP)
    else:
        ob, lse_b, gathered = _causal_fwd(_mla_cfg(S, MLA_FWD_GROUP), Q, K, V, name=p + "mla", out_dtype=MXU_DTYPE, stat_heads=MLA_BWD_GROUP,
                                          carry=comm.late_weights_exchange())
        W = {**W, **comm.late_weights(gathered)}
    o = jnp.concatenate([oa, ob], axis=1)
    out = _mm(o, W[p + "w_out"], mode="nn", name=p + "out", res=x)
    return out, (x, h, z, qa, ka, va, cqn, ckvn, sink, oa, lse_a, Q, K, V, ob, lse_b, o), W


def _even_bwd(p, dx, saved, tabs, W, vec, dW, dvec, comm=None):
    x, h, z, qa, ka, va, cqn, ckvn, sink, oa, lse_a, Q, K, V, ob, lse_b, o = saved
    S = x.shape[0]
    do = _mm(dx, W[p + "w_out"], mode="nt", name=p + "out_dx", out_dtype=MXU_DTYPE)
    dW[p + "w_out"] = _dw(o, dx, name=p + "out_dw")
    doa, dob = do[:, :SWA_HEADS * HEAD_DIM], do[:, SWA_HEADS * HEAD_DIM:]
    cfg = _swa_cfg(S)
    delta, dsink = _attn_delta(cfg, oa, doa, name=p + "swa_delta", lse=lse_a, sink=sink)
    dvec[p + "sinks"] = dsink
    dqa, dka, dva = _band_bwd(cfg, qa, ka, va, doa, lse_a, delta, name=p + "swa_bwd")
    cfg = _mla_cfg(S, MLA_BWD_GROUP)
    if comm is None:
        dQ, dK, dV = _causal_bwd(cfg, Q, K, V, ob, dob, lse_b, name=p + "mla_bwd")
    else:
        dQ, dK, dV, landed = _causal_bwd(cfg, Q, K, V, ob, dob, lse_b, name=p + "mla_bwd",
                                         carry=comm.late_grads_exchange(dW))
        comm.late_grads_landed(landed)
    dqb, dkvb, dkr = _mla_prep_bwd(dQ, dK, dV, tabs, name=p + "mla_prep_bwd")
    dcqn = _mm(dqb, W[p + "w_uq"], mode="nt", name=p + "uq_dx")
    dW[p + "w_uq"] = _dw(cqn, dqb, name=p + "uq_dw")
    dckvn = _mm(dkvb, W[p + "w_ukv"], mode="nt", name=p + "ukv_dx")
    dW[p + "w_ukv"] = _dw(ckvn, dkvb, name=p + "ukv_dw")
    dz, dvec[p + "q_norm"], dvec[p + "kv_norm"] = _l0_prep_bwd(
        z, tabs, vec[p + "q_norm"], vec[p + "kv_norm"], dqa, dka, dva, dcqn, dckvn, dkr, name=p + "prep_bwd")
    dh = _mm(dz, W[p + "w_in"], mode="nt", name=p + "in_dx")
    dW[p + "w_in"] = _dw(h, dz, name=p + "in_dw")
    dx_in, dvec[p + "mix_norm"] = _rmsnorm_bwd(x, vec[p + "mix_norm"], dh, name=p + "mix_norm_bwd", dres=dx)
    return dx_in


def _odd_fwd(p, x, tabs, W, vec):
    S = x.shape[0]
    assert S % (DIL_PATTERNS[-1][1] * BLOCK) == 0, "keys past the end of the sequence are never attended"
    h = _rmsnorm(x, vec[p + "mix_norm"], name=p + "mix_norm")
    qkv = _mm(h, W[p + "w_qkv"], mode="nn", name=p + "qkv")
    qkv_by_d = _l1_prep(qkv, tabs, name=p + "prep")
    outs, lses = {}, {}
    for window, dil in DIL_PATTERNS:
        outs[dil], lses[dil] = _band_fwd(_dil_cfg(S, window, dil), *qkv_by_d[dil], name=p + "dil%d" % dil)
    o, w1, w4, w16 = _merge(outs, lses, name=p + "merge")
    out = _mm(o, W[p + "w_out"], mode="nn", name=p + "out", res=x)
    return out, (x, h, qkv_by_d, lses, dict(zip(DILATIONS, (w1, w4, w16))), o)


def _odd_bwd(p, dx, saved, tabs, W, vec, dW, dvec):
    x, h, qkv_by_d, lses, ws, o = saved
    S = x.shape[0]
    do = _mm(dx, W[p + "w_out"], mode="nt", name=p + "out_dx")
    dW[p + "w_out"] = _dw(o, dx, name=p + "out_dw")
    dos, deltas = _merge_bwd(do, o, ws, name=p + "merge_bwd")
    grads = {}
    for window, dil in DIL_PATTERNS:
        grads[dil] = _band_bwd(_dil_cfg(S, window, dil), *qkv_by_d[dil], dos[dil], lses[dil], deltas[dil],
                               name=p + "dil%d_bwd" % dil)
    dqkv = _l1_prep_bwd(grads, tabs, name=p + "prep_bwd")
    dh = _mm(dqkv, W[p + "w_qkv"], mode="nt", name=p + "qkv_dx")
    dW[p + "w_qkv"] = _dw(h, dqkv, name=p + "qkv_dw")
    dx_in, dvec[p + "mix_norm"] = _rmsnorm_bwd(x, vec[p + "mix_norm"], dh, name=p + "mix_norm_bwd", dres=dx)
    return dx_in


def _local_step(x, mem, positions, target, W, vec, comm=None):
    tabs = _rope_tables(positions)
    x1, s_mix0, W = _even_fwd("l0_", x, tabs, W, vec, comm)
    x2, s_x0 = _cross_fwd("l0_", x1, mem, W, vec)
    x3, s_f0 = _ffn_fwd("l0_", x2, W, vec)
    x4, s_mix1 = _odd_fwd("l1_", x3, tabs, W, vec)
    x5, s_x1 = _cross_fwd("l1_", x4, mem, W, vec)
    x6, s_f1 = _ffn_fwd("l1_", x5, W, vec)
    dW, dvec = {}, {}
    dx, dvec["final_norm"], sq = _loss_head(x6, vec["final_norm"], target, name="loss_head")
    dx = _ffn_bwd("l1_", dx, s_f1, W, vec, dW, dvec)
    dx = _cross_bwd("l1_", dx, s_x1, mem, W, vec, dW, dvec)
    dx = _odd_bwd("l1_", dx, s_mix1, tabs, W, vec, dW, dvec)
    dx = _ffn_bwd("l0_", dx, s_f0, W, vec, dW, dvec)
    dx = _cross_bwd("l0_", dx, s_x0, mem, W, vec, dW, dvec)
    dx = _even_bwd("l0_", dx, s_mix0, tabs, W, vec, dW, dvec, comm)
    return sq, dx, dW, dvec


_LAYER_MATS = {
    0: [("w_in", "col"), ("w_uq", "col"), ("w_ukv", "col"), ("w_out", "row"), ("w_xq", "row"), ("w_xkv", "row"),
        ("w_xo", "col"), ("w_gate", "col"), ("w_up", "col"), ("w_down", "row")],
    1: [("w_qkv", "col"), ("w_out", "row"), ("w_xq", "row"), ("w_xkv", "row"), ("w_xo", "col"), ("w_gate", "col"),
        ("w_up", "col"), ("w_down", "row")],
}
MATS = [("l%d_%s" % (l, n), kind) for l in (0, 1) for n, kind in _LAYER_MATS[l]]
_LAYER_VECS = {0: ["mix_norm", "sinks", "q_norm", "kv_norm", "x_norm", "mem_norm", "ffn_norm"],
               1: ["mix_norm", "x_norm", "mem_norm", "ffn_norm"]}
VECS = ["l%d_%s" % (l, n) for l in (0, 1) for n in _LAYER_VECS[l]] + ["final_norm"]
WEIGHT_ORDER = (["l0_mix_norm", "l0_w_in", "l0_sinks", "l0_q_norm", "l0_w_uq", "l0_kv_norm", "l0_w_ukv", "l0_w_out",
                 "l0_x_norm", "l0_mem_norm", "l0_w_xq", "l0_w_xkv", "l0_w_xo", "l0_ffn_norm", "l0_w_gate", "l0_w_up",
                 "l0_w_down", "l1_mix_norm", "l1_w_qkv", "l1_w_out", "l1_x_norm", "l1_mem_norm", "l1_w_xq",
                 "l1_w_xkv", "l1_w_xo", "l1_ffn_norm", "l1_w_gate", "l1_w_up", "l1_w_down", "final_norm"])
PACK_COLS = 1024
PACK_ROW_TILE = 2 * SUM_ROW_TILE
VEC_ROWS = 16
LOSS_ROW = len(VECS)
N_CHIPS = 4


class _Group:
    def __init__(self, mats, shards):
        self.mats, self.shards = mats, shards
        self.layout, off = {}, 0
        for name, _ in mats:
            n = shards[name].size // PACK_COLS
            assert n * PACK_COLS == shards[name].size
            self.layout[name] = (off, n)
            off += n
        self.used = off
        self.rows = -(-off // PACK_ROW_TILE) * PACK_ROW_TILE

    def pack(self, tensors, dtype):
        parts = [tensors[name].astype(dtype).reshape(-1, PACK_COLS) for name, _ in self.mats]
        return jnp.concatenate(parts + [jnp.zeros((self.rows - self.used, PACK_COLS), dtype)], axis=0)

    def unpack(self, packed):
        return {name: packed[off:off + n].reshape(self.shards[name].shape) for name, (off, n) in self.layout.items()}

    def full_weights(self, gathered):
        W = {}
        for name, kind in self.mats:
            off, n = self.layout[name]
            r, cw = self.shards[name].shape
            blocks = gathered[:, off:off + n].reshape(N_CHIPS, r, cw)
            W[name] = blocks.reshape(N_CHIPS * r, cw) if kind == "row" else (
                jnp.transpose(blocks, (1, 0, 2)).reshape(r, N_CHIPS * cw))
        if "l0_w_in" in W:
            W["l0_w_in"] = jnp.pad(W["l0_w_in"], ((0, 0), (0, Z_END - W["l0_w_in"].shape[1])))
        if "l0_w_uq" in W:
            uq = W["l0_w_uq"].reshape(MLA_Q_RANK, MLA_HEADS, MLA_NOPE + MLA_ROPE)
            uq = jnp.pad(uq, ((0, 0), (0, 0), (0, LANES - MLA_NOPE - MLA_ROPE)))
            W["l0_w_uq"] = uq.reshape(MLA_Q_RANK, MLA_HEADS * LANES)
        return W

    def pack_grads(self, dW):
        parts = []
        for name, kind in self.mats:
            r, cw = self.shards[name].shape
            g = dW[name]
            if name == "l0_w_in":
                g = g[:, :Z_KR + MLA_ROPE]
            if name == "l0_w_uq":
                g = g.reshape(MLA_Q_RANK, MLA_HEADS, LANES)[:, :, :MLA_NOPE + MLA_ROPE].reshape(MLA_Q_RANK, -1)
            if kind == "col":
                g = jnp.transpose(g.reshape(r, N_CHIPS, cw), (1, 0, 2))
            parts.append(g.reshape(N_CHIPS, -1, PACK_COLS).astype(EXCHANGE_DTYPE))
        pad = jnp.zeros((N_CHIPS, self.rows - self.used, PACK_COLS), EXCHANGE_DTYPE)
        return jnp.concatenate(parts + [pad], axis=1)


def _pack_vecs(vecs):
    rows = [jnp.pad(vecs[n].reshape(-1).astype(F32), (0, PACK_COLS - vecs[n].size)) for n in VECS]
    rows += [jnp.zeros((PACK_COLS,), F32)] * (VEC_ROWS - len(rows))
    return jnp.stack(rows)


def _unpack_vecs(packed, like):
    return {n: packed[i, :like[n].size].reshape(like[n].shape) for i, n in enumerate(VECS)}


EARLY_MATS = [m for m in MATS if m[0] in ("l0_w_in", "l0_w_uq", "l0_w_ukv")]
LATE_MATS = [m for m in MATS if m not in EARLY_MATS]


class _StepComm:
    def __init__(self, shards):
        self.early, self.late = _Group(EARLY_MATS, shards), _Group(LATE_MATS, shards)
        self.half_index = lax.axis_index("c").astype(jnp.int32).reshape(1)
        self.late_grads = None

    def early_weights(self):
        src = self.early.pack(self.early.shards, MXU_DTYPE)
        return self.early.full_weights(_run_exchange(_gather_exchange(src), name="gather_early"))

    def late_weights_exchange(self):
        return _gather_exchange(self.late.pack(self.late.shards, MXU_DTYPE))

    def late_weights(self, gathered):
        return self.late.full_weights(gathered)

    def _chip_sum(self, group, dW, tag):
        grads = group.pack_grads(dW)
        theirs = _swap_other_half(grads, name="swap_other_half_" + tag)
        return _sum_cores(grads, theirs, self.half_index, name="sum_cores_" + tag)

    def _finish(self, parts, tag):
        return _join_halves(_sum_chips(parts, self.half_index, name="sum_chips_" + tag), name="join_halves_" + tag)

    def late_grads_exchange(self, dW):
        return _scatter_exchange(self._chip_sum(self.late, dW, "late"))

    def late_grads_landed(self, parts):
        self.late_grads = self._finish(parts, "late")

    def early_grads(self, dW):
        parts = _run_exchange(_scatter_exchange(self._chip_sum(self.early, dW, "early")), name="scatter_early")
        return self._finish(parts, "early")


def _step(a):
    weights = {n: a[n] for n in WEIGHT_ORDER}
    shards = {n: weights[n] for n, _ in MATS}
    vec = {n: weights[n] for n in VECS}
    comm = _StepComm(shards)
    sq, grad_x, dW, dvec = _local_step(a["x"][0], a["mem"][0], a["positions"], a["loss_target"][0],
                                       comm.early_weights(), vec, comm)

    dvec = dict(dvec)
    dvec["l0_sinks"] = dvec["l0_sinks"][0, :SWA_HEADS]
    small = _pack_vecs(dvec)
    small = small.at[LOSS_ROW, 0].set(0.5 / a["x"].shape[-1] * jnp.sum(sq))
    small = _allreduce_small(small, name="reduce_gains")
    loss = small[LOSS_ROW, 0]
    g_s = small.at[LOSS_ROW, 0].set(0.0)
    d_s, m_s, v_s = _adamw(_pack_vecs(vec), g_s, _pack_vecs({n: a["m_" + n] for n in VECS}),
                           _pack_vecs({n: a["v_" + n] for n in VECS}), name="adamw_gains")
    got = [_unpack_vecs(packed, vec) for packed in (g_s, d_s, m_s, v_s)]

    for group, g_w in ((comm.late, comm.late_grads), (comm.early, comm.early_grads(dW))):
        for n, g in group.unpack(g_w).items():
            results = (g,) + tuple(_adamw(shards[n], g, a["m_" + n], a["v_" + n], name="adamw_" + n))
            for kind, value in zip(got, results):
                kind[n] = value

    out = [loss, grad_x[None]]
    for kind in got:
        out += [kind[n] for n in WEIGHT_ORDER]
    return tuple(out)


def kernel(x, mem, positions, l0_mix_norm, l0_w_in, l0_sinks, l0_q_norm, l0_w_uq, l0_kv_norm, l0_w_ukv, l0_w_out, l0_x_norm, l0_mem_norm, l0_w_xq, l0_w_xkv, l0_w_xo, l0_ffn_norm, l0_w_gate, l0_w_up, l0_w_down, l1_mix_norm, l1_w_qkv, l1_w_out, l1_x_norm, l1_mem_norm, l1_w_xq, l1_w_xkv, l1_w_xo, l1_ffn_norm, l1_w_gate, l1_w_up, l1_w_down, final_norm, loss_target, m_l0_mix_norm, m_l0_w_in, m_l0_sinks, m_l0_q_norm, m_l0_w_uq, m_l0_kv_norm, m_l0_w_ukv, m_l0_w_out, m_l0_x_norm, m_l0_mem_norm, m_l0_w_xq, m_l0_w_xkv, m_l0_w_xo, m_l0_ffn_norm, m_l0_w_gate, m_l0_w_up, m_l0_w_down, m_l1_mix_norm, m_l1_w_qkv, m_l1_w_out, m_l1_x_norm, m_l1_mem_norm, m_l1_w_xq, m_l1_w_xkv, m_l1_w_xo, m_l1_ffn_norm, m_l1_w_gate, m_l1_w_up, m_l1_w_down, m_final_norm, v_l0_mix_norm, v_l0_w_in, v_l0_sinks, v_l0_q_norm, v_l0_w_uq, v_l0_kv_norm, v_l0_w_ukv, v_l0_w_out, v_l0_x_norm, v_l0_mem_norm, v_l0_w_xq, v_l0_w_xkv, v_l0_w_xo, v_l0_ffn_norm, v_l0_w_gate, v_l0_w_up, v_l0_w_down, v_l1_mix_norm, v_l1_w_qkv, v_l1_w_out, v_l1_x_norm, v_l1_mem_norm, v_l1_w_xq, v_l1_w_xkv, v_l1_w_xo, v_l1_ffn_norm, v_l1_w_gate, v_l1_w_up, v_l1_w_down, v_final_norm):
    return _step(dict(locals()))
```

```python
import functools

import jax
import jax.numpy as jnp
import numpy as np
from jax import lax
from jax.experimental import pallas as pl
from jax.experimental.pallas import tpu as pltpu

F32 = jnp.float32
MXU_DTYPE = jnp.bfloat16
LANES = 128
VMEM_LIMIT_BYTES = 56 * 1024 * 1024

NORM_EPS = 1e-6
ROPE_THETA = 10000.0
BLOCK = 128
HEAD_DIM = 64
SWA_HEADS, SWA_KV_HEADS, SWA_WINDOW = 8, 2, 128
MLA_HEADS, MLA_Q_RANK, MLA_KV_RANK, MLA_NOPE, MLA_ROPE, MLA_V = 8, 384, 256, 64, 32, 64
DIL_HEADS = 16
DIL_PATTERNS = ((128, 1), (512, 4), (2048, 16))
X_HEADS, X_HEAD_DIM = 4, 128
ADAM_LR, ADAM_B1, ADAM_B2, ADAM_EPS, ADAM_WD, ADAM_STEP = 0.001, 0.9, 0.999, 1e-08, 0.01, 10
MESH = pl.DeviceIdType.MESH
NEG_BIG = -1e30

NN = (((1,), (0,)), ((), ()))
NT = (((1,), (1,)), ((), ()))


def _dot(a, b, dims=NN):
    return lax.dot_general(a.astype(MXU_DTYPE), b.astype(MXU_DTYPE), dims, preferred_element_type=F32)


def _pcall(body, *, name, dims=None, **kw):
    params = pltpu.CompilerParams(dimension_semantics=dims, vmem_limit_bytes=VMEM_LIMIT_BYTES)
    return pl.pallas_call(body, name=name, compiler_params=params, **kw)


def _tile(n, pref):
    t = (min(pref, n) // LANES) * LANES
    while t >= LANES:
        if n % t == 0:
            return t
        t -= LANES
    return n


SUBLANES_PACKED = 16


def _row_tile(n, pref):
    t = (min(pref, n) // SUBLANES_PACKED) * SUBLANES_PACKED
    while t >= SUBLANES_PACKED:
        if n % t == 0:
            return t
        t -= SUBLANES_PACKED
    return n


def _lane(shape):
    return lax.broadcasted_iota(jnp.int32, shape, 1)


def _cols_to_lanes(cols, rows):
    lane = _lane((rows, LANES))
    out = jnp.zeros((rows, LANES), F32)
    for j, col in enumerate(cols):
        out = jnp.where(lane == j, col, out)
    return out


def _mm(a, b, *, mode, name, res=None, out_dtype=F32, tm=1408, tn=1536, tk=1408):
    if mode == "nn":
        (M, K), (K2, N) = a.shape, b.shape
    elif mode == "nt":
        (M, K), (N, K2) = a.shape, b.shape
    else:
        (K, M), (K2, N) = a.shape, b.shape
    assert K == K2, (a.shape, b.shape, mode)
    tm, tn, tk = _tile(M, tm), _tile(N, tn), _tile(K, tk)
    nk = K // tk
    in_place = out_dtype == F32 or nk == 1

    def body(*refs):
        refs = list(refs)
        a_ref, b_ref = refs[:2]
        r_ref = refs[2] if res is not None else None
        o_ref = refs[3 if res is not None else 2]
        acc = o_ref if in_place else refs[-1]
        k = pl.program_id(2)
        if mode == "nn":
            part = _dot(a_ref[...], b_ref[...], NN)
        elif mode == "nt":
            part = _dot(a_ref[...], b_ref[...], NT)
        else:
            part = _dot(a_ref[...].T, b_ref[...], NN)
        if nk == 1:
            o_ref[...] = (part if res is None else part + r_ref[...].astype(F32)).astype(o_ref.dtype)
            return

        @pl.when(k == 0)
        def _():
            acc[...] = part if res is None else part + r_ref[...].astype(F32)

        @pl.when(k > 0)
        def _():
            acc[...] += part

        if not in_place:
            @pl.when(k == nk - 1)
            def _():
                o_ref[...] = acc[...].astype(o_ref.dtype)

    if mode == "nn":
        a_spec = pl.BlockSpec((tm, tk), lambda i, j, k: (i, k))
        b_spec = pl.BlockSpec((tk, tn), lambda i, j, k: (k, j))
    elif mode == "nt":
        a_spec = pl.BlockSpec((tm, tk), lambda i, j, k: (i, k))
        b_spec = pl.BlockSpec((tn, tk), lambda i, j, k: (j, k))
    else:
        a_spec = pl.BlockSpec((tk, tm), lambda i, j, k: (k, i))
        b_spec = pl.BlockSpec((tk, tn), lambda i, j, k: (k, j))
    o_spec = pl.BlockSpec((tm, tn), lambda i, j, k: (i, j))
    in_specs = [a_spec, b_spec] + ([] if res is None else [o_spec])
    args = (a, b) + (() if res is None else (res,))
    return _pcall(
        body, name=name, dims=("parallel", "parallel", "arbitrary"),
        grid=(M // tm, N // tn, nk), in_specs=in_specs, out_specs=o_spec,
        out_shape=jax.ShapeDtypeStruct((M, N), out_dtype),
        scratch_shapes=[] if in_place else [pltpu.VMEM((tm, tn), F32)],
    )(*args)


EXCHANGE_DTYPE = jnp.bfloat16


def _dw(a, b, *, name):
    return _mm(a, b, mode="tn", name=name, out_dtype=EXCHANGE_DTYPE)


def _rms_parts(xf):
    r = lax.rsqrt(jnp.mean(xf * xf, axis=-1, keepdims=True) + NORM_EPS)
    return xf * r, r


def _rms_bwd_rows(xf, g, dy):
    xhat, r = _rms_parts(xf)
    dxhat = dy * g
    dx = r * (dxhat - xhat * jnp.mean(dxhat * xhat, axis=-1, keepdims=True))
    return dx, dy * xhat


def _dx_norm_bwd(a, w, x, g, dres, *, name, res=None, tm=512, tk=1408):
    (M, K), N = a.shape, w.shape[0]
    tm, tk = _tile(M, tm), _tile(K, tk)
    nk = K // tk

    def body(*refs):
        refs = list(refs)
        a_ref, w_ref, x_ref, g_ref, dr_ref = refs[:5]
        r_ref = refs[5] if res is not None else None
        dx_ref, dg_ref = refs[-2:]
        i, k = pl.program_id(0), pl.program_id(1)
        part = _dot(a_ref[...], w_ref[...], NT)

        @pl.when(k == 0)
        def _():
            dx_ref[...] = part if res is None else part + r_ref[...]

        @pl.when(k > 0)
        def _():
            dx_ref[...] += part

        @pl.when(k == nk - 1)
        def _():
            dx, dgp = _rms_bwd_rows(x_ref[...], g_ref[...], dx_ref[...])
            dx_ref[...] = dx + dr_ref[...]

            @pl.when(i == 0)
            def _():
                dg_ref[...] = jnp.zeros_like(dg_ref)

            dg_ref[...] += jnp.sum(dgp, axis=0, keepdims=True)

    row = pl.BlockSpec((tm, N), lambda i, k: (i, 0))
    vec = pl.BlockSpec((1, N), lambda i, k: (0, 0))
    in_specs = [pl.BlockSpec((tm, tk), lambda i, k: (i, k)), pl.BlockSpec((N, tk), lambda i, k: (0, k)), row, vec, row]
    args = [a, w, x, g.reshape(1, N), dres]
    if res is not None:
        in_specs.append(row)
        args.append(res)
    return _pcall(
        body, name=name, dims=("arbitrary", "arbitrary"), grid=(M // tm, nk), in_specs=in_specs,
        out_specs=[row, vec], out_shape=[_sds((M, N)), _sds((1, N))],
    )(*args)


def _rmsnorm(x, g, *, name, out_dtype=MXU_DTYPE, tm=512):
    M, D = x.shape
    tm = _tile(M, tm)

    def body(x_ref, g_ref, o_ref):
        xhat, _ = _rms_parts(x_ref[...].astype(F32))
        o_ref[...] = (xhat * g_ref[...]).astype(o_ref.dtype)

    return _pcall(
        body, name=name, dims=("parallel",), grid=(M // tm,),
        in_specs=[pl.BlockSpec((tm, D), lambda i: (i, 0)), pl.BlockSpec((1, D), lambda i: (0, 0))],
        out_specs=pl.BlockSpec((tm, D), lambda i: (i, 0)),
        out_shape=jax.ShapeDtypeStruct((M, D), out_dtype),
    )(x, g.reshape(1, D))


def _rmsnorm_bwd(x, g, dy, *, name, dres=None, tm=512):
    M, D = x.shape
    tm = _tile(M, tm)

    def body(*refs):
        if dres is None:
            x_ref, g_ref, dy_ref, dx_ref, dg_ref = refs
        else:
            x_ref, g_ref, dy_ref, dr_ref, dx_ref, dg_ref = refs
        dx, dgp = _rms_bwd_rows(x_ref[...].astype(F32), g_ref[...], dy_ref[...].astype(F32))
        if dres is not None:
            dx = dx + dr_ref[...]
        dx_ref[...] = dx

        @pl.when(pl.program_id(0) == 0)
        def _():
            dg_ref[...] = jnp.zeros_like(dg_ref)

        dg_ref[...] += jnp.sum(dgp, axis=0, keepdims=True)

    row = pl.BlockSpec((tm, D), lambda i: (i, 0))
    vec = pl.BlockSpec((1, D), lambda i: (0, 0))
    in_specs = [row, vec, row] + ([] if dres is None else [row])
    args = (x, g.reshape(1, D), dy) + (() if dres is None else (dres,))
    return _pcall(
        body, name=name, dims=("arbitrary",), grid=(M // tm,), in_specs=in_specs, out_specs=[row, vec],
        out_shape=[jax.ShapeDtypeStruct((M, D), F32), jax.ShapeDtypeStruct((1, D), F32)],
    )(*args)


def _rope_chunk(t, c, s, half):
    lane = _lane(t.shape)
    swapped = jnp.where((lane % (2 * half)) < half, pltpu.roll(t, LANES - half, 1), pltpu.roll(t, half, 1))
    return t * c + swapped * s


def _rope_tables(positions):
    pos = positions.reshape(-1).astype(F32)[:, None]

    def table(dh, first, copies, sine, fill=0.0):
        half = dh // 2
        lane = np.arange(LANES)
        inside = (lane >= first) & (lane < first + copies * dh)
        idx = np.where(inside, (lane - first) % half, 0)
        inv_freq = ROPE_THETA ** (-jnp.asarray(2 * idx, F32) / dh)
        sign = np.where((lane - first) % dh < half, -1.0, 1.0) if sine else np.ones(LANES)
        ang = pos * inv_freq[None, :]
        val = (jnp.sin(ang) if sine else jnp.cos(ang)) * jnp.asarray(sign, F32)[None, :]
        return jnp.where(jnp.asarray(inside)[None, :], val, fill)

    return dict(
        c64=table(HEAD_DIM, 0, 2, False), s64=table(HEAD_DIM, 0, 2, True),
        ck=table(MLA_ROPE, 0, 1, False), sk=table(MLA_ROPE, 0, 1, True),
        cm=jnp.where(jnp.asarray(np.arange(LANES) < MLA_NOPE)[None, :], 1.0, table(MLA_ROPE, MLA_NOPE, 1, False)),
        sm=table(MLA_ROPE, MLA_NOPE, 1, True),
    )


def _attn_steps(mode, n_other, t_self, t_other):
    if mode == "band":
        assert t_self == t_other
        return 2
    return n_other


def _kv_block(mode, qi, kj):
    if mode == "band":
        return jnp.maximum(qi - 1 + kj, 0), (qi + kj) >= 1
    if mode == "causal":
        return jnp.minimum(kj, qi), kj <= qi
    return kj, None


def _q_block(mode, ki, qj, nq):
    if mode == "band":
        return jnp.minimum(ki + qj, nq - 1), (ki + qj) <= nq - 1
    if mode == "causal":
        return jnp.maximum(qj, ki), qj >= ki
    return qj, None


def _mask(mode, max_dist, qpos, kpos):
    d = qpos - kpos
    if mode == "band":
        return (d >= 0) & (d <= max_dist)
    if mode == "causal":
        return d >= 0
    return None


def _when(cond, fn):
    if cond is None:
        fn()
    else:
        pl.when(cond)(fn)


class _Attn:
    def __init__(self, *, T, Tk, G, nh, rep, dqk, dv, tq, tk, mode, scale, qcol, kcol, vcol, ocol, o_width,
                 max_dist=0):
        self.__dict__.update(locals())
        self.nkv = nh // rep
        assert T % tq == 0 and Tk % tk == 0 and nh <= LANES


def _attn_fwd(cfg, q, k, v, *, name, sink=None, out_dtype=F32):
    c = cfg
    nq, nk = c.T // c.tq, c.Tk // c.tk
    steps = _attn_steps(c.mode, nk, c.tq, c.tk)

    def body(*refs):
        if sink is None:
            q_ref, k_ref, v_ref, o_ref, lse_ref, m_scr, l_scr, acc = refs
        else:
            q_ref, k_ref, v_ref, sink_ref, o_ref, lse_ref, m_scr, l_scr, acc = refs
        qi, kj = pl.program_id(1), pl.program_id(2)
        kb, valid = _kv_block(c.mode, qi, kj)

        @pl.when(kj == 0)
        def _():
            if sink is None:
                m_scr[...] = jnp.full_like(m_scr, NEG_BIG)
                l_scr[...] = jnp.zeros_like(l_scr)
            else:
                m_scr[...] = jnp.broadcast_to(sink_ref[...], m_scr.shape)
                l_scr[...] = jnp.ones_like(l_scr)
            acc[...] = jnp.zeros_like(acc)

        def step():
            qpos = qi * c.tq + lax.broadcasted_iota(jnp.int32, (c.tq, c.tk), 0)
            kpos = kb * c.tk + lax.broadcasted_iota(jnp.int32, (c.tq, c.tk), 1)
            mask = _mask(c.mode, c.max_dist, qpos, kpos)
            for j in range(c.nh):
                g = j // c.rep
                s = _dot(q_ref[:, j * c.dqk:(j + 1) * c.dqk], k_ref[:, g * c.dqk:(g + 1) * c.dqk], NT) * c.scale
                if mask is not None:
                    s = jnp.where(mask, s, -jnp.inf)
                m_prev = m_scr[:, j:j + 1]
                m_new = jnp.maximum(m_prev, jnp.max(s, axis=1, keepdims=True))
                alpha = jnp.exp(m_prev - m_new)
                p = jnp.exp(s - m_new)
                l_scr[:, j:j + 1] = alpha * l_scr[:, j:j + 1] + jnp.sum(p, axis=1, keepdims=True)
                acc[:, j * c.dv:(j + 1) * c.dv] = (
                    alpha * acc[:, j * c.dv:(j + 1) * c.dv] + _dot(p, v_ref[:, g * c.dv:(g + 1) * c.dv], NN))
                m_scr[:, j:j + 1] = m_new

        _when(valid, step)

        @pl.when(kj == steps - 1)
        def _():
            for j in range(c.nh):
                o_ref[:, j * c.dv:(j + 1) * c.dv] = (
                    acc[:, j * c.dv:(j + 1) * c.dv] / l_scr[:, j:j + 1]).astype(o_ref.dtype)
            lane = _lane((c.tq, LANES))
            lse_ref[...] = jnp.where(lane < c.nh, m_scr[...] + jnp.log(jnp.maximum(l_scr[...], 1e-37)), 0.0)

    in_specs = [
        pl.BlockSpec((c.tq, c.nh * c.dqk), lambda g, i, j: (i, c.qcol(g))),
        pl.BlockSpec((c.tk, c.nkv * c.dqk), lambda g, i, j: (_kv_block(c.mode, i, j)[0], c.kcol(g))),
        pl.BlockSpec((c.tk, c.nkv * c.dv), lambda g, i, j: (_kv_block(c.mode, i, j)[0], c.vcol(g))),
    ]
    args = [q, k, v]
    if sink is not None:
        in_specs.append(pl.BlockSpec((1, LANES), lambda g, i, j: (0, 0)))
        args.append(sink)
    return _pcall(
        body, name=name, dims=("parallel", "parallel", "arbitrary"), grid=(c.G, nq, steps),
        in_specs=in_specs,
        out_specs=[pl.BlockSpec((c.tq, c.nh * c.dv), lambda g, i, j: (i, c.ocol(g))),
                   pl.BlockSpec((c.tq, LANES), lambda g, i, j: (i, g))],
        out_shape=[jax.ShapeDtypeStruct((c.T, c.o_width), out_dtype),
                   jax.ShapeDtypeStruct((c.T, LANES * c.G), F32)],
        scratch_shapes=[pltpu.VMEM((c.tq, LANES), F32), pltpu.VMEM((c.tq, LANES), F32),
                        pltpu.VMEM((c.tq, c.nh * c.dv), F32)],
    )(*args)


def _attn_delta(cfg, o, do, *, name, w=None, lse=None, sink=None, tm=512):
    c = cfg
    tm = _tile(c.T, tm)
    width = c.nh * c.dv

    def body(*refs):
        refs = list(refs)
        o_ref, do_ref = refs[:2]
        rest = refs[2:]
        w_ref = rest.pop(0) if w is not None else None
        lse_ref, sink_ref = (rest.pop(0), rest.pop(0)) if sink is not None else (None, None)
        d_ref = rest.pop(0)
        prod = o_ref[...].astype(F32) * do_ref[...].astype(F32)
        cols = [jnp.sum(prod[:, j * c.dv:(j + 1) * c.dv], axis=1, keepdims=True) for j in range(c.nh)]
        delta = _cols_to_lanes(cols, tm)
        if w is not None:
            delta = delta * w_ref[...]
        d_ref[...] = delta
        if sink is not None:
            ds_ref = rest.pop(0)

            @pl.when(pl.program_id(1) == 0)
            def _():
                ds_ref[...] = jnp.zeros_like(ds_ref)

            lane = _lane((tm, LANES))
            ps = jnp.where(lane < c.nh, jnp.exp(sink_ref[...] - lse_ref[...]), 0.0)
            ds_ref[...] -= jnp.sum(ps * delta, axis=0, keepdims=True)

    stat = pl.BlockSpec((tm, LANES), lambda g, i: (i, g))
    in_specs = [pl.BlockSpec((tm, width), lambda g, i: (i, c.ocol(g)))] * 2
    args = [o, do]
    out_specs, out_shape = [stat], [jax.ShapeDtypeStruct((c.T, LANES * c.G), F32)]
    if w is not None:
        in_specs.append(stat)
        args.append(w)
    if sink is not None:
        assert c.G == 1
        in_specs += [stat, pl.BlockSpec((1, LANES), lambda g, i: (0, 0))]
        args += [lse, sink]
        out_specs.append(pl.BlockSpec((1, LANES), lambda g, i: (0, 0)))
        out_shape.append(jax.ShapeDtypeStruct((1, LANES), F32))
    out = _pcall(
        body, name=name, dims=("arbitrary", "arbitrary"), grid=(c.G, c.T // tm),
        in_specs=in_specs, out_specs=out_specs, out_shape=out_shape,
    )(*args)
    return out if sink is not None else (out[0], None)


def _attn_dq(cfg, q, k, v, do, lse, delta, *, name, init=None, out_dtype=F32):
    c = cfg
    nq, nk = c.T // c.tq, c.Tk // c.tk
    steps = _attn_steps(c.mode, nk, c.tq, c.tk)
    qw = c.nh * c.dqk

    def body(*refs):
        if init is None:
            q_ref, k_ref, v_ref, do_ref, lse_ref, d_ref, dq_ref, acc = refs
        else:
            q_ref, k_ref, v_ref, do_ref, lse_ref, d_ref, init_ref, dq_ref, acc = refs
        qi, kj = pl.program_id(1), pl.program_id(2)
        kb, valid = _kv_block(c.mode, qi, kj)

        @pl.when(kj == 0)
        def _():
            acc[...] = jnp.zeros_like(acc) if init is None else init_ref[...].astype(F32)

        def step():
            qpos = qi * c.tq + lax.broadcasted_iota(jnp.int32, (c.tq, c.tk), 0)
            kpos = kb * c.tk + lax.broadcasted_iota(jnp.int32, (c.tq, c.tk), 1)
            mask = _mask(c.mode, c.max_dist, qpos, kpos)
            for j in range(c.nh):
                g = j // c.rep
                kh = k_ref[:, g * c.dqk:(g + 1) * c.dqk]
                s = _dot(q_ref[:, j * c.dqk:(j + 1) * c.dqk], kh, NT) * c.scale
                if mask is not None:
                    s = jnp.where(mask, s, -jnp.inf)
                p = jnp.exp(s - lse_ref[:, j:j + 1])
                dp = _dot(do_ref[:, j * c.dv:(j + 1) * c.dv], v_ref[:, g * c.dv:(g + 1) * c.dv], NT)
                ds = p * (dp - d_ref[:, j:j + 1]) * c.scale
                acc[:, j * c.dqk:(j + 1) * c.dqk] += _dot(ds, kh, NN)

        _when(valid, step)

        @pl.when(kj == steps - 1)
        def _():
            dq_ref[...] = acc[...].astype(dq_ref.dtype)

    kvb = lambda i, j: _kv_block(c.mode, i, j)[0]
    qspec = pl.BlockSpec((c.tq, qw), lambda g, i, j: (i, c.qcol(g)))
    stat = pl.BlockSpec((c.tq, LANES), lambda g, i, j: (i, g))
    in_specs = [
        qspec,
        pl.BlockSpec((c.tk, c.nkv * c.dqk), lambda g, i, j: (kvb(i, j), c.kcol(g))),
        pl.BlockSpec((c.tk, c.nkv * c.dv), lambda g, i, j: (kvb(i, j), c.vcol(g))),
        pl.BlockSpec((c.tq, c.nh * c.dv), lambda g, i, j: (i, c.ocol(g))),
        stat, stat,
    ]
    args = [q, k, v, do, lse, delta]
    dq_spec = pl.BlockSpec((c.tq, qw), lambda g, i, j: (i, g))
    if init is not None:
        in_specs.append(dq_spec)
        args.append(init)
    return _pcall(
        body, name=name, dims=("parallel", "parallel", "arbitrary"), grid=(c.G, nq, steps),
        in_specs=in_specs, out_specs=dq_spec,
        out_shape=jax.ShapeDtypeStruct((c.T, c.G * qw), out_dtype),
        scratch_shapes=[pltpu.VMEM((c.tq, qw), F32)],
    )(*args)


def _attn_dkv(cfg, q, k, v, do, lse, delta, *, name, init=None, out_dtype=F32):
    c = cfg
    nq, nk = c.T // c.tq, c.Tk // c.tk
    steps = _attn_steps(c.mode, nq, c.tk, c.tq)
    kw, vw = c.nkv * c.dqk, c.nkv * c.dv

    def body(*refs):
        if init is None:
            q_ref, k_ref, v_ref, do_ref, lse_ref, d_ref, dk_ref, dv_ref, dk_acc, dv_acc = refs
        else:
            q_ref, k_ref, v_ref, do_ref, lse_ref, d_ref, ik_ref, iv_ref, dk_ref, dv_ref, dk_acc, dv_acc = refs
        ki, qj = pl.program_id(1), pl.program_id(2)
        qb, valid = _q_block(c.mode, ki, qj, nq)

        @pl.when(qj == 0)
        def _():
            dk_acc[...] = jnp.zeros_like(dk_acc) if init is None else ik_ref[...].astype(F32)
            dv_acc[...] = jnp.zeros_like(dv_acc) if init is None else iv_ref[...].astype(F32)

        def step():
            kpos = ki * c.tk + lax.broadcasted_iota(jnp.int32, (c.tk, c.tq), 0)
            qpos = qb * c.tq + lax.broadcasted_iota(jnp.int32, (c.tk, c.tq), 1)
            mask = _mask(c.mode, c.max_dist, qpos, kpos)
            lse_t = lse_ref[...].T
            d_t = d_ref[...].T
            for j in range(c.nh):
                g = j // c.rep
                qh = q_ref[:, j * c.dqk:(j + 1) * c.dqk]
                doh = do_ref[:, j * c.dv:(j + 1) * c.dv]
                s_t = _dot(k_ref[:, g * c.dqk:(g + 1) * c.dqk], qh, NT) * c.scale
                if mask is not None:
                    s_t = jnp.where(mask, s_t, -jnp.inf)
                p_t = jnp.exp(s_t - lse_t[j:j + 1, :])
                dv_acc[:, g * c.dv:(g + 1) * c.dv] += _dot(p_t, doh, NN)
                dp_t = _dot(v_ref[:, g * c.dv:(g + 1) * c.dv], doh, NT)
                ds_t = p_t * (dp_t - d_t[j:j + 1, :]) * c.scale
                dk_acc[:, g * c.dqk:(g + 1) * c.dqk] += _dot(ds_t, qh, NN)

        _when(valid, step)

        @pl.when(qj == steps - 1)
        def _():
            dk_ref[...] = dk_acc[...].astype(dk_ref.dtype)
            dv_ref[...] = dv_acc[...].astype(dv_ref.dtype)

    qbi = lambda i, j: _q_block(c.mode, i, j, nq)[0]
    stat = pl.BlockSpec((c.tq, LANES), lambda g, i, j: (qbi(i, j), g))
    in_specs = [
        pl.BlockSpec((c.tq, c.nh * c.dqk), lambda g, i, j: (qbi(i, j), c.qcol(g))),
        pl.BlockSpec((c.tk, kw), lambda g, i, j: (i, c.kcol(g))),
        pl.BlockSpec((c.tk, vw), lambda g, i, j: (i, c.vcol(g))),
        pl.BlockSpec((c.tq, c.nh * c.dv), lambda g, i, j: (qbi(i, j), c.ocol(g))),
        stat, stat,
    ]
    args = [q, k, v, do, lse, delta]
    dk_spec = pl.BlockSpec((c.tk, kw), lambda g, i, j: (i, g))
    dv_spec = pl.BlockSpec((c.tk, vw), lambda g, i, j: (i, g))
    if init is not None:
        in_specs += [dk_spec, dv_spec]
        args += list(init)
    return _pcall(
        body, name=name, dims=("parallel", "parallel", "arbitrary"), grid=(c.G, nk, steps),
        in_specs=in_specs, out_specs=[dk_spec, dv_spec],
        out_shape=[jax.ShapeDtypeStruct((c.Tk, c.G * kw), out_dtype),
                   jax.ShapeDtypeStruct((c.Tk, c.G * vw), out_dtype)],
        scratch_shapes=[pltpu.VMEM((c.tk, kw), F32), pltpu.VMEM((c.tk, vw), F32)],
    )(*args)


TN = (((0,), (0,)), ((), ()))


def _band_mask(c, i):
    key = lax.broadcasted_iota(jnp.int32, (2 * BLOCK, BLOCK), 0)
    qry = lax.broadcasted_iota(jnp.int32, (2 * BLOCK, BLOCK), 1)
    d = BLOCK + qry - key
    return (d >= 0) & (d <= c.max_dist) & ((key >= BLOCK) | (i > 0))


def _head_pairs(c):
    return c.rep == 1 and c.dqk == c.dv == LANES // 2 and c.nh % 2 == 0


def _block_diagonal(pair):
    lane = _lane(pair.shape)
    zero = jnp.zeros_like(pair)
    return jnp.concatenate([jnp.where(lane < LANES // 2, pair, zero), jnp.where(lane >= LANES // 2, pair, zero)], axis=0)


def _own_blocks(t):
    n = t.shape[1] // 2
    rows = lax.broadcasted_iota(jnp.int32, (LANES, n), 0)
    return jnp.where(rows < LANES // 2, t[:, :n], t[:, n:])


def _rows_to_stats(rows, n):
    return jnp.concatenate(rows + [jnp.zeros((LANES - len(rows), n), F32)], axis=0).T


def _band_fwd(cfg, q, k, v, *, name, sink=None, out_dtype=F32):
    c = cfg
    assert c.mode == "band" and c.tq == c.tk == BLOCK and c.T == c.Tk
    nq = c.T // BLOCK

    def body(*refs):
        if sink is None:
            q_ref, kp_ref, kc_ref, vp_ref, vc_ref, o_ref, lse_ref = refs
        else:
            q_ref, kp_ref, kc_ref, vp_ref, vc_ref, sink_ref, o_ref, lse_ref = refs
        mask = _band_mask(c, pl.program_id(1))
        k2 = jnp.concatenate([kp_ref[...], kc_ref[...]], axis=0)
        v2 = jnp.concatenate([vp_ref[...], vc_ref[...]], axis=0)
        lses = []
        if _head_pairs(c):
            mask2 = jnp.concatenate([mask, mask], axis=1)
            pair_lanes = [slice(pc * LANES, (pc + 1) * LANES) for pc in range(c.nh // 2)]
            score = lambda sl: _dot(k2[:, sl], _block_diagonal(q_ref[:, sl]), NT)
            ahead, behind = score(pair_lanes[0]), None

            def finish(entry):
                sl, o_t, l = entry
                o_ref[:, sl] = _own_blocks(o_t / l).T.astype(o_ref.dtype)

            for pc, sl in enumerate(pair_lanes):
                s = ahead * c.scale
                if pc + 1 < len(pair_lanes):
                    ahead = score(pair_lanes[pc + 1])
                s = jnp.where(mask2, s, -jnp.inf)
                m = jnp.max(s, axis=0, keepdims=True)
                p = jnp.exp(s - m)
                l = jnp.sum(p, axis=0, keepdims=True)
                if behind is not None:
                    finish(behind)
                behind = (sl, _dot(v2[:, sl], p, TN), l)
                lse = m + jnp.log(l)
                lses += [lse[:, :BLOCK], lse[:, BLOCK:]]
            finish(behind)
        heads = [] if _head_pairs(c) else list(range(c.nh))
        score_of = lambda j: _dot(k2[:, (j // c.rep) * c.dqk:(j // c.rep + 1) * c.dqk],
                                  q_ref[:, j * c.dqk:(j + 1) * c.dqk], NT)
        ahead = score_of(0) if heads else None
        for j in heads:
            g = j // c.rep
            s = ahead * c.scale
            if j + 1 < c.nh:
                ahead = score_of(j + 1)
            s = jnp.where(mask, s, -jnp.inf)
            m = jnp.max(s, axis=0, keepdims=True)
            if sink is not None:
                sk = sink_ref[:, j:j + 1]
                m = jnp.maximum(m, sk)
            p = jnp.exp(s - m)
            l = jnp.sum(p, axis=0, keepdims=True)
            if sink is not None:
                l = l + jnp.exp(sk - m)
            o_t = _dot(v2[:, g * c.dv:(g + 1) * c.dv], p, TN)
            o_ref[:, j * c.dv:(j + 1) * c.dv] = (o_t / l).T.astype(o_ref.dtype)
            lses.append(m + jnp.log(l))
        lse_ref[...] = _rows_to_stats(lses, BLOCK)

    prev = lambda i: jnp.maximum(i - 1, 0)
    kw, vw = c.nkv * c.dqk, c.nkv * c.dv
    in_specs = [
        pl.BlockSpec((BLOCK, c.nh * c.dqk), lambda g, i: (i, c.qcol(g))),
        pl.BlockSpec((BLOCK, kw), lambda g, i: (prev(i), c.kcol(g))),
        pl.BlockSpec((BLOCK, kw), lambda g, i: (i, c.kcol(g))),
        pl.BlockSpec((BLOCK, vw), lambda g, i: (prev(i), c.vcol(g))),
        pl.BlockSpec((BLOCK, vw), lambda g, i: (i, c.vcol(g))),
    ]
    args = [q, k, k, v, v]
    if sink is not None:
        in_specs.append(pl.BlockSpec((1, LANES), lambda g, i: (0, 0)))
        args.append(sink)
    return _pcall(
        body, name=name, dims=("parallel", "parallel"), grid=(c.G, nq), in_specs=in_specs,
        out_specs=[pl.BlockSpec((BLOCK, c.nh * c.dv), lambda g, i: (i, c.ocol(g))),
                   pl.BlockSpec((BLOCK, LANES), lambda g, i: (i, g))],
        out_shape=[jax.ShapeDtypeStruct((c.T, c.o_width), out_dtype),
                   jax.ShapeDtypeStruct((c.T, LANES * c.G), F32)],
    )(*args)


def _band_bwd(cfg, q, k, v, do, lse, delta, *, name):
    c = cfg
    assert c.mode == "band" and c.tq == c.tk == BLOCK and c.T == c.Tk
    nq = c.T // BLOCK
    qw, kw, vw = c.nh * c.dqk, c.nkv * c.dqk, c.nkv * c.dv

    def body(q_ref, kp_ref, kc_ref, vp_ref, vc_ref, do_ref, lse_ref, d_ref, dq_ref, dk_ref, dv_ref, dk_c, dv_c):
        n = pl.program_id(1)

        @pl.when(n == 0)
        def _():
            dk_c[...] = jnp.zeros_like(dk_c)
            dv_c[...] = jnp.zeros_like(dv_c)

        @pl.when(n < nq)
        def _():
            mask = _band_mask(c, n)
            k2 = jnp.concatenate([kp_ref[...], kc_ref[...]], axis=0)
            v2 = jnp.concatenate([vp_ref[...], vc_ref[...]], axis=0)
            lse_t, d_t = lse_ref[...].T, d_ref[...].T
            if _head_pairs(c):
                mask2 = jnp.concatenate([mask, mask], axis=1)
                pair_lanes = [slice(pc * LANES, (pc + 1) * LANES) for pc in range(c.nh // 2)]

                def first(sl):
                    q_bd, do_bd = _block_diagonal(q_ref[:, sl]), _block_diagonal(do_ref[:, sl])
                    return q_bd, do_bd, k2[:, sl], _dot(k2[:, sl], q_bd, NT), _dot(v2[:, sl], do_bd, NT)

                def finish(entry):
                    sl, dq_t, dv_pair, dk_pair = entry
                    dq_ref[:, sl] = _own_blocks(dq_t).T
                    dk_ref[:, sl] = dk_c[:, sl] + dk_pair[:BLOCK]
                    dv_ref[:, sl] = dv_c[:, sl] + dv_pair[:BLOCK]
                    dk_c[:, sl] = dk_pair[BLOCK:]
                    dv_c[:, sl] = dv_pair[BLOCK:]

                ahead, behind = first(pair_lanes[0]), None
                for pc, sl in enumerate(pair_lanes):
                    q_bd, do_bd, kp, s, dp = ahead
                    if pc + 1 < len(pair_lanes):
                        ahead = first(pair_lanes[pc + 1])
                    both = lambda t: jnp.concatenate([t[2 * pc:2 * pc + 1, :], t[2 * pc + 1:2 * pc + 2, :]], axis=1)
                    p = jnp.exp(jnp.where(mask2, s * c.scale, -jnp.inf) - both(lse_t))
                    ds = p * (dp - both(d_t)) * c.scale
                    entry = (sl, _dot(kp, ds, TN), _dot(p, do_bd, NN), _dot(ds, q_bd, NN))
                    if behind is not None:
                        finish(behind)
                    behind = entry
                finish(behind)
                return
            dk2, dv2 = [None] * c.nkv, [None] * c.nkv

            def first_of(j):
                g = j // c.rep
                qh, doh = q_ref[:, j * c.dqk:(j + 1) * c.dqk], do_ref[:, j * c.dv:(j + 1) * c.dv]
                kh = k2[:, g * c.dqk:(g + 1) * c.dqk]
                return qh, doh, kh, _dot(kh, qh, NT), _dot(v2[:, g * c.dv:(g + 1) * c.dv], doh, NT)

            ahead = first_of(0)
            for j in range(c.nh):
                g = j // c.rep
                qh, doh, kh, s, dp = ahead
                if j + 1 < c.nh:
                    ahead = first_of(j + 1)
                p = jnp.exp(jnp.where(mask, s * c.scale, -jnp.inf) - lse_t[j:j + 1, :])
                ds = p * (dp - d_t[j:j + 1, :]) * c.scale
                dq_ref[:, j * c.dqk:(j + 1) * c.dqk] = _dot(kh, ds, TN).T
                dvh, dkh = _dot(p, doh, NN), _dot(ds, qh, NN)
                dv2[g] = dvh if dv2[g] is None else dv2[g] + dvh
                dk2[g] = dkh if dk2[g] is None else dk2[g] + dkh
            for g in range(c.nkv):
                ks, vs = slice(g * c.dqk, (g + 1) * c.dqk), slice(g * c.dv, (g + 1) * c.dv)
                dk_ref[:, ks] = dk_c[:, ks] + dk2[g][:BLOCK]
                dv_ref[:, vs] = dv_c[:, vs] + dv2[g][:BLOCK]
                dk_c[:, ks] = dk2[g][BLOCK:]
                dv_c[:, vs] = dv2[g][BLOCK:]

        @pl.when(n == nq)
        def _():
            dk_ref[...] = dk_c[...]
            dv_ref[...] = dv_c[...]

    cur = lambda n: jnp.minimum(n, nq - 1)
    prev = lambda n: jnp.maximum(cur(n) - 1, 0)
    out_blk = lambda n: jnp.maximum(n - 1, 0)
    stat = pl.BlockSpec((BLOCK, LANES), lambda g, n: (cur(n), g))
    dq_spec = pl.BlockSpec((BLOCK, qw), lambda g, n: (cur(n), g))
    dk_spec = pl.BlockSpec((BLOCK, kw), lambda g, n: (out_blk(n), g))
    dv_spec = pl.BlockSpec((BLOCK, vw), lambda g, n: (out_blk(n), g))
    in_specs = [
        pl.BlockSpec((BLOCK, qw), lambda g, n: (cur(n), c.qcol(g))),
        pl.BlockSpec((BLOCK, kw), lambda g, n: (prev(n), c.kcol(g))),
        pl.BlockSpec((BLOCK, kw), lambda g, n: (cur(n), c.kcol(g))),
        pl.BlockSpec((BLOCK, vw), lambda g, n: (prev(n), c.vcol(g))),
        pl.BlockSpec((BLOCK, vw), lambda g, n: (cur(n), c.vcol(g))),
        pl.BlockSpec((BLOCK, c.nh * c.dv), lambda g, n: (cur(n), c.ocol(g))),
        stat, stat,
    ]
    return _pcall(
        body, name=name, dims=("parallel", "arbitrary"), grid=(c.G, nq + 1), in_specs=in_specs,
        out_specs=[dq_spec, dk_spec, dv_spec],
        out_shape=[_sds((c.T, c.G * qw)), _sds((c.T, c.G * kw)), _sds((c.T, c.G * vw))],
        scratch_shapes=[pltpu.VMEM((BLOCK, kw), F32), pltpu.VMEM((BLOCK, vw), F32)],
    )(q, k, k, v, v, do, lse, delta)


def _causal_pairs(n, kv_major):
    pairs =[(i, j) for j in range(n) for i in range(j, n)] if kv_major else [(i, j) for i in range(n) for j in range(i + 1)]
    return jnp.asarray(np.array([p[0] for p in pairs], np.int32)), jnp.asarray(np.array([p[1] for p in pairs], np.int32))


def _causal_mask(t):
    return lax.broadcasted_iota(jnp.int32, (t, t), 0) >= lax.broadcasted_iota(jnp.int32, (t, t), 1)


def _carrying(body, n_in, n_out, n_scratch, grid, carry):
    if carry is None:
        return body
    G, P = grid

    def wrapped(*refs):
        refs = list(refs)
        prefetch, refs = refs[:2], refs[2:]
        ins, src = refs[:n_in], refs[n_in]
        outs, out = refs[n_in + 1:n_in + 1 + n_out], refs[n_in + 1 + n_out]
        scratch, sems = refs[n_in + 2 + n_out:n_in + 2 + n_out + n_scratch], refs[n_in + 2 + n_out + n_scratch:]
        step = pl.program_id(0) * P + pl.program_id(1)
        carry.run([src, out] + sems, step, G * P, at_end=False)
        body(*prefetch, *ins, *outs, *scratch)
        carry.run([src, out] + sems, step, G * P, at_end=True)

    return wrapped


def _carry_specs(carry):
    if carry is None:
        return [], [], [], [], []
    any_space = pl.BlockSpec(memory_space=pl.ANY)
    return [any_space], [any_space], [carry.out_shape], list(carry.sems), [carry.src]


def _causal_fwd(cfg, q, k, v, *, name, out_dtype=F32, stat_heads=None, carry=None):
    c = cfg
    assert c.mode == "causal" and c.tq == c.tk and c.T == c.Tk
    t, n = c.tq, c.T // c.tq
    stat_heads = stat_heads or c.nh
    stat_blocks = c.nh // stat_heads
    assert stat_blocks * stat_heads == c.nh
    qi_tab, kj_tab = _causal_pairs(n, kv_major=False)
    n_pairs = int(qi_tab.shape[0])

    def body(qi_ref, kj_ref, q_ref, k_ref, v_ref, o_ref, lse_ref, m_scr, l_scr, acc):
        pair = pl.program_id(1)
        qi, kj = qi_ref[pair], kj_ref[pair]

        @pl.when(kj == 0)
        def _():
            m_scr[...] = jnp.full_like(m_scr, NEG_BIG)
            l_scr[...] = jnp.zeros_like(l_scr)
            acc[...] = jnp.zeros_like(acc)

        def step(diagonal):
            mask = None
            if diagonal:
                mask = lax.broadcasted_iota(jnp.int32, (t, t), 1) >= lax.broadcasted_iota(jnp.int32, (t, t), 0)
            scores = [_dot(k_ref[:, (j // c.rep) * c.dqk:(j // c.rep + 1) * c.dqk],
                           q_ref[:, j * c.dqk:(j + 1) * c.dqk], NT) for j in range(c.nh)]
            for j in range(c.nh):
                g = j // c.rep
                s = scores[j] * c.scale
                if diagonal:
                    s = jnp.where(mask, s, -jnp.inf)
                m_prev = m_scr[j]
                m_new = jnp.maximum(m_prev, jnp.max(s, axis=0, keepdims=True))
                alpha = jnp.exp(m_prev - m_new)
                p = jnp.exp(s - m_new)
                l_scr[j] = alpha * l_scr[j] + jnp.sum(p, axis=0, keepdims=True)
                acc[j] = alpha * acc[j] + _dot(v_ref[:, g * c.dv:(g + 1) * c.dv], p, TN)
                m_scr[j] = m_new

        pl.when(kj == qi)(lambda: step(True))
        pl.when(kj != qi)(lambda: step(False))

        @pl.when(kj == qi)
        def _():
            rows = []
            for j in range(c.nh):
                o_ref[:, j * c.dv:(j + 1) * c.dv] = (acc[j] / l_scr[j]).T.astype(o_ref.dtype)
                rows.append(m_scr[j] + jnp.log(l_scr[j]))
            for b in range(stat_blocks):
                lse_ref[:, b * LANES:(b + 1) * LANES] = _rows_to_stats(rows[b * stat_heads:(b + 1) * stat_heads], t)

    x_in, x_out, x_shapes, x_scratch, x_args = _carry_specs(carry)
    grid_spec = pltpu.PrefetchScalarGridSpec(
        num_scalar_prefetch=2, grid=(c.G, n_pairs),
        in_specs=[pl.BlockSpec((t, c.nh * c.dqk), lambda g, p, qi, kj: (qi[p], c.qcol(g))),
                  pl.BlockSpec((t, c.nkv * c.dqk), lambda g, p, qi, kj: (kj[p], c.kcol(g))),
                  pl.BlockSpec((t, c.nkv * c.dv), lambda g, p, qi, kj: (kj[p], c.vcol(g)))] + x_in,
        out_specs=[pl.BlockSpec((t, c.nh * c.dv), lambda g, p, qi, kj: (qi[p], c.ocol(g))),
                   pl.BlockSpec((t, LANES * stat_blocks), lambda g, p, qi, kj: (qi[p], g))] + x_out,
        scratch_shapes=[pltpu.VMEM((c.nh, 1, t), F32), pltpu.VMEM((c.nh, 1, t), F32),
                        pltpu.VMEM((c.nh, c.dv, t), F32)] + x_scratch)
    return _pcall(
        _carrying(body, 3, 2, 3, (c.G, n_pairs), carry), name=name,
        dims=("arbitrary", "arbitrary") if carry is not None else ("parallel", "arbitrary"), grid_spec=grid_spec,
        out_shape=[jax.ShapeDtypeStruct((c.T, c.o_width), out_dtype),
                   jax.ShapeDtypeStruct((c.T, LANES * c.G * stat_blocks), F32)] + x_shapes,
    )(qi_tab, kj_tab, q, k, v, *x_args)


def _causal_bwd(cfg, q, k, v, do, lse, delta, *, name, carry=None):
    c = cfg
    assert c.mode == "causal" and c.tq == c.tk and c.T == c.Tk
    t, n = c.tq, c.T // c.tq
    qw, kw, vw = c.nh * c.dqk, c.nkv * c.dqk, c.nkv * c.dv
    qi_tab, kj_tab = _causal_pairs(n, kv_major=True)

    def body(qi_ref, kj_ref, q_ref, k_ref, v_ref, do_ref, lse_ref, d_ref, dq_ref, dk_ref, dv_ref, dk_acc, dv_acc):
        pair = pl.program_id(1)
        qi, kj = qi_ref[pair], kj_ref[pair]

        @pl.when(pair == 0)
        def _():
            dq_ref[...] = jnp.zeros_like(dq_ref)

        @pl.when(qi == kj)
        def _():
            dk_acc[...] = jnp.zeros_like(dk_acc)
            dv_acc[...] = jnp.zeros_like(dv_acc)

        rows = pl.ds(pl.multiple_of(qi * t, t), t)

        def step(diagonal):
            mask = _causal_mask(t) if diagonal else None
            for j in range(c.nh):
                g = j // c.rep
                qs, ks, vs = (slice(j * c.dqk, (j + 1) * c.dqk), slice(g * c.dqk, (g + 1) * c.dqk),
                              slice(g * c.dv, (g + 1) * c.dv))
                qh, doh, kh = q_ref[:, qs], do_ref[:, j * c.dv:(j + 1) * c.dv], k_ref[:, ks]
                s = _dot(qh, kh, NT) * c.scale
                if diagonal:
                    s = jnp.where(mask, s, -jnp.inf)
                p = jnp.exp(s - lse_ref[:, j:j + 1])
                ds = p * (_dot(doh, v_ref[:, vs], NT) - d_ref[:, j:j + 1]) * c.scale
                dq_ref[rows, qs] += _dot(ds, kh, NN)
                dv_acc[g] += _dot(doh, p, TN)
                dk_acc[g] += _dot(qh, ds, TN)

        pl.when(qi == kj)(lambda: step(True))
        pl.when(qi != kj)(lambda: step(False))

        @pl.when(qi == n - 1)
        def _():
            for g in range(c.nkv):
                dk_ref[:, g * c.dqk:(g + 1) * c.dqk] = dk_acc[g].T
                dv_ref[:, g * c.dv:(g + 1) * c.dv] = dv_acc[g].T

    stat = pl.BlockSpec((t, LANES), lambda g, p, qi, kj: (qi[p], g))
    o_spec = pl.BlockSpec((t, c.nh * c.dv), lambda g, p, qi, kj: (qi[p], c.ocol(g)))
    n_pairs = int(qi_tab.shape[0])
    x_in, x_out, x_shapes, x_scratch, x_args = _carry_specs(carry)
    grid_spec = pltpu.PrefetchScalarGridSpec(
        num_scalar_prefetch=2, grid=(c.G, n_pairs),
        in_specs=[pl.BlockSpec((t, qw), lambda g, p, qi, kj: (qi[p], c.qcol(g))),
                  pl.BlockSpec((t, kw), lambda g, p, qi, kj: (kj[p], c.kcol(g))),
                  pl.BlockSpec((t, vw), lambda g, p, qi, kj: (kj[p], c.vcol(g))),
                  o_spec, stat, stat] + x_in,
        out_specs=[pl.BlockSpec((c.T, qw), lambda g, p, qi, kj: (0, g)),
                   pl.BlockSpec((t, kw), lambda g, p, qi, kj: (kj[p], g)),
                   pl.BlockSpec((t, vw), lambda g, p, qi, kj: (kj[p], g))] + x_out,
        scratch_shapes=[pltpu.VMEM((c.nkv, c.dqk, t), F32), pltpu.VMEM((c.nkv, c.dv, t), F32)] + x_scratch)
    return _pcall(
        _carrying(body, 6, 3, 2, (c.G, n_pairs), carry), name=name,
        dims=("arbitrary", "arbitrary") if carry is not None else ("parallel", "arbitrary"), grid_spec=grid_spec,
        out_shape=[_sds((c.T, c.G * qw)), _sds((c.T, c.G * kw)), _sds((c.T, c.G * vw))] + x_shapes,
    )(qi_tab, kj_tab, q, k, v, do, lse, delta, *x_args)


def _rowwise(body, ins, outs, *, name, rows, tm=512, accs=(), scratch=()):
    tm = _row_tile(rows, tm)

    def spec(a):
        if a.shape[0] == 1:
            return pl.BlockSpec((1, a.shape[1]), lambda i: (0, 0))
        d = rows // a.shape[0]
        assert d * a.shape[0] == rows and tm % d == 0
        return pl.BlockSpec((tm // d, a.shape[1]), lambda i: (i, 0))

    return _pcall(
        functools.partial(body, tm), name=name, dims=("arbitrary" if accs else "parallel",), grid=(rows // tm,),
        in_specs=[spec(a) for a in ins], out_specs=[spec(a) for a in outs], out_shape=list(outs),
        scratch_shapes=list(scratch),
    )(*ins)


def _sds(shape, dtype=F32):
    return jax.ShapeDtypeStruct(shape, dtype)


def _acc_rows(ref, val):
    @pl.when(pl.program_id(0) == 0)
    def _():
        ref[...] = jnp.zeros_like(ref)

    ref[...] += jnp.sum(val, axis=0, keepdims=True)


Z_QA, Z_KA, Z_VA, Z_CQ, Z_CKV, Z_KR, Z_END = 0, 512, 640, 768, 1152, 1408, 1536


def _l0_prep(z, tabs, q_norm, kv_norm, *, name):
    S = z.shape[0]

    def body(tm, z_ref, c64, s64, ck, sk, gq, gkv, qa_o, ka_o, va_o, cq_o, ckv_o, kr_o):
        for i in range(4):
            sl = slice(Z_QA + i * LANES, Z_QA + (i + 1) * LANES)
            qa_o[:, i * LANES:(i + 1) * LANES] = _rope_chunk(z_ref[:, sl], c64[...], s64[...], 32).astype(qa_o.dtype)
        ka_o[...] = _rope_chunk(z_ref[:, Z_KA:Z_VA], c64[...], s64[...], 32).astype(ka_o.dtype)
        va_o[...] = z_ref[:, Z_VA:Z_CQ].astype(va_o.dtype)
        cq_o[...] = (_rms_parts(z_ref[:, Z_CQ:Z_CKV])[0] * gq[...]).astype(cq_o.dtype)
        ckv_o[...] = (_rms_parts(z_ref[:, Z_CKV:Z_KR])[0] * gkv[...]).astype(ckv_o.dtype)
        kr_o[...] = _rope_chunk(z_ref[:, Z_KR:Z_END], ck[...], sk[...], 16)

    outs = [_sds((S, 512), MXU_DTYPE), _sds((S, 128), MXU_DTYPE), _sds((S, 128), MXU_DTYPE),
            _sds((S, MLA_Q_RANK), MXU_DTYPE), _sds((S, MLA_KV_RANK), MXU_DTYPE), _sds((S, LANES))]
    ins = [z, tabs["c64"], tabs["s64"], tabs["ck"], tabs["sk"], q_norm.reshape(1, -1), kv_norm.reshape(1, -1)]
    return _rowwise(body, ins, outs, name=name, rows=S)


def _l0_prep_bwd(z, tabs, q_norm, kv_norm, dqa, dka, dva, dcq, dckv, dkr, *, name):
    S = z.shape[0]

    def body(tm, z_ref, c64, s64, ck, sk, gq, gkv, dqa_r, dka_r, dva_r, dcq_r, dckv_r, dkr_r, dz_o, dgq_o, dgkv_o):
        for i in range(4):
            sl = slice(i * LANES, (i + 1) * LANES)
            dz_o[:, sl] = _rope_chunk(dqa_r[:, sl].astype(F32), c64[...], -s64[...], 32).astype(dz_o.dtype)
        dz_o[:, Z_KA:Z_VA] = _rope_chunk(dka_r[...].astype(F32), c64[...], -s64[...], 32).astype(dz_o.dtype)
        dz_o[:, Z_VA:Z_CQ] = dva_r[...].astype(dz_o.dtype)
        dx, dgp = _rms_bwd_rows(z_ref[:, Z_CQ:Z_CKV], gq[...], dcq_r[...].astype(F32))
        dz_o[:, Z_CQ:Z_CKV] = dx.astype(dz_o.dtype)
        _acc_rows(dgq_o, dgp)
        dx, dgp = _rms_bwd_rows(z_ref[:, Z_CKV:Z_KR], gkv[...], dckv_r[...].astype(F32))
        dz_o[:, Z_CKV:Z_KR] = dx.astype(dz_o.dtype)
        _acc_rows(dgkv_o, dgp)
        dz_o[:, Z_KR:Z_END] = _rope_chunk(dkr_r[...], ck[...], -sk[...], 16).astype(dz_o.dtype)

    outs = [_sds((S, Z_END), MXU_DTYPE), _sds((1, MLA_Q_RANK)), _sds((1, MLA_KV_RANK))]
    ins = [z, tabs["c64"], tabs["s64"], tabs["ck"], tabs["sk"], q_norm.reshape(1, -1), kv_norm.reshape(1, -1),
           dqa, dka, dva, dcq, dckv, dkr]
    return _rowwise(body, ins, outs, name=name, rows=S, accs=(1, 2))


def _mla_prep(qb, kvb, kr, tabs, *, name):
    S = qb.shape[0]

    def body(tm, qb_r, kvb_r, kr_r, cm, sm, q_o, k_o, v_o):
        lane = _lane((tm, LANES))
        kr_at_64 = pltpu.roll(kr_r[...], 64, 1)
        for h in range(MLA_HEADS):
            sl = slice(h * LANES, (h + 1) * LANES)
            q_o[:, sl] = _rope_chunk(qb_r[:, sl], cm[...], sm[...], 16).astype(q_o.dtype)
            k_o[:, sl] = jnp.where(lane < 64, kvb_r[:, sl], kr_at_64).astype(k_o.dtype)
        for p in range(MLA_HEADS // 2):
            even = pltpu.roll(kvb_r[:, (2 * p) * LANES:(2 * p + 1) * LANES], 64, 1)
            odd = kvb_r[:, (2 * p + 1) * LANES:(2 * p + 2) * LANES]
            v_o[:, p * LANES:(p + 1) * LANES] = jnp.where(lane < 64, even, odd).astype(v_o.dtype)

    outs = [_sds((S, 1024), MXU_DTYPE), _sds((S, 1024), MXU_DTYPE), _sds((S, 512), MXU_DTYPE)]
    return _rowwise(body, [qb, kvb, kr, tabs["cm"], tabs["sm"]], outs, name=name, rows=S)


def _mla_prep_bwd(dq, dk, dv, tabs, *, name):
    S = dq.shape[0]

    def body(tm, dq_r, dk_r, dv_r, cm, sm, dqb_o, dkvb_o, dkr_o):
        lane = _lane((tm, LANES))
        dkr = jnp.zeros((tm, LANES), F32)
        for h in range(MLA_HEADS):
            sl = slice(h * LANES, (h + 1) * LANES)
            dqb_o[:, sl] = _rope_chunk(dq_r[:, sl].astype(F32), cm[...], -sm[...], 16).astype(dqb_o.dtype)
            dkh = dk_r[:, sl].astype(F32)
            dvp = dv_r[:, (h // 2) * LANES:(h // 2 + 1) * LANES].astype(F32)
            dvh = pltpu.roll(dvp, 64, 1) if h % 2 == 0 else dvp
            dkvb_o[:, sl] = jnp.where(lane < 64, dkh, dvh).astype(dkvb_o.dtype)
            dkr = dkr + pltpu.roll(dkh, 64, 1)
        dkr_o[...] = jnp.where(lane < MLA_ROPE, dkr, 0.0)

    outs = [_sds((S, 1024), MXU_DTYPE), _sds((S, 1024), MXU_DTYPE), _sds((S, LANES))]
    return _rowwise(body, [dq, dk, dv, tabs["cm"], tabs["sm"]], outs, name=name, rows=S)


DILATIONS = tuple(d for _, d in DIL_PATTERNS)
QKV_CHUNKS = 8


def _to_branch(nat, c0, chunks, out_ref, d, rows):
    width = chunks * LANES
    for r in range(d):
        tok = pl.ds(r, rows // d, stride=d) if d > 1 else slice(None)
        for c in range(chunks):
            out_ref[:, r * width + c * LANES:r * width + (c + 1) * LANES] = nat[c0 + c, tok, :].astype(out_ref.dtype)


def _from_branch(in_ref, nat, c0, chunks, d, rows, add=False):
    width = chunks * LANES
    for r in range(d):
        tok = pl.ds(r, rows // d, stride=d) if d > 1 else slice(None)
        for c in range(chunks):
            val = in_ref[:, r * width + c * LANES:r * width + (c + 1) * LANES].astype(F32)
            nat[c0 + c, tok, :] = nat[c0 + c, tok, :] + val if add else val


def _branch_sds(S, width, d, dtype):
    return _sds((S // d, d * width), dtype)


def _l1_prep(qkv, tabs, *, name):
    S = qkv.shape[0]

    def body(tm, x_r, c64, s64, *rest):
        outs, nat = rest[:-1], rest[-1]
        for i in range(QKV_CHUNKS):
            sl = slice(i * LANES, (i + 1) * LANES)
            nat[i] = _rope_chunk(x_r[:, sl], c64[...], s64[...], 32)
            nat[QKV_CHUNKS + i] = _rope_chunk(x_r[:, 1024 + i * LANES:1024 + (i + 1) * LANES], c64[...], s64[...], 32)
            nat[2 * QKV_CHUNKS + i] = x_r[:, 2048 + i * LANES:2048 + (i + 1) * LANES]
        for b, d in enumerate(DILATIONS):
            for t in range(3):
                _to_branch(nat, t * QKV_CHUNKS, QKV_CHUNKS, outs[3 * b + t], d, tm)

    outs = [_branch_sds(S, 1024, d, MXU_DTYPE) for d in DILATIONS for _ in range(3)]
    got = _rowwise(body, [qkv, tabs["c64"], tabs["s64"]], outs, name=name, rows=S,
                   scratch=[pltpu.VMEM((3 * QKV_CHUNKS, _row_tile(S, 512), LANES), F32)])
    return {d: tuple(got[3 * b:3 * b + 3]) for b, d in enumerate(DILATIONS)}


def _l1_prep_bwd(grads, tabs, *, name):
    S = grads[1][0].shape[0]

    def body(tm, *rest):
        ins, (c64, s64, o, nat) = rest[:9], rest[9:]
        for b, d in enumerate(DILATIONS):
            for t in range(3):
                _from_branch(ins[3 * b + t], nat, t * QKV_CHUNKS, QKV_CHUNKS, d, tm, add=b > 0)
        for i in range(QKV_CHUNKS):
            sl = slice(i * LANES, (i + 1) * LANES)
            o[:, sl] = _rope_chunk(nat[i], c64[...], -s64[...], 32).astype(o.dtype)
            o[:, 1024 + i * LANES:1024 + (i + 1) * LANES] = _rope_chunk(
                nat[QKV_CHUNKS + i], c64[...], -s64[...], 32).astype(o.dtype)
            o[:, 2048 + i * LANES:2048 + (i + 1) * LANES] = nat[2 * QKV_CHUNKS + i].astype(o.dtype)

    ins = [g for d in DILATIONS for g in grads[d]] + [tabs["c64"], tabs["s64"]]
    return _rowwise(body, ins, [_sds((S, 3072), MXU_DTYPE)], name=name, rows=S, tm=256,
                    scratch=[pltpu.VMEM((3 * QKV_CHUNKS, _row_tile(S, 256), LANES), F32)])[0]


def _sigmoid(x):
    return 1.0 / (1.0 + jnp.exp(-x))


FFN_ROW_TILE, FFN_COL_TILE = 512, 1408


def _gate_up(h, w_gate, w_up, *, name):
    (M, K), N = h.shape, w_gate.shape[1]
    tm, tn = _tile(M, FFN_ROW_TILE), _tile(N, FFN_COL_TILE)

    def body(h_ref, wg_ref, wu_ref, g_ref, u_ref, a_ref):
        g = _dot(h_ref[...], wg_ref[...], NN)
        u = _dot(h_ref[...], wu_ref[...], NN)
        g_ref[...] = g
        u_ref[...] = u
        a_ref[...] = (g * _sigmoid(g) * u).astype(a_ref.dtype)

    w_spec = pl.BlockSpec((K, tn), lambda j, i: (0, j))
    o_spec = pl.BlockSpec((tm, tn), lambda j, i: (i, j))
    return _pcall(
        body, name=name, dims=("parallel", "parallel"), grid=(N // tn, M // tm),
        in_specs=[pl.BlockSpec((tm, K), lambda j, i: (i, 0)), w_spec, w_spec], out_specs=[o_spec] * 3,
        out_shape=[_sds((M, N)), _sds((M, N)), _sds((M, N), MXU_DTYPE)],
    )(h, w_gate, w_up)


def _gate_up_bwd(dx, w_down, gate, up, *, name):
    (M, K), N = dx.shape, w_down.shape[0]
    tm, tn = _tile(M, FFN_ROW_TILE), _tile(N, FFN_COL_TILE)

    def body(dx_ref, w_ref, g_ref, u_ref, dg_ref, du_ref):
        d = _dot(dx_ref[...], w_ref[...], NT)
        g = g_ref[...]
        sg = _sigmoid(g)
        dg_ref[...] = (d * u_ref[...] * (sg * (1.0 + g * (1.0 - sg)))).astype(dg_ref.dtype)
        du_ref[...] = (d * g * sg).astype(du_ref.dtype)

    o_spec = pl.BlockSpec((tm, tn), lambda j, i: (i, j))
    return _pcall(
        body, name=name, dims=("parallel", "parallel"), grid=(N // tn, M // tm),
        in_specs=[pl.BlockSpec((tm, K), lambda j, i: (i, 0)), pl.BlockSpec((tn, K), lambda j, i: (j, 0)),
                  o_spec, o_spec],
        out_specs=[o_spec] * 2, out_shape=[_sds((M, N), MXU_DTYPE)] * 2,
    )(dx, w_down, gate, up)


def _head_pair_weights(w, c, rows):
    return jnp.where(_lane((rows, LANES)) < HEAD_DIM, w[:, 2 * c:2 * c + 1], w[:, 2 * c + 1:2 * c + 2])


def _merge(outs_by_d, lses_by_d, *, name):
    S = outs_by_d[1].shape[0]
    far = DILATIONS[1:]

    def body(tm, o1, o4, o16, l1, l4, l16, o_o, w1_o, w4_o, w16_o, nat_o, nat_l):
        for b, (o_r, l_r, d) in enumerate(zip((o4, o16), (l4, l16), far)):
            _from_branch(o_r, nat_o, b * QKV_CHUNKS, QKV_CHUNKS, d, tm)
            _from_branch(l_r, nat_l, b, 1, d, tm)
        ls = [l1[...], nat_l[0], nat_l[1]]
        m = jnp.maximum(jnp.maximum(ls[0], ls[1]), ls[2])
        es = [jnp.exp(l - m) for l in ls]
        tot = es[0] + es[1] + es[2]
        ws = [e / tot for e in es]
        for w_o, w in zip((w1_o, w4_o, w16_o), ws):
            w_o[...] = w
        for c in range(QKV_CHUNKS):
            sl = slice(c * LANES, (c + 1) * LANES)
            parts = (o1[:, sl], nat_o[c], nat_o[QKV_CHUNKS + c])
            o_o[:, sl] = sum(_head_pair_weights(w, c, tm) * part for w, part in zip(ws, parts))

    ins = [outs_by_d[d] for d in DILATIONS] + [lses_by_d[d] for d in DILATIONS]
    outs = [_sds((S, 1024))] + [_sds((S, LANES))] * 3
    rows = _row_tile(S, 256)
    return _rowwise(body, ins, outs, name=name, rows=S, tm=256,
                    scratch=[pltpu.VMEM((2 * QKV_CHUNKS, rows, LANES), F32), pltpu.VMEM((2, rows, LANES), F32)])


def _merge_bwd(do, o, ws, *, name):
    S = do.shape[0]

    def body(tm, do_r, o_r, w1, w4, w16, d1, d4, d16, e1, e4, e16, nat, nat_l):
        prod = do_r[...] * o_r[...]
        sums = _cols_to_lanes([jnp.sum(prod[:, j * HEAD_DIM:(j + 1) * HEAD_DIM], axis=1, keepdims=True)
                               for j in range(DIL_HEADS)], tm)
        for w_r, d_o, e_o, d in zip((w1, w4, w16), (d1, d4, d16), (e1, e4, e16), DILATIONS):
            w = w_r[...]
            nat_l[0] = w * sums
            _to_branch(nat_l, 0, 1, e_o, d, tm)
            for c in range(QKV_CHUNKS):
                nat[c] = _head_pair_weights(w, c, tm) * do_r[:, c * LANES:(c + 1) * LANES]
            _to_branch(nat, 0, QKV_CHUNKS, d_o, d, tm)

    outs = [_branch_sds(S, 1024, d, MXU_DTYPE) for d in DILATIONS] + [_branch_sds(S, LANES, d, F32) for d in DILATIONS]
    rows = _row_tile(S, 256)
    got = _rowwise(body, [do, o] + [ws[d] for d in DILATIONS], outs, name=name, rows=S, tm=256,
                   scratch=[pltpu.VMEM((QKV_CHUNKS, rows, LANES), F32), pltpu.VMEM((1, rows, LANES), F32)])
    return dict(zip(DILATIONS, got[:3])), dict(zip(DILATIONS, got[3:]))


def _loss_head(x, g, target, *, name):
    S, D = x.shape

    def body(tm, x_r, g_r, t_r, dx_o, dg_o, sq_o):
        xf = x_r[...]
        xhat, _ = _rms_parts(xf)
        err = xhat * g_r[...] - t_r[...]
        dx, dgp = _rms_bwd_rows(xf, g_r[...], err * (1.0 / D))
        dx_o[...] = dx
        _acc_rows(dg_o, dgp)
        _acc_rows(sq_o, err * err)

    return _rowwise(body, [x, g.reshape(1, D), target], [_sds((S, D)), _sds((1, D)), _sds((1, D))],
                    name=name, rows=S, accs=(1, 2))


def _adamw(w, g, m, v, *, name):
    c1 = 1.0 - ADAM_B1 ** ADAM_STEP
    c2 = 1.0 - ADAM_B2 ** ADAM_STEP

    def body(tm, w_r, g_r, m_r, v_r, d_o, m_o, v_o):
        g = g_r[...]
        m_new = ADAM_B1 * m_r[...] + (1.0 - ADAM_B1) * g
        v_new = ADAM_B2 * v_r[...] + (1.0 - ADAM_B2) * (g * g)
        m_o[...] = m_new
        v_o[...] = v_new
        d_o[...] = -ADAM_LR * ((m_new / c1) / (jnp.sqrt(v_new / c2) + ADAM_EPS) + ADAM_WD * w_r[...])

    return _rowwise(body, [w, g, m, v], [_sds(w.shape)] * 3, name=name, rows=w.shape[0], tm=256)


SUM_ROW_TILE = 256


def _sum_cores(grads, theirs, half_index, *, name):
    _, R, C = grads.shape
    h = R // 2
    nb = h // SUM_ROW_TILE

    def body(c_ref, g_ref, t_ref, o_ref):
        o_ref[...] = (g_ref[...].astype(F32) + t_ref[...].astype(F32)).astype(o_ref.dtype)

    grid_spec = pltpu.PrefetchScalarGridSpec(
        num_scalar_prefetch=1, grid=(4, nb),
        in_specs=[pl.BlockSpec((1, SUM_ROW_TILE, C), lambda k, i, c_ref: (k, c_ref[0] * nb + i, 0)),
                  pl.BlockSpec((1, SUM_ROW_TILE, C), lambda k, i, c_ref: (k, i, 0))],
        out_specs=pl.BlockSpec((1, SUM_ROW_TILE, C), lambda k, i, c_ref: (k, i, 0)))
    return _pcall(body, name=name, dims=("parallel", "parallel"), grid_spec=grid_spec,
                  out_shape=_sds((4, h, C), grads.dtype))(half_index, grads, theirs)


def _sum_chips(parts, half_index, *, name):
    _, h, C = parts.shape
    nb = h // SUM_ROW_TILE

    def body(c_ref, p_ref, o_ref):
        p = [p_ref[k].astype(F32) for k in range(4)]
        o_ref[...] = ((p[0] + p[1]) + p[2]) + p[3]

    grid_spec = pltpu.PrefetchScalarGridSpec(
        num_scalar_prefetch=1, grid=(nb,),
        in_specs=[pl.BlockSpec((4, SUM_ROW_TILE, C), lambda i, c_ref: (0, i, 0))],
        out_specs=pl.BlockSpec((SUM_ROW_TILE, C), lambda i, c_ref: (c_ref[0] * nb + i, 0)))
    return _pcall(body, name=name, dims=("parallel",), grid_spec=grid_spec,
                  out_shape=_sds((2 * h, C)))(half_index, parts)


def _position():
    return lax.axis_index("x"), lax.axis_index("y"), lax.axis_index("c")


def _chip_peers(x, y):
    return [(1 - x, y), (x, 1 - y), (1 - x, 1 - y)]


_HBM = pl.BlockSpec(memory_space=pltpu.HBM)
LOCAL_COPY_CHUNKS = 8


def _local_copies(src_ref, dst_ref, sems):
    rows = src_ref.shape[0] // LOCAL_COPY_CHUNKS
    assert rows * LOCAL_COPY_CHUNKS == src_ref.shape[0]
    return [pltpu.make_async_copy(src_ref.at[pl.ds(i * rows, rows)], dst_ref.at[pl.ds(i * rows, rows)], sems.at[i])
            for i in range(LOCAL_COPY_CHUNKS)]


class _Exchange:
    def __init__(self, src, out_shape, sems, stages):
        self.src, self.out_shape, self.sems, self.stages = src, out_shape, sems, stages

    def run(self, refs, step, n_steps, at_end):
        for fraction, fn in self.stages:
            if (fraction == 1.0) == at_end:
                pl.when(step == int(round(fraction * (n_steps - 1))))(functools.partial(fn, *refs))


def _run_exchange(ex, *, name):
    def body(*refs):
        for _, fn in ex.stages:
            fn(*refs)

    return pl.pallas_call(
        body, name=name, in_specs=[_HBM], out_specs=_HBM, out_shape=ex.out_shape, scratch_shapes=list(ex.sems),
    )(ex.src)


def _gather_exchange(src):
    R, C = src.shape
    h = R // 2

    def plan(src_ref, out_ref, send_sems, recv_sems, local_sems):
        x, y, c = _position()
        me = 2 * x + y
        peers = _chip_peers(x, y)
        mine, other = pl.ds(c * h, h), pl.ds((1 - c) * h, h)

        def copy(sem, src_part, dst_part, device):
            return pltpu.make_async_remote_copy(
                src_ref=src_part, dst_ref=dst_part, send_sem=send_sems.at[sem], recv_sem=recv_sems.at[sem],
                device_id=device, device_id_type=MESH)

        landed = [out_ref.at[2 * px + py, mine] for px, py in peers]
        theirs = [out_ref.at[2 * px + py, other] for px, py in peers]
        return dict(
            sends=lambda: [copy(j, src_ref.at[mine], out_ref.at[me, mine], (px, py, c))
                           for j, (px, py) in enumerate(peers)],
            local=lambda: _local_copies(src_ref, out_ref.at[me], local_sems),
            arrivals=lambda: [copy(j, landed[j], landed[j], (px, py, c)) for j, (px, py) in enumerate(peers)],
            passed=lambda: [copy(3 + j, landed[j], landed[j], (x, y, 1 - c)) for j in range(3)],
            from_sibling=lambda: [copy(3 + j, theirs[j], theirs[j], (x, y, 1 - c)) for j in range(3)])

    def start(*refs):
        p = plan(*refs)
        for cp in p["sends"]() + p["local"]():
            cp.start()

    def pass_on(*refs):
        p = plan(*refs)
        for arrival, forward in zip(p["arrivals"](), p["passed"]()):
            arrival.wait_recv()
            forward.start()

    def finish(*refs):
        p = plan(*refs)
        for cp in p["from_sibling"]():
            cp.wait_recv()
        for cp in p["sends"]() + p["passed"]():
            cp.wait_send()
        for cp in p["local"]():
            cp.wait()

    sems = [pltpu.SemaphoreType.DMA((6,)), pltpu.SemaphoreType.DMA((6,)), pltpu.SemaphoreType.DMA((LOCAL_COPY_CHUNKS,))]
    return _Exchange(src, jax.ShapeDtypeStruct((4, R, C), src.dtype), sems, [(0.0, start), (0.6, pass_on), (1.0, finish)])


def _swap_other_half(src, *, name):
    _, R, C = src.shape
    h = R // 2

    def body(src_ref, out_ref, send_sem, recv_sem):
        x, y, c = _position()
        cp = pltpu.make_async_remote_copy(
            src_ref=src_ref.at[:, pl.ds((1 - c) * h, h)], dst_ref=out_ref, send_sem=send_sem, recv_sem=recv_sem,
            device_id=(x, y, 1 - c), device_id_type=MESH)
        cp.start()
        cp.wait()

    return pl.pallas_call(
        body, name=name, in_specs=[_HBM], out_specs=_HBM, out_shape=jax.ShapeDtypeStruct((4, h, C), src.dtype),
        scratch_shapes=[pltpu.SemaphoreType.DMA, pltpu.SemaphoreType.DMA],
    )(src)


def _scatter_exchange(src):
    def plan(src_ref, out_ref, send_sems, recv_sems, local_sems):
        x, y, c = _position()
        me = 2 * x + y
        peers = _chip_peers(x, y)

        def copy(j, src_block, dst_slot):
            px, py = peers[j]
            return pltpu.make_async_remote_copy(
                src_ref=src_ref.at[src_block], dst_ref=out_ref.at[dst_slot], send_sem=send_sems.at[j],
                recv_sem=recv_sems.at[j], device_id=(px, py, c), device_id_type=MESH)

        return dict(sends=lambda: [copy(j, 2 * px + py, me) for j, (px, py) in enumerate(peers)],
                    arrivals=lambda: [copy(j, me, 2 * px + py) for j, (px, py) in enumerate(peers)],
                    local=lambda: _local_copies(src_ref.at[me], out_ref.at[me], local_sems))

    def start(*refs):
        p = plan(*refs)
        for cp in p["sends"]() + p["local"]():
            cp.start()

    def finish(*refs):
        p = plan(*refs)
        for cp in p["arrivals"]():
            cp.wait_recv()
        for cp in p["sends"]():
            cp.wait_send()
        for cp in p["local"]():
            cp.wait()

    sems = [pltpu.SemaphoreType.DMA((3,)), pltpu.SemaphoreType.DMA((3,)), pltpu.SemaphoreType.DMA((LOCAL_COPY_CHUNKS,))]
    return _Exchange(src, jax.ShapeDtypeStruct(src.shape, src.dtype), sems, [(0.0, start), (1.0, finish)])


def _join_halves(src, *, name):
    R, C = src.shape
    h = R // 2

    def body(src_ref, out_ref, send_sem, recv_sem):
        x, y, c = _position()
        mine, theirs = pl.ds(c * h, h), pl.ds((1 - c) * h, h)
        cp = pltpu.make_async_remote_copy(
            src_ref=src_ref.at[mine], dst_ref=out_ref.at[mine], send_sem=send_sem, recv_sem=recv_sem,
            device_id=(x, y, 1 - c), device_id_type=MESH)
        cp.start()
        pltpu.make_async_remote_copy(
            src_ref=src_ref.at[theirs], dst_ref=out_ref.at[theirs], send_sem=send_sem, recv_sem=recv_sem,
            device_id=(x, y, 1 - c), device_id_type=MESH).wait_recv()
        cp.wait_send()

    return pl.pallas_call(
        body, name=name, in_specs=[_HBM], out_specs=_HBM, out_shape=jax.ShapeDtypeStruct((R, C), src.dtype),
        input_output_aliases={0: 0},
        scratch_shapes=[pltpu.SemaphoreType.DMA, pltpu.SemaphoreType.DMA],
    )(src)


def _allreduce_small(vec, *, name):
    R, C = vec.shape

    def body(v_ref, o_ref, slots, send_sems, recv_sems):
        x, y, c = _position()
        me = 4 * x + 2 * y + c

        def peer(k):
            return x ^ ((k >> 2) & 1), y ^ ((k >> 1) & 1), c ^ (k & 1)

        def copy(k, slot):
            return pltpu.make_async_remote_copy(
                src_ref=v_ref, dst_ref=slots.at[slot], send_sem=send_sems.at[k - 1], recv_sem=recv_sems.at[k - 1],
                device_id=peer(k), device_id_type=MESH)

        slots[me] = v_ref[...]
        sends = [copy(k, me) for k in range(1, 8)]
        for cp in sends:
            cp.start()
        for k in range(1, 8):
            px, py, pc = peer(k)
            copy(k, 4 * px + 2 * py + pc).wait_recv()
        total = slots[0]
        for d in range(1, 8):
            total = total + slots[d]
        o_ref[...] = total
        for cp in sends:
            cp.wait_send()

    vmem = pl.BlockSpec(memory_space=pltpu.VMEM)
    return pl.pallas_call(
        body, name=name, in_specs=[vmem], out_specs=vmem, out_shape=jax.ShapeDtypeStruct((R, C), vec.dtype),
        scratch_shapes=[pltpu.VMEM((8, R, C), vec.dtype), pltpu.SemaphoreType.DMA((7,)), pltpu.SemaphoreType.DMA((7,))],
    )(vec)


def _cross_cfg(S, mem_len):
    return _Attn(T=S, Tk=mem_len, G=1, nh=X_HEADS, rep=1, dqk=X_HEAD_DIM, dv=X_HEAD_DIM, tq=512, tk=mem_len,
                 mode="none", scale=X_HEAD_DIM ** -0.5, qcol=lambda g: 0, kcol=lambda g: 0, vcol=lambda g: 1,
                 ocol=lambda g: 0, o_width=X_HEADS * X_HEAD_DIM)


def _swa_cfg(S):
    return _Attn(T=S, Tk=S, G=1, nh=SWA_HEADS, rep=SWA_HEADS // SWA_KV_HEADS, dqk=HEAD_DIM, dv=HEAD_DIM, tq=BLOCK,
                 tk=BLOCK, mode="band", max_dist=SWA_WINDOW - 1, scale=HEAD_DIM ** -0.5, qcol=lambda g: 0,
                 kcol=lambda g: 0, vcol=lambda g: 0, ocol=lambda g: 0, o_width=SWA_HEADS * HEAD_DIM)


MLA_FWD_GROUP = 8
MLA_BWD_GROUP = 2


def _mla_cfg(S, group):
    t = _tile(S, 512)
    return _Attn(T=S, Tk=S, G=MLA_HEADS // group, nh=group, rep=1, dqk=LANES, dv=MLA_V, tq=t, tk=t, mode="causal",
                 scale=(MLA_NOPE + MLA_ROPE) ** -0.5, qcol=lambda g: g, kcol=lambda g: g, vcol=lambda g: g,
                 ocol=lambda g: g, o_width=MLA_HEADS * MLA_V)


def _dil_cfg(S, window, dil):
    return _Attn(T=S // dil, Tk=S // dil, G=dil, nh=DIL_HEADS, rep=1, dqk=HEAD_DIM, dv=HEAD_DIM, tq=BLOCK, tk=BLOCK,
                 mode="band", max_dist=window // dil, scale=HEAD_DIM ** -0.5, qcol=lambda g: g, kcol=lambda g: g,
                 vcol=lambda g: g, ocol=lambda g: g, o_width=dil * DIL_HEADS * HEAD_DIM)


def _cross_fwd(p, x, mem, W, vec):
    S = x.shape[0]
    cfg = _cross_cfg(S, mem.shape[0])
    hx = _rmsnorm(x, vec[p + "x_norm"], name=p + "x_norm")
    qx = _mm(hx, W[p + "w_xq"], mode="nn", name=p + "xq", out_dtype=MXU_DTYPE)
    memn = _rmsnorm(mem, vec[p + "mem_norm"], name=p + "mem_norm")
    kvx = _mm(memn, W[p + "w_xkv"], mode="nn", name=p + "xkv", out_dtype=MXU_DTYPE)
    ox, lse = _attn_fwd(cfg, qx, kvx, kvx, name=p + "x_attn", out_dtype=MXU_DTYPE)
    out = _mm(ox, W[p + "w_xo"], mode="nn", name=p + "xo", res=x)
    return out, (x, hx, qx, memn, kvx, ox, lse)


def _cross_bwd(p, dx, saved, mem, W, vec, dW, dvec):
    x, hx, qx, memn, kvx, ox, lse = saved
    cfg = _cross_cfg(x.shape[0], mem.shape[0])
    dox = _mm(dx, W[p + "w_xo"], mode="nt", name=p + "xo_dx", out_dtype=MXU_DTYPE)
    dW[p + "w_xo"] = _dw(ox, dx, name=p + "xo_dw")
    delta, _ = _attn_delta(cfg, ox, dox, name=p + "x_delta")
    dqx = _attn_dq(cfg, qx, kvx, kvx, dox, lse, delta, name=p + "x_dq", out_dtype=MXU_DTYPE)
    dkx, dvx = _attn_dkv(cfg, qx, kvx, kvx, dox, lse, delta, name=p + "x_dkv", out_dtype=MXU_DTYPE)
    dkvx = jnp.concatenate([dkx, dvx], axis=1)
    dW[p + "w_xq"] = _dw(hx, dqx, name=p + "xq_dw")
    dW[p + "w_xkv"] = _dw(memn, dkvx, name=p + "xkv_dw")
    dmemn = _mm(dkvx, W[p + "w_xkv"], mode="nt", name=p + "xkv_dx")
    _, dvec[p + "mem_norm"] = _rmsnorm_bwd(mem, vec[p + "mem_norm"], dmemn, name=p + "mem_norm_bwd")
    dx_in, dvec[p + "x_norm"] = _dx_norm_bwd(dqx, W[p + "w_xq"], x, vec[p + "x_norm"], dx, name=p + "xq_dx")
    return dx_in


def _ffn_fwd(p, x, W, vec):
    hf = _rmsnorm(x, vec[p + "ffn_norm"], name=p + "ffn_norm")
    gate, up, act = _gate_up(hf, W[p + "w_gate"], W[p + "w_up"], name=p + "gate_up")
    out = _mm(act, W[p + "w_down"], mode="nn", name=p + "down", res=x)
    return out, (x, hf, gate, up, act)


def _ffn_bwd(p, dx, saved, W, vec, dW, dvec):
    x, hf, gate, up, act = saved
    dW[p + "w_down"] = _dw(act, dx, name=p + "down_dw")
    dgate, dup = _gate_up_bwd(dx, W[p + "w_down"], gate, up, name=p + "gate_up_bwd")
    dhf = _mm(dgate, W[p + "w_gate"], mode="nt", name=p + "gate_dx")
    dW[p + "w_gate"] = _dw(hf, dgate, name=p + "gate_dw")
    dW[p + "w_up"] = _dw(hf, dup, name=p + "up_dw")
    dx_in, dvec[p + "ffn_norm"] = _dx_norm_bwd(dup, W[p + "w_up"], x, vec[p + "ffn_norm"], dx, name=p + "up_dx",
                                               res=dhf)
    return dx_in


def _even_fwd(p, x, tabs, W, vec, comm=None):
    S = x.shape[0]
    h = _rmsnorm(x, vec[p + "mix_norm"], name=p + "mix_norm")
    z = _mm(h, W[p + "w_in"], mode="nn", name=p + "in")
    qa, ka, va, cqn, ckvn, kr = _l0_prep(z, tabs, vec[p + "q_norm"], vec[p + "kv_norm"], name=p + "prep")
    sink = jnp.pad(vec[p + "sinks"], (0, LANES - SWA_HEADS)).reshape(1, LANES)
    oa, lse_a = _band_fwd(_swa_cfg(S), qa, ka, va, name=p + "swa", sink=sink, out_dtype=MXU_DTYPE)
    qb = _mm(cqn, W[p + "w_uq"], mode="nn", name=p + "uq")
    kvb = _mm(ckvn, W[p + "w_ukv"], mode="nn", name=p + "ukv")
    Q, K, V = _mla_prep(qb, kvb, kr, tabs, name=p + "mla_prep")
    if comm is None:
        ob, lse_b = _causal_fwd(_mla_cfg(S, MLA_FWD_GROUP), Q, K, V, name=p + "mla", out_dtype=MXU_DTYPE, stat_heads=MLA_BWD_GROUP)
    else:
        ob, lse_b, gathered = _causal_fwd(_mla_cfg(S, MLA_FWD_GROUP), Q, K, V, name=p + "mla", out_dtype=MXU_DTYPE, stat_heads=MLA_BWD_GROUP,
                                          carry=comm.late_weights_exchange())
        W = {**W, **comm.late_weights(gathered)}
    o = jnp.concatenate([oa, ob], axis=1)
    out = _mm(o, W[p + "w_out"], mode="nn", name=p + "out", res=x)
    return out, (x, h, z, qa, ka, va, cqn, ckvn, sink, oa, lse_a, Q, K, V, ob, lse_b, o), W


def _even_bwd(p, dx, saved, tabs, W, vec, dW, dvec, comm=None):
    x, h, z, qa, ka, va, cqn, ckvn, sink, oa, lse_a, Q, K, V, ob, lse_b, o = saved
    S = x.shape[0]
    do = _mm(dx, W[p + "w_out"], mode="nt", name=p + "out_dx", out_dtype=MXU_DTYPE)
    dW[p + "w_out"] = _dw(o, dx, name=p + "out_dw")
    doa, dob = do[:, :SWA_HEADS * HEAD_DIM], do[:, SWA_HEADS * HEAD_DIM:]
    cfg = _swa_cfg(S)
    delta, dsink = _attn_delta(cfg, oa, doa, name=p + "swa_delta", lse=lse_a, sink=sink)
    dvec[p + "sinks"] = dsink
    dqa, dka, dva = _band_bwd(cfg, qa, ka, va, doa, lse_a, delta, name=p + "swa_bwd")
    cfg = _mla_cfg(S, MLA_BWD_GROUP)
    delta, _ = _attn_delta(cfg, ob, dob, name=p + "mla_delta")
    if comm is None:
        dQ, dK, dV = _causal_bwd(cfg, Q, K, V, dob, lse_b, delta, name=p + "mla_bwd")
    else:
        dQ, dK, dV, landed = _causal_bwd(cfg, Q, K, V, dob, lse_b, delta, name=p + "mla_bwd",
                                         carry=comm.late_grads_exchange(dW))
        comm.late_grads_landed(landed)
    dqb, dkvb, dkr = _mla_prep_bwd(dQ, dK, dV, tabs, name=p + "mla_prep_bwd")
    dcqn = _mm(dqb, W[p + "w_uq"], mode="nt", name=p + "uq_dx")
    dW[p + "w_uq"] = _dw(cqn, dqb, name=p + "uq_dw")
    dckvn = _mm(dkvb, W[p + "w_ukv"], mode="nt", name=p + "ukv_dx")
    dW[p + "w_ukv"] = _dw(ckvn, dkvb, name=p + "ukv_dw")
    dz, dvec[p + "q_norm"], dvec[p + "kv_norm"] = _l0_prep_bwd(
        z, tabs, vec[p + "q_norm"], vec[p + "kv_norm"], dqa, dka, dva, dcqn, dckvn, dkr, name=p + "prep_bwd")
    dW[p + "w_in"] = _dw(h, dz, name=p + "in_dw")
    dx_in, dvec[p + "mix_norm"] = _dx_norm_bwd(dz, W[p + "w_in"], x, vec[p + "mix_norm"], dx, name=p + "in_dx")
    return dx_in


def _odd_fwd(p, x, tabs, W, vec):
    S = x.shape[0]
    assert S % (DIL_PATTERNS[-1][1] * BLOCK) == 0, "keys past the end of the sequence are never attended"
    h = _rmsnorm(x, vec[p + "mix_norm"], name=p + "mix_norm")
    qkv = _mm(h, W[p + "w_qkv"], mode="nn", name=p + "qkv")
    qkv_by_d = _l1_prep(qkv, tabs, name=p + "prep")
    outs, lses = {}, {}
    for window, dil in DIL_PATTERNS:
        outs[dil], lses[dil] = _band_fwd(_dil_cfg(S, window, dil), *qkv_by_d[dil], name=p + "dil%d" % dil)
    o, w1, w4, w16 = _merge(outs, lses, name=p + "merge")
    out = _mm(o, W[p + "w_out"], mode="nn", name=p + "out", res=x)
    return out, (x, h, qkv_by_d, lses, dict(zip(DILATIONS, (w1, w4, w16))), o)


def _odd_bwd(p, dx, saved, tabs, W, vec, dW, dvec):
    x, h, qkv_by_d, lses, ws, o = saved
    S = x.shape[0]
    do = _mm(dx, W[p + "w_out"], mode="nt", name=p + "out_dx")
    dW[p + "w_out"] = _dw(o, dx, name=p + "out_dw")
    dos, deltas = _merge_bwd(do, o, ws, name=p + "merge_bwd")
    grads = {}
    for window, dil in DIL_PATTERNS:
        grads[dil] = _band_bwd(_dil_cfg(S, window, dil), *qkv_by_d[dil], dos[dil], lses[dil], deltas[dil],
                               name=p + "dil%d_bwd" % dil)
    dqkv = _l1_prep_bwd(grads, tabs, name=p + "prep_bwd")
    dW[p + "w_qkv"] = _dw(h, dqkv, name=p + "qkv_dw")
    dx_in, dvec[p + "mix_norm"] = _dx_norm_bwd(dqkv, W[p + "w_qkv"], x, vec[p + "mix_norm"], dx, name=p + "qkv_dx")
    return dx_in


def _local_step(x, mem, positions, target, W, vec, comm=None):
    tabs = _rope_tables(positions)
    x1, s_mix0, W = _even_fwd("l0_", x, tabs, W, vec, comm)
    x2, s_x0 = _cross_fwd("l0_", x1, mem, W, vec)
    x3, s_f0 = _ffn_fwd("l0_", x2, W, vec)
    x4, s_mix1 = _odd_fwd("l1_", x3, tabs, W, vec)
    x5, s_x1 = _cross_fwd("l1_", x4, mem, W, vec)
    x6, s_f1 = _ffn_fwd("l1_", x5, W, vec)
    dW, dvec = {}, {}
    dx, dvec["final_norm"], sq = _loss_head(x6, vec["final_norm"], target, name="loss_head")
    dx = _ffn_bwd("l1_", dx, s_f1, W, vec, dW, dvec)
    dx = _cross_bwd("l1_", dx, s_x1, mem, W, vec, dW, dvec)
    dx = _odd_bwd("l1_", dx, s_mix1, tabs, W, vec, dW, dvec)
    dx = _ffn_bwd("l0_", dx, s_f0, W, vec, dW, dvec)
    dx = _cross_bwd("l0_", dx, s_x0, mem, W, vec, dW, dvec)
    dx = _even_bwd("l0_", dx, s_mix0, tabs, W, vec, dW, dvec, comm)
    return sq, dx, dW, dvec


_LAYER_MATS = {
    0: [("w_in", "col"), ("w_uq", "col"), ("w_ukv", "col"), ("w_out", "row"), ("w_xq", "row"), ("w_xkv", "row"),
        ("w_xo", "col"), ("w_gate", "col"), ("w_up", "col"), ("w_down", "row")],
    1: [("w_qkv", "col"), ("w_out", "row"), ("w_xq", "row"), ("w_xkv", "row"), ("w_xo", "col"), ("w_gate", "col"),
        ("w_up", "col"), ("w_down", "row")],
}
MATS = [("l%d_%s" % (l, n), kind) for l in (0, 1) for n, kind in _LAYER_MATS[l]]
_LAYER_VECS = {0: ["mix_norm", "sinks", "q_norm", "kv_norm", "x_norm", "mem_norm", "ffn_norm"],
               1: ["mix_norm", "x_norm", "mem_norm", "ffn_norm"]}
VECS = ["l%d_%s" % (l, n) for l in (0, 1) for n in _LAYER_VECS[l]] + ["final_norm"]
WEIGHT_ORDER = (["l0_mix_norm", "l0_w_in", "l0_sinks", "l0_q_norm", "l0_w_uq", "l0_kv_norm", "l0_w_ukv", "l0_w_out",
                 "l0_x_norm", "l0_mem_norm", "l0_w_xq", "l0_w_xkv", "l0_w_xo", "l0_ffn_norm", "l0_w_gate", "l0_w_up",
                 "l0_w_down", "l1_mix_norm", "l1_w_qkv", "l1_w_out", "l1_x_norm", "l1_mem_norm", "l1_w_xq",
                 "l1_w_xkv", "l1_w_xo", "l1_ffn_norm", "l1_w_gate", "l1_w_up", "l1_w_down", "final_norm"])
PACK_COLS = 1024
PACK_ROW_TILE = 2 * SUM_ROW_TILE
VEC_ROWS = 16
LOSS_ROW = len(VECS)
N_CHIPS = 4


class _Group:
    def __init__(self, mats, shards):
        self.mats, self.shards = mats, shards
        self.layout, off = {}, 0
        for name, _ in mats:
            n = shards[name].size // PACK_COLS
            assert n * PACK_COLS == shards[name].size
            self.layout[name] = (off, n)
            off += n
        self.used = off
        self.rows = -(-off // PACK_ROW_TILE) * PACK_ROW_TILE

    def pack(self, tensors, dtype):
        parts = [tensors[name].astype(dtype).reshape(-1, PACK_COLS) for name, _ in self.mats]
        return jnp.concatenate(parts + [jnp.zeros((self.rows - self.used, PACK_COLS), dtype)], axis=0)

    def unpack(self, packed):
        return {name: packed[off:off + n].reshape(self.shards[name].shape) for name, (off, n) in self.layout.items()}

    def full_weights(self, gathered):
        W = {}
        for name, kind in self.mats:
            off, n = self.layout[name]
            r, cw = self.shards[name].shape
            blocks = gathered[:, off:off + n].reshape(N_CHIPS, r, cw)
            W[name] = blocks.reshape(N_CHIPS * r, cw) if kind == "row" else (
                jnp.transpose(blocks, (1, 0, 2)).reshape(r, N_CHIPS * cw))
        if "l0_w_in" in W:
            W["l0_w_in"] = jnp.pad(W["l0_w_in"], ((0, 0), (0, Z_END - W["l0_w_in"].shape[1])))
        if "l0_w_uq" in W:
            uq = W["l0_w_uq"].reshape(MLA_Q_RANK, MLA_HEADS, MLA_NOPE + MLA_ROPE)
            uq = jnp.pad(uq, ((0, 0), (0, 0), (0, LANES - MLA_NOPE - MLA_ROPE)))
            W["l0_w_uq"] = uq.reshape(MLA_Q_RANK, MLA_HEADS * LANES)
        return W

    def pack_grads(self, dW):
        parts = []
        for name, kind in self.mats:
            r, cw = self.shards[name].shape
            g = dW[name]
            if name == "l0_w_in":
                g = g[:, :Z_KR + MLA_ROPE]
            if name == "l0_w_uq":
                g = g.reshape(MLA_Q_RANK, MLA_HEADS, LANES)[:, :, :MLA_NOPE + MLA_ROPE].reshape(MLA_Q_RANK, -1)
            if kind == "col":
                g = jnp.transpose(g.reshape(r, N_CHIPS, cw), (1, 0, 2))
            parts.append(g.reshape(N_CHIPS, -1, PACK_COLS).astype(EXCHANGE_DTYPE))
        pad = jnp.zeros((N_CHIPS, self.rows - self.used, PACK_COLS), EXCHANGE_DTYPE)
        return jnp.concatenate(parts + [pad], axis=1)


def _pack_vecs(vecs):
    rows = [jnp.pad(vecs[n].reshape(-1).astype(F32), (0, PACK_COLS - vecs[n].size)) for n in VECS]
    rows += [jnp.zeros((PACK_COLS,), F32)] * (VEC_ROWS - len(rows))
    return jnp.stack(rows)


def _unpack_vecs(packed, like):
    return {n: packed[i, :like[n].size].reshape(like[n].shape) for i, n in enumerate(VECS)}


EARLY_MATS = [m for m in MATS if m[0] in ("l0_w_in", "l0_w_uq", "l0_w_ukv")]
LATE_MATS = [m for m in MATS if m not in EARLY_MATS]


class _StepComm:
    def __init__(self, shards):
        self.early, self.late = _Group(EARLY_MATS, shards), _Group(LATE_MATS, shards)
        self.half_index = lax.axis_index("c").astype(jnp.int32).reshape(1)
        self.late_grads = None

    def early_weights(self):
        src = self.early.pack(self.early.shards, MXU_DTYPE)
        return self.early.full_weights(_run_exchange(_gather_exchange(src), name="gather_early"))

    def late_weights_exchange(self):
        return _gather_exchange(self.late.pack(self.late.shards, MXU_DTYPE))

    def late_weights(self, gathered):
        return self.late.full_weights(gathered)

    def _chip_sum(self, group, dW, tag):
        grads = group.pack_grads(dW)
        theirs = _swap_other_half(grads, name="swap_other_half_" + tag)
        return _sum_cores(grads, theirs, self.half_index, name="sum_cores_" + tag)

    def _finish(self, parts, tag):
        return _join_halves(_sum_chips(parts, self.half_index, name="sum_chips_" + tag), name="join_halves_" + tag)

    def late_grads_exchange(self, dW):
        return _scatter_exchange(self._chip_sum(self.late, dW, "late"))

    def late_grads_landed(self, parts):
        self.late_grads = self._finish(parts, "late")

    def early_grads(self, dW):
        parts = _run_exchange(_scatter_exchange(self._chip_sum(self.early, dW, "early")), name="scatter_early")
        return self._finish(parts, "early")


def _step(a):
    weights = {n: a[n] for n in WEIGHT_ORDER}
    shards = {n: weights[n] for n, _ in MATS}
    vec = {n: weights[n] for n in VECS}
    comm = _StepComm(shards)
    sq, grad_x, dW, dvec = _local_step(a["x"][0], a["mem"][0], a["positions"], a["loss_target"][0],
                                       comm.early_weights(), vec, comm)

    dvec = dict(dvec)
    dvec["l0_sinks"] = dvec["l0_sinks"][0, :SWA_HEADS]
    small = _pack_vecs(dvec)
    small = small.at[LOSS_ROW, 0].set(0.5 / a["x"].shape[-1] * jnp.sum(sq))
    small = _allreduce_small(small, name="reduce_gains")
    loss = small[LOSS_ROW, 0]
    g_s = small.at[LOSS_ROW, 0].set(0.0)
    d_s, m_s, v_s = _adamw(_pack_vecs(vec), g_s, _pack_vecs({n: a["m_" + n] for n in VECS}),
                           _pack_vecs({n: a["v_" + n] for n in VECS}), name="adamw_gains")
    got = [_unpack_vecs(packed, vec) for packed in (g_s, d_s, m_s, v_s)]

    for group, g_w in ((comm.late, comm.late_grads), (comm.early, comm.early_grads(dW))):
        for n, g in group.unpack(g_w).items():
            results = (g,) + tuple(_adamw(shards[n], g, a["m_" + n], a["v_" + n], name="adamw_" + n))
            for kind, value in zip(got, results):
                kind[n] = value

    out = [loss, grad_x[None]]
    for kind in got:
        out += [kind[n] for n in WEIGHT_ORDER]
    return tuple(out)


def kernel(x, mem, positions, l0_mix_norm, l0_w_in, l0_sinks, l0_q_norm, l0_w_uq, l0_kv_norm, l0_w_ukv, l0_w_out, l0_x_norm, l0_mem_norm, l0_w_xq, l0_w_xkv, l0_w_xo, l0_ffn_norm, l0_w_gate, l0_w_up, l0_w_down, l1_mix_norm, l1_w_qkv, l1_w_out, l1_x_norm, l1_mem_norm, l1_w_xq, l1_w_xkv, l1_w_xo, l1_ffn_norm, l1_w_gate, l1_w_up, l1_w_down, final_norm, loss_target, m_l0_mix_norm, m_l0_w_in, m_l0_sinks, m_l0_q_norm, m_l0_w_uq, m_l0_kv_norm, m_l0_w_ukv, m_l0_w_out, m_l0_x_norm, m_l0_mem_norm, m_l0_w_xq, m_l0_w_xkv, m_l0_w_xo, m_l0_ffn_norm, m_l0_w_gate, m_l0_w_up, m_l0_w_down, m_l1_mix_norm, m_l1_w_qkv, m_l1_w_out, m_l1_x_norm, m_l1_mem_norm, m_l1_w_xq, m_l1_w_xkv, m_l1_w_xo, m_l1_ffn_norm, m_l1_w_gate, m_l1_w_up, m_l1_w_down, m_final_norm, v_l0_mix_norm, v_l0_w_in, v_l0_sinks, v_l0_q_norm, v_l0_w_uq, v_l0_kv_norm, v_l0_w_ukv, v_l0_w_out, v_l0_x_norm, v_l0_mem_norm, v_l0_w_xq, v_l0_w_xkv, v_l0_w_xo, v_l0_ffn_norm, v_l0_w_gate, v_l0_w_up, v_l0_w_down, v_l1_mix_norm, v_l1_w_qkv, v_l1_w_out, v_l1_x_norm, v_l1_mem_norm, v_l1_w_xq, v_l1_w_xkv, v_l1_w_xo, v_l1_ffn_norm, v_l1_w_gate, v_l1_w_up, v_l1_w_down, v_final_norm):
    return _step(dict(locals()))
```

```python
import functools

import jax
import jax.numpy as jnp
import numpy as np
from jax import lax
from jax.experimental import pallas as pl
from jax.experimental.pallas import tpu as pltpu

F32 = jnp.float32
MXU_DTYPE = jnp.bfloat16
LANES = 128
VMEM_LIMIT_BYTES = 56 * 1024 * 1024

NORM_EPS = 1e-6
ROPE_THETA = 10000.0
BLOCK = 128
HEAD_DIM = 64
SWA_HEADS, SWA_KV_HEADS, SWA_WINDOW = 8, 2, 128
MLA_HEADS, MLA_Q_RANK, MLA_KV_RANK, MLA_NOPE, MLA_ROPE, MLA_V = 8, 384, 256, 64, 32, 64
DIL_HEADS = 16
DIL_PATTERNS = ((128, 1), (512, 4), (2048, 16))
X_HEADS, X_HEAD_DIM = 4, 128
ADAM_LR, ADAM_B1, ADAM_B2, ADAM_EPS, ADAM_WD, ADAM_STEP = 0.001, 0.9, 0.999, 1e-08, 0.01, 10
MESH = pl.DeviceIdType.MESH
NEG_BIG = -1e30

NN = (((1,), (0,)), ((), ()))
NT = (((1,), (1,)), ((), ()))


def _dot(a, b, dims=NN):
    return lax.dot_general(a.astype(MXU_DTYPE), b.astype(MXU_DTYPE), dims, preferred_element_type=F32)


def _pcall(body, *, name, dims=None, **kw):
    params = pltpu.CompilerParams(dimension_semantics=dims, vmem_limit_bytes=VMEM_LIMIT_BYTES)
    return pl.pallas_call(body, name=name, compiler_params=params, **kw)


def _tile(n, pref):
    t = (min(pref, n) // LANES) * LANES
    while t >= LANES:
        if n % t == 0:
            return t
        t -= LANES
    return n


SUBLANES_PACKED = 16


def _row_tile(n, pref):
    t = (min(pref, n) // SUBLANES_PACKED) * SUBLANES_PACKED
    while t >= SUBLANES_PACKED:
        if n % t == 0:
            return t
        t -= SUBLANES_PACKED
    return n


def _lane(shape):
    return lax.broadcasted_iota(jnp.int32, shape, 1)


def _cols_to_lanes(cols, rows):
    lane = _lane((rows, LANES))
    out = jnp.zeros((rows, LANES), F32)
    for j, col in enumerate(cols):
        out = jnp.where(lane == j, col, out)
    return out


def _mm(a, b, *, mode, name, res=None, out_dtype=F32, tm=1408, tn=1536, tk=1408):
    if mode == "nn":
        (M, K), (K2, N) = a.shape, b.shape
    elif mode == "nt":
        (M, K), (N, K2) = a.shape, b.shape
    else:
        (K, M), (K2, N) = a.shape, b.shape
    assert K == K2, (a.shape, b.shape, mode)
    tm, tn, tk = _tile(M, tm), _tile(N, tn), _tile(K, tk)
    nk = K // tk
    in_place = out_dtype == F32 or nk == 1

    def body(*refs):
        refs = list(refs)
        a_ref, b_ref = refs[:2]
        r_ref = refs[2] if res is not None else None
        o_ref = refs[3 if res is not None else 2]
        acc = o_ref if in_place else refs[-1]
        k = pl.program_id(2)
        if mode == "nn":
            part = _dot(a_ref[...], b_ref[...], NN)
        elif mode == "nt":
            part = _dot(a_ref[...], b_ref[...], NT)
        else:
            part = _dot(a_ref[...].T, b_ref[...], NN)
        if nk == 1:
            o_ref[...] = (part if res is None else part + r_ref[...].astype(F32)).astype(o_ref.dtype)
            return

        @pl.when(k == 0)
        def _():
            acc[...] = part if res is None else part + r_ref[...].astype(F32)

        @pl.when(k > 0)
        def _():
            acc[...] += part

        if not in_place:
            @pl.when(k == nk - 1)
            def _():
                o_ref[...] = acc[...].astype(o_ref.dtype)

    if mode == "nn":
        a_spec = pl.BlockSpec((tm, tk), lambda i, j, k: (i, k))
        b_spec = pl.BlockSpec((tk, tn), lambda i, j, k: (k, j))
    elif mode == "nt":
        a_spec = pl.BlockSpec((tm, tk), lambda i, j, k: (i, k))
        b_spec = pl.BlockSpec((tn, tk), lambda i, j, k: (j, k))
    else:
        a_spec = pl.BlockSpec((tk, tm), lambda i, j, k: (k, i))
        b_spec = pl.BlockSpec((tk, tn), lambda i, j, k: (k, j))
    o_spec = pl.BlockSpec((tm, tn), lambda i, j, k: (i, j))
    in_specs = [a_spec, b_spec] + ([] if res is None else [o_spec])
    args = (a, b) + (() if res is None else (res,))
    return _pcall(
        body, name=name, dims=("parallel", "parallel", "arbitrary"),
        grid=(M // tm, N // tn, nk), in_specs=in_specs, out_specs=o_spec,
        out_shape=jax.ShapeDtypeStruct((M, N), out_dtype),
        scratch_shapes=[] if in_place else [pltpu.VMEM((tm, tn), F32)],
    )(*args)


EXCHANGE_DTYPE = jnp.bfloat16


def _dw(a, b, *, name):
    return _mm(a, b, mode="tn", name=name, out_dtype=EXCHANGE_DTYPE)


def _rms_parts(xf):
    r = lax.rsqrt(jnp.mean(xf * xf, axis=-1, keepdims=True) + NORM_EPS)
    return xf * r, r


def _rms_bwd_rows(xf, g, dy):
    xhat, r = _rms_parts(xf)
    dxhat = dy * g
    dx = r * (dxhat - xhat * jnp.mean(dxhat * xhat, axis=-1, keepdims=True))
    return dx, dy * xhat


def _dx_norm_bwd(a, w, x, g, dres, *, name, res=None, tm=1024, tk=1408):
    (M, K), N = a.shape, w.shape[0]
    tm, tk = _tile(M, tm), _tile(K, tk)
    nk = K // tk

    def body(*refs):
        refs = list(refs)
        a_ref, w_ref, x_ref, g_ref, dr_ref = refs[:5]
        r_ref = refs[5] if res is not None else None
        dx_ref, dg_ref = refs[-2:]
        i, k = pl.program_id(0), pl.program_id(1)
        part = _dot(a_ref[...], w_ref[...], NT)

        @pl.when(k == 0)
        def _():
            dx_ref[...] = part if res is None else part + r_ref[...]

        @pl.when(k > 0)
        def _():
            dx_ref[...] += part

        @pl.when(k == nk - 1)
        def _():
            dx, dgp = _rms_bwd_rows(x_ref[...], g_ref[...], dx_ref[...])
            dx_ref[...] = dx + dr_ref[...]

            @pl.when(i == 0)
            def _():
                dg_ref[...] = jnp.zeros_like(dg_ref)

            dg_ref[...] += jnp.sum(dgp, axis=0, keepdims=True)

    row = pl.BlockSpec((tm, N), lambda i, k: (i, 0))
    vec = pl.BlockSpec((1, N), lambda i, k: (0, 0))
    in_specs = [pl.BlockSpec((tm, tk), lambda i, k: (i, k)), pl.BlockSpec((N, tk), lambda i, k: (0, k)), row, vec, row]
    args = [a, w, x, g.reshape(1, N), dres]
    if res is not None:
        in_specs.append(row)
        args.append(res)
    return _pcall(
        body, name=name, dims=("arbitrary", "arbitrary"), grid=(M // tm, nk), in_specs=in_specs,
        out_specs=[row, vec], out_shape=[_sds((M, N)), _sds((1, N))],
    )(*args)


def _rmsnorm(x, g, *, name, out_dtype=MXU_DTYPE, tm=512):
    M, D = x.shape
    tm = _tile(M, tm)

    def body(x_ref, g_ref, o_ref):
        xhat, _ = _rms_parts(x_ref[...].astype(F32))
        o_ref[...] = (xhat * g_ref[...]).astype(o_ref.dtype)

    return _pcall(
        body, name=name, dims=("parallel",), grid=(M // tm,),
        in_specs=[pl.BlockSpec((tm, D), lambda i: (i, 0)), pl.BlockSpec((1, D), lambda i: (0, 0))],
        out_specs=pl.BlockSpec((tm, D), lambda i: (i, 0)),
        out_shape=jax.ShapeDtypeStruct((M, D), out_dtype),
    )(x, g.reshape(1, D))


def _rmsnorm_bwd(x, g, dy, *, name, dres=None, tm=512):
    M, D = x.shape
    tm = _tile(M, tm)

    def body(*refs):
        if dres is None:
            x_ref, g_ref, dy_ref, dx_ref, dg_ref = refs
        else:
            x_ref, g_ref, dy_ref, dr_ref, dx_ref, dg_ref = refs
        dx, dgp = _rms_bwd_rows(x_ref[...].astype(F32), g_ref[...], dy_ref[...].astype(F32))
        if dres is not None:
            dx = dx + dr_ref[...]
        dx_ref[...] = dx

        @pl.when(pl.program_id(0) == 0)
        def _():
            dg_ref[...] = jnp.zeros_like(dg_ref)

        dg_ref[...] += jnp.sum(dgp, axis=0, keepdims=True)

    row = pl.BlockSpec((tm, D), lambda i: (i, 0))
    vec = pl.BlockSpec((1, D), lambda i: (0, 0))
    in_specs = [row, vec, row] + ([] if dres is None else [row])
    args = (x, g.reshape(1, D), dy) + (() if dres is None else (dres,))
    return _pcall(
        body, name=name, dims=("arbitrary",), grid=(M // tm,), in_specs=in_specs, out_specs=[row, vec],
        out_shape=[jax.ShapeDtypeStruct((M, D), F32), jax.ShapeDtypeStruct((1, D), F32)],
    )(*args)


def _rope_chunk(t, c, s, half):
    lane = _lane(t.shape)
    swapped = jnp.where((lane % (2 * half)) < half, pltpu.roll(t, LANES - half, 1), pltpu.roll(t, half, 1))
    return t * c + swapped * s


def _rope_tables(positions):
    pos = positions.reshape(-1).astype(F32)[:, None]

    def table(dh, first, copies, sine, fill=0.0):
        half = dh // 2
        lane = np.arange(LANES)
        inside = (lane >= first) & (lane < first + copies * dh)
        idx = np.where(inside, (lane - first) % half, 0)
        inv_freq = ROPE_THETA ** (-jnp.asarray(2 * idx, F32) / dh)
        sign = np.where((lane - first) % dh < half, -1.0, 1.0) if sine else np.ones(LANES)
        ang = pos * inv_freq[None, :]
        val = (jnp.sin(ang) if sine else jnp.cos(ang)) * jnp.asarray(sign, F32)[None, :]
        return jnp.where(jnp.asarray(inside)[None, :], val, fill)

    return dict(
        c64=table(HEAD_DIM, 0, 2, False), s64=table(HEAD_DIM, 0, 2, True),
        ck=table(MLA_ROPE, 0, 1, False), sk=table(MLA_ROPE, 0, 1, True),
        cm=jnp.where(jnp.asarray(np.arange(LANES) < MLA_NOPE)[None, :], 1.0, table(MLA_ROPE, MLA_NOPE, 1, False)),
        sm=table(MLA_ROPE, MLA_NOPE, 1, True),
    )


def _attn_steps(mode, n_other, t_self, t_other):
    if mode == "band":
        assert t_self == t_other
        return 2
    return n_other


def _kv_block(mode, qi, kj):
    if mode == "band":
        return jnp.maximum(qi - 1 + kj, 0), (qi + kj) >= 1
    if mode == "causal":
        return jnp.minimum(kj, qi), kj <= qi
    return kj, None


def _q_block(mode, ki, qj, nq):
    if mode == "band":
        return jnp.minimum(ki + qj, nq - 1), (ki + qj) <= nq - 1
    if mode == "causal":
        return jnp.maximum(qj, ki), qj >= ki
    return qj, None


def _mask(mode, max_dist, qpos, kpos):
    d = qpos - kpos
    if mode == "band":
        return (d >= 0) & (d <= max_dist)
    if mode == "causal":
        return d >= 0
    return None


def _when(cond, fn):
    if cond is None:
        fn()
    else:
        pl.when(cond)(fn)


class _Attn:
    def __init__(self, *, T, Tk, G, nh, rep, dqk, dv, tq, tk, mode, scale, qcol, kcol, vcol, ocol, o_width,
                 max_dist=0):
        self.__dict__.update(locals())
        self.nkv = nh // rep
        assert T % tq == 0 and Tk % tk == 0 and nh <= LANES


def _attn_fwd(cfg, q, k, v, *, name, sink=None, out_dtype=F32):
    c = cfg
    nq, nk = c.T // c.tq, c.Tk // c.tk
    steps = _attn_steps(c.mode, nk, c.tq, c.tk)

    def body(*refs):
        if sink is None:
            q_ref, k_ref, v_ref, o_ref, lse_ref, m_scr, l_scr, acc = refs
        else:
            q_ref, k_ref, v_ref, sink_ref, o_ref, lse_ref, m_scr, l_scr, acc = refs
        qi, kj = pl.program_id(1), pl.program_id(2)
        kb, valid = _kv_block(c.mode, qi, kj)

        @pl.when(kj == 0)
        def _():
            if sink is None:
                m_scr[...] = jnp.full_like(m_scr, NEG_BIG)
                l_scr[...] = jnp.zeros_like(l_scr)
            else:
                m_scr[...] = jnp.broadcast_to(sink_ref[...], m_scr.shape)
                l_scr[...] = jnp.ones_like(l_scr)
            acc[...] = jnp.zeros_like(acc)

        def step():
            qpos = qi * c.tq + lax.broadcasted_iota(jnp.int32, (c.tq, c.tk), 0)
            kpos = kb * c.tk + lax.broadcasted_iota(jnp.int32, (c.tq, c.tk), 1)
            mask = _mask(c.mode, c.max_dist, qpos, kpos)
            for j in range(c.nh):
                g = j // c.rep
                s = _dot(q_ref[:, j * c.dqk:(j + 1) * c.dqk], k_ref[:, g * c.dqk:(g + 1) * c.dqk], NT) * c.scale
                if mask is not None:
                    s = jnp.where(mask, s, -jnp.inf)
                m_prev = m_scr[:, j:j + 1]
                m_new = jnp.maximum(m_prev, jnp.max(s, axis=1, keepdims=True))
                alpha = jnp.exp(m_prev - m_new)
                p = jnp.exp(s - m_new)
                l_scr[:, j:j + 1] = alpha * l_scr[:, j:j + 1] + jnp.sum(p, axis=1, keepdims=True)
                acc[:, j * c.dv:(j + 1) * c.dv] = (
                    alpha * acc[:, j * c.dv:(j + 1) * c.dv] + _dot(p, v_ref[:, g * c.dv:(g + 1) * c.dv], NN))
                m_scr[:, j:j + 1] = m_new

        _when(valid, step)

        @pl.when(kj == steps - 1)
        def _():
            for j in range(c.nh):
                o_ref[:, j * c.dv:(j + 1) * c.dv] = (
                    acc[:, j * c.dv:(j + 1) * c.dv] / l_scr[:, j:j + 1]).astype(o_ref.dtype)
            lane = _lane((c.tq, LANES))
            lse_ref[...] = jnp.where(lane < c.nh, m_scr[...] + jnp.log(jnp.maximum(l_scr[...], 1e-37)), 0.0)

    in_specs = [
        pl.BlockSpec((c.tq, c.nh * c.dqk), lambda g, i, j: (i, c.qcol(g))),
        pl.BlockSpec((c.tk, c.nkv * c.dqk), lambda g, i, j: (_kv_block(c.mode, i, j)[0], c.kcol(g))),
        pl.BlockSpec((c.tk, c.nkv * c.dv), lambda g, i, j: (_kv_block(c.mode, i, j)[0], c.vcol(g))),
    ]
    args = [q, k, v]
    if sink is not None:
        in_specs.append(pl.BlockSpec((1, LANES), lambda g, i, j: (0, 0)))
        args.append(sink)
    return _pcall(
        body, name=name, dims=("parallel", "parallel", "arbitrary"), grid=(c.G, nq, steps),
        in_specs=in_specs,
        out_specs=[pl.BlockSpec((c.tq, c.nh * c.dv), lambda g, i, j: (i, c.ocol(g))),
                   pl.BlockSpec((c.tq, LANES), lambda g, i, j: (i, g))],
        out_shape=[jax.ShapeDtypeStruct((c.T, c.o_width), out_dtype),
                   jax.ShapeDtypeStruct((c.T, LANES * c.G), F32)],
        scratch_shapes=[pltpu.VMEM((c.tq, LANES), F32), pltpu.VMEM((c.tq, LANES), F32),
                        pltpu.VMEM((c.tq, c.nh * c.dv), F32)],
    )(*args)


def _attn_delta(cfg, o, do, *, name, w=None, lse=None, sink=None, tm=512):
    c = cfg
    tm = _tile(c.T, tm)
    width = c.nh * c.dv

    def body(*refs):
        refs = list(refs)
        o_ref, do_ref = refs[:2]
        rest = refs[2:]
        w_ref = rest.pop(0) if w is not None else None
        lse_ref, sink_ref = (rest.pop(0), rest.pop(0)) if sink is not None else (None, None)
        d_ref = rest.pop(0)
        prod = o_ref[...].astype(F32) * do_ref[...].astype(F32)
        cols = [jnp.sum(prod[:, j * c.dv:(j + 1) * c.dv], axis=1, keepdims=True) for j in range(c.nh)]
        delta = _cols_to_lanes(cols, tm)
        if w is not None:
            delta = delta * w_ref[...]
        d_ref[...] = delta
        if sink is not None:
            ds_ref = rest.pop(0)

            @pl.when(pl.program_id(1) == 0)
            def _():
                ds_ref[...] = jnp.zeros_like(ds_ref)

            lane = _lane((tm, LANES))
            ps = jnp.where(lane < c.nh, jnp.exp(sink_ref[...] - lse_ref[...]), 0.0)
            ds_ref[...] -= jnp.sum(ps * delta, axis=0, keepdims=True)

    stat = pl.BlockSpec((tm, LANES), lambda g, i: (i, g))
    in_specs = [pl.BlockSpec((tm, width), lambda g, i: (i, c.ocol(g)))] * 2
    args = [o, do]
    out_specs, out_shape = [stat], [jax.ShapeDtypeStruct((c.T, LANES * c.G), F32)]
    if w is not None:
        in_specs.append(stat)
        args.append(w)
    if sink is not None:
        assert c.G == 1
        in_specs += [stat, pl.BlockSpec((1, LANES), lambda g, i: (0, 0))]
        args += [lse, sink]
        out_specs.append(pl.BlockSpec((1, LANES), lambda g, i: (0, 0)))
        out_shape.append(jax.ShapeDtypeStruct((1, LANES), F32))
    out = _pcall(
        body, name=name, dims=("arbitrary", "arbitrary"), grid=(c.G, c.T // tm),
        in_specs=in_specs, out_specs=out_specs, out_shape=out_shape,
    )(*args)
    return out if sink is not None else (out[0], None)


def _attn_dq(cfg, q, k, v, do, lse, delta, *, name, init=None, out_dtype=F32):
    c = cfg
    nq, nk = c.T // c.tq, c.Tk // c.tk
    steps = _attn_steps(c.mode, nk, c.tq, c.tk)
    qw = c.nh * c.dqk

    def body(*refs):
        if init is None:
            q_ref, k_ref, v_ref, do_ref, lse_ref, d_ref, dq_ref, acc = refs
        else:
            q_ref, k_ref, v_ref, do_ref, lse_ref, d_ref, init_ref, dq_ref, acc = refs
        qi, kj = pl.program_id(1), pl.program_id(2)
        kb, valid = _kv_block(c.mode, qi, kj)

        @pl.when(kj == 0)
        def _():
            acc[...] = jnp.zeros_like(acc) if init is None else init_ref[...].astype(F32)

        def step():
            qpos = qi * c.tq + lax.broadcasted_iota(jnp.int32, (c.tq, c.tk), 0)
            kpos = kb * c.tk + lax.broadcasted_iota(jnp.int32, (c.tq, c.tk), 1)
            mask = _mask(c.mode, c.max_dist, qpos, kpos)
            for j in range(c.nh):
                g = j // c.rep
                kh = k_ref[:, g * c.dqk:(g + 1) * c.dqk]
                s = _dot(q_ref[:, j * c.dqk:(j + 1) * c.dqk], kh, NT) * c.scale
                if mask is not None:
                    s = jnp.where(mask, s, -jnp.inf)
                p = jnp.exp(s - lse_ref[:, j:j + 1])
                dp = _dot(do_ref[:, j * c.dv:(j + 1) * c.dv], v_ref[:, g * c.dv:(g + 1) * c.dv], NT)
                ds = p * (dp - d_ref[:, j:j + 1]) * c.scale
                acc[:, j * c.dqk:(j + 1) * c.dqk] += _dot(ds, kh, NN)

        _when(valid, step)

        @pl.when(kj == steps - 1)
        def _():
            dq_ref[...] = acc[...].astype(dq_ref.dtype)

    kvb = lambda i, j: _kv_block(c.mode, i, j)[0]
    qspec = pl.BlockSpec((c.tq, qw), lambda g, i, j: (i, c.qcol(g)))
    stat = pl.BlockSpec((c.tq, LANES), lambda g, i, j: (i, g))
    in_specs = [
        qspec,
        pl.BlockSpec((c.tk, c.nkv * c.dqk), lambda g, i, j: (kvb(i, j), c.kcol(g))),
        pl.BlockSpec((c.tk, c.nkv * c.dv), lambda g, i, j: (kvb(i, j), c.vcol(g))),
        pl.BlockSpec((c.tq, c.nh * c.dv), lambda g, i, j: (i, c.ocol(g))),
        stat, stat,
    ]
    args = [q, k, v, do, lse, delta]
    dq_spec = pl.BlockSpec((c.tq, qw), lambda g, i, j: (i, g))
    if init is not None:
        in_specs.append(dq_spec)
        args.append(init)
    return _pcall(
        body, name=name, dims=("parallel", "parallel", "arbitrary"), grid=(c.G, nq, steps),
        in_specs=in_specs, out_specs=dq_spec,
        out_shape=jax.ShapeDtypeStruct((c.T, c.G * qw), out_dtype),
        scratch_shapes=[pltpu.VMEM((c.tq, qw), F32)],
    )(*args)


def _attn_dkv(cfg, q, k, v, do, lse, delta, *, name, init=None, out_dtype=F32):
    c = cfg
    nq, nk = c.T // c.tq, c.Tk // c.tk
    steps = _attn_steps(c.mode, nq, c.tk, c.tq)
    kw, vw = c.nkv * c.dqk, c.nkv * c.dv

    def body(*refs):
        if init is None:
            q_ref, k_ref, v_ref, do_ref, lse_ref, d_ref, dk_ref, dv_ref, dk_acc, dv_acc = refs
        else:
            q_ref, k_ref, v_ref, do_ref, lse_ref, d_ref, ik_ref, iv_ref, dk_ref, dv_ref, dk_acc, dv_acc = refs
        ki, qj = pl.program_id(1), pl.program_id(2)
        qb, valid = _q_block(c.mode, ki, qj, nq)

        @pl.when(qj == 0)
        def _():
            dk_acc[...] = jnp.zeros_like(dk_acc) if init is None else ik_ref[...].astype(F32)
            dv_acc[...] = jnp.zeros_like(dv_acc) if init is None else iv_ref[...].astype(F32)

        def step():
            kpos = ki * c.tk + lax.broadcasted_iota(jnp.int32, (c.tk, c.tq), 0)
            qpos = qb * c.tq + lax.broadcasted_iota(jnp.int32, (c.tk, c.tq), 1)
            mask = _mask(c.mode, c.max_dist, qpos, kpos)
            lse_t = lse_ref[...].T
            d_t = d_ref[...].T
            for j in range(c.nh):
                g = j // c.rep
                qh = q_ref[:, j * c.dqk:(j + 1) * c.dqk]
                doh = do_ref[:, j * c.dv:(j + 1) * c.dv]
                s_t = _dot(k_ref[:, g * c.dqk:(g + 1) * c.dqk], qh, NT) * c.scale
                if mask is not None:
                    s_t = jnp.where(mask, s_t, -jnp.inf)
                p_t = jnp.exp(s_t - lse_t[j:j + 1, :])
                dv_acc[:, g * c.dv:(g + 1) * c.dv] += _dot(p_t, doh, NN)
                dp_t = _dot(v_ref[:, g * c.dv:(g + 1) * c.dv], doh, NT)
                ds_t = p_t * (dp_t - d_t[j:j + 1, :]) * c.scale
                dk_acc[:, g * c.dqk:(g + 1) * c.dqk] += _dot(ds_t, qh, NN)

        _when(valid, step)

        @pl.when(qj == steps - 1)
        def _():
            dk_ref[...] = dk_acc[...].astype(dk_ref.dtype)
            dv_ref[...] = dv_acc[...].astype(dv_ref.dtype)

    qbi = lambda i, j: _q_block(c.mode, i, j, nq)[0]
    stat = pl.BlockSpec((c.tq, LANES), lambda g, i, j: (qbi(i, j), g))
    in_specs = [
        pl.BlockSpec((c.tq, c.nh * c.dqk), lambda g, i, j: (qbi(i, j), c.qcol(g))),
        pl.BlockSpec((c.tk, kw), lambda g, i, j: (i, c.kcol(g))),
        pl.BlockSpec((c.tk, vw), lambda g, i, j: (i, c.vcol(g))),
        pl.BlockSpec((c.tq, c.nh * c.dv), lambda g, i, j: (qbi(i, j), c.ocol(g))),
        stat, stat,
    ]
    args = [q, k, v, do, lse, delta]
    dk_spec = pl.BlockSpec((c.tk, kw), lambda g, i, j: (i, g))
    dv_spec = pl.BlockSpec((c.tk, vw), lambda g, i, j: (i, g))
    if init is not None:
        in_specs += [dk_spec, dv_spec]
        args += list(init)
    return _pcall(
        body, name=name, dims=("parallel", "parallel", "arbitrary"), grid=(c.G, nk, steps),
        in_specs=in_specs, out_specs=[dk_spec, dv_spec],
        out_shape=[jax.ShapeDtypeStruct((c.Tk, c.G * kw), out_dtype),
                   jax.ShapeDtypeStruct((c.Tk, c.G * vw), out_dtype)],
        scratch_shapes=[pltpu.VMEM((c.tk, kw), F32), pltpu.VMEM((c.tk, vw), F32)],
    )(*args)


TN = (((0,), (0,)), ((), ()))


def _band_mask(c, i):
    key = lax.broadcasted_iota(jnp.int32, (2 * BLOCK, BLOCK), 0)
    qry = lax.broadcasted_iota(jnp.int32, (2 * BLOCK, BLOCK), 1)
    d = BLOCK + qry - key
    return (d >= 0) & (d <= c.max_dist) & ((key >= BLOCK) | (i > 0))


def _head_pairs(c):
    return c.rep == 1 and c.dqk == c.dv == LANES // 2 and c.nh % 2 == 0


def _block_diagonal(pair):
    lane = _lane(pair.shape)
    zero = jnp.zeros_like(pair)
    return jnp.concatenate([jnp.where(lane < LANES // 2, pair, zero), jnp.where(lane >= LANES // 2, pair, zero)], axis=0)


def _own_blocks(t):
    n = t.shape[1] // 2
    rows = lax.broadcasted_iota(jnp.int32, (LANES, n), 0)
    return jnp.where(rows < LANES // 2, t[:, :n], t[:, n:])


def _rows_to_stats(rows, n):
    return jnp.concatenate(rows + [jnp.zeros((LANES - len(rows), n), F32)], axis=0).T


def _band_fwd(cfg, q, k, v, *, name, sink=None, out_dtype=F32):
    c = cfg
    assert c.mode == "band" and c.tq == c.tk == BLOCK and c.T == c.Tk
    nq = c.T // BLOCK

    def body(*refs):
        if sink is None:
            q_ref, kp_ref, kc_ref, vp_ref, vc_ref, o_ref, lse_ref = refs
        else:
            q_ref, kp_ref, kc_ref, vp_ref, vc_ref, sink_ref, o_ref, lse_ref = refs
        mask = _band_mask(c, pl.program_id(1))
        k2 = jnp.concatenate([kp_ref[...], kc_ref[...]], axis=0)
        v2 = jnp.concatenate([vp_ref[...], vc_ref[...]], axis=0)
        lses = []
        if _head_pairs(c):
            mask2 = jnp.concatenate([mask, mask], axis=1)
            pair_lanes = [slice(pc * LANES, (pc + 1) * LANES) for pc in range(c.nh // 2)]
            score = lambda sl: _dot(k2[:, sl], _block_diagonal(q_ref[:, sl]), NT)
            ahead, behind = score(pair_lanes[0]), None

            def finish(entry):
                sl, o_t, l = entry
                o_ref[:, sl] = _own_blocks(o_t / l).T.astype(o_ref.dtype)

            for pc, sl in enumerate(pair_lanes):
                s = ahead * c.scale
                if pc + 1 < len(pair_lanes):
                    ahead = score(pair_lanes[pc + 1])
                s = jnp.where(mask2, s, -jnp.inf)
                m = jnp.max(s, axis=0, keepdims=True)
                p = jnp.exp(s - m)
                l = jnp.sum(p, axis=0, keepdims=True)
                if behind is not None:
                    finish(behind)
                behind = (sl, _dot(v2[:, sl], p, TN), l)
                lse = m + jnp.log(l)
                lses += [lse[:, :BLOCK], lse[:, BLOCK:]]
            finish(behind)
        heads = [] if _head_pairs(c) else list(range(c.nh))
        score_of = lambda j: _dot(k2[:, (j // c.rep) * c.dqk:(j // c.rep + 1) * c.dqk],
                                  q_ref[:, j * c.dqk:(j + 1) * c.dqk], NT)
        ahead = score_of(0) if heads else None
        for j in heads:
            g = j // c.rep
            s = ahead * c.scale
            if j + 1 < c.nh:
                ahead = score_of(j + 1)
            s = jnp.where(mask, s, -jnp.inf)
            m = jnp.max(s, axis=0, keepdims=True)
            if sink is not None:
                sk = sink_ref[:, j:j + 1]
                m = jnp.maximum(m, sk)
            p = jnp.exp(s - m)
            l = jnp.sum(p, axis=0, keepdims=True)
            if sink is not None:
                l = l + jnp.exp(sk - m)
            o_t = _dot(v2[:, g * c.dv:(g + 1) * c.dv], p, TN)
            o_ref[:, j * c.dv:(j + 1) * c.dv] = (o_t / l).T.astype(o_ref.dtype)
            lses.append(m + jnp.log(l))
        lse_ref[...] = _rows_to_stats(lses, BLOCK)

    prev = lambda i: jnp.maximum(i - 1, 0)
    kw, vw = c.nkv * c.dqk, c.nkv * c.dv
    in_specs = [
        pl.BlockSpec((BLOCK, c.nh * c.dqk), lambda g, i: (i, c.qcol(g))),
        pl.BlockSpec((BLOCK, kw), lambda g, i: (prev(i), c.kcol(g))),
        pl.BlockSpec((BLOCK, kw), lambda g, i: (i, c.kcol(g))),
        pl.BlockSpec((BLOCK, vw), lambda g, i: (prev(i), c.vcol(g))),
        pl.BlockSpec((BLOCK, vw), lambda g, i: (i, c.vcol(g))),
    ]
    args = [q, k, k, v, v]
    if sink is not None:
        in_specs.append(pl.BlockSpec((1, LANES), lambda g, i: (0, 0)))
        args.append(sink)
    return _pcall(
        body, name=name, dims=("parallel", "parallel"), grid=(c.G, nq), in_specs=in_specs,
        out_specs=[pl.BlockSpec((BLOCK, c.nh * c.dv), lambda g, i: (i, c.ocol(g))),
                   pl.BlockSpec((BLOCK, LANES), lambda g, i: (i, g))],
        out_shape=[jax.ShapeDtypeStruct((c.T, c.o_width), out_dtype),
                   jax.ShapeDtypeStruct((c.T, LANES * c.G), F32)],
    )(*args)


def _band_bwd(cfg, q, k, v, do, lse, delta, *, name):
    c = cfg
    assert c.mode == "band" and c.tq == c.tk == BLOCK and c.T == c.Tk
    nq = c.T // BLOCK
    qw, kw, vw = c.nh * c.dqk, c.nkv * c.dqk, c.nkv * c.dv

    def body(q_ref, kp_ref, kc_ref, vp_ref, vc_ref, do_ref, lse_ref, d_ref, dq_ref, dk_ref, dv_ref, dk_c, dv_c):
        n = pl.program_id(1)

        @pl.when(n == 0)
        def _():
            dk_c[...] = jnp.zeros_like(dk_c)
            dv_c[...] = jnp.zeros_like(dv_c)

        @pl.when(n < nq)
        def _():
            mask = _band_mask(c, n)
            k2 = jnp.concatenate([kp_ref[...], kc_ref[...]], axis=0)
            v2 = jnp.concatenate([vp_ref[...], vc_ref[...]], axis=0)
            lse_t, d_t = lse_ref[...].T, d_ref[...].T
            if _head_pairs(c):
                mask2 = jnp.concatenate([mask, mask], axis=1)
                pair_lanes = [slice(pc * LANES, (pc + 1) * LANES) for pc in range(c.nh // 2)]

                def first(sl):
                    q_bd, do_bd = _block_diagonal(q_ref[:, sl]), _block_diagonal(do_ref[:, sl])
                    return q_bd, do_bd, k2[:, sl], _dot(k2[:, sl], q_bd, NT), _dot(v2[:, sl], do_bd, NT)

                def finish(entry):
                    sl, dq_t, dv_pair, dk_pair = entry
                    dq_ref[:, sl] = _own_blocks(dq_t).T
                    dk_ref[:, sl] = dk_c[:, sl] + dk_pair[:BLOCK]
                    dv_ref[:, sl] = dv_c[:, sl] + dv_pair[:BLOCK]
                    dk_c[:, sl] = dk_pair[BLOCK:]
                    dv_c[:, sl] = dv_pair[BLOCK:]

                ahead, behind = first(pair_lanes[0]), None
                for pc, sl in enumerate(pair_lanes):
                    q_bd, do_bd, kp, s, dp = ahead
                    if pc + 1 < len(pair_lanes):
                        ahead = first(pair_lanes[pc + 1])
                    both = lambda t: jnp.concatenate([t[2 * pc:2 * pc + 1, :], t[2 * pc + 1:2 * pc + 2, :]], axis=1)
                    p = jnp.exp(jnp.where(mask2, s * c.scale, -jnp.inf) - both(lse_t))
                    ds = p * (dp - both(d_t)) * c.scale
                    entry = (sl, _dot(kp, ds, TN), _dot(p, do_bd, NN), _dot(ds, q_bd, NN))
                    if behind is not None:
                        finish(behind)
                    behind = entry
                finish(behind)
                return
            dk2, dv2 = [None] * c.nkv, [None] * c.nkv

            def first_of(j):
                g = j // c.rep
                qh, doh = q_ref[:, j * c.dqk:(j + 1) * c.dqk], do_ref[:, j * c.dv:(j + 1) * c.dv]
                kh = k2[:, g * c.dqk:(g + 1) * c.dqk]
                return qh, doh, kh, _dot(kh, qh, NT), _dot(v2[:, g * c.dv:(g + 1) * c.dv], doh, NT)

            ahead = first_of(0)
            for j in range(c.nh):
                g = j // c.rep
                qh, doh, kh, s, dp = ahead
                if j + 1 < c.nh:
                    ahead = first_of(j + 1)
                p = jnp.exp(jnp.where(mask, s * c.scale, -jnp.inf) - lse_t[j:j + 1, :])
                ds = p * (dp - d_t[j:j + 1, :]) * c.scale
                dq_ref[:, j * c.dqk:(j + 1) * c.dqk] = _dot(kh, ds, TN).T
                dvh, dkh = _dot(p, doh, NN), _dot(ds, qh, NN)
                dv2[g] = dvh if dv2[g] is None else dv2[g] + dvh
                dk2[g] = dkh if dk2[g] is None else dk2[g] + dkh
            for g in range(c.nkv):
                ks, vs = slice(g * c.dqk, (g + 1) * c.dqk), slice(g * c.dv, (g + 1) * c.dv)
                dk_ref[:, ks] = dk_c[:, ks] + dk2[g][:BLOCK]
                dv_ref[:, vs] = dv_c[:, vs] + dv2[g][:BLOCK]
                dk_c[:, ks] = dk2[g][BLOCK:]
                dv_c[:, vs] = dv2[g][BLOCK:]

        @pl.when(n == nq)
        def _():
            dk_ref[...] = dk_c[...]
            dv_ref[...] = dv_c[...]

    cur = lambda n: jnp.minimum(n, nq - 1)
    prev = lambda n: jnp.maximum(cur(n) - 1, 0)
    out_blk = lambda n: jnp.maximum(n - 1, 0)
    stat = pl.BlockSpec((BLOCK, LANES), lambda g, n: (cur(n), g))
    dq_spec = pl.BlockSpec((BLOCK, qw), lambda g, n: (cur(n), g))
    dk_spec = pl.BlockSpec((BLOCK, kw), lambda g, n: (out_blk(n), g))
    dv_spec = pl.BlockSpec((BLOCK, vw), lambda g, n: (out_blk(n), g))
    in_specs = [
        pl.BlockSpec((BLOCK, qw), lambda g, n: (cur(n), c.qcol(g))),
        pl.BlockSpec((BLOCK, kw), lambda g, n: (prev(n), c.kcol(g))),
        pl.BlockSpec((BLOCK, kw), lambda g, n: (cur(n), c.kcol(g))),
        pl.BlockSpec((BLOCK, vw), lambda g, n: (prev(n), c.vcol(g))),
        pl.BlockSpec((BLOCK, vw), lambda g, n: (cur(n), c.vcol(g))),
        pl.BlockSpec((BLOCK, c.nh * c.dv), lambda g, n: (cur(n), c.ocol(g))),
        stat, stat,
    ]
    return _pcall(
        body, name=name, dims=("parallel", "arbitrary"), grid=(c.G, nq + 1), in_specs=in_specs,
        out_specs=[dq_spec, dk_spec, dv_spec],
        out_shape=[_sds((c.T, c.G * qw)), _sds((c.T, c.G * kw)), _sds((c.T, c.G * vw))],
        scratch_shapes=[pltpu.VMEM((BLOCK, kw), F32), pltpu.VMEM((BLOCK, vw), F32)],
    )(q, k, k, v, v, do, lse, delta)


def _causal_pairs(n, kv_major):
    pairs =[(i, j) for j in range(n) for i in range(j, n)] if kv_major else [(i, j) for i in range(n) for j in range(i + 1)]
    return jnp.asarray(np.array([p[0] for p in pairs], np.int32)), jnp.asarray(np.array([p[1] for p in pairs], np.int32))


def _causal_mask(t):
    return lax.broadcasted_iota(jnp.int32, (t, t), 0) >= lax.broadcasted_iota(jnp.int32, (t, t), 1)


def _carrying(body, n_in, n_out, n_scratch, grid, carry):
    if carry is None:
        return body
    G, P = grid

    def wrapped(*refs):
        refs = list(refs)
        prefetch, refs = refs[:2], refs[2:]
        ins, src = refs[:n_in], refs[n_in]
        outs, out = refs[n_in + 1:n_in + 1 + n_out], refs[n_in + 1 + n_out]
        scratch, sems = refs[n_in + 2 + n_out:n_in + 2 + n_out + n_scratch], refs[n_in + 2 + n_out + n_scratch:]
        step = pl.program_id(0) * P + pl.program_id(1)
        carry.run([src, out] + sems, step, G * P, at_end=False)
        body(*prefetch, *ins, *outs, *scratch)
        carry.run([src, out] + sems, step, G * P, at_end=True)

    return wrapped


def _carry_specs(carry):
    if carry is None:
        return [], [], [], [], []
    any_space = pl.BlockSpec(memory_space=pl.ANY)
    return [any_space], [any_space], [carry.out_shape], list(carry.sems), [carry.src]


def _causal_fwd(cfg, q, k, v, *, name, out_dtype=F32, stat_heads=None, carry=None):
    c = cfg
    assert c.mode == "causal" and c.tq == c.tk and c.T == c.Tk
    t, n = c.tq, c.T // c.tq
    stat_heads = stat_heads or c.nh
    stat_blocks = c.nh // stat_heads
    assert stat_blocks * stat_heads == c.nh
    qi_tab, kj_tab = _causal_pairs(n, kv_major=False)
    n_pairs = int(qi_tab.shape[0])

    def body(qi_ref, kj_ref, q_ref, k_ref, v_ref, o_ref, lse_ref, m_scr, l_scr, acc):
        pair = pl.program_id(1)
        qi, kj = qi_ref[pair], kj_ref[pair]

        @pl.when(kj == 0)
        def _():
            m_scr[...] = jnp.full_like(m_scr, NEG_BIG)
            l_scr[...] = jnp.zeros_like(l_scr)
            acc[...] = jnp.zeros_like(acc)

        def step(diagonal):
            mask = None
            if diagonal:
                mask = lax.broadcasted_iota(jnp.int32, (t, t), 1) >= lax.broadcasted_iota(jnp.int32, (t, t), 0)
            scores = [_dot(k_ref[:, (j // c.rep) * c.dqk:(j // c.rep + 1) * c.dqk],
                           q_ref[:, j * c.dqk:(j + 1) * c.dqk], NT) for j in range(c.nh)]
            for j in range(c.nh):
                g = j // c.rep
                s = scores[j] * c.scale
                if diagonal:
                    s = jnp.where(mask, s, -jnp.inf)
                m_prev = m_scr[j]
                m_new = jnp.maximum(m_prev, jnp.max(s, axis=0, keepdims=True))
                alpha = jnp.exp(m_prev - m_new)
                p = jnp.exp(s - m_new)
                l_scr[j] = alpha * l_scr[j] + jnp.sum(p, axis=0, keepdims=True)
                acc[j] = alpha * acc[j] + _dot(v_ref[:, g * c.dv:(g + 1) * c.dv], p, TN)
                m_scr[j] = m_new

        pl.when(kj == qi)(lambda: step(True))
        pl.when(kj != qi)(lambda: step(False))

        @pl.when(kj == qi)
        def _():
            rows = []
            for j in range(c.nh):
                o_ref[:, j * c.dv:(j + 1) * c.dv] = (acc[j] / l_scr[j]).T.astype(o_ref.dtype)
                rows.append(m_scr[j] + jnp.log(l_scr[j]))
            for b in range(stat_blocks):
                lse_ref[:, b * LANES:(b + 1) * LANES] = _rows_to_stats(rows[b * stat_heads:(b + 1) * stat_heads], t)

    x_in, x_out, x_shapes, x_scratch, x_args = _carry_specs(carry)
    grid_spec = pltpu.PrefetchScalarGridSpec(
        num_scalar_prefetch=2, grid=(c.G, n_pairs),
        in_specs=[pl.BlockSpec((t, c.nh * c.dqk), lambda g, p, qi, kj: (qi[p], c.qcol(g))),
                  pl.BlockSpec((t, c.nkv * c.dqk), lambda g, p, qi, kj: (kj[p], c.kcol(g))),
                  pl.BlockSpec((t, c.nkv * c.dv), lambda g, p, qi, kj: (kj[p], c.vcol(g)))] + x_in,
        out_specs=[pl.BlockSpec((t, c.nh * c.dv), lambda g, p, qi, kj: (qi[p], c.ocol(g))),
                   pl.BlockSpec((t, LANES * stat_blocks), lambda g, p, qi, kj: (qi[p], g))] + x_out,
        scratch_shapes=[pltpu.VMEM((c.nh, 1, t), F32), pltpu.VMEM((c.nh, 1, t), F32),
                        pltpu.VMEM((c.nh, c.dv, t), F32)] + x_scratch)
    return _pcall(
        _carrying(body, 3, 2, 3, (c.G, n_pairs), carry), name=name,
        dims=("arbitrary", "arbitrary") if carry is not None else ("parallel", "arbitrary"), grid_spec=grid_spec,
        out_shape=[jax.ShapeDtypeStruct((c.T, c.o_width), out_dtype),
                   jax.ShapeDtypeStruct((c.T, LANES * c.G * stat_blocks), F32)] + x_shapes,
    )(qi_tab, kj_tab, q, k, v, *x_args)


def _causal_bwd(cfg, q, k, v, do, lse, delta, *, name, carry=None):
    c = cfg
    assert c.mode == "causal" and c.tq == c.tk and c.T == c.Tk
    t, n = c.tq, c.T // c.tq
    qw, kw, vw = c.nh * c.dqk, c.nkv * c.dqk, c.nkv * c.dv
    qi_tab, kj_tab = _causal_pairs(n, kv_major=True)

    def body(qi_ref, kj_ref, q_ref, k_ref, v_ref, do_ref, lse_ref, d_ref, dq_ref, dk_ref, dv_ref, dk_acc, dv_acc):
        pair = pl.program_id(1)
        qi, kj = qi_ref[pair], kj_ref[pair]

        @pl.when(pair == 0)
        def _():
            dq_ref[...] = jnp.zeros_like(dq_ref)

        @pl.when(qi == kj)
        def _():
            dk_acc[...] = jnp.zeros_like(dk_acc)
            dv_acc[...] = jnp.zeros_like(dv_acc)

        rows = pl.ds(pl.multiple_of(qi * t, t), t)

        def step(diagonal):
            mask = _causal_mask(t) if diagonal else None
            for j in range(c.nh):
                g = j // c.rep
                qs, ks, vs = (slice(j * c.dqk, (j + 1) * c.dqk), slice(g * c.dqk, (g + 1) * c.dqk),
                              slice(g * c.dv, (g + 1) * c.dv))
                qh, doh, kh = q_ref[:, qs], do_ref[:, j * c.dv:(j + 1) * c.dv], k_ref[:, ks]
                s = _dot(qh, kh, NT) * c.scale
                if diagonal:
                    s = jnp.where(mask, s, -jnp.inf)
                p = jnp.exp(s - lse_ref[:, j:j + 1])
                ds = p * (_dot(doh, v_ref[:, vs], NT) - d_ref[:, j:j + 1]) * c.scale
                dq_ref[rows, qs] += _dot(ds, kh, NN)
                dv_acc[g] += _dot(doh, p, TN)
                dk_acc[g] += _dot(qh, ds, TN)

        pl.when(qi == kj)(lambda: step(True))
        pl.when(qi != kj)(lambda: step(False))

        @pl.when(qi == n - 1)
        def _():
            for g in range(c.nkv):
                dk_ref[:, g * c.dqk:(g + 1) * c.dqk] = dk_acc[g].T
                dv_ref[:, g * c.dv:(g + 1) * c.dv] = dv_acc[g].T

    stat = pl.BlockSpec((t, LANES), lambda g, p, qi, kj: (qi[p], g))
    o_spec = pl.BlockSpec((t, c.nh * c.dv), lambda g, p, qi, kj: (qi[p], c.ocol(g)))
    n_pairs = int(qi_tab.shape[0])
    x_in, x_out, x_shapes, x_scratch, x_args = _carry_specs(carry)
    grid_spec = pltpu.PrefetchScalarGridSpec(
        num_scalar_prefetch=2, grid=(c.G, n_pairs),
        in_specs=[pl.BlockSpec((t, qw), lambda g, p, qi, kj: (qi[p], c.qcol(g))),
                  pl.BlockSpec((t, kw), lambda g, p, qi, kj: (kj[p], c.kcol(g))),
                  pl.BlockSpec((t, vw), lambda g, p, qi, kj: (kj[p], c.vcol(g))),
                  o_spec, stat, stat] + x_in,
        out_specs=[pl.BlockSpec((c.T, qw), lambda g, p, qi, kj: (0, g)),
                   pl.BlockSpec((t, kw), lambda g, p, qi, kj: (kj[p], g)),
                   pl.BlockSpec((t, vw), lambda g, p, qi, kj: (kj[p], g))] + x_out,
        scratch_shapes=[pltpu.VMEM((c.nkv, c.dqk, t), F32), pltpu.VMEM((c.nkv, c.dv, t), F32)] + x_scratch)
    return _pcall(
        _carrying(body, 6, 3, 2, (c.G, n_pairs), carry), name=name,
        dims=("arbitrary", "arbitrary") if carry is not None else ("parallel", "arbitrary"), grid_spec=grid_spec,
        out_shape=[_sds((c.T, c.G * qw)), _sds((c.T, c.G * kw)), _sds((c.T, c.G * vw))] + x_shapes,
    )(qi_tab, kj_tab, q, k, v, do, lse, delta, *x_args)


def _rowwise(body, ins, outs, *, name, rows, tm=512, accs=(), scratch=()):
    tm = _row_tile(rows, tm)

    def spec(a):
        if a.shape[0] == 1:
            return pl.BlockSpec((1, a.shape[1]), lambda i: (0, 0))
        d = rows // a.shape[0]
        assert d * a.shape[0] == rows and tm % d == 0
        return pl.BlockSpec((tm // d, a.shape[1]), lambda i: (i, 0))

    return _pcall(
        functools.partial(body, tm), name=name, dims=("arbitrary" if accs else "parallel",), grid=(rows // tm,),
        in_specs=[spec(a) for a in ins], out_specs=[spec(a) for a in outs], out_shape=list(outs),
        scratch_shapes=list(scratch),
    )(*ins)


def _sds(shape, dtype=F32):
    return jax.ShapeDtypeStruct(shape, dtype)


def _acc_rows(ref, val):
    @pl.when(pl.program_id(0) == 0)
    def _():
        ref[...] = jnp.zeros_like(ref)

    ref[...] += jnp.sum(val, axis=0, keepdims=True)


Z_QA, Z_KA, Z_VA, Z_CQ, Z_CKV, Z_KR, Z_END = 0, 512, 640, 768, 1152, 1408, 1536


def _l0_prep(z, tabs, q_norm, kv_norm, *, name):
    S = z.shape[0]

    def body(tm, z_ref, c64, s64, ck, sk, gq, gkv, qa_o, ka_o, va_o, cq_o, ckv_o, kr_o):
        for i in range(4):
            sl = slice(Z_QA + i * LANES, Z_QA + (i + 1) * LANES)
            qa_o[:, i * LANES:(i + 1) * LANES] = _rope_chunk(z_ref[:, sl], c64[...], s64[...], 32).astype(qa_o.dtype)
        ka_o[...] = _rope_chunk(z_ref[:, Z_KA:Z_VA], c64[...], s64[...], 32).astype(ka_o.dtype)
        va_o[...] = z_ref[:, Z_VA:Z_CQ].astype(va_o.dtype)
        cq_o[...] = (_rms_parts(z_ref[:, Z_CQ:Z_CKV])[0] * gq[...]).astype(cq_o.dtype)
        ckv_o[...] = (_rms_parts(z_ref[:, Z_CKV:Z_KR])[0] * gkv[...]).astype(ckv_o.dtype)
        kr_o[...] = _rope_chunk(z_ref[:, Z_KR:Z_END], ck[...], sk[...], 16)

    outs = [_sds((S, 512), MXU_DTYPE), _sds((S, 128), MXU_DTYPE), _sds((S, 128), MXU_DTYPE),
            _sds((S, MLA_Q_RANK), MXU_DTYPE), _sds((S, MLA_KV_RANK), MXU_DTYPE), _sds((S, LANES))]
    ins = [z, tabs["c64"], tabs["s64"], tabs["ck"], tabs["sk"], q_norm.reshape(1, -1), kv_norm.reshape(1, -1)]
    return _rowwise(body, ins, outs, name=name, rows=S)


def _l0_prep_bwd(z, tabs, q_norm, kv_norm, dqa, dka, dva, dcq, dckv, dkr, *, name):
    S = z.shape[0]

    def body(tm, z_ref, c64, s64, ck, sk, gq, gkv, dqa_r, dka_r, dva_r, dcq_r, dckv_r, dkr_r, dz_o, dgq_o, dgkv_o):
        for i in range(4):
            sl = slice(i * LANES, (i + 1) * LANES)
            dz_o[:, sl] = _rope_chunk(dqa_r[:, sl].astype(F32), c64[...], -s64[...], 32).astype(dz_o.dtype)
        dz_o[:, Z_KA:Z_VA] = _rope_chunk(dka_r[...].astype(F32), c64[...], -s64[...], 32).astype(dz_o.dtype)
        dz_o[:, Z_VA:Z_CQ] = dva_r[...].astype(dz_o.dtype)
        dx, dgp = _rms_bwd_rows(z_ref[:, Z_CQ:Z_CKV], gq[...], dcq_r[...].astype(F32))
        dz_o[:, Z_CQ:Z_CKV] = dx.astype(dz_o.dtype)
        _acc_rows(dgq_o, dgp)
        dx, dgp = _rms_bwd_rows(z_ref[:, Z_CKV:Z_KR], gkv[...], dckv_r[...].astype(F32))
        dz_o[:, Z_CKV:Z_KR] = dx.astype(dz_o.dtype)
        _acc_rows(dgkv_o, dgp)
        dz_o[:, Z_KR:Z_END] = _rope_chunk(dkr_r[...], ck[...], -sk[...], 16).astype(dz_o.dtype)

    outs = [_sds((S, Z_END), MXU_DTYPE), _sds((1, MLA_Q_RANK)), _sds((1, MLA_KV_RANK))]
    ins = [z, tabs["c64"], tabs["s64"], tabs["ck"], tabs["sk"], q_norm.reshape(1, -1), kv_norm.reshape(1, -1),
           dqa, dka, dva, dcq, dckv, dkr]
    return _rowwise(body, ins, outs, name=name, rows=S, accs=(1, 2))


def _mla_prep(qb, kvb, kr, tabs, *, name):
    S = qb.shape[0]

    def body(tm, qb_r, kvb_r, kr_r, cm, sm, q_o, k_o, v_o):
        lane = _lane((tm, LANES))
        kr_at_64 = pltpu.roll(kr_r[...], 64, 1)
        for h in range(MLA_HEADS):
            sl = slice(h * LANES, (h + 1) * LANES)
            q_o[:, sl] = _rope_chunk(qb_r[:, sl], cm[...], sm[...], 16).astype(q_o.dtype)
            k_o[:, sl] = jnp.where(lane < 64, kvb_r[:, sl], kr_at_64).astype(k_o.dtype)
        for p in range(MLA_HEADS // 2):
            even = pltpu.roll(kvb_r[:, (2 * p) * LANES:(2 * p + 1) * LANES], 64, 1)
            odd = kvb_r[:, (2 * p + 1) * LANES:(2 * p + 2) * LANES]
            v_o[:, p * LANES:(p + 1) * LANES] = jnp.where(lane < 64, even, odd).astype(v_o.dtype)

    outs = [_sds((S, 1024), MXU_DTYPE), _sds((S, 1024), MXU_DTYPE), _sds((S, 512), MXU_DTYPE)]
    return _rowwise(body, [qb, kvb, kr, tabs["cm"], tabs["sm"]], outs, name=name, rows=S)


def _mla_prep_bwd(dq, dk, dv, tabs, *, name):
    S = dq.shape[0]

    def body(tm, dq_r, dk_r, dv_r, cm, sm, dqb_o, dkvb_o, dkr_o):
        lane = _lane((tm, LANES))
        dkr = jnp.zeros((tm, LANES), F32)
        for h in range(MLA_HEADS):
            sl = slice(h * LANES, (h + 1) * LANES)
            dqb_o[:, sl] = _rope_chunk(dq_r[:, sl].astype(F32), cm[...], -sm[...], 16).astype(dqb_o.dtype)
            dkh = dk_r[:, sl].astype(F32)
            dvp = dv_r[:, (h // 2) * LANES:(h // 2 + 1) * LANES].astype(F32)
            dvh = pltpu.roll(dvp, 64, 1) if h % 2 == 0 else dvp
            dkvb_o[:, sl] = jnp.where(lane < 64, dkh, dvh).astype(dkvb_o.dtype)
            dkr = dkr + pltpu.roll(dkh, 64, 1)
        dkr_o[...] = jnp.where(lane < MLA_ROPE, dkr, 0.0)

    outs = [_sds((S, 1024), MXU_DTYPE), _sds((S, 1024), MXU_DTYPE), _sds((S, LANES))]
    return _rowwise(body, [dq, dk, dv, tabs["cm"], tabs["sm"]], outs, name=name, rows=S)


DILATIONS = tuple(d for _, d in DIL_PATTERNS)
QKV_CHUNKS = 8


def _to_branch(nat, c0, chunks, out_ref, d, rows):
    width = chunks * LANES
    for r in range(d):
        tok = pl.ds(r, rows // d, stride=d) if d > 1 else slice(None)
        for c in range(chunks):
            out_ref[:, r * width + c * LANES:r * width + (c + 1) * LANES] = nat[c0 + c, tok, :].astype(out_ref.dtype)


def _from_branch(in_ref, nat, c0, chunks, d, rows, add=False):
    width = chunks * LANES
    for r in range(d):
        tok = pl.ds(r, rows // d, stride=d) if d > 1 else slice(None)
        for c in range(chunks):
            val = in_ref[:, r * width + c * LANES:r * width + (c + 1) * LANES].astype(F32)
            nat[c0 + c, tok, :] = nat[c0 + c, tok, :] + val if add else val


def _branch_sds(S, width, d, dtype):
    return _sds((S // d, d * width), dtype)


def _l1_prep(qkv, tabs, *, name):
    S = qkv.shape[0]

    def body(tm, x_r, c64, s64, *rest):
        outs, nat = rest[:-1], rest[-1]
        for i in range(QKV_CHUNKS):
            sl = slice(i * LANES, (i + 1) * LANES)
            nat[i] = _rope_chunk(x_r[:, sl], c64[...], s64[...], 32)
            nat[QKV_CHUNKS + i] = _rope_chunk(x_r[:, 1024 + i * LANES:1024 + (i + 1) * LANES], c64[...], s64[...], 32)
            nat[2 * QKV_CHUNKS + i] = x_r[:, 2048 + i * LANES:2048 + (i + 1) * LANES]
        for b, d in enumerate(DILATIONS):
            for t in range(3):
                _to_branch(nat, t * QKV_CHUNKS, QKV_CHUNKS, outs[3 * b + t], d, tm)

    outs = [_branch_sds(S, 1024, d, MXU_DTYPE) for d in DILATIONS for _ in range(3)]
    got = _rowwise(body, [qkv, tabs["c64"], tabs["s64"]], outs, name=name, rows=S,
                   scratch=[pltpu.VMEM((3 * QKV_CHUNKS, _row_tile(S, 512), LANES), F32)])
    return {d: tuple(got[3 * b:3 * b + 3]) for b, d in enumerate(DILATIONS)}


def _l1_prep_bwd(grads, tabs, *, name):
    S = grads[1][0].shape[0]

    def body(tm, *rest):
        ins, (c64, s64, o, nat) = rest[:9], rest[9:]
        for b, d in enumerate(DILATIONS):
            for t in range(3):
                _from_branch(ins[3 * b + t], nat, t * QKV_CHUNKS, QKV_CHUNKS, d, tm, add=b > 0)
        for i in range(QKV_CHUNKS):
            sl = slice(i * LANES, (i + 1) * LANES)
            o[:, sl] = _rope_chunk(nat[i], c64[...], -s64[...], 32).astype(o.dtype)
            o[:, 1024 + i * LANES:1024 + (i + 1) * LANES] = _rope_chunk(
                nat[QKV_CHUNKS + i], c64[...], -s64[...], 32).astype(o.dtype)
            o[:, 2048 + i * LANES:2048 + (i + 1) * LANES] = nat[2 * QKV_CHUNKS + i].astype(o.dtype)

    ins = [g for d in DILATIONS for g in grads[d]] + [tabs["c64"], tabs["s64"]]
    return _rowwise(body, ins, [_sds((S, 3072), MXU_DTYPE)], name=name, rows=S, tm=256,
                    scratch=[pltpu.VMEM((3 * QKV_CHUNKS, _row_tile(S, 256), LANES), F32)])[0]


def _sigmoid(x):
    return 1.0 / (1.0 + jnp.exp(-x))


FFN_ROW_TILE, FFN_COL_TILE = 512, 1408


def _gate_up(h, w_gate, w_up, *, name):
    (M, K), N = h.shape, w_gate.shape[1]
    tm, tn = _tile(M, FFN_ROW_TILE), _tile(N, FFN_COL_TILE)

    def body(h_ref, wg_ref, wu_ref, g_ref, u_ref, a_ref):
        g = _dot(h_ref[...], wg_ref[...], NN)
        u = _dot(h_ref[...], wu_ref[...], NN)
        g_ref[...] = g
        u_ref[...] = u
        a_ref[...] = (g * _sigmoid(g) * u).astype(a_ref.dtype)

    w_spec = pl.BlockSpec((K, tn), lambda j, i: (0, j))
    o_spec = pl.BlockSpec((tm, tn), lambda j, i: (i, j))
    return _pcall(
        body, name=name, dims=("parallel", "parallel"), grid=(N // tn, M // tm),
        in_specs=[pl.BlockSpec((tm, K), lambda j, i: (i, 0)), w_spec, w_spec], out_specs=[o_spec] * 3,
        out_shape=[_sds((M, N)), _sds((M, N)), _sds((M, N), MXU_DTYPE)],
    )(h, w_gate, w_up)


def _gate_up_bwd(dx, w_down, gate, up, *, name):
    (M, K), N = dx.shape, w_down.shape[0]
    tm, tn = _tile(M, FFN_ROW_TILE), _tile(N, FFN_COL_TILE)

    def body(dx_ref, w_ref, g_ref, u_ref, dg_ref, du_ref):
        d = _dot(dx_ref[...], w_ref[...], NT)
        g = g_ref[...]
        sg = _sigmoid(g)
        dg_ref[...] = (d * u_ref[...] * (sg * (1.0 + g * (1.0 - sg)))).astype(dg_ref.dtype)
        du_ref[...] = (d * g * sg).astype(du_ref.dtype)

    o_spec = pl.BlockSpec((tm, tn), lambda j, i: (i, j))
    return _pcall(
        body, name=name, dims=("parallel", "parallel"), grid=(N // tn, M // tm),
        in_specs=[pl.BlockSpec((tm, K), lambda j, i: (i, 0)), pl.BlockSpec((tn, K), lambda j, i: (j, 0)),
                  o_spec, o_spec],
        out_specs=[o_spec] * 2, out_shape=[_sds((M, N), MXU_DTYPE)] * 2,
    )(dx, w_down, gate, up)


def _head_pair_weights(w, c, rows):
    return jnp.where(_lane((rows, LANES)) < HEAD_DIM, w[:, 2 * c:2 * c + 1], w[:, 2 * c + 1:2 * c + 2])


def _merge(outs_by_d, lses_by_d, *, name):
    S = outs_by_d[1].shape[0]
    far = DILATIONS[1:]

    def body(tm, o1, o4, o16, l1, l4, l16, o_o, w1_o, w4_o, w16_o, nat_o, nat_l):
        for b, (o_r, l_r, d) in enumerate(zip((o4, o16), (l4, l16), far)):
            _from_branch(o_r, nat_o, b * QKV_CHUNKS, QKV_CHUNKS, d, tm)
            _from_branch(l_r, nat_l, b, 1, d, tm)
        ls = [l1[...], nat_l[0], nat_l[1]]
        m = jnp.maximum(jnp.maximum(ls[0], ls[1]), ls[2])
        es = [jnp.exp(l - m) for l in ls]
        tot = es[0] + es[1] + es[2]
        ws = [e / tot for e in es]
        for w_o, w in zip((w1_o, w4_o, w16_o), ws):
            w_o[...] = w
        for c in range(QKV_CHUNKS):
            sl = slice(c * LANES, (c + 1) * LANES)
            parts = (o1[:, sl], nat_o[c], nat_o[QKV_CHUNKS + c])
            o_o[:, sl] = sum(_head_pair_weights(w, c, tm) * part for w, part in zip(ws, parts))

    ins = [outs_by_d[d] for d in DILATIONS] + [lses_by_d[d] for d in DILATIONS]
    outs = [_sds((S, 1024))] + [_sds((S, LANES))] * 3
    rows = _row_tile(S, 256)
    return _rowwise(body, ins, outs, name=name, rows=S, tm=256,
                    scratch=[pltpu.VMEM((2 * QKV_CHUNKS, rows, LANES), F32), pltpu.VMEM((2, rows, LANES), F32)])


def _merge_bwd(do, o, ws, *, name):
    S = do.shape[0]

    def body(tm, do_r, o_r, w1, w4, w16, d1, d4, d16, e1, e4, e16, nat, nat_l):
        prod = do_r[...] * o_r[...]
        sums = _cols_to_lanes([jnp.sum(prod[:, j * HEAD_DIM:(j + 1) * HEAD_DIM], axis=1, keepdims=True)
                               for j in range(DIL_HEADS)], tm)
        for w_r, d_o, e_o, d in zip((w1, w4, w16), (d1, d4, d16), (e1, e4, e16), DILATIONS):
            w = w_r[...]
            nat_l[0] = w * sums
            _to_branch(nat_l, 0, 1, e_o, d, tm)
            for c in range(QKV_CHUNKS):
                nat[c] = _head_pair_weights(w, c, tm) * do_r[:, c * LANES:(c + 1) * LANES]
            _to_branch(nat, 0, QKV_CHUNKS, d_o, d, tm)

    outs = [_branch_sds(S, 1024, d, MXU_DTYPE) for d in DILATIONS] + [_branch_sds(S, LANES, d, F32) for d in DILATIONS]
    rows = _row_tile(S, 256)
    got = _rowwise(body, [do, o] + [ws[d] for d in DILATIONS], outs, name=name, rows=S, tm=256,
                   scratch=[pltpu.VMEM((QKV_CHUNKS, rows, LANES), F32), pltpu.VMEM((1, rows, LANES), F32)])
    return dict(zip(DILATIONS, got[:3])), dict(zip(DILATIONS, got[3:]))


def _loss_head(x, g, target, *, name):
    S, D = x.shape

    def body(tm, x_r, g_r, t_r, dx_o, dg_o, sq_o):
        xf = x_r[...]
        xhat, _ = _rms_parts(xf)
        err = xhat * g_r[...] - t_r[...]
        dx, dgp = _rms_bwd_rows(xf, g_r[...], err * (1.0 / D))
        dx_o[...] = dx
        _acc_rows(dg_o, dgp)
        _acc_rows(sq_o, err * err)

    return _rowwise(body, [x, g.reshape(1, D), target], [_sds((S, D)), _sds((1, D)), _sds((1, D))],
                    name=name, rows=S, accs=(1, 2))


def _adamw(w, g, m, v, *, name):
    c1 = 1.0 - ADAM_B1 ** ADAM_STEP
    c2 = 1.0 - ADAM_B2 ** ADAM_STEP

    def body(tm, w_r, g_r, m_r, v_r, d_o, m_o, v_o):
        g = g_r[...]
        m_new = ADAM_B1 * m_r[...] + (1.0 - ADAM_B1) * g
        v_new = ADAM_B2 * v_r[...] + (1.0 - ADAM_B2) * (g * g)
        m_o[...] = m_new
        v_o[...] = v_new
        d_o[...] = -ADAM_LR * ((m_new / c1) / (jnp.sqrt(v_new / c2) + ADAM_EPS) + ADAM_WD * w_r[...])

    return _rowwise(body, [w, g, m, v], [_sds(w.shape)] * 3, name=name, rows=w.shape[0], tm=256)


SUM_ROW_TILE = 256


def _sum_cores(grads, theirs, half_index, *, name):
    _, R, C = grads.shape
    h = R // 2
    nb = h // SUM_ROW_TILE

    def body(c_ref, g_ref, t_ref, o_ref):
        o_ref[...] = (g_ref[...].astype(F32) + t_ref[...].astype(F32)).astype(o_ref.dtype)

    grid_spec = pltpu.PrefetchScalarGridSpec(
        num_scalar_prefetch=1, grid=(4, nb),
        in_specs=[pl.BlockSpec((1, SUM_ROW_TILE, C), lambda k, i, c_ref: (k, c_ref[0] * nb + i, 0)),
                  pl.BlockSpec((1, SUM_ROW_TILE, C), lambda k, i, c_ref: (k, i, 0))],
        out_specs=pl.BlockSpec((1, SUM_ROW_TILE, C), lambda k, i, c_ref: (k, i, 0)))
    return _pcall(body, name=name, dims=("parallel", "parallel"), grid_spec=grid_spec,
                  out_shape=_sds((4, h, C), grads.dtype))(half_index, grads, theirs)


def _sum_chips(parts, half_index, *, name):
    _, h, C = parts.shape
    nb = h // SUM_ROW_TILE

    def body(c_ref, p_ref, o_ref):
        p = [p_ref[k].astype(F32) for k in range(4)]
        o_ref[...] = ((p[0] + p[1]) + p[2]) + p[3]

    grid_spec = pltpu.PrefetchScalarGridSpec(
        num_scalar_prefetch=1, grid=(nb,),
        in_specs=[pl.BlockSpec((4, SUM_ROW_TILE, C), lambda i, c_ref: (0, i, 0))],
        out_specs=pl.BlockSpec((SUM_ROW_TILE, C), lambda i, c_ref: (c_ref[0] * nb + i, 0)))
    return _pcall(body, name=name, dims=("parallel",), grid_spec=grid_spec,
                  out_shape=_sds((2 * h, C)))(half_index, parts)


def _position():
    return lax.axis_index("x"), lax.axis_index("y"), lax.axis_index("c")


def _chip_peers(x, y):
    return [(1 - x, y), (x, 1 - y), (1 - x, 1 - y)]


_HBM = pl.BlockSpec(memory_space=pltpu.HBM)
LOCAL_COPY_CHUNKS = 8


def _local_copies(src_ref, dst_ref, sems):
    rows = src_ref.shape[0] // LOCAL_COPY_CHUNKS
    assert rows * LOCAL_COPY_CHUNKS == src_ref.shape[0]
    return [pltpu.make_async_copy(src_ref.at[pl.ds(i * rows, rows)], dst_ref.at[pl.ds(i * rows, rows)], sems.at[i])
            for i in range(LOCAL_COPY_CHUNKS)]


class _Exchange:
    def __init__(self, src, out_shape, sems, stages):
        self.src, self.out_shape, self.sems, self.stages = src, out_shape, sems, stages

    def run(self, refs, step, n_steps, at_end):
        for fraction, fn in self.stages:
            if (fraction == 1.0) == at_end:
                pl.when(step == int(round(fraction * (n_steps - 1))))(functools.partial(fn, *refs))


def _run_exchange(ex, *, name):
    def body(*refs):
        for _, fn in ex.stages:
            fn(*refs)

    return pl.pallas_call(
        body, name=name, in_specs=[_HBM], out_specs=_HBM, out_shape=ex.out_shape, scratch_shapes=list(ex.sems),
    )(ex.src)


def _gather_exchange(src):
    R, C = src.shape
    h = R // 2

    def plan(src_ref, out_ref, send_sems, recv_sems, local_sems):
        x, y, c = _position()
        me = 2 * x + y
        peers = _chip_peers(x, y)
        mine, other = pl.ds(c * h, h), pl.ds((1 - c) * h, h)

        def copy(sem, src_part, dst_part, device):
            return pltpu.make_async_remote_copy(
                src_ref=src_part, dst_ref=dst_part, send_sem=send_sems.at[sem], recv_sem=recv_sems.at[sem],
                device_id=device, device_id_type=MESH)

        landed = [out_ref.at[2 * px + py, mine] for px, py in peers]
        theirs = [out_ref.at[2 * px + py, other] for px, py in peers]
        return dict(
            sends=lambda: [copy(j, src_ref.at[mine], out_ref.at[me, mine], (px, py, c))
                           for j, (px, py) in enumerate(peers)],
            local=lambda: _local_copies(src_ref, out_ref.at[me], local_sems),
            arrivals=lambda: [copy(j, landed[j], landed[j], (px, py, c)) for j, (px, py) in enumerate(peers)],
            passed=lambda: [copy(3 + j, landed[j], landed[j], (x, y, 1 - c)) for j in range(3)],
            from_sibling=lambda: [copy(3 + j, theirs[j], theirs[j], (x, y, 1 - c)) for j in range(3)])

    def start(*refs):
        p = plan(*refs)
        for cp in p["sends"]() + p["local"]():
            cp.start()

    def pass_on(*refs):
        p = plan(*refs)
        for arrival, forward in zip(p["arrivals"](), p["passed"]()):
            arrival.wait_recv()
            forward.start()

    def finish(*refs):
        p = plan(*refs)
        for cp in p["from_sibling"]():
            cp.wait_recv()
        for cp in p["sends"]() + p["passed"]():
            cp.wait_send()
        for cp in p["local"]():
            cp.wait()

    sems = [pltpu.SemaphoreType.DMA((6,)), pltpu.SemaphoreType.DMA((6,)), pltpu.SemaphoreType.DMA((LOCAL_COPY_CHUNKS,))]
    return _Exchange(src, jax.ShapeDtypeStruct((4, R, C), src.dtype), sems, [(0.0, start), (0.6, pass_on), (1.0, finish)])


def _swap_other_half(src, *, name):
    _, R, C = src.shape
    h = R // 2

    def body(src_ref, out_ref, send_sem, recv_sem):
        x, y, c = _position()
        cp = pltpu.make_async_remote_copy(
            src_ref=src_ref.at[:, pl.ds((1 - c) * h, h)], dst_ref=out_ref, send_sem=send_sem, recv_sem=recv_sem,
            device_id=(x, y, 1 - c), device_id_type=MESH)
        cp.start()
        cp.wait()

    return pl.pallas_call(
        body, name=name, in_specs=[_HBM], out_specs=_HBM, out_shape=jax.ShapeDtypeStruct((4, h, C), src.dtype),
        scratch_shapes=[pltpu.SemaphoreType.DMA, pltpu.SemaphoreType.DMA],
    )(src)


def _scatter_exchange(src):
    def plan(src_ref, out_ref, send_sems, recv_sems, local_sems):
        x, y, c = _position()
        me = 2 * x + y
        peers = _chip_peers(x, y)

        def copy(j, src_block, dst_slot):
            px, py = peers[j]
            return pltpu.make_async_remote_copy(
                src_ref=src_ref.at[src_block], dst_ref=out_ref.at[dst_slot], send_sem=send_sems.at[j],
                recv_sem=recv_sems.at[j], device_id=(px, py, c), device_id_type=MESH)

        return dict(sends=lambda: [copy(j, 2 * px + py, me) for j, (px, py) in enumerate(peers)],
                    arrivals=lambda: [copy(j, me, 2 * px + py) for j, (px, py) in enumerate(peers)],
                    local=lambda: _local_copies(src_ref.at[me], out_ref.at[me], local_sems))

    def start(*refs):
        p = plan(*refs)
        for cp in p["sends"]() + p["local"]():
            cp.start()

    def finish(*refs):
        p = plan(*refs)
        for cp in p["arrivals"]():
            cp.wait_recv()
        for cp in p["sends"]():
            cp.wait_send()
        for cp in p["local"]():
            cp.wait()

    sems = [pltpu.SemaphoreType.DMA((3,)), pltpu.SemaphoreType.DMA((3,)), pltpu.SemaphoreType.DMA((LOCAL_COPY_CHUNKS,))]
    return _Exchange(src, jax.ShapeDtypeStruct(src.shape, src.dtype), sems, [(0.0, start), (1.0, finish)])


def _join_halves(src, *, name):
    R, C = src.shape
    h = R // 2

    def body(src_ref, out_ref, send_sem, recv_sem):
        x, y, c = _position()
        mine, theirs = pl.ds(c * h, h), pl.ds((1 - c) * h, h)
        cp = pltpu.make_async_remote_copy(
            src_ref=src_ref.at[mine], dst_ref=out_ref.at[mine], send_sem=send_sem, recv_sem=recv_sem,
            device_id=(x, y, 1 - c), device_id_type=MESH)
        cp.start()
        pltpu.make_async_remote_copy(
            src_ref=src_ref.at[theirs], dst_ref=out_ref.at[theirs], send_sem=send_sem, recv_sem=recv_sem,
            device_id=(x, y, 1 - c), device_id_type=MESH).wait_recv()
        cp.wait_send()

    return pl.pallas_call(
        body, name=name, in_specs=[_HBM], out_specs=_HBM, out_shape=jax.ShapeDtypeStruct((R, C), src.dtype),
        input_output_aliases={0: 0},
        scratch_shapes=[pltpu.SemaphoreType.DMA, pltpu.SemaphoreType.DMA],
    )(src)


def _allreduce_small(vec, *, name):
    R, C = vec.shape

    def body(v_ref, o_ref, slots, send_sems, recv_sems):
        x, y, c = _position()
        me = 4 * x + 2 * y + c

        def peer(k):
            return x ^ ((k >> 2) & 1), y ^ ((k >> 1) & 1), c ^ (k & 1)

        def copy(k, slot):
            return pltpu.make_async_remote_copy(
                src_ref=v_ref, dst_ref=slots.at[slot], send_sem=send_sems.at[k - 1], recv_sem=recv_sems.at[k - 1],
                device_id=peer(k), device_id_type=MESH)

        slots[me] = v_ref[...]
        sends = [copy(k, me) for k in range(1, 8)]
        for cp in sends:
            cp.start()
        for k in range(1, 8):
            px, py, pc = peer(k)
            copy(k, 4 * px + 2 * py + pc).wait_recv()
        total = slots[0]
        for d in range(1, 8):
            total = total + slots[d]
        o_ref[...] = total
        for cp in sends:
            cp.wait_send()

    vmem = pl.BlockSpec(memory_space=pltpu.VMEM)
    return pl.pallas_call(
        body, name=name, in_specs=[vmem], out_specs=vmem, out_shape=jax.ShapeDtypeStruct((R, C), vec.dtype),
        scratch_shapes=[pltpu.VMEM((8, R, C), vec.dtype), pltpu.SemaphoreType.DMA((7,)), pltpu.SemaphoreType.DMA((7,))],
    )(vec)


def _cross_cfg(S, mem_len):
    return _Attn(T=S, Tk=mem_len, G=1, nh=X_HEADS, rep=1, dqk=X_HEAD_DIM, dv=X_HEAD_DIM, tq=512, tk=mem_len,
                 mode="none", scale=X_HEAD_DIM ** -0.5, qcol=lambda g: 0, kcol=lambda g: 0, vcol=lambda g: 1,
                 ocol=lambda g: 0, o_width=X_HEADS * X_HEAD_DIM)


def _swa_cfg(S):
    return _Attn(T=S, Tk=S, G=1, nh=SWA_HEADS, rep=SWA_HEADS // SWA_KV_HEADS, dqk=HEAD_DIM, dv=HEAD_DIM, tq=BLOCK,
                 tk=BLOCK, mode="band", max_dist=SWA_WINDOW - 1, scale=HEAD_DIM ** -0.5, qcol=lambda g: 0,
                 kcol=lambda g: 0, vcol=lambda g: 0, ocol=lambda g: 0, o_width=SWA_HEADS * HEAD_DIM)


MLA_FWD_GROUP = 8
MLA_BWD_GROUP = 4


def _mla_cfg(S, group):
    t = _tile(S, 512)
    return _Attn(T=S, Tk=S, G=MLA_HEADS // group, nh=group, rep=1, dqk=LANES, dv=MLA_V, tq=t, tk=t, mode="causal",
                 scale=(MLA_NOPE + MLA_ROPE) ** -0.5, qcol=lambda g: g, kcol=lambda g: g, vcol=lambda g: g,
                 ocol=lambda g: g, o_width=MLA_HEADS * MLA_V)


def _dil_cfg(S, window, dil):
    return _Attn(T=S // dil, Tk=S // dil, G=dil, nh=DIL_HEADS, rep=1, dqk=HEAD_DIM, dv=HEAD_DIM, tq=BLOCK, tk=BLOCK,
                 mode="band", max_dist=window // dil, scale=HEAD_DIM ** -0.5, qcol=lambda g: g, kcol=lambda g: g,
                 vcol=lambda g: g, ocol=lambda g: g, o_width=dil * DIL_HEADS * HEAD_DIM)


def _cross_fwd(p, x, mem, W, vec):
    S = x.shape[0]
    cfg = _cross_cfg(S, mem.shape[0])
    hx = _rmsnorm(x, vec[p + "x_norm"], name=p + "x_norm")
    qx = _mm(hx, W[p + "w_xq"], mode="nn", name=p + "xq", out_dtype=MXU_DTYPE)
    memn = _rmsnorm(mem, vec[p + "mem_norm"], name=p + "mem_norm")
    kvx = _mm(memn, W[p + "w_xkv"], mode="nn", name=p + "xkv", out_dtype=MXU_DTYPE)
    ox, lse = _attn_fwd(cfg, qx, kvx, kvx, name=p + "x_attn", out_dtype=MXU_DTYPE)
    out = _mm(ox, W[p + "w_xo"], mode="nn", name=p + "xo", res=x)
    return out, (x, hx, qx, memn, kvx, ox, lse)


def _cross_bwd(p, dx, saved, mem, W, vec, dW, dvec):
    x, hx, qx, memn, kvx, ox, lse = saved
    cfg = _cross_cfg(x.shape[0], mem.shape[0])
    dox = _mm(dx, W[p + "w_xo"], mode="nt", name=p + "xo_dx", out_dtype=MXU_DTYPE)
    dW[p + "w_xo"] = _dw(ox, dx, name=p + "xo_dw")
    delta, _ = _attn_delta(cfg, ox, dox, name=p + "x_delta")
    dqx = _attn_dq(cfg, qx, kvx, kvx, dox, lse, delta, name=p + "x_dq", out_dtype=MXU_DTYPE)
    dkx, dvx = _attn_dkv(cfg, qx, kvx, kvx, dox, lse, delta, name=p + "x_dkv", out_dtype=MXU_DTYPE)
    dkvx = jnp.concatenate([dkx, dvx], axis=1)
    dW[p + "w_xq"] = _dw(hx, dqx, name=p + "xq_dw")
    dW[p + "w_xkv"] = _dw(memn, dkvx, name=p + "xkv_dw")
    dmemn = _mm(dkvx, W[p + "w_xkv"], mode="nt", name=p + "xkv_dx")
    _, dvec[p + "mem_norm"] = _rmsnorm_bwd(mem, vec[p + "mem_norm"], dmemn, name=p + "mem_norm_bwd")
    dx_in, dvec[p + "x_norm"] = _dx_norm_bwd(dqx, W[p + "w_xq"], x, vec[p + "x_norm"], dx, name=p + "xq_dx")
    return dx_in


def _ffn_fwd(p, x, W, vec):
    hf = _rmsnorm(x, vec[p + "ffn_norm"], name=p + "ffn_norm")
    gate, up, act = _gate_up(hf, W[p + "w_gate"], W[p + "w_up"], name=p + "gate_up")
    out = _mm(act, W[p + "w_down"], mode="nn", name=p + "down", res=x)
    return out, (x, hf, gate, up, act)


def _ffn_bwd(p, dx, saved, W, vec, dW, dvec):
    x, hf, gate, up, act = saved
    dW[p + "w_down"] = _dw(act, dx, name=p + "down_dw")
    dgate, dup = _gate_up_bwd(dx, W[p + "w_down"], gate, up, name=p + "gate_up_bwd")
    dhf = _mm(dgate, W[p + "w_gate"], mode="nt", name=p + "gate_dx")
    dW[p + "w_gate"] = _dw(hf, dgate, name=p + "gate_dw")
    dW[p + "w_up"] = _dw(hf, dup, name=p + "up_dw")
    dx_in, dvec[p + "ffn_norm"] = _dx_norm_bwd(dup, W[p + "w_up"], x, vec[p + "ffn_norm"], dx, name=p + "up_dx",
                                               res=dhf)
    return dx_in


def _even_fwd(p, x, tabs, W, vec, comm=None):
    S = x.shape[0]
    h = _rmsnorm(x, vec[p + "mix_norm"], name=p + "mix_norm")
    z = _mm(h, W[p + "w_in"], mode="nn", name=p + "in")
    qa, ka, va, cqn, ckvn, kr = _l0_prep(z, tabs, vec[p + "q_norm"], vec[p + "kv_norm"], name=p + "prep")
    sink = jnp.pad(vec[p + "sinks"], (0, LANES - SWA_HEADS)).reshape(1, LANES)
    oa, lse_a = _band_fwd(_swa_cfg(S), qa, ka, va, name=p + "swa", sink=sink, out_dtype=MXU_DTYPE)
    qb = _mm(cqn, W[p + "w_uq"], mode="nn", name=p + "uq")
    kvb = _mm(ckvn, W[p + "w_ukv"], mode="nn", name=p + "ukv")
    Q, K, V = _mla_prep(qb, kvb, kr, tabs, name=p + "mla_prep")
    if comm is None:
        ob, lse_b = _causal_fwd(_mla_cfg(S, MLA_FWD_GROUP), Q, K, V, name=p + "mla", out_dtype=MXU_DTYPE, stat_heads=MLA_BWD_GROUP)
    else:
        ob, lse_b, gathered = _causal_fwd(_mla_cfg(S, MLA_FWD_GROUP), Q, K, V, name=p + "mla", out_dtype=MXU_DTYPE, stat_heads=MLA_BWD_GROUP,
                                          carry=comm.late_weights_exchange())
        W = {**W, **comm.late_weights(gathered)}
    o = jnp.concatenate([oa, ob], axis=1)
    out = _mm(o, W[p + "w_out"], mode="nn", name=p + "out", res=x)
    return out, (x, h, z, qa, ka, va, cqn, ckvn, sink, oa, lse_a, Q, K, V, ob, lse_b, o), W


def _even_bwd(p, dx, saved, tabs, W, vec, dW, dvec, comm=None):
    x, h, z, qa, ka, va, cqn, ckvn, sink, oa, lse_a, Q, K, V, ob, lse_b, o = saved
    S = x.shape[0]
    do = _mm(dx, W[p + "w_out"], mode="nt", name=p + "out_dx", out_dtype=MXU_DTYPE)
    dW[p + "w_out"] = _dw(o, dx, name=p + "out_dw")
    doa, dob = do[:, :SWA_HEADS * HEAD_DIM], do[:, SWA_HEADS * HEAD_DIM:]
    cfg = _swa_cfg(S)
    delta, dsink = _attn_delta(cfg, oa, doa, name=p + "swa_delta", lse=lse_a, sink=sink)
    dvec[p + "sinks"] = dsink
    dqa, dka, dva = _band_bwd(cfg, qa, ka, va, doa, lse_a, delta, name=p + "swa_bwd")
    cfg = _mla_cfg(S, MLA_BWD_GROUP)
    delta, _ = _attn_delta(cfg, ob, dob, name=p + "mla_delta")
    if comm is None:
        dQ, dK, dV = _causal_bwd(cfg, Q, K, V, dob, lse_b, delta, name=p + "mla_bwd")
    else:
        dQ, dK, dV, landed = _causal_bwd(cfg, Q, K, V, dob, lse_b, delta, name=p + "mla_bwd",
                                         carry=comm.late_grads_exchange(dW))
        comm.late_grads_landed(landed)
    dqb, dkvb, dkr = _mla_prep_bwd(dQ, dK, dV, tabs, name=p + "mla_prep_bwd")
    dcqn = _mm(dqb, W[p + "w_uq"], mode="nt", name=p + "uq_dx")
    dW[p + "w_uq"] = _dw(cqn, dqb, name=p + "uq_dw")
    dckvn = _mm(dkvb, W[p + "w_ukv"], mode="nt", name=p + "ukv_dx")
    dW[p + "w_ukv"] = _dw(ckvn, dkvb, name=p + "ukv_dw")
    dz, dvec[p + "q_norm"], dvec[p + "kv_norm"] = _l0_prep_bwd(
        z, tabs, vec[p + "q_norm"], vec[p + "kv_norm"], dqa, dka, dva, dcqn, dckvn, dkr, name=p + "prep_bwd")
    dW[p + "w_in"] = _dw(h, dz, name=p + "in_dw")
    dx_in, dvec[p + "mix_norm"] = _dx_norm_bwd(dz, W[p + "w_in"], x, vec[p + "mix_norm"], dx, name=p + "in_dx")
    return dx_in


def _odd_fwd(p, x, tabs, W, vec):
    S = x.shape[0]
    assert S % (DIL_PATTERNS[-1][1] * BLOCK) == 0, "keys past the end of the sequence are never attended"
    h = _rmsnorm(x, vec[p + "mix_norm"], name=p + "mix_norm")
    qkv = _mm(h, W[p + "w_qkv"], mode="nn", name=p + "qkv")
    qkv_by_d = _l1_prep(qkv, tabs, name=p + "prep")
    outs, lses = {}, {}
    for window, dil in DIL_PATTERNS:
        outs[dil], lses[dil] = _band_fwd(_dil_cfg(S, window, dil), *qkv_by_d[dil], name=p + "dil%d" % dil)
    o, w1, w4, w16 = _merge(outs, lses, name=p + "merge")
    out = _mm(o, W[p + "w_out"], mode="nn", name=p + "out", res=x)
    return out, (x, h, qkv_by_d, lses, dict(zip(DILATIONS, (w1, w4, w16))), o)


def _odd_bwd(p, dx, saved, tabs, W, vec, dW, dvec):
    x, h, qkv_by_d, lses, ws, o = saved
    S = x.shape[0]
    do = _mm(dx, W[p + "w_out"], mode="nt", name=p + "out_dx")
    dW[p + "w_out"] = _dw(o, dx, name=p + "out_dw")
    dos, deltas = _merge_bwd(do, o, ws, name=p + "merge_bwd")
    grads = {}
    for window, dil in DIL_PATTERNS:
        grads[dil] = _band_bwd(_dil_cfg(S, window, dil), *qkv_by_d[dil], dos[dil], lses[dil], deltas[dil],
                               name=p + "dil%d_bwd" % dil)
    dqkv = _l1_prep_bwd(grads, tabs, name=p + "prep_bwd")
    dW[p + "w_qkv"] = _dw(h, dqkv, name=p + "qkv_dw")
    dx_in, dvec[p + "mix_norm"] = _dx_norm_bwd(dqkv, W[p + "w_qkv"], x, vec[p + "mix_norm"], dx, name=p + "qkv_dx")
    return dx_in


def _local_step(x, mem, positions, target, W, vec, comm=None):
    tabs = _rope_tables(positions)
    x1, s_mix0, W = _even_fwd("l0_", x, tabs, W, vec, comm)
    x2, s_x0 = _cross_fwd("l0_", x1, mem, W, vec)
    x3, s_f0 = _ffn_fwd("l0_", x2, W, vec)
    x4, s_mix1 = _odd_fwd("l1_", x3, tabs, W, vec)
    x5, s_x1 = _cross_fwd("l1_", x4, mem, W, vec)
    x6, s_f1 = _ffn_fwd("l1_", x5, W, vec)
    dW, dvec = {}, {}
    dx, dvec["final_norm"], sq = _loss_head(x6, vec["final_norm"], target, name="loss_head")
    dx = _ffn_bwd("l1_", dx, s_f1, W, vec, dW, dvec)
    dx = _cross_bwd("l1_", dx, s_x1, mem, W, vec, dW, dvec)
    dx = _odd_bwd("l1_", dx, s_mix1, tabs, W, vec, dW, dvec)
    dx = _ffn_bwd("l0_", dx, s_f0, W, vec, dW, dvec)
    dx = _cross_bwd("l0_", dx, s_x0, mem, W, vec, dW, dvec)
    dx = _even_bwd("l0_", dx, s_mix0, tabs, W, vec, dW, dvec, comm)
    return sq, dx, dW, dvec


_LAYER_MATS = {
    0: [("w_in", "col"), ("w_uq", "col"), ("w_ukv", "col"), ("w_out", "row"), ("w_xq", "row"), ("w_xkv", "row"),
        ("w_xo", "col"), ("w_gate", "col"), ("w_up", "col"), ("w_down", "row")],
    1: [("w_qkv", "col"), ("w_out", "row"), ("w_xq", "row"), ("w_xkv", "row"), ("w_xo", "col"), ("w_gate", "col"),
        ("w_up", "col"), ("w_down", "row")],
}
MATS = [("l%d_%s" % (l, n), kind) for l in (0, 1) for n, kind in _LAYER_MATS[l]]
_LAYER_VECS = {0: ["mix_norm", "sinks", "q_norm", "kv_norm", "x_norm", "mem_norm", "ffn_norm"],
               1: ["mix_norm", "x_norm", "mem_norm", "ffn_norm"]}
VECS = ["l%d_%s" % (l, n) for l in (0, 1) for n in _LAYER_VECS[l]] + ["final_norm"]
WEIGHT_ORDER = (["l0_mix_norm", "l0_w_in", "l0_sinks", "l0_q_norm", "l0_w_uq", "l0_kv_norm", "l0_w_ukv", "l0_w_out",
                 "l0_x_norm", "l0_mem_norm", "l0_w_xq", "l0_w_xkv", "l0_w_xo", "l0_ffn_norm", "l0_w_gate", "l0_w_up",
                 "l0_w_down", "l1_mix_norm", "l1_w_qkv", "l1_w_out", "l1_x_norm", "l1_mem_norm", "l1_w_xq",
                 "l1_w_xkv", "l1_w_xo", "l1_ffn_norm", "l1_w_gate", "l1_w_up", "l1_w_down", "final_norm"])
PACK_COLS = 1024
PACK_ROW_TILE = 2 * SUM_ROW_TILE
VEC_ROWS = 16
LOSS_ROW = len(VECS)
N_CHIPS = 4


class _Group:
    def __init__(self, mats, shards):
        self.mats, self.shards = mats, shards
        self.layout, off = {}, 0
        for name, _ in mats:
            n = shards[name].size // PACK_COLS
            assert n * PACK_COLS == shards[name].size
            self.layout[name] = (off, n)
            off += n
        self.used = off
        self.rows = -(-off // PACK_ROW_TILE) * PACK_ROW_TILE

    def pack(self, tensors, dtype):
        parts = [tensors[name].astype(dtype).reshape(-1, PACK_COLS) for name, _ in self.mats]
        return jnp.concatenate(parts + [jnp.zeros((self.rows - self.used, PACK_COLS), dtype)], axis=0)

    def unpack(self, packed):
        return {name: packed[off:off + n].reshape(self.shards[name].shape) for name, (off, n) in self.layout.items()}

    def full_weights(self, gathered):
        W = {}
        for name, kind in self.mats:
            off, n = self.layout[name]
            r, cw = self.shards[name].shape
            blocks = gathered[:, off:off + n].reshape(N_CHIPS, r, cw)
            W[name] = blocks.reshape(N_CHIPS * r, cw) if kind == "row" else (
                jnp.transpose(blocks, (1, 0, 2)).reshape(r, N_CHIPS * cw))
        if "l0_w_in" in W:
            W["l0_w_in"] = jnp.pad(W["l0_w_in"], ((0, 0), (0, Z_END - W["l0_w_in"].shape[1])))
        if "l0_w_uq" in W:
            uq = W["l0_w_uq"].reshape(MLA_Q_RANK, MLA_HEADS, MLA_NOPE + MLA_ROPE)
            uq = jnp.pad(uq, ((0, 0), (0, 0), (0, LANES - MLA_NOPE - MLA_ROPE)))
            W["l0_w_uq"] = uq.reshape(MLA_Q_RANK, MLA_HEADS * LANES)
        return W

    def pack_grads(self, dW):
        parts = []
        for name, kind in self.mats:
            r, cw = self.shards[name].shape
            g = dW[name]
            if name == "l0_w_in":
                g = g[:, :Z_KR + MLA_ROPE]
            if name == "l0_w_uq":
                g = g.reshape(MLA_Q_RANK, MLA_HEADS, LANES)[:, :, :MLA_NOPE + MLA_ROPE].reshape(MLA_Q_RANK, -1)
            if kind == "col":
                g = jnp.transpose(g.reshape(r, N_CHIPS, cw), (1, 0, 2))
            parts.append(g.reshape(N_CHIPS, -1, PACK_COLS).astype(EXCHANGE_DTYPE))
        pad = jnp.zeros((N_CHIPS, self.rows - self.used, PACK_COLS), EXCHANGE_DTYPE)
        return jnp.concatenate(parts + [pad], axis=1)


def _pack_vecs(vecs):
    rows = [jnp.pad(vecs[n].reshape(-1).astype(F32), (0, PACK_COLS - vecs[n].size)) for n in VECS]
    rows += [jnp.zeros((PACK_COLS,), F32)] * (VEC_ROWS - len(rows))
    return jnp.stack(rows)


def _unpack_vecs(packed, like):
    return {n: packed[i, :like[n].size].reshape(like[n].shape) for i, n in enumerate(VECS)}


EARLY_MATS = [m for m in MATS if m[0] in ("l0_w_in", "l0_w_uq", "l0_w_ukv")]
LATE_MATS = [m for m in MATS if m not in EARLY_MATS]


class _StepComm:
    def __init__(self, shards):
        self.early, self.late = _Group(EARLY_MATS, shards), _Group(LATE_MATS, shards)
        self.half_index = lax.axis_index("c").astype(jnp.int32).reshape(1)
        self.late_grads = None

    def early_weights(self):
        src = self.early.pack(self.early.shards, MXU_DTYPE)
        return self.early.full_weights(_run_exchange(_gather_exchange(src), name="gather_early"))

    def late_weights_exchange(self):
        return _gather_exchange(self.late.pack(self.late.shards, MXU_DTYPE))

    def late_weights(self, gathered):
        return self.late.full_weights(gathered)

    def _chip_sum(self, group, dW, tag):
        grads = group.pack_grads(dW)
        theirs = _swap_other_half(grads, name="swap_other_half_" + tag)
        return _sum_cores(grads, theirs, self.half_index, name="sum_cores_" + tag)

    def _finish(self, parts, tag):
        return _join_halves(_sum_chips(parts, self.half_index, name="sum_chips_" + tag), name="join_halves_" + tag)

    def late_grads_exchange(self, dW):
        return _scatter_exchange(self._chip_sum(self.late, dW, "late"))

    def late_grads_landed(self, parts):
        self.late_grads = self._finish(parts, "late")

    def early_grads(self, dW):
        parts = _run_exchange(_scatter_exchange(self._chip_sum(self.early, dW, "early")), name="scatter_early")
        return self._finish(parts, "early")


def _step(a):
    weights = {n: a[n] for n in WEIGHT_ORDER}
    shards = {n: weights[n] for n, _ in MATS}
    vec = {n: weights[n] for n in VECS}
    comm = _StepComm(shards)
    sq, grad_x, dW, dvec = _local_step(a["x"][0], a["mem"][0], a["positions"], a["loss_target"][0],
                                       comm.early_weights(), vec, comm)

    dvec = dict(dvec)
    dvec["l0_sinks"] = dvec["l0_sinks"][0, :SWA_HEADS]
    small = _pack_vecs(dvec)
    small = small.at[LOSS_ROW, 0].set(0.5 / a["x"].shape[-1] * jnp.sum(sq))
    small = _allreduce_small(small, name="reduce_gains")
    loss = small[LOSS_ROW, 0]
    g_s = small.at[LOSS_ROW, 0].set(0.0)
    d_s, m_s, v_s = _adamw(_pack_vecs(vec), g_s, _pack_vecs({n: a["m_" + n] for n in VECS}),
                           _pack_vecs({n: a["v_" + n] for n in VECS}), name="adamw_gains")
    got = [_unpack_vecs(packed, vec) for packed in (g_s, d_s, m_s, v_s)]

    for group, g_w in ((comm.late, comm.late_grads), (comm.early, comm.early_grads(dW))):
        for n, g in group.unpack(g_w).items():
            results = (g,) + tuple(_adamw(shards[n], g, a["m_" + n], a["v_" + n], name="adamw_" + n))
            for kind, value in zip(got, results):
                kind[n] = value

    out = [loss, grad_x[None]]
    for kind in got:
        out += [kind[n] for n in WEIGHT_ORDER]
    return tuple(out)


def kernel(x, mem, positions, l0_mix_norm, l0_w_in, l0_sinks, l0_q_norm, l0_w_uq, l0_kv_norm, l0_w_ukv, l0_w_out, l0_x_norm, l0_mem_norm, l0_w_xq, l0_w_xkv, l0_w_xo, l0_ffn_norm, l0_w_gate, l0_w_up, l0_w_down, l1_mix_norm, l1_w_qkv, l1_w_out, l1_x_norm, l1_mem_norm, l1_w_xq, l1_w_xkv, l1_w_xo, l1_ffn_norm, l1_w_gate, l1_w_up, l1_w_down, final_norm, loss_target, m_l0_mix_norm, m_l0_w_in, m_l0_sinks, m_l0_q_norm, m_l0_w_uq, m_l0_kv_norm, m_l0_w_ukv, m_l0_w_out, m_l0_x_norm, m_l0_mem_norm, m_l0_w_xq, m_l0_w_xkv, m_l0_w_xo, m_l0_ffn_norm, m_l0_w_gate, m_l0_w_up, m_l0_w_down, m_l1_mix_norm, m_l1_w_qkv, m_l1_w_out, m_l1_x_norm, m_l1_mem_norm, m_l1_w_xq, m_l1_w_xkv, m_l1_w_xo, m_l1_ffn_norm, m_l1_w_gate, m_l1_w_up, m_l1_w_down, m_final_norm, v_l0_mix_norm, v_l0_w_in, v_l0_sinks, v_l0_q_norm, v_l0_w_uq, v_l0_kv_norm, v_l0_w_ukv, v_l0_w_out, v_l0_x_norm, v_l0_mem_norm, v_l0_w_xq, v_l0_w_xkv, v_l0_w_xo, v_l0_ffn_norm, v_l0_w_gate, v_l0_w_up, v_l0_w_down, v_l1_mix_norm, v_l1_w_qkv, v_l1_w_out, v_l1_x_norm, v_l1_mem_norm, v_l1_w_xq, v_l1_w_xkv, v_l1_w_xo, v_l1_ffn_norm, v_l1_w_gate, v_l1_w_up, v_l1_w_down, v_final_norm):
    return _step(dict(locals()))
```

```python
import functools

import jax
import jax.numpy as jnp
import numpy as np
from jax import lax
from jax.experimental import pallas as pl
from jax.experimental.pallas import tpu as pltpu

F32 = jnp.float32
MXU_DTYPE = jnp.bfloat16
LANES = 128
VMEM_LIMIT_BYTES = 56 * 1024 * 1024

NORM_EPS = 1e-6
ROPE_THETA = 10000.0
BLOCK = 128
HEAD_DIM = 64
SWA_HEADS, SWA_KV_HEADS, SWA_WINDOW = 8, 2, 128
MLA_HEADS, MLA_Q_RANK, MLA_KV_RANK, MLA_NOPE, MLA_ROPE, MLA_V = 8, 384, 256, 64, 32, 64
DIL_HEADS = 16
DIL_PATTERNS = ((128, 1), (512, 4), (2048, 16))
X_HEADS, X_HEAD_DIM = 4, 128
ADAM_LR, ADAM_B1, ADAM_B2, ADAM_EPS, ADAM_WD, ADAM_STEP = 0.001, 0.9, 0.999, 1e-08, 0.01, 10
MESH = pl.DeviceIdType.MESH
NEG_BIG = -1e30

NN = (((1,), (0,)), ((), ()))
NT = (((1,), (1,)), ((), ()))


def _dot(a, b, dims=NN):
    return lax.dot_general(a.astype(MXU_DTYPE), b.astype(MXU_DTYPE), dims, preferred_element_type=F32)


def _pcall(body, *, name, dims=None, **kw):
    params = pltpu.CompilerParams(dimension_semantics=dims, vmem_limit_bytes=VMEM_LIMIT_BYTES)
    return pl.pallas_call(body, name=name, compiler_params=params, **kw)


def _tile(n, pref):
    t = (min(pref, n) // LANES) * LANES
    while t >= LANES:
        if n % t == 0:
            return t
        t -= LANES
    return n


SUBLANES_PACKED = 16


def _row_tile(n, pref):
    t = (min(pref, n) // SUBLANES_PACKED) * SUBLANES_PACKED
    while t >= SUBLANES_PACKED:
        if n % t == 0:
            return t
        t -= SUBLANES_PACKED
    return n


def _lane(shape):
    return lax.broadcasted_iota(jnp.int32, shape, 1)


def _cols_to_lanes(cols, rows):
    lane = _lane((rows, LANES))
    out = jnp.zeros((rows, LANES), F32)
    for j, col in enumerate(cols):
        out = jnp.where(lane == j, col, out)
    return out


def _mm(a, b, *, mode, name, res=None, out_dtype=F32, tm=1408, tn=1536, tk=1408):
    if mode == "nn":
        (M, K), (K2, N) = a.shape, b.shape
    elif mode == "nt":
        (M, K), (N, K2) = a.shape, b.shape
    else:
        (K, M), (K2, N) = a.shape, b.shape
    assert K == K2, (a.shape, b.shape, mode)
    tm, tn, tk = _tile(M, tm), _tile(N, tn), _tile(K, tk)
    nk = K // tk
    in_place = out_dtype == F32 or nk == 1

    def body(*refs):
        refs = list(refs)
        a_ref, b_ref = refs[:2]
        r_ref = refs[2] if res is not None else None
        o_ref = refs[3 if res is not None else 2]
        acc = o_ref if in_place else refs[-1]
        k = pl.program_id(2)
        if mode == "nn":
            part = _dot(a_ref[...], b_ref[...], NN)
        elif mode == "nt":
            part = _dot(a_ref[...], b_ref[...], NT)
        else:
            part = _dot(a_ref[...].T, b_ref[...], NN)
        if nk == 1:
            o_ref[...] = (part if res is None else part + r_ref[...].astype(F32)).astype(o_ref.dtype)
            return

        @pl.when(k == 0)
        def _():
            acc[...] = part if res is None else part + r_ref[...].astype(F32)

        @pl.when(k > 0)
        def _():
            acc[...] += part

        if not in_place:
            @pl.when(k == nk - 1)
            def _():
                o_ref[...] = acc[...].astype(o_ref.dtype)

    if mode == "nn":
        a_spec = pl.BlockSpec((tm, tk), lambda i, j, k: (i, k))
        b_spec = pl.BlockSpec((tk, tn), lambda i, j, k: (k, j))
    elif mode == "nt":
        a_spec = pl.BlockSpec((tm, tk), lambda i, j, k: (i, k))
        b_spec = pl.BlockSpec((tn, tk), lambda i, j, k: (j, k))
    else:
        a_spec = pl.BlockSpec((tk, tm), lambda i, j, k: (k, i))
        b_spec = pl.BlockSpec((tk, tn), lambda i, j, k: (k, j))
    o_spec = pl.BlockSpec((tm, tn), lambda i, j, k: (i, j))
    in_specs = [a_spec, b_spec] + ([] if res is None else [o_spec])
    args = (a, b) + (() if res is None else (res,))
    return _pcall(
        body, name=name, dims=("parallel", "parallel", "arbitrary"),
        grid=(M // tm, N // tn, nk), in_specs=in_specs, out_specs=o_spec,
        out_shape=jax.ShapeDtypeStruct((M, N), out_dtype),
        scratch_shapes=[] if in_place else [pltpu.VMEM((tm, tn), F32)],
    )(*args)


EXCHANGE_DTYPE = jnp.bfloat16


def _dw(a, b, *, name):
    return _mm(a, b, mode="tn", name=name, out_dtype=EXCHANGE_DTYPE)


def _rms_parts(xf):
    r = lax.rsqrt(jnp.mean(xf * xf, axis=-1, keepdims=True) + NORM_EPS)
    return xf * r, r


def _rms_bwd_rows(xf, g, dy):
    xhat, r = _rms_parts(xf)
    dxhat = dy * g
    dx = r * (dxhat - xhat * jnp.mean(dxhat * xhat, axis=-1, keepdims=True))
    return dx, dy * xhat


def _dx_norm_bwd(a, w, x, g, dres, *, name, res=None, tm=1024, tk=1408):
    (M, K), N = a.shape, w.shape[0]
    tm, tk = _tile(M, tm), _tile(K, tk)
    nk = K // tk

    def body(*refs):
        refs = list(refs)
        a_ref, w_ref, x_ref, g_ref, dr_ref = refs[:5]
        r_ref = refs[5] if res is not None else None
        dx_ref, dg_ref = refs[-2:]
        i, k = pl.program_id(0), pl.program_id(1)
        part = _dot(a_ref[...], w_ref[...], NT)

        @pl.when(k == 0)
        def _():
            dx_ref[...] = part if res is None else part + r_ref[...]

        @pl.when(k > 0)
        def _():
            dx_ref[...] += part

        @pl.when(k == nk - 1)
        def _():
            dx, dgp = _rms_bwd_rows(x_ref[...], g_ref[...], dx_ref[...])
            dx_ref[...] = dx + dr_ref[...]

            @pl.when(i == 0)
            def _():
                dg_ref[...] = jnp.zeros_like(dg_ref)

            dg_ref[...] += jnp.sum(dgp, axis=0, keepdims=True)

    row = pl.BlockSpec((tm, N), lambda i, k: (i, 0))
    vec = pl.BlockSpec((1, N), lambda i, k: (0, 0))
    in_specs = [pl.BlockSpec((tm, tk), lambda i, k: (i, k)), pl.BlockSpec((N, tk), lambda i, k: (0, k)), row, vec, row]
    args = [a, w, x, g.reshape(1, N), dres]
    if res is not None:
        in_specs.append(row)
        args.append(res)
    return _pcall(
        body, name=name, dims=("arbitrary", "arbitrary"), grid=(M // tm, nk), in_specs=in_specs,
        out_specs=[row, vec], out_shape=[_sds((M, N)), _sds((1, N))],
    )(*args)


def _rmsnorm(x, g, *, name, out_dtype=MXU_DTYPE, tm=512):
    M, D = x.shape
    tm = _tile(M, tm)

    def body(x_ref, g_ref, o_ref):
        xhat, _ = _rms_parts(x_ref[...].astype(F32))
        o_ref[...] = (xhat * g_ref[...]).astype(o_ref.dtype)

    return _pcall(
        body, name=name, dims=("parallel",), grid=(M // tm,),
        in_specs=[pl.BlockSpec((tm, D), lambda i: (i, 0)), pl.BlockSpec((1, D), lambda i: (0, 0))],
        out_specs=pl.BlockSpec((tm, D), lambda i: (i, 0)),
        out_shape=jax.ShapeDtypeStruct((M, D), out_dtype),
    )(x, g.reshape(1, D))


def _rmsnorm_bwd(x, g, dy, *, name, dres=None, tm=512):
    M, D = x.shape
    tm = _tile(M, tm)

    def body(*refs):
        if dres is None:
            x_ref, g_ref, dy_ref, dx_ref, dg_ref = refs
        else:
            x_ref, g_ref, dy_ref, dr_ref, dx_ref, dg_ref = refs
        dx, dgp = _rms_bwd_rows(x_ref[...].astype(F32), g_ref[...], dy_ref[...].astype(F32))
        if dres is not None:
            dx = dx + dr_ref[...]
        dx_ref[...] = dx

        @pl.when(pl.program_id(0) == 0)
        def _():
            dg_ref[...] = jnp.zeros_like(dg_ref)

        dg_ref[...] += jnp.sum(dgp, axis=0, keepdims=True)

    row = pl.BlockSpec((tm, D), lambda i: (i, 0))
    vec = pl.BlockSpec((1, D), lambda i: (0, 0))
    in_specs = [row, vec, row] + ([] if dres is None else [row])
    args = (x, g.reshape(1, D), dy) + (() if dres is None else (dres,))
    return _pcall(
        body, name=name, dims=("arbitrary",), grid=(M // tm,), in_specs=in_specs, out_specs=[row, vec],
        out_shape=[jax.ShapeDtypeStruct((M, D), F32), jax.ShapeDtypeStruct((1, D), F32)],
    )(*args)


def _rope_chunk(t, c, s, half):
    lane = _lane(t.shape)
    swapped = jnp.where((lane % (2 * half)) < half, pltpu.roll(t, LANES - half, 1), pltpu.roll(t, half, 1))
    return t * c + swapped * s


def _rope_tables(positions):
    pos = positions.reshape(-1).astype(F32)[:, None]

    def table(dh, first, copies, sine, fill=0.0):
        half = dh // 2
        lane = np.arange(LANES)
        inside = (lane >= first) & (lane < first + copies * dh)
        idx = np.where(inside, (lane - first) % half, 0)
        inv_freq = ROPE_THETA ** (-jnp.asarray(2 * idx, F32) / dh)
        sign = np.where((lane - first) % dh < half, -1.0, 1.0) if sine else np.ones(LANES)
        ang = pos * inv_freq[None, :]
        val = (jnp.sin(ang) if sine else jnp.cos(ang)) * jnp.asarray(sign, F32)[None, :]
        return jnp.where(jnp.asarray(inside)[None, :], val, fill)

    return dict(
        c64=table(HEAD_DIM, 0, 2, False), s64=table(HEAD_DIM, 0, 2, True),
        ck=table(MLA_ROPE, 0, 1, False), sk=table(MLA_ROPE, 0, 1, True),
        cm=jnp.where(jnp.asarray(np.arange(LANES) < MLA_NOPE)[None, :], 1.0, table(MLA_ROPE, MLA_NOPE, 1, False)),
        sm=table(MLA_ROPE, MLA_NOPE, 1, True),
    )


class _Attn:
    def __init__(self, *, T, Tk, G, nh, rep, dqk, dv, tq, tk, mode, scale, qcol, kcol, vcol, ocol, o_width,
                 max_dist=0):
        self.__dict__.update(locals())
        self.nkv = nh // rep
        assert T % tq == 0 and Tk % tk == 0 and nh <= LANES


def _attn_delta(cfg, o, do, *, name, w=None, lse=None, sink=None, tm=512):
    c = cfg
    tm = _tile(c.T, tm)
    width = c.nh * c.dv

    def body(*refs):
        refs = list(refs)
        o_ref, do_ref = refs[:2]
        rest = refs[2:]
        w_ref = rest.pop(0) if w is not None else None
        lse_ref, sink_ref = (rest.pop(0), rest.pop(0)) if sink is not None else (None, None)
        d_ref = rest.pop(0)
        prod = o_ref[...].astype(F32) * do_ref[...].astype(F32)
        cols = [jnp.sum(prod[:, j * c.dv:(j + 1) * c.dv], axis=1, keepdims=True) for j in range(c.nh)]
        delta = _cols_to_lanes(cols, tm)
        if w is not None:
            delta = delta * w_ref[...]
        d_ref[...] = delta
        if sink is not None:
            ds_ref = rest.pop(0)

            @pl.when(pl.program_id(1) == 0)
            def _():
                ds_ref[...] = jnp.zeros_like(ds_ref)

            lane = _lane((tm, LANES))
            ps = jnp.where(lane < c.nh, jnp.exp(sink_ref[...] - lse_ref[...]), 0.0)
            ds_ref[...] -= jnp.sum(ps * delta, axis=0, keepdims=True)

    stat = pl.BlockSpec((tm, LANES), lambda g, i: (i, g))
    in_specs = [pl.BlockSpec((tm, width), lambda g, i: (i, c.ocol(g)))] * 2
    args = [o, do]
    out_specs, out_shape = [stat], [jax.ShapeDtypeStruct((c.T, LANES * c.G), F32)]
    if w is not None:
        in_specs.append(stat)
        args.append(w)
    if sink is not None:
        assert c.G == 1
        in_specs += [stat, pl.BlockSpec((1, LANES), lambda g, i: (0, 0))]
        args += [lse, sink]
        out_specs.append(pl.BlockSpec((1, LANES), lambda g, i: (0, 0)))
        out_shape.append(jax.ShapeDtypeStruct((1, LANES), F32))
    out = _pcall(
        body, name=name, dims=("arbitrary", "arbitrary"), grid=(c.G, c.T // tm),
        in_specs=in_specs, out_specs=out_specs, out_shape=out_shape,
    )(*args)
    return out if sink is not None else (out[0], None)


TN = (((0,), (0,)), ((), ()))


def _band_mask(c, i):
    key = lax.broadcasted_iota(jnp.int32, (2 * BLOCK, BLOCK), 0)
    qry = lax.broadcasted_iota(jnp.int32, (2 * BLOCK, BLOCK), 1)
    d = BLOCK + qry - key
    return (d >= 0) & (d <= c.max_dist) & ((key >= BLOCK) | (i > 0))


def _head_pairs(c):
    return c.rep == 1 and c.dqk == c.dv == LANES // 2 and c.nh % 2 == 0


def _block_diagonal(pair):
    lane = _lane(pair.shape)
    zero = jnp.zeros_like(pair)
    return jnp.concatenate([jnp.where(lane < LANES // 2, pair, zero), jnp.where(lane >= LANES // 2, pair, zero)], axis=0)


def _own_blocks(t):
    n = t.shape[1] // 2
    rows = lax.broadcasted_iota(jnp.int32, (LANES, n), 0)
    return jnp.where(rows < LANES // 2, t[:, :n], t[:, n:])


def _rows_to_stats(rows, n):
    return jnp.concatenate(rows + [jnp.zeros((LANES - len(rows), n), F32)], axis=0).T


def _band_fwd(cfg, q, k, v, *, name, sink=None, out_dtype=F32):
    c = cfg
    assert c.mode == "band" and c.tq == c.tk == BLOCK and c.T == c.Tk
    nq = c.T // BLOCK

    def body(*refs):
        if sink is None:
            q_ref, kp_ref, kc_ref, vp_ref, vc_ref, o_ref, lse_ref = refs
        else:
            q_ref, kp_ref, kc_ref, vp_ref, vc_ref, sink_ref, o_ref, lse_ref = refs
        mask = _band_mask(c, pl.program_id(1))
        k2 = jnp.concatenate([kp_ref[...], kc_ref[...]], axis=0)
        v2 = jnp.concatenate([vp_ref[...], vc_ref[...]], axis=0)
        lses = []
        if _head_pairs(c):
            mask2 = jnp.concatenate([mask, mask], axis=1)
            pair_lanes = [slice(pc * LANES, (pc + 1) * LANES) for pc in range(c.nh // 2)]
            score = lambda sl: _dot(k2[:, sl], _block_diagonal(q_ref[:, sl]), NT)
            ahead, behind = score(pair_lanes[0]), None

            def finish(entry):
                sl, o_t, l = entry
                o_ref[:, sl] = _own_blocks(o_t / l).T.astype(o_ref.dtype)

            for pc, sl in enumerate(pair_lanes):
                s = ahead * c.scale
                if pc + 1 < len(pair_lanes):
                    ahead = score(pair_lanes[pc + 1])
                s = jnp.where(mask2, s, -jnp.inf)
                m = jnp.max(s, axis=0, keepdims=True)
                p = jnp.exp(s - m)
                l = jnp.sum(p, axis=0, keepdims=True)
                if behind is not None:
                    finish(behind)
                behind = (sl, _dot(v2[:, sl], p, TN), l)
                lse = m + jnp.log(l)
                lses += [lse[:, :BLOCK], lse[:, BLOCK:]]
            finish(behind)
        heads = [] if _head_pairs(c) else list(range(c.nh))
        score_of = lambda j: _dot(k2[:, (j // c.rep) * c.dqk:(j // c.rep + 1) * c.dqk],
                                  q_ref[:, j * c.dqk:(j + 1) * c.dqk], NT)
        ahead = score_of(0) if heads else None
        for j in heads:
            g = j // c.rep
            s = ahead * c.scale
            if j + 1 < c.nh:
                ahead = score_of(j + 1)
            s = jnp.where(mask, s, -jnp.inf)
            m = jnp.max(s, axis=0, keepdims=True)
            if sink is not None:
                sk = sink_ref[:, j:j + 1]
                m = jnp.maximum(m, sk)
            p = jnp.exp(s - m)
            l = jnp.sum(p, axis=0, keepdims=True)
            if sink is not None:
                l = l + jnp.exp(sk - m)
            o_t = _dot(v2[:, g * c.dv:(g + 1) * c.dv], p, TN)
            o_ref[:, j * c.dv:(j + 1) * c.dv] = (o_t / l).T.astype(o_ref.dtype)
            lses.append(m + jnp.log(l))
        lse_ref[...] = _rows_to_stats(lses, BLOCK)

    prev = lambda i: jnp.maximum(i - 1, 0)
    kw, vw = c.nkv * c.dqk, c.nkv * c.dv
    in_specs = [
        pl.BlockSpec((BLOCK, c.nh * c.dqk), lambda g, i: (i, c.qcol(g))),
        pl.BlockSpec((BLOCK, kw), lambda g, i: (prev(i), c.kcol(g))),
        pl.BlockSpec((BLOCK, kw), lambda g, i: (i, c.kcol(g))),
        pl.BlockSpec((BLOCK, vw), lambda g, i: (prev(i), c.vcol(g))),
        pl.BlockSpec((BLOCK, vw), lambda g, i: (i, c.vcol(g))),
    ]
    args = [q, k, k, v, v]
    if sink is not None:
        in_specs.append(pl.BlockSpec((1, LANES), lambda g, i: (0, 0)))
        args.append(sink)
    return _pcall(
        body, name=name, dims=("parallel", "parallel"), grid=(c.G, nq), in_specs=in_specs,
        out_specs=[pl.BlockSpec((BLOCK, c.nh * c.dv), lambda g, i: (i, c.ocol(g))),
                   pl.BlockSpec((BLOCK, LANES), lambda g, i: (i, g))],
        out_shape=[jax.ShapeDtypeStruct((c.T, c.o_width), out_dtype),
                   jax.ShapeDtypeStruct((c.T, LANES * c.G), F32)],
    )(*args)


def _band_bwd(cfg, q, k, v, do, lse, delta, *, name):
    c = cfg
    assert c.mode == "band" and c.tq == c.tk == BLOCK and c.T == c.Tk
    nq = c.T // BLOCK
    qw, kw, vw = c.nh * c.dqk, c.nkv * c.dqk, c.nkv * c.dv

    def body(q_ref, kp_ref, kc_ref, vp_ref, vc_ref, do_ref, lse_ref, d_ref, dq_ref, dk_ref, dv_ref, dk_c, dv_c):
        n = pl.program_id(1)

        @pl.when(n == 0)
        def _():
            dk_c[...] = jnp.zeros_like(dk_c)
            dv_c[...] = jnp.zeros_like(dv_c)

        @pl.when(n < nq)
        def _():
            mask = _band_mask(c, n)
            k2 = jnp.concatenate([kp_ref[...], kc_ref[...]], axis=0)
            v2 = jnp.concatenate([vp_ref[...], vc_ref[...]], axis=0)
            lse_t, d_t = lse_ref[...].T, d_ref[...].T
            if _head_pairs(c):
                mask2 = jnp.concatenate([mask, mask], axis=1)
                pair_lanes = [slice(pc * LANES, (pc + 1) * LANES) for pc in range(c.nh // 2)]

                def first(sl):
                    q_bd, do_bd = _block_diagonal(q_ref[:, sl]), _block_diagonal(do_ref[:, sl])
                    return q_bd, do_bd, k2[:, sl], _dot(k2[:, sl], q_bd, NT), _dot(v2[:, sl], do_bd, NT)

                def finish(entry):
                    sl, dq_t, dv_pair, dk_pair = entry
                    dq_ref[:, sl] = _own_blocks(dq_t).T
                    dk_ref[:, sl] = dk_c[:, sl] + dk_pair[:BLOCK]
                    dv_ref[:, sl] = dv_c[:, sl] + dv_pair[:BLOCK]
                    dk_c[:, sl] = dk_pair[BLOCK:]
                    dv_c[:, sl] = dv_pair[BLOCK:]

                ahead, behind = first(pair_lanes[0]), None
                for pc, sl in enumerate(pair_lanes):
                    q_bd, do_bd, kp, s, dp = ahead
                    if pc + 1 < len(pair_lanes):
                        ahead = first(pair_lanes[pc + 1])
                    both = lambda t: jnp.concatenate([t[2 * pc:2 * pc + 1, :], t[2 * pc + 1:2 * pc + 2, :]], axis=1)
                    p = jnp.exp(jnp.where(mask2, s * c.scale, -jnp.inf) - both(lse_t))
                    ds = p * (dp - both(d_t)) * c.scale
                    entry = (sl, _dot(kp, ds, TN), _dot(p, do_bd, NN), _dot(ds, q_bd, NN))
                    if behind is not None:
                        finish(behind)
                    behind = entry
                finish(behind)
                return
            dk2, dv2 = [None] * c.nkv, [None] * c.nkv

            def first_of(j):
                g = j // c.rep
                qh, doh = q_ref[:, j * c.dqk:(j + 1) * c.dqk], do_ref[:, j * c.dv:(j + 1) * c.dv]
                kh = k2[:, g * c.dqk:(g + 1) * c.dqk]
                return qh, doh, kh, _dot(kh, qh, NT), _dot(v2[:, g * c.dv:(g + 1) * c.dv], doh, NT)

            ahead = first_of(0)
            for j in range(c.nh):
                g = j // c.rep
                qh, doh, kh, s, dp = ahead
                if j + 1 < c.nh:
                    ahead = first_of(j + 1)
                p = jnp.exp(jnp.where(mask, s * c.scale, -jnp.inf) - lse_t[j:j + 1, :])
                ds = p * (dp - d_t[j:j + 1, :]) * c.scale
                dq_ref[:, j * c.dqk:(j + 1) * c.dqk] = _dot(kh, ds, TN).T
                dvh, dkh = _dot(p, doh, NN), _dot(ds, qh, NN)
                dv2[g] = dvh if dv2[g] is None else dv2[g] + dvh
                dk2[g] = dkh if dk2[g] is None else dk2[g] + dkh
            for g in range(c.nkv):
                ks, vs = slice(g * c.dqk, (g + 1) * c.dqk), slice(g * c.dv, (g + 1) * c.dv)
                dk_ref[:, ks] = dk_c[:, ks] + dk2[g][:BLOCK]
                dv_ref[:, vs] = dv_c[:, vs] + dv2[g][:BLOCK]
                dk_c[:, ks] = dk2[g][BLOCK:]
                dv_c[:, vs] = dv2[g][BLOCK:]

        @pl.when(n == nq)
        def _():
            dk_ref[...] = dk_c[...]
            dv_ref[...] = dv_c[...]

    cur = lambda n: jnp.minimum(n, nq - 1)
    prev = lambda n: jnp.maximum(cur(n) - 1, 0)
    out_blk = lambda n: jnp.maximum(n - 1, 0)
    stat = pl.BlockSpec((BLOCK, LANES), lambda g, n: (cur(n), g))
    dq_spec = pl.BlockSpec((BLOCK, qw), lambda g, n: (cur(n), g))
    dk_spec = pl.BlockSpec((BLOCK, kw), lambda g, n: (out_blk(n), g))
    dv_spec = pl.BlockSpec((BLOCK, vw), lambda g, n: (out_blk(n), g))
    in_specs = [
        pl.BlockSpec((BLOCK, qw), lambda g, n: (cur(n), c.qcol(g))),
        pl.BlockSpec((BLOCK, kw), lambda g, n: (prev(n), c.kcol(g))),
        pl.BlockSpec((BLOCK, kw), lambda g, n: (cur(n), c.kcol(g))),
        pl.BlockSpec((BLOCK, vw), lambda g, n: (prev(n), c.vcol(g))),
        pl.BlockSpec((BLOCK, vw), lambda g, n: (cur(n), c.vcol(g))),
        pl.BlockSpec((BLOCK, c.nh * c.dv), lambda g, n: (cur(n), c.ocol(g))),
        stat, stat,
    ]
    return _pcall(
        body, name=name, dims=("parallel", "arbitrary"), grid=(c.G, nq + 1), in_specs=in_specs,
        out_specs=[dq_spec, dk_spec, dv_spec],
        out_shape=[_sds((c.T, c.G * qw)), _sds((c.T, c.G * kw)), _sds((c.T, c.G * vw))],
        scratch_shapes=[pltpu.VMEM((BLOCK, kw), F32), pltpu.VMEM((BLOCK, vw), F32)],
    )(q, k, k, v, v, do, lse, delta)


def _causal_pairs(n, kv_major):
    pairs =[(i, j) for j in range(n) for i in range(j, n)] if kv_major else [(i, j) for i in range(n) for j in range(i + 1)]
    return jnp.asarray(np.array([p[0] for p in pairs], np.int32)), jnp.asarray(np.array([p[1] for p in pairs], np.int32))


def _causal_mask(t):
    return lax.broadcasted_iota(jnp.int32, (t, t), 0) >= lax.broadcasted_iota(jnp.int32, (t, t), 1)


def _carrying(body, n_in, n_out, n_scratch, grid, carry):
    if carry is None:
        return body
    G, P = grid

    def wrapped(*refs):
        refs = list(refs)
        prefetch, refs = refs[:2], refs[2:]
        ins, src = refs[:n_in], refs[n_in]
        outs, out = refs[n_in + 1:n_in + 1 + n_out], refs[n_in + 1 + n_out]
        scratch, sems = refs[n_in + 2 + n_out:n_in + 2 + n_out + n_scratch], refs[n_in + 2 + n_out + n_scratch:]
        step = pl.program_id(0) * P + pl.program_id(1)
        carry.run([src, out] + sems, step, G * P, at_end=False)
        body(*prefetch, *ins, *outs, *scratch)
        carry.run([src, out] + sems, step, G * P, at_end=True)

    return wrapped


def _carry_specs(carry):
    if carry is None:
        return [], [], [], [], []
    any_space = pl.BlockSpec(memory_space=pl.ANY)
    return [any_space], [any_space], [carry.out_shape], list(carry.sems), [carry.src]


def _causal_fwd(cfg, q, k, v, *, name, out_dtype=F32, stat_heads=None, carry=None):
    c = cfg
    assert c.mode == "causal" and c.tq == c.tk and c.T == c.Tk
    t, n = c.tq, c.T // c.tq
    stat_heads = stat_heads or c.nh
    stat_blocks = c.nh // stat_heads
    assert stat_blocks * stat_heads == c.nh
    qi_tab, kj_tab = _causal_pairs(n, kv_major=False)
    n_pairs = int(qi_tab.shape[0])

    def body(qi_ref, kj_ref, q_ref, k_ref, v_ref, o_ref, lse_ref, m_scr, l_scr, acc):
        pair = pl.program_id(1)
        qi, kj = qi_ref[pair], kj_ref[pair]

        @pl.when(kj == 0)
        def _():
            m_scr[...] = jnp.full_like(m_scr, NEG_BIG)
            l_scr[...] = jnp.zeros_like(l_scr)
            acc[...] = jnp.zeros_like(acc)

        def step(diagonal):
            mask = None
            if diagonal:
                mask = lax.broadcasted_iota(jnp.int32, (t, t), 1) >= lax.broadcasted_iota(jnp.int32, (t, t), 0)
            scores = [_dot(k_ref[:, (j // c.rep) * c.dqk:(j // c.rep + 1) * c.dqk],
                           q_ref[:, j * c.dqk:(j + 1) * c.dqk], NT) for j in range(c.nh)]
            for j in range(c.nh):
                g = j // c.rep
                s = scores[j] * c.scale
                if diagonal:
                    s = jnp.where(mask, s, -jnp.inf)
                m_prev = m_scr[j]
                m_new = jnp.maximum(m_prev, jnp.max(s, axis=0, keepdims=True))
                alpha = jnp.exp(m_prev - m_new)
                p = jnp.exp(s - m_new)
                l_scr[j] = alpha * l_scr[j] + jnp.sum(p, axis=0, keepdims=True)
                acc[j] = alpha * acc[j] + _dot(v_ref[:, g * c.dv:(g + 1) * c.dv], p, TN)
                m_scr[j] = m_new

        pl.when(kj == qi)(lambda: step(True))
        pl.when(kj != qi)(lambda: step(False))

        @pl.when(kj == qi)
        def _():
            rows = []
            for j in range(c.nh):
                o_ref[:, j * c.dv:(j + 1) * c.dv] = (acc[j] / l_scr[j]).T.astype(o_ref.dtype)
                rows.append(m_scr[j] + jnp.log(l_scr[j]))
            for b in range(stat_blocks):
                lse_ref[:, b * LANES:(b + 1) * LANES] = _rows_to_stats(rows[b * stat_heads:(b + 1) * stat_heads], t)

    x_in, x_out, x_shapes, x_scratch, x_args = _carry_specs(carry)
    grid_spec = pltpu.PrefetchScalarGridSpec(
        num_scalar_prefetch=2, grid=(c.G, n_pairs),
        in_specs=[pl.BlockSpec((t, c.nh * c.dqk), lambda g, p, qi, kj: (qi[p], c.qcol(g))),
                  pl.BlockSpec((t, c.nkv * c.dqk), lambda g, p, qi, kj: (kj[p], c.kcol(g))),
                  pl.BlockSpec((t, c.nkv * c.dv), lambda g, p, qi, kj: (kj[p], c.vcol(g)))] + x_in,
        out_specs=[pl.BlockSpec((t, c.nh * c.dv), lambda g, p, qi, kj: (qi[p], c.ocol(g))),
                   pl.BlockSpec((t, LANES * stat_blocks), lambda g, p, qi, kj: (qi[p], g))] + x_out,
        scratch_shapes=[pltpu.VMEM((c.nh, 1, t), F32), pltpu.VMEM((c.nh, 1, t), F32),
                        pltpu.VMEM((c.nh, c.dv, t), F32)] + x_scratch)
    return _pcall(
        _carrying(body, 3, 2, 3, (c.G, n_pairs), carry), name=name,
        dims=("arbitrary", "arbitrary") if carry is not None else ("parallel", "arbitrary"), grid_spec=grid_spec,
        out_shape=[jax.ShapeDtypeStruct((c.T, c.o_width), out_dtype),
                   jax.ShapeDtypeStruct((c.T, LANES * c.G * stat_blocks), F32)] + x_shapes,
    )(qi_tab, kj_tab, q, k, v, *x_args)


def _causal_bwd(cfg, q, k, v, do, lse, delta, *, name, carry=None):
    c = cfg
    assert c.mode == "causal" and c.tq == c.tk and c.T == c.Tk
    t, n = c.tq, c.T // c.tq
    qw, kw, vw = c.nh * c.dqk, c.nkv * c.dqk, c.nkv * c.dv
    qi_tab, kj_tab = _causal_pairs(n, kv_major=True)

    def body(qi_ref, kj_ref, q_ref, k_ref, v_ref, do_ref, lse_ref, d_ref, dq_ref, dk_ref, dv_ref, dk_acc, dv_acc):
        pair = pl.program_id(1)
        qi, kj = qi_ref[pair], kj_ref[pair]

        @pl.when(pair == 0)
        def _():
            dq_ref[...] = jnp.zeros_like(dq_ref)

        @pl.when(qi == kj)
        def _():
            dk_acc[...] = jnp.zeros_like(dk_acc)
            dv_acc[...] = jnp.zeros_like(dv_acc)

        rows = pl.ds(pl.multiple_of(qi * t, t), t)

        def step(diagonal):
            mask = _causal_mask(t) if diagonal else None
            for j in range(c.nh):
                g = j // c.rep
                qs, ks, vs = (slice(j * c.dqk, (j + 1) * c.dqk), slice(g * c.dqk, (g + 1) * c.dqk),
                              slice(g * c.dv, (g + 1) * c.dv))
                qh, doh, kh = q_ref[:, qs], do_ref[:, j * c.dv:(j + 1) * c.dv], k_ref[:, ks]
                s = _dot(qh, kh, NT) * c.scale
                if diagonal:
                    s = jnp.where(mask, s, -jnp.inf)
                p = jnp.exp(s - lse_ref[:, j:j + 1])
                ds = p * (_dot(doh, v_ref[:, vs], NT) - d_ref[:, j:j + 1]) * c.scale
                dq_ref[rows, qs] += _dot(ds, kh, NN)
                dv_acc[g] += _dot(doh, p, TN)
                dk_acc[g] += _dot(qh, ds, TN)

        pl.when(qi == kj)(lambda: step(True))
        pl.when(qi != kj)(lambda: step(False))

        @pl.when(qi == n - 1)
        def _():
            for g in range(c.nkv):
                dk_ref[:, g * c.dqk:(g + 1) * c.dqk] = dk_acc[g].T
                dv_ref[:, g * c.dv:(g + 1) * c.dv] = dv_acc[g].T

    stat = pl.BlockSpec((t, LANES), lambda g, p, qi, kj: (qi[p], g))
    o_spec = pl.BlockSpec((t, c.nh * c.dv), lambda g, p, qi, kj: (qi[p], c.ocol(g)))
    n_pairs = int(qi_tab.shape[0])
    x_in, x_out, x_shapes, x_scratch, x_args = _carry_specs(carry)
    grid_spec = pltpu.PrefetchScalarGridSpec(
        num_scalar_prefetch=2, grid=(c.G, n_pairs),
        in_specs=[pl.BlockSpec((t, qw), lambda g, p, qi, kj: (qi[p], c.qcol(g))),
                  pl.BlockSpec((t, kw), lambda g, p, qi, kj: (kj[p], c.kcol(g))),
                  pl.BlockSpec((t, vw), lambda g, p, qi, kj: (kj[p], c.vcol(g))),
                  o_spec, stat, stat] + x_in,
        out_specs=[pl.BlockSpec((c.T, qw), lambda g, p, qi, kj: (0, g)),
                   pl.BlockSpec((t, kw), lambda g, p, qi, kj: (kj[p], g)),
                   pl.BlockSpec((t, vw), lambda g, p, qi, kj: (kj[p], g))] + x_out,
        scratch_shapes=[pltpu.VMEM((c.nkv, c.dqk, t), F32), pltpu.VMEM((c.nkv, c.dv, t), F32)] + x_scratch)
    return _pcall(
        _carrying(body, 6, 3, 2, (c.G, n_pairs), carry), name=name,
        dims=("arbitrary", "arbitrary") if carry is not None else ("parallel", "arbitrary"), grid_spec=grid_spec,
        out_shape=[_sds((c.T, c.G * qw)), _sds((c.T, c.G * kw)), _sds((c.T, c.G * vw))] + x_shapes,
    )(qi_tab, kj_tab, q, k, v, do, lse, delta, *x_args)


def _rowwise(body, ins, outs, *, name, rows, tm=512, accs=(), scratch=()):
    tm = _row_tile(rows, tm)

    def spec(a):
        if a.shape[0] == 1:
            return pl.BlockSpec((1, a.shape[1]), lambda i: (0, 0))
        d = rows // a.shape[0]
        assert d * a.shape[0] == rows and tm % d == 0
        return pl.BlockSpec((tm // d, a.shape[1]), lambda i: (i, 0))

    return _pcall(
        functools.partial(body, tm), name=name, dims=("arbitrary" if accs else "parallel",), grid=(rows // tm,),
        in_specs=[spec(a) for a in ins], out_specs=[spec(a) for a in outs], out_shape=list(outs),
        scratch_shapes=list(scratch),
    )(*ins)


def _sds(shape, dtype=F32):
    return jax.ShapeDtypeStruct(shape, dtype)


def _acc_rows(ref, val):
    @pl.when(pl.program_id(0) == 0)
    def _():
        ref[...] = jnp.zeros_like(ref)

    ref[...] += jnp.sum(val, axis=0, keepdims=True)


Z_QA, Z_KA, Z_VA, Z_CQ, Z_CKV, Z_KR, Z_END = 0, 512, 640, 768, 1152, 1408, 1536


def _l0_prep(z, tabs, q_norm, kv_norm, *, name):
    S = z.shape[0]

    def body(tm, z_ref, c64, s64, ck, sk, gq, gkv, qa_o, ka_o, va_o, cq_o, ckv_o, kr_o):
        for i in range(4):
            sl = slice(Z_QA + i * LANES, Z_QA + (i + 1) * LANES)
            qa_o[:, i * LANES:(i + 1) * LANES] = _rope_chunk(z_ref[:, sl], c64[...], s64[...], 32).astype(qa_o.dtype)
        ka_o[...] = _rope_chunk(z_ref[:, Z_KA:Z_VA], c64[...], s64[...], 32).astype(ka_o.dtype)
        va_o[...] = z_ref[:, Z_VA:Z_CQ].astype(va_o.dtype)
        cq_o[...] = (_rms_parts(z_ref[:, Z_CQ:Z_CKV])[0] * gq[...]).astype(cq_o.dtype)
        ckv_o[...] = (_rms_parts(z_ref[:, Z_CKV:Z_KR])[0] * gkv[...]).astype(ckv_o.dtype)
        kr_o[...] = _rope_chunk(z_ref[:, Z_KR:Z_END], ck[...], sk[...], 16)

    outs = [_sds((S, 512), MXU_DTYPE), _sds((S, 128), MXU_DTYPE), _sds((S, 128), MXU_DTYPE),
            _sds((S, MLA_Q_RANK), MXU_DTYPE), _sds((S, MLA_KV_RANK), MXU_DTYPE), _sds((S, LANES))]
    ins = [z, tabs["c64"], tabs["s64"], tabs["ck"], tabs["sk"], q_norm.reshape(1, -1), kv_norm.reshape(1, -1)]
    return _rowwise(body, ins, outs, name=name, rows=S)


def _l0_prep_bwd(z, tabs, q_norm, kv_norm, dqa, dka, dva, dcq, dckv, dkr, *, name):
    S = z.shape[0]

    def body(tm, z_ref, c64, s64, ck, sk, gq, gkv, dqa_r, dka_r, dva_r, dcq_r, dckv_r, dkr_r, dz_o, dgq_o, dgkv_o):
        for i in range(4):
            sl = slice(i * LANES, (i + 1) * LANES)
            dz_o[:, sl] = _rope_chunk(dqa_r[:, sl].astype(F32), c64[...], -s64[...], 32).astype(dz_o.dtype)
        dz_o[:, Z_KA:Z_VA] = _rope_chunk(dka_r[...].astype(F32), c64[...], -s64[...], 32).astype(dz_o.dtype)
        dz_o[:, Z_VA:Z_CQ] = dva_r[...].astype(dz_o.dtype)
        dx, dgp = _rms_bwd_rows(z_ref[:, Z_CQ:Z_CKV], gq[...], dcq_r[...].astype(F32))
        dz_o[:, Z_CQ:Z_CKV] = dx.astype(dz_o.dtype)
        _acc_rows(dgq_o, dgp)
        dx, dgp = _rms_bwd_rows(z_ref[:, Z_CKV:Z_KR], gkv[...], dckv_r[...].astype(F32))
        dz_o[:, Z_CKV:Z_KR] = dx.astype(dz_o.dtype)
        _acc_rows(dgkv_o, dgp)
        dz_o[:, Z_KR:Z_END] = _rope_chunk(dkr_r[...], ck[...], -sk[...], 16).astype(dz_o.dtype)

    outs = [_sds((S, Z_END), MXU_DTYPE), _sds((1, MLA_Q_RANK)), _sds((1, MLA_KV_RANK))]
    ins = [z, tabs["c64"], tabs["s64"], tabs["ck"], tabs["sk"], q_norm.reshape(1, -1), kv_norm.reshape(1, -1),
           dqa, dka, dva, dcq, dckv, dkr]
    return _rowwise(body, ins, outs, name=name, rows=S, accs=(1, 2))


def _mla_prep(qb, kvb, kr, tabs, *, name):
    S = qb.shape[0]

    def body(tm, qb_r, kvb_r, kr_r, cm, sm, q_o, k_o, v_o):
        lane = _lane((tm, LANES))
        kr_at_64 = pltpu.roll(kr_r[...], 64, 1)
        for h in range(MLA_HEADS):
            sl = slice(h * LANES, (h + 1) * LANES)
            q_o[:, sl] = _rope_chunk(qb_r[:, sl], cm[...], sm[...], 16).astype(q_o.dtype)
            k_o[:, sl] = jnp.where(lane < 64, kvb_r[:, sl], kr_at_64).astype(k_o.dtype)
        for p in range(MLA_HEADS // 2):
            even = pltpu.roll(kvb_r[:, (2 * p) * LANES:(2 * p + 1) * LANES], 64, 1)
            odd = kvb_r[:, (2 * p + 1) * LANES:(2 * p + 2) * LANES]
            v_o[:, p * LANES:(p + 1) * LANES] = jnp.where(lane < 64, even, odd).astype(v_o.dtype)

    outs = [_sds((S, 1024), MXU_DTYPE), _sds((S, 1024), MXU_DTYPE), _sds((S, 512), MXU_DTYPE)]
    return _rowwise(body, [qb, kvb, kr, tabs["cm"], tabs["sm"]], outs, name=name, rows=S)


def _mla_prep_bwd(dq, dk, dv, tabs, *, name):
    S = dq.shape[0]

    def body(tm, dq_r, dk_r, dv_r, cm, sm, dqb_o, dkvb_o, dkr_o):
        lane = _lane((tm, LANES))
        dkr = jnp.zeros((tm, LANES), F32)
        for h in range(MLA_HEADS):
            sl = slice(h * LANES, (h + 1) * LANES)
            dqb_o[:, sl] = _rope_chunk(dq_r[:, sl].astype(F32), cm[...], -sm[...], 16).astype(dqb_o.dtype)
            dkh = dk_r[:, sl].astype(F32)
            dvp = dv_r[:, (h // 2) * LANES:(h // 2 + 1) * LANES].astype(F32)
            dvh = pltpu.roll(dvp, 64, 1) if h % 2 == 0 else dvp
            dkvb_o[:, sl] = jnp.where(lane < 64, dkh, dvh).astype(dkvb_o.dtype)
            dkr = dkr + pltpu.roll(dkh, 64, 1)
        dkr_o[...] = jnp.where(lane < MLA_ROPE, dkr, 0.0)

    outs = [_sds((S, 1024), MXU_DTYPE), _sds((S, 1024), MXU_DTYPE), _sds((S, LANES))]
    return _rowwise(body, [dq, dk, dv, tabs["cm"], tabs["sm"]], outs, name=name, rows=S)


DILATIONS = tuple(d for _, d in DIL_PATTERNS)
QKV_CHUNKS = 8


def _to_branch(nat, c0, chunks, out_ref, d, rows):
    width = chunks * LANES
    for r in range(d):
        tok = pl.ds(r, rows // d, stride=d) if d > 1 else slice(None)
        for c in range(chunks):
            out_ref[:, r * width + c * LANES:r * width + (c + 1) * LANES] = nat[c0 + c, tok, :].astype(out_ref.dtype)


def _from_branch(in_ref, nat, c0, chunks, d, rows, add=False):
    width = chunks * LANES
    for r in range(d):
        tok = pl.ds(r, rows // d, stride=d) if d > 1 else slice(None)
        for c in range(chunks):
            val = in_ref[:, r * width + c * LANES:r * width + (c + 1) * LANES].astype(F32)
            nat[c0 + c, tok, :] = nat[c0 + c, tok, :] + val if add else val


def _branch_sds(S, width, d, dtype):
    return _sds((S // d, d * width), dtype)


def _l1_prep(qkv, tabs, *, name):
    S = qkv.shape[0]

    def body(tm, x_r, c64, s64, *rest):
        outs, nat = rest[:-1], rest[-1]
        for i in range(QKV_CHUNKS):
            sl = slice(i * LANES, (i + 1) * LANES)
            nat[i] = _rope_chunk(x_r[:, sl], c64[...], s64[...], 32)
            nat[QKV_CHUNKS + i] = _rope_chunk(x_r[:, 1024 + i * LANES:1024 + (i + 1) * LANES], c64[...], s64[...], 32)
            nat[2 * QKV_CHUNKS + i] = x_r[:, 2048 + i * LANES:2048 + (i + 1) * LANES]
        for b, d in enumerate(DILATIONS):
            for t in range(3):
                _to_branch(nat, t * QKV_CHUNKS, QKV_CHUNKS, outs[3 * b + t], d, tm)

    outs = [_branch_sds(S, 1024, d, MXU_DTYPE) for d in DILATIONS for _ in range(3)]
    got = _rowwise(body, [qkv, tabs["c64"], tabs["s64"]], outs, name=name, rows=S,
                   scratch=[pltpu.VMEM((3 * QKV_CHUNKS, _row_tile(S, 512), LANES), F32)])
    return {d: tuple(got[3 * b:3 * b + 3]) for b, d in enumerate(DILATIONS)}


def _l1_prep_bwd(grads, tabs, *, name):
    S = grads[1][0].shape[0]

    def body(tm, *rest):
        ins, (c64, s64, o, nat) = rest[:9], rest[9:]
        for b, d in enumerate(DILATIONS):
            for t in range(3):
                _from_branch(ins[3 * b + t], nat, t * QKV_CHUNKS, QKV_CHUNKS, d, tm, add=b > 0)
        for i in range(QKV_CHUNKS):
            sl = slice(i * LANES, (i + 1) * LANES)
            o[:, sl] = _rope_chunk(nat[i], c64[...], -s64[...], 32).astype(o.dtype)
            o[:, 1024 + i * LANES:1024 + (i + 1) * LANES] = _rope_chunk(
                nat[QKV_CHUNKS + i], c64[...], -s64[...], 32).astype(o.dtype)
            o[:, 2048 + i * LANES:2048 + (i + 1) * LANES] = nat[2 * QKV_CHUNKS + i].astype(o.dtype)

    ins = [g for d in DILATIONS for g in grads[d]] + [tabs["c64"], tabs["s64"]]
    return _rowwise(body, ins, [_sds((S, 3072), MXU_DTYPE)], name=name, rows=S, tm=256,
                    scratch=[pltpu.VMEM((3 * QKV_CHUNKS, _row_tile(S, 256), LANES), F32)])[0]


def _sigmoid(x):
    return 1.0 / (1.0 + jnp.exp(-x))


FFN_ROW_TILE, FFN_COL_TILE = 512, 1408


def _gate_up(h, w_gate, w_up, *, name):
    (M, K), N = h.shape, w_gate.shape[1]
    tm, tn = _tile(M, FFN_ROW_TILE), _tile(N, FFN_COL_TILE)

    def body(h_ref, wg_ref, wu_ref, g_ref, u_ref, a_ref):
        g = _dot(h_ref[...], wg_ref[...], NN)
        u = _dot(h_ref[...], wu_ref[...], NN)
        g_ref[...] = g
        u_ref[...] = u
        a_ref[...] = (g * _sigmoid(g) * u).astype(a_ref.dtype)

    w_spec = pl.BlockSpec((K, tn), lambda j, i: (0, j))
    o_spec = pl.BlockSpec((tm, tn), lambda j, i: (i, j))
    return _pcall(
        body, name=name, dims=("parallel", "parallel"), grid=(N // tn, M // tm),
        in_specs=[pl.BlockSpec((tm, K), lambda j, i: (i, 0)), w_spec, w_spec], out_specs=[o_spec] * 3,
        out_shape=[_sds((M, N)), _sds((M, N)), _sds((M, N), MXU_DTYPE)],
    )(h, w_gate, w_up)


def _gate_up_bwd(dx, w_down, gate, up, *, name):
    (M, K), N = dx.shape, w_down.shape[0]
    tm, tn = _tile(M, FFN_ROW_TILE), _tile(N, FFN_COL_TILE)

    def body(dx_ref, w_ref, g_ref, u_ref, dg_ref, du_ref):
        d = _dot(dx_ref[...], w_ref[...], NT)
        g = g_ref[...]
        sg = _sigmoid(g)
        dg_ref[...] = (d * u_ref[...] * (sg * (1.0 + g * (1.0 - sg)))).astype(dg_ref.dtype)
        du_ref[...] = (d * g * sg).astype(du_ref.dtype)

    o_spec = pl.BlockSpec((tm, tn), lambda j, i: (i, j))
    return _pcall(
        body, name=name, dims=("parallel", "parallel"), grid=(N // tn, M // tm),
        in_specs=[pl.BlockSpec((tm, K), lambda j, i: (i, 0)), pl.BlockSpec((tn, K), lambda j, i: (j, 0)),
                  o_spec, o_spec],
        out_specs=[o_spec] * 2, out_shape=[_sds((M, N), MXU_DTYPE)] * 2,
    )(dx, w_down, gate, up)


def _head_pair_weights(w, c, rows):
    return jnp.where(_lane((rows, LANES)) < HEAD_DIM, w[:, 2 * c:2 * c + 1], w[:, 2 * c + 1:2 * c + 2])


def _merge(outs_by_d, lses_by_d, *, name):
    S = outs_by_d[1].shape[0]
    far = DILATIONS[1:]

    def body(tm, o1, o4, o16, l1, l4, l16, o_o, w1_o, w4_o, w16_o, nat_o, nat_l):
        for b, (o_r, l_r, d) in enumerate(zip((o4, o16), (l4, l16), far)):
            _from_branch(o_r, nat_o, b * QKV_CHUNKS, QKV_CHUNKS, d, tm)
            _from_branch(l_r, nat_l, b, 1, d, tm)
        ls = [l1[...], nat_l[0], nat_l[1]]
        m = jnp.maximum(jnp.maximum(ls[0], ls[1]), ls[2])
        es = [jnp.exp(l - m) for l in ls]
        tot = es[0] + es[1] + es[2]
        ws = [e / tot for e in es]
        for w_o, w in zip((w1_o, w4_o, w16_o), ws):
            w_o[...] = w
        for c in range(QKV_CHUNKS):
            sl = slice(c * LANES, (c + 1) * LANES)
            parts = (o1[:, sl], nat_o[c], nat_o[QKV_CHUNKS + c])
            o_o[:, sl] = sum(_head_pair_weights(w, c, tm) * part for w, part in zip(ws, parts))

    ins = [outs_by_d[d] for d in DILATIONS] + [lses_by_d[d] for d in DILATIONS]
    outs = [_sds((S, 1024))] + [_sds((S, LANES))] * 3
    rows = _row_tile(S, 256)
    return _rowwise(body, ins, outs, name=name, rows=S, tm=256,
                    scratch=[pltpu.VMEM((2 * QKV_CHUNKS, rows, LANES), F32), pltpu.VMEM((2, rows, LANES), F32)])


def _merge_bwd(do, o, ws, *, name):
    S = do.shape[0]

    def body(tm, do_r, o_r, w1, w4, w16, d1, d4, d16, e1, e4, e16, nat, nat_l):
        prod = do_r[...] * o_r[...]
        sums = _cols_to_lanes([jnp.sum(prod[:, j * HEAD_DIM:(j + 1) * HEAD_DIM], axis=1, keepdims=True)
                               for j in range(DIL_HEADS)], tm)
        for w_r, d_o, e_o, d in zip((w1, w4, w16), (d1, d4, d16), (e1, e4, e16), DILATIONS):
            w = w_r[...]
            nat_l[0] = w * sums
            _to_branch(nat_l, 0, 1, e_o, d, tm)
            for c in range(QKV_CHUNKS):
                nat[c] = _head_pair_weights(w, c, tm) * do_r[:, c * LANES:(c + 1) * LANES]
            _to_branch(nat, 0, QKV_CHUNKS, d_o, d, tm)

    outs = [_branch_sds(S, 1024, d, MXU_DTYPE) for d in DILATIONS] + [_branch_sds(S, LANES, d, F32) for d in DILATIONS]
    rows = _row_tile(S, 256)
    got = _rowwise(body, [do, o] + [ws[d] for d in DILATIONS], outs, name=name, rows=S, tm=256,
                   scratch=[pltpu.VMEM((QKV_CHUNKS, rows, LANES), F32), pltpu.VMEM((1, rows, LANES), F32)])
    return dict(zip(DILATIONS, got[:3])), dict(zip(DILATIONS, got[3:]))


def _loss_head(x, g, target, *, name):
    S, D = x.shape

    def body(tm, x_r, g_r, t_r, dx_o, dg_o, sq_o):
        xf = x_r[...]
        xhat, _ = _rms_parts(xf)
        err = xhat * g_r[...] - t_r[...]
        dx, dgp = _rms_bwd_rows(xf, g_r[...], err * (1.0 / D))
        dx_o[...] = dx
        _acc_rows(dg_o, dgp)
        _acc_rows(sq_o, err * err)

    return _rowwise(body, [x, g.reshape(1, D), target], [_sds((S, D)), _sds((1, D)), _sds((1, D))],
                    name=name, rows=S, accs=(1, 2))


def _adamw(w, g, m, v, *, name):
    c1 = 1.0 - ADAM_B1 ** ADAM_STEP
    c2 = 1.0 - ADAM_B2 ** ADAM_STEP

    def body(tm, w_r, g_r, m_r, v_r, d_o, m_o, v_o):
        g = g_r[...]
        m_new = ADAM_B1 * m_r[...] + (1.0 - ADAM_B1) * g
        v_new = ADAM_B2 * v_r[...] + (1.0 - ADAM_B2) * (g * g)
        m_o[...] = m_new
        v_o[...] = v_new
        d_o[...] = -ADAM_LR * ((m_new / c1) / (jnp.sqrt(v_new / c2) + ADAM_EPS) + ADAM_WD * w_r[...])

    return _rowwise(body, [w, g, m, v], [_sds(w.shape)] * 3, name=name, rows=w.shape[0], tm=256)


SUM_ROW_TILE = 256


def _sum_cores(grads, theirs, half_index, *, name):
    _, R, C = grads.shape
    h = R // 2
    nb = h // SUM_ROW_TILE

    def body(c_ref, g_ref, t_ref, o_ref):
        o_ref[...] = (g_ref[...].astype(F32) + t_ref[...].astype(F32)).astype(o_ref.dtype)

    grid_spec = pltpu.PrefetchScalarGridSpec(
        num_scalar_prefetch=1, grid=(4, nb),
        in_specs=[pl.BlockSpec((1, SUM_ROW_TILE, C), lambda k, i, c_ref: (k, c_ref[0] * nb + i, 0)),
                  pl.BlockSpec((1, SUM_ROW_TILE, C), lambda k, i, c_ref: (k, i, 0))],
        out_specs=pl.BlockSpec((1, SUM_ROW_TILE, C), lambda k, i, c_ref: (k, i, 0)))
    return _pcall(body, name=name, dims=("parallel", "parallel"), grid_spec=grid_spec,
                  out_shape=_sds((4, h, C), grads.dtype))(half_index, grads, theirs)


def _sum_chips(parts, half_index, *, name):
    _, h, C = parts.shape
    nb = h // SUM_ROW_TILE

    def body(c_ref, p_ref, o_ref):
        p = [p_ref[k].astype(F32) for k in range(4)]
        o_ref[...] = ((p[0] + p[1]) + p[2]) + p[3]

    grid_spec = pltpu.PrefetchScalarGridSpec(
        num_scalar_prefetch=1, grid=(nb,),
        in_specs=[pl.BlockSpec((4, SUM_ROW_TILE, C), lambda i, c_ref: (0, i, 0))],
        out_specs=pl.BlockSpec((SUM_ROW_TILE, C), lambda i, c_ref: (c_ref[0] * nb + i, 0)))
    return _pcall(body, name=name, dims=("parallel",), grid_spec=grid_spec,
                  out_shape=_sds((2 * h, C)))(half_index, parts)


def _position():
    return lax.axis_index("x"), lax.axis_index("y"), lax.axis_index("c")


def _chip_peers(x, y):
    return [(1 - x, y), (x, 1 - y), (1 - x, 1 - y)]


_HBM = pl.BlockSpec(memory_space=pltpu.HBM)
LOCAL_COPY_CHUNKS = 8


def _local_copies(src_ref, dst_ref, sems):
    rows = src_ref.shape[0] // LOCAL_COPY_CHUNKS
    assert rows * LOCAL_COPY_CHUNKS == src_ref.shape[0]
    return [pltpu.make_async_copy(src_ref.at[pl.ds(i * rows, rows)], dst_ref.at[pl.ds(i * rows, rows)], sems.at[i])
            for i in range(LOCAL_COPY_CHUNKS)]


class _Exchange:
    def __init__(self, src, out_shape, sems, stages):
        self.src, self.out_shape, self.sems, self.stages = src, out_shape, sems, stages

    def run(self, refs, step, n_steps, at_end):
        for fraction, fn in self.stages:
            if (fraction == 1.0) == at_end:
                pl.when(step == int(round(fraction * (n_steps - 1))))(functools.partial(fn, *refs))


def _run_exchange(ex, *, name):
    def body(*refs):
        for _, fn in ex.stages:
            fn(*refs)

    return pl.pallas_call(
        body, name=name, in_specs=[_HBM], out_specs=_HBM, out_shape=ex.out_shape, scratch_shapes=list(ex.sems),
    )(ex.src)


def _gather_exchange(src):
    R, C = src.shape
    h = R // 2

    def plan(src_ref, out_ref, send_sems, recv_sems, local_sems):
        x, y, c = _position()
        me = 2 * x + y
        peers = _chip_peers(x, y)
        mine, other = pl.ds(c * h, h), pl.ds((1 - c) * h, h)

        def copy(sem, src_part, dst_part, device):
            return pltpu.make_async_remote_copy(
                src_ref=src_part, dst_ref=dst_part, send_sem=send_sems.at[sem], recv_sem=recv_sems.at[sem],
                device_id=device, device_id_type=MESH)

        landed = [out_ref.at[2 * px + py, mine] for px, py in peers]
        theirs = [out_ref.at[2 * px + py, other] for px, py in peers]
        return dict(
            sends=lambda: [copy(j, src_ref.at[mine], out_ref.at[me, mine], (px, py, c))
                           for j, (px, py) in enumerate(peers)],
            local=lambda: _local_copies(src_ref, out_ref.at[me], local_sems),
            arrivals=lambda: [copy(j, landed[j], landed[j], (px, py, c)) for j, (px, py) in enumerate(peers)],
            passed=lambda: [copy(3 + j, landed[j], landed[j], (x, y, 1 - c)) for j in range(3)],
            from_sibling=lambda: [copy(3 + j, theirs[j], theirs[j], (x, y, 1 - c)) for j in range(3)])

    def start(*refs):
        p = plan(*refs)
        for cp in p["sends"]() + p["local"]():
            cp.start()

    def pass_on(*refs):
        p = plan(*refs)
        for arrival, forward in zip(p["arrivals"](), p["passed"]()):
            arrival.wait_recv()
            forward.start()

    def finish(*refs):
        p = plan(*refs)
        for cp in p["from_sibling"]():
            cp.wait_recv()
        for cp in p["sends"]() + p["passed"]():
            cp.wait_send()
        for cp in p["local"]():
            cp.wait()

    sems = [pltpu.SemaphoreType.DMA((6,)), pltpu.SemaphoreType.DMA((6,)), pltpu.SemaphoreType.DMA((LOCAL_COPY_CHUNKS,))]
    return _Exchange(src, jax.ShapeDtypeStruct((4, R, C), src.dtype), sems, [(0.0, start), (0.6, pass_on), (1.0, finish)])


def _swap_other_half(src, *, name):
    _, R, C = src.shape
    h = R // 2

    def body(src_ref, out_ref, send_sem, recv_sem):
        x, y, c = _position()
        cp = pltpu.make_async_remote_copy(
            src_ref=src_ref.at[:, pl.ds((1 - c) * h, h)], dst_ref=out_ref, send_sem=send_sem, recv_sem=recv_sem,
            device_id=(x, y, 1 - c), device_id_type=MESH)
        cp.start()
        cp.wait()

    return pl.pallas_call(
        body, name=name, in_specs=[_HBM], out_specs=_HBM, out_shape=jax.ShapeDtypeStruct((4, h, C), src.dtype),
        scratch_shapes=[pltpu.SemaphoreType.DMA, pltpu.SemaphoreType.DMA],
    )(src)


def _scatter_exchange(src):
    def plan(src_ref, out_ref, send_sems, recv_sems, local_sems):
        x, y, c = _position()
        me = 2 * x + y
        peers = _chip_peers(x, y)

        def copy(j, src_block, dst_slot):
            px, py = peers[j]
            return pltpu.make_async_remote_copy(
                src_ref=src_ref.at[src_block], dst_ref=out_ref.at[dst_slot], send_sem=send_sems.at[j],
                recv_sem=recv_sems.at[j], device_id=(px, py, c), device_id_type=MESH)

        return dict(sends=lambda: [copy(j, 2 * px + py, me) for j, (px, py) in enumerate(peers)],
                    arrivals=lambda: [copy(j, me, 2 * px + py) for j, (px, py) in enumerate(peers)],
                    local=lambda: _local_copies(src_ref.at[me], out_ref.at[me], local_sems))

    def start(*refs):
        p = plan(*refs)
        for cp in p["sends"]() + p["local"]():
            cp.start()

    def finish(*refs):
        p = plan(*refs)
        for cp in p["arrivals"]():
            cp.wait_recv()
        for cp in p["sends"]():
            cp.wait_send()
        for cp in p["local"]():
            cp.wait()

    sems = [pltpu.SemaphoreType.DMA((3,)), pltpu.SemaphoreType.DMA((3,)), pltpu.SemaphoreType.DMA((LOCAL_COPY_CHUNKS,))]
    return _Exchange(src, jax.ShapeDtypeStruct(src.shape, src.dtype), sems, [(0.0, start), (1.0, finish)])


def _join_halves(src, *, name):
    R, C = src.shape
    h = R // 2

    def body(src_ref, out_ref, send_sem, recv_sem):
        x, y, c = _position()
        mine, theirs = pl.ds(c * h, h), pl.ds((1 - c) * h, h)
        cp = pltpu.make_async_remote_copy(
            src_ref=src_ref.at[mine], dst_ref=out_ref.at[mine], send_sem=send_sem, recv_sem=recv_sem,
            device_id=(x, y, 1 - c), device_id_type=MESH)
        cp.start()
        pltpu.make_async_remote_copy(
            src_ref=src_ref.at[theirs], dst_ref=out_ref.at[theirs], send_sem=send_sem, recv_sem=recv_sem,
            device_id=(x, y, 1 - c), device_id_type=MESH).wait_recv()
        cp.wait_send()

    return pl.pallas_call(
        body, name=name, in_specs=[_HBM], out_specs=_HBM, out_shape=jax.ShapeDtypeStruct((R, C), src.dtype),
        input_output_aliases={0: 0},
        scratch_shapes=[pltpu.SemaphoreType.DMA, pltpu.SemaphoreType.DMA],
    )(src)


def _allreduce_small(vec, *, name):
    R, C = vec.shape

    def body(v_ref, o_ref, slots, send_sems, recv_sems):
        x, y, c = _position()
        me = 4 * x + 2 * y + c

        def peer(k):
            return x ^ ((k >> 2) & 1), y ^ ((k >> 1) & 1), c ^ (k & 1)

        def copy(k, slot):
            return pltpu.make_async_remote_copy(
                src_ref=v_ref, dst_ref=slots.at[slot], send_sem=send_sems.at[k - 1], recv_sem=recv_sems.at[k - 1],
                device_id=peer(k), device_id_type=MESH)

        slots[me] = v_ref[...]
        sends = [copy(k, me) for k in range(1, 8)]
        for cp in sends:
            cp.start()
        for k in range(1, 8):
            px, py, pc = peer(k)
            copy(k, 4 * px + 2 * py + pc).wait_recv()
        total = slots[0]
        for d in range(1, 8):
            total = total + slots[d]
        o_ref[...] = total
        for cp in sends:
            cp.wait_send()

    vmem = pl.BlockSpec(memory_space=pltpu.VMEM)
    return pl.pallas_call(
        body, name=name, in_specs=[vmem], out_specs=vmem, out_shape=jax.ShapeDtypeStruct((R, C), vec.dtype),
        scratch_shapes=[pltpu.VMEM((8, R, C), vec.dtype), pltpu.SemaphoreType.DMA((7,)), pltpu.SemaphoreType.DMA((7,))],
    )(vec)


def _swa_cfg(S):
    return _Attn(T=S, Tk=S, G=1, nh=SWA_HEADS, rep=SWA_HEADS // SWA_KV_HEADS, dqk=HEAD_DIM, dv=HEAD_DIM, tq=BLOCK,
                 tk=BLOCK, mode="band", max_dist=SWA_WINDOW - 1, scale=HEAD_DIM ** -0.5, qcol=lambda g: 0,
                 kcol=lambda g: 0, vcol=lambda g: 0, ocol=lambda g: 0, o_width=SWA_HEADS * HEAD_DIM)


MLA_FWD_GROUP = 8
MLA_BWD_GROUP = 4


def _mla_cfg(S, group):
    t = _tile(S, 512)
    return _Attn(T=S, Tk=S, G=MLA_HEADS // group, nh=group, rep=1, dqk=LANES, dv=MLA_V, tq=t, tk=t, mode="causal",
                 scale=(MLA_NOPE + MLA_ROPE) ** -0.5, qcol=lambda g: g, kcol=lambda g: g, vcol=lambda g: g,
                 ocol=lambda g: g, o_width=MLA_HEADS * MLA_V)


def _dil_cfg(S, window, dil):
    return _Attn(T=S // dil, Tk=S // dil, G=dil, nh=DIL_HEADS, rep=1, dqk=HEAD_DIM, dv=HEAD_DIM, tq=BLOCK, tk=BLOCK,
                 mode="band", max_dist=window // dil, scale=HEAD_DIM ** -0.5, qcol=lambda g: g, kcol=lambda g: g,
                 vcol=lambda g: g, ocol=lambda g: g, o_width=dil * DIL_HEADS * HEAD_DIM)


X_ROW_TILE = 512


def _memory_attn(q, kv, *, name):
    S, width = q.shape
    M = kv.shape[0]
    tq = _tile(S, X_ROW_TILE)
    scale = X_HEAD_DIM ** -0.5
    head = lambda j: slice(j * X_HEAD_DIM, (j + 1) * X_HEAD_DIM)

    def body(q_ref, kv_ref, o_ref, lse_ref):
        score = lambda j: _dot(kv_ref[:, head(j)], q_ref[:, head(j)], NT)
        ahead, rows = score(0), []
        for j in range(X_HEADS):
            s = ahead * scale
            if j + 1 < X_HEADS:
                ahead = score(j + 1)
            m = jnp.max(s, axis=0, keepdims=True)
            pr = jnp.exp(s - m)
            l = jnp.sum(pr, axis=0, keepdims=True)
            o_t = _dot(kv_ref[:, head(X_HEADS + j)], pr, TN)
            o_ref[:, head(j)] = (o_t / l).T.astype(o_ref.dtype)
            rows.append(m + jnp.log(l))
        lse_ref[...] = _rows_to_stats(rows, tq)

    return _pcall(
        body, name=name, dims=("parallel",), grid=(S // tq,),
        in_specs=[pl.BlockSpec((tq, width), lambda i: (i, 0)), pl.BlockSpec((M, 2 * width), lambda i: (0, 0))],
        out_specs=[pl.BlockSpec((tq, width), lambda i: (i, 0)), pl.BlockSpec((tq, LANES), lambda i: (i, 0))],
        out_shape=[_sds((S, width), MXU_DTYPE), _sds((S, LANES))],
    )(q, kv)


def _memory_attn_bwd(q, kv, o, do, lse, *, name):
    S, width = q.shape
    M = kv.shape[0]
    tq = _tile(S, X_ROW_TILE)
    n = S // tq
    scale = X_HEAD_DIM ** -0.5
    head = lambda j: slice(j * X_HEAD_DIM, (j + 1) * X_HEAD_DIM)

    def body(q_ref, kv_ref, o_ref, do_ref, lse_ref, dq_ref, dkv_ref, acc):
        i = pl.program_id(0)

        @pl.when(i == 0)
        def _():
            acc[...] = jnp.zeros_like(acc)

        lse_t = lse_ref[...].T

        def first(j):
            return (_dot(kv_ref[:, head(j)], q_ref[:, head(j)], NT),
                    _dot(kv_ref[:, head(X_HEADS + j)], do_ref[:, head(j)], NT))

        ahead = first(0)
        for j in range(X_HEADS):
            s, dp = ahead
            if j + 1 < X_HEADS:
                ahead = first(j + 1)
            row_term = jnp.sum((do_ref[:, head(j)].astype(F32) * o_ref[:, head(j)].astype(F32)).T, axis=0, keepdims=True)
            pr = jnp.exp(s * scale - lse_t[j:j + 1, :])
            ds = pr * (dp - row_term) * scale
            dq_ref[:, head(j)] = _dot(kv_ref[:, head(j)], ds, TN).T.astype(dq_ref.dtype)
            acc[:, head(j)] += _dot(ds, q_ref[:, head(j)], NN)
            acc[:, head(X_HEADS + j)] += _dot(pr, do_ref[:, head(j)], NN)

        @pl.when(i == n - 1)
        def _():
            dkv_ref[...] = acc[...].astype(dkv_ref.dtype)

    row = pl.BlockSpec((tq, width), lambda i: (i, 0))
    whole = pl.BlockSpec((M, 2 * width), lambda i: (0, 0))
    return _pcall(
        body, name=name, dims=("arbitrary",), grid=(n,),
        in_specs=[row, whole, row, row, pl.BlockSpec((tq, LANES), lambda i: (i, 0))], out_specs=[row, whole],
        out_shape=[_sds((S, width), MXU_DTYPE), _sds((M, 2 * width), MXU_DTYPE)],
        scratch_shapes=[pltpu.VMEM((M, 2 * width), F32)],
    )(q, kv, o, do, lse)


def _cross_fwd(p, x, mem, W, vec):
    hx = _rmsnorm(x, vec[p + "x_norm"], name=p + "x_norm")
    qx = _mm(hx, W[p + "w_xq"], mode="nn", name=p + "xq", out_dtype=MXU_DTYPE)
    memn = _rmsnorm(mem, vec[p + "mem_norm"], name=p + "mem_norm")
    kvx = _mm(memn, W[p + "w_xkv"], mode="nn", name=p + "xkv", out_dtype=MXU_DTYPE)
    ox, lse = _memory_attn(qx, kvx, name=p + "x_attn")
    out = _mm(ox, W[p + "w_xo"], mode="nn", name=p + "xo", res=x)
    return out, (x, hx, qx, memn, kvx, ox, lse)


def _cross_bwd(p, dx, saved, mem, W, vec, dW, dvec):
    x, hx, qx, memn, kvx, ox, lse = saved
    dox = _mm(dx, W[p + "w_xo"], mode="nt", name=p + "xo_dx", out_dtype=MXU_DTYPE)
    dW[p + "w_xo"] = _dw(ox, dx, name=p + "xo_dw")
    dqx, dkvx = _memory_attn_bwd(qx, kvx, ox, dox, lse, name=p + "x_attn_bwd")
    dW[p + "w_xq"] = _dw(hx, dqx, name=p + "xq_dw")
    dW[p + "w_xkv"] = _dw(memn, dkvx, name=p + "xkv_dw")
    dmemn = _mm(dkvx, W[p + "w_xkv"], mode="nt", name=p + "xkv_dx")
    _, dvec[p + "mem_norm"] = _rmsnorm_bwd(mem, vec[p + "mem_norm"], dmemn, name=p + "mem_norm_bwd")
    dx_in, dvec[p + "x_norm"] = _dx_norm_bwd(dqx, W[p + "w_xq"], x, vec[p + "x_norm"], dx, name=p + "xq_dx")
    return dx_in


def _ffn_fwd(p, x, W, vec):
    hf = _rmsnorm(x, vec[p + "ffn_norm"], name=p + "ffn_norm")
    gate, up, act = _gate_up(hf, W[p + "w_gate"], W[p + "w_up"], name=p + "gate_up")
    out = _mm(act, W[p + "w_down"], mode="nn", name=p + "down", res=x)
    return out, (x, hf, gate, up, act)


def _ffn_bwd(p, dx, saved, W, vec, dW, dvec):
    x, hf, gate, up, act = saved
    dW[p + "w_down"] = _dw(act, dx, name=p + "down_dw")
    dgate, dup = _gate_up_bwd(dx, W[p + "w_down"], gate, up, name=p + "gate_up_bwd")
    dhf = _mm(dgate, W[p + "w_gate"], mode="nt", name=p + "gate_dx")
    dW[p + "w_gate"] = _dw(hf, dgate, name=p + "gate_dw")
    dW[p + "w_up"] = _dw(hf, dup, name=p + "up_dw")
    dx_in, dvec[p + "ffn_norm"] = _dx_norm_bwd(dup, W[p + "w_up"], x, vec[p + "ffn_norm"], dx, name=p + "up_dx",
                                               res=dhf)
    return dx_in


def _even_fwd(p, x, tabs, W, vec, comm=None):
    S = x.shape[0]
    h = _rmsnorm(x, vec[p + "mix_norm"], name=p + "mix_norm")
    z = _mm(h, W[p + "w_in"], mode="nn", name=p + "in")
    qa, ka, va, cqn, ckvn, kr = _l0_prep(z, tabs, vec[p + "q_norm"], vec[p + "kv_norm"], name=p + "prep")
    sink = jnp.pad(vec[p + "sinks"], (0, LANES - SWA_HEADS)).reshape(1, LANES)
    oa, lse_a = _band_fwd(_swa_cfg(S), qa, ka, va, name=p + "swa", sink=sink, out_dtype=MXU_DTYPE)
    qb = _mm(cqn, W[p + "w_uq"], mode="nn", name=p + "uq")
    kvb = _mm(ckvn, W[p + "w_ukv"], mode="nn", name=p + "ukv")
    Q, K, V = _mla_prep(qb, kvb, kr, tabs, name=p + "mla_prep")
    if comm is None:
        ob, lse_b = _causal_fwd(_mla_cfg(S, MLA_FWD_GROUP), Q, K, V, name=p + "mla", out_dtype=MXU_DTYPE, stat_heads=MLA_BWD_GROUP)
    else:
        ob, lse_b, gathered = _causal_fwd(_mla_cfg(S, MLA_FWD_GROUP), Q, K, V, name=p + "mla", out_dtype=MXU_DTYPE, stat_heads=MLA_BWD_GROUP,
                                          carry=comm.late_weights_exchange())
        W = {**W, **comm.late_weights(gathered)}
    o = jnp.concatenate([oa, ob], axis=1)
    out = _mm(o, W[p + "w_out"], mode="nn", name=p + "out", res=x)
    return out, (x, h, z, qa, ka, va, cqn, ckvn, sink, oa, lse_a, Q, K, V, ob, lse_b, o), W


def _even_bwd(p, dx, saved, tabs, W, vec, dW, dvec, comm=None):
    x, h, z, qa, ka, va, cqn, ckvn, sink, oa, lse_a, Q, K, V, ob, lse_b, o = saved
    S = x.shape[0]
    do = _mm(dx, W[p + "w_out"], mode="nt", name=p + "out_dx", out_dtype=MXU_DTYPE)
    dW[p + "w_out"] = _dw(o, dx, name=p + "out_dw")
    doa, dob = do[:, :SWA_HEADS * HEAD_DIM], do[:, SWA_HEADS * HEAD_DIM:]
    cfg = _swa_cfg(S)
    delta, dsink = _attn_delta(cfg, oa, doa, name=p + "swa_delta", lse=lse_a, sink=sink)
    dvec[p + "sinks"] = dsink
    dqa, dka, dva = _band_bwd(cfg, qa, ka, va, doa, lse_a, delta, name=p + "swa_bwd")
    cfg = _mla_cfg(S, MLA_BWD_GROUP)
    delta, _ = _attn_delta(cfg, ob, dob, name=p + "mla_delta")
    if comm is None:
        dQ, dK, dV = _causal_bwd(cfg, Q, K, V, dob, lse_b, delta, name=p + "mla_bwd")
    else:
        dQ, dK, dV, landed = _causal_bwd(cfg, Q, K, V, dob, lse_b, delta, name=p + "mla_bwd",
                                         carry=comm.late_grads_exchange(dW))
        comm.late_grads_landed(landed)
    dqb, dkvb, dkr = _mla_prep_bwd(dQ, dK, dV, tabs, name=p + "mla_prep_bwd")
    dcqn = _mm(dqb, W[p + "w_uq"], mode="nt", name=p + "uq_dx")
    dW[p + "w_uq"] = _dw(cqn, dqb, name=p + "uq_dw")
    dckvn = _mm(dkvb, W[p + "w_ukv"], mode="nt", name=p + "ukv_dx")
    dW[p + "w_ukv"] = _dw(ckvn, dkvb, name=p + "ukv_dw")
    dz, dvec[p + "q_norm"], dvec[p + "kv_norm"] = _l0_prep_bwd(
        z, tabs, vec[p + "q_norm"], vec[p + "kv_norm"], dqa, dka, dva, dcqn, dckvn, dkr, name=p + "prep_bwd")
    dW[p + "w_in"] = _dw(h, dz, name=p + "in_dw")
    dx_in, dvec[p + "mix_norm"] = _dx_norm_bwd(dz, W[p + "w_in"], x, vec[p + "mix_norm"], dx, name=p + "in_dx")
    return dx_in


def _odd_fwd(p, x, tabs, W, vec):
    S = x.shape[0]
    assert S % (DIL_PATTERNS[-1][1] * BLOCK) == 0, "keys past the end of the sequence are never attended"
    h = _rmsnorm(x, vec[p + "mix_norm"], name=p + "mix_norm")
    qkv = _mm(h, W[p + "w_qkv"], mode="nn", name=p + "qkv")
    qkv_by_d = _l1_prep(qkv, tabs, name=p + "prep")
    outs, lses = {}, {}
    for window, dil in DIL_PATTERNS:
        outs[dil], lses[dil] = _band_fwd(_dil_cfg(S, window, dil), *qkv_by_d[dil], name=p + "dil%d" % dil)
    o, w1, w4, w16 = _merge(outs, lses, name=p + "merge")
    out = _mm(o, W[p + "w_out"], mode="nn", name=p + "out", res=x)
    return out, (x, h, qkv_by_d, lses, dict(zip(DILATIONS, (w1, w4, w16))), o)


def _odd_bwd(p, dx, saved, tabs, W, vec, dW, dvec):
    x, h, qkv_by_d, lses, ws, o = saved
    S = x.shape[0]
    do = _mm(dx, W[p + "w_out"], mode="nt", name=p + "out_dx")
    dW[p + "w_out"] = _dw(o, dx, name=p + "out_dw")
    dos, deltas = _merge_bwd(do, o, ws, name=p + "merge_bwd")
    grads = {}
    for window, dil in DIL_PATTERNS:
        grads[dil] = _band_bwd(_dil_cfg(S, window, dil), *qkv_by_d[dil], dos[dil], lses[dil], deltas[dil],
                               name=p + "dil%d_bwd" % dil)
    dqkv = _l1_prep_bwd(grads, tabs, name=p + "prep_bwd")
    dW[p + "w_qkv"] = _dw(h, dqkv, name=p + "qkv_dw")
    dx_in, dvec[p + "mix_norm"] = _dx_norm_bwd(dqkv, W[p + "w_qkv"], x, vec[p + "mix_norm"], dx, name=p + "qkv_dx")
    return dx_in


def _local_step(x, mem, positions, target, W, vec, comm=None):
    tabs = _rope_tables(positions)
    x1, s_mix0, W = _even_fwd("l0_", x, tabs, W, vec, comm)
    x2, s_x0 = _cross_fwd("l0_", x1, mem, W, vec)
    x3, s_f0 = _ffn_fwd("l0_", x2, W, vec)
    x4, s_mix1 = _odd_fwd("l1_", x3, tabs, W, vec)
    x5, s_x1 = _cross_fwd("l1_", x4, mem, W, vec)
    x6, s_f1 = _ffn_fwd("l1_", x5, W, vec)
    dW, dvec = {}, {}
    dx, dvec["final_norm"], sq = _loss_head(x6, vec["final_norm"], target, name="loss_head")
    dx = _ffn_bwd("l1_", dx, s_f1, W, vec, dW, dvec)
    dx = _cross_bwd("l1_", dx, s_x1, mem, W, vec, dW, dvec)
    dx = _odd_bwd("l1_", dx, s_mix1, tabs, W, vec, dW, dvec)
    dx = _ffn_bwd("l0_", dx, s_f0, W, vec, dW, dvec)
    dx = _cross_bwd("l0_", dx, s_x0, mem, W, vec, dW, dvec)
    dx = _even_bwd("l0_", dx, s_mix0, tabs, W, vec, dW, dvec, comm)
    return sq, dx, dW, dvec


_LAYER_MATS = {
    0: [("w_in", "col"), ("w_uq", "col"), ("w_ukv", "col"), ("w_out", "row"), ("w_xq", "row"), ("w_xkv", "row"),
        ("w_xo", "col"), ("w_gate", "col"), ("w_up", "col"), ("w_down", "row")],
    1: [("w_qkv", "col"), ("w_out", "row"), ("w_xq", "row"), ("w_xkv", "row"), ("w_xo", "col"), ("w_gate", "col"),
        ("w_up", "col"), ("w_down", "row")],
}
MATS = [("l%d_%s" % (l, n), kind) for l in (0, 1) for n, kind in _LAYER_MATS[l]]
_LAYER_VECS = {0: ["mix_norm", "sinks", "q_norm", "kv_norm", "x_norm", "mem_norm", "ffn_norm"],
               1: ["mix_norm", "x_norm", "mem_norm", "ffn_norm"]}
VECS = ["l%d_%s" % (l, n) for l in (0, 1) for n in _LAYER_VECS[l]] + ["final_norm"]
WEIGHT_ORDER = (["l0_mix_norm", "l0_w_in", "l0_sinks", "l0_q_norm", "l0_w_uq", "l0_kv_norm", "l0_w_ukv", "l0_w_out",
                 "l0_x_norm", "l0_mem_norm", "l0_w_xq", "l0_w_xkv", "l0_w_xo", "l0_ffn_norm", "l0_w_gate", "l0_w_up",
                 "l0_w_down", "l1_mix_norm", "l1_w_qkv", "l1_w_out", "l1_x_norm", "l1_mem_norm", "l1_w_xq",
                 "l1_w_xkv", "l1_w_xo", "l1_ffn_norm", "l1_w_gate", "l1_w_up", "l1_w_down", "final_norm"])
PACK_COLS = 1024
PACK_ROW_TILE = 2 * SUM_ROW_TILE
VEC_ROWS = 16
LOSS_ROW = len(VECS)
N_CHIPS = 4


class _Group:
    def __init__(self, mats, shards):
        self.mats, self.shards = mats, shards
        self.layout, off = {}, 0
        for name, _ in mats:
            n = shards[name].size // PACK_COLS
            assert n * PACK_COLS == shards[name].size
            self.layout[name] = (off, n)
            off += n
        self.used = off
        self.rows = -(-off // PACK_ROW_TILE) * PACK_ROW_TILE

    def pack(self, tensors, dtype):
        parts = [tensors[name].astype(dtype).reshape(-1, PACK_COLS) for name, _ in self.mats]
        return jnp.concatenate(parts + [jnp.zeros((self.rows - self.used, PACK_COLS), dtype)], axis=0)

    def unpack(self, packed):
        return {name: packed[off:off + n].reshape(self.shards[name].shape) for name, (off, n) in self.layout.items()}

    def full_weights(self, gathered):
        W = {}
        for name, kind in self.mats:
            off, n = self.layout[name]
            r, cw = self.shards[name].shape
            blocks = gathered[:, off:off + n].reshape(N_CHIPS, r, cw)
            W[name] = blocks.reshape(N_CHIPS * r, cw) if kind == "row" else (
                jnp.transpose(blocks, (1, 0, 2)).reshape(r, N_CHIPS * cw))
        if "l0_w_in" in W:
            W["l0_w_in"] = jnp.pad(W["l0_w_in"], ((0, 0), (0, Z_END - W["l0_w_in"].shape[1])))
        if "l0_w_uq" in W:
            uq = W["l0_w_uq"].reshape(MLA_Q_RANK, MLA_HEADS, MLA_NOPE + MLA_ROPE)
            uq = jnp.pad(uq, ((0, 0), (0, 0), (0, LANES - MLA_NOPE - MLA_ROPE)))
            W["l0_w_uq"] = uq.reshape(MLA_Q_RANK, MLA_HEADS * LANES)
        return W

    def pack_grads(self, dW):
        parts = []
        for name, kind in self.mats:
            r, cw = self.shards[name].shape
            g = dW[name]
            if name == "l0_w_in":
                g = g[:, :Z_KR + MLA_ROPE]
            if name == "l0_w_uq":
                g = g.reshape(MLA_Q_RANK, MLA_HEADS, LANES)[:, :, :MLA_NOPE + MLA_ROPE].reshape(MLA_Q_RANK, -1)
            if kind == "col":
                g = jnp.transpose(g.reshape(r, N_CHIPS, cw), (1, 0, 2))
            parts.append(g.reshape(N_CHIPS, -1, PACK_COLS).astype(EXCHANGE_DTYPE))
        pad = jnp.zeros((N_CHIPS, self.rows - self.used, PACK_COLS), EXCHANGE_DTYPE)
        return jnp.concatenate(parts + [pad], axis=1)


def _pack_vecs(vecs):
    rows = [jnp.pad(vecs[n].reshape(-1).astype(F32), (0, PACK_COLS - vecs[n].size)) for n in VECS]
    rows += [jnp.zeros((PACK_COLS,), F32)] * (VEC_ROWS - len(rows))
    return jnp.stack(rows)


def _unpack_vecs(packed, like):
    return {n: packed[i, :like[n].size].reshape(like[n].shape) for i, n in enumerate(VECS)}


EARLY_MATS = [m for m in MATS if m[0] in ("l0_w_in", "l0_w_uq", "l0_w_ukv")]
LATE_MATS = [m for m in MATS if m not in EARLY_MATS]


class _StepComm:
    def __init__(self, shards):
        self.early, self.late = _Group(EARLY_MATS, shards), _Group(LATE_MATS, shards)
        self.half_index = lax.axis_index("c").astype(jnp.int32).reshape(1)
        self.late_grads = None

    def early_weights(self):
        src = self.early.pack(self.early.shards, MXU_DTYPE)
        return self.early.full_weights(_run_exchange(_gather_exchange(src), name="gather_early"))

    def late_weights_exchange(self):
        return _gather_exchange(self.late.pack(self.late.shards, MXU_DTYPE))

    def late_weights(self, gathered):
        return self.late.full_weights(gathered)

    def _chip_sum(self, group, dW, tag):
        grads = group.pack_grads(dW)
        theirs = _swap_other_half(grads, name="swap_other_half_" + tag)
        return _sum_cores(grads, theirs, self.half_index, name="sum_cores_" + tag)

    def _finish(self, parts, tag):
        return _join_halves(_sum_chips(parts, self.half_index, name="sum_chips_" + tag), name="join_halves_" + tag)

    def late_grads_exchange(self, dW):
        return _scatter_exchange(self._chip_sum(self.late, dW, "late"))

    def late_grads_landed(self, parts):
        self.late_grads = self._finish(parts, "late")

    def early_grads(self, dW):
        parts = _run_exchange(_scatter_exchange(self._chip_sum(self.early, dW, "early")), name="scatter_early")
        return self._finish(parts, "early")


def _step(a):
    weights = {n: a[n] for n in WEIGHT_ORDER}
    shards = {n: weights[n] for n, _ in MATS}
    vec = {n: weights[n] for n in VECS}
    comm = _StepComm(shards)
    sq, grad_x, dW, dvec = _local_step(a["x"][0], a["mem"][0], a["positions"], a["loss_target"][0],
                                       comm.early_weights(), vec, comm)

    dvec = dict(dvec)
    dvec["l0_sinks"] = dvec["l0_sinks"][0, :SWA_HEADS]
    small = _pack_vecs(dvec)
    small = small.at[LOSS_ROW, 0].set(0.5 / a["x"].shape[-1] * jnp.sum(sq))
    small = _allreduce_small(small, name="reduce_gains")
    loss = small[LOSS_ROW, 0]
    g_s = small.at[LOSS_ROW, 0].set(0.0)
    d_s, m_s, v_s = _adamw(_pack_vecs(vec), g_s, _pack_vecs({n: a["m_" + n] for n in VECS}),
                           _pack_vecs({n: a["v_" + n] for n in VECS}), name="adamw_gains")
    got = [_unpack_vecs(packed, vec) for packed in (g_s, d_s, m_s, v_s)]

    for group, g_w in ((comm.late, comm.late_grads), (comm.early, comm.early_grads(dW))):
        for n, g in group.unpack(g_w).items():
            results = (g,) + tuple(_adamw(shards[n], g, a["m_" + n], a["v_" + n], name="adamw_" + n))
            for kind, value in zip(got, results):
                kind[n] = value

    out = [loss, grad_x[None]]
    for kind in got:
        out += [kind[n] for n in WEIGHT_ORDER]
    return tuple(out)


def kernel(x, mem, positions, l0_mix_norm, l0_w_in, l0_sinks, l0_q_norm, l0_w_uq, l0_kv_norm, l0_w_ukv, l0_w_out, l0_x_norm, l0_mem_norm, l0_w_xq, l0_w_xkv, l0_w_xo, l0_ffn_norm, l0_w_gate, l0_w_up, l0_w_down, l1_mix_norm, l1_w_qkv, l1_w_out, l1_x_norm, l1_mem_norm, l1_w_xq, l1_w_xkv, l1_w_xo, l1_ffn_norm, l1_w_gate, l1_w_up, l1_w_down, final_norm, loss_target, m_l0_mix_norm, m_l0_w_in, m_l0_sinks, m_l0_q_norm, m_l0_w_uq, m_l0_kv_norm, m_l0_w_ukv, m_l0_w_out, m_l0_x_norm, m_l0_mem_norm, m_l0_w_xq, m_l0_w_xkv, m_l0_w_xo, m_l0_ffn_norm, m_l0_w_gate, m_l0_w_up, m_l0_w_down, m_l1_mix_norm, m_l1_w_qkv, m_l1_w_out, m_l1_x_norm, m_l1_mem_norm, m_l1_w_xq, m_l1_w_xkv, m_l1_w_xo, m_l1_ffn_norm, m_l1_w_gate, m_l1_w_up, m_l1_w_down, m_final_norm, v_l0_mix_norm, v_l0_w_in, v_l0_sinks, v_l0_q_norm, v_l0_w_uq, v_l0_kv_norm, v_l0_w_ukv, v_l0_w_out, v_l0_x_norm, v_l0_mem_norm, v_l0_w_xq, v_l0_w_xkv, v_l0_w_xo, v_l0_ffn_norm, v_l0_w_gate, v_l0_w_up, v_l0_w_down, v_l1_mix_norm, v_l1_w_qkv, v_l1_w_out, v_l1_x_norm, v_l1_mem_norm, v_l1_w_xq, v_l1_w_xkv, v_l1_w_xo, v_l1_ffn_norm, v_l1_w_gate, v_l1_w_up, v_l1_w_down, v_final_norm):
    return _step(dict(locals()))
```

```python
import functools

import jax
import jax.numpy as jnp
import numpy as np
from jax import lax
from jax.experimental import pallas as pl
from jax.experimental.pallas import tpu as pltpu

F32 = jnp.float32
MXU_DTYPE = jnp.bfloat16
LANES = 128
VMEM_LIMIT_BYTES = 56 * 1024 * 1024

NORM_EPS = 1e-6
ROPE_THETA = 10000.0
BLOCK = 128
HEAD_DIM = 64
SWA_HEADS, SWA_KV_HEADS, SWA_WINDOW = 8, 2, 128
MLA_HEADS, MLA_Q_RANK, MLA_KV_RANK, MLA_NOPE, MLA_ROPE, MLA_V = 8, 384, 256, 64, 32, 64
DIL_HEADS = 16
DIL_PATTERNS = ((128, 1), (512, 4), (2048, 16))
X_HEADS, X_HEAD_DIM = 4, 128
ADAM_LR, ADAM_B1, ADAM_B2, ADAM_EPS, ADAM_WD, ADAM_STEP = 0.001, 0.9, 0.999, 1e-08, 0.01, 10
MESH = pl.DeviceIdType.MESH
NEG_BIG = -1e30

NN = (((1,), (0,)), ((), ()))
NT = (((1,), (1,)), ((), ()))


def _dot(a, b, dims=NN):
    return lax.dot_general(a.astype(MXU_DTYPE), b.astype(MXU_DTYPE), dims, preferred_element_type=F32)


def _pcall(body, *, name, dims=None, **kw):
    params = pltpu.CompilerParams(dimension_semantics=dims, vmem_limit_bytes=VMEM_LIMIT_BYTES)
    return pl.pallas_call(body, name=name, compiler_params=params, **kw)


def _tile(n, pref):
    t = (min(pref, n) // LANES) * LANES
    while t >= LANES:
        if n % t == 0:
            return t
        t -= LANES
    return n


SUBLANES_PACKED = 16


def _row_tile(n, pref):
    t = (min(pref, n) // SUBLANES_PACKED) * SUBLANES_PACKED
    while t >= SUBLANES_PACKED:
        if n % t == 0:
            return t
        t -= SUBLANES_PACKED
    return n


def _lane(shape):
    return lax.broadcasted_iota(jnp.int32, shape, 1)


def _cols_to_lanes(cols, rows):
    lane = _lane((rows, LANES))
    out = jnp.zeros((rows, LANES), F32)
    for j, col in enumerate(cols):
        out = jnp.where(lane == j, col, out)
    return out


def _mm(a, b, *, mode, name, res=None, out_dtype=F32, tm=1408, tn=1536, tk=1408):
    if mode == "nn":
        (M, K), (K2, N) = a.shape, b.shape
    elif mode == "nt":
        (M, K), (N, K2) = a.shape, b.shape
    else:
        (K, M), (K2, N) = a.shape, b.shape
    assert K == K2, (a.shape, b.shape, mode)
    tm, tn, tk = _tile(M, tm), _tile(N, tn), _tile(K, tk)
    nk = K // tk
    in_place = out_dtype == F32 or nk == 1

    def body(*refs):
        refs = list(refs)
        a_ref, b_ref = refs[:2]
        r_ref = refs[2] if res is not None else None
        o_ref = refs[3 if res is not None else 2]
        acc = o_ref if in_place else refs[-1]
        k = pl.program_id(2)
        if mode == "nn":
            part = _dot(a_ref[...], b_ref[...], NN)
        elif mode == "nt":
            part = _dot(a_ref[...], b_ref[...], NT)
        else:
            part = _dot(a_ref[...].T, b_ref[...], NN)
        if nk == 1:
            o_ref[...] = (part if res is None else part + r_ref[...].astype(F32)).astype(o_ref.dtype)
            return

        @pl.when(k == 0)
        def _():
            acc[...] = part if res is None else part + r_ref[...].astype(F32)

        @pl.when(k > 0)
        def _():
            acc[...] += part

        if not in_place:
            @pl.when(k == nk - 1)
            def _():
                o_ref[...] = acc[...].astype(o_ref.dtype)

    if mode == "nn":
        a_spec = pl.BlockSpec((tm, tk), lambda i, j, k: (i, k))
        b_spec = pl.BlockSpec((tk, tn), lambda i, j, k: (k, j))
    elif mode == "nt":
        a_spec = pl.BlockSpec((tm, tk), lambda i, j, k: (i, k))
        b_spec = pl.BlockSpec((tn, tk), lambda i, j, k: (j, k))
    else:
        a_spec = pl.BlockSpec((tk, tm), lambda i, j, k: (k, i))
        b_spec = pl.BlockSpec((tk, tn), lambda i, j, k: (k, j))
    o_spec = pl.BlockSpec((tm, tn), lambda i, j, k: (i, j))
    in_specs = [a_spec, b_spec] + ([] if res is None else [o_spec])
    args = (a, b) + (() if res is None else (res,))
    return _pcall(
        body, name=name, dims=("parallel", "parallel", "arbitrary"),
        grid=(M // tm, N // tn, nk), in_specs=in_specs, out_specs=o_spec,
        out_shape=jax.ShapeDtypeStruct((M, N), out_dtype),
        scratch_shapes=[] if in_place else [pltpu.VMEM((tm, tn), F32)],
    )(*args)


EXCHANGE_DTYPE = jnp.bfloat16


def _dw(a, b, *, name):
    return _mm(a, b, mode="tn", name=name, out_dtype=EXCHANGE_DTYPE)


def _rms_parts(xf):
    r = lax.rsqrt(jnp.mean(xf * xf, axis=-1, keepdims=True) + NORM_EPS)
    return xf * r, r


def _rms_bwd_rows(xf, g, dy):
    xhat, r = _rms_parts(xf)
    dxhat = dy * g
    dx = r * (dxhat - xhat * jnp.mean(dxhat * xhat, axis=-1, keepdims=True))
    return dx, dy * xhat


def _dx_norm_bwd(a, w, x, g, dres, *, name, res=None, tm=1024, tk=1408):
    (M, K), N = a.shape, w.shape[0]
    tm, tk = _tile(M, tm), _tile(K, tk)
    nk = K // tk

    def body(*refs):
        refs = list(refs)
        a_ref, w_ref, x_ref, g_ref, dr_ref = refs[:5]
        r_ref = refs[5] if res is not None else None
        dx_ref, dg_ref = refs[-2:]
        i, k = pl.program_id(0), pl.program_id(1)
        part = _dot(a_ref[...], w_ref[...], NT)

        @pl.when(k == 0)
        def _():
            dx_ref[...] = part if res is None else part + r_ref[...]

        @pl.when(k > 0)
        def _():
            dx_ref[...] += part

        @pl.when(k == nk - 1)
        def _():
            dx, dgp = _rms_bwd_rows(x_ref[...], g_ref[...], dx_ref[...])
            dx_ref[...] = dx + dr_ref[...]

            @pl.when(i == 0)
            def _():
                dg_ref[...] = jnp.zeros_like(dg_ref)

            dg_ref[...] += jnp.sum(dgp, axis=0, keepdims=True)

    row = pl.BlockSpec((tm, N), lambda i, k: (i, 0))
    vec = pl.BlockSpec((1, N), lambda i, k: (0, 0))
    in_specs = [pl.BlockSpec((tm, tk), lambda i, k: (i, k)), pl.BlockSpec((N, tk), lambda i, k: (0, k)), row, vec, row]
    args = [a, w, x, g.reshape(1, N), dres]
    if res is not None:
        in_specs.append(row)
        args.append(res)
    return _pcall(
        body, name=name, dims=("arbitrary", "arbitrary"), grid=(M // tm, nk), in_specs=in_specs,
        out_specs=[row, vec], out_shape=[_sds((M, N)), _sds((1, N))],
    )(*args)


def _rmsnorm(x, g, *, name, out_dtype=MXU_DTYPE, tm=512):
    M, D = x.shape
    tm = _tile(M, tm)

    def body(x_ref, g_ref, o_ref):
        xhat, _ = _rms_parts(x_ref[...].astype(F32))
        o_ref[...] = (xhat * g_ref[...]).astype(o_ref.dtype)

    return _pcall(
        body, name=name, dims=("parallel",), grid=(M // tm,),
        in_specs=[pl.BlockSpec((tm, D), lambda i: (i, 0)), pl.BlockSpec((1, D), lambda i: (0, 0))],
        out_specs=pl.BlockSpec((tm, D), lambda i: (i, 0)),
        out_shape=jax.ShapeDtypeStruct((M, D), out_dtype),
    )(x, g.reshape(1, D))


def _rmsnorm_bwd(x, g, dy, *, name, dres=None, tm=512):
    M, D = x.shape
    tm = _tile(M, tm)

    def body(*refs):
        if dres is None:
            x_ref, g_ref, dy_ref, dx_ref, dg_ref = refs
        else:
            x_ref, g_ref, dy_ref, dr_ref, dx_ref, dg_ref = refs
        dx, dgp = _rms_bwd_rows(x_ref[...].astype(F32), g_ref[...], dy_ref[...].astype(F32))
        if dres is not None:
            dx = dx + dr_ref[...]
        dx_ref[...] = dx

        @pl.when(pl.program_id(0) == 0)
        def _():
            dg_ref[...] = jnp.zeros_like(dg_ref)

        dg_ref[...] += jnp.sum(dgp, axis=0, keepdims=True)

    row = pl.BlockSpec((tm, D), lambda i: (i, 0))
    vec = pl.BlockSpec((1, D), lambda i: (0, 0))
    in_specs = [row, vec, row] + ([] if dres is None else [row])
    args = (x, g.reshape(1, D), dy) + (() if dres is None else (dres,))
    return _pcall(
        body, name=name, dims=("arbitrary",), grid=(M // tm,), in_specs=in_specs, out_specs=[row, vec],
        out_shape=[jax.ShapeDtypeStruct((M, D), F32), jax.ShapeDtypeStruct((1, D), F32)],
    )(*args)


def _rope_chunk(t, c, s, half):
    lane = _lane(t.shape)
    swapped = jnp.where((lane % (2 * half)) < half, pltpu.roll(t, LANES - half, 1), pltpu.roll(t, half, 1))
    return t * c + swapped * s


def _rope_tables(positions):
    pos = positions.reshape(-1).astype(F32)[:, None]

    def table(dh, first, copies, sine, fill=0.0):
        half = dh // 2
        lane = np.arange(LANES)
        inside = (lane >= first) & (lane < first + copies * dh)
        idx = np.where(inside, (lane - first) % half, 0)
        inv_freq = ROPE_THETA ** (-jnp.asarray(2 * idx, F32) / dh)
        sign = np.where((lane - first) % dh < half, -1.0, 1.0) if sine else np.ones(LANES)
        ang = pos * inv_freq[None, :]
        val = (jnp.sin(ang) if sine else jnp.cos(ang)) * jnp.asarray(sign, F32)[None, :]
        return jnp.where(jnp.asarray(inside)[None, :], val, fill)

    return dict(
        c64=table(HEAD_DIM, 0, 2, False), s64=table(HEAD_DIM, 0, 2, True),
        ck=table(MLA_ROPE, 0, 1, False), sk=table(MLA_ROPE, 0, 1, True),
        cm=jnp.where(jnp.asarray(np.arange(LANES) < MLA_NOPE)[None, :], 1.0, table(MLA_ROPE, MLA_NOPE, 1, False)),
        sm=table(MLA_ROPE, MLA_NOPE, 1, True),
    )


class _Attn:
    def __init__(self, *, T, Tk, G, nh, rep, dqk, dv, tq, tk, mode, scale, qcol, kcol, vcol, ocol, o_width,
                 max_dist=0):
        self.__dict__.update(locals())
        self.nkv = nh // rep
        assert T % tq == 0 and Tk % tk == 0 and nh <= LANES


def _attn_delta(cfg, o, do, *, name, lse=None, sink=None, tm=512):
    c = cfg
    tm = _tile(c.T, tm)
    width = c.nh * c.dv

    def body(*refs):
        refs = list(refs)
        o_ref, do_ref = refs[:2]
        rest = refs[2:]
        lse_ref, sink_ref = (rest.pop(0), rest.pop(0)) if sink is not None else (None, None)
        d_ref = rest.pop(0)
        prod = o_ref[...].astype(F32) * do_ref[...].astype(F32)
        cols = [jnp.sum(prod[:, j * c.dv:(j + 1) * c.dv], axis=1, keepdims=True) for j in range(c.nh)]
        delta = _cols_to_lanes(cols, tm)
        d_ref[...] = delta
        if sink is not None:
            ds_ref = rest.pop(0)

            @pl.when(pl.program_id(1) == 0)
            def _():
                ds_ref[...] = jnp.zeros_like(ds_ref)

            lane = _lane((tm, LANES))
            ps = jnp.where(lane < c.nh, jnp.exp(sink_ref[...] - lse_ref[...]), 0.0)
            ds_ref[...] -= jnp.sum(ps * delta, axis=0, keepdims=True)

    stat = pl.BlockSpec((tm, LANES), lambda g, i: (i, g))
    in_specs = [pl.BlockSpec((tm, width), lambda g, i: (i, c.ocol(g)))] * 2
    args = [o, do]
    out_specs, out_shape = [stat], [jax.ShapeDtypeStruct((c.T, LANES * c.G), F32)]
    if sink is not None:
        assert c.G == 1
        in_specs += [stat, pl.BlockSpec((1, LANES), lambda g, i: (0, 0))]
        args += [lse, sink]
        out_specs.append(pl.BlockSpec((1, LANES), lambda g, i: (0, 0)))
        out_shape.append(jax.ShapeDtypeStruct((1, LANES), F32))
    out = _pcall(
        body, name=name, dims=("arbitrary", "arbitrary"), grid=(c.G, c.T // tm),
        in_specs=in_specs, out_specs=out_specs, out_shape=out_shape,
    )(*args)
    return out if sink is not None else (out[0], None)


TN = (((0,), (0,)), ((), ()))


def _band_mask(c, i):
    key = lax.broadcasted_iota(jnp.int32, (2 * BLOCK, BLOCK), 0)
    qry = lax.broadcasted_iota(jnp.int32, (2 * BLOCK, BLOCK), 1)
    d = BLOCK + qry - key
    return (d >= 0) & (d <= c.max_dist) & ((key >= BLOCK) | (i > 0))


def _head_pairs(c):
    return c.rep == 1 and c.dqk == c.dv == LANES // 2 and c.nh % 2 == 0


def _block_diagonal(pair):
    lane = _lane(pair.shape)
    zero = jnp.zeros_like(pair)
    return jnp.concatenate([jnp.where(lane < LANES // 2, pair, zero), jnp.where(lane >= LANES // 2, pair, zero)], axis=0)


def _own_blocks(t):
    n = t.shape[1] // 2
    rows = lax.broadcasted_iota(jnp.int32, (LANES, n), 0)
    return jnp.where(rows < LANES // 2, t[:, :n], t[:, n:])


def _rows_to_stats(rows, n):
    return jnp.concatenate(rows + [jnp.zeros((LANES - len(rows), n), F32)], axis=0).T


def _band_fwd(cfg, q, k, v, *, name, sink=None, out_dtype=F32):
    c = cfg
    assert c.mode == "band" and c.tq == c.tk == BLOCK and c.T == c.Tk
    nq = c.T // BLOCK

    def body(*refs):
        if sink is None:
            q_ref, kp_ref, kc_ref, vp_ref, vc_ref, o_ref, lse_ref = refs
        else:
            q_ref, kp_ref, kc_ref, vp_ref, vc_ref, sink_ref, o_ref, lse_ref = refs
        mask = _band_mask(c, pl.program_id(1))
        k2 = jnp.concatenate([kp_ref[...], kc_ref[...]], axis=0)
        v2 = jnp.concatenate([vp_ref[...], vc_ref[...]], axis=0)
        lses = []
        if _head_pairs(c):
            mask2 = jnp.concatenate([mask, mask], axis=1)
            pair_lanes = [slice(pc * LANES, (pc + 1) * LANES) for pc in range(c.nh // 2)]
            score = lambda sl: _dot(k2[:, sl], _block_diagonal(q_ref[:, sl]), NT)
            ahead, behind = score(pair_lanes[0]), None

            def finish(entry):
                sl, o_t, l = entry
                o_ref[:, sl] = _own_blocks(o_t / l).T.astype(o_ref.dtype)

            for pc, sl in enumerate(pair_lanes):
                s = ahead * c.scale
                if pc + 1 < len(pair_lanes):
                    ahead = score(pair_lanes[pc + 1])
                s = jnp.where(mask2, s, -jnp.inf)
                m = jnp.max(s, axis=0, keepdims=True)
                p = jnp.exp(s - m)
                l = jnp.sum(p, axis=0, keepdims=True)
                if behind is not None:
                    finish(behind)
                behind = (sl, _dot(v2[:, sl], p, TN), l)
                lse = m + jnp.log(l)
                lses += [lse[:, :BLOCK], lse[:, BLOCK:]]
            finish(behind)
        heads = [] if _head_pairs(c) else list(range(c.nh))
        score_of = lambda j: _dot(k2[:, (j // c.rep) * c.dqk:(j // c.rep + 1) * c.dqk],
                                  q_ref[:, j * c.dqk:(j + 1) * c.dqk], NT)
        ahead = score_of(0) if heads else None
        for j in heads:
            g = j // c.rep
            s = ahead * c.scale
            if j + 1 < c.nh:
                ahead = score_of(j + 1)
            s = jnp.where(mask, s, -jnp.inf)
            m = jnp.max(s, axis=0, keepdims=True)
            if sink is not None:
                sk = sink_ref[:, j:j + 1]
                m = jnp.maximum(m, sk)
            p = jnp.exp(s - m)
            l = jnp.sum(p, axis=0, keepdims=True)
            if sink is not None:
                l = l + jnp.exp(sk - m)
            o_t = _dot(v2[:, g * c.dv:(g + 1) * c.dv], p, TN)
            o_ref[:, j * c.dv:(j + 1) * c.dv] = (o_t / l).T.astype(o_ref.dtype)
            lses.append(m + jnp.log(l))
        lse_ref[...] = _rows_to_stats(lses, BLOCK)

    prev = lambda i: jnp.maximum(i - 1, 0)
    kw, vw = c.nkv * c.dqk, c.nkv * c.dv
    in_specs = [
        pl.BlockSpec((BLOCK, c.nh * c.dqk), lambda g, i: (i, c.qcol(g))),
        pl.BlockSpec((BLOCK, kw), lambda g, i: (prev(i), c.kcol(g))),
        pl.BlockSpec((BLOCK, kw), lambda g, i: (i, c.kcol(g))),
        pl.BlockSpec((BLOCK, vw), lambda g, i: (prev(i), c.vcol(g))),
        pl.BlockSpec((BLOCK, vw), lambda g, i: (i, c.vcol(g))),
    ]
    args = [q, k, k, v, v]
    if sink is not None:
        in_specs.append(pl.BlockSpec((1, LANES), lambda g, i: (0, 0)))
        args.append(sink)
    return _pcall(
        body, name=name, dims=("parallel", "parallel"), grid=(c.G, nq), in_specs=in_specs,
        out_specs=[pl.BlockSpec((BLOCK, c.nh * c.dv), lambda g, i: (i, c.ocol(g))),
                   pl.BlockSpec((BLOCK, LANES), lambda g, i: (i, g))],
        out_shape=[jax.ShapeDtypeStruct((c.T, c.o_width), out_dtype),
                   jax.ShapeDtypeStruct((c.T, LANES * c.G), F32)],
    )(*args)


def _band_bwd(cfg, q, k, v, do, lse, delta, *, name):
    c = cfg
    assert c.mode == "band" and c.tq == c.tk == BLOCK and c.T == c.Tk
    nq = c.T // BLOCK
    qw, kw, vw = c.nh * c.dqk, c.nkv * c.dqk, c.nkv * c.dv

    def body(q_ref, kp_ref, kc_ref, vp_ref, vc_ref, do_ref, lse_ref, d_ref, dq_ref, dk_ref, dv_ref, dk_c, dv_c):
        n = pl.program_id(1)

        @pl.when(n == 0)
        def _():
            dk_c[...] = jnp.zeros_like(dk_c)
            dv_c[...] = jnp.zeros_like(dv_c)

        @pl.when(n < nq)
        def _():
            mask = _band_mask(c, n)
            k2 = jnp.concatenate([kp_ref[...], kc_ref[...]], axis=0)
            v2 = jnp.concatenate([vp_ref[...], vc_ref[...]], axis=0)
            lse_t, d_t = lse_ref[...].T, d_ref[...].T
            if _head_pairs(c):
                mask2 = jnp.concatenate([mask, mask], axis=1)
                pair_lanes = [slice(pc * LANES, (pc + 1) * LANES) for pc in range(c.nh // 2)]

                def first(sl):
                    q_bd, do_bd = _block_diagonal(q_ref[:, sl]), _block_diagonal(do_ref[:, sl])
                    return q_bd, do_bd, k2[:, sl], _dot(k2[:, sl], q_bd, NT), _dot(v2[:, sl], do_bd, NT)

                def finish(entry):
                    sl, dq_t, dv_pair, dk_pair = entry
                    dq_ref[:, sl] = _own_blocks(dq_t).T
                    dk_ref[:, sl] = dk_c[:, sl] + dk_pair[:BLOCK]
                    dv_ref[:, sl] = dv_c[:, sl] + dv_pair[:BLOCK]
                    dk_c[:, sl] = dk_pair[BLOCK:]
                    dv_c[:, sl] = dv_pair[BLOCK:]

                ahead, behind = first(pair_lanes[0]), None
                for pc, sl in enumerate(pair_lanes):
                    q_bd, do_bd, kp, s, dp = ahead
                    if pc + 1 < len(pair_lanes):
                        ahead = first(pair_lanes[pc + 1])
                    both = lambda t: jnp.concatenate([t[2 * pc:2 * pc + 1, :], t[2 * pc + 1:2 * pc + 2, :]], axis=1)
                    p = jnp.exp(jnp.where(mask2, s * c.scale, -jnp.inf) - both(lse_t))
                    ds = p * (dp - both(d_t)) * c.scale
                    entry = (sl, _dot(kp, ds, TN), _dot(p, do_bd, NN), _dot(ds, q_bd, NN))
                    if behind is not None:
                        finish(behind)
                    behind = entry
                finish(behind)
                return
            dk2, dv2 = [None] * c.nkv, [None] * c.nkv

            def first_of(j):
                g = j // c.rep
                qh, doh = q_ref[:, j * c.dqk:(j + 1) * c.dqk], do_ref[:, j * c.dv:(j + 1) * c.dv]
                kh = k2[:, g * c.dqk:(g + 1) * c.dqk]
                return qh, doh, kh, _dot(kh, qh, NT), _dot(v2[:, g * c.dv:(g + 1) * c.dv], doh, NT)

            ahead = first_of(0)
            for j in range(c.nh):
                g = j // c.rep
                qh, doh, kh, s, dp = ahead
                if j + 1 < c.nh:
                    ahead = first_of(j + 1)
                p = jnp.exp(jnp.where(mask, s * c.scale, -jnp.inf) - lse_t[j:j + 1, :])
                ds = p * (dp - d_t[j:j + 1, :]) * c.scale
                dq_ref[:, j * c.dqk:(j + 1) * c.dqk] = _dot(kh, ds, TN).T
                dvh, dkh = _dot(p, doh, NN), _dot(ds, qh, NN)
                dv2[g] = dvh if dv2[g] is None else dv2[g] + dvh
                dk2[g] = dkh if dk2[g] is None else dk2[g] + dkh
            for g in range(c.nkv):
                ks, vs = slice(g * c.dqk, (g + 1) * c.dqk), slice(g * c.dv, (g + 1) * c.dv)
                dk_ref[:, ks] = dk_c[:, ks] + dk2[g][:BLOCK]
                dv_ref[:, vs] = dv_c[:, vs] + dv2[g][:BLOCK]
                dk_c[:, ks] = dk2[g][BLOCK:]
                dv_c[:, vs] = dv2[g][BLOCK:]

        @pl.when(n == nq)
        def _():
            dk_ref[...] = dk_c[...]
            dv_ref[...] = dv_c[...]

    cur = lambda n: jnp.minimum(n, nq - 1)
    prev = lambda n: jnp.maximum(cur(n) - 1, 0)
    out_blk = lambda n: jnp.maximum(n - 1, 0)
    stat = pl.BlockSpec((BLOCK, LANES), lambda g, n: (cur(n), g))
    dq_spec = pl.BlockSpec((BLOCK, qw), lambda g, n: (cur(n), g))
    dk_spec = pl.BlockSpec((BLOCK, kw), lambda g, n: (out_blk(n), g))
    dv_spec = pl.BlockSpec((BLOCK, vw), lambda g, n: (out_blk(n), g))
    in_specs = [
        pl.BlockSpec((BLOCK, qw), lambda g, n: (cur(n), c.qcol(g))),
        pl.BlockSpec((BLOCK, kw), lambda g, n: (prev(n), c.kcol(g))),
        pl.BlockSpec((BLOCK, kw), lambda g, n: (cur(n), c.kcol(g))),
        pl.BlockSpec((BLOCK, vw), lambda g, n: (prev(n), c.vcol(g))),
        pl.BlockSpec((BLOCK, vw), lambda g, n: (cur(n), c.vcol(g))),
        pl.BlockSpec((BLOCK, c.nh * c.dv), lambda g, n: (cur(n), c.ocol(g))),
        stat, stat,
    ]
    return _pcall(
        body, name=name, dims=("parallel", "arbitrary"), grid=(c.G, nq + 1), in_specs=in_specs,
        out_specs=[dq_spec, dk_spec, dv_spec],
        out_shape=[_sds((c.T, c.G * qw)), _sds((c.T, c.G * kw)), _sds((c.T, c.G * vw))],
        scratch_shapes=[pltpu.VMEM((BLOCK, kw), F32), pltpu.VMEM((BLOCK, vw), F32)],
    )(q, k, k, v, v, do, lse, delta)


def _causal_pairs(n, kv_major):
    pairs =[(i, j) for j in range(n) for i in range(j, n)] if kv_major else [(i, j) for i in range(n) for j in range(i + 1)]
    return jnp.asarray(np.array([p[0] for p in pairs], np.int32)), jnp.asarray(np.array([p[1] for p in pairs], np.int32))


def _causal_mask(t):
    return lax.broadcasted_iota(jnp.int32, (t, t), 0) >= lax.broadcasted_iota(jnp.int32, (t, t), 1)


def _carrying(body, n_in, n_out, n_scratch, grid, carry):
    if carry is None:
        return body
    G, P = grid

    def wrapped(*refs):
        refs = list(refs)
        prefetch, refs = refs[:2], refs[2:]
        ins, src = refs[:n_in], refs[n_in]
        outs, out = refs[n_in + 1:n_in + 1 + n_out], refs[n_in + 1 + n_out]
        scratch, sems = refs[n_in + 2 + n_out:n_in + 2 + n_out + n_scratch], refs[n_in + 2 + n_out + n_scratch:]
        step = pl.program_id(0) * P + pl.program_id(1)
        carry.run([src, out] + sems, step, G * P, at_end=False)
        body(*prefetch, *ins, *outs, *scratch)
        carry.run([src, out] + sems, step, G * P, at_end=True)

    return wrapped


def _carry_specs(carry):
    if carry is None:
        return [], [], [], [], []
    any_space = pl.BlockSpec(memory_space=pl.ANY)
    return [any_space], [any_space], [carry.out_shape], list(carry.sems), [carry.src]


def _causal_fwd(cfg, q, k, v, *, name, out_dtype=F32, stat_heads=None, carry=None):
    c = cfg
    assert c.mode == "causal" and c.tq == c.tk and c.T == c.Tk
    t, n = c.tq, c.T // c.tq
    stat_heads = stat_heads or c.nh
    stat_blocks = c.nh // stat_heads
    assert stat_blocks * stat_heads == c.nh
    qi_tab, kj_tab = _causal_pairs(n, kv_major=False)
    n_pairs = int(qi_tab.shape[0])

    def body(qi_ref, kj_ref, q_ref, k_ref, v_ref, o_ref, lse_ref, m_scr, l_scr, acc):
        pair = pl.program_id(1)
        qi, kj = qi_ref[pair], kj_ref[pair]

        @pl.when(kj == 0)
        def _():
            m_scr[...] = jnp.full_like(m_scr, NEG_BIG)
            l_scr[...] = jnp.zeros_like(l_scr)
            acc[...] = jnp.zeros_like(acc)

        def step(diagonal):
            mask = None
            if diagonal:
                mask = lax.broadcasted_iota(jnp.int32, (t, t), 1) >= lax.broadcasted_iota(jnp.int32, (t, t), 0)
            scores = [_dot(k_ref[:, (j // c.rep) * c.dqk:(j // c.rep + 1) * c.dqk],
                           q_ref[:, j * c.dqk:(j + 1) * c.dqk], NT) for j in range(c.nh)]
            for j in range(c.nh):
                g = j // c.rep
                s = scores[j] * c.scale
                if diagonal:
                    s = jnp.where(mask, s, -jnp.inf)
                m_prev = m_scr[j]
                m_new = jnp.maximum(m_prev, jnp.max(s, axis=0, keepdims=True))
                alpha = jnp.exp(m_prev - m_new)
                p = jnp.exp(s - m_new)
                l_scr[j] = alpha * l_scr[j] + jnp.sum(p, axis=0, keepdims=True)
                acc[j] = alpha * acc[j] + _dot(v_ref[:, g * c.dv:(g + 1) * c.dv], p, TN)
                m_scr[j] = m_new

        pl.when(kj == qi)(lambda: step(True))
        pl.when(kj != qi)(lambda: step(False))

        @pl.when(kj == qi)
        def _():
            rows = []
            for j in range(c.nh):
                o_ref[:, j * c.dv:(j + 1) * c.dv] = (acc[j] / l_scr[j]).T.astype(o_ref.dtype)
                rows.append(m_scr[j] + jnp.log(l_scr[j]))
            for b in range(stat_blocks):
                lse_ref[:, b * LANES:(b + 1) * LANES] = _rows_to_stats(rows[b * stat_heads:(b + 1) * stat_heads], t)

    x_in, x_out, x_shapes, x_scratch, x_args = _carry_specs(carry)
    grid_spec = pltpu.PrefetchScalarGridSpec(
        num_scalar_prefetch=2, grid=(c.G, n_pairs),
        in_specs=[pl.BlockSpec((t, c.nh * c.dqk), lambda g, p, qi, kj: (qi[p], c.qcol(g))),
                  pl.BlockSpec((t, c.nkv * c.dqk), lambda g, p, qi, kj: (kj[p], c.kcol(g))),
                  pl.BlockSpec((t, c.nkv * c.dv), lambda g, p, qi, kj: (kj[p], c.vcol(g)))] + x_in,
        out_specs=[pl.BlockSpec((t, c.nh * c.dv), lambda g, p, qi, kj: (qi[p], c.ocol(g))),
                   pl.BlockSpec((t, LANES * stat_blocks), lambda g, p, qi, kj: (qi[p], g))] + x_out,
        scratch_shapes=[pltpu.VMEM((c.nh, 1, t), F32), pltpu.VMEM((c.nh, 1, t), F32),
                        pltpu.VMEM((c.nh, c.dv, t), F32)] + x_scratch)
    return _pcall(
        _carrying(body, 3, 2, 3, (c.G, n_pairs), carry), name=name,
        dims=("arbitrary", "arbitrary") if carry is not None else ("parallel", "arbitrary"), grid_spec=grid_spec,
        out_shape=[jax.ShapeDtypeStruct((c.T, c.o_width), out_dtype),
                   jax.ShapeDtypeStruct((c.T, LANES * c.G * stat_blocks), F32)] + x_shapes,
    )(qi_tab, kj_tab, q, k, v, *x_args)


def _causal_bwd(cfg, q, k, v, do, lse, delta, *, name, carry=None):
    c = cfg
    assert c.mode == "causal" and c.tq == c.tk and c.T == c.Tk
    t, n = c.tq, c.T // c.tq
    qw, kw, vw = c.nh * c.dqk, c.nkv * c.dqk, c.nkv * c.dv
    qi_tab, kj_tab = _causal_pairs(n, kv_major=True)

    def body(qi_ref, kj_ref, q_ref, k_ref, v_ref, do_ref, lse_ref, d_ref, dq_ref, dk_ref, dv_ref, dk_acc, dv_acc):
        pair = pl.program_id(1)
        qi, kj = qi_ref[pair], kj_ref[pair]

        @pl.when(pair == 0)
        def _():
            dq_ref[...] = jnp.zeros_like(dq_ref)

        @pl.when(qi == kj)
        def _():
            dk_acc[...] = jnp.zeros_like(dk_acc)
            dv_acc[...] = jnp.zeros_like(dv_acc)

        rows = pl.ds(pl.multiple_of(qi * t, t), t)

        def step(diagonal):
            mask = _causal_mask(t) if diagonal else None
            for j in range(c.nh):
                g = j // c.rep
                qs, ks, vs = (slice(j * c.dqk, (j + 1) * c.dqk), slice(g * c.dqk, (g + 1) * c.dqk),
                              slice(g * c.dv, (g + 1) * c.dv))
                qh, doh, kh = q_ref[:, qs], do_ref[:, j * c.dv:(j + 1) * c.dv], k_ref[:, ks]
                s = _dot(qh, kh, NT) * c.scale
                if diagonal:
                    s = jnp.where(mask, s, -jnp.inf)
                p = jnp.exp(s - lse_ref[:, j:j + 1])
                ds = p * (_dot(doh, v_ref[:, vs], NT) - d_ref[:, j:j + 1]) * c.scale
                dq_ref[rows, qs] += _dot(ds, kh, NN)
                dv_acc[g] += _dot(doh, p, TN)
                dk_acc[g] += _dot(qh, ds, TN)

        pl.when(qi == kj)(lambda: step(True))
        pl.when(qi != kj)(lambda: step(False))

        @pl.when(qi == n - 1)
        def _():
            for g in range(c.nkv):
                dk_ref[:, g * c.dqk:(g + 1) * c.dqk] = dk_acc[g].T
                dv_ref[:, g * c.dv:(g + 1) * c.dv] = dv_acc[g].T

    stat = pl.BlockSpec((t, LANES), lambda g, p, qi, kj: (qi[p], g))
    o_spec = pl.BlockSpec((t, c.nh * c.dv), lambda g, p, qi, kj: (qi[p], c.ocol(g)))
    n_pairs = int(qi_tab.shape[0])
    x_in, x_out, x_shapes, x_scratch, x_args = _carry_specs(carry)
    grid_spec = pltpu.PrefetchScalarGridSpec(
        num_scalar_prefetch=2, grid=(c.G, n_pairs),
        in_specs=[pl.BlockSpec((t, qw), lambda g, p, qi, kj: (qi[p], c.qcol(g))),
                  pl.BlockSpec((t, kw), lambda g, p, qi, kj: (kj[p], c.kcol(g))),
                  pl.BlockSpec((t, vw), lambda g, p, qi, kj: (kj[p], c.vcol(g))),
                  o_spec, stat, stat] + x_in,
        out_specs=[pl.BlockSpec((c.T, qw), lambda g, p, qi, kj: (0, g)),
                   pl.BlockSpec((t, kw), lambda g, p, qi, kj: (kj[p], g)),
                   pl.BlockSpec((t, vw), lambda g, p, qi, kj: (kj[p], g))] + x_out,
        scratch_shapes=[pltpu.VMEM((c.nkv, c.dqk, t), F32), pltpu.VMEM((c.nkv, c.dv, t), F32)] + x_scratch)
    return _pcall(
        _carrying(body, 6, 3, 2, (c.G, n_pairs), carry), name=name,
        dims=("arbitrary", "arbitrary") if carry is not None else ("parallel", "arbitrary"), grid_spec=grid_spec,
        out_shape=[_sds((c.T, c.G * qw)), _sds((c.T, c.G * kw)), _sds((c.T, c.G * vw))] + x_shapes,
    )(qi_tab, kj_tab, q, k, v, do, lse, delta, *x_args)


def _rowwise(body, ins, outs, *, name, rows, tm=512, accs=(), scratch=()):
    tm = _row_tile(rows, tm)

    def spec(a):
        if a.shape[0] == 1:
            return pl.BlockSpec((1, a.shape[1]), lambda i: (0, 0))
        d = rows // a.shape[0]
        assert d * a.shape[0] == rows and tm % d == 0
        return pl.BlockSpec((tm // d, a.shape[1]), lambda i: (i, 0))

    return _pcall(
        functools.partial(body, tm), name=name, dims=("arbitrary" if accs else "parallel",), grid=(rows // tm,),
        in_specs=[spec(a) for a in ins], out_specs=[spec(a) for a in outs], out_shape=list(outs),
        scratch_shapes=list(scratch),
    )(*ins)


def _sds(shape, dtype=F32):
    return jax.ShapeDtypeStruct(shape, dtype)


def _acc_rows(ref, val):
    @pl.when(pl.program_id(0) == 0)
    def _():
        ref[...] = jnp.zeros_like(ref)

    ref[...] += jnp.sum(val, axis=0, keepdims=True)


Z_QA, Z_KA, Z_VA, Z_CQ, Z_CKV, Z_KR, Z_END = 0, 512, 640, 768, 1152, 1408, 1536


def _l0_prep(z, tabs, q_norm, kv_norm, *, name):
    S = z.shape[0]

    def body(tm, z_ref, c64, s64, ck, sk, gq, gkv, qa_o, ka_o, va_o, cq_o, ckv_o, kr_o):
        for i in range(4):
            sl = slice(Z_QA + i * LANES, Z_QA + (i + 1) * LANES)
            qa_o[:, i * LANES:(i + 1) * LANES] = _rope_chunk(z_ref[:, sl], c64[...], s64[...], 32).astype(qa_o.dtype)
        ka_o[...] = _rope_chunk(z_ref[:, Z_KA:Z_VA], c64[...], s64[...], 32).astype(ka_o.dtype)
        va_o[...] = z_ref[:, Z_VA:Z_CQ].astype(va_o.dtype)
        cq_o[...] = (_rms_parts(z_ref[:, Z_CQ:Z_CKV])[0] * gq[...]).astype(cq_o.dtype)
        ckv_o[...] = (_rms_parts(z_ref[:, Z_CKV:Z_KR])[0] * gkv[...]).astype(ckv_o.dtype)
        kr_o[...] = _rope_chunk(z_ref[:, Z_KR:Z_END], ck[...], sk[...], 16)

    outs = [_sds((S, 512), MXU_DTYPE), _sds((S, 128), MXU_DTYPE), _sds((S, 128), MXU_DTYPE),
            _sds((S, MLA_Q_RANK), MXU_DTYPE), _sds((S, MLA_KV_RANK), MXU_DTYPE), _sds((S, LANES))]
    ins = [z, tabs["c64"], tabs["s64"], tabs["ck"], tabs["sk"], q_norm.reshape(1, -1), kv_norm.reshape(1, -1)]
    return _rowwise(body, ins, outs, name=name, rows=S)


def _l0_prep_bwd(z, tabs, q_norm, kv_norm, dqa, dka, dva, dcq, dckv, dkr, *, name):
    S = z.shape[0]

    def body(tm, z_ref, c64, s64, ck, sk, gq, gkv, dqa_r, dka_r, dva_r, dcq_r, dckv_r, dkr_r, dz_o, dgq_o, dgkv_o):
        for i in range(4):
            sl = slice(i * LANES, (i + 1) * LANES)
            dz_o[:, sl] = _rope_chunk(dqa_r[:, sl].astype(F32), c64[...], -s64[...], 32).astype(dz_o.dtype)
        dz_o[:, Z_KA:Z_VA] = _rope_chunk(dka_r[...].astype(F32), c64[...], -s64[...], 32).astype(dz_o.dtype)
        dz_o[:, Z_VA:Z_CQ] = dva_r[...].astype(dz_o.dtype)
        dx, dgp = _rms_bwd_rows(z_ref[:, Z_CQ:Z_CKV], gq[...], dcq_r[...].astype(F32))
        dz_o[:, Z_CQ:Z_CKV] = dx.astype(dz_o.dtype)
        _acc_rows(dgq_o, dgp)
        dx, dgp = _rms_bwd_rows(z_ref[:, Z_CKV:Z_KR], gkv[...], dckv_r[...].astype(F32))
        dz_o[:, Z_CKV:Z_KR] = dx.astype(dz_o.dtype)
        _acc_rows(dgkv_o, dgp)
        dz_o[:, Z_KR:Z_END] = _rope_chunk(dkr_r[...], ck[...], -sk[...], 16).astype(dz_o.dtype)

    outs = [_sds((S, Z_END), MXU_DTYPE), _sds((1, MLA_Q_RANK)), _sds((1, MLA_KV_RANK))]
    ins = [z, tabs["c64"], tabs["s64"], tabs["ck"], tabs["sk"], q_norm.reshape(1, -1), kv_norm.reshape(1, -1),
           dqa, dka, dva, dcq, dckv, dkr]
    return _rowwise(body, ins, outs, name=name, rows=S, accs=(1, 2))


def _mla_prep(qb, kvb, kr, tabs, *, name):
    S = qb.shape[0]

    def body(tm, qb_r, kvb_r, kr_r, cm, sm, q_o, k_o, v_o):
        lane = _lane((tm, LANES))
        kr_at_64 = pltpu.roll(kr_r[...], 64, 1)
        for h in range(MLA_HEADS):
            sl = slice(h * LANES, (h + 1) * LANES)
            q_o[:, sl] = _rope_chunk(qb_r[:, sl], cm[...], sm[...], 16).astype(q_o.dtype)
            k_o[:, sl] = jnp.where(lane < 64, kvb_r[:, sl], kr_at_64).astype(k_o.dtype)
        for p in range(MLA_HEADS // 2):
            even = pltpu.roll(kvb_r[:, (2 * p) * LANES:(2 * p + 1) * LANES], 64, 1)
            odd = kvb_r[:, (2 * p + 1) * LANES:(2 * p + 2) * LANES]
            v_o[:, p * LANES:(p + 1) * LANES] = jnp.where(lane < 64, even, odd).astype(v_o.dtype)

    outs = [_sds((S, 1024), MXU_DTYPE), _sds((S, 1024), MXU_DTYPE), _sds((S, 512), MXU_DTYPE)]
    return _rowwise(body, [qb, kvb, kr, tabs["cm"], tabs["sm"]], outs, name=name, rows=S)


def _mla_prep_bwd(dq, dk, dv, tabs, *, name):
    S = dq.shape[0]

    def body(tm, dq_r, dk_r, dv_r, cm, sm, dqb_o, dkvb_o, dkr_o):
        lane = _lane((tm, LANES))
        dkr = jnp.zeros((tm, LANES), F32)
        for h in range(MLA_HEADS):
            sl = slice(h * LANES, (h + 1) * LANES)
            dqb_o[:, sl] = _rope_chunk(dq_r[:, sl].astype(F32), cm[...], -sm[...], 16).astype(dqb_o.dtype)
            dkh = dk_r[:, sl].astype(F32)
            dvp = dv_r[:, (h // 2) * LANES:(h // 2 + 1) * LANES].astype(F32)
            dvh = pltpu.roll(dvp, 64, 1) if h % 2 == 0 else dvp
            dkvb_o[:, sl] = jnp.where(lane < 64, dkh, dvh).astype(dkvb_o.dtype)
            dkr = dkr + pltpu.roll(dkh, 64, 1)
        dkr_o[...] = jnp.where(lane < MLA_ROPE, dkr, 0.0)

    outs = [_sds((S, 1024), MXU_DTYPE), _sds((S, 1024), MXU_DTYPE), _sds((S, LANES))]
    return _rowwise(body, [dq, dk, dv, tabs["cm"], tabs["sm"]], outs, name=name, rows=S)


DILATIONS = tuple(d for _, d in DIL_PATTERNS)
QKV_CHUNKS = 8


def _to_branch(nat, c0, chunks, out_ref, d, rows):
    width = chunks * LANES
    for r in range(d):
        tok = pl.ds(r, rows // d, stride=d) if d > 1 else slice(None)
        for c in range(chunks):
            out_ref[:, r * width + c * LANES:r * width + (c + 1) * LANES] = nat[c0 + c, tok, :].astype(out_ref.dtype)


def _from_branch(in_ref, nat, c0, chunks, d, rows, add=False):
    width = chunks * LANES
    for r in range(d):
        tok = pl.ds(r, rows // d, stride=d) if d > 1 else slice(None)
        for c in range(chunks):
            val = in_ref[:, r * width + c * LANES:r * width + (c + 1) * LANES].astype(F32)
            nat[c0 + c, tok, :] = nat[c0 + c, tok, :] + val if add else val


def _branch_sds(S, width, d, dtype):
    return _sds((S // d, d * width), dtype)


def _l1_prep(qkv, tabs, *, name):
    S = qkv.shape[0]

    def body(tm, x_r, c64, s64, *rest):
        outs, nat = rest[:-1], rest[-1]
        for i in range(QKV_CHUNKS):
            sl = slice(i * LANES, (i + 1) * LANES)
            nat[i] = _rope_chunk(x_r[:, sl], c64[...], s64[...], 32)
            nat[QKV_CHUNKS + i] = _rope_chunk(x_r[:, 1024 + i * LANES:1024 + (i + 1) * LANES], c64[...], s64[...], 32)
            nat[2 * QKV_CHUNKS + i] = x_r[:, 2048 + i * LANES:2048 + (i + 1) * LANES]
        for b, d in enumerate(DILATIONS):
            for t in range(3):
                _to_branch(nat, t * QKV_CHUNKS, QKV_CHUNKS, outs[3 * b + t], d, tm)

    outs = [_branch_sds(S, 1024, d, MXU_DTYPE) for d in DILATIONS for _ in range(3)]
    got = _rowwise(body, [qkv, tabs["c64"], tabs["s64"]], outs, name=name, rows=S,
                   scratch=[pltpu.VMEM((3 * QKV_CHUNKS, _row_tile(S, 512), LANES), F32)])
    return {d: tuple(got[3 * b:3 * b + 3]) for b, d in enumerate(DILATIONS)}


def _l1_prep_bwd(grads, tabs, *, name):
    S = grads[1][0].shape[0]

    def body(tm, *rest):
        ins, (c64, s64, o, nat) = rest[:9], rest[9:]
        for b, d in enumerate(DILATIONS):
            for t in range(3):
                _from_branch(ins[3 * b + t], nat, t * QKV_CHUNKS, QKV_CHUNKS, d, tm, add=b > 0)
        for i in range(QKV_CHUNKS):
            sl = slice(i * LANES, (i + 1) * LANES)
            o[:, sl] = _rope_chunk(nat[i], c64[...], -s64[...], 32).astype(o.dtype)
            o[:, 1024 + i * LANES:1024 + (i + 1) * LANES] = _rope_chunk(
                nat[QKV_CHUNKS + i], c64[...], -s64[...], 32).astype(o.dtype)
            o[:, 2048 + i * LANES:2048 + (i + 1) * LANES] = nat[2 * QKV_CHUNKS + i].astype(o.dtype)

    ins = [g for d in DILATIONS for g in grads[d]] + [tabs["c64"], tabs["s64"]]
    return _rowwise(body, ins, [_sds((S, 3072), MXU_DTYPE)], name=name, rows=S, tm=256,
                    scratch=[pltpu.VMEM((3 * QKV_CHUNKS, _row_tile(S, 256), LANES), F32)])[0]


def _sigmoid(x):
    return 1.0 / (1.0 + jnp.exp(-x))


FFN_ROW_TILE, FFN_COL_TILE = 1024, 1408


def _gate_up(h, w_gate, w_up, *, name):
    (M, K), N = h.shape, w_gate.shape[1]
    tm, tn = _tile(M, FFN_ROW_TILE), _tile(N, FFN_COL_TILE)

    def body(h_ref, wg_ref, wu_ref, g_ref, u_ref, a_ref):
        g = _dot(h_ref[...], wg_ref[...], NN)
        u = _dot(h_ref[...], wu_ref[...], NN)
        g_ref[...] = g
        u_ref[...] = u
        a_ref[...] = (g * _sigmoid(g) * u).astype(a_ref.dtype)

    w_spec = pl.BlockSpec((K, tn), lambda j, i: (0, j))
    o_spec = pl.BlockSpec((tm, tn), lambda j, i: (i, j))
    return _pcall(
        body, name=name, dims=("parallel", "parallel"), grid=(N // tn, M // tm),
        in_specs=[pl.BlockSpec((tm, K), lambda j, i: (i, 0)), w_spec, w_spec], out_specs=[o_spec] * 3,
        out_shape=[_sds((M, N)), _sds((M, N)), _sds((M, N), MXU_DTYPE)],
    )(h, w_gate, w_up)


def _gate_up_bwd(dx, w_down, gate, up, *, name):
    (M, K), N = dx.shape, w_down.shape[0]
    tm, tn = _tile(M, FFN_ROW_TILE), _tile(N, FFN_COL_TILE)

    def body(dx_ref, w_ref, g_ref, u_ref, dg_ref, du_ref):
        d = _dot(dx_ref[...], w_ref[...], NT)
        g = g_ref[...]
        sg = _sigmoid(g)
        dg_ref[...] = (d * u_ref[...] * (sg * (1.0 + g * (1.0 - sg)))).astype(dg_ref.dtype)
        du_ref[...] = (d * g * sg).astype(du_ref.dtype)

    o_spec = pl.BlockSpec((tm, tn), lambda j, i: (i, j))
    return _pcall(
        body, name=name, dims=("parallel", "parallel"), grid=(N // tn, M // tm),
        in_specs=[pl.BlockSpec((tm, K), lambda j, i: (i, 0)), pl.BlockSpec((tn, K), lambda j, i: (j, 0)),
                  o_spec, o_spec],
        out_specs=[o_spec] * 2, out_shape=[_sds((M, N), MXU_DTYPE)] * 2,
    )(dx, w_down, gate, up)


def _head_pair_weights(w, c, rows):
    return jnp.where(_lane((rows, LANES)) < HEAD_DIM, w[:, 2 * c:2 * c + 1], w[:, 2 * c + 1:2 * c + 2])


def _merge(outs_by_d, lses_by_d, *, name):
    S = outs_by_d[1].shape[0]
    far = DILATIONS[1:]

    def body(tm, o1, o4, o16, l1, l4, l16, o_o, w1_o, w4_o, w16_o, nat_o, nat_l):
        for b, (o_r, l_r, d) in enumerate(zip((o4, o16), (l4, l16), far)):
            _from_branch(o_r, nat_o, b * QKV_CHUNKS, QKV_CHUNKS, d, tm)
            _from_branch(l_r, nat_l, b, 1, d, tm)
        ls = [l1[...], nat_l[0], nat_l[1]]
        m = jnp.maximum(jnp.maximum(ls[0], ls[1]), ls[2])
        es = [jnp.exp(l - m) for l in ls]
        tot = es[0] + es[1] + es[2]
        ws = [e / tot for e in es]
        for w_o, w in zip((w1_o, w4_o, w16_o), ws):
            w_o[...] = w
        for c in range(QKV_CHUNKS):
            sl = slice(c * LANES, (c + 1) * LANES)
            parts = (o1[:, sl], nat_o[c], nat_o[QKV_CHUNKS + c])
            o_o[:, sl] = sum(_head_pair_weights(w, c, tm) * part for w, part in zip(ws, parts))

    ins = [outs_by_d[d] for d in DILATIONS] + [lses_by_d[d] for d in DILATIONS]
    outs = [_sds((S, 1024))] + [_sds((S, LANES))] * 3
    rows = _row_tile(S, 256)
    return _rowwise(body, ins, outs, name=name, rows=S, tm=256,
                    scratch=[pltpu.VMEM((2 * QKV_CHUNKS, rows, LANES), F32), pltpu.VMEM((2, rows, LANES), F32)])


def _merge_bwd(do, o, ws, *, name):
    S = do.shape[0]

    def body(tm, do_r, o_r, w1, w4, w16, d1, d4, d16, e1, e4, e16, nat, nat_l):
        prod = do_r[...] * o_r[...]
        sums = _cols_to_lanes([jnp.sum(prod[:, j * HEAD_DIM:(j + 1) * HEAD_DIM], axis=1, keepdims=True)
                               for j in range(DIL_HEADS)], tm)
        for w_r, d_o, e_o, d in zip((w1, w4, w16), (d1, d4, d16), (e1, e4, e16), DILATIONS):
            w = w_r[...]
            nat_l[0] = w * sums
            _to_branch(nat_l, 0, 1, e_o, d, tm)
            for c in range(QKV_CHUNKS):
                nat[c] = _head_pair_weights(w, c, tm) * do_r[:, c * LANES:(c + 1) * LANES]
            _to_branch(nat, 0, QKV_CHUNKS, d_o, d, tm)

    outs = [_branch_sds(S, 1024, d, MXU_DTYPE) for d in DILATIONS] + [_branch_sds(S, LANES, d, F32) for d in DILATIONS]
    rows = _row_tile(S, 256)
    got = _rowwise(body, [do, o] + [ws[d] for d in DILATIONS], outs, name=name, rows=S, tm=256,
                   scratch=[pltpu.VMEM((QKV_CHUNKS, rows, LANES), F32), pltpu.VMEM((1, rows, LANES), F32)])
    return dict(zip(DILATIONS, got[:3])), dict(zip(DILATIONS, got[3:]))


def _loss_head(x, g, target, *, name):
    S, D = x.shape

    def body(tm, x_r, g_r, t_r, dx_o, dg_o, sq_o):
        xf = x_r[...]
        xhat, _ = _rms_parts(xf)
        err = xhat * g_r[...] - t_r[...]
        dx, dgp = _rms_bwd_rows(xf, g_r[...], err * (1.0 / D))
        dx_o[...] = dx
        _acc_rows(dg_o, dgp)
        _acc_rows(sq_o, err * err)

    return _rowwise(body, [x, g.reshape(1, D), target], [_sds((S, D)), _sds((1, D)), _sds((1, D))],
                    name=name, rows=S, accs=(1, 2))


def _adamw(w, g, m, v, *, name):
    c1 = 1.0 - ADAM_B1 ** ADAM_STEP
    c2 = 1.0 - ADAM_B2 ** ADAM_STEP

    def body(tm, w_r, g_r, m_r, v_r, d_o, m_o, v_o):
        g = g_r[...]
        m_new = ADAM_B1 * m_r[...] + (1.0 - ADAM_B1) * g
        v_new = ADAM_B2 * v_r[...] + (1.0 - ADAM_B2) * (g * g)
        m_o[...] = m_new
        v_o[...] = v_new
        d_o[...] = -ADAM_LR * ((m_new / c1) / (jnp.sqrt(v_new / c2) + ADAM_EPS) + ADAM_WD * w_r[...])

    return _rowwise(body, [w, g, m, v], [_sds(w.shape)] * 3, name=name, rows=w.shape[0], tm=256)


SUM_ROW_TILE = 256


def _sum_cores(grads, theirs, half_index, *, name):
    _, R, C = grads.shape
    h = R // 2
    nb = h // SUM_ROW_TILE

    def body(c_ref, g_ref, t_ref, o_ref):
        o_ref[...] = (g_ref[...].astype(F32) + t_ref[...].astype(F32)).astype(o_ref.dtype)

    grid_spec = pltpu.PrefetchScalarGridSpec(
        num_scalar_prefetch=1, grid=(4, nb),
        in_specs=[pl.BlockSpec((1, SUM_ROW_TILE, C), lambda k, i, c_ref: (k, c_ref[0] * nb + i, 0)),
                  pl.BlockSpec((1, SUM_ROW_TILE, C), lambda k, i, c_ref: (k, i, 0))],
        out_specs=pl.BlockSpec((1, SUM_ROW_TILE, C), lambda k, i, c_ref: (k, i, 0)))
    return _pcall(body, name=name, dims=("parallel", "parallel"), grid_spec=grid_spec,
                  out_shape=_sds((4, h, C), grads.dtype))(half_index, grads, theirs)


def _sum_chips(parts, half_index, *, name):
    _, h, C = parts.shape
    nb = h // SUM_ROW_TILE

    def body(c_ref, p_ref, o_ref):
        p = [p_ref[k].astype(F32) for k in range(4)]
        o_ref[...] = ((p[0] + p[1]) + p[2]) + p[3]

    grid_spec = pltpu.PrefetchScalarGridSpec(
        num_scalar_prefetch=1, grid=(nb,),
        in_specs=[pl.BlockSpec((4, SUM_ROW_TILE, C), lambda i, c_ref: (0, i, 0))],
        out_specs=pl.BlockSpec((SUM_ROW_TILE, C), lambda i, c_ref: (c_ref[0] * nb + i, 0)))
    return _pcall(body, name=name, dims=("parallel",), grid_spec=grid_spec,
                  out_shape=_sds((2 * h, C)))(half_index, parts)


def _position():
    return lax.axis_index("x"), lax.axis_index("y"), lax.axis_index("c")


def _chip_peers(x, y):
    return [(1 - x, y), (x, 1 - y), (1 - x, 1 - y)]


_HBM = pl.BlockSpec(memory_space=pltpu.HBM)
LOCAL_COPY_CHUNKS = 8


def _local_copies(src_ref, dst_ref, sems):
    rows = src_ref.shape[0] // LOCAL_COPY_CHUNKS
    assert rows * LOCAL_COPY_CHUNKS == src_ref.shape[0]
    return [pltpu.make_async_copy(src_ref.at[pl.ds(i * rows, rows)], dst_ref.at[pl.ds(i * rows, rows)], sems.at[i])
            for i in range(LOCAL_COPY_CHUNKS)]


class _Exchange:
    def __init__(self, src, out_shape, sems, stages):
        self.src, self.out_shape, self.sems, self.stages = src, out_shape, sems, stages

    def run(self, refs, step, n_steps, at_end):
        for fraction, fn in self.stages:
            if (fraction == 1.0) == at_end:
                pl.when(step == int(round(fraction * (n_steps - 1))))(functools.partial(fn, *refs))


def _run_exchange(ex, *, name):
    def body(*refs):
        for _, fn in ex.stages:
            fn(*refs)

    return pl.pallas_call(
        body, name=name, in_specs=[_HBM], out_specs=_HBM, out_shape=ex.out_shape, scratch_shapes=list(ex.sems),
    )(ex.src)


def _gather_exchange(src):
    R, C = src.shape
    h = R // 2

    def plan(src_ref, out_ref, send_sems, recv_sems, local_sems):
        x, y, c = _position()
        me = 2 * x + y
        peers = _chip_peers(x, y)
        mine, other = pl.ds(c * h, h), pl.ds((1 - c) * h, h)

        def copy(sem, src_part, dst_part, device):
            return pltpu.make_async_remote_copy(
                src_ref=src_part, dst_ref=dst_part, send_sem=send_sems.at[sem], recv_sem=recv_sems.at[sem],
                device_id=device, device_id_type=MESH)

        landed = [out_ref.at[2 * px + py, mine] for px, py in peers]
        theirs = [out_ref.at[2 * px + py, other] for px, py in peers]
        return dict(
            sends=lambda: [copy(j, src_ref.at[mine], out_ref.at[me, mine], (px, py, c))
                           for j, (px, py) in enumerate(peers)],
            local=lambda: _local_copies(src_ref, out_ref.at[me], local_sems),
            arrivals=lambda: [copy(j, landed[j], landed[j], (px, py, c)) for j, (px, py) in enumerate(peers)],
            passed=lambda: [copy(3 + j, landed[j], landed[j], (x, y, 1 - c)) for j in range(3)],
            from_sibling=lambda: [copy(3 + j, theirs[j], theirs[j], (x, y, 1 - c)) for j in range(3)])

    def start(*refs):
        p = plan(*refs)
        for cp in p["sends"]() + p["local"]():
            cp.start()

    def pass_on(*refs):
        p = plan(*refs)
        for arrival, forward in zip(p["arrivals"](), p["passed"]()):
            arrival.wait_recv()
            forward.start()

    def finish(*refs):
        p = plan(*refs)
        for cp in p["from_sibling"]():
            cp.wait_recv()
        for cp in p["sends"]() + p["passed"]():
            cp.wait_send()
        for cp in p["local"]():
            cp.wait()

    sems = [pltpu.SemaphoreType.DMA((6,)), pltpu.SemaphoreType.DMA((6,)), pltpu.SemaphoreType.DMA((LOCAL_COPY_CHUNKS,))]
    return _Exchange(src, jax.ShapeDtypeStruct((4, R, C), src.dtype), sems, [(0.0, start), (0.6, pass_on), (1.0, finish)])


def _swap_other_half(src, *, name):
    _, R, C = src.shape
    h = R // 2

    def body(src_ref, out_ref, send_sem, recv_sem):
        x, y, c = _position()
        cp = pltpu.make_async_remote_copy(
            src_ref=src_ref.at[:, pl.ds((1 - c) * h, h)], dst_ref=out_ref, send_sem=send_sem, recv_sem=recv_sem,
            device_id=(x, y, 1 - c), device_id_type=MESH)
        cp.start()
        cp.wait()

    return pl.pallas_call(
        body, name=name, in_specs=[_HBM], out_specs=_HBM, out_shape=jax.ShapeDtypeStruct((4, h, C), src.dtype),
        scratch_shapes=[pltpu.SemaphoreType.DMA, pltpu.SemaphoreType.DMA],
    )(src)


def _scatter_exchange(src):
    def plan(src_ref, out_ref, send_sems, recv_sems, local_sems):
        x, y, c = _position()
        me = 2 * x + y
        peers = _chip_peers(x, y)

        def copy(j, src_block, dst_slot):
            px, py = peers[j]
            return pltpu.make_async_remote_copy(
                src_ref=src_ref.at[src_block], dst_ref=out_ref.at[dst_slot], send_sem=send_sems.at[j],
                recv_sem=recv_sems.at[j], device_id=(px, py, c), device_id_type=MESH)

        return dict(sends=lambda: [copy(j, 2 * px + py, me) for j, (px, py) in enumerate(peers)],
                    arrivals=lambda: [copy(j, me, 2 * px + py) for j, (px, py) in enumerate(peers)],
                    local=lambda: _local_copies(src_ref.at[me], out_ref.at[me], local_sems))

    def start(*refs):
        p = plan(*refs)
        for cp in p["sends"]() + p["local"]():
            cp.start()

    def finish(*refs):
        p = plan(*refs)
        for cp in p["arrivals"]():
            cp.wait_recv()
        for cp in p["sends"]():
            cp.wait_send()
        for cp in p["local"]():
            cp.wait()

    sems = [pltpu.SemaphoreType.DMA((3,)), pltpu.SemaphoreType.DMA((3,)), pltpu.SemaphoreType.DMA((LOCAL_COPY_CHUNKS,))]
    return _Exchange(src, jax.ShapeDtypeStruct(src.shape, src.dtype), sems, [(0.0, start), (1.0, finish)])


def _join_halves(src, *, name):
    R, C = src.shape
    h = R // 2

    def body(src_ref, out_ref, send_sem, recv_sem):
        x, y, c = _position()
        mine, theirs = pl.ds(c * h, h), pl.ds((1 - c) * h, h)
        cp = pltpu.make_async_remote_copy(
            src_ref=src_ref.at[mine], dst_ref=out_ref.at[mine], send_sem=send_sem, recv_sem=recv_sem,
            device_id=(x, y, 1 - c), device_id_type=MESH)
        cp.start()
        pltpu.make_async_remote_copy(
            src_ref=src_ref.at[theirs], dst_ref=out_ref.at[theirs], send_sem=send_sem, recv_sem=recv_sem,
            device_id=(x, y, 1 - c), device_id_type=MESH).wait_recv()
        cp.wait_send()

    return pl.pallas_call(
        body, name=name, in_specs=[_HBM], out_specs=_HBM, out_shape=jax.ShapeDtypeStruct((R, C), src.dtype),
        input_output_aliases={0: 0},
        scratch_shapes=[pltpu.SemaphoreType.DMA, pltpu.SemaphoreType.DMA],
    )(src)


def _allreduce_small(vec, *, name):
    R, C = vec.shape

    def body(v_ref, o_ref, slots, send_sems, recv_sems):
        x, y, c = _position()
        me = 4 * x + 2 * y + c

        def peer(k):
            return x ^ ((k >> 2) & 1), y ^ ((k >> 1) & 1), c ^ (k & 1)

        def copy(k, slot):
            return pltpu.make_async_remote_copy(
                src_ref=v_ref, dst_ref=slots.at[slot], send_sem=send_sems.at[k - 1], recv_sem=recv_sems.at[k - 1],
                device_id=peer(k), device_id_type=MESH)

        slots[me] = v_ref[...]
        sends = [copy(k, me) for k in range(1, 8)]
        for cp in sends:
            cp.start()
        for k in range(1, 8):
            px, py, pc = peer(k)
            copy(k, 4 * px + 2 * py + pc).wait_recv()
        total = slots[0]
        for d in range(1, 8):
            total = total + slots[d]
        o_ref[...] = total
        for cp in sends:
            cp.wait_send()

    vmem = pl.BlockSpec(memory_space=pltpu.VMEM)
    return pl.pallas_call(
        body, name=name, in_specs=[vmem], out_specs=vmem, out_shape=jax.ShapeDtypeStruct((R, C), vec.dtype),
        scratch_shapes=[pltpu.VMEM((8, R, C), vec.dtype), pltpu.SemaphoreType.DMA((7,)), pltpu.SemaphoreType.DMA((7,))],
    )(vec)


def _swa_cfg(S):
    return _Attn(T=S, Tk=S, G=1, nh=SWA_HEADS, rep=SWA_HEADS // SWA_KV_HEADS, dqk=HEAD_DIM, dv=HEAD_DIM, tq=BLOCK,
                 tk=BLOCK, mode="band", max_dist=SWA_WINDOW - 1, scale=HEAD_DIM ** -0.5, qcol=lambda g: 0,
                 kcol=lambda g: 0, vcol=lambda g: 0, ocol=lambda g: 0, o_width=SWA_HEADS * HEAD_DIM)


MLA_FWD_GROUP = 8
MLA_BWD_GROUP = 4


def _mla_cfg(S, group):
    t = _tile(S, 512)
    return _Attn(T=S, Tk=S, G=MLA_HEADS // group, nh=group, rep=1, dqk=LANES, dv=MLA_V, tq=t, tk=t, mode="causal",
                 scale=(MLA_NOPE + MLA_ROPE) ** -0.5, qcol=lambda g: g, kcol=lambda g: g, vcol=lambda g: g,
                 ocol=lambda g: g, o_width=MLA_HEADS * MLA_V)


def _dil_cfg(S, window, dil):
    return _Attn(T=S // dil, Tk=S // dil, G=dil, nh=DIL_HEADS, rep=1, dqk=HEAD_DIM, dv=HEAD_DIM, tq=BLOCK, tk=BLOCK,
                 mode="band", max_dist=window // dil, scale=HEAD_DIM ** -0.5, qcol=lambda g: g, kcol=lambda g: g,
                 vcol=lambda g: g, ocol=lambda g: g, o_width=dil * DIL_HEADS * HEAD_DIM)


X_ROW_TILE = 512


def _memory_attn(q, kv, *, name):
    S, width = q.shape
    M = kv.shape[0]
    tq = _tile(S, X_ROW_TILE)
    scale = X_HEAD_DIM ** -0.5
    head = lambda j: slice(j * X_HEAD_DIM, (j + 1) * X_HEAD_DIM)

    def body(q_ref, kv_ref, o_ref, lse_ref):
        score = lambda j: _dot(kv_ref[:, head(j)], q_ref[:, head(j)], NT)
        ahead, rows = score(0), []
        for j in range(X_HEADS):
            s = ahead * scale
            if j + 1 < X_HEADS:
                ahead = score(j + 1)
            m = jnp.max(s, axis=0, keepdims=True)
            pr = jnp.exp(s - m)
            l = jnp.sum(pr, axis=0, keepdims=True)
            o_t = _dot(kv_ref[:, head(X_HEADS + j)], pr, TN)
            o_ref[:, head(j)] = (o_t / l).T.astype(o_ref.dtype)
            rows.append(m + jnp.log(l))
        lse_ref[...] = _rows_to_stats(rows, tq)

    return _pcall(
        body, name=name, dims=("parallel",), grid=(S // tq,),
        in_specs=[pl.BlockSpec((tq, width), lambda i: (i, 0)), pl.BlockSpec((M, 2 * width), lambda i: (0, 0))],
        out_specs=[pl.BlockSpec((tq, width), lambda i: (i, 0)), pl.BlockSpec((tq, LANES), lambda i: (i, 0))],
        out_shape=[_sds((S, width), MXU_DTYPE), _sds((S, LANES))],
    )(q, kv)


def _memory_attn_bwd(q, kv, o, do, lse, *, name):
    S, width = q.shape
    M = kv.shape[0]
    tq = _tile(S, X_ROW_TILE)
    n = S // tq
    scale = X_HEAD_DIM ** -0.5
    head = lambda j: slice(j * X_HEAD_DIM, (j + 1) * X_HEAD_DIM)

    def body(q_ref, kv_ref, o_ref, do_ref, lse_ref, dq_ref, dkv_ref, acc):
        i = pl.program_id(0)

        @pl.when(i == 0)
        def _():
            acc[...] = jnp.zeros_like(acc)

        lse_t = lse_ref[...].T

        def first(j):
            return (_dot(kv_ref[:, head(j)], q_ref[:, head(j)], NT),
                    _dot(kv_ref[:, head(X_HEADS + j)], do_ref[:, head(j)], NT))

        ahead = first(0)
        for j in range(X_HEADS):
            s, dp = ahead
            if j + 1 < X_HEADS:
                ahead = first(j + 1)
            row_term = jnp.sum((do_ref[:, head(j)].astype(F32) * o_ref[:, head(j)].astype(F32)).T, axis=0, keepdims=True)
            pr = jnp.exp(s * scale - lse_t[j:j + 1, :])
            ds = pr * (dp - row_term) * scale
            dq_ref[:, head(j)] = _dot(kv_ref[:, head(j)], ds, TN).T.astype(dq_ref.dtype)
            acc[:, head(j)] += _dot(ds, q_ref[:, head(j)], NN)
            acc[:, head(X_HEADS + j)] += _dot(pr, do_ref[:, head(j)], NN)

        @pl.when(i == n - 1)
        def _():
            dkv_ref[...] = acc[...].astype(dkv_ref.dtype)

    row = pl.BlockSpec((tq, width), lambda i: (i, 0))
    whole = pl.BlockSpec((M, 2 * width), lambda i: (0, 0))
    return _pcall(
        body, name=name, dims=("arbitrary",), grid=(n,),
        in_specs=[row, whole, row, row, pl.BlockSpec((tq, LANES), lambda i: (i, 0))], out_specs=[row, whole],
        out_shape=[_sds((S, width), MXU_DTYPE), _sds((M, 2 * width), MXU_DTYPE)],
        scratch_shapes=[pltpu.VMEM((M, 2 * width), F32)],
    )(q, kv, o, do, lse)


def _cross_fwd(p, x, mem, W, vec):
    hx = _rmsnorm(x, vec[p + "x_norm"], name=p + "x_norm")
    qx = _mm(hx, W[p + "w_xq"], mode="nn", name=p + "xq", out_dtype=MXU_DTYPE)
    memn = _rmsnorm(mem, vec[p + "mem_norm"], name=p + "mem_norm")
    kvx = _mm(memn, W[p + "w_xkv"], mode="nn", name=p + "xkv", out_dtype=MXU_DTYPE)
    ox, lse = _memory_attn(qx, kvx, name=p + "x_attn")
    out = _mm(ox, W[p + "w_xo"], mode="nn", name=p + "xo", res=x)
    return out, (x, hx, qx, memn, kvx, ox, lse)


def _cross_bwd(p, dx, saved, mem, W, vec, dW, dvec):
    x, hx, qx, memn, kvx, ox, lse = saved
    dox = _mm(dx, W[p + "w_xo"], mode="nt", name=p + "xo_dx", out_dtype=MXU_DTYPE)
    dW[p + "w_xo"] = _dw(ox, dx, name=p + "xo_dw")
    dqx, dkvx = _memory_attn_bwd(qx, kvx, ox, dox, lse, name=p + "x_attn_bwd")
    dW[p + "w_xq"] = _dw(hx, dqx, name=p + "xq_dw")
    dW[p + "w_xkv"] = _dw(memn, dkvx, name=p + "xkv_dw")
    dmemn = _mm(dkvx, W[p + "w_xkv"], mode="nt", name=p + "xkv_dx")
    _, dvec[p + "mem_norm"] = _rmsnorm_bwd(mem, vec[p + "mem_norm"], dmemn, name=p + "mem_norm_bwd")
    dx_in, dvec[p + "x_norm"] = _dx_norm_bwd(dqx, W[p + "w_xq"], x, vec[p + "x_norm"], dx, name=p + "xq_dx")
    return dx_in


def _ffn_fwd(p, x, W, vec):
    hf = _rmsnorm(x, vec[p + "ffn_norm"], name=p + "ffn_norm")
    gate, up, act = _gate_up(hf, W[p + "w_gate"], W[p + "w_up"], name=p + "gate_up")
    out = _mm(act, W[p + "w_down"], mode="nn", name=p + "down", res=x)
    return out, (x, hf, gate, up, act)


def _ffn_bwd(p, dx, saved, W, vec, dW, dvec):
    x, hf, gate, up, act = saved
    dW[p + "w_down"] = _dw(act, dx, name=p + "down_dw")
    dgate, dup = _gate_up_bwd(dx, W[p + "w_down"], gate, up, name=p + "gate_up_bwd")
    dhf = _mm(dgate, W[p + "w_gate"], mode="nt", name=p + "gate_dx")
    dW[p + "w_gate"] = _dw(hf, dgate, name=p + "gate_dw")
    dW[p + "w_up"] = _dw(hf, dup, name=p + "up_dw")
    dx_in, dvec[p + "ffn_norm"] = _dx_norm_bwd(dup, W[p + "w_up"], x, vec[p + "ffn_norm"], dx, name=p + "up_dx",
                                               res=dhf)
    return dx_in


def _even_fwd(p, x, tabs, W, vec, comm=None):
    S = x.shape[0]
    h = _rmsnorm(x, vec[p + "mix_norm"], name=p + "mix_norm")
    z = _mm(h, W[p + "w_in"], mode="nn", name=p + "in")
    qa, ka, va, cqn, ckvn, kr = _l0_prep(z, tabs, vec[p + "q_norm"], vec[p + "kv_norm"], name=p + "prep")
    sink = jnp.pad(vec[p + "sinks"], (0, LANES - SWA_HEADS)).reshape(1, LANES)
    oa, lse_a = _band_fwd(_swa_cfg(S), qa, ka, va, name=p + "swa", sink=sink, out_dtype=MXU_DTYPE)
    qb = _mm(cqn, W[p + "w_uq"], mode="nn", name=p + "uq")
    kvb = _mm(ckvn, W[p + "w_ukv"], mode="nn", name=p + "ukv")
    Q, K, V = _mla_prep(qb, kvb, kr, tabs, name=p + "mla_prep")
    if comm is None:
        ob, lse_b = _causal_fwd(_mla_cfg(S, MLA_FWD_GROUP), Q, K, V, name=p + "mla", out_dtype=MXU_DTYPE, stat_heads=MLA_BWD_GROUP)
    else:
        ob, lse_b, gathered = _causal_fwd(_mla_cfg(S, MLA_FWD_GROUP), Q, K, V, name=p + "mla", out_dtype=MXU_DTYPE, stat_heads=MLA_BWD_GROUP,
                                          carry=comm.late_weights_exchange())
        W = {**W, **comm.late_weights(gathered)}
    o = jnp.concatenate([oa, ob], axis=1)
    out = _mm(o, W[p + "w_out"], mode="nn", name=p + "out", res=x)
    return out, (x, h, z, qa, ka, va, cqn, ckvn, sink, oa, lse_a, Q, K, V, ob, lse_b, o), W


def _even_bwd(p, dx, saved, tabs, W, vec, dW, dvec, comm=None):
    x, h, z, qa, ka, va, cqn, ckvn, sink, oa, lse_a, Q, K, V, ob, lse_b, o = saved
    S = x.shape[0]
    do = _mm(dx, W[p + "w_out"], mode="nt", name=p + "out_dx", out_dtype=MXU_DTYPE)
    dW[p + "w_out"] = _dw(o, dx, name=p + "out_dw")
    doa, dob = do[:, :SWA_HEADS * HEAD_DIM], do[:, SWA_HEADS * HEAD_DIM:]
    cfg = _swa_cfg(S)
    delta, dsink = _attn_delta(cfg, oa, doa, name=p + "swa_delta", lse=lse_a, sink=sink)
    dvec[p + "sinks"] = dsink
    dqa, dka, dva = _band_bwd(cfg, qa, ka, va, doa, lse_a, delta, name=p + "swa_bwd")
    cfg = _mla_cfg(S, MLA_BWD_GROUP)
    delta, _ = _attn_delta(cfg, ob, dob, name=p + "mla_delta")
    if comm is None:
        dQ, dK, dV = _causal_bwd(cfg, Q, K, V, dob, lse_b, delta, name=p + "mla_bwd")
    else:
        dQ, dK, dV, landed = _causal_bwd(cfg, Q, K, V, dob, lse_b, delta, name=p + "mla_bwd",
                                         carry=comm.late_grads_exchange(dW))
        comm.late_grads_landed(landed)
    dqb, dkvb, dkr = _mla_prep_bwd(dQ, dK, dV, tabs, name=p + "mla_prep_bwd")
    dcqn = _mm(dqb, W[p + "w_uq"], mode="nt", name=p + "uq_dx")
    dW[p + "w_uq"] = _dw(cqn, dqb, name=p + "uq_dw")
    dckvn = _mm(dkvb, W[p + "w_ukv"], mode="nt", name=p + "ukv_dx")
    dW[p + "w_ukv"] = _dw(ckvn, dkvb, name=p + "ukv_dw")
    dz, dvec[p + "q_norm"], dvec[p + "kv_norm"] = _l0_prep_bwd(
        z, tabs, vec[p + "q_norm"], vec[p + "kv_norm"], dqa, dka, dva, dcqn, dckvn, dkr, name=p + "prep_bwd")
    dW[p + "w_in"] = _dw(h, dz, name=p + "in_dw")
    dx_in, dvec[p + "mix_norm"] = _dx_norm_bwd(dz, W[p + "w_in"], x, vec[p + "mix_norm"], dx, name=p + "in_dx")
    return dx_in


def _odd_fwd(p, x, tabs, W, vec):
    S = x.shape[0]
    assert S % (DIL_PATTERNS[-1][1] * BLOCK) == 0, "keys past the end of the sequence are never attended"
    h = _rmsnorm(x, vec[p + "mix_norm"], name=p + "mix_norm")
    qkv = _mm(h, W[p + "w_qkv"], mode="nn", name=p + "qkv")
    qkv_by_d = _l1_prep(qkv, tabs, name=p + "prep")
    outs, lses = {}, {}
    for window, dil in DIL_PATTERNS:
        outs[dil], lses[dil] = _band_fwd(_dil_cfg(S, window, dil), *qkv_by_d[dil], name=p + "dil%d" % dil)
    o, w1, w4, w16 = _merge(outs, lses, name=p + "merge")
    out = _mm(o, W[p + "w_out"], mode="nn", name=p + "out", res=x)
    return out, (x, h, qkv_by_d, lses, dict(zip(DILATIONS, (w1, w4, w16))), o)


def _odd_bwd(p, dx, saved, tabs, W, vec, dW, dvec):
    x, h, qkv_by_d, lses, ws, o = saved
    S = x.shape[0]
    do = _mm(dx, W[p + "w_out"], mode="nt", name=p + "out_dx")
    dW[p + "w_out"] = _dw(o, dx, name=p + "out_dw")
    dos, deltas = _merge_bwd(do, o, ws, name=p + "merge_bwd")
    grads = {}
    for window, dil in DIL_PATTERNS:
        grads[dil] = _band_bwd(_dil_cfg(S, window, dil), *qkv_by_d[dil], dos[dil], lses[dil], deltas[dil],
                               name=p + "dil%d_bwd" % dil)
    dqkv = _l1_prep_bwd(grads, tabs, name=p + "prep_bwd")
    dW[p + "w_qkv"] = _dw(h, dqkv, name=p + "qkv_dw")
    dx_in, dvec[p + "mix_norm"] = _dx_norm_bwd(dqkv, W[p + "w_qkv"], x, vec[p + "mix_norm"], dx, name=p + "qkv_dx")
    return dx_in


def _local_step(x, mem, positions, target, W, vec, comm=None):
    tabs = _rope_tables(positions)
    x1, s_mix0, W = _even_fwd("l0_", x, tabs, W, vec, comm)
    x2, s_x0 = _cross_fwd("l0_", x1, mem, W, vec)
    x3, s_f0 = _ffn_fwd("l0_", x2, W, vec)
    x4, s_mix1 = _odd_fwd("l1_", x3, tabs, W, vec)
    x5, s_x1 = _cross_fwd("l1_", x4, mem, W, vec)
    x6, s_f1 = _ffn_fwd("l1_", x5, W, vec)
    dW, dvec = {}, {}
    dx, dvec["final_norm"], sq = _loss_head(x6, vec["final_norm"], target, name="loss_head")
    dx = _ffn_bwd("l1_", dx, s_f1, W, vec, dW, dvec)
    dx = _cross_bwd("l1_", dx, s_x1, mem, W, vec, dW, dvec)
    dx = _odd_bwd("l1_", dx, s_mix1, tabs, W, vec, dW, dvec)
    dx = _ffn_bwd("l0_", dx, s_f0, W, vec, dW, dvec)
    dx = _cross_bwd("l0_", dx, s_x0, mem, W, vec, dW, dvec)
    dx = _even_bwd("l0_", dx, s_mix0, tabs, W, vec, dW, dvec, comm)
    return sq, dx, dW, dvec


_LAYER_MATS = {
    0: [("w_in", "col"), ("w_uq", "col"), ("w_ukv", "col"), ("w_out", "row"), ("w_xq", "row"), ("w_xkv", "row"),
        ("w_xo", "col"), ("w_gate", "col"), ("w_up", "col"), ("w_down", "row")],
    1: [("w_qkv", "col"), ("w_out", "row"), ("w_xq", "row"), ("w_xkv", "row"), ("w_xo", "col"), ("w_gate", "col"),
        ("w_up", "col"), ("w_down", "row")],
}
MATS = [("l%d_%s" % (l, n), kind) for l in (0, 1) for n, kind in _LAYER_MATS[l]]
_LAYER_VECS = {0: ["mix_norm", "sinks", "q_norm", "kv_norm", "x_norm", "mem_norm", "ffn_norm"],
               1: ["mix_norm", "x_norm", "mem_norm", "ffn_norm"]}
VECS = ["l%d_%s" % (l, n) for l in (0, 1) for n in _LAYER_VECS[l]] + ["final_norm"]
WEIGHT_ORDER = (["l0_mix_norm", "l0_w_in", "l0_sinks", "l0_q_norm", "l0_w_uq", "l0_kv_norm", "l0_w_ukv", "l0_w_out",
                 "l0_x_norm", "l0_mem_norm", "l0_w_xq", "l0_w_xkv", "l0_w_xo", "l0_ffn_norm", "l0_w_gate", "l0_w_up",
                 "l0_w_down", "l1_mix_norm", "l1_w_qkv", "l1_w_out", "l1_x_norm", "l1_mem_norm", "l1_w_xq",
                 "l1_w_xkv", "l1_w_xo", "l1_ffn_norm", "l1_w_gate", "l1_w_up", "l1_w_down", "final_norm"])
PACK_COLS = 1024
PACK_ROW_TILE = 2 * SUM_ROW_TILE
VEC_ROWS = 16
LOSS_ROW = len(VECS)
N_CHIPS = 4


class _Group:
    def __init__(self, mats, shards):
        self.mats, self.shards = mats, shards
        self.layout, off = {}, 0
        for name, _ in mats:
            n = shards[name].size // PACK_COLS
            assert n * PACK_COLS == shards[name].size
            self.layout[name] = (off, n)
            off += n
        self.used = off
        self.rows = -(-off // PACK_ROW_TILE) * PACK_ROW_TILE

    def pack(self, tensors, dtype):
        parts = [tensors[name].astype(dtype).reshape(-1, PACK_COLS) for name, _ in self.mats]
        return jnp.concatenate(parts + [jnp.zeros((self.rows - self.used, PACK_COLS), dtype)], axis=0)

    def unpack(self, packed):
        return {name: packed[off:off + n].reshape(self.shards[name].shape) for name, (off, n) in self.layout.items()}

    def full_weights(self, gathered):
        W = {}
        for name, kind in self.mats:
            off, n = self.layout[name]
            r, cw = self.shards[name].shape
            blocks = gathered[:, off:off + n].reshape(N_CHIPS, r, cw)
            W[name] = blocks.reshape(N_CHIPS * r, cw) if kind == "row" else (
                jnp.transpose(blocks, (1, 0, 2)).reshape(r, N_CHIPS * cw))
        if "l0_w_in" in W:
            W["l0_w_in"] = jnp.pad(W["l0_w_in"], ((0, 0), (0, Z_END - W["l0_w_in"].shape[1])))
        if "l0_w_uq" in W:
            uq = W["l0_w_uq"].reshape(MLA_Q_RANK, MLA_HEADS, MLA_NOPE + MLA_ROPE)
            uq = jnp.pad(uq, ((0, 0), (0, 0), (0, LANES - MLA_NOPE - MLA_ROPE)))
            W["l0_w_uq"] = uq.reshape(MLA_Q_RANK, MLA_HEADS * LANES)
        return W

    def pack_grads(self, dW):
        parts = []
        for name, kind in self.mats:
            r, cw = self.shards[name].shape
            g = dW[name]
            if name == "l0_w_in":
                g = g[:, :Z_KR + MLA_ROPE]
            if name == "l0_w_uq":
                g = g.reshape(MLA_Q_RANK, MLA_HEADS, LANES)[:, :, :MLA_NOPE + MLA_ROPE].reshape(MLA_Q_RANK, -1)
            if kind == "col":
                g = jnp.transpose(g.reshape(r, N_CHIPS, cw), (1, 0, 2))
            parts.append(g.reshape(N_CHIPS, -1, PACK_COLS).astype(EXCHANGE_DTYPE))
        pad = jnp.zeros((N_CHIPS, self.rows - self.used, PACK_COLS), EXCHANGE_DTYPE)
        return jnp.concatenate(parts + [pad], axis=1)


def _pack_vecs(vecs):
    rows = [jnp.pad(vecs[n].reshape(-1).astype(F32), (0, PACK_COLS - vecs[n].size)) for n in VECS]
    rows += [jnp.zeros((PACK_COLS,), F32)] * (VEC_ROWS - len(rows))
    return jnp.stack(rows)


def _unpack_vecs(packed, like):
    return {n: packed[i, :like[n].size].reshape(like[n].shape) for i, n in enumerate(VECS)}


EARLY_MATS = [m for m in MATS if m[0] in ("l0_w_in", "l0_w_uq", "l0_w_ukv")]
LATE_MATS = [m for m in MATS if m not in EARLY_MATS]


class _StepComm:
    def __init__(self, shards):
        self.early, self.late = _Group(EARLY_MATS, shards), _Group(LATE_MATS, shards)
        self.half_index = lax.axis_index("c").astype(jnp.int32).reshape(1)
        self.late_grads = None

    def early_weights(self):
        src = self.early.pack(self.early.shards, MXU_DTYPE)
        return self.early.full_weights(_run_exchange(_gather_exchange(src), name="gather_early"))

    def late_weights_exchange(self):
        return _gather_exchange(self.late.pack(self.late.shards, MXU_DTYPE))

    def late_weights(self, gathered):
        return self.late.full_weights(gathered)

    def _chip_sum(self, group, dW, tag):
        grads = group.pack_grads(dW)
        theirs = _swap_other_half(grads, name="swap_other_half_" + tag)
        return _sum_cores(grads, theirs, self.half_index, name="sum_cores_" + tag)

    def _finish(self, parts, tag):
        return _join_halves(_sum_chips(parts, self.half_index, name="sum_chips_" + tag), name="join_halves_" + tag)

    def late_grads_exchange(self, dW):
        return _scatter_exchange(self._chip_sum(self.late, dW, "late"))

    def late_grads_landed(self, parts):
        self.late_grads = self._finish(parts, "late")

    def early_grads(self, dW):
        parts = _run_exchange(_scatter_exchange(self._chip_sum(self.early, dW, "early")), name="scatter_early")
        return self._finish(parts, "early")


def _step(a):
    weights = {n: a[n] for n in WEIGHT_ORDER}
    shards = {n: weights[n] for n, _ in MATS}
    vec = {n: weights[n] for n in VECS}
    comm = _StepComm(shards)
    sq, grad_x, dW, dvec = _local_step(a["x"][0], a["mem"][0], a["positions"], a["loss_target"][0],
                                       comm.early_weights(), vec, comm)

    dvec = dict(dvec)
    dvec["l0_sinks"] = dvec["l0_sinks"][0, :SWA_HEADS]
    small = _pack_vecs(dvec)
    small = small.at[LOSS_ROW, 0].set(0.5 / a["x"].shape[-1] * jnp.sum(sq))
    small = _allreduce_small(small, name="reduce_gains")
    loss = small[LOSS_ROW, 0]
    g_s = small.at[LOSS_ROW, 0].set(0.0)
    d_s, m_s, v_s = _adamw(_pack_vecs(vec), g_s, _pack_vecs({n: a["m_" + n] for n in VECS}),
                           _pack_vecs({n: a["v_" + n] for n in VECS}), name="adamw_gains")
    got = [_unpack_vecs(packed, vec) for packed in (g_s, d_s, m_s, v_s)]

    for group, g_w in ((comm.late, comm.late_grads), (comm.early, comm.early_grads(dW))):
        for n, g in group.unpack(g_w).items():
            results = (g,) + tuple(_adamw(shards[n], g, a["m_" + n], a["v_" + n], name="adamw_" + n))
            for kind, value in zip(got, results):
                kind[n] = value

    out = [loss, grad_x[None]]
    for kind in got:
        out += [kind[n] for n in WEIGHT_ORDER]
    return tuple(out)


def kernel(x, mem, positions, l0_mix_norm, l0_w_in, l0_sinks, l0_q_norm, l0_w_uq, l0_kv_norm, l0_w_ukv, l0_w_out, l0_x_norm, l0_mem_norm, l0_w_xq, l0_w_xkv, l0_w_xo, l0_ffn_norm, l0_w_gate, l0_w_up, l0_w_down, l1_mix_norm, l1_w_qkv, l1_w_out, l1_x_norm, l1_mem_norm, l1_w_xq, l1_w_xkv, l1_w_xo, l1_ffn_norm, l1_w_gate, l1_w_up, l1_w_down, final_norm, loss_target, m_l0_mix_norm, m_l0_w_in, m_l0_sinks, m_l0_q_norm, m_l0_w_uq, m_l0_kv_norm, m_l0_w_ukv, m_l0_w_out, m_l0_x_norm, m_l0_mem_norm, m_l0_w_xq, m_l0_w_xkv, m_l0_w_xo, m_l0_ffn_norm, m_l0_w_gate, m_l0_w_up, m_l0_w_down, m_l1_mix_norm, m_l1_w_qkv, m_l1_w_out, m_l1_x_norm, m_l1_mem_norm, m_l1_w_xq, m_l1_w_xkv, m_l1_w_xo, m_l1_ffn_norm, m_l1_w_gate, m_l1_w_up, m_l1_w_down, m_final_norm, v_l0_mix_norm, v_l0_w_in, v_l0_sinks, v_l0_q_norm, v_l0_w_uq, v_l0_kv_norm, v_l0_w_ukv, v_l0_w_out, v_l0_x_norm, v_l0_mem_norm, v_l0_w_xq, v_l0_w_xkv, v_l0_w_xo, v_l0_ffn_norm, v_l0_w_gate, v_l0_w_up, v_l0_w_down, v_l1_mix_norm, v_l1_w_qkv, v_l1_w_out, v_l1_x_norm, v_l1_mem_norm, v_l1_w_xq, v_l1_w_xkv, v_l1_w_xo, v_l1_ffn_norm, v_l1_w_gate, v_l1_w_up, v_l1_w_down, v_final_norm):
    return _step(dict(locals()))
```

```python
import functools

import jax
import jax.numpy as jnp
import numpy as np
from jax import lax
from jax.experimental import pallas as pl
from jax.experimental.pallas import tpu as pltpu

F32 = jnp.float32
MXU_DTYPE = jnp.bfloat16
LANES = 128
VMEM_LIMIT_BYTES = 56 * 1024 * 1024

NORM_EPS = 1e-6
ROPE_THETA = 10000.0
BLOCK = 128
HEAD_DIM = 64
SWA_HEADS, SWA_KV_HEADS, SWA_WINDOW = 8, 2, 128
MLA_HEADS, MLA_Q_RANK, MLA_KV_RANK, MLA_NOPE, MLA_ROPE, MLA_V = 8, 384, 256, 64, 32, 64
DIL_HEADS = 16
DIL_PATTERNS = ((128, 1), (512, 4), (2048, 16))
X_HEADS, X_HEAD_DIM = 4, 128
ADAM_LR, ADAM_B1, ADAM_B2, ADAM_EPS, ADAM_WD, ADAM_STEP = 0.001, 0.9, 0.999, 1e-08, 0.01, 10
MESH = pl.DeviceIdType.MESH
NEG_BIG = -1e30

NN = (((1,), (0,)), ((), ()))
NT = (((1,), (1,)), ((), ()))


def _dot(a, b, dims=NN):
    return lax.dot_general(a.astype(MXU_DTYPE), b.astype(MXU_DTYPE), dims, preferred_element_type=F32)


def _pcall(body, *, name, dims=None, **kw):
    params = pltpu.CompilerParams(dimension_semantics=dims, vmem_limit_bytes=VMEM_LIMIT_BYTES)
    return pl.pallas_call(body, name=name, compiler_params=params, **kw)


def _tile(n, pref):
    t = (min(pref, n) // LANES) * LANES
    while t >= LANES:
        if n % t == 0:
            return t
        t -= LANES
    return n


SUBLANES_PACKED = 16


def _row_tile(n, pref):
    t = (min(pref, n) // SUBLANES_PACKED) * SUBLANES_PACKED
    while t >= SUBLANES_PACKED:
        if n % t == 0:
            return t
        t -= SUBLANES_PACKED
    return n


def _lane(shape):
    return lax.broadcasted_iota(jnp.int32, shape, 1)


def _cols_to_lanes(cols, rows):
    lane = _lane((rows, LANES))
    out = jnp.zeros((rows, LANES), F32)
    for j, col in enumerate(cols):
        out = jnp.where(lane == j, col, out)
    return out


def _mm(a, b, *, mode, name, res=None, out_dtype=F32, tm=1408, tn=1536, tk=1408):
    if mode == "nn":
        (M, K), (K2, N) = a.shape, b.shape
    elif mode == "nt":
        (M, K), (N, K2) = a.shape, b.shape
    else:
        (K, M), (K2, N) = a.shape, b.shape
    assert K == K2, (a.shape, b.shape, mode)
    tm, tn, tk = _tile(M, tm), _tile(N, tn), _tile(K, tk)
    nk = K // tk
    in_place = out_dtype == F32 or nk == 1

    def body(*refs):
        refs = list(refs)
        a_ref, b_ref = refs[:2]
        r_ref = refs[2] if res is not None else None
        o_ref = refs[3 if res is not None else 2]
        acc = o_ref if in_place else refs[-1]
        k = pl.program_id(2)
        if mode == "nn":
            part = _dot(a_ref[...], b_ref[...], NN)
        elif mode == "nt":
            part = _dot(a_ref[...], b_ref[...], NT)
        else:
            part = _dot(a_ref[...].T, b_ref[...], NN)
        if nk == 1:
            o_ref[...] = (part if res is None else part + r_ref[...].astype(F32)).astype(o_ref.dtype)
            return

        @pl.when(k == 0)
        def _():
            acc[...] = part if res is None else part + r_ref[...].astype(F32)

        @pl.when(k > 0)
        def _():
            acc[...] += part

        if not in_place:
            @pl.when(k == nk - 1)
            def _():
                o_ref[...] = acc[...].astype(o_ref.dtype)

    if mode == "nn":
        a_spec = pl.BlockSpec((tm, tk), lambda i, j, k: (i, k))
        b_spec = pl.BlockSpec((tk, tn), lambda i, j, k: (k, j))
    elif mode == "nt":
        a_spec = pl.BlockSpec((tm, tk), lambda i, j, k: (i, k))
        b_spec = pl.BlockSpec((tn, tk), lambda i, j, k: (j, k))
    else:
        a_spec = pl.BlockSpec((tk, tm), lambda i, j, k: (k, i))
        b_spec = pl.BlockSpec((tk, tn), lambda i, j, k: (k, j))
    o_spec = pl.BlockSpec((tm, tn), lambda i, j, k: (i, j))
    in_specs = [a_spec, b_spec] + ([] if res is None else [o_spec])
    args = (a, b) + (() if res is None else (res,))
    return _pcall(
        body, name=name, dims=("parallel", "parallel", "arbitrary"),
        grid=(M // tm, N // tn, nk), in_specs=in_specs, out_specs=o_spec,
        out_shape=jax.ShapeDtypeStruct((M, N), out_dtype),
        scratch_shapes=[] if in_place else [pltpu.VMEM((tm, tn), F32)],
    )(*args)


EXCHANGE_DTYPE = jnp.bfloat16


def _dw(a, b, *, name):
    return _mm(a, b, mode="tn", name=name, out_dtype=EXCHANGE_DTYPE)


def _rms_parts(xf):
    r = lax.rsqrt(jnp.mean(xf * xf, axis=-1, keepdims=True) + NORM_EPS)
    return xf * r, r


def _rms_bwd_rows(xf, g, dy):
    xhat, r = _rms_parts(xf)
    dxhat = dy * g
    dx = r * (dxhat - xhat * jnp.mean(dxhat * xhat, axis=-1, keepdims=True))
    return dx, dy * xhat


def _dx_norm_bwd(a, w, x, g, dres, *, name, res=None, tm=1024, tk=1408):
    (M, K), N = a.shape, w.shape[0]
    tm, tk = _tile(M, tm), _tile(K, tk)
    nk = K // tk

    def body(*refs):
        refs = list(refs)
        a_ref, w_ref, x_ref, g_ref, dr_ref = refs[:5]
        r_ref = refs[5] if res is not None else None
        dx_ref, dg_ref = refs[-2:]
        i, k = pl.program_id(0), pl.program_id(1)
        part = _dot(a_ref[...], w_ref[...], NT)

        @pl.when(k == 0)
        def _():
            dx_ref[...] = part if res is None else part + r_ref[...]

        @pl.when(k > 0)
        def _():
            dx_ref[...] += part

        @pl.when(k == nk - 1)
        def _():
            dx, dgp = _rms_bwd_rows(x_ref[...], g_ref[...], dx_ref[...])
            dx_ref[...] = dx + dr_ref[...]

            @pl.when(i == 0)
            def _():
                dg_ref[...] = jnp.zeros_like(dg_ref)

            dg_ref[...] += jnp.sum(dgp, axis=0, keepdims=True)

    row = pl.BlockSpec((tm, N), lambda i, k: (i, 0))
    vec = pl.BlockSpec((1, N), lambda i, k: (0, 0))
    in_specs = [pl.BlockSpec((tm, tk), lambda i, k: (i, k)), pl.BlockSpec((N, tk), lambda i, k: (0, k)), row, vec, row]
    args = [a, w, x, g.reshape(1, N), dres]
    if res is not None:
        in_specs.append(row)
        args.append(res)
    return _pcall(
        body, name=name, dims=("arbitrary", "arbitrary"), grid=(M // tm, nk), in_specs=in_specs,
        out_specs=[row, vec], out_shape=[_sds((M, N)), _sds((1, N))],
    )(*args)


def _rmsnorm(x, g, *, name, out_dtype=MXU_DTYPE, tm=1024):
    M, D = x.shape
    tm = _tile(M, tm)

    def body(x_ref, g_ref, o_ref):
        xhat, _ = _rms_parts(x_ref[...].astype(F32))
        o_ref[...] = (xhat * g_ref[...]).astype(o_ref.dtype)

    return _pcall(
        body, name=name, dims=("parallel",), grid=(M // tm,),
        in_specs=[pl.BlockSpec((tm, D), lambda i: (i, 0)), pl.BlockSpec((1, D), lambda i: (0, 0))],
        out_specs=pl.BlockSpec((tm, D), lambda i: (i, 0)),
        out_shape=jax.ShapeDtypeStruct((M, D), out_dtype),
    )(x, g.reshape(1, D))


def _rmsnorm_bwd(x, g, dy, *, name, dres=None, tm=512):
    M, D = x.shape
    tm = _tile(M, tm)

    def body(*refs):
        if dres is None:
            x_ref, g_ref, dy_ref, dx_ref, dg_ref = refs
        else:
            x_ref, g_ref, dy_ref, dr_ref, dx_ref, dg_ref = refs
        dx, dgp = _rms_bwd_rows(x_ref[...].astype(F32), g_ref[...], dy_ref[...].astype(F32))
        if dres is not None:
            dx = dx + dr_ref[...]
        dx_ref[...] = dx

        @pl.when(pl.program_id(0) == 0)
        def _():
            dg_ref[...] = jnp.zeros_like(dg_ref)

        dg_ref[...] += jnp.sum(dgp, axis=0, keepdims=True)

    row = pl.BlockSpec((tm, D), lambda i: (i, 0))
    vec = pl.BlockSpec((1, D), lambda i: (0, 0))
    in_specs = [row, vec, row] + ([] if dres is None else [row])
    args = (x, g.reshape(1, D), dy) + (() if dres is None else (dres,))
    return _pcall(
        body, name=name, dims=("arbitrary",), grid=(M // tm,), in_specs=in_specs, out_specs=[row, vec],
        out_shape=[jax.ShapeDtypeStruct((M, D), F32), jax.ShapeDtypeStruct((1, D), F32)],
    )(*args)


def _rope_chunk(t, c, s, half):
    lane = _lane(t.shape)
    swapped = jnp.where((lane % (2 * half)) < half, pltpu.roll(t, LANES - half, 1), pltpu.roll(t, half, 1))
    return t * c + swapped * s


def _rope_tables(positions):
    pos = positions.reshape(-1).astype(F32)[:, None]

    def table(dh, first, copies, sine, fill=0.0):
        half = dh // 2
        lane = np.arange(LANES)
        inside = (lane >= first) & (lane < first + copies * dh)
        idx = np.where(inside, (lane - first) % half, 0)
        inv_freq = ROPE_THETA ** (-jnp.asarray(2 * idx, F32) / dh)
        sign = np.where((lane - first) % dh < half, -1.0, 1.0) if sine else np.ones(LANES)
        ang = pos * inv_freq[None, :]
        val = (jnp.sin(ang) if sine else jnp.cos(ang)) * jnp.asarray(sign, F32)[None, :]
        return jnp.where(jnp.asarray(inside)[None, :], val, fill)

    return dict(
        c64=table(HEAD_DIM, 0, 2, False), s64=table(HEAD_DIM, 0, 2, True),
        ck=table(MLA_ROPE, 0, 1, False), sk=table(MLA_ROPE, 0, 1, True),
        cm=jnp.where(jnp.asarray(np.arange(LANES) < MLA_NOPE)[None, :], 1.0, table(MLA_ROPE, MLA_NOPE, 1, False)),
        sm=table(MLA_ROPE, MLA_NOPE, 1, True),
    )


class _Attn:
    def __init__(self, *, T, Tk, G, nh, rep, dqk, dv, tq, tk, mode, scale, qcol, kcol, vcol, ocol, o_width,
                 max_dist=0):
        self.__dict__.update(locals())
        self.nkv = nh // rep
        assert T % tq == 0 and Tk % tk == 0 and nh <= LANES


def _attn_delta(cfg, o, do, *, name, lse=None, sink=None, tm=512):
    c = cfg
    tm = _tile(c.T, tm)
    width = c.nh * c.dv

    def body(*refs):
        refs = list(refs)
        o_ref, do_ref = refs[:2]
        rest = refs[2:]
        lse_ref, sink_ref = (rest.pop(0), rest.pop(0)) if sink is not None else (None, None)
        d_ref = rest.pop(0)
        prod = o_ref[...].astype(F32) * do_ref[...].astype(F32)
        cols = [jnp.sum(prod[:, j * c.dv:(j + 1) * c.dv], axis=1, keepdims=True) for j in range(c.nh)]
        delta = _cols_to_lanes(cols, tm)
        d_ref[...] = delta
        if sink is not None:
            ds_ref = rest.pop(0)

            @pl.when(pl.program_id(1) == 0)
            def _():
                ds_ref[...] = jnp.zeros_like(ds_ref)

            lane = _lane((tm, LANES))
            ps = jnp.where(lane < c.nh, jnp.exp(sink_ref[...] - lse_ref[...]), 0.0)
            ds_ref[...] -= jnp.sum(ps * delta, axis=0, keepdims=True)

    stat = pl.BlockSpec((tm, LANES), lambda g, i: (i, g))
    in_specs = [pl.BlockSpec((tm, width), lambda g, i: (i, c.ocol(g)))] * 2
    args = [o, do]
    out_specs, out_shape = [stat], [jax.ShapeDtypeStruct((c.T, LANES * c.G), F32)]
    if sink is not None:
        assert c.G == 1
        in_specs += [stat, pl.BlockSpec((1, LANES), lambda g, i: (0, 0))]
        args += [lse, sink]
        out_specs.append(pl.BlockSpec((1, LANES), lambda g, i: (0, 0)))
        out_shape.append(jax.ShapeDtypeStruct((1, LANES), F32))
    out = _pcall(
        body, name=name, dims=("arbitrary", "arbitrary"), grid=(c.G, c.T // tm),
        in_specs=in_specs, out_specs=out_specs, out_shape=out_shape,
    )(*args)
    return out if sink is not None else (out[0], None)


TN = (((0,), (0,)), ((), ()))


def _band_mask(c, i):
    key = lax.broadcasted_iota(jnp.int32, (2 * BLOCK, BLOCK), 0)
    qry = lax.broadcasted_iota(jnp.int32, (2 * BLOCK, BLOCK), 1)
    d = BLOCK + qry - key
    return (d >= 0) & (d <= c.max_dist) & ((key >= BLOCK) | (i > 0))


def _head_pairs(c):
    return c.rep == 1 and c.dqk == c.dv == LANES // 2 and c.nh % 2 == 0


def _block_diagonal(pair):
    lane = _lane(pair.shape)
    zero = jnp.zeros_like(pair)
    return jnp.concatenate([jnp.where(lane < LANES // 2, pair, zero), jnp.where(lane >= LANES // 2, pair, zero)], axis=0)


def _own_blocks(t):
    n = t.shape[1] // 2
    rows = lax.broadcasted_iota(jnp.int32, (LANES, n), 0)
    return jnp.where(rows < LANES // 2, t[:, :n], t[:, n:])


def _rows_to_stats(rows, n):
    return jnp.concatenate(rows + [jnp.zeros((LANES - len(rows), n), F32)], axis=0).T


def _band_fwd(cfg, q, k, v, *, name, sink=None, out_dtype=F32):
    c = cfg
    assert c.mode == "band" and c.tq == c.tk == BLOCK and c.T == c.Tk
    nq = c.T // BLOCK

    def body(*refs):
        if sink is None:
            q_ref, kp_ref, kc_ref, vp_ref, vc_ref, o_ref, lse_ref = refs
        else:
            q_ref, kp_ref, kc_ref, vp_ref, vc_ref, sink_ref, o_ref, lse_ref = refs
        mask = _band_mask(c, pl.program_id(1))
        k2 = jnp.concatenate([kp_ref[...], kc_ref[...]], axis=0)
        v2 = jnp.concatenate([vp_ref[...], vc_ref[...]], axis=0)
        lses = []
        if _head_pairs(c):
            mask2 = jnp.concatenate([mask, mask], axis=1)
            pair_lanes = [slice(pc * LANES, (pc + 1) * LANES) for pc in range(c.nh // 2)]
            score = lambda sl: _dot(k2[:, sl], _block_diagonal(q_ref[:, sl]), NT)
            ahead, behind = score(pair_lanes[0]), None

            def finish(entry):
                sl, o_t, l = entry
                o_ref[:, sl] = _own_blocks(o_t / l).T.astype(o_ref.dtype)

            for pc, sl in enumerate(pair_lanes):
                s = ahead * c.scale
                if pc + 1 < len(pair_lanes):
                    ahead = score(pair_lanes[pc + 1])
                s = jnp.where(mask2, s, -jnp.inf)
                m = jnp.max(s, axis=0, keepdims=True)
                p = jnp.exp(s - m)
                l = jnp.sum(p, axis=0, keepdims=True)
                if behind is not None:
                    finish(behind)
                behind = (sl, _dot(v2[:, sl], p, TN), l)
                lse = m + jnp.log(l)
                lses += [lse[:, :BLOCK], lse[:, BLOCK:]]
            finish(behind)
        heads = [] if _head_pairs(c) else list(range(c.nh))
        score_of = lambda j: _dot(k2[:, (j // c.rep) * c.dqk:(j // c.rep + 1) * c.dqk],
                                  q_ref[:, j * c.dqk:(j + 1) * c.dqk], NT)
        ahead = score_of(0) if heads else None
        for j in heads:
            g = j // c.rep
            s = ahead * c.scale
            if j + 1 < c.nh:
                ahead = score_of(j + 1)
            s = jnp.where(mask, s, -jnp.inf)
            m = jnp.max(s, axis=0, keepdims=True)
            if sink is not None:
                sk = sink_ref[:, j:j + 1]
                m = jnp.maximum(m, sk)
            p = jnp.exp(s - m)
            l = jnp.sum(p, axis=0, keepdims=True)
            if sink is not None:
                l = l + jnp.exp(sk - m)
            o_t = _dot(v2[:, g * c.dv:(g + 1) * c.dv], p, TN)
            o_ref[:, j * c.dv:(j + 1) * c.dv] = (o_t / l).T.astype(o_ref.dtype)
            lses.append(m + jnp.log(l))
        lse_ref[...] = _rows_to_stats(lses, BLOCK)

    prev = lambda i: jnp.maximum(i - 1, 0)
    kw, vw = c.nkv * c.dqk, c.nkv * c.dv
    in_specs = [
        pl.BlockSpec((BLOCK, c.nh * c.dqk), lambda g, i: (i, c.qcol(g))),
        pl.BlockSpec((BLOCK, kw), lambda g, i: (prev(i), c.kcol(g))),
        pl.BlockSpec((BLOCK, kw), lambda g, i: (i, c.kcol(g))),
        pl.BlockSpec((BLOCK, vw), lambda g, i: (prev(i), c.vcol(g))),
        pl.BlockSpec((BLOCK, vw), lambda g, i: (i, c.vcol(g))),
    ]
    args = [q, k, k, v, v]
    if sink is not None:
        in_specs.append(pl.BlockSpec((1, LANES), lambda g, i: (0, 0)))
        args.append(sink)
    return _pcall(
        body, name=name, dims=("parallel", "parallel"), grid=(c.G, nq), in_specs=in_specs,
        out_specs=[pl.BlockSpec((BLOCK, c.nh * c.dv), lambda g, i: (i, c.ocol(g))),
                   pl.BlockSpec((BLOCK, LANES), lambda g, i: (i, g))],
        out_shape=[jax.ShapeDtypeStruct((c.T, c.o_width), out_dtype),
                   jax.ShapeDtypeStruct((c.T, LANES * c.G), F32)],
    )(*args)


def _band_bwd(cfg, q, k, v, do, lse, delta, *, name):
    c = cfg
    assert c.mode == "band" and c.tq == c.tk == BLOCK and c.T == c.Tk
    nq = c.T // BLOCK
    qw, kw, vw = c.nh * c.dqk, c.nkv * c.dqk, c.nkv * c.dv

    def body(q_ref, kp_ref, kc_ref, vp_ref, vc_ref, do_ref, lse_ref, d_ref, dq_ref, dk_ref, dv_ref, dk_c, dv_c):
        n = pl.program_id(1)

        @pl.when(n == 0)
        def _():
            dk_c[...] = jnp.zeros_like(dk_c)
            dv_c[...] = jnp.zeros_like(dv_c)

        @pl.when(n < nq)
        def _():
            mask = _band_mask(c, n)
            k2 = jnp.concatenate([kp_ref[...], kc_ref[...]], axis=0)
            v2 = jnp.concatenate([vp_ref[...], vc_ref[...]], axis=0)
            lse_t, d_t = lse_ref[...].T, d_ref[...].T
            if _head_pairs(c):
                mask2 = jnp.concatenate([mask, mask], axis=1)
                pair_lanes = [slice(pc * LANES, (pc + 1) * LANES) for pc in range(c.nh // 2)]

                def first(sl):
                    q_bd, do_bd = _block_diagonal(q_ref[:, sl]), _block_diagonal(do_ref[:, sl])
                    return q_bd, do_bd, k2[:, sl], _dot(k2[:, sl], q_bd, NT), _dot(v2[:, sl], do_bd, NT)

                def finish(entry):
                    sl, dq_t, dv_pair, dk_pair = entry
                    dq_ref[:, sl] = _own_blocks(dq_t).T
                    dk_ref[:, sl] = dk_c[:, sl] + dk_pair[:BLOCK]
                    dv_ref[:, sl] = dv_c[:, sl] + dv_pair[:BLOCK]
                    dk_c[:, sl] = dk_pair[BLOCK:]
                    dv_c[:, sl] = dv_pair[BLOCK:]

                ahead, behind = first(pair_lanes[0]), None
                for pc, sl in enumerate(pair_lanes):
                    q_bd, do_bd, kp, s, dp = ahead
                    if pc + 1 < len(pair_lanes):
                        ahead = first(pair_lanes[pc + 1])
                    both = lambda t: jnp.concatenate([t[2 * pc:2 * pc + 1, :], t[2 * pc + 1:2 * pc + 2, :]], axis=1)
                    p = jnp.exp(jnp.where(mask2, s * c.scale, -jnp.inf) - both(lse_t))
                    ds = p * (dp - both(d_t)) * c.scale
                    entry = (sl, _dot(kp, ds, TN), _dot(p, do_bd, NN), _dot(ds, q_bd, NN))
                    if behind is not None:
                        finish(behind)
                    behind = entry
                finish(behind)
                return
            dk2, dv2 = [None] * c.nkv, [None] * c.nkv

            def first_of(j):
                g = j // c.rep
                qh, doh = q_ref[:, j * c.dqk:(j + 1) * c.dqk], do_ref[:, j * c.dv:(j + 1) * c.dv]
                kh = k2[:, g * c.dqk:(g + 1) * c.dqk]
                return qh, doh, kh, _dot(kh, qh, NT), _dot(v2[:, g * c.dv:(g + 1) * c.dv], doh, NT)

            ahead = first_of(0)
            for j in range(c.nh):
                g = j // c.rep
                qh, doh, kh, s, dp = ahead
                if j + 1 < c.nh:
                    ahead = first_of(j + 1)
                p = jnp.exp(jnp.where(mask, s * c.scale, -jnp.inf) - lse_t[j:j + 1, :])
                ds = p * (dp - d_t[j:j + 1, :]) * c.scale
                dq_ref[:, j * c.dqk:(j + 1) * c.dqk] = _dot(kh, ds, TN).T
                dvh, dkh = _dot(p, doh, NN), _dot(ds, qh, NN)
                dv2[g] = dvh if dv2[g] is None else dv2[g] + dvh
                dk2[g] = dkh if dk2[g] is None else dk2[g] + dkh
            for g in range(c.nkv):
                ks, vs = slice(g * c.dqk, (g + 1) * c.dqk), slice(g * c.dv, (g + 1) * c.dv)
                dk_ref[:, ks] = dk_c[:, ks] + dk2[g][:BLOCK]
                dv_ref[:, vs] = dv_c[:, vs] + dv2[g][:BLOCK]
                dk_c[:, ks] = dk2[g][BLOCK:]
                dv_c[:, vs] = dv2[g][BLOCK:]

        @pl.when(n == nq)
        def _():
            dk_ref[...] = dk_c[...]
            dv_ref[...] = dv_c[...]

    cur = lambda n: jnp.minimum(n, nq - 1)
    prev = lambda n: jnp.maximum(cur(n) - 1, 0)
    out_blk = lambda n: jnp.maximum(n - 1, 0)
    stat = pl.BlockSpec((BLOCK, LANES), lambda g, n: (cur(n), g))
    dq_spec = pl.BlockSpec((BLOCK, qw), lambda g, n: (cur(n), g))
    dk_spec = pl.BlockSpec((BLOCK, kw), lambda g, n: (out_blk(n), g))
    dv_spec = pl.BlockSpec((BLOCK, vw), lambda g, n: (out_blk(n), g))
    in_specs = [
        pl.BlockSpec((BLOCK, qw), lambda g, n: (cur(n), c.qcol(g))),
        pl.BlockSpec((BLOCK, kw), lambda g, n: (prev(n), c.kcol(g))),
        pl.BlockSpec((BLOCK, kw), lambda g, n: (cur(n), c.kcol(g))),
        pl.BlockSpec((BLOCK, vw), lambda g, n: (prev(n), c.vcol(g))),
        pl.BlockSpec((BLOCK, vw), lambda g, n: (cur(n), c.vcol(g))),
        pl.BlockSpec((BLOCK, c.nh * c.dv), lambda g, n: (cur(n), c.ocol(g))),
        stat, stat,
    ]
    return _pcall(
        body, name=name, dims=("parallel", "arbitrary"), grid=(c.G, nq + 1), in_specs=in_specs,
        out_specs=[dq_spec, dk_spec, dv_spec],
        out_shape=[_sds((c.T, c.G * qw)), _sds((c.T, c.G * kw)), _sds((c.T, c.G * vw))],
        scratch_shapes=[pltpu.VMEM((BLOCK, kw), F32), pltpu.VMEM((BLOCK, vw), F32)],
    )(q, k, k, v, v, do, lse, delta)


def _causal_pairs(n, kv_major):
    pairs =[(i, j) for j in range(n) for i in range(j, n)] if kv_major else [(i, j) for i in range(n) for j in range(i + 1)]
    return jnp.asarray(np.array([p[0] for p in pairs], np.int32)), jnp.asarray(np.array([p[1] for p in pairs], np.int32))


def _causal_mask(t):
    return lax.broadcasted_iota(jnp.int32, (t, t), 0) >= lax.broadcasted_iota(jnp.int32, (t, t), 1)


def _carrying(body, n_in, n_out, n_scratch, grid, carry):
    if carry is None:
        return body
    G, P = grid

    def wrapped(*refs):
        refs = list(refs)
        prefetch, refs = refs[:2], refs[2:]
        ins, src = refs[:n_in], refs[n_in]
        outs, out = refs[n_in + 1:n_in + 1 + n_out], refs[n_in + 1 + n_out]
        scratch, sems = refs[n_in + 2 + n_out:n_in + 2 + n_out + n_scratch], refs[n_in + 2 + n_out + n_scratch:]
        step = pl.program_id(0) * P + pl.program_id(1)
        carry.run([src, out] + sems, step, G * P, at_end=False)
        body(*prefetch, *ins, *outs, *scratch)
        carry.run([src, out] + sems, step, G * P, at_end=True)

    return wrapped


def _carry_specs(carry):
    if carry is None:
        return [], [], [], [], []
    any_space = pl.BlockSpec(memory_space=pl.ANY)
    return [any_space], [any_space], [carry.out_shape], list(carry.sems), [carry.src]


def _causal_fwd(cfg, q, k, v, *, name, out_dtype=F32, stat_heads=None, carry=None):
    c = cfg
    assert c.mode == "causal" and c.tq == c.tk and c.T == c.Tk
    t, n = c.tq, c.T // c.tq
    stat_heads = stat_heads or c.nh
    stat_blocks = c.nh // stat_heads
    assert stat_blocks * stat_heads == c.nh
    qi_tab, kj_tab = _causal_pairs(n, kv_major=False)
    n_pairs = int(qi_tab.shape[0])

    def body(qi_ref, kj_ref, q_ref, k_ref, v_ref, o_ref, lse_ref, m_scr, l_scr, acc):
        pair = pl.program_id(1)
        qi, kj = qi_ref[pair], kj_ref[pair]

        @pl.when(kj == 0)
        def _():
            m_scr[...] = jnp.full_like(m_scr, NEG_BIG)
            l_scr[...] = jnp.zeros_like(l_scr)
            acc[...] = jnp.zeros_like(acc)

        def step(diagonal):
            mask = None
            if diagonal:
                mask = lax.broadcasted_iota(jnp.int32, (t, t), 1) >= lax.broadcasted_iota(jnp.int32, (t, t), 0)
            scores = [_dot(k_ref[:, (j // c.rep) * c.dqk:(j // c.rep + 1) * c.dqk],
                           q_ref[:, j * c.dqk:(j + 1) * c.dqk], NT) for j in range(c.nh)]
            for j in range(c.nh):
                g = j // c.rep
                s = scores[j] * c.scale
                if diagonal:
                    s = jnp.where(mask, s, -jnp.inf)
                m_prev = m_scr[j]
                m_new = jnp.maximum(m_prev, jnp.max(s, axis=0, keepdims=True))
                alpha = jnp.exp(m_prev - m_new)
                p = jnp.exp(s - m_new)
                l_scr[j] = alpha * l_scr[j] + jnp.sum(p, axis=0, keepdims=True)
                acc[j] = alpha * acc[j] + _dot(v_ref[:, g * c.dv:(g + 1) * c.dv], p, TN)
                m_scr[j] = m_new

        pl.when(kj == qi)(lambda: step(True))
        pl.when(kj != qi)(lambda: step(False))

        @pl.when(kj == qi)
        def _():
            rows = []
            for j in range(c.nh):
                o_ref[:, j * c.dv:(j + 1) * c.dv] = (acc[j] / l_scr[j]).T.astype(o_ref.dtype)
                rows.append(m_scr[j] + jnp.log(l_scr[j]))
            for b in range(stat_blocks):
                lse_ref[:, b * LANES:(b + 1) * LANES] = _rows_to_stats(rows[b * stat_heads:(b + 1) * stat_heads], t)

    x_in, x_out, x_shapes, x_scratch, x_args = _carry_specs(carry)
    grid_spec = pltpu.PrefetchScalarGridSpec(
        num_scalar_prefetch=2, grid=(c.G, n_pairs),
        in_specs=[pl.BlockSpec((t, c.nh * c.dqk), lambda g, p, qi, kj: (qi[p], c.qcol(g))),
                  pl.BlockSpec((t, c.nkv * c.dqk), lambda g, p, qi, kj: (kj[p], c.kcol(g))),
                  pl.BlockSpec((t, c.nkv * c.dv), lambda g, p, qi, kj: (kj[p], c.vcol(g)))] + x_in,
        out_specs=[pl.BlockSpec((t, c.nh * c.dv), lambda g, p, qi, kj: (qi[p], c.ocol(g))),
                   pl.BlockSpec((t, LANES * stat_blocks), lambda g, p, qi, kj: (qi[p], g))] + x_out,
        scratch_shapes=[pltpu.VMEM((c.nh, 1, t), F32), pltpu.VMEM((c.nh, 1, t), F32),
                        pltpu.VMEM((c.nh, c.dv, t), F32)] + x_scratch)
    return _pcall(
        _carrying(body, 3, 2, 3, (c.G, n_pairs), carry), name=name,
        dims=("arbitrary", "arbitrary") if carry is not None else ("parallel", "arbitrary"), grid_spec=grid_spec,
        out_shape=[jax.ShapeDtypeStruct((c.T, c.o_width), out_dtype),
                   jax.ShapeDtypeStruct((c.T, LANES * c.G * stat_blocks), F32)] + x_shapes,
    )(qi_tab, kj_tab, q, k, v, *x_args)


def _causal_bwd(cfg, q, k, v, do, lse, delta, *, name, carry=None):
    c = cfg
    assert c.mode == "causal" and c.tq == c.tk and c.T == c.Tk
    t, n = c.tq, c.T // c.tq
    qw, kw, vw = c.nh * c.dqk, c.nkv * c.dqk, c.nkv * c.dv
    qi_tab, kj_tab = _causal_pairs(n, kv_major=True)

    def body(qi_ref, kj_ref, q_ref, k_ref, v_ref, do_ref, lse_ref, d_ref, dq_ref, dk_ref, dv_ref, dk_acc, dv_acc):
        pair = pl.program_id(1)
        qi, kj = qi_ref[pair], kj_ref[pair]

        @pl.when(pair == 0)
        def _():
            dq_ref[...] = jnp.zeros_like(dq_ref)

        @pl.when(qi == kj)
        def _():
            dk_acc[...] = jnp.zeros_like(dk_acc)
            dv_acc[...] = jnp.zeros_like(dv_acc)

        rows = pl.ds(pl.multiple_of(qi * t, t), t)

        def step(diagonal):
            mask = _causal_mask(t) if diagonal else None
            for j in range(c.nh):
                g = j // c.rep
                qs, ks, vs = (slice(j * c.dqk, (j + 1) * c.dqk), slice(g * c.dqk, (g + 1) * c.dqk),
                              slice(g * c.dv, (g + 1) * c.dv))
                qh, doh, kh = q_ref[:, qs], do_ref[:, j * c.dv:(j + 1) * c.dv], k_ref[:, ks]
                s = _dot(qh, kh, NT) * c.scale
                if diagonal:
                    s = jnp.where(mask, s, -jnp.inf)
                p = jnp.exp(s - lse_ref[:, j:j + 1])
                ds = p * (_dot(doh, v_ref[:, vs], NT) - d_ref[:, j:j + 1]) * c.scale
                dq_ref[rows, qs] += _dot(ds, kh, NN)
                dv_acc[g] += _dot(doh, p, TN)
                dk_acc[g] += _dot(qh, ds, TN)

        pl.when(qi == kj)(lambda: step(True))
        pl.when(qi != kj)(lambda: step(False))

        @pl.when(qi == n - 1)
        def _():
            for g in range(c.nkv):
                dk_ref[:, g * c.dqk:(g + 1) * c.dqk] = dk_acc[g].T
                dv_ref[:, g * c.dv:(g + 1) * c.dv] = dv_acc[g].T

    stat = pl.BlockSpec((t, LANES), lambda g, p, qi, kj: (qi[p], g))
    o_spec = pl.BlockSpec((t, c.nh * c.dv), lambda g, p, qi, kj: (qi[p], c.ocol(g)))
    n_pairs = int(qi_tab.shape[0])
    x_in, x_out, x_shapes, x_scratch, x_args = _carry_specs(carry)
    grid_spec = pltpu.PrefetchScalarGridSpec(
        num_scalar_prefetch=2, grid=(c.G, n_pairs),
        in_specs=[pl.BlockSpec((t, qw), lambda g, p, qi, kj: (qi[p], c.qcol(g))),
                  pl.BlockSpec((t, kw), lambda g, p, qi, kj: (kj[p], c.kcol(g))),
                  pl.BlockSpec((t, vw), lambda g, p, qi, kj: (kj[p], c.vcol(g))),
                  o_spec, stat, stat] + x_in,
        out_specs=[pl.BlockSpec((c.T, qw), lambda g, p, qi, kj: (0, g)),
                   pl.BlockSpec((t, kw), lambda g, p, qi, kj: (kj[p], g)),
                   pl.BlockSpec((t, vw), lambda g, p, qi, kj: (kj[p], g))] + x_out,
        scratch_shapes=[pltpu.VMEM((c.nkv, c.dqk, t), F32), pltpu.VMEM((c.nkv, c.dv, t), F32)] + x_scratch)
    return _pcall(
        _carrying(body, 6, 3, 2, (c.G, n_pairs), carry), name=name,
        dims=("arbitrary", "arbitrary") if carry is not None else ("parallel", "arbitrary"), grid_spec=grid_spec,
        out_shape=[_sds((c.T, c.G * qw)), _sds((c.T, c.G * kw)), _sds((c.T, c.G * vw))] + x_shapes,
    )(qi_tab, kj_tab, q, k, v, do, lse, delta, *x_args)


def _rowwise(body, ins, outs, *, name, rows, tm=512, accs=(), scratch=()):
    tm = _row_tile(rows, tm)

    def spec(a):
        if a.shape[0] == 1:
            return pl.BlockSpec((1, a.shape[1]), lambda i: (0, 0))
        d = rows // a.shape[0]
        assert d * a.shape[0] == rows and tm % d == 0
        return pl.BlockSpec((tm // d, a.shape[1]), lambda i: (i, 0))

    return _pcall(
        functools.partial(body, tm), name=name, dims=("arbitrary" if accs else "parallel",), grid=(rows // tm,),
        in_specs=[spec(a) for a in ins], out_specs=[spec(a) for a in outs], out_shape=list(outs),
        scratch_shapes=list(scratch),
    )(*ins)


def _sds(shape, dtype=F32):
    return jax.ShapeDtypeStruct(shape, dtype)


def _acc_rows(ref, val):
    @pl.when(pl.program_id(0) == 0)
    def _():
        ref[...] = jnp.zeros_like(ref)

    ref[...] += jnp.sum(val, axis=0, keepdims=True)


Z_QA, Z_KA, Z_VA, Z_CQ, Z_CKV, Z_KR, Z_END = 0, 512, 640, 768, 1152, 1408, 1536


def _l0_prep(z, tabs, q_norm, kv_norm, *, name):
    S = z.shape[0]

    def body(tm, z_ref, c64, s64, ck, sk, gq, gkv, qa_o, ka_o, va_o, cq_o, ckv_o, kr_o):
        for i in range(4):
            sl = slice(Z_QA + i * LANES, Z_QA + (i + 1) * LANES)
            qa_o[:, i * LANES:(i + 1) * LANES] = _rope_chunk(z_ref[:, sl], c64[...], s64[...], 32).astype(qa_o.dtype)
        ka_o[...] = _rope_chunk(z_ref[:, Z_KA:Z_VA], c64[...], s64[...], 32).astype(ka_o.dtype)
        va_o[...] = z_ref[:, Z_VA:Z_CQ].astype(va_o.dtype)
        cq_o[...] = (_rms_parts(z_ref[:, Z_CQ:Z_CKV])[0] * gq[...]).astype(cq_o.dtype)
        ckv_o[...] = (_rms_parts(z_ref[:, Z_CKV:Z_KR])[0] * gkv[...]).astype(ckv_o.dtype)
        kr_o[...] = _rope_chunk(z_ref[:, Z_KR:Z_END], ck[...], sk[...], 16)

    outs = [_sds((S, 512), MXU_DTYPE), _sds((S, 128), MXU_DTYPE), _sds((S, 128), MXU_DTYPE),
            _sds((S, MLA_Q_RANK), MXU_DTYPE), _sds((S, MLA_KV_RANK), MXU_DTYPE), _sds((S, LANES))]
    ins = [z, tabs["c64"], tabs["s64"], tabs["ck"], tabs["sk"], q_norm.reshape(1, -1), kv_norm.reshape(1, -1)]
    return _rowwise(body, ins, outs, name=name, rows=S)


def _l0_prep_bwd(z, tabs, q_norm, kv_norm, dqa, dka, dva, dcq, dckv, dkr, *, name):
    S = z.shape[0]

    def body(tm, z_ref, c64, s64, ck, sk, gq, gkv, dqa_r, dka_r, dva_r, dcq_r, dckv_r, dkr_r, dz_o, dgq_o, dgkv_o):
        for i in range(4):
            sl = slice(i * LANES, (i + 1) * LANES)
            dz_o[:, sl] = _rope_chunk(dqa_r[:, sl].astype(F32), c64[...], -s64[...], 32).astype(dz_o.dtype)
        dz_o[:, Z_KA:Z_VA] = _rope_chunk(dka_r[...].astype(F32), c64[...], -s64[...], 32).astype(dz_o.dtype)
        dz_o[:, Z_VA:Z_CQ] = dva_r[...].astype(dz_o.dtype)
        dx, dgp = _rms_bwd_rows(z_ref[:, Z_CQ:Z_CKV], gq[...], dcq_r[...].astype(F32))
        dz_o[:, Z_CQ:Z_CKV] = dx.astype(dz_o.dtype)
        _acc_rows(dgq_o, dgp)
        dx, dgp = _rms_bwd_rows(z_ref[:, Z_CKV:Z_KR], gkv[...], dckv_r[...].astype(F32))
        dz_o[:, Z_CKV:Z_KR] = dx.astype(dz_o.dtype)
        _acc_rows(dgkv_o, dgp)
        dz_o[:, Z_KR:Z_END] = _rope_chunk(dkr_r[...], ck[...], -sk[...], 16).astype(dz_o.dtype)

    outs = [_sds((S, Z_END), MXU_DTYPE), _sds((1, MLA_Q_RANK)), _sds((1, MLA_KV_RANK))]
    ins = [z, tabs["c64"], tabs["s64"], tabs["ck"], tabs["sk"], q_norm.reshape(1, -1), kv_norm.reshape(1, -1),
           dqa, dka, dva, dcq, dckv, dkr]
    return _rowwise(body, ins, outs, name=name, rows=S, accs=(1, 2))


def _mla_prep(qb, kvb, kr, tabs, *, name):
    S = qb.shape[0]

    def body(tm, qb_r, kvb_r, kr_r, cm, sm, q_o, k_o, v_o):
        lane = _lane((tm, LANES))
        kr_at_64 = pltpu.roll(kr_r[...], 64, 1)
        for h in range(MLA_HEADS):
            sl = slice(h * LANES, (h + 1) * LANES)
            q_o[:, sl] = _rope_chunk(qb_r[:, sl], cm[...], sm[...], 16).astype(q_o.dtype)
            k_o[:, sl] = jnp.where(lane < 64, kvb_r[:, sl], kr_at_64).astype(k_o.dtype)
        for p in range(MLA_HEADS // 2):
            even = pltpu.roll(kvb_r[:, (2 * p) * LANES:(2 * p + 1) * LANES], 64, 1)
            odd = kvb_r[:, (2 * p + 1) * LANES:(2 * p + 2) * LANES]
            v_o[:, p * LANES:(p + 1) * LANES] = jnp.where(lane < 64, even, odd).astype(v_o.dtype)

    outs = [_sds((S, 1024), MXU_DTYPE), _sds((S, 1024), MXU_DTYPE), _sds((S, 512), MXU_DTYPE)]
    return _rowwise(body, [qb, kvb, kr, tabs["cm"], tabs["sm"]], outs, name=name, rows=S)


def _mla_prep_bwd(dq, dk, dv, tabs, *, name):
    S = dq.shape[0]

    def body(tm, dq_r, dk_r, dv_r, cm, sm, dqb_o, dkvb_o, dkr_o):
        lane = _lane((tm, LANES))
        dkr = jnp.zeros((tm, LANES), F32)
        for h in range(MLA_HEADS):
            sl = slice(h * LANES, (h + 1) * LANES)
            dqb_o[:, sl] = _rope_chunk(dq_r[:, sl].astype(F32), cm[...], -sm[...], 16).astype(dqb_o.dtype)
            dkh = dk_r[:, sl].astype(F32)
            dvp = dv_r[:, (h // 2) * LANES:(h // 2 + 1) * LANES].astype(F32)
            dvh = pltpu.roll(dvp, 64, 1) if h % 2 == 0 else dvp
            dkvb_o[:, sl] = jnp.where(lane < 64, dkh, dvh).astype(dkvb_o.dtype)
            dkr = dkr + pltpu.roll(dkh, 64, 1)
        dkr_o[...] = jnp.where(lane < MLA_ROPE, dkr, 0.0)

    outs = [_sds((S, 1024), MXU_DTYPE), _sds((S, 1024), MXU_DTYPE), _sds((S, LANES))]
    return _rowwise(body, [dq, dk, dv, tabs["cm"], tabs["sm"]], outs, name=name, rows=S)


DILATIONS = tuple(d for _, d in DIL_PATTERNS)
QKV_CHUNKS = 8


def _to_branch(nat, c0, chunks, out_ref, d, rows):
    width = chunks * LANES
    for r in range(d):
        tok = pl.ds(r, rows // d, stride=d) if d > 1 else slice(None)
        for c in range(chunks):
            out_ref[:, r * width + c * LANES:r * width + (c + 1) * LANES] = nat[c0 + c, tok, :].astype(out_ref.dtype)


def _from_branch(in_ref, nat, c0, chunks, d, rows, add=False):
    width = chunks * LANES
    for r in range(d):
        tok = pl.ds(r, rows // d, stride=d) if d > 1 else slice(None)
        for c in range(chunks):
            val = in_ref[:, r * width + c * LANES:r * width + (c + 1) * LANES].astype(F32)
            nat[c0 + c, tok, :] = nat[c0 + c, tok, :] + val if add else val


def _branch_sds(S, width, d, dtype):
    return _sds((S // d, d * width), dtype)


def _l1_prep(qkv, tabs, *, name):
    S = qkv.shape[0]

    def body(tm, x_r, c64, s64, *rest):
        outs, nat = rest[:-1], rest[-1]
        for i in range(QKV_CHUNKS):
            sl = slice(i * LANES, (i + 1) * LANES)
            nat[i] = _rope_chunk(x_r[:, sl], c64[...], s64[...], 32)
            nat[QKV_CHUNKS + i] = _rope_chunk(x_r[:, 1024 + i * LANES:1024 + (i + 1) * LANES], c64[...], s64[...], 32)
            nat[2 * QKV_CHUNKS + i] = x_r[:, 2048 + i * LANES:2048 + (i + 1) * LANES]
        for b, d in enumerate(DILATIONS):
            for t in range(3):
                _to_branch(nat, t * QKV_CHUNKS, QKV_CHUNKS, outs[3 * b + t], d, tm)

    outs = [_branch_sds(S, 1024, d, MXU_DTYPE) for d in DILATIONS for _ in range(3)]
    got = _rowwise(body, [qkv, tabs["c64"], tabs["s64"]], outs, name=name, rows=S,
                   scratch=[pltpu.VMEM((3 * QKV_CHUNKS, _row_tile(S, 512), LANES), F32)])
    return {d: tuple(got[3 * b:3 * b + 3]) for b, d in enumerate(DILATIONS)}


def _l1_prep_bwd(grads, tabs, *, name):
    S = grads[1][0].shape[0]

    def body(tm, *rest):
        ins, (c64, s64, o, nat) = rest[:9], rest[9:]
        for b, d in enumerate(DILATIONS):
            for t in range(3):
                _from_branch(ins[3 * b + t], nat, t * QKV_CHUNKS, QKV_CHUNKS, d, tm, add=b > 0)
        for i in range(QKV_CHUNKS):
            sl = slice(i * LANES, (i + 1) * LANES)
            o[:, sl] = _rope_chunk(nat[i], c64[...], -s64[...], 32).astype(o.dtype)
            o[:, 1024 + i * LANES:1024 + (i + 1) * LANES] = _rope_chunk(
                nat[QKV_CHUNKS + i], c64[...], -s64[...], 32).astype(o.dtype)
            o[:, 2048 + i * LANES:2048 + (i + 1) * LANES] = nat[2 * QKV_CHUNKS + i].astype(o.dtype)

    ins = [g for d in DILATIONS for g in grads[d]] + [tabs["c64"], tabs["s64"]]
    return _rowwise(body, ins, [_sds((S, 3072), MXU_DTYPE)], name=name, rows=S, tm=256,
                    scratch=[pltpu.VMEM((3 * QKV_CHUNKS, _row_tile(S, 256), LANES), F32)])[0]


def _sigmoid(x):
    return 1.0 / (1.0 + jnp.exp(-x))


FFN_ROW_TILE, FFN_COL_TILE = 1024, 1408


def _gate_up(h, w_gate, w_up, *, name):
    (M, K), N = h.shape, w_gate.shape[1]
    tm, tn = _tile(M, FFN_ROW_TILE), _tile(N, FFN_COL_TILE)

    def body(h_ref, wg_ref, wu_ref, g_ref, u_ref, a_ref):
        g = _dot(h_ref[...], wg_ref[...], NN)
        u = _dot(h_ref[...], wu_ref[...], NN)
        g_ref[...] = g
        u_ref[...] = u
        a_ref[...] = (g * _sigmoid(g) * u).astype(a_ref.dtype)

    w_spec = pl.BlockSpec((K, tn), lambda j, i: (0, j))
    o_spec = pl.BlockSpec((tm, tn), lambda j, i: (i, j))
    return _pcall(
        body, name=name, dims=("parallel", "parallel"), grid=(N // tn, M // tm),
        in_specs=[pl.BlockSpec((tm, K), lambda j, i: (i, 0)), w_spec, w_spec], out_specs=[o_spec] * 3,
        out_shape=[_sds((M, N)), _sds((M, N)), _sds((M, N), MXU_DTYPE)],
    )(h, w_gate, w_up)


def _gate_up_bwd(dx, w_down, gate, up, *, name):
    (M, K), N = dx.shape, w_down.shape[0]
    tm, tn = _tile(M, FFN_ROW_TILE), _tile(N, FFN_COL_TILE)

    def body(dx_ref, w_ref, g_ref, u_ref, dg_ref, du_ref):
        d = _dot(dx_ref[...], w_ref[...], NT)
        g = g_ref[...]
        sg = _sigmoid(g)
        dg_ref[...] = (d * u_ref[...] * (sg * (1.0 + g * (1.0 - sg)))).astype(dg_ref.dtype)
        du_ref[...] = (d * g * sg).astype(du_ref.dtype)

    o_spec = pl.BlockSpec((tm, tn), lambda j, i: (i, j))
    return _pcall(
        body, name=name, dims=("parallel", "parallel"), grid=(N // tn, M // tm),
        in_specs=[pl.BlockSpec((tm, K), lambda j, i: (i, 0)), pl.BlockSpec((tn, K), lambda j, i: (j, 0)),
                  o_spec, o_spec],
        out_specs=[o_spec] * 2, out_shape=[_sds((M, N), MXU_DTYPE)] * 2,
    )(dx, w_down, gate, up)


def _head_pair_weights(w, c, rows):
    return jnp.where(_lane((rows, LANES)) < HEAD_DIM, w[:, 2 * c:2 * c + 1], w[:, 2 * c + 1:2 * c + 2])


def _merge(outs_by_d, lses_by_d, *, name):
    S = outs_by_d[1].shape[0]
    far = DILATIONS[1:]

    def body(tm, o1, o4, o16, l1, l4, l16, o_o, w1_o, w4_o, w16_o, nat_o, nat_l):
        for b, (o_r, l_r, d) in enumerate(zip((o4, o16), (l4, l16), far)):
            _from_branch(o_r, nat_o, b * QKV_CHUNKS, QKV_CHUNKS, d, tm)
            _from_branch(l_r, nat_l, b, 1, d, tm)
        ls = [l1[...], nat_l[0], nat_l[1]]
        m = jnp.maximum(jnp.maximum(ls[0], ls[1]), ls[2])
        es = [jnp.exp(l - m) for l in ls]
        tot = es[0] + es[1] + es[2]
        ws = [e / tot for e in es]
        for w_o, w in zip((w1_o, w4_o, w16_o), ws):
            w_o[...] = w
        for c in range(QKV_CHUNKS):
            sl = slice(c * LANES, (c + 1) * LANES)
            parts = (o1[:, sl], nat_o[c], nat_o[QKV_CHUNKS + c])
            o_o[:, sl] = sum(_head_pair_weights(w, c, tm) * part for w, part in zip(ws, parts))

    ins = [outs_by_d[d] for d in DILATIONS] + [lses_by_d[d] for d in DILATIONS]
    outs = [_sds((S, 1024))] + [_sds((S, LANES))] * 3
    rows = _row_tile(S, 512)
    return _rowwise(body, ins, outs, name=name, rows=S, tm=512,
                    scratch=[pltpu.VMEM((2 * QKV_CHUNKS, rows, LANES), F32), pltpu.VMEM((2, rows, LANES), F32)])


def _merge_bwd(do, o, ws, *, name):
    S = do.shape[0]

    def body(tm, do_r, o_r, w1, w4, w16, d1, d4, d16, e1, e4, e16, nat, nat_l):
        prod = do_r[...] * o_r[...]
        sums = _cols_to_lanes([jnp.sum(prod[:, j * HEAD_DIM:(j + 1) * HEAD_DIM], axis=1, keepdims=True)
                               for j in range(DIL_HEADS)], tm)
        for w_r, d_o, e_o, d in zip((w1, w4, w16), (d1, d4, d16), (e1, e4, e16), DILATIONS):
            w = w_r[...]
            nat_l[0] = w * sums
            _to_branch(nat_l, 0, 1, e_o, d, tm)
            for c in range(QKV_CHUNKS):
                nat[c] = _head_pair_weights(w, c, tm) * do_r[:, c * LANES:(c + 1) * LANES]
            _to_branch(nat, 0, QKV_CHUNKS, d_o, d, tm)

    outs = [_branch_sds(S, 1024, d, MXU_DTYPE) for d in DILATIONS] + [_branch_sds(S, LANES, d, F32) for d in DILATIONS]
    rows = _row_tile(S, 512)
    got = _rowwise(body, [do, o] + [ws[d] for d in DILATIONS], outs, name=name, rows=S, tm=512,
                   scratch=[pltpu.VMEM((QKV_CHUNKS, rows, LANES), F32), pltpu.VMEM((1, rows, LANES), F32)])
    return dict(zip(DILATIONS, got[:3])), dict(zip(DILATIONS, got[3:]))


def _loss_head(x, g, target, *, name):
    S, D = x.shape

    def body(tm, x_r, g_r, t_r, dx_o, dg_o, sq_o):
        xf = x_r[...]
        xhat, _ = _rms_parts(xf)
        err = xhat * g_r[...] - t_r[...]
        dx, dgp = _rms_bwd_rows(xf, g_r[...], err * (1.0 / D))
        dx_o[...] = dx
        _acc_rows(dg_o, dgp)
        _acc_rows(sq_o, err * err)

    return _rowwise(body, [x, g.reshape(1, D), target], [_sds((S, D)), _sds((1, D)), _sds((1, D))],
                    name=name, rows=S, accs=(1, 2))


def _adamw(w, g, m, v, *, name):
    c1 = 1.0 - ADAM_B1 ** ADAM_STEP
    c2 = 1.0 - ADAM_B2 ** ADAM_STEP

    def body(tm, w_r, g_r, m_r, v_r, d_o, m_o, v_o):
        g = g_r[...]
        m_new = ADAM_B1 * m_r[...] + (1.0 - ADAM_B1) * g
        v_new = ADAM_B2 * v_r[...] + (1.0 - ADAM_B2) * (g * g)
        m_o[...] = m_new
        v_o[...] = v_new
        d_o[...] = -ADAM_LR * ((m_new / c1) / (jnp.sqrt(v_new / c2) + ADAM_EPS) + ADAM_WD * w_r[...])

    return _rowwise(body, [w, g, m, v], [_sds(w.shape)] * 3, name=name, rows=w.shape[0], tm=256)


SUM_ROW_TILE = 256


def _sum_cores(grads, theirs, half_index, *, name):
    _, R, C = grads.shape
    h = R // 2
    nb = h // SUM_ROW_TILE

    def body(c_ref, g_ref, t_ref, o_ref):
        o_ref[...] = (g_ref[...].astype(F32) + t_ref[...].astype(F32)).astype(o_ref.dtype)

    grid_spec = pltpu.PrefetchScalarGridSpec(
        num_scalar_prefetch=1, grid=(4, nb),
        in_specs=[pl.BlockSpec((1, SUM_ROW_TILE, C), lambda k, i, c_ref: (k, c_ref[0] * nb + i, 0)),
                  pl.BlockSpec((1, SUM_ROW_TILE, C), lambda k, i, c_ref: (k, i, 0))],
        out_specs=pl.BlockSpec((1, SUM_ROW_TILE, C), lambda k, i, c_ref: (k, i, 0)))
    return _pcall(body, name=name, dims=("parallel", "parallel"), grid_spec=grid_spec,
                  out_shape=_sds((4, h, C), grads.dtype))(half_index, grads, theirs)


def _sum_chips(parts, half_index, *, name):
    _, h, C = parts.shape
    nb = h // SUM_ROW_TILE

    def body(c_ref, p_ref, o_ref):
        p = [p_ref[k].astype(F32) for k in range(4)]
        o_ref[...] = ((p[0] + p[1]) + p[2]) + p[3]

    grid_spec = pltpu.PrefetchScalarGridSpec(
        num_scalar_prefetch=1, grid=(nb,),
        in_specs=[pl.BlockSpec((4, SUM_ROW_TILE, C), lambda i, c_ref: (0, i, 0))],
        out_specs=pl.BlockSpec((SUM_ROW_TILE, C), lambda i, c_ref: (c_ref[0] * nb + i, 0)))
    return _pcall(body, name=name, dims=("parallel",), grid_spec=grid_spec,
                  out_shape=_sds((2 * h, C)))(half_index, parts)


def _position():
    return lax.axis_index("x"), lax.axis_index("y"), lax.axis_index("c")


def _chip_peers(x, y):
    return [(1 - x, y), (x, 1 - y), (1 - x, 1 - y)]


_HBM = pl.BlockSpec(memory_space=pltpu.HBM)
LOCAL_COPY_CHUNKS = 8


def _local_copies(src_ref, dst_ref, sems):
    rows = src_ref.shape[0] // LOCAL_COPY_CHUNKS
    assert rows * LOCAL_COPY_CHUNKS == src_ref.shape[0]
    return [pltpu.make_async_copy(src_ref.at[pl.ds(i * rows, rows)], dst_ref.at[pl.ds(i * rows, rows)], sems.at[i])
            for i in range(LOCAL_COPY_CHUNKS)]


class _Exchange:
    def __init__(self, src, out_shape, sems, stages):
        self.src, self.out_shape, self.sems, self.stages = src, out_shape, sems, stages

    def run(self, refs, step, n_steps, at_end):
        for fraction, fn in self.stages:
            if (fraction == 1.0) == at_end:
                pl.when(step == int(round(fraction * (n_steps - 1))))(functools.partial(fn, *refs))


def _run_exchange(ex, *, name):
    def body(*refs):
        for _, fn in ex.stages:
            fn(*refs)

    return pl.pallas_call(
        body, name=name, in_specs=[_HBM], out_specs=_HBM, out_shape=ex.out_shape, scratch_shapes=list(ex.sems),
    )(ex.src)


def _gather_exchange(src):
    R, C = src.shape
    h = R // 2

    def plan(src_ref, out_ref, send_sems, recv_sems, local_sems):
        x, y, c = _position()
        me = 2 * x + y
        peers = _chip_peers(x, y)
        mine, other = pl.ds(c * h, h), pl.ds((1 - c) * h, h)

        def copy(sem, src_part, dst_part, device):
            return pltpu.make_async_remote_copy(
                src_ref=src_part, dst_ref=dst_part, send_sem=send_sems.at[sem], recv_sem=recv_sems.at[sem],
                device_id=device, device_id_type=MESH)

        landed = [out_ref.at[2 * px + py, mine] for px, py in peers]
        theirs = [out_ref.at[2 * px + py, other] for px, py in peers]
        return dict(
            sends=lambda: [copy(j, src_ref.at[mine], out_ref.at[me, mine], (px, py, c))
                           for j, (px, py) in enumerate(peers)],
            local=lambda: _local_copies(src_ref, out_ref.at[me], local_sems),
            arrivals=lambda: [copy(j, landed[j], landed[j], (px, py, c)) for j, (px, py) in enumerate(peers)],
            passed=lambda: [copy(3 + j, landed[j], landed[j], (x, y, 1 - c)) for j in range(3)],
            from_sibling=lambda: [copy(3 + j, theirs[j], theirs[j], (x, y, 1 - c)) for j in range(3)])

    def start(*refs):
        p = plan(*refs)
        for cp in p["sends"]() + p["local"]():
            cp.start()

    def pass_on(*refs):
        p = plan(*refs)
        for arrival, forward in zip(p["arrivals"](), p["passed"]()):
            arrival.wait_recv()
            forward.start()

    def finish(*refs):
        p = plan(*refs)
        for cp in p["from_sibling"]():
            cp.wait_recv()
        for cp in p["sends"]() + p["passed"]():
            cp.wait_send()
        for cp in p["local"]():
            cp.wait()

    sems = [pltpu.SemaphoreType.DMA((6,)), pltpu.SemaphoreType.DMA((6,)), pltpu.SemaphoreType.DMA((LOCAL_COPY_CHUNKS,))]
    return _Exchange(src, jax.ShapeDtypeStruct((4, R, C), src.dtype), sems, [(0.0, start), (0.6, pass_on), (1.0, finish)])


def _swap_other_half(src, *, name):
    _, R, C = src.shape
    h = R // 2

    def body(src_ref, out_ref, send_sem, recv_sem):
        x, y, c = _position()
        cp = pltpu.make_async_remote_copy(
            src_ref=src_ref.at[:, pl.ds((1 - c) * h, h)], dst_ref=out_ref, send_sem=send_sem, recv_sem=recv_sem,
            device_id=(x, y, 1 - c), device_id_type=MESH)
        cp.start()
        cp.wait()

    return pl.pallas_call(
        body, name=name, in_specs=[_HBM], out_specs=_HBM, out_shape=jax.ShapeDtypeStruct((4, h, C), src.dtype),
        scratch_shapes=[pltpu.SemaphoreType.DMA, pltpu.SemaphoreType.DMA],
    )(src)


def _scatter_exchange(src):
    def plan(src_ref, out_ref, send_sems, recv_sems, local_sems):
        x, y, c = _position()
        me = 2 * x + y
        peers = _chip_peers(x, y)

        def copy(j, src_block, dst_slot):
            px, py = peers[j]
            return pltpu.make_async_remote_copy(
                src_ref=src_ref.at[src_block], dst_ref=out_ref.at[dst_slot], send_sem=send_sems.at[j],
                recv_sem=recv_sems.at[j], device_id=(px, py, c), device_id_type=MESH)

        return dict(sends=lambda: [copy(j, 2 * px + py, me) for j, (px, py) in enumerate(peers)],
                    arrivals=lambda: [copy(j, me, 2 * px + py) for j, (px, py) in enumerate(peers)],
                    local=lambda: _local_copies(src_ref.at[me], out_ref.at[me], local_sems))

    def start(*refs):
        p = plan(*refs)
        for cp in p["sends"]() + p["local"]():
            cp.start()

    def finish(*refs):
        p = plan(*refs)
        for cp in p["arrivals"]():
            cp.wait_recv()
        for cp in p["sends"]():
            cp.wait_send()
        for cp in p["local"]():
            cp.wait()

    sems = [pltpu.SemaphoreType.DMA((3,)), pltpu.SemaphoreType.DMA((3,)), pltpu.SemaphoreType.DMA((LOCAL_COPY_CHUNKS,))]
    return _Exchange(src, jax.ShapeDtypeStruct(src.shape, src.dtype), sems, [(0.0, start), (1.0, finish)])


def _join_halves(src, *, name):
    R, C = src.shape
    h = R // 2

    def body(src_ref, out_ref, send_sem, recv_sem):
        x, y, c = _position()
        mine, theirs = pl.ds(c * h, h), pl.ds((1 - c) * h, h)
        cp = pltpu.make_async_remote_copy(
            src_ref=src_ref.at[mine], dst_ref=out_ref.at[mine], send_sem=send_sem, recv_sem=recv_sem,
            device_id=(x, y, 1 - c), device_id_type=MESH)
        cp.start()
        pltpu.make_async_remote_copy(
            src_ref=src_ref.at[theirs], dst_ref=out_ref.at[theirs], send_sem=send_sem, recv_sem=recv_sem,
            device_id=(x, y, 1 - c), device_id_type=MESH).wait_recv()
        cp.wait_send()

    return pl.pallas_call(
        body, name=name, in_specs=[_HBM], out_specs=_HBM, out_shape=jax.ShapeDtypeStruct((R, C), src.dtype),
        input_output_aliases={0: 0},
        scratch_shapes=[pltpu.SemaphoreType.DMA, pltpu.SemaphoreType.DMA],
    )(src)


def _allreduce_small(vec, *, name):
    R, C = vec.shape

    def body(v_ref, o_ref, slots, send_sems, recv_sems):
        x, y, c = _position()
        me = 4 * x + 2 * y + c

        def peer(k):
            return x ^ ((k >> 2) & 1), y ^ ((k >> 1) & 1), c ^ (k & 1)

        def copy(k, slot):
            return pltpu.make_async_remote_copy(
                src_ref=v_ref, dst_ref=slots.at[slot], send_sem=send_sems.at[k - 1], recv_sem=recv_sems.at[k - 1],
                device_id=peer(k), device_id_type=MESH)

        slots[me] = v_ref[...]
        sends = [copy(k, me) for k in range(1, 8)]
        for cp in sends:
            cp.start()
        for k in range(1, 8):
            px, py, pc = peer(k)
            copy(k, 4 * px + 2 * py + pc).wait_recv()
        total = slots[0]
        for d in range(1, 8):
            total = total + slots[d]
        o_ref[...] = total
        for cp in sends:
            cp.wait_send()

    vmem = pl.BlockSpec(memory_space=pltpu.VMEM)
    return pl.pallas_call(
        body, name=name, in_specs=[vmem], out_specs=vmem, out_shape=jax.ShapeDtypeStruct((R, C), vec.dtype),
        scratch_shapes=[pltpu.VMEM((8, R, C), vec.dtype), pltpu.SemaphoreType.DMA((7,)), pltpu.SemaphoreType.DMA((7,))],
    )(vec)


def _swa_cfg(S):
    return _Attn(T=S, Tk=S, G=1, nh=SWA_HEADS, rep=SWA_HEADS // SWA_KV_HEADS, dqk=HEAD_DIM, dv=HEAD_DIM, tq=BLOCK,
                 tk=BLOCK, mode="band", max_dist=SWA_WINDOW - 1, scale=HEAD_DIM ** -0.5, qcol=lambda g: 0,
                 kcol=lambda g: 0, vcol=lambda g: 0, ocol=lambda g: 0, o_width=SWA_HEADS * HEAD_DIM)


MLA_FWD_GROUP = 8
MLA_BWD_GROUP = 4


def _mla_cfg(S, group):
    t = _tile(S, 512)
    return _Attn(T=S, Tk=S, G=MLA_HEADS // group, nh=group, rep=1, dqk=LANES, dv=MLA_V, tq=t, tk=t, mode="causal",
                 scale=(MLA_NOPE + MLA_ROPE) ** -0.5, qcol=lambda g: g, kcol=lambda g: g, vcol=lambda g: g,
                 ocol=lambda g: g, o_width=MLA_HEADS * MLA_V)


def _dil_cfg(S, window, dil):
    return _Attn(T=S // dil, Tk=S // dil, G=dil, nh=DIL_HEADS, rep=1, dqk=HEAD_DIM, dv=HEAD_DIM, tq=BLOCK, tk=BLOCK,
                 mode="band", max_dist=window // dil, scale=HEAD_DIM ** -0.5, qcol=lambda g: g, kcol=lambda g: g,
                 vcol=lambda g: g, ocol=lambda g: g, o_width=dil * DIL_HEADS * HEAD_DIM)


X_ROW_TILE = 512


def _memory_attn(q, kv, *, name):
    S, width = q.shape
    M = kv.shape[0]
    tq = _tile(S, X_ROW_TILE)
    scale = X_HEAD_DIM ** -0.5
    head = lambda j: slice(j * X_HEAD_DIM, (j + 1) * X_HEAD_DIM)

    def body(q_ref, kv_ref, o_ref, lse_ref):
        score = lambda j: _dot(kv_ref[:, head(j)], q_ref[:, head(j)], NT)
        ahead, rows = score(0), []
        for j in range(X_HEADS):
            s = ahead * scale
            if j + 1 < X_HEADS:
                ahead = score(j + 1)
            m = jnp.max(s, axis=0, keepdims=True)
            pr = jnp.exp(s - m)
            l = jnp.sum(pr, axis=0, keepdims=True)
            o_t = _dot(kv_ref[:, head(X_HEADS + j)], pr, TN)
            o_ref[:, head(j)] = (o_t / l).T.astype(o_ref.dtype)
            rows.append(m + jnp.log(l))
        lse_ref[...] = _rows_to_stats(rows, tq)

    return _pcall(
        body, name=name, dims=("parallel",), grid=(S // tq,),
        in_specs=[pl.BlockSpec((tq, width), lambda i: (i, 0)), pl.BlockSpec((M, 2 * width), lambda i: (0, 0))],
        out_specs=[pl.BlockSpec((tq, width), lambda i: (i, 0)), pl.BlockSpec((tq, LANES), lambda i: (i, 0))],
        out_shape=[_sds((S, width), MXU_DTYPE), _sds((S, LANES))],
    )(q, kv)


def _memory_attn_bwd(q, kv, o, do, lse, *, name):
    S, width = q.shape
    M = kv.shape[0]
    tq = _tile(S, X_ROW_TILE)
    n = S // tq
    scale = X_HEAD_DIM ** -0.5
    head = lambda j: slice(j * X_HEAD_DIM, (j + 1) * X_HEAD_DIM)

    def body(q_ref, kv_ref, o_ref, do_ref, lse_ref, dq_ref, dkv_ref, acc):
        i = pl.program_id(0)

        @pl.when(i == 0)
        def _():
            acc[...] = jnp.zeros_like(acc)

        lse_t = lse_ref[...].T

        def first(j):
            return (_dot(kv_ref[:, head(j)], q_ref[:, head(j)], NT),
                    _dot(kv_ref[:, head(X_HEADS + j)], do_ref[:, head(j)], NT))

        ahead = first(0)
        for j in range(X_HEADS):
            s, dp = ahead
            if j + 1 < X_HEADS:
                ahead = first(j + 1)
            row_term = jnp.sum((do_ref[:, head(j)].astype(F32) * o_ref[:, head(j)].astype(F32)).T, axis=0, keepdims=True)
            pr = jnp.exp(s * scale - lse_t[j:j + 1, :])
            ds = pr * (dp - row_term) * scale
            dq_ref[:, head(j)] = _dot(kv_ref[:, head(j)], ds, TN).T.astype(dq_ref.dtype)
            acc[:, head(j)] += _dot(ds, q_ref[:, head(j)], NN)
            acc[:, head(X_HEADS + j)] += _dot(pr, do_ref[:, head(j)], NN)

        @pl.when(i == n - 1)
        def _():
            dkv_ref[...] = acc[...].astype(dkv_ref.dtype)

    row = pl.BlockSpec((tq, width), lambda i: (i, 0))
    whole = pl.BlockSpec((M, 2 * width), lambda i: (0, 0))
    return _pcall(
        body, name=name, dims=("arbitrary",), grid=(n,),
        in_specs=[row, whole, row, row, pl.BlockSpec((tq, LANES), lambda i: (i, 0))], out_specs=[row, whole],
        out_shape=[_sds((S, width), MXU_DTYPE), _sds((M, 2 * width), MXU_DTYPE)],
        scratch_shapes=[pltpu.VMEM((M, 2 * width), F32)],
    )(q, kv, o, do, lse)


def _cross_fwd(p, x, mem, W, vec):
    hx = _rmsnorm(x, vec[p + "x_norm"], name=p + "x_norm")
    qx = _mm(hx, W[p + "w_xq"], mode="nn", name=p + "xq", out_dtype=MXU_DTYPE)
    memn = _rmsnorm(mem, vec[p + "mem_norm"], name=p + "mem_norm")
    kvx = _mm(memn, W[p + "w_xkv"], mode="nn", name=p + "xkv", out_dtype=MXU_DTYPE)
    ox, lse = _memory_attn(qx, kvx, name=p + "x_attn")
    out = _mm(ox, W[p + "w_xo"], mode="nn", name=p + "xo", res=x)
    return out, (x, hx, qx, memn, kvx, ox, lse)


def _cross_bwd(p, dx, saved, mem, W, vec, dW, dvec):
    x, hx, qx, memn, kvx, ox, lse = saved
    dox = _mm(dx, W[p + "w_xo"], mode="nt", name=p + "xo_dx", out_dtype=MXU_DTYPE)
    dW[p + "w_xo"] = _dw(ox, dx, name=p + "xo_dw")
    dqx, dkvx = _memory_attn_bwd(qx, kvx, ox, dox, lse, name=p + "x_attn_bwd")
    dW[p + "w_xq"] = _dw(hx, dqx, name=p + "xq_dw")
    dW[p + "w_xkv"] = _dw(memn, dkvx, name=p + "xkv_dw")
    dmemn = _mm(dkvx, W[p + "w_xkv"], mode="nt", name=p + "xkv_dx")
    _, dvec[p + "mem_norm"] = _rmsnorm_bwd(mem, vec[p + "mem_norm"], dmemn, name=p + "mem_norm_bwd")
    dx_in, dvec[p + "x_norm"] = _dx_norm_bwd(dqx, W[p + "w_xq"], x, vec[p + "x_norm"], dx, name=p + "xq_dx")
    return dx_in


def _ffn_fwd(p, x, W, vec):
    hf = _rmsnorm(x, vec[p + "ffn_norm"], name=p + "ffn_norm")
    gate, up, act = _gate_up(hf, W[p + "w_gate"], W[p + "w_up"], name=p + "gate_up")
    out = _mm(act, W[p + "w_down"], mode="nn", name=p + "down", res=x)
    return out, (x, hf, gate, up, act)


def _ffn_bwd(p, dx, saved, W, vec, dW, dvec):
    x, hf, gate, up, act = saved
    dW[p + "w_down"] = _dw(act, dx, name=p + "down_dw")
    dgate, dup = _gate_up_bwd(dx, W[p + "w_down"], gate, up, name=p + "gate_up_bwd")
    dhf = _mm(dgate, W[p + "w_gate"], mode="nt", name=p + "gate_dx")
    dW[p + "w_gate"] = _dw(hf, dgate, name=p + "gate_dw")
    dW[p + "w_up"] = _dw(hf, dup, name=p + "up_dw")
    dx_in, dvec[p + "ffn_norm"] = _dx_norm_bwd(dup, W[p + "w_up"], x, vec[p + "ffn_norm"], dx, name=p + "up_dx",
                                               res=dhf)
    return dx_in


def _even_fwd(p, x, tabs, W, vec, comm=None):
    S = x.shape[0]
    h = _rmsnorm(x, vec[p + "mix_norm"], name=p + "mix_norm")
    z = _mm(h, W[p + "w_in"], mode="nn", name=p + "in")
    qa, ka, va, cqn, ckvn, kr = _l0_prep(z, tabs, vec[p + "q_norm"], vec[p + "kv_norm"], name=p + "prep")
    sink = jnp.pad(vec[p + "sinks"], (0, LANES - SWA_HEADS)).reshape(1, LANES)
    oa, lse_a = _band_fwd(_swa_cfg(S), qa, ka, va, name=p + "swa", sink=sink, out_dtype=MXU_DTYPE)
    qb = _mm(cqn, W[p + "w_uq"], mode="nn", name=p + "uq")
    kvb = _mm(ckvn, W[p + "w_ukv"], mode="nn", name=p + "ukv")
    Q, K, V = _mla_prep(qb, kvb, kr, tabs, name=p + "mla_prep")
    if comm is None:
        ob, lse_b = _causal_fwd(_mla_cfg(S, MLA_FWD_GROUP), Q, K, V, name=p + "mla", out_dtype=MXU_DTYPE, stat_heads=MLA_BWD_GROUP)
    else:
        ob, lse_b, gathered = _causal_fwd(_mla_cfg(S, MLA_FWD_GROUP), Q, K, V, name=p + "mla", out_dtype=MXU_DTYPE, stat_heads=MLA_BWD_GROUP,
                                          carry=comm.late_weights_exchange())
        W = {**W, **comm.late_weights(gathered)}
    o = jnp.concatenate([oa, ob], axis=1)
    out = _mm(o, W[p + "w_out"], mode="nn", name=p + "out", res=x)
    return out, (x, h, z, qa, ka, va, cqn, ckvn, sink, oa, lse_a, Q, K, V, ob, lse_b, o), W


def _even_bwd(p, dx, saved, tabs, W, vec, dW, dvec, comm=None):
    x, h, z, qa, ka, va, cqn, ckvn, sink, oa, lse_a, Q, K, V, ob, lse_b, o = saved
    S = x.shape[0]
    do = _mm(dx, W[p + "w_out"], mode="nt", name=p + "out_dx", out_dtype=MXU_DTYPE)
    dW[p + "w_out"] = _dw(o, dx, name=p + "out_dw")
    doa, dob = do[:, :SWA_HEADS * HEAD_DIM], do[:, SWA_HEADS * HEAD_DIM:]
    cfg = _swa_cfg(S)
    delta, dsink = _attn_delta(cfg, oa, doa, name=p + "swa_delta", lse=lse_a, sink=sink)
    dvec[p + "sinks"] = dsink
    dqa, dka, dva = _band_bwd(cfg, qa, ka, va, doa, lse_a, delta, name=p + "swa_bwd")
    cfg = _mla_cfg(S, MLA_BWD_GROUP)
    delta, _ = _attn_delta(cfg, ob, dob, name=p + "mla_delta")
    if comm is None:
        dQ, dK, dV = _causal_bwd(cfg, Q, K, V, dob, lse_b, delta, name=p + "mla_bwd")
    else:
        dQ, dK, dV, landed = _causal_bwd(cfg, Q, K, V, dob, lse_b, delta, name=p + "mla_bwd",
                                         carry=comm.late_grads_exchange(dW))
        comm.late_grads_landed(landed)
    dqb, dkvb, dkr = _mla_prep_bwd(dQ, dK, dV, tabs, name=p + "mla_prep_bwd")
    dcqn = _mm(dqb, W[p + "w_uq"], mode="nt", name=p + "uq_dx")
    dW[p + "w_uq"] = _dw(cqn, dqb, name=p + "uq_dw")
    dckvn = _mm(dkvb, W[p + "w_ukv"], mode="nt", name=p + "ukv_dx")
    dW[p + "w_ukv"] = _dw(ckvn, dkvb, name=p + "ukv_dw")
    dz, dvec[p + "q_norm"], dvec[p + "kv_norm"] = _l0_prep_bwd(
        z, tabs, vec[p + "q_norm"], vec[p + "kv_norm"], dqa, dka, dva, dcqn, dckvn, dkr, name=p + "prep_bwd")
    dW[p + "w_in"] = _dw(h, dz, name=p + "in_dw")
    dx_in, dvec[p + "mix_norm"] = _dx_norm_bwd(dz, W[p + "w_in"], x, vec[p + "mix_norm"], dx, name=p + "in_dx")
    return dx_in


def _odd_fwd(p, x, tabs, W, vec):
    S = x.shape[0]
    assert S % (DIL_PATTERNS[-1][1] * BLOCK) == 0, "keys past the end of the sequence are never attended"
    h = _rmsnorm(x, vec[p + "mix_norm"], name=p + "mix_norm")
    qkv = _mm(h, W[p + "w_qkv"], mode="nn", name=p + "qkv")
    qkv_by_d = _l1_prep(qkv, tabs, name=p + "prep")
    outs, lses = {}, {}
    for window, dil in DIL_PATTERNS:
        outs[dil], lses[dil] = _band_fwd(_dil_cfg(S, window, dil), *qkv_by_d[dil], name=p + "dil%d" % dil)
    o, w1, w4, w16 = _merge(outs, lses, name=p + "merge")
    out = _mm(o, W[p + "w_out"], mode="nn", name=p + "out", res=x)
    return out, (x, h, qkv_by_d, lses, dict(zip(DILATIONS, (w1, w4, w16))), o)


def _odd_bwd(p, dx, saved, tabs, W, vec, dW, dvec):
    x, h, qkv_by_d, lses, ws, o = saved
    S = x.shape[0]
    do = _mm(dx, W[p + "w_out"], mode="nt", name=p + "out_dx")
    dW[p + "w_out"] = _dw(o, dx, name=p + "out_dw")
    dos, deltas = _merge_bwd(do, o, ws, name=p + "merge_bwd")
    grads = {}
    for window, dil in DIL_PATTERNS:
        grads[dil] = _band_bwd(_dil_cfg(S, window, dil), *qkv_by_d[dil], dos[dil], lses[dil], deltas[dil],
                               name=p + "dil%d_bwd" % dil)
    dqkv = _l1_prep_bwd(grads, tabs, name=p + "prep_bwd")
    dW[p + "w_qkv"] = _dw(h, dqkv, name=p + "qkv_dw")
    dx_in, dvec[p + "mix_norm"] = _dx_norm_bwd(dqkv, W[p + "w_qkv"], x, vec[p + "mix_norm"], dx, name=p + "qkv_dx")
    return dx_in


def _local_step(x, mem, positions, target, W, vec, comm=None):
    tabs = _rope_tables(positions)
    x1, s_mix0, W = _even_fwd("l0_", x, tabs, W, vec, comm)
    x2, s_x0 = _cross_fwd("l0_", x1, mem, W, vec)
    x3, s_f0 = _ffn_fwd("l0_", x2, W, vec)
    x4, s_mix1 = _odd_fwd("l1_", x3, tabs, W, vec)
    x5, s_x1 = _cross_fwd("l1_", x4, mem, W, vec)
    x6, s_f1 = _ffn_fwd("l1_", x5, W, vec)
    dW, dvec = {}, {}
    dx, dvec["final_norm"], sq = _loss_head(x6, vec["final_norm"], target, name="loss_head")
    dx = _ffn_bwd("l1_", dx, s_f1, W, vec, dW, dvec)
    dx = _cross_bwd("l1_", dx, s_x1, mem, W, vec, dW, dvec)
    dx = _odd_bwd("l1_", dx, s_mix1, tabs, W, vec, dW, dvec)
    dx = _ffn_bwd("l0_", dx, s_f0, W, vec, dW, dvec)
    dx = _cross_bwd("l0_", dx, s_x0, mem, W, vec, dW, dvec)
    dx = _even_bwd("l0_", dx, s_mix0, tabs, W, vec, dW, dvec, comm)
    return sq, dx, dW, dvec


_LAYER_MATS = {
    0: [("w_in", "col"), ("w_uq", "col"), ("w_ukv", "col"), ("w_out", "row"), ("w_xq", "row"), ("w_xkv", "row"),
        ("w_xo", "col"), ("w_gate", "col"), ("w_up", "col"), ("w_down", "row")],
    1: [("w_qkv", "col"), ("w_out", "row"), ("w_xq", "row"), ("w_xkv", "row"), ("w_xo", "col"), ("w_gate", "col"),
        ("w_up", "col"), ("w_down", "row")],
}
MATS = [("l%d_%s" % (l, n), kind) for l in (0, 1) for n, kind in _LAYER_MATS[l]]
_LAYER_VECS = {0: ["mix_norm", "sinks", "q_norm", "kv_norm", "x_norm", "mem_norm", "ffn_norm"],
               1: ["mix_norm", "x_norm", "mem_norm", "ffn_norm"]}
VECS = ["l%d_%s" % (l, n) for l in (0, 1) for n in _LAYER_VECS[l]] + ["final_norm"]
WEIGHT_ORDER = (["l0_mix_norm", "l0_w_in", "l0_sinks", "l0_q_norm", "l0_w_uq", "l0_kv_norm", "l0_w_ukv", "l0_w_out",
                 "l0_x_norm", "l0_mem_norm", "l0_w_xq", "l0_w_xkv", "l0_w_xo", "l0_ffn_norm", "l0_w_gate", "l0_w_up",
                 "l0_w_down", "l1_mix_norm", "l1_w_qkv", "l1_w_out", "l1_x_norm", "l1_mem_norm", "l1_w_xq",
                 "l1_w_xkv", "l1_w_xo", "l1_ffn_norm", "l1_w_gate", "l1_w_up", "l1_w_down", "final_norm"])
PACK_COLS = 1024
PACK_ROW_TILE = 2 * SUM_ROW_TILE
VEC_ROWS = 16
LOSS_ROW = len(VECS)
N_CHIPS = 4


class _Group:
    def __init__(self, mats, shards):
        self.mats, self.shards = mats, shards
        self.layout, off = {}, 0
        for name, _ in mats:
            n = shards[name].size // PACK_COLS
            assert n * PACK_COLS == shards[name].size
            self.layout[name] = (off, n)
            off += n
        self.used = off
        self.rows = -(-off // PACK_ROW_TILE) * PACK_ROW_TILE

    def pack(self, tensors, dtype):
        parts = [tensors[name].astype(dtype).reshape(-1, PACK_COLS) for name, _ in self.mats]
        return jnp.concatenate(parts + [jnp.zeros((self.rows - self.used, PACK_COLS), dtype)], axis=0)

    def unpack(self, packed):
        return {name: packed[off:off + n].reshape(self.shards[name].shape) for name, (off, n) in self.layout.items()}

    def full_weights(self, gathered):
        W = {}
        for name, kind in self.mats:
            off, n = self.layout[name]
            r, cw = self.shards[name].shape
            blocks = gathered[:, off:off + n].reshape(N_CHIPS, r, cw)
            W[name] = blocks.reshape(N_CHIPS * r, cw) if kind == "row" else (
                jnp.transpose(blocks, (1, 0, 2)).reshape(r, N_CHIPS * cw))
        if "l0_w_in" in W:
            W["l0_w_in"] = jnp.pad(W["l0_w_in"], ((0, 0), (0, Z_END - W["l0_w_in"].shape[1])))
        if "l0_w_uq" in W:
            uq = W["l0_w_uq"].reshape(MLA_Q_RANK, MLA_HEADS, MLA_NOPE + MLA_ROPE)
            uq = jnp.pad(uq, ((0, 0), (0, 0), (0, LANES - MLA_NOPE - MLA_ROPE)))
            W["l0_w_uq"] = uq.reshape(MLA_Q_RANK, MLA_HEADS * LANES)
        return W

    def pack_grads(self, dW):
        parts = []
        for name, kind in self.mats:
            r, cw = self.shards[name].shape
            g = dW[name]
            if name == "l0_w_in":
                g = g[:, :Z_KR + MLA_ROPE]
            if name == "l0_w_uq":
                g = g.reshape(MLA_Q_RANK, MLA_HEADS, LANES)[:, :, :MLA_NOPE + MLA_ROPE].reshape(MLA_Q_RANK, -1)
            if kind == "col":
                g = jnp.transpose(g.reshape(r, N_CHIPS, cw), (1, 0, 2))
            parts.append(g.reshape(N_CHIPS, -1, PACK_COLS).astype(EXCHANGE_DTYPE))
        pad = jnp.zeros((N_CHIPS, self.rows - self.used, PACK_COLS), EXCHANGE_DTYPE)
        return jnp.concatenate(parts + [pad], axis=1)


def _pack_vecs(vecs):
    rows = [jnp.pad(vecs[n].reshape(-1).astype(F32), (0, PACK_COLS - vecs[n].size)) for n in VECS]
    rows += [jnp.zeros((PACK_COLS,), F32)] * (VEC_ROWS - len(rows))
    return jnp.stack(rows)


def _unpack_vecs(packed, like):
    return {n: packed[i, :like[n].size].reshape(like[n].shape) for i, n in enumerate(VECS)}


EARLY_MATS = [m for m in MATS if m[0] in ("l0_w_in", "l0_w_uq", "l0_w_ukv")]
LATE_MATS = [m for m in MATS if m not in EARLY_MATS]


class _StepComm:
    def __init__(self, shards):
        self.early, self.late = _Group(EARLY_MATS, shards), _Group(LATE_MATS, shards)
        self.half_index = lax.axis_index("c").astype(jnp.int32).reshape(1)
        self.late_grads = None

    def early_weights(self):
        src = self.early.pack(self.early.shards, MXU_DTYPE)
        return self.early.full_weights(_run_exchange(_gather_exchange(src), name="gather_early"))

    def late_weights_exchange(self):
        return _gather_exchange(self.late.pack(self.late.shards, MXU_DTYPE))

    def late_weights(self, gathered):
        return self.late.full_weights(gathered)

    def _chip_sum(self, group, dW, tag):
        grads = group.pack_grads(dW)
        theirs = _swap_other_half(grads, name="swap_other_half_" + tag)
        return _sum_cores(grads, theirs, self.half_index, name="sum_cores_" + tag)

    def _finish(self, parts, tag):
        return _join_halves(_sum_chips(parts, self.half_index, name="sum_chips_" + tag), name="join_halves_" + tag)

    def late_grads_exchange(self, dW):
        return _scatter_exchange(self._chip_sum(self.late, dW, "late"))

    def late_grads_landed(self, parts):
        self.late_grads = self._finish(parts, "late")

    def early_grads(self, dW):
        parts = _run_exchange(_scatter_exchange(self._chip_sum(self.early, dW, "early")), name="scatter_early")
        return self._finish(parts, "early")


def _step(a):
    weights = {n: a[n] for n in WEIGHT_ORDER}
    shards = {n: weights[n] for n, _ in MATS}
    vec = {n: weights[n] for n in VECS}
    comm = _StepComm(shards)
    sq, grad_x, dW, dvec = _local_step(a["x"][0], a["mem"][0], a["positions"], a["loss_target"][0],
                                       comm.early_weights(), vec, comm)

    dvec = dict(dvec)
    dvec["l0_sinks"] = dvec["l0_sinks"][0, :SWA_HEADS]
    small = _pack_vecs(dvec)
    small = small.at[LOSS_ROW, 0].set(0.5 / a["x"].shape[-1] * jnp.sum(sq))
    small = _allreduce_small(small, name="reduce_gains")
    loss = small[LOSS_ROW, 0]
    g_s = small.at[LOSS_ROW, 0].set(0.0)
    d_s, m_s, v_s = _adamw(_pack_vecs(vec), g_s, _pack_vecs({n: a["m_" + n] for n in VECS}),
                           _pack_vecs({n: a["v_" + n] for n in VECS}), name="adamw_gains")
    got = [_unpack_vecs(packed, vec) for packed in (g_s, d_s, m_s, v_s)]

    for group, g_w in ((comm.late, comm.late_grads), (comm.early, comm.early_grads(dW))):
        for n, g in group.unpack(g_w).items():
            results = (g,) + tuple(_adamw(shards[n], g, a["m_" + n], a["v_" + n], name="adamw_" + n))
            for kind, value in zip(got, results):
                kind[n] = value

    out = [loss, grad_x[None]]
    for kind in got:
        out += [kind[n] for n in WEIGHT_ORDER]
    return tuple(out)


def kernel(x, mem, positions, l0_mix_norm, l0_w_in, l0_sinks, l0_q_norm, l0_w_uq, l0_kv_norm, l0_w_ukv, l0_w_out, l0_x_norm, l0_mem_norm, l0_w_xq, l0_w_xkv, l0_w_xo, l0_ffn_norm, l0_w_gate, l0_w_up, l0_w_down, l1_mix_norm, l1_w_qkv, l1_w_out, l1_x_norm, l1_mem_norm, l1_w_xq, l1_w_xkv, l1_w_xo, l1_ffn_norm, l1_w_gate, l1_w_up, l1_w_down, final_norm, loss_target, m_l0_mix_norm, m_l0_w_in, m_l0_sinks, m_l0_q_norm, m_l0_w_uq, m_l0_kv_norm, m_l0_w_ukv, m_l0_w_out, m_l0_x_norm, m_l0_mem_norm, m_l0_w_xq, m_l0_w_xkv, m_l0_w_xo, m_l0_ffn_norm, m_l0_w_gate, m_l0_w_up, m_l0_w_down, m_l1_mix_norm, m_l1_w_qkv, m_l1_w_out, m_l1_x_norm, m_l1_mem_norm, m_l1_w_xq, m_l1_w_xkv, m_l1_w_xo, m_l1_ffn_norm, m_l1_w_gate, m_l1_w_up, m_l1_w_down, m_final_norm, v_l0_mix_norm, v_l0_w_in, v_l0_sinks, v_l0_q_norm, v_l0_w_uq, v_l0_kv_norm, v_l0_w_ukv, v_l0_w_out, v_l0_x_norm, v_l0_mem_norm, v_l0_w_xq, v_l0_w_xkv, v_l0_w_xo, v_l0_ffn_norm, v_l0_w_gate, v_l0_w_up, v_l0_w_down, v_l1_mix_norm, v_l1_w_qkv, v_l1_w_out, v_l1_x_norm, v_l1_mem_norm, v_l1_w_xq, v_l1_w_xkv, v_l1_w_xo, v_l1_ffn_norm, v_l1_w_gate, v_l1_w_up, v_l1_w_down, v_final_norm):
    return _step(dict(locals()))
```
